```python
import math
import jax, jax.numpy as jnp
from jax import lax
import numpy as np

D_MODEL = 1024
BATCH = 8
SEQ = 8192
DEPTH = 4

N_MIXERS = 3
D_FF = 2816
RMS_EPS = 1e-6
S5_WIDTH = D_MODEL
S5_GROUP = 16
S5_GROUPS = S5_WIDTH // S5_GROUP
S5_STATE = 64
S5_CHUNK = 128
S5_DT_MIN = 1e-3
S5_DT_MAX = 1e-1
SB_HEAD_DIM = 64
SB_HEADS = D_MODEL // SB_HEAD_DIM
SB_BLOCK = 128
LRU_WIDTH = D_MODEL
LRU_BLOCK_WIDTH = 256
LRU_BLOCKS = LRU_WIDTH // LRU_BLOCK_WIDTH
LRU_CONV = 4
LRU_C = 8.0
N_S5 = len(range(0, DEPTH, N_MIXERS))
N_SB = len(range(1, DEPTH, N_MIXERS))
N_LRU = len(range(2, DEPTH, N_MIXERS))

kernel_name = "hybrid_s5_stickbreak_rglru_macaron"


def rms_norm(x, g):
    xf = x.astype(jnp.float32)
    y = xf * lax.rsqrt(jnp.mean(xf * xf, axis=-1, keepdims=True) + RMS_EPS)
    return (y * g.astype(jnp.float32)).astype(x.dtype)


def swiglu_ffn(h, w_in, w_out):
    gate, up = jnp.split(h @ w_in, 2, axis=-1)
    return (jax.nn.silu(gate) * up) @ w_out


def _linear_recurrence_op(left, right):
    a_l, b_l = left
    a_r, b_r = right
    return a_r * a_l, a_r * b_l + b_r


def s5_mixer(h, w_in, lam_re, lam_im, log_dt, b_re, b_im, c_re, c_im, d_skip, w_out):
    f32 = jnp.float32
    bsz, seq, _ = h.shape
    u = (h @ w_in).astype(f32)
    lam = lax.complex(lam_re.astype(f32), lam_im.astype(f32))
    lam_dt = lam * jnp.exp(log_dt.astype(f32))[:, None]
    lam_bar = jnp.exp(lam_dt)
    b_bar = ((lam_bar - 1.0) / lam)[:, :, None] * lax.complex(b_re.astype(f32), b_im.astype(f32))
    c = lax.complex(c_re.astype(f32), c_im.astype(f32))
    decay_pow = jnp.exp(lam_dt[None] * jnp.arange(1, S5_CHUNK + 1, dtype=f32)[:, None, None])
    n_chunks = seq // S5_CHUNK
    u_blocks = u.reshape(bsz, n_chunks, S5_CHUNK, S5_GROUPS, S5_GROUP).transpose(1, 0, 2, 3, 4)
    a_elems = jnp.broadcast_to(lam_bar, (bsz, S5_CHUNK, S5_GROUPS, S5_STATE))

    def chunk_step(state, u_blk):
        bu = jnp.einsum('gph,btgh->btgp', b_bar, u_blk.astype(jnp.complex64))
        _, s = lax.associative_scan(_linear_recurrence_op, (a_elems, bu), axis=1)
        s = s + decay_pow[None] * state[:, None]
        y_blk = jnp.einsum('ghp,btgp->btgh', c, s).real
        return s[:, -1], y_blk

    state0 = jnp.zeros((bsz, S5_GROUPS, S5_STATE), jnp.complex64)
    _, y = lax.scan(chunk_step, state0, u_blocks)
    y = y.transpose(1, 0, 2, 3, 4).reshape(bsz, seq, S5_WIDTH) + d_skip.astype(f32) * u
    z = jax.nn.gelu(y).astype(h.dtype)
    val, gate = jnp.split(z @ w_out, 2, axis=-1)
    return val * jax.nn.sigmoid(gate)


def stick_breaking_mixer(h, w_qkv, w_out):
    f32 = jnp.float32
    bsz, seq, _ = h.shape
    qkv = (h @ w_qkv).reshape(bsz, seq, 3, SB_HEADS, SB_HEAD_DIM)
    q = qkv[:, :, 0].astype(f32) * (SB_HEAD_DIM ** -0.5)
    k = qkv[:, :, 1].astype(f32)
    v = qkv[:, :, 2].astype(f32)
    n_blocks = seq // SB_BLOCK
    q_blocks = q.reshape(bsz, n_blocks, SB_BLOCK, SB_HEADS, SB_HEAD_DIM).transpose(1, 0, 2, 3, 4)
    key_pos = jnp.arange(seq)

    def attend(args):
        q_blk, blk = args
        logits = jnp.einsum('bqhd,bkhd->bhqk', q_blk, k)
        query_pos = blk * SB_BLOCK + jnp.arange(SB_BLOCK)
        causal = key_pos[None, :] < query_pos[:, None]
        log_keep = jnp.where(causal, jax.nn.log_sigmoid(-logits), 0.0)
        log_remaining = lax.cumsum(log_keep, axis=3, reverse=True) - log_keep
        weights = jnp.where(causal, jnp.exp(jax.nn.log_sigmoid(logits) + log_remaining), 0.0)
        return jnp.einsum('bhqk,bkhd->bqhd', weights, v)

    o = lax.map(attend, (q_blocks, jnp.arange(n_blocks)))
    o = o.transpose(1, 0, 2, 3, 4).reshape(bsz, seq, D_MODEL).astype(h.dtype)
    return o @ w_out


def rglru_mixer(h, w_in, conv_w, conv_b, w_a, b_a, w_x, b_x, lam, w_out):
    f32 = jnp.float32
    bsz, seq, _ = h.shape
    branch_gelu, branch_rnn = jnp.split(h @ w_in, 2, axis=-1)
    xc = lax.conv_general_dilated(
        branch_rnn, conv_w[:, None, :], window_strides=(1,), padding=[(LRU_CONV - 1, 0)],
        dimension_numbers=('NWC', 'WIO', 'NWC'), feature_group_count=LRU_WIDTH) + conv_b
    xb = xc.reshape(bsz, seq, LRU_BLOCKS, LRU_BLOCK_WIDTH)
    r = jax.nn.sigmoid(jnp.einsum('blnc,ncd->blnd', xb, w_a) + b_a).reshape(bsz, seq, LRU_WIDTH)
    i = jax.nn.sigmoid(jnp.einsum('blnc,ncd->blnd', xb, w_x) + b_x).reshape(bsz, seq, LRU_WIDTH)
    log_a = (-LRU_C * r.astype(f32)) * jax.nn.softplus(-lam.astype(f32))
    a = jnp.exp(log_a)
    gated_x = (i * xc).astype(f32) * jnp.sqrt(-jnp.expm1(2.0 * log_a))
    _, hseq = lax.associative_scan(_linear_recurrence_op, (a, gated_x), axis=1)
    y = (jax.nn.gelu(branch_gelu.astype(f32)) * hseq).astype(h.dtype)
    return y @ w_out


def _fwd_setup_inputs(seed: int = 0) -> dict:
    key = jax.random.key(seed)
    ks = iter(jax.random.split(key, 40))

    def dense(shape, fan_in):
        return jax.random.normal(next(ks), shape, jnp.float32) * (fan_in ** -0.5)

    def gain(shape):
        return 1.0 + 0.02 * jax.random.normal(next(ks), shape, jnp.float32)

    def small(shape):
        return 0.01 * jax.random.normal(next(ks), shape, jnp.float32)

    x = jax.random.normal(next(ks), (BATCH, SEQ, D_MODEL), jnp.float32)
    inp = {"x": x}
    inp["ffn1_norm"] = gain((DEPTH, D_MODEL))
    inp["ffn1_w_in"] = dense((DEPTH, D_MODEL, 2 * D_FF), D_MODEL)
    inp["ffn1_w_out"] = dense((DEPTH, D_FF, D_MODEL), D_FF)
    inp["mix_norm"] = gain((DEPTH, D_MODEL))
    inp["ffn2_norm"] = gain((DEPTH, D_MODEL))
    inp["ffn2_w_in"] = dense((DEPTH, D_MODEL, 2 * D_FF), D_MODEL)
    inp["ffn2_w_out"] = dense((DEPTH, D_FF, D_MODEL), D_FF)
    inp["final_norm"] = gain((D_MODEL,))
    inp["s5_w_in"] = dense((N_S5, D_MODEL, S5_WIDTH), D_MODEL)
    inp["s5_lam_re"] = -0.5 + small((N_S5, S5_GROUPS, S5_STATE))
    inp["s5_lam_im"] = (math.pi * jnp.arange(S5_STATE, dtype=jnp.float32))[None, None, :] + small((N_S5, S5_GROUPS, S5_STATE))
    inp["s5_log_dt"] = jax.random.uniform(next(ks), (N_S5, S5_GROUPS), jnp.float32,
                                          math.log(S5_DT_MIN), math.log(S5_DT_MAX))
    inp["s5_b_re"] = dense((N_S5, S5_GROUPS, S5_STATE, S5_GROUP), 2 * S5_GROUP)
    inp["s5_b_im"] = dense((N_S5, S5_GROUPS, S5_STATE, S5_GROUP), 2 * S5_GROUP)
    inp["s5_c_re"] = dense((N_S5, S5_GROUPS, S5_GROUP, S5_STATE), S5_STATE)
    inp["s5_c_im"] = dense((N_S5, S5_GROUPS, S5_GROUP, S5_STATE), S5_STATE)
    inp["s5_d"] = jax.random.normal(next(ks), (N_S5, S5_WIDTH), jnp.float32)
    inp["s5_w_out"] = dense((N_S5, S5_WIDTH, 2 * D_MODEL), S5_WIDTH)
    inp["sb_w_qkv"] = dense((N_SB, D_MODEL, 3 * D_MODEL), D_MODEL)
    inp["sb_w_out"] = dense((N_SB, D_MODEL, D_MODEL), D_MODEL)
    inp["lru_w_in"] = dense((N_LRU, D_MODEL, 2 * LRU_WIDTH), D_MODEL)
    inp["lru_conv_w"] = dense((N_LRU, LRU_CONV, LRU_WIDTH), LRU_CONV)
    inp["lru_conv_b"] = small((N_LRU, LRU_WIDTH))
    inp["lru_w_a"] = dense((N_LRU, LRU_BLOCKS, LRU_BLOCK_WIDTH, LRU_BLOCK_WIDTH), LRU_BLOCK_WIDTH)
    inp["lru_b_a"] = small((N_LRU, LRU_BLOCKS, LRU_BLOCK_WIDTH))
    inp["lru_w_x"] = dense((N_LRU, LRU_BLOCKS, LRU_BLOCK_WIDTH, LRU_BLOCK_WIDTH), LRU_BLOCK_WIDTH)
    inp["lru_b_x"] = small((N_LRU, LRU_BLOCKS, LRU_BLOCK_WIDTH))
    a0 = jax.random.uniform(next(ks), (N_LRU, LRU_WIDTH), jnp.float32, 0.9, 0.999)
    p = jnp.exp(jnp.log(a0) / LRU_C)
    inp["lru_lambda"] = jnp.log(p) - jnp.log1p(-p)
    inp["lru_w_out"] = dense((N_LRU, LRU_WIDTH, D_MODEL), LRU_WIDTH)
    return inp


def _fwd_reference(x, ffn1_norm, ffn1_w_in, ffn1_w_out, mix_norm, ffn2_norm, ffn2_w_in, ffn2_w_out, final_norm,
              s5_w_in, s5_lam_re, s5_lam_im, s5_log_dt, s5_b_re, s5_b_im, s5_c_re, s5_c_im, s5_d, s5_w_out,
              sb_w_qkv, sb_w_out,
              lru_w_in, lru_conv_w, lru_conv_b, lru_w_a, lru_b_a, lru_w_x, lru_b_x, lru_lambda, lru_w_out):
    h = x
    for layer in range(DEPTH):
        h = h + 0.5 * swiglu_ffn(rms_norm(h, ffn1_norm[layer]), ffn1_w_in[layer], ffn1_w_out[layer])
        hn = rms_norm(h, mix_norm[layer])
        kind = layer % N_MIXERS
        j = layer // N_MIXERS
        if kind == 0:
            mixed = s5_mixer(hn, s5_w_in[j], s5_lam_re[j], s5_lam_im[j], s5_log_dt[j], s5_b_re[j], s5_b_im[j],
                             s5_c_re[j], s5_c_im[j], s5_d[j], s5_w_out[j])
        elif kind == 1:
            mixed = stick_breaking_mixer(hn, sb_w_qkv[j], sb_w_out[j])
        else:
            mixed = rglru_mixer(hn, lru_w_in[j], lru_conv_w[j], lru_conv_b[j], lru_w_a[j], lru_b_a[j],
                                lru_w_x[j], lru_b_x[j], lru_lambda[j], lru_w_out[j])
        h = h + mixed
        h = h + 0.5 * swiglu_ffn(rms_norm(h, ffn2_norm[layer]), ffn2_w_in[layer], ffn2_w_out[layer])
    return rms_norm(h, final_norm)


import jax as _jax
import jax.numpy as _jnp

TWIN_FORMAT = 'train_step'
FWD_PARAMS = ['x', 'ffn1_norm', 'ffn1_w_in', 'ffn1_w_out', 'mix_norm', 'ffn2_norm', 'ffn2_w_in', 'ffn2_w_out', 'final_norm', 's5_w_in', 's5_lam_re', 's5_lam_im', 's5_log_dt', 's5_b_re', 's5_b_im', 's5_c_re', 's5_c_im', 's5_d', 's5_w_out', 'sb_w_qkv', 'sb_w_out', 'lru_w_in', 'lru_conv_w', 'lru_conv_b', 'lru_w_a', 'lru_b_a', 'lru_w_x', 'lru_b_x', 'lru_lambda', 'lru_w_out']
TWIN_WEIGHTS = ['ffn1_norm', 'ffn1_w_in', 'ffn1_w_out', 'mix_norm', 'ffn2_norm', 'ffn2_w_in', 'ffn2_w_out', 'final_norm', 's5_w_in', 's5_lam_re', 's5_lam_im', 's5_log_dt', 's5_b_re', 's5_b_im', 's5_c_re', 's5_c_im', 's5_d', 's5_w_out', 'sb_w_qkv', 'sb_w_out', 'lru_w_in', 'lru_conv_w', 'lru_conv_b', 'lru_w_a', 'lru_b_a', 'lru_w_x', 'lru_b_x', 'lru_lambda', 'lru_w_out']
TWIN_DIFF_INPUT = 'x'
TWIN_INPUTS = ['x', 'ffn1_norm', 'ffn1_w_in', 'ffn1_w_out', 'mix_norm', 'ffn2_norm', 'ffn2_w_in', 'ffn2_w_out', 'final_norm', 's5_w_in', 's5_lam_re', 's5_lam_im', 's5_log_dt', 's5_b_re', 's5_b_im', 's5_c_re', 's5_c_im', 's5_d', 's5_w_out', 'sb_w_qkv', 'sb_w_out', 'lru_w_in', 'lru_conv_w', 'lru_conv_b', 'lru_w_a', 'lru_b_a', 'lru_w_x', 'lru_b_x', 'lru_lambda', 'lru_w_out', 'loss_target', 'm_ffn1_norm', 'm_ffn1_w_in', 'm_ffn1_w_out', 'm_mix_norm', 'm_ffn2_norm', 'm_ffn2_w_in', 'm_ffn2_w_out', 'm_final_norm', 'm_s5_w_in', 'm_s5_lam_re', 'm_s5_lam_im', 'm_s5_log_dt', 'm_s5_b_re', 'm_s5_b_im', 'm_s5_c_re', 'm_s5_c_im', 'm_s5_d', 'm_s5_w_out', 'm_sb_w_qkv', 'm_sb_w_out', 'm_lru_w_in', 'm_lru_conv_w', 'm_lru_conv_b', 'm_lru_w_a', 'm_lru_b_a', 'm_lru_w_x', 'm_lru_b_x', 'm_lru_lambda', 'm_lru_w_out', 'v_ffn1_norm', 'v_ffn1_w_in', 'v_ffn1_w_out', 'v_mix_norm', 'v_ffn2_norm', 'v_ffn2_w_in', 'v_ffn2_w_out', 'v_final_norm', 'v_s5_w_in', 'v_s5_lam_re', 'v_s5_lam_im', 'v_s5_log_dt', 'v_s5_b_re', 'v_s5_b_im', 'v_s5_c_re', 'v_s5_c_im', 'v_s5_d', 'v_s5_w_out', 'v_sb_w_qkv', 'v_sb_w_out', 'v_lru_w_in', 'v_lru_conv_w', 'v_lru_conv_b', 'v_lru_w_a', 'v_lru_b_a', 'v_lru_w_x', 'v_lru_b_x', 'v_lru_lambda', 'v_lru_w_out']
TWIN_OUTPUTS = ['loss', 'grad_x', 'grad_ffn1_norm', 'grad_ffn1_w_in', 'grad_ffn1_w_out', 'grad_mix_norm', 'grad_ffn2_norm', 'grad_ffn2_w_in', 'grad_ffn2_w_out', 'grad_final_norm', 'grad_s5_w_in', 'grad_s5_lam_re', 'grad_s5_lam_im', 'grad_s5_log_dt', 'grad_s5_b_re', 'grad_s5_b_im', 'grad_s5_c_re', 'grad_s5_c_im', 'grad_s5_d', 'grad_s5_w_out', 'grad_sb_w_qkv', 'grad_sb_w_out', 'grad_lru_w_in', 'grad_lru_conv_w', 'grad_lru_conv_b', 'grad_lru_w_a', 'grad_lru_b_a', 'grad_lru_w_x', 'grad_lru_b_x', 'grad_lru_lambda', 'grad_lru_w_out', 'delta_ffn1_norm', 'delta_ffn1_w_in', 'delta_ffn1_w_out', 'delta_mix_norm', 'delta_ffn2_norm', 'delta_ffn2_w_in', 'delta_ffn2_w_out', 'delta_final_norm', 'delta_s5_w_in', 'delta_s5_lam_re', 'delta_s5_lam_im', 'delta_s5_log_dt', 'delta_s5_b_re', 'delta_s5_b_im', 'delta_s5_c_re', 'delta_s5_c_im', 'delta_s5_d', 'delta_s5_w_out', 'delta_sb_w_qkv', 'delta_sb_w_out', 'delta_lru_w_in', 'delta_lru_conv_w', 'delta_lru_conv_b', 'delta_lru_w_a', 'delta_lru_b_a', 'delta_lru_w_x', 'delta_lru_b_x', 'delta_lru_lambda', 'delta_lru_w_out', 'new_m_ffn1_norm', 'new_m_ffn1_w_in', 'new_m_ffn1_w_out', 'new_m_mix_norm', 'new_m_ffn2_norm', 'new_m_ffn2_w_in', 'new_m_ffn2_w_out', 'new_m_final_norm', 'new_m_s5_w_in', 'new_m_s5_lam_re', 'new_m_s5_lam_im', 'new_m_s5_log_dt', 'new_m_s5_b_re', 'new_m_s5_b_im', 'new_m_s5_c_re', 'new_m_s5_c_im', 'new_m_s5_d', 'new_m_s5_w_out', 'new_m_sb_w_qkv', 'new_m_sb_w_out', 'new_m_lru_w_in', 'new_m_lru_conv_w', 'new_m_lru_conv_b', 'new_m_lru_w_a', 'new_m_lru_b_a', 'new_m_lru_w_x', 'new_m_lru_b_x', 'new_m_lru_lambda', 'new_m_lru_w_out', 'new_v_ffn1_norm', 'new_v_ffn1_w_in', 'new_v_ffn1_w_out', 'new_v_mix_norm', 'new_v_ffn2_norm', 'new_v_ffn2_w_in', 'new_v_ffn2_w_out', 'new_v_final_norm', 'new_v_s5_w_in', 'new_v_s5_lam_re', 'new_v_s5_lam_im', 'new_v_s5_log_dt', 'new_v_s5_b_re', 'new_v_s5_b_im', 'new_v_s5_c_re', 'new_v_s5_c_im', 'new_v_s5_d', 'new_v_s5_w_out', 'new_v_sb_w_qkv', 'new_v_sb_w_out', 'new_v_lru_w_in', 'new_v_lru_conv_w', 'new_v_lru_conv_b', 'new_v_lru_w_a', 'new_v_lru_b_a', 'new_v_lru_w_x', 'new_v_lru_b_x', 'new_v_lru_lambda', 'new_v_lru_w_out']
TWIN_LEAF_KINDS = {'loss': 'loss', 'grad_x': 'grad_x', 'grad_ffn1_norm': 'grad_w', 'grad_ffn1_w_in': 'grad_w', 'grad_ffn1_w_out': 'grad_w', 'grad_mix_norm': 'grad_w', 'grad_ffn2_norm': 'grad_w', 'grad_ffn2_w_in': 'grad_w', 'grad_ffn2_w_out': 'grad_w', 'grad_final_norm': 'grad_w', 'grad_s5_w_in': 'grad_w', 'grad_s5_lam_re': 'grad_w', 'grad_s5_lam_im': 'grad_w', 'grad_s5_log_dt': 'grad_w', 'grad_s5_b_re': 'grad_w', 'grad_s5_b_im': 'grad_w', 'grad_s5_c_re': 'grad_w', 'grad_s5_c_im': 'grad_w', 'grad_s5_d': 'grad_w', 'grad_s5_w_out': 'grad_w', 'grad_sb_w_qkv': 'grad_w', 'grad_sb_w_out': 'grad_w', 'grad_lru_w_in': 'grad_w', 'grad_lru_conv_w': 'grad_w', 'grad_lru_conv_b': 'grad_w', 'grad_lru_w_a': 'grad_w', 'grad_lru_b_a': 'grad_w', 'grad_lru_w_x': 'grad_w', 'grad_lru_b_x': 'grad_w', 'grad_lru_lambda': 'grad_w', 'grad_lru_w_out': 'grad_w', 'delta_ffn1_norm': 'delta_w', 'delta_ffn1_w_in': 'delta_w', 'delta_ffn1_w_out': 'delta_w', 'delta_mix_norm': 'delta_w', 'delta_ffn2_norm': 'delta_w', 'delta_ffn2_w_in': 'delta_w', 'delta_ffn2_w_out': 'delta_w', 'delta_final_norm': 'delta_w', 'delta_s5_w_in': 'delta_w', 'delta_s5_lam_re': 'delta_w', 'delta_s5_lam_im': 'delta_w', 'delta_s5_log_dt': 'delta_w', 'delta_s5_b_re': 'delta_w', 'delta_s5_b_im': 'delta_w', 'delta_s5_c_re': 'delta_w', 'delta_s5_c_im': 'delta_w', 'delta_s5_d': 'delta_w', 'delta_s5_w_out': 'delta_w', 'delta_sb_w_qkv': 'delta_w', 'delta_sb_w_out': 'delta_w', 'delta_lru_w_in': 'delta_w', 'delta_lru_conv_w': 'delta_w', 'delta_lru_conv_b': 'delta_w', 'delta_lru_w_a': 'delta_w', 'delta_lru_b_a': 'delta_w', 'delta_lru_w_x': 'delta_w', 'delta_lru_b_x': 'delta_w', 'delta_lru_lambda': 'delta_w', 'delta_lru_w_out': 'delta_w', 'new_m_ffn1_norm': 'new_m', 'new_m_ffn1_w_in': 'new_m', 'new_m_ffn1_w_out': 'new_m', 'new_m_mix_norm': 'new_m', 'new_m_ffn2_norm': 'new_m', 'new_m_ffn2_w_in': 'new_m', 'new_m_ffn2_w_out': 'new_m', 'new_m_final_norm': 'new_m', 'new_m_s5_w_in': 'new_m', 'new_m_s5_lam_re': 'new_m', 'new_m_s5_lam_im': 'new_m', 'new_m_s5_log_dt': 'new_m', 'new_m_s5_b_re': 'new_m', 'new_m_s5_b_im': 'new_m', 'new_m_s5_c_re': 'new_m', 'new_m_s5_c_im': 'new_m', 'new_m_s5_d': 'new_m', 'new_m_s5_w_out': 'new_m', 'new_m_sb_w_qkv': 'new_m', 'new_m_sb_w_out': 'new_m', 'new_m_lru_w_in': 'new_m', 'new_m_lru_conv_w': 'new_m', 'new_m_lru_conv_b': 'new_m', 'new_m_lru_w_a': 'new_m', 'new_m_lru_b_a': 'new_m', 'new_m_lru_w_x': 'new_m', 'new_m_lru_b_x': 'new_m', 'new_m_lru_lambda': 'new_m', 'new_m_lru_w_out': 'new_m', 'new_v_ffn1_norm': 'new_v', 'new_v_ffn1_w_in': 'new_v', 'new_v_ffn1_w_out': 'new_v', 'new_v_mix_norm': 'new_v', 'new_v_ffn2_norm': 'new_v', 'new_v_ffn2_w_in': 'new_v', 'new_v_ffn2_w_out': 'new_v', 'new_v_final_norm': 'new_v', 'new_v_s5_w_in': 'new_v', 'new_v_s5_lam_re': 'new_v', 'new_v_s5_lam_im': 'new_v', 'new_v_s5_log_dt': 'new_v', 'new_v_s5_b_re': 'new_v', 'new_v_s5_b_im': 'new_v', 'new_v_s5_c_re': 'new_v', 'new_v_s5_c_im': 'new_v', 'new_v_s5_d': 'new_v', 'new_v_s5_w_out': 'new_v', 'new_v_sb_w_qkv': 'new_v', 'new_v_sb_w_out': 'new_v', 'new_v_lru_w_in': 'new_v', 'new_v_lru_conv_w': 'new_v', 'new_v_lru_conv_b': 'new_v', 'new_v_lru_w_a': 'new_v', 'new_v_lru_b_a': 'new_v', 'new_v_lru_w_x': 'new_v', 'new_v_lru_b_x': 'new_v', 'new_v_lru_lambda': 'new_v', 'new_v_lru_w_out': 'new_v'}


def _forward(args):
    return _fwd_reference(*[args[k] for k in FWD_PARAMS])


def _output_shape():
    def fwd():
        inp = _fwd_setup_inputs(0)
        return _fwd_reference(*[inp[k] for k in FWD_PARAMS])
    out = _jax.eval_shape(fwd)
    return out.shape, out.dtype

N_MICROBATCH = 1
ADAM_LR = 0.001
ADAM_B1 = 0.9
ADAM_B2 = 0.999
ADAM_EPS = 1e-08
ADAM_WD = 0.01
ADAM_STEP = 10
PER_EXAMPLE_BATCH_AXIS = {'x': 0, 'loss_target': 0}
SHARED_INPUTS = []
_WEIGHT_DTYPES = {'ffn1_norm': _jnp.float32, 'ffn1_w_in': _jnp.float32, 'ffn1_w_out': _jnp.float32, 'mix_norm': _jnp.float32, 'ffn2_norm': _jnp.float32, 'ffn2_w_in': _jnp.float32, 'ffn2_w_out': _jnp.float32, 'final_norm': _jnp.float32, 's5_w_in': _jnp.float32, 's5_lam_re': _jnp.float32, 's5_lam_im': _jnp.float32, 's5_log_dt': _jnp.float32, 's5_b_re': _jnp.float32, 's5_b_im': _jnp.float32, 's5_c_re': _jnp.float32, 's5_c_im': _jnp.float32, 's5_d': _jnp.float32, 's5_w_out': _jnp.float32, 'sb_w_qkv': _jnp.float32, 'sb_w_out': _jnp.float32, 'lru_w_in': _jnp.float32, 'lru_conv_w': _jnp.float32, 'lru_conv_b': _jnp.float32, 'lru_w_a': _jnp.float32, 'lru_b_a': _jnp.float32, 'lru_w_x': _jnp.float32, 'lru_b_x': _jnp.float32, 'lru_lambda': _jnp.float32, 'lru_w_out': _jnp.float32}
MOMENT_SCALE = {'ffn1_norm': 1.033074e-01, 'ffn1_w_in': 4.387373e-02, 'ffn1_w_out': 7.159282e-02, 'mix_norm': 1.367322e-01, 'ffn2_norm': 9.121427e-02, 'ffn2_w_in': 3.781797e-02, 'ffn2_w_out': 6.169771e-02, 'final_norm': 6.420489e+01, 's5_w_in': 8.903872e-02, 's5_lam_re': 7.443808e-03, 's5_lam_im': 6.957270e-03, 's5_log_dt': 4.465652e+00, 's5_b_re': 4.462729e-03, 's5_b_im': 4.541392e-03, 's5_c_re': 6.260103e-03, 's5_c_im': 6.459412e-03, 's5_d': 1.396365e-01, 's5_w_out': 8.889015e-02, 'sb_w_qkv': 1.067617e-01, 'sb_w_out': 1.622320e-01, 'lru_w_in': 1.410344e-01, 'lru_conv_w': 1.381492e-01, 'lru_conv_b': 8.837800e-01, 'lru_w_a': 1.827340e-02, 'lru_b_a': 3.046821e-02, 'lru_w_x': 3.347239e-02, 'lru_b_x': 5.910306e-02, 'lru_lambda': 7.031365e-02, 'lru_w_out': 1.412849e-01}


def _to_microbatches(a, axis):
    t = _jnp.moveaxis(a, axis, 0)
    t = t.reshape((N_MICROBATCH, t.shape[0] // N_MICROBATCH) + t.shape[1:])
    return _jnp.moveaxis(t, 1, axis + 1)


def setup_inputs(seed: int = 0) -> dict:
    inp = _fwd_setup_inputs(seed)
    key = _jax.random.fold_in(_jax.random.key(seed), 7919)
    shape, _ = _output_shape()
    out = dict(inp)
    out["loss_target"] = _jax.random.normal(_jax.random.fold_in(key, 0), shape, _jnp.float32)
    for i, name in enumerate(TWIN_WEIGHTS):
        w = inp[name].astype(_jnp.float32)
        if MOMENT_SCALE is None:
            s = _jnp.sqrt(_jnp.mean(_jnp.square(w)) + 1e-30)
        else:
            s = MOMENT_SCALE[name]
        km, kv = _jax.random.split(_jax.random.fold_in(key, i + 1))
        out[name] = w
        out["m_" + name] = s * _jax.random.normal(km, w.shape, _jnp.float32)
        out["v_" + name] = (s * s) * _jax.random.uniform(kv, w.shape, _jnp.float32, 0.5, 1.5)
    if N_MICROBATCH > 1:
        for name, axis in PER_EXAMPLE_BATCH_AXIS.items():
            out[name] = _to_microbatches(out[name], axis)
    return {'x': out['x'], 'ffn1_norm': out['ffn1_norm'], 'ffn1_w_in': out['ffn1_w_in'], 'ffn1_w_out': out['ffn1_w_out'], 'mix_norm': out['mix_norm'], 'ffn2_norm': out['ffn2_norm'], 'ffn2_w_in': out['ffn2_w_in'], 'ffn2_w_out': out['ffn2_w_out'], 'final_norm': out['final_norm'], 's5_w_in': out['s5_w_in'], 's5_lam_re': out['s5_lam_re'], 's5_lam_im': out['s5_lam_im'], 's5_log_dt': out['s5_log_dt'], 's5_b_re': out['s5_b_re'], 's5_b_im': out['s5_b_im'], 's5_c_re': out['s5_c_re'], 's5_c_im': out['s5_c_im'], 's5_d': out['s5_d'], 's5_w_out': out['s5_w_out'], 'sb_w_qkv': out['sb_w_qkv'], 'sb_w_out': out['sb_w_out'], 'lru_w_in': out['lru_w_in'], 'lru_conv_w': out['lru_conv_w'], 'lru_conv_b': out['lru_conv_b'], 'lru_w_a': out['lru_w_a'], 'lru_b_a': out['lru_b_a'], 'lru_w_x': out['lru_w_x'], 'lru_b_x': out['lru_b_x'], 'lru_lambda': out['lru_lambda'], 'lru_w_out': out['lru_w_out'], 'loss_target': out['loss_target'], 'm_ffn1_norm': out['m_ffn1_norm'], 'm_ffn1_w_in': out['m_ffn1_w_in'], 'm_ffn1_w_out': out['m_ffn1_w_out'], 'm_mix_norm': out['m_mix_norm'], 'm_ffn2_norm': out['m_ffn2_norm'], 'm_ffn2_w_in': out['m_ffn2_w_in'], 'm_ffn2_w_out': out['m_ffn2_w_out'], 'm_final_norm': out['m_final_norm'], 'm_s5_w_in': out['m_s5_w_in'], 'm_s5_lam_re': out['m_s5_lam_re'], 'm_s5_lam_im': out['m_s5_lam_im'], 'm_s5_log_dt': out['m_s5_log_dt'], 'm_s5_b_re': out['m_s5_b_re'], 'm_s5_b_im': out['m_s5_b_im'], 'm_s5_c_re': out['m_s5_c_re'], 'm_s5_c_im': out['m_s5_c_im'], 'm_s5_d': out['m_s5_d'], 'm_s5_w_out': out['m_s5_w_out'], 'm_sb_w_qkv': out['m_sb_w_qkv'], 'm_sb_w_out': out['m_sb_w_out'], 'm_lru_w_in': out['m_lru_w_in'], 'm_lru_conv_w': out['m_lru_conv_w'], 'm_lru_conv_b': out['m_lru_conv_b'], 'm_lru_w_a': out['m_lru_w_a'], 'm_lru_b_a': out['m_lru_b_a'], 'm_lru_w_x': out['m_lru_w_x'], 'm_lru_b_x': out['m_lru_b_x'], 'm_lru_lambda': out['m_lru_lambda'], 'm_lru_w_out': out['m_lru_w_out'], 'v_ffn1_norm': out['v_ffn1_norm'], 'v_ffn1_w_in': out['v_ffn1_w_in'], 'v_ffn1_w_out': out['v_ffn1_w_out'], 'v_mix_norm': out['v_mix_norm'], 'v_ffn2_norm': out['v_ffn2_norm'], 'v_ffn2_w_in': out['v_ffn2_w_in'], 'v_ffn2_w_out': out['v_ffn2_w_out'], 'v_final_norm': out['v_final_norm'], 'v_s5_w_in': out['v_s5_w_in'], 'v_s5_lam_re': out['v_s5_lam_re'], 'v_s5_lam_im': out['v_s5_lam_im'], 'v_s5_log_dt': out['v_s5_log_dt'], 'v_s5_b_re': out['v_s5_b_re'], 'v_s5_b_im': out['v_s5_b_im'], 'v_s5_c_re': out['v_s5_c_re'], 'v_s5_c_im': out['v_s5_c_im'], 'v_s5_d': out['v_s5_d'], 'v_s5_w_out': out['v_s5_w_out'], 'v_sb_w_qkv': out['v_sb_w_qkv'], 'v_sb_w_out': out['v_sb_w_out'], 'v_lru_w_in': out['v_lru_w_in'], 'v_lru_conv_w': out['v_lru_conv_w'], 'v_lru_conv_b': out['v_lru_conv_b'], 'v_lru_w_a': out['v_lru_w_a'], 'v_lru_b_a': out['v_lru_b_a'], 'v_lru_w_x': out['v_lru_w_x'], 'v_lru_b_x': out['v_lru_b_x'], 'v_lru_lambda': out['v_lru_lambda'], 'v_lru_w_out': out['v_lru_w_out']}


def _loss(weights, diff, rest, loss_target):
    with _jax.named_scope("forward"):
        args = {**rest, TWIN_DIFF_INPUT: diff, **{k: w.astype(_WEIGHT_DTYPES[k]) for k, w in weights.items()}}
        y = _forward(args)
    with _jax.named_scope("loss_head"):
        err = _jnp.square(y.astype(_jnp.float32) - loss_target)
        return 0.5 * _jnp.sum(_jnp.mean(err, axis=-1)) if err.ndim else 0.5 * err


def _adamw(w, g, m, v):
    m = ADAM_B1 * m + (1.0 - ADAM_B1) * g
    v = ADAM_B2 * v + (1.0 - ADAM_B2) * _jnp.square(g)
    m_hat = m / (1.0 - ADAM_B1 ** ADAM_STEP)
    v_hat = v / (1.0 - ADAM_B2 ** ADAM_STEP)
    delta = -ADAM_LR * (m_hat / (_jnp.sqrt(v_hat) + ADAM_EPS) + ADAM_WD * w)
    return delta, m, v


def reference(x, ffn1_norm, ffn1_w_in, ffn1_w_out, mix_norm, ffn2_norm, ffn2_w_in, ffn2_w_out, final_norm, s5_w_in, s5_lam_re, s5_lam_im, s5_log_dt, s5_b_re, s5_b_im, s5_c_re, s5_c_im, s5_d, s5_w_out, sb_w_qkv, sb_w_out, lru_w_in, lru_conv_w, lru_conv_b, lru_w_a, lru_b_a, lru_w_x, lru_b_x, lru_lambda, lru_w_out, loss_target, m_ffn1_norm, m_ffn1_w_in, m_ffn1_w_out, m_mix_norm, m_ffn2_norm, m_ffn2_w_in, m_ffn2_w_out, m_final_norm, m_s5_w_in, m_s5_lam_re, m_s5_lam_im, m_s5_log_dt, m_s5_b_re, m_s5_b_im, m_s5_c_re, m_s5_c_im, m_s5_d, m_s5_w_out, m_sb_w_qkv, m_sb_w_out, m_lru_w_in, m_lru_conv_w, m_lru_conv_b, m_lru_w_a, m_lru_b_a, m_lru_w_x, m_lru_b_x, m_lru_lambda, m_lru_w_out, v_ffn1_norm, v_ffn1_w_in, v_ffn1_w_out, v_mix_norm, v_ffn2_norm, v_ffn2_w_in, v_ffn2_w_out, v_final_norm, v_s5_w_in, v_s5_lam_re, v_s5_lam_im, v_s5_log_dt, v_s5_b_re, v_s5_b_im, v_s5_c_re, v_s5_c_im, v_s5_d, v_s5_w_out, v_sb_w_qkv, v_sb_w_out, v_lru_w_in, v_lru_conv_w, v_lru_conv_b, v_lru_w_a, v_lru_b_a, v_lru_w_x, v_lru_b_x, v_lru_lambda, v_lru_w_out):
    given = dict(x=x, ffn1_norm=ffn1_norm, ffn1_w_in=ffn1_w_in, ffn1_w_out=ffn1_w_out, mix_norm=mix_norm, ffn2_norm=ffn2_norm, ffn2_w_in=ffn2_w_in, ffn2_w_out=ffn2_w_out, final_norm=final_norm, s5_w_in=s5_w_in, s5_lam_re=s5_lam_re, s5_lam_im=s5_lam_im, s5_log_dt=s5_log_dt, s5_b_re=s5_b_re, s5_b_im=s5_b_im, s5_c_re=s5_c_re, s5_c_im=s5_c_im, s5_d=s5_d, s5_w_out=s5_w_out, sb_w_qkv=sb_w_qkv, sb_w_out=sb_w_out, lru_w_in=lru_w_in, lru_conv_w=lru_conv_w, lru_conv_b=lru_conv_b, lru_w_a=lru_w_a, lru_b_a=lru_b_a, lru_w_x=lru_w_x, lru_b_x=lru_b_x, lru_lambda=lru_lambda, lru_w_out=lru_w_out, loss_target=loss_target, m_ffn1_norm=m_ffn1_norm, m_ffn1_w_in=m_ffn1_w_in, m_ffn1_w_out=m_ffn1_w_out, m_mix_norm=m_mix_norm, m_ffn2_norm=m_ffn2_norm, m_ffn2_w_in=m_ffn2_w_in, m_ffn2_w_out=m_ffn2_w_out, m_final_norm=m_final_norm, m_s5_w_in=m_s5_w_in, m_s5_lam_re=m_s5_lam_re, m_s5_lam_im=m_s5_lam_im, m_s5_log_dt=m_s5_log_dt, m_s5_b_re=m_s5_b_re, m_s5_b_im=m_s5_b_im, m_s5_c_re=m_s5_c_re, m_s5_c_im=m_s5_c_im, m_s5_d=m_s5_d, m_s5_w_out=m_s5_w_out, m_sb_w_qkv=m_sb_w_qkv, m_sb_w_out=m_sb_w_out, m_lru_w_in=m_lru_w_in, m_lru_conv_w=m_lru_conv_w, m_lru_conv_b=m_lru_conv_b, m_lru_w_a=m_lru_w_a, m_lru_b_a=m_lru_b_a, m_lru_w_x=m_lru_w_x, m_lru_b_x=m_lru_b_x, m_lru_lambda=m_lru_lambda, m_lru_w_out=m_lru_w_out, v_ffn1_norm=v_ffn1_norm, v_ffn1_w_in=v_ffn1_w_in, v_ffn1_w_out=v_ffn1_w_out, v_mix_norm=v_mix_norm, v_ffn2_norm=v_ffn2_norm, v_ffn2_w_in=v_ffn2_w_in, v_ffn2_w_out=v_ffn2_w_out, v_final_norm=v_final_norm, v_s5_w_in=v_s5_w_in, v_s5_lam_re=v_s5_lam_re, v_s5_lam_im=v_s5_lam_im, v_s5_log_dt=v_s5_log_dt, v_s5_b_re=v_s5_b_re, v_s5_b_im=v_s5_b_im, v_s5_c_re=v_s5_c_re, v_s5_c_im=v_s5_c_im, v_s5_d=v_s5_d, v_s5_w_out=v_s5_w_out, v_sb_w_qkv=v_sb_w_qkv, v_sb_w_out=v_sb_w_out, v_lru_w_in=v_lru_w_in, v_lru_conv_w=v_lru_conv_w, v_lru_conv_b=v_lru_conv_b, v_lru_w_a=v_lru_w_a, v_lru_b_a=v_lru_b_a, v_lru_w_x=v_lru_w_x, v_lru_b_x=v_lru_b_x, v_lru_lambda=v_lru_lambda, v_lru_w_out=v_lru_w_out)
    weights = {n: given[n] for n in TWIN_WEIGHTS}
    shared = {n: given[n] for n in SHARED_INPUTS}
    per_example = {n: given[n] for n in ['x']}
    grad_fn = _jax.value_and_grad(_loss, argnums=(0, 1))

    def one_microbatch(ex, loss_target):
        ex = dict(ex)
        diff = ex.pop(TWIN_DIFF_INPUT)
        return grad_fn(weights, diff, {**shared, **ex}, loss_target)

    if N_MICROBATCH == 1:
        loss, (grad_w, grad_x) = one_microbatch(per_example, given["loss_target"])
    else:
        def body(carry, xs):
            loss_sum, grad_sum = carry
            l_k, (gw_k, gx_k) = one_microbatch(xs[0], xs[1])
            with _jax.named_scope("update"):
                return (loss_sum + l_k, _jax.tree.map(_jnp.add, grad_sum, gw_k)), gx_k

        init = (_jnp.zeros((), _jnp.float32), _jax.tree.map(_jnp.zeros_like, weights))
        (loss, grad_w), grad_x = _jax.lax.scan(body, init, (per_example, given["loss_target"]))
    with _jax.named_scope("update"):
        delta_w, new_m, new_v = {}, {}, {}
        for n in TWIN_WEIGHTS:
            delta_w[n], new_m[n], new_v[n] = _adamw(weights[n], grad_w[n], given["m_" + n], given["v_" + n])
    return (loss, grad_x, *[grad_w[n] for n in TWIN_WEIGHTS], *[delta_w[n] for n in TWIN_WEIGHTS],
            *[new_m[n] for n in TWIN_WEIGHTS], *[new_v[n] for n in TWIN_WEIGHTS])
```

```python
import functools
import math

import jax
import jax.numpy as jnp
from jax import lax
from jax.experimental import pallas as pl
from jax.experimental.pallas import tpu as pltpu

F32 = jnp.float32
BF16 = jnp.bfloat16
HI = lax.Precision.HIGHEST
MESH = pl.DeviceIdType.MESH

N_DEV = 8
RMS_EPS = 1e-6
S5_GROUP = 16
S5_CHUNK = 16
SB_HEAD_DIM = 64
LRU_CONV = 4
LRU_C = 8.0
ADAM_LR, ADAM_B1, ADAM_B2, ADAM_EPS, ADAM_WD, ADAM_STEP = 0.001, 0.9, 0.999, 1e-08, 0.01, 10
VMEM_LIMIT_BYTES = 56 * 1024 * 1024
SUBLANES = 8

NN = (((1,), (0,)), ((), ()))
NT = (((1,), (1,)), ((), ()))
TN = (((0,), (0,)), ((), ()))

SHARD_AXIS = dict(
    ffn1_w_in=2, ffn1_w_out=1, ffn2_w_in=2, ffn2_w_out=1, s5_w_in=1, s5_d=1, s5_w_out=2, sb_w_qkv=2, sb_w_out=1,
    lru_w_in=2, lru_conv_w=2, lru_conv_b=1, lru_w_a=2, lru_b_a=2, lru_w_x=2, lru_b_x=2, lru_lambda=1, lru_w_out=1)
MIXER_BIG = ("s5_w_in", "s5_w_out", "sb_w_qkv", "sb_w_out", "lru_w_in", "lru_w_a", "lru_w_x", "lru_w_out")
SMALL_SHARDED = ("s5_d", "lru_conv_w", "lru_conv_b", "lru_b_a", "lru_b_x", "lru_lambda")
REPLICATED = ("ffn1_norm", "mix_norm", "ffn2_norm", "final_norm", "s5_lam_re", "s5_lam_im", "s5_log_dt",
              "s5_b_re", "s5_b_im", "s5_c_re", "s5_c_im")
WEIGHTS = ("ffn1_norm", "ffn1_w_in", "ffn1_w_out", "mix_norm", "ffn2_norm", "ffn2_w_in", "ffn2_w_out", "final_norm",
           "s5_w_in", "s5_lam_re", "s5_lam_im", "s5_log_dt", "s5_b_re", "s5_b_im", "s5_c_re", "s5_c_im", "s5_d",
           "s5_w_out", "sb_w_qkv", "sb_w_out", "lru_w_in", "lru_conv_w", "lru_conv_b", "lru_w_a", "lru_b_a",
           "lru_w_x", "lru_b_x", "lru_lambda", "lru_w_out")


def _dot(a, b, dims=NN, prec=None):
    return lax.dot_general(a, b, dims, precision=prec, preferred_element_type=F32)


def _params(*sem):
    return pltpu.CompilerParams(dimension_semantics=sem, vmem_limit_bytes=VMEM_LIMIT_BYTES)


def _tile(n, pref):
    return min(pref, n)


def _sigmoid(x):
    return jax.nn.sigmoid(x)


def _softplus(x):
    return jnp.maximum(x, 0.0) + jnp.log(1.0 + jnp.exp(-jnp.abs(x)))


_GELU_C = math.sqrt(2.0 / math.pi)


def _gelu(x):
    return 0.5 * x * (1.0 + jnp.tanh(_GELU_C * (x + 0.044715 * x * x * x)))


def _gelu_grad(x):
    t = jnp.tanh(_GELU_C * (x + 0.044715 * x * x * x))
    return 0.5 * (1.0 + t) + 0.5 * x * (1.0 - t * t) * _GELU_C * (1.0 + 3.0 * 0.044715 * x * x)


def _rms(x):
    r = lax.rsqrt(jnp.mean(x * x, axis=1, keepdims=True) + RMS_EPS)
    return r, x * r


def _rms_bwd(dhn, xhat, r, g):
    dxhat = dhn * g
    return r * (dxhat - xhat * jnp.mean(dxhat * xhat, axis=1, keepdims=True))


def _one_minus_a2_sqrt(log_a):
    t = jnp.tanh(log_a)
    return jnp.sqrt(-2.0 * t / (1.0 - t))


def _shift_down(cur, prev8, k, first):
    if k == 0:
        return cur
    row8 = lax.broadcasted_iota(jnp.int32, prev8.shape, 0)
    rolled = pltpu.roll(cur, k, 0)
    edge = jnp.where(first, 0.0, pltpu.roll(prev8, k, 0))
    top = jnp.where(row8 < k, edge, rolled[0:SUBLANES])
    return jnp.concatenate([top, rolled[SUBLANES:]], axis=0)


def _shift_up(cur, next8, k, last):
    if k == 0:
        return cur
    tm = cur.shape[0]
    row8 = lax.broadcasted_iota(jnp.int32, next8.shape, 0)
    rolled = pltpu.roll(cur, tm - k, 0)
    edge = jnp.where(last, 0.0, pltpu.roll(next8, SUBLANES - k, 0))
    bottom = jnp.where(row8 >= SUBLANES - k, edge, rolled[tm - SUBLANES:tm])
    return jnp.concatenate([rolled[:tm - SUBLANES], bottom], axis=0)


def _rowwise(name, fn, ins, out_tiled, out_acc, n_rows, tm, reverse=False):
    nt = n_rows // tm
    per8 = tm // SUBLANES
    n8 = n_rows // SUBLANES
    n_in, n_ot = len(ins), len(out_tiled)

    def pos(i):
        return nt - 1 - i if reverse else i

    in_specs, args = [], []
    for spec in ins:
        kind, arr = spec[0], spec[1]
        args.append(arr)
        if kind == 'b':
            in_specs.append(pl.BlockSpec(arr.shape, lambda i, nd=arr.ndim: (0,) * nd))
        elif kind == 't':
            in_specs.append(pl.BlockSpec((tm, spec[2]), lambda i, cb=spec[3]: (pos(i), cb)))
        elif kind == 'p':
            in_specs.append(pl.BlockSpec((SUBLANES, spec[2]),
                                         lambda i, cb=spec[3]: (jnp.maximum(pos(i) * per8 - 1, 0), cb)))
        else:
            in_specs.append(pl.BlockSpec((SUBLANES, spec[2]),
                                         lambda i, cb=spec[3]: (jnp.minimum((pos(i) + 1) * per8, n8 - 1), cb)))

    def body(*refs):
        i = pl.program_id(0)
        vals = [r[...] for r in refs[:n_in]]
        outs = refs[n_in:]
        touts, aouts = fn(pos(i), nt, *vals)
        for r, v in zip(outs[:n_ot], touts):
            r[...] = v.astype(r.dtype)
        if out_acc:
            @pl.when(i == 0)
            def _():
                for r in outs[n_ot:]:
                    r[...] = jnp.zeros(r.shape, r.dtype)
            for r, v in zip(outs[n_ot:], aouts):
                r[...] += v

    out_specs = [pl.BlockSpec((tm, n), lambda i: (pos(i), 0)) for n, _ in out_tiled]
    out_specs += [pl.BlockSpec((r, n), lambda i: (0, 0)) for r, n in out_acc]
    out_shape = [jax.ShapeDtypeStruct((n_rows, n), dt) for n, dt in out_tiled]
    out_shape += [jax.ShapeDtypeStruct((r, n), F32) for r, n in out_acc]
    return pl.pallas_call(body, grid=(nt,), in_specs=in_specs, out_specs=out_specs, out_shape=out_shape, name=name,
                          compiler_params=_params("arbitrary"))(*args)


def _all_gather(name, block):
    def body(x_ref, out_ref, send_sems, recv_sems, local_sem):
        x, y, c = lax.axis_index("x"), lax.axis_index("y"), lax.axis_index("c")
        me, sibling = (x, y, c), (x, y, 1 - c)
        chips = [(1 - x, y), (x, 1 - y), (1 - x, 1 - y)]

        def rows(px, py, pc):
            return out_ref.at[4 * px + 2 * py + pc]

        def copy(k, blk, to, src=None):
            return pltpu.make_async_remote_copy(
                src_ref=rows(*blk) if src is None else src, dst_ref=rows(*blk),
                send_sem=send_sems.at[k], recv_sem=recv_sems.at[k], device_id=to, device_id_type=MESH)

        mine = pltpu.make_async_copy(x_ref, rows(*me), local_sem)
        mine.start()
        first = [copy(0, me, sibling, src=x_ref)]
        first += [copy(1 + j, me, (*chip, c), src=x_ref) for j, chip in enumerate(chips)]
        for cp in first:
            cp.start()
        passed = [copy(4 + j, (*chip, c), sibling) for j, chip in enumerate(chips)]
        for j, chip in enumerate(chips):
            copy(1 + j, (*chip, c), me).wait_recv()
            passed[j].start()
        copy(0, sibling, me).wait_recv()
        for j, chip in enumerate(chips):
            copy(4 + j, (*chip, 1 - c), me).wait_recv()
        for cp in first + passed:
            cp.wait_send()
        mine.wait()

    return pl.pallas_call(
        body, name=name, out_shape=jax.ShapeDtypeStruct((N_DEV,) + block.shape, block.dtype),
        in_specs=[pl.BlockSpec(memory_space=pl.ANY)], out_specs=pl.BlockSpec(memory_space=pl.ANY),
        scratch_shapes=[pltpu.SemaphoreType.DMA((7,)), pltpu.SemaphoreType.DMA((7,)), pltpu.SemaphoreType.DMA(())],
    )(block)


def _exchange(name, send):
    def body(s_ref, r_ref, send_sems, recv_sems, local_sem):
        x, y, c = lax.axis_index("x"), lax.axis_index("y"), lax.axis_index("c")
        me = 4 * x + 2 * y + c
        mine = pltpu.make_async_copy(s_ref.at[me], r_ref.at[me], local_sem)
        mine.start()
        copies = []
        for k in range(1, N_DEV):
            dx, dy, dc = (k >> 2) & 1, (k >> 1) & 1, k & 1
            px = 1 - x if dx else x
            py = 1 - y if dy else y
            pc = 1 - c if dc else c
            peer = 4 * px + 2 * py + pc
            copies.append((pltpu.make_async_remote_copy(
                src_ref=s_ref.at[peer], dst_ref=r_ref.at[me], send_sem=send_sems.at[k - 1],
                recv_sem=recv_sems.at[k - 1], device_id=(px, py, pc), device_id_type=MESH), peer))
        for cp, _ in copies:
            cp.start()
        for k, (cp, peer) in enumerate(copies):
            pltpu.make_async_remote_copy(
                src_ref=s_ref.at[peer], dst_ref=r_ref.at[peer], send_sem=send_sems.at[k], recv_sem=recv_sems.at[k],
                device_id=(x, y, c), device_id_type=MESH).wait_recv()
        for cp, _ in copies:
            cp.wait_send()
        mine.wait()

    return pl.pallas_call(
        body, name=name, out_shape=jax.ShapeDtypeStruct(send.shape, send.dtype),
        in_specs=[pl.BlockSpec(memory_space=pl.ANY)], out_specs=pl.BlockSpec(memory_space=pl.ANY),
        scratch_shapes=[pltpu.SemaphoreType.DMA((7,)), pltpu.SemaphoreType.DMA((7,)), pltpu.SemaphoreType.DMA(())],
    )(send)


def _unshard(gathered, axis):
    local = gathered.shape[1:]
    full = jnp.moveaxis(gathered, 0, axis)
    return full.reshape(local[:axis] + (N_DEV * local[axis],) + local[axis + 1:])


def _shard_blocks(full, axis):
    s = full.shape
    cut = full.reshape(s[:axis] + (N_DEV, s[axis] // N_DEV) + s[axis + 1:])
    return jnp.moveaxis(cut, axis, 0)


def _mm_fwd(name, a, w, out_dtype, gain=None, resid=None, glu=False):
    n_rows, k = a.shape
    n = w.shape[1]
    tm = _tile(n_rows, 512)
    n_out = n // 2 if glu else n

    def body(*refs):
        it = iter(refs)
        a_ref, w_ref = next(it), next(it)
        g_ref = next(it) if gain is not None else None
        r_ref = next(it) if resid is not None else None
        outs = list(it)
        av = a_ref[...]
        if g_ref is not None:
            _, xhat = _rms(av)
            av = xhat * g_ref[...]
        res = _dot(av.astype(BF16), w_ref[...])
        if glu:
            outs[1][...] = res.astype(outs[1].dtype)
            res = res[:, :n_out] * _sigmoid(res[:, n_out:])
        if r_ref is not None:
            res = res + r_ref[...]
        outs[0][...] = res.astype(outs[0].dtype)

    args = [a, w]
    in_specs = [pl.BlockSpec((tm, k), lambda i: (i, 0)), pl.BlockSpec((k, n), lambda i: (0, 0))]
    if gain is not None:
        args.append(gain)
        in_specs.append(pl.BlockSpec((1, k), lambda i: (0, 0)))
    if resid is not None:
        args.append(resid)
        in_specs.append(pl.BlockSpec((tm, n_out), lambda i: (i, 0)))
    out_shape = [jax.ShapeDtypeStruct((n_rows, n_out), out_dtype)]
    out_specs = [pl.BlockSpec((tm, n_out), lambda i: (i, 0))]
    if glu:
        out_shape.append(jax.ShapeDtypeStruct((n_rows, n), F32))
        out_specs.append(pl.BlockSpec((tm, n), lambda i: (i, 0)))
    return pl.pallas_call(body, grid=(n_rows // tm,), in_specs=in_specs, out_specs=out_specs, out_shape=out_shape,
                          name=name, compiler_params=_params("parallel"))(*args)


def _mm_bwd(name, a, d, w, gain=None, dres=None):
    n_rows, k = a.shape
    n = w.shape[1]
    tm = _tile(n_rows, 512)

    def body(*refs):
        it = iter(refs)
        a_ref, d_ref, w_ref = next(it), next(it), next(it)
        g_ref = next(it) if gain is not None else None
        r_ref = next(it) if gain is not None else None
        da_ref, dw_ref = next(it), next(it)
        dg_ref = next(it) if gain is not None else None
        i = pl.program_id(0)

        @pl.when(i == 0)
        def _():
            dw_ref[...] = jnp.zeros(dw_ref.shape, F32)
            if dg_ref is not None:
                dg_ref[...] = jnp.zeros(dg_ref.shape, F32)

        av = a_ref[...]
        dv = d_ref[...].astype(BF16)
        if g_ref is not None:
            r, xhat = _rms(av)
            ab = (xhat * g_ref[...]).astype(BF16)
        else:
            ab = av.astype(BF16)
        dw_ref[...] += _dot(ab, dv, TN)
        da = _dot(dv, w_ref[...], NT)
        if g_ref is not None:
            dg_ref[...] += jnp.sum(da * xhat, axis=0, keepdims=True)
            da = r_ref[...] + _rms_bwd(da, xhat, r, g_ref[...])
        da_ref[...] = da.astype(da_ref.dtype)

    args = [a, d, w]
    in_specs = [pl.BlockSpec((tm, k), lambda i: (i, 0)), pl.BlockSpec((tm, n), lambda i: (i, 0)),
                pl.BlockSpec((k, n), lambda i: (0, 0))]
    out_shape = [jax.ShapeDtypeStruct((n_rows, k), F32), jax.ShapeDtypeStruct((k, n), F32)]
    out_specs = [pl.BlockSpec((tm, k), lambda i: (i, 0)), pl.BlockSpec((k, n), lambda i: (0, 0))]
    if gain is not None:
        args += [gain, dres]
        in_specs += [pl.BlockSpec((1, k), lambda i: (0, 0)), pl.BlockSpec((tm, k), lambda i: (i, 0))]
        out_shape.append(jax.ShapeDtypeStruct((1, k), F32))
        out_specs.append(pl.BlockSpec((1, k), lambda i: (0, 0)))
    return pl.pallas_call(body, grid=(n_rows // tm,), in_specs=in_specs, out_specs=out_specs, out_shape=out_shape,
                          name=name, compiler_params=_params("arbitrary"))(*args)


def _ffn_fwd(name, x, gain, wi, wo, layer):
    n_rows, dm = x.shape
    _, nj, _, _, fb = wi.shape
    tm = _tile(n_rows, 512)

    def body(x_ref, g_ref, wi_ref, wo_ref, y_ref):
        xv = x_ref[...]
        _, xhat = _rms(xv)
        hn = (xhat * g_ref[...]).astype(BF16)
        acc = jnp.zeros((tm, dm), F32)
        for j in range(nj):
            gate = _dot(hn, wi_ref[0, j])
            up = _dot(hn, wi_ref[1, j])
            act = (gate * _sigmoid(gate) * up).astype(BF16)
            acc = acc + _dot(act, wo_ref[j].reshape(fb, dm))
        y_ref[...] = xv + 0.5 * acc

    return pl.pallas_call(
        body, grid=(n_rows // tm,), name=name,
        in_specs=[pl.BlockSpec((tm, dm), lambda i: (i, 0)), pl.BlockSpec((1, dm), lambda i: (0, 0)),
                  pl.BlockSpec((2, nj, None, dm, fb), lambda i: (0, 0, layer, 0, 0)),
                  pl.BlockSpec((nj, 2, None, fb // 2, dm), lambda i: (0, 0, layer, 0, 0))],
        out_specs=pl.BlockSpec((tm, dm), lambda i: (i, 0)),
        out_shape=jax.ShapeDtypeStruct((n_rows, dm), F32), compiler_params=_params("parallel"))(x, gain, wi, wo)


def _ffn_bwd_block(name, x, dy, gain, wi, wo, layer, j, acc):
    n_rows, dm = x.shape
    _, nj, _, _, fb = wi.shape
    tm = _tile(n_rows, 512)
    last = j == nj - 1

    def body(*refs):
        it = iter(refs)
        x_ref, dy_ref, g_ref, wi_ref, wo_ref = next(it), next(it), next(it), next(it), next(it)
        acc_ref = next(it) if acc is not None else None
        out_ref, dwi_ref, dwo_ref = next(it), next(it), next(it)
        dg_ref = next(it) if last else None
        i = pl.program_id(0)

        @pl.when(i == 0)
        def _():
            dwi_ref[...] = jnp.zeros(dwi_ref.shape, F32)
            dwo_ref[...] = jnp.zeros(dwo_ref.shape, F32)
            if last:
                dg_ref[...] = jnp.zeros(dg_ref.shape, F32)

        xv, dyv, g = x_ref[...], dy_ref[...], g_ref[...]
        r, xhat = _rms(xv)
        hn = (xhat * g).astype(BF16)
        wg, wu, wob = wi_ref[0], wi_ref[1], wo_ref[...].reshape(fb, dm)
        gate = _dot(hn, wg)
        up = _dot(hn, wu)
        s = _sigmoid(gate)
        silu = gate * s
        act = (silu * up).astype(BF16)
        dout = (0.5 * dyv).astype(BF16)
        dact = _dot(dout, wob, NT)
        dwo_ref[...] += _dot(act, dout, TN)
        dgate = (dact * up * (s * (1.0 + gate * (1.0 - s)))).astype(BF16)
        dup = (dact * silu).astype(BF16)
        dwi_ref[0] += _dot(hn, dgate, TN)
        dwi_ref[1] += _dot(hn, dup, TN)
        tot = _dot(dgate, wg, NT) + _dot(dup, wu, NT)
        if acc_ref is not None:
            tot = tot + acc_ref[...]
        if last:
            out_ref[...] = dyv + _rms_bwd(tot, xhat, r, g)
            dg_ref[...] += jnp.sum(tot * xhat, axis=0, keepdims=True)
        else:
            out_ref[...] = tot

    tok = pl.BlockSpec((tm, dm), lambda i: (i, 0))
    args = [x, dy, gain, wi, wo]
    in_specs = [tok, tok, pl.BlockSpec((1, dm), lambda i: (0, 0)),
                pl.BlockSpec((2, None, None, dm, fb), lambda i: (0, j, layer, 0, 0)),
                pl.BlockSpec((None, 2, None, fb // 2, dm), lambda i: (j, 0, layer, 0, 0))]
    if acc is not None:
        args.append(acc)
        in_specs.append(tok)
    out_specs = [tok, pl.BlockSpec((2, dm, fb), lambda i: (0, 0, 0)), pl.BlockSpec((fb, dm), lambda i: (0, 0))]
    out_shape = [jax.ShapeDtypeStruct((n_rows, dm), F32), jax.ShapeDtypeStruct((2, dm, fb), F32),
                 jax.ShapeDtypeStruct((fb, dm), F32)]
    if last:
        out_specs.append(pl.BlockSpec((1, dm), lambda i: (0, 0)))
        out_shape.append(jax.ShapeDtypeStruct((1, dm), F32))
    return pl.pallas_call(body, grid=(n_rows // tm,), name=name, in_specs=in_specs, out_specs=out_specs,
                          out_shape=out_shape, compiler_params=_params("arbitrary"))(*args)


def _ffn_bwd(name, x, dy, gain, wi, wo, layer):
    nj = wi.shape[1]
    acc, dwi, dwo = None, [], []
    for j in range(nj):
        res = _ffn_bwd_block("%s_%d" % (name, j), x, dy, gain, wi, wo, layer, j, acc)
        acc = res[0]
        dwi.append(res[1])
        dwo.append(res[2])
    return acc, jnp.stack(dwi, axis=1), jnp.stack(dwo, axis=0), res[3]


def _scan8(a, x, reverse):
    row = lax.broadcasted_iota(jnp.int32, a.shape, 0)
    for k in (1, 2, 4):
        if reverse:
            keep = row < SUBLANES - k
            a_s, x_s = pltpu.roll(a, SUBLANES - k, 0), pltpu.roll(x, SUBLANES - k, 0)
        else:
            keep = row >= k
            a_s, x_s = pltpu.roll(a, k, 0), pltpu.roll(x, k, 0)
        x = a * jnp.where(keep, x_s, 0.0) + x
        a = a * jnp.where(keep, a_s, 1.0)
    return a, x


def _scan_tile(a_ref, x_ref, h_ref, carry, reverse, rows):
    groups = rows // SUBLANES

    def step(n, c):
        gidx = groups - 1 - n if reverse else n
        sl = pl.ds(pl.multiple_of(gidx * SUBLANES, SUBLANES), SUBLANES)
        a_cum, h0 = _scan8(a_ref[sl, :], x_ref[sl, :], reverse)
        h = a_cum * c + h0
        h_ref[sl, :] = h
        return h[0:1] if reverse else h[SUBLANES - 1:SUBLANES]

    return lax.fori_loop(0, groups, step, carry)


def _s5_mats(lam_re, lam_im, log_dt, b_re, b_im, c_re, c_im):
    lc = S5_CHUNK
    groups, p = lam_re.shape
    h = b_re.shape[-1]
    lam = lax.complex(lam_re, lam_im)
    lam_dt = lam * jnp.exp(log_dt)[:, None]
    lam_bar = jnp.exp(lam_dt)
    b_bar = ((lam_bar - 1.0) / lam)[:, :, None] * lax.complex(b_re, b_im)
    c = lax.complex(c_re, c_im)
    pw = jnp.exp(lam_dt[None] * jnp.arange(lc + 1, dtype=F32)[:, None, None])
    resp = jnp.einsum('ghp,tgp,gpk->tghk', c, pw[:lc], b_bar, precision=HI).real
    s_idx = jnp.arange(lc)[:, None]
    u_idx = jnp.arange(lc)[None, :]
    onehot = (jnp.arange(lc)[:, None, None] == (u_idx - s_idx)[None]).astype(F32)
    m = jnp.einsum('tghk,tsu->gskuh', resp, onehot, precision=HI).reshape(groups, lc * h, lc * h)
    w = pw[lc - 1::-1][:lc].transpose(1, 0, 2)[:, :, None, :] * b_bar.transpose(0, 2, 1)[:, None]
    bm = jnp.concatenate([w.real, w.imag], axis=-1).reshape(groups, lc * h, 2 * p)
    v = c[:, None] * pw[1:lc + 1].transpose(1, 0, 2)[:, :, None, :]
    v = v.transpose(0, 3, 1, 2)
    cm = jnp.concatenate([v.real, -v.imag], axis=1).reshape(groups, 2 * p, lc * h)
    a = jnp.concatenate([pw[lc].real, pw[lc].imag], axis=-1)
    return m, bm, cm, a


def _s5_powers(lam_re, lam_im, log_dt):
    lam_dt = lax.complex(lam_re, lam_im) * jnp.exp(log_dt)[:, None]
    pw = jnp.exp(lam_dt[None] * (S5_CHUNK * jnp.arange(1, 9, dtype=F32))[:, None, None])

    def c1(z):
        return jnp.concatenate([z.real, z.real], axis=-1).reshape(z.shape[0], -1)

    def c2(z):
        return jnp.concatenate([-z.imag, z.imag], axis=-1).reshape(z.shape[0], -1)

    p1, p2 = c1(pw), c2(pw)
    apw = jnp.stack([p1[0], p2[0], p1[1], p2[1], p1[3], p2[3], jnp.zeros_like(p1[0]), jnp.zeros_like(p1[0])])
    fwd = jnp.concatenate([p1, p2], axis=0)
    rev = jnp.concatenate([c1(pw[::-1]), c2(pw[::-1])], axis=0)
    return apw, fwd, rev


def _cmul(c1, c2, x, half, conj=False):
    sw = pltpu.roll(x, half, 1)
    return c1 * x - c2 * sw if conj else c1 * x + c2 * sw


def _s5_fwd(name, ut, m, bm, cm, apw, arows):
    nc, _ = ut.shape
    groups, lh, _ = m.shape
    p2 = bm.shape[2]
    gb = min(8, groups)

    def body(u_ref, m_ref, b_ref, c_ref, apw_ref, ar_ref, y_ref, sp_ref, xs_ref):
        for gi in range(gb):
            xs_ref[:, gi * p2:(gi + 1) * p2] = _dot(u_ref[:, gi * lh:(gi + 1) * lh], b_ref[gi], prec=HI)
        row = lax.broadcasted_iota(jnp.int32, (SUBLANES, p2), 0)

        def step(n, carry):
            sl = pl.ds(pl.multiple_of(n * SUBLANES, SUBLANES), SUBLANES)
            new = []
            for gi in range(gb):
                ln = slice(gi * p2, (gi + 1) * p2)
                x = xs_ref[sl, ln]
                for q, k in enumerate((1, 2, 4)):
                    xs = jnp.where(row >= k, pltpu.roll(x, k, 0), 0.0)
                    x = x + _cmul(apw_ref[2 * q:2 * q + 1, ln], apw_ref[2 * q + 1:2 * q + 2, ln], xs, p2 // 2)
                cb = jnp.broadcast_to(carry[gi], (SUBLANES, p2))
                s8 = x + _cmul(ar_ref[0:8, ln], ar_ref[8:16, ln], cb, p2 // 2)
                sp_ref[sl, ln] = jnp.where(row >= 1, pltpu.roll(s8, 1, 0), cb)
                new.append(s8[SUBLANES - 1:SUBLANES])
            return tuple(new)

        lax.fori_loop(0, nc // SUBLANES, step, tuple(jnp.zeros((1, p2), F32) for _ in range(gb)))
        for gi in range(gb):
            y_ref[:, gi * lh:(gi + 1) * lh] = (
                _dot(u_ref[:, gi * lh:(gi + 1) * lh], m_ref[gi], prec=HI)
                + _dot(sp_ref[:, gi * p2:(gi + 1) * p2], c_ref[gi], prec=HI))

    return pl.pallas_call(
        body, grid=(groups // gb,), name=name,
        in_specs=[pl.BlockSpec((nc, gb * lh), lambda g: (0, g)), pl.BlockSpec((gb, lh, lh), lambda g: (g, 0, 0)),
                  pl.BlockSpec((gb, lh, p2), lambda g: (g, 0, 0)), pl.BlockSpec((gb, p2, lh), lambda g: (g, 0, 0)),
                  pl.BlockSpec((8, gb * p2), lambda g: (0, g)), pl.BlockSpec((16, gb * p2), lambda g: (0, g))],
        out_specs=[pl.BlockSpec((nc, gb * lh), lambda g: (0, g)), pl.BlockSpec((nc, gb * p2), lambda g: (0, g))],
        out_shape=[jax.ShapeDtypeStruct((nc, groups * lh), F32), jax.ShapeDtypeStruct((nc, groups * p2), F32)],
        scratch_shapes=[pltpu.VMEM((nc, gb * p2), F32)], compiler_params=_params("parallel"),
    )(ut, m, bm, cm, apw, arows)


def _s5_bwd(name, ut, dyt, sprev, m, bm, cm, apw, arows_rev):
    nc, _ = ut.shape
    groups, lh, _ = m.shape
    p2 = bm.shape[2]
    half = p2 // 2
    gb = min(8, groups)

    def body(u_ref, dy_ref, sp_ref, m_ref, b_ref, c_ref, apw_ref, ar_ref,
             du_ref, dm_ref, db_ref, dc_ref, da_ref, ds_ref, gx_ref):
        for gi in range(gb):
            ds_ref[:, gi * p2:(gi + 1) * p2] = _dot(dy_ref[:, gi * lh:(gi + 1) * lh], c_ref[gi], NT, prec=HI)
        row = lax.broadcasted_iota(jnp.int32, (SUBLANES, p2), 0)
        lane = lax.broadcasted_iota(jnp.int32, (SUBLANES, p2), 1)
        ngroups = nc // SUBLANES

        def step(n, state):
            carry, nxt, dacc = state
            sl = pl.ds(pl.multiple_of((ngroups - 1 - n) * SUBLANES, SUBLANES), SUBLANES)
            new_c, new_n, new_d = [], [], []
            for gi in range(gb):
                ln = slice(gi * p2, (gi + 1) * p2)
                d8 = ds_ref[sl, ln]
                x = jnp.where(row < SUBLANES - 1, pltpu.roll(d8, SUBLANES - 1, 0),
                              jnp.broadcast_to(nxt[gi], (SUBLANES, p2)))
                for q, k in enumerate((1, 2, 4)):
                    xs = jnp.where(row < SUBLANES - k, pltpu.roll(x, SUBLANES - k, 0), 0.0)
                    x = x + _cmul(apw_ref[2 * q:2 * q + 1, ln], apw_ref[2 * q + 1:2 * q + 2, ln], xs, half, conj=True)
                cb = jnp.broadcast_to(carry[gi], (SUBLANES, p2))
                g8 = x + _cmul(ar_ref[0:8, ln], ar_ref[8:16, ln], cb, half, conj=True)
                gx_ref[sl, ln] = g8
                s8 = sp_ref[sl, ln]
                p1 = g8 * s8
                pq = g8 * pltpu.roll(s8, half, 1)
                d_a = jnp.where(lane < half, p1 + pltpu.roll(p1, half, 1), pq - pltpu.roll(pq, half, 1))
                new_c.append(g8[0:1])
                new_n.append(d8[0:1])
                new_d.append(dacc[gi] + jnp.sum(d_a, axis=0, keepdims=True))
            return tuple(new_c), tuple(new_n), tuple(new_d)

        zeros = tuple(jnp.zeros((1, p2), F32) for _ in range(gb))
        _, _, dacc = lax.fori_loop(0, ngroups, step, (zeros, zeros, zeros))
        for gi in range(gb):
            ln = slice(gi * p2, (gi + 1) * p2)
            lu = slice(gi * lh, (gi + 1) * lh)
            da_ref[:, ln] = dacc[gi]
            ug, dyg, gxg = u_ref[:, lu], dy_ref[:, lu], gx_ref[:, ln]
            du_ref[:, lu] = _dot(dyg, m_ref[gi], NT, prec=HI) + _dot(gxg, b_ref[gi], NT, prec=HI)
            dm_ref[gi] = _dot(ug, dyg, TN, prec=HI)
            dc_ref[gi] = _dot(sp_ref[:, ln], dyg, TN, prec=HI)
            db_ref[gi] = _dot(ug, gxg, TN, prec=HI)

    tok_u = pl.BlockSpec((nc, gb * lh), lambda g: (0, g))
    tok_s = pl.BlockSpec((nc, gb * p2), lambda g: (0, g))
    mat_m = pl.BlockSpec((gb, lh, lh), lambda g: (g, 0, 0))
    mat_b = pl.BlockSpec((gb, lh, p2), lambda g: (g, 0, 0))
    mat_c = pl.BlockSpec((gb, p2, lh), lambda g: (g, 0, 0))
    return pl.pallas_call(
        body, grid=(groups // gb,), name=name,
        in_specs=[tok_u, tok_u, tok_s, mat_m, mat_b, mat_c,
                  pl.BlockSpec((8, gb * p2), lambda g: (0, g)), pl.BlockSpec((16, gb * p2), lambda g: (0, g))],
        out_specs=[tok_u, mat_m, mat_b, mat_c, pl.BlockSpec((1, gb * p2), lambda g: (0, g))],
        out_shape=[jax.ShapeDtypeStruct((nc, groups * lh), F32), jax.ShapeDtypeStruct(m.shape, F32),
                   jax.ShapeDtypeStruct(bm.shape, F32), jax.ShapeDtypeStruct(cm.shape, F32),
                   jax.ShapeDtypeStruct((1, groups * p2), F32)],
        scratch_shapes=[pltpu.VMEM((nc, gb * p2), F32), pltpu.VMEM((nc, gb * p2), F32)],
        compiler_params=_params("parallel"),
    )(ut, dyt, sprev, m, bm, cm, apw, arows_rev)


def _to_chunks(u, groups):
    n_rows, width = u.shape
    h = width // groups
    nc = n_rows // S5_CHUNK
    return u.reshape(nc, S5_CHUNK, groups, h).transpose(0, 2, 1, 3).reshape(nc, groups * S5_CHUNK * h)


def _from_chunks(yt, groups):
    nc, cols = yt.shape
    h = cols // (groups * S5_CHUNK)
    return yt.reshape(nc, groups, S5_CHUNK, h).transpose(0, 2, 1, 3).reshape(nc * S5_CHUNK, groups * h)


def _split_dot(v, tri):
    hi = v.astype(BF16)
    lo = (v - hi.astype(F32)).astype(BF16)
    return _dot(hi, tri) + _dot(lo, tri)


def _sb_block(qb, kblk, scale, causal):
    z = _dot(qb, kblk, NT) * scale
    sp = _softplus(z)
    lk = -sp
    if causal is not None:
        lk = jnp.where(causal, lk, 0.0)
    return lk, z - sp


def _sb_fwd(name, q, k, v):
    heads, n_rows, hd = q.shape
    tq = _tile(n_rows // 2, 256)
    scale = hd ** -0.5

    def body(q_ref, k_ref, v_ref, o_ref):
        qi = pl.program_id(1)
        qb = q_ref[0]
        row = lax.broadcasted_iota(jnp.int32, (tq, tq), 0)
        col = lax.broadcasted_iota(jnp.int32, (tq, tq), 1)
        tri = (row > col).astype(BF16)

        def block(kb, carry, acc, causal):
            ks = pl.ds(pl.multiple_of(kb * tq, tq), tq)
            lk, lb = _sb_block(qb, k_ref[0, ks, :], scale, causal)
            a = jnp.exp(lb + _split_dot(lk, tri) + carry)
            if causal is not None:
                a = jnp.where(causal, a, 0.0)
            acc = acc + _dot(a.astype(BF16), v_ref[0, ks, :])
            return carry + jnp.sum(lk, axis=1, keepdims=True), acc

        carry, acc = block(qi, jnp.zeros((tq, 1), F32), jnp.zeros((tq, hd), F32), col < row)
        _, acc = lax.fori_loop(0, qi, lambda n, st: block(qi - 1 - n, st[0], st[1], None), (carry, acc))
        o_ref[0] = acc

    kv = pl.BlockSpec((1, n_rows, hd), lambda h, i: (h, 0, 0))
    qs = pl.BlockSpec((1, tq, hd), lambda h, i: (h, i, 0))
    return pl.pallas_call(body, grid=(heads, n_rows // tq), in_specs=[qs, kv, kv], out_specs=qs, name=name,
                          out_shape=jax.ShapeDtypeStruct((heads, n_rows, hd), F32),
                          compiler_params=_params("parallel", "arbitrary"))(q, k, v)


def _sb_bwd(name, q, k, v, o, do):
    heads, n_rows, hd = q.shape
    tq = _tile(n_rows // 2, 256)
    scale = hd ** -0.5

    def body(q_ref, k_ref, v_ref, o_ref, do_ref, dq_ref, dk_ref, dv_ref):
        qi = pl.program_id(1)

        @pl.when(qi == 0)
        def _():
            dk_ref[...] = jnp.zeros(dk_ref.shape, F32)
            dv_ref[...] = jnp.zeros(dv_ref.shape, F32)

        qb = q_ref[0]
        dob16 = do_ref[0].astype(BF16)
        delta = jnp.sum(dob16.astype(F32) * o_ref[0], axis=1, keepdims=True)
        row = lax.broadcasted_iota(jnp.int32, (tq, tq), 0)
        col = lax.broadcasted_iota(jnp.int32, (tq, tq), 1)
        tri = (row > col).astype(BF16)
        tri_incl = (row >= col).astype(BF16)

        def block(kb, carry, pcarry, dq, causal):
            ks = pl.ds(pl.multiple_of(kb * tq, tq), tq)
            kblk, vblk = k_ref[0, ks, :], v_ref[0, ks, :]
            lk, lb = _sb_block(qb, kblk, scale, causal)
            a = jnp.exp(lb + _split_dot(lk, tri) + carry)
            if causal is not None:
                a = jnp.where(causal, a, 0.0)
            a16 = a.astype(BF16)
            p = _dot(dob16, vblk, NT) * a16.astype(F32)
            beta = jnp.exp(lb)
            dz = p * (1.0 - beta) - beta * (delta - pcarry - _split_dot(p, tri_incl))
            if causal is not None:
                dz = jnp.where(causal, dz, 0.0)
            dz16 = (dz * scale).astype(BF16)
            dk_ref[0, ks, :] += _dot(dz16, qb, TN)
            dv_ref[0, ks, :] += _dot(a16, dob16, TN)
            return (carry + jnp.sum(lk, axis=1, keepdims=True), pcarry + jnp.sum(p, axis=1, keepdims=True),
                    dq + _dot(dz16, kblk))

        zero = jnp.zeros((tq, 1), F32)
        st = block(qi, zero, zero, jnp.zeros((tq, hd), F32), col < row)
        st = lax.fori_loop(0, qi, lambda n, s: block(qi - 1 - n, s[0], s[1], s[2], None), st)
        dq_ref[0] = st[2]

    kv = pl.BlockSpec((1, n_rows, hd), lambda h, i: (h, 0, 0))
    qs = pl.BlockSpec((1, tq, hd), lambda h, i: (h, i, 0))
    full = jax.ShapeDtypeStruct((heads, n_rows, hd), F32)
    return pl.pallas_call(body, grid=(heads, n_rows // tq), in_specs=[qs, kv, kv, qs, qs], out_specs=[qs, kv, kv],
                          out_shape=[full, full, full], name=name,
                          compiler_params=_params("parallel", "arbitrary"))(q, k, v, o, do)


def _block_diag(xb, w_ref_val, dims):
    nb = w_ref_val.shape[0]
    bw = xb.shape[1] // nb
    return jnp.concatenate([_dot(xb[:, n * bw:(n + 1) * bw], w_ref_val[n], dims) for n in range(nb)], axis=1)


def _lru_gates_fwd(name, gx, conv_w, conv_b, wa, ba, wx, bx, lam):
    n_rows, w2 = gx.shape
    w = w2 // 2
    tm = _tile(n_rows, 256)

    def fn(i, nt, br, prev, cw, cb, wa_v, ba_v, wx_v, bx_v, lam_v):
        xc = cb + sum(cw[k:k + 1] * _shift_down(br, prev, LRU_CONV - 1 - k, i == 0) for k in range(LRU_CONV))
        xb = xc.astype(BF16)
        r = _sigmoid(_block_diag(xb, wa_v, NN) + ba_v)
        ig = _sigmoid(_block_diag(xb, wx_v, NN) + bx_v)
        log_a = (-LRU_C * r) * _softplus(-lam_v)
        a = jnp.exp(log_a)
        gated = (ig * xc) * _one_minus_a2_sqrt(log_a)
        return (xc, r, ig, a, gated), ()

    return _rowwise(name, fn, [('t', gx, w, 1), ('p', gx, w, 1), ('b', conv_w), ('b', conv_b), ('b', wa), ('b', ba),
                               ('b', wx), ('b', bx), ('b', lam)], [(w, F32)] * 5, [], n_rows, tm)


def _lru_scan_fwd(name, a, gated, gx):
    n_rows, w = a.shape
    tm = _tile(n_rows, 256)

    def body(a_ref, x_ref, bg_ref, hs_ref, y_ref, carry_ref):
        @pl.when(pl.program_id(0) == 0)
        def _():
            carry_ref[...] = jnp.zeros(carry_ref.shape, F32)
        carry_ref[...] = _scan_tile(a_ref, x_ref, hs_ref, carry_ref[...], False, tm)
        y_ref[...] = (_gelu(bg_ref[...]) * hs_ref[...]).astype(BF16)

    tok = pl.BlockSpec((tm, w), lambda i: (i, 0))
    return pl.pallas_call(body, grid=(n_rows // tm,), in_specs=[tok, tok, tok], out_specs=[tok, tok], name=name,
                          out_shape=[jax.ShapeDtypeStruct((n_rows, w), F32), jax.ShapeDtypeStruct((n_rows, w), BF16)],
                          scratch_shapes=[pltpu.VMEM((1, w), F32)], compiler_params=_params("arbitrary"))(a, gated, gx)


def _lru_scan_bwd(name, a, dy, gx):
    n_rows, w = a.shape
    tm = _tile(n_rows, 256)
    nt = n_rows // tm
    per8 = tm // SUBLANES

    def body(a_ref, an_ref, dy_ref, bg_ref, lam_ref, carry_ref, aup_ref, dhs_ref):
        i = pl.program_id(0)

        @pl.when(i == 0)
        def _():
            carry_ref[...] = jnp.zeros(carry_ref.shape, F32)
        aup_ref[...] = _shift_up(a_ref[...], an_ref[...], 1, i == 0)
        dhs_ref[...] = dy_ref[...] * _gelu(bg_ref[...])
        carry_ref[...] = _scan_tile(aup_ref, dhs_ref, lam_ref, carry_ref[...], True, tm)

    tok = pl.BlockSpec((tm, w), lambda i: (nt - 1 - i, 0))
    nxt = pl.BlockSpec((SUBLANES, w), lambda i: (jnp.minimum((nt - i) * per8, n_rows // SUBLANES - 1), 0))
    return pl.pallas_call(body, grid=(nt,), in_specs=[tok, nxt, tok, tok], out_specs=tok, name=name,
                          out_shape=jax.ShapeDtypeStruct((n_rows, w), F32),
                          scratch_shapes=[pltpu.VMEM((1, w), F32), pltpu.VMEM((tm, w), F32), pltpu.VMEM((tm, w), F32)],
                          compiler_params=_params("arbitrary"))(a, a, dy, gx)


def _lru_gates_bwd(name, lam_t, hs, xc, r, ig, a, wa, wx, lam):
    n_rows, w = xc.shape
    nb, bw, _ = wa.shape
    tm = _tile(n_rows, 256)

    def fn(i, nt, lt, hs_v, hs_prev, xc_v, r_v, ig_v, a_v, wa_v, wx_v, lam_v):
        sp = _softplus(-lam_v)
        log_a = (-LRU_C * r_v) * sp
        mult = _one_minus_a2_sqrt(log_a)
        d_a = lt * _shift_down(hs_v, hs_prev, 1, i == 0)
        d_ig = lt * xc_v * mult
        d_mult = lt * ig_v * xc_v
        d_log_a = d_a * a_v - d_mult * (a_v * a_v) / mult
        d_ra = d_log_a * (-LRU_C * sp) * r_v * (1.0 - r_v)
        d_ia = d_ig * ig_v * (1.0 - ig_v)
        d_ra16, d_ia16, xb = d_ra.astype(BF16), d_ia.astype(BF16), xc_v.astype(BF16)
        dxc = lt * ig_v * mult + _block_diag(d_ra16, wa_v, NT) + _block_diag(d_ia16, wx_v, NT)
        dwa = jnp.concatenate([_dot(xb[:, n * bw:(n + 1) * bw], d_ra16[:, n * bw:(n + 1) * bw], TN)
                               for n in range(nb)], axis=0)
        dwx = jnp.concatenate([_dot(xb[:, n * bw:(n + 1) * bw], d_ia16[:, n * bw:(n + 1) * bw], TN)
                               for n in range(nb)], axis=0)
        col = lambda t: jnp.sum(t, axis=0, keepdims=True)
        return (dxc,), (dwa, dwx, col(d_ra), col(d_ia), col(d_log_a * (-LRU_C * r_v)))

    tiled = lambda arr: ('t', arr, w, 0)
    return _rowwise(name, fn, [tiled(lam_t), tiled(hs), ('p', hs, w, 0), tiled(xc), tiled(r), tiled(ig), tiled(a),
                               ('b', wa), ('b', wx), ('b', lam)],
                    [(w, F32)], [(nb * bw, bw), (nb * bw, bw), (1, w), (1, w), (1, w)], n_rows, tm)


def _lru_conv_bwd(name, dxc, gx, dy, hs, conv_w):
    n_rows, w = dxc.shape
    tm = _tile(n_rows, 256)

    def fn(i, nt, dxc_v, dxc_next, bg, br, br_prev, dy_v, hs_v, cw):
        dbr = sum(cw[k:k + 1] * _shift_up(dxc_v, dxc_next, LRU_CONV - 1 - k, i == nt - 1) for k in range(LRU_CONV))
        dbg = dy_v * hs_v * _gelu_grad(bg)
        dcw = [jnp.sum(dxc_v * _shift_down(br, br_prev, LRU_CONV - 1 - k, i == 0), axis=0, keepdims=True)
               for k in range(LRU_CONV)]
        dcw = jnp.concatenate(dcw + [jnp.zeros((SUBLANES - LRU_CONV, w), F32)], axis=0)
        return (jnp.concatenate([dbg, dbr], axis=1),), (dcw, jnp.sum(dxc_v, axis=0, keepdims=True))

    return _rowwise(name, fn, [('t', dxc, w, 0), ('n', dxc, w, 0), ('t', gx, w, 0), ('t', gx, w, 1), ('p', gx, w, 1),
                               ('t', dy, w, 0), ('t', hs, w, 0), ('b', conv_w)],
                    [(2 * w, BF16)], [(SUBLANES, w), (1, w)], n_rows, tm)


def _loss_head(name, h, gain, target):
    n_rows, dm = h.shape
    tm = _tile(n_rows, 512)

    def fn(i, nt, hv, tv, g):
        r, xhat = _rms(hv)
        err = xhat * g - tv
        dy = err * (1.0 / dm)
        return ((_rms_bwd(dy, xhat, r, g),),
                (jnp.sum(err * err, axis=0, keepdims=True), jnp.sum(dy * xhat, axis=0, keepdims=True)))

    return _rowwise(name, fn, [('t', h, dm, 0), ('t', target, dm, 0), ('b', gain)], [(dm, F32)], [(1, dm), (1, dm)],
                    n_rows, tm)


def _adamw(name, gparts, w, m, v):
    n_parts, n_rows, cols = gparts.shape
    tr = n_rows
    for cand in (256, 128, 64, 32, 16, 8):
        if n_rows % cand == 0:
            tr = cand
            break
    c1 = 1.0 - ADAM_B1 ** ADAM_STEP
    c2 = 1.0 - ADAM_B2 ** ADAM_STEP

    def body(gp_ref, w_ref, m_ref, v_ref, g_ref, d_ref, nm_ref, nv_ref):
        g = gp_ref[0]
        for p in range(1, n_parts):
            g = g + gp_ref[p]
        m_new = ADAM_B1 * m_ref[...] + (1.0 - ADAM_B1) * g
        v_new = ADAM_B2 * v_ref[...] + (1.0 - ADAM_B2) * (g * g)
        m_hat = m_new / c1
        v_hat = v_new / c2
        g_ref[...] = g
        d_ref[...] = -ADAM_LR * (m_hat / (jnp.sqrt(v_hat) + ADAM_EPS) + ADAM_WD * w_ref[...])
        nm_ref[...] = m_new
        nv_ref[...] = v_new

    blk = pl.BlockSpec((tr, cols), lambda i: (i, 0))
    shp = jax.ShapeDtypeStruct((n_rows, cols), F32)
    return pl.pallas_call(body, grid=(n_rows // tr,), name=name,
                          in_specs=[pl.BlockSpec((n_parts, tr, cols), lambda i: (0, i, 0)), blk, blk, blk],
                          out_specs=[blk, blk, blk, blk], out_shape=[shp, shp, shp, shp],
                          compiler_params=_params("parallel"))(gparts, w, m, v)


def _pack_rows(arrays, cols, lead=0):
    flat = [a.reshape(a.shape[:lead] + (-1,)) for a in arrays]
    cat = jnp.concatenate(flat, axis=lead) if len(flat) > 1 else flat[0]
    n = cat.shape[lead]
    pad = (-n) % (cols * SUBLANES)
    if pad:
        cat = jnp.pad(cat, [(0, 0)] * lead + [(0, pad)])
    return cat.reshape(cat.shape[:lead] + (-1, cols))


def _unpack_rows(packed, shapes, lead=0):
    flat = packed.reshape(packed.shape[:lead] + (-1,))
    out, off = [], 0
    for s in shapes:
        n = math.prod(s)
        out.append(lax.slice_in_dim(flat, off, off + n, axis=lead).reshape(flat.shape[:lead] + tuple(s)))
        off += n
    return out


def kernel(x, ffn1_norm, ffn1_w_in, ffn1_w_out, mix_norm, ffn2_norm, ffn2_w_in, ffn2_w_out, final_norm, s5_w_in, s5_lam_re, s5_lam_im, s5_log_dt, s5_b_re, s5_b_im, s5_c_re, s5_c_im, s5_d, s5_w_out, sb_w_qkv, sb_w_out, lru_w_in, lru_conv_w, lru_conv_b, lru_w_a, lru_b_a, lru_w_x, lru_b_x, lru_lambda, lru_w_out, loss_target, m_ffn1_norm, m_ffn1_w_in, m_ffn1_w_out, m_mix_norm, m_ffn2_norm, m_ffn2_w_in, m_ffn2_w_out, m_final_norm, m_s5_w_in, m_s5_lam_re, m_s5_lam_im, m_s5_log_dt, m_s5_b_re, m_s5_b_im, m_s5_c_re, m_s5_c_im, m_s5_d, m_s5_w_out, m_sb_w_qkv, m_sb_w_out, m_lru_w_in, m_lru_conv_w, m_lru_conv_b, m_lru_w_a, m_lru_b_a, m_lru_w_x, m_lru_b_x, m_lru_lambda, m_lru_w_out, v_ffn1_norm, v_ffn1_w_in, v_ffn1_w_out, v_mix_norm, v_ffn2_norm, v_ffn2_w_in, v_ffn2_w_out, v_final_norm, v_s5_w_in, v_s5_lam_re, v_s5_lam_im, v_s5_log_dt, v_s5_b_re, v_s5_b_im, v_s5_c_re, v_s5_c_im, v_s5_d, v_s5_w_out, v_sb_w_qkv, v_sb_w_out, v_lru_w_in, v_lru_conv_w, v_lru_conv_b, v_lru_w_a, v_lru_b_a, v_lru_w_x, v_lru_b_x, v_lru_lambda, v_lru_w_out):
    local = dict(locals())
    W = {n: local[n] for n in WEIGHTS}
    M = {n: local["m_" + n] for n in WEIGHTS}
    V = {n: local["v_" + n] for n in WEIGHTS}

    h0 = x[0]
    target = loss_target[0]
    n_rows, dm = h0.shape
    depth = ffn1_norm.shape[0]

    def ffn_weights(tag):
        wi = _all_gather("ag_%s_w_in" % tag, W[tag + "_w_in"].astype(BF16))
        wo = _all_gather("ag_%s_w_out" % tag, W[tag + "_w_out"].astype(BF16))
        return (wi.reshape((2, N_DEV // 2) + wi.shape[1:]), wo.reshape((N_DEV // 2, 2) + wo.shape[1:]))

    ffn_w = {"ffn1": ffn_weights("ffn1"), "ffn2": ffn_weights("ffn2")}
    mix_shapes = [W[n].shape for n in MIXER_BIG]
    mix_g = _all_gather("ag_mixers", _pack_rows([W[n].astype(BF16) for n in MIXER_BIG], dm))
    full = {n: _unshard(a, SHARD_AXIS[n]) for n, a in zip(MIXER_BIG, _unpack_rows(mix_g, mix_shapes, lead=1))}
    small_shapes = [W[n].shape for n in SMALL_SHARDED]
    small_g = _all_gather("ag_small", _pack_rows([W[n] for n in SMALL_SHARDED], 128))
    full.update({n: _unshard(a, SHARD_AXIS[n])
                 for n, a in zip(SMALL_SHARDED, _unpack_rows(small_g, small_shapes, lead=1))})

    n_s5 = s5_w_in.shape[0]
    s5_groups = s5_lam_re.shape[1]
    heads = dm // SB_HEAD_DIM

    grads = {}
    saved = []
    h = h0

    for layer in range(depth):
        kind, j = layer % 3, layer // 3
        rec = {"h0": h}
        h = _ffn_fwd("ffn1_fwd_%d" % layer, h, ffn1_norm[layer:layer + 1], *ffn_w["ffn1"], layer)
        rec["h1"] = h
        gain = mix_norm[layer:layer + 1]
        if kind == 0:
            (u,) = _mm_fwd("s5_in_%d" % layer, h, full["s5_w_in"][j], F32, gain=gain)
            pars = (s5_lam_re[j], s5_lam_im[j], s5_log_dt[j], s5_b_re[j], s5_b_im[j], s5_c_re[j], s5_c_im[j])
            mats, mats_vjp = jax.vjp(_s5_mats, *pars)
            apw, ar_fwd, ar_rev = _s5_powers(*pars[:3])
            ut = _to_chunks(u, s5_groups)
            yt, sprev = _s5_fwd("s5_core_%d" % layer, ut, *mats[:3], apw, ar_fwd)
            ys = _from_chunks(yt, s5_groups)
            d_skip = full["s5_d"][j:j + 1]
            (z,) = _rowwise("s5_gelu_%d" % layer, lambda i, nt, ys_v, u_v, d_v: ((_gelu(ys_v + d_v * u_v),), ()),
                            [('t', ys, dm, 0), ('t', u, dm, 0), ('b', d_skip)], [(dm, BF16)], [], n_rows,
                            _tile(n_rows, 512))
            h, vg = _mm_fwd("s5_out_%d" % layer, z, full["s5_w_out"][j], F32, resid=h, glu=True)
            rec.update(u=u, ut=ut, ys=ys, sprev=sprev, z=z, vg=vg, mats=mats, mats_vjp=mats_vjp, apw=apw,
                       ar_rev=ar_rev, d_skip=d_skip)
        elif kind == 1:
            (qkv,) = _mm_fwd("sb_in_%d" % layer, h, full["sb_w_qkv"][j], BF16, gain=gain)
            qkv_h = qkv.reshape(n_rows, 3, heads, SB_HEAD_DIM).transpose(1, 2, 0, 3)
            o = _sb_fwd("sb_attn_%d" % layer, qkv_h[0], qkv_h[1], qkv_h[2])
            o_flat = o.transpose(1, 0, 2).reshape(n_rows, dm).astype(BF16)
            (h,) = _mm_fwd("sb_out_%d" % layer, o_flat, full["sb_w_out"][j], F32, resid=h)
            rec.update(qkv_h=qkv_h, o=o, o_flat=o_flat)
        else:
            (gx,) = _mm_fwd("lru_in_%d" % layer, h, full["lru_w_in"][j], F32, gain=gain)
            wa, wx = full["lru_w_a"][j], full["lru_w_x"][j]
            ba, bx = full["lru_b_a"][j].reshape(1, dm), full["lru_b_x"][j].reshape(1, dm)
            lam_row = full["lru_lambda"][j:j + 1]
            xc, r, ig, a, gated = _lru_gates_fwd("lru_gates_%d" % layer, gx, full["lru_conv_w"][j],
                                                 full["lru_conv_b"][j:j + 1], wa, ba, wx, bx, lam_row)
            hs, y = _lru_scan_fwd("lru_scan_%d" % layer, a, gated, gx)
            (h,) = _mm_fwd("lru_out_%d" % layer, y, full["lru_w_out"][j], F32, resid=h)
            rec.update(gx=gx, xc=xc, r=r, ig=ig, a=a, hs=hs, y=y, wa=wa, wx=wx, lam_row=lam_row)
        rec["h2"] = h
        h = _ffn_fwd("ffn2_fwd_%d" % layer, h, ffn2_norm[layer:layer + 1], *ffn_w["ffn2"], layer)
        saved.append(rec)

    dh, err2, dgf = _loss_head("loss_head", h, final_norm.reshape(1, dm), target)
    loss = lax.psum(0.5 / dm * jnp.sum(err2), ("x", "y", "c"))
    grads["final_norm"] = dgf.reshape(final_norm.shape)

    per_layer = {n: [None] * depth for n in ("ffn1_norm", "mix_norm", "ffn2_norm", "ffn1_w_in", "ffn1_w_out",
                                             "ffn2_w_in", "ffn2_w_out")}
    mixer_grads = {}

    def put(name, j, value, count):
        mixer_grads.setdefault(name, [None] * count)[j] = value

    for layer in reversed(range(depth)):
        kind, j = layer % 3, layer // 3
        rec = saved[layer]
        dh, dwi, dwo, dg = _ffn_bwd("ffn2_bwd_%d" % layer, rec["h2"], dh, ffn2_norm[layer:layer + 1],
                                    *ffn_w["ffn2"], layer)
        per_layer["ffn2_w_in"][layer], per_layer["ffn2_w_out"][layer], per_layer["ffn2_norm"][layer] = dwi, dwo, dg
        gain = mix_norm[layer:layer + 1]
        if kind == 0:
            dvg, = _rowwise("s5_glu_bwd_%d" % layer,
                            lambda i, nt, d_v, vg_v: ((jnp.concatenate(
                                [d_v * _sigmoid(vg_v[:, dm:]),
                                 d_v * vg_v[:, :dm] * _sigmoid(vg_v[:, dm:]) * (1.0 - _sigmoid(vg_v[:, dm:]))],
                                axis=1),), ()),
                            [('t', dh, dm, 0), ('t', rec["vg"], 2 * dm, 0)], [(2 * dm, BF16)], [], n_rows,
                            _tile(n_rows, 256))
            dz, dw_out = _mm_bwd("s5_out_bwd_%d" % layer, rec["z"], dvg, full["s5_w_out"][j])

            def gelu_bwd(i, nt, dz_v, ys_v, u_v, d_v):
                dy_v = dz_v * _gelu_grad(ys_v + d_v * u_v)
                return (dy_v,), (jnp.sum(dy_v * u_v, axis=0, keepdims=True),)

            dys, dd = _rowwise("s5_gelu_bwd_%d" % layer, gelu_bwd,
                               [('t', dz, dm, 0), ('t', rec["ys"], dm, 0), ('t', rec["u"], dm, 0),
                                ('b', rec["d_skip"])], [(dm, F32)], [(1, dm)], n_rows, _tile(n_rows, 512))
            m_, bm_, cm_, _ = rec["mats"]
            dut, dm_m, dm_b, dm_c, d_a = _s5_bwd("s5_core_bwd_%d" % layer, rec["ut"], _to_chunks(dys, s5_groups),
                                                rec["sprev"], m_, bm_, cm_, rec["apw"], rec["ar_rev"])
            dpars = rec["mats_vjp"]((dm_m, dm_b, dm_c, d_a.reshape(s5_groups, -1)))
            for nme, val in zip(("s5_lam_re", "s5_lam_im", "s5_log_dt", "s5_b_re", "s5_b_im", "s5_c_re", "s5_c_im"),
                                dpars):
                put(nme, j, val, n_s5)
            (du,) = _rowwise("s5_du_%d" % layer, lambda i, nt, a_v, dy_v, d_v: ((a_v + dy_v * d_v,), ()),
                             [('t', _from_chunks(dut, s5_groups), dm, 0), ('t', dys, dm, 0), ('b', rec["d_skip"])],
                             [(dm, BF16)], [], n_rows, _tile(n_rows, 512))
            dh, dw_in, dgm = _mm_bwd("s5_in_bwd_%d" % layer, rec["h1"], du, full["s5_w_in"][j], gain=gain, dres=dh)
            put("s5_d", j, dd[0], n_s5)
            put("s5_w_out", j, dw_out, n_s5)
            put("s5_w_in", j, dw_in, n_s5)
        elif kind == 1:
            do_flat, dw_out = _mm_bwd("sb_out_bwd_%d" % layer, rec["o_flat"], dh, full["sb_w_out"][j])
            do = do_flat.reshape(n_rows, heads, SB_HEAD_DIM).transpose(1, 0, 2)
            qkv_h = rec["qkv_h"]
            dq, dk, dv = _sb_bwd("sb_attn_bwd_%d" % layer, qkv_h[0], qkv_h[1], qkv_h[2], rec["o"], do)
            dqkv = jnp.stack([dq, dk, dv]).transpose(2, 0, 1, 3).reshape(n_rows, 3 * dm).astype(BF16)
            dh, dw_in, dgm = _mm_bwd("sb_in_bwd_%d" % layer, rec["h1"], dqkv, full["sb_w_qkv"][j], gain=gain, dres=dh)
            put("sb_w_out", j, dw_out, 1)
            put("sb_w_qkv", j, dw_in, 1)
        else:
            dy, dw_out = _mm_bwd("lru_out_bwd_%d" % layer, rec["y"], dh, full["lru_w_out"][j])
            lam_t = _lru_scan_bwd("lru_scan_bwd_%d" % layer, rec["a"], dy, rec["gx"])
            dxc, dwa, dwx, dba, dbx, dsp = _lru_gates_bwd("lru_gates_bwd_%d" % layer, lam_t, rec["hs"], rec["xc"],
                                                          rec["r"], rec["ig"], rec["a"], rec["wa"], rec["wx"],
                                                          rec["lam_row"])
            dgx, dcw, dcb = _lru_conv_bwd("lru_conv_bwd_%d" % layer, dxc, rec["gx"], dy, rec["hs"],
                                          full["lru_conv_w"][j])
            dh, dw_in, dgm = _mm_bwd("lru_in_bwd_%d" % layer, rec["h1"], dgx, full["lru_w_in"][j], gain=gain, dres=dh)
            nb = rec["wa"].shape[0]
            put("lru_w_out", j, dw_out, 1)
            put("lru_w_in", j, dw_in, 1)
            put("lru_w_a", j, dwa.reshape(rec["wa"].shape), 1)
            put("lru_w_x", j, dwx.reshape(rec["wx"].shape), 1)
            put("lru_b_a", j, dba.reshape(nb, -1), 1)
            put("lru_b_x", j, dbx.reshape(nb, -1), 1)
            put("lru_conv_w", j, dcw[:LRU_CONV], 1)
            put("lru_conv_b", j, dcb[0], 1)
            put("lru_lambda", j, (dsp * -_sigmoid(-rec["lam_row"]))[0], 1)
        per_layer["mix_norm"][layer] = dgm
        dh, dwi, dwo, dg = _ffn_bwd("ffn1_bwd_%d" % layer, rec["h0"], dh, ffn1_norm[layer:layer + 1],
                                    *ffn_w["ffn1"], layer)
        per_layer["ffn1_w_in"][layer], per_layer["ffn1_w_out"][layer], per_layer["ffn1_norm"][layer] = dwi, dwo, dg

    grad_x = dh[None]
    for n in ("ffn1_norm", "mix_norm", "ffn2_norm"):
        grads[n] = jnp.concatenate(per_layer[n], axis=0)
    for n, parts in mixer_grads.items():
        grads[n] = jnp.stack(parts)

    out_g, out_d, out_m, out_v = {}, {}, {}, {}

    def finish(names, res, shapes):
        for n, g_, d_, m_, v_ in zip(names, *[_unpack_rows(t, shapes) for t in res]):
            out_g[n], out_d[n], out_m[n], out_v[n] = g_, d_, m_, v_

    for tag in ("ffn1", "ffn2"):
        n_in, n_out = tag + "_w_in", tag + "_w_out"
        dwi = jnp.stack(per_layer[n_in], axis=2)
        send = dwi.reshape((N_DEV, -1, dm))
        recv = _exchange("xchg_" + n_in, send)
        finish([n_in], _adamw("adamw_" + n_in, recv, *[t[n_in].reshape(-1, dm) for t in (W, M, V)]), [W[n_in].shape])
        dwo = jnp.stack([t.reshape(N_DEV // 2, 2, -1, dm) for t in per_layer[n_out]], axis=2)
        recv = _exchange("xchg_" + n_out, dwo.reshape((N_DEV, -1, dm)))
        finish([n_out], _adamw("adamw_" + n_out, recv, *[t[n_out].reshape(-1, dm) for t in (W, M, V)]),
               [W[n_out].shape])

    send = _pack_rows([_shard_blocks(grads[n], SHARD_AXIS[n]) for n in MIXER_BIG], dm, lead=1)
    recv = _exchange("xchg_mixers", send)
    finish(MIXER_BIG, _adamw("adamw_mixers", recv, *[_pack_rows([t[n] for n in MIXER_BIG], dm) for t in (W, M, V)]),
           mix_shapes)

    small_names = REPLICATED + SMALL_SHARDED
    small_full_shapes = [grads[n].shape for n in small_names]
    parts = _all_gather("ag_small_grads", _pack_rows([grads[n] for n in small_names], 128))
    zero = jnp.zeros(parts.shape[1:], F32)
    summed = _adamw("sum_small_grads", parts, zero, zero, zero)[0]
    small_sum = dict(zip(small_names, _unpack_rows(summed, small_full_shapes)))
    me = 4 * lax.axis_index("x") + 2 * lax.axis_index("y") + lax.axis_index("c")
    rep_shapes = [W[n].shape for n in REPLICATED]
    g_rep = _pack_rows([small_sum[n] for n in REPLICATED], 128)[None]
    finish(REPLICATED, _adamw("adamw_replicated", g_rep, *[_pack_rows([t[n] for n in REPLICATED], 128)
                                                           for t in (W, M, V)]), rep_shapes)
    g_loc = []
    for n in SMALL_SHARDED:
        ax = SHARD_AXIS[n]
        size = W[n].shape[ax]
        g_loc.append(lax.dynamic_slice_in_dim(small_sum[n], me * size, size, axis=ax))
    finish(SMALL_SHARDED, _adamw("adamw_small", _pack_rows(g_loc, 128)[None],
                                 *[_pack_rows([t[n] for n in SMALL_SHARDED], 128) for t in (W, M, V)]), small_shapes)

    return (loss, grad_x, *[out_g[n] for n in WEIGHTS], *[out_d[n] for n in WEIGHTS],
            *[out_m[n] for n in WEIGHTS], *[out_v[n] for n in WEIGHTS])
```

```python
import functools
import math

import jax
import jax.numpy as jnp
from jax import lax
from jax.experimental import pallas as pl
from jax.experimental.pallas import tpu as pltpu

F32 = jnp.float32
BF16 = jnp.bfloat16
HI = lax.Precision.HIGHEST
MESH = pl.DeviceIdType.MESH

N_DEV = 8
RMS_EPS = 1e-6
S5_GROUP = 16
S5_CHUNK = 16
S5_OCTET = 128 // S5_GROUP
S5_REGROUP_ROWS = 32
SB_HEAD_DIM = 64
SB_UNDERFLOW = -104.0
LRU_CONV = 4
LRU_C = 8.0
ADAM_LR, ADAM_B1, ADAM_B2, ADAM_EPS, ADAM_WD, ADAM_STEP = 0.001, 0.9, 0.999, 1e-08, 0.01, 10
VMEM_LIMIT_BYTES = 56 * 1024 * 1024
SUBLANES = 8
PACK_ROWS = 256

NN = (((1,), (0,)), ((), ()))
NT = (((1,), (1,)), ((), ()))
TN = (((0,), (0,)), ((), ()))

SHARD_AXIS = dict(
    ffn1_w_in=2, ffn1_w_out=1, ffn2_w_in=2, ffn2_w_out=1, s5_w_in=1, s5_d=1, s5_w_out=2, sb_w_qkv=2, sb_w_out=1,
    lru_w_in=2, lru_conv_w=2, lru_conv_b=1, lru_w_a=2, lru_b_a=2, lru_w_x=2, lru_b_x=2, lru_lambda=1, lru_w_out=1)
MIXER_BIG = ("s5_w_in", "s5_w_out", "sb_w_qkv", "sb_w_out", "lru_w_in", "lru_w_a", "lru_w_x", "lru_w_out")
SMALL_SHARDED = ("s5_d", "lru_conv_w", "lru_conv_b", "lru_b_a", "lru_b_x", "lru_lambda")
REPLICATED = ("ffn1_norm", "mix_norm", "ffn2_norm", "final_norm", "s5_lam_re", "s5_lam_im", "s5_log_dt",
              "s5_b_re", "s5_b_im", "s5_c_re", "s5_c_im")
WEIGHTS = ("ffn1_norm", "ffn1_w_in", "ffn1_w_out", "mix_norm", "ffn2_norm", "ffn2_w_in", "ffn2_w_out", "final_norm",
           "s5_w_in", "s5_lam_re", "s5_lam_im", "s5_log_dt", "s5_b_re", "s5_b_im", "s5_c_re", "s5_c_im", "s5_d",
           "s5_w_out", "sb_w_qkv", "sb_w_out", "lru_w_in", "lru_conv_w", "lru_conv_b", "lru_w_a", "lru_b_a",
           "lru_w_x", "lru_b_x", "lru_lambda", "lru_w_out")


def _dot(a, b, dims=NN, prec=None):
    return lax.dot_general(a, b, dims, precision=prec, preferred_element_type=F32)


def _params(*sem):
    return pltpu.CompilerParams(dimension_semantics=sem, vmem_limit_bytes=VMEM_LIMIT_BYTES)


def _tile(n, pref):
    return min(pref, n)


def _sigmoid(x):
    return jax.nn.sigmoid(x)


def _softplus(x):
    return jnp.maximum(x, 0.0) + jnp.log(1.0 + jnp.exp(-jnp.abs(x)))


_GELU_C = math.sqrt(2.0 / math.pi)


def _gelu(x):
    return 0.5 * x * (1.0 + jnp.tanh(_GELU_C * (x + 0.044715 * x * x * x)))


def _gelu_grad(x):
    t = jnp.tanh(_GELU_C * (x + 0.044715 * x * x * x))
    return 0.5 * (1.0 + t) + 0.5 * x * (1.0 - t * t) * _GELU_C * (1.0 + 3.0 * 0.044715 * x * x)


def _rms(x):
    r = lax.rsqrt(jnp.mean(x * x, axis=1, keepdims=True) + RMS_EPS)
    return r, x * r


def _rms_bwd(dhn, xhat, r, g):
    dxhat = dhn * g
    return r * (dxhat - xhat * jnp.mean(dxhat * xhat, axis=1, keepdims=True))


def _one_minus_a2_sqrt(log_a):
    t = jnp.tanh(log_a)
    return jnp.sqrt(-2.0 * t / (1.0 - t))


def _shift_down(cur, prev8, k, first):
    if k == 0:
        return cur
    row8 = lax.broadcasted_iota(jnp.int32, prev8.shape, 0)
    rolled = pltpu.roll(cur, k, 0)
    edge = jnp.where(first, 0.0, pltpu.roll(prev8, k, 0))
    top = jnp.where(row8 < k, edge, rolled[0:SUBLANES])
    return jnp.concatenate([top, rolled[SUBLANES:]], axis=0)


def _shift_up(cur, next8, k, last):
    if k == 0:
        return cur
    tm = cur.shape[0]
    row8 = lax.broadcasted_iota(jnp.int32, next8.shape, 0)
    rolled = pltpu.roll(cur, tm - k, 0)
    edge = jnp.where(last, 0.0, pltpu.roll(next8, SUBLANES - k, 0))
    bottom = jnp.where(row8 >= SUBLANES - k, edge, rolled[tm - SUBLANES:tm])
    return jnp.concatenate([rolled[:tm - SUBLANES], bottom], axis=0)


def _rowwise(name, fn, ins, out_tiled, out_acc, n_rows, tm, reverse=False):
    nt = n_rows // tm
    per8 = tm // SUBLANES
    n8 = n_rows // SUBLANES
    n_in, n_ot = len(ins), len(out_tiled)

    def pos(i):
        return nt - 1 - i if reverse else i

    in_specs, args = [], []
    for spec in ins:
        kind, arr = spec[0], spec[1]
        args.append(arr)
        if kind == 'b':
            in_specs.append(pl.BlockSpec(arr.shape, lambda i, nd=arr.ndim: (0,) * nd))
        elif kind == 't':
            in_specs.append(pl.BlockSpec((tm, spec[2]), lambda i, cb=spec[3]: (pos(i), cb)))
        elif kind == 'p':
            in_specs.append(pl.BlockSpec((SUBLANES, spec[2]),
                                         lambda i, cb=spec[3]: (jnp.maximum(pos(i) * per8 - 1, 0), cb)))
        else:
            in_specs.append(pl.BlockSpec((SUBLANES, spec[2]),
                                         lambda i, cb=spec[3]: (jnp.minimum((pos(i) + 1) * per8, n8 - 1), cb)))

    def body(*refs):
        i = pl.program_id(0)
        vals = [r[...] for r in refs[:n_in]]
        outs = refs[n_in:]
        touts, aouts = fn(pos(i), nt, *vals)
        for r, v in zip(outs[:n_ot], touts):
            r[...] = v.astype(r.dtype)
        if out_acc:
            @pl.when(i == 0)
            def _():
                for r in outs[n_ot:]:
                    r[...] = jnp.zeros(r.shape, r.dtype)
            for r, v in zip(outs[n_ot:], aouts):
                r[...] += v

    out_specs = [pl.BlockSpec((tm, n), lambda i: (pos(i), 0)) for n, _ in out_tiled]
    out_specs += [pl.BlockSpec((r, n), lambda i: (0, 0)) for r, n in out_acc]
    out_shape = [jax.ShapeDtypeStruct((n_rows, n), dt) for n, dt in out_tiled]
    out_shape += [jax.ShapeDtypeStruct((r, n), F32) for r, n in out_acc]
    return pl.pallas_call(body, grid=(nt,), in_specs=in_specs, out_specs=out_specs, out_shape=out_shape, name=name,
                          compiler_params=_params("arbitrary"))(*args)


def _all_gather(name, block):
    def body(x_ref, out_ref, send_sems, recv_sems, local_sem):
        x, y, c = lax.axis_index("x"), lax.axis_index("y"), lax.axis_index("c")
        me, sibling = (x, y, c), (x, y, 1 - c)
        chips = [(1 - x, y), (x, 1 - y), (1 - x, 1 - y)]

        def rows(px, py, pc):
            return out_ref.at[4 * px + 2 * py + pc]

        def copy(k, blk, to, src=None):
            return pltpu.make_async_remote_copy(
                src_ref=rows(*blk) if src is None else src, dst_ref=rows(*blk),
                send_sem=send_sems.at[k], recv_sem=recv_sems.at[k], device_id=to, device_id_type=MESH)

        mine = pltpu.make_async_copy(x_ref, rows(*me), local_sem)
        mine.start()
        first = [copy(0, me, sibling, src=x_ref)]
        first += [copy(1 + j, me, (*chip, c), src=x_ref) for j, chip in enumerate(chips)]
        for cp in first:
            cp.start()
        passed = [copy(4 + j, (*chip, c), sibling) for j, chip in enumerate(chips)]
        for j, chip in enumerate(chips):
            copy(1 + j, (*chip, c), me).wait_recv()
            passed[j].start()
        copy(0, sibling, me).wait_recv()
        for j, chip in enumerate(chips):
            copy(4 + j, (*chip, 1 - c), me).wait_recv()
        for cp in first + passed:
            cp.wait_send()
        mine.wait()

    return pl.pallas_call(
        body, name=name, out_shape=jax.ShapeDtypeStruct((N_DEV,) + block.shape, block.dtype),
        in_specs=[pl.BlockSpec(memory_space=pl.ANY)], out_specs=pl.BlockSpec(memory_space=pl.ANY),
        scratch_shapes=[pltpu.SemaphoreType.DMA((7,)), pltpu.SemaphoreType.DMA((7,)), pltpu.SemaphoreType.DMA(())],
    )(block)


def _exchange(name, send):
    def body(s_ref, r_ref, send_sems, recv_sems, local_sem):
        x, y, c = lax.axis_index("x"), lax.axis_index("y"), lax.axis_index("c")
        me = 4 * x + 2 * y + c
        mine = pltpu.make_async_copy(s_ref.at[me], r_ref.at[me], local_sem)
        mine.start()
        copies = []
        for k in range(1, N_DEV):
            dx, dy, dc = (k >> 2) & 1, (k >> 1) & 1, k & 1
            px = 1 - x if dx else x
            py = 1 - y if dy else y
            pc = 1 - c if dc else c
            peer = 4 * px + 2 * py + pc
            copies.append((pltpu.make_async_remote_copy(
                src_ref=s_ref.at[peer], dst_ref=r_ref.at[me], send_sem=send_sems.at[k - 1],
                recv_sem=recv_sems.at[k - 1], device_id=(px, py, pc), device_id_type=MESH), peer))
        for cp, _ in copies:
            cp.start()
        for k, (cp, peer) in enumerate(copies):
            pltpu.make_async_remote_copy(
                src_ref=s_ref.at[peer], dst_ref=r_ref.at[peer], send_sem=send_sems.at[k], recv_sem=recv_sems.at[k],
                device_id=(x, y, c), device_id_type=MESH).wait_recv()
        for cp, _ in copies:
            cp.wait_send()
        mine.wait()

    return pl.pallas_call(
        body, name=name, out_shape=jax.ShapeDtypeStruct(send.shape, send.dtype),
        in_specs=[pl.BlockSpec(memory_space=pl.ANY)], out_specs=pl.BlockSpec(memory_space=pl.ANY),
        scratch_shapes=[pltpu.SemaphoreType.DMA((7,)), pltpu.SemaphoreType.DMA((7,)), pltpu.SemaphoreType.DMA(())],
    )(send)


def _unshard(gathered, axis):
    local = gathered.shape[1:]
    full = jnp.moveaxis(gathered, 0, axis)
    return full.reshape(local[:axis] + (N_DEV * local[axis],) + local[axis + 1:])


def _shard_blocks(full, axis):
    s = full.shape
    cut = full.reshape(s[:axis] + (N_DEV, s[axis] // N_DEV) + s[axis + 1:])
    return jnp.moveaxis(cut, axis, 0)


def _mm_fwd(name, a, w, out_dtype, gain=None, resid=None, glu=False):
    n_rows, k = a.shape
    n = w.shape[1]
    tm = _tile(n_rows, 512)
    n_out = n // 2 if glu else n

    def body(*refs):
        it = iter(refs)
        a_ref, w_ref = next(it), next(it)
        g_ref = next(it) if gain is not None else None
        r_ref = next(it) if resid is not None else None
        outs = list(it)
        av = a_ref[...]
        if g_ref is not None:
            _, xhat = _rms(av)
            av = xhat * g_ref[...]
        res = _dot(av.astype(BF16), w_ref[...])
        if glu:
            outs[1][...] = res.astype(outs[1].dtype)
            res = res[:, :n_out] * _sigmoid(res[:, n_out:])
        if r_ref is not None:
            res = res + r_ref[...]
        outs[0][...] = res.astype(outs[0].dtype)

    args = [a, w]
    in_specs = [pl.BlockSpec((tm, k), lambda i: (i, 0)), pl.BlockSpec((k, n), lambda i: (0, 0))]
    if gain is not None:
        args.append(gain)
        in_specs.append(pl.BlockSpec((1, k), lambda i: (0, 0)))
    if resid is not None:
        args.append(resid)
        in_specs.append(pl.BlockSpec((tm, n_out), lambda i: (i, 0)))
    out_shape = [jax.ShapeDtypeStruct((n_rows, n_out), out_dtype)]
    out_specs = [pl.BlockSpec((tm, n_out), lambda i: (i, 0))]
    if glu:
        out_shape.append(jax.ShapeDtypeStruct((n_rows, n), F32))
        out_specs.append(pl.BlockSpec((tm, n), lambda i: (i, 0)))
    return pl.pallas_call(body, grid=(n_rows // tm,), in_specs=in_specs, out_specs=out_specs, out_shape=out_shape,
                          name=name, compiler_params=_params("parallel"))(*args)


def _mm_bwd(name, a, d, w, gain=None, dres=None):
    n_rows, k = a.shape
    n = w.shape[1]
    tm = _tile(n_rows, 512)

    def body(*refs):
        it = iter(refs)
        a_ref, d_ref, w_ref = next(it), next(it), next(it)
        g_ref = next(it) if gain is not None else None
        r_ref = next(it) if gain is not None else None
        da_ref, dw_ref = next(it), next(it)
        dg_ref = next(it) if gain is not None else None
        i = pl.program_id(0)

        @pl.when(i == 0)
        def _():
            dw_ref[...] = jnp.zeros(dw_ref.shape, F32)
            if dg_ref is not None:
                dg_ref[...] = jnp.zeros(dg_ref.shape, F32)

        av = a_ref[...]
        dv = d_ref[...].astype(BF16)
        if g_ref is not None:
            r, xhat = _rms(av)
            ab = (xhat * g_ref[...]).astype(BF16)
        else:
            ab = av.astype(BF16)
        dw_ref[...] += _dot(ab, dv, TN)
        da = _dot(dv, w_ref[...], NT)
        if g_ref is not None:
            dg_ref[...] += jnp.sum(da * xhat, axis=0, keepdims=True)
            da = r_ref[...] + _rms_bwd(da, xhat, r, g_ref[...])
        da_ref[...] = da.astype(da_ref.dtype)

    args = [a, d, w]
    in_specs = [pl.BlockSpec((tm, k), lambda i: (i, 0)), pl.BlockSpec((tm, n), lambda i: (i, 0)),
                pl.BlockSpec((k, n), lambda i: (0, 0))]
    out_shape = [jax.ShapeDtypeStruct((n_rows, k), F32), jax.ShapeDtypeStruct((k, n), F32)]
    out_specs = [pl.BlockSpec((tm, k), lambda i: (i, 0)), pl.BlockSpec((k, n), lambda i: (0, 0))]
    if gain is not None:
        args += [gain, dres]
        in_specs += [pl.BlockSpec((1, k), lambda i: (0, 0)), pl.BlockSpec((tm, k), lambda i: (i, 0))]
        out_shape.append(jax.ShapeDtypeStruct((1, k), F32))
        out_specs.append(pl.BlockSpec((1, k), lambda i: (0, 0)))
    return pl.pallas_call(body, grid=(n_rows // tm,), in_specs=in_specs, out_specs=out_specs, out_shape=out_shape,
                          name=name, compiler_params=_params("arbitrary"))(*args)


def _ffn_fwd(name, x, gain, wi, wo, layer):
    n_rows, dm = x.shape
    _, nj, _, _, fb = wi.shape
    tm = _tile(n_rows, 512)

    def body(x_ref, g_ref, wi_ref, wo_ref, y_ref):
        xv = x_ref[...]
        _, xhat = _rms(xv)
        hn = (xhat * g_ref[...]).astype(BF16)
        acc = jnp.zeros((tm, dm), F32)
        for j in range(nj):
            gate = _dot(hn, wi_ref[0, j])
            up = _dot(hn, wi_ref[1, j])
            act = (gate * _sigmoid(gate) * up).astype(BF16)
            acc = acc + _dot(act, wo_ref[j].reshape(fb, dm))
        y_ref[...] = xv + 0.5 * acc

    return pl.pallas_call(
        body, grid=(n_rows // tm,), name=name,
        in_specs=[pl.BlockSpec((tm, dm), lambda i: (i, 0)), pl.BlockSpec((1, dm), lambda i: (0, 0)),
                  pl.BlockSpec((2, nj, None, dm, fb), lambda i: (0, 0, layer, 0, 0)),
                  pl.BlockSpec((nj, 2, None, fb // 2, dm), lambda i: (0, 0, layer, 0, 0))],
        out_specs=pl.BlockSpec((tm, dm), lambda i: (i, 0)),
        out_shape=jax.ShapeDtypeStruct((n_rows, dm), F32), compiler_params=_params("parallel"))(x, gain, wi, wo)


def _ffn_bwd_block(name, x, dy, gain, wi, wo, layer, j, acc):
    n_rows, dm = x.shape
    _, nj, _, _, fb = wi.shape
    tm = _tile(n_rows, 512)
    last = j == nj - 1

    def body(*refs):
        it = iter(refs)
        x_ref, dy_ref, g_ref, wi_ref, wo_ref = next(it), next(it), next(it), next(it), next(it)
        acc_ref = next(it) if acc is not None else None
        out_ref, dwi_ref, dwo_ref = next(it), next(it), next(it)
        dg_ref = next(it) if last else None
        i = pl.program_id(0)

        @pl.when(i == 0)
        def _():
            dwi_ref[...] = jnp.zeros(dwi_ref.shape, F32)
            dwo_ref[...] = jnp.zeros(dwo_ref.shape, F32)
            if last:
                dg_ref[...] = jnp.zeros(dg_ref.shape, F32)

        xv, dyv, g = x_ref[...], dy_ref[...], g_ref[...]
        r, xhat = _rms(xv)
        hn = (xhat * g).astype(BF16)
        wg, wu, wob = wi_ref[0], wi_ref[1], wo_ref[...].reshape(fb, dm)
        gate = _dot(hn, wg)
        up = _dot(hn, wu)
        s = _sigmoid(gate)
        silu = gate * s
        act = (silu * up).astype(BF16)
        dout = (0.5 * dyv).astype(BF16)
        dact = _dot(dout, wob, NT)
        dwo_ref[...] += _dot(act, dout, TN)
        dgate = (dact * up * (s * (1.0 + gate * (1.0 - s)))).astype(BF16)
        dup = (dact * silu).astype(BF16)
        dwi_ref[0] += _dot(hn, dgate, TN)
        dwi_ref[1] += _dot(hn, dup, TN)
        tot = _dot(dgate, wg, NT) + _dot(dup, wu, NT)
        if acc_ref is not None:
            tot = tot + acc_ref[...]
        if last:
            out_ref[...] = dyv + _rms_bwd(tot, xhat, r, g)
            dg_ref[...] += jnp.sum(tot * xhat, axis=0, keepdims=True)
        else:
            out_ref[...] = tot

    tok = pl.BlockSpec((tm, dm), lambda i: (i, 0))
    args = [x, dy, gain, wi, wo]
    in_specs = [tok, tok, pl.BlockSpec((1, dm), lambda i: (0, 0)),
                pl.BlockSpec((2, None, None, dm, fb), lambda i: (0, j, layer, 0, 0)),
                pl.BlockSpec((None, 2, None, fb // 2, dm), lambda i: (j, 0, layer, 0, 0))]
    if acc is not None:
        args.append(acc)
        in_specs.append(tok)
    out_specs = [tok, pl.BlockSpec((2, dm, fb), lambda i: (0, 0, 0)), pl.BlockSpec((fb, dm), lambda i: (0, 0))]
    out_shape = [jax.ShapeDtypeStruct((n_rows, dm), F32), jax.ShapeDtypeStruct((2, dm, fb), F32),
                 jax.ShapeDtypeStruct((fb, dm), F32)]
    if last:
        out_specs.append(pl.BlockSpec((1, dm), lambda i: (0, 0)))
        out_shape.append(jax.ShapeDtypeStruct((1, dm), F32))
    return pl.pallas_call(body, grid=(n_rows // tm,), name=name, in_specs=in_specs, out_specs=out_specs,
                          out_shape=out_shape, compiler_params=_params("arbitrary"))(*args)


def _ffn_bwd(name, x, dy, gain, wi, wo, layer):
    nj = wi.shape[1]
    acc, dwi, dwo = None, [], []
    for j in range(nj):
        res = _ffn_bwd_block("%s_%d" % (name, j), x, dy, gain, wi, wo, layer, j, acc)
        acc = res[0]
        dwi.append(res[1])
        dwo.append(res[2])
    return acc, jnp.stack(dwi, axis=1), jnp.stack(dwo, axis=0), res[3]


def _scan8(a, x, reverse):
    row = lax.broadcasted_iota(jnp.int32, a.shape, 0)
    for k in (1, 2, 4):
        if reverse:
            keep = row < SUBLANES - k
            a_s, x_s = pltpu.roll(a, SUBLANES - k, 0), pltpu.roll(x, SUBLANES - k, 0)
        else:
            keep = row >= k
            a_s, x_s = pltpu.roll(a, k, 0), pltpu.roll(x, k, 0)
        x = a * jnp.where(keep, x_s, 0.0) + x
        a = a * jnp.where(keep, a_s, 1.0)
    return a, x


def _scan_tile(a_ref, x_ref, h_ref, carry, reverse, rows):
    groups = rows // SUBLANES

    def step(n, c):
        gidx = groups - 1 - n if reverse else n
        sl = pl.ds(pl.multiple_of(gidx * SUBLANES, SUBLANES), SUBLANES)
        a_cum, h0 = _scan8(a_ref[sl, :], x_ref[sl, :], reverse)
        h = a_cum * c + h0
        h_ref[sl, :] = h
        return h[0:1] if reverse else h[SUBLANES - 1:SUBLANES]

    return lax.fori_loop(0, groups, step, carry)


def _s5_mats(lam_re, lam_im, log_dt, b_re, b_im, c_re, c_im):
    lc = S5_CHUNK
    groups, p = lam_re.shape
    h = b_re.shape[-1]
    lam = lax.complex(lam_re, lam_im)
    lam_dt = lam * jnp.exp(log_dt)[:, None]
    lam_bar = jnp.exp(lam_dt)
    b_bar = ((lam_bar - 1.0) / lam)[:, :, None] * lax.complex(b_re, b_im)
    c = lax.complex(c_re, c_im)
    pw = jnp.exp(lam_dt[None] * jnp.arange(lc + 1, dtype=F32)[:, None, None])
    resp = jnp.einsum('ghp,tgp,gpk->tghk', c, pw[:lc], b_bar, precision=HI).real
    s_idx = jnp.arange(lc)[:, None]
    u_idx = jnp.arange(lc)[None, :]
    onehot = (jnp.arange(lc)[:, None, None] == (u_idx - s_idx)[None]).astype(F32)
    m = jnp.einsum('tghk,tsu->gskuh', resp, onehot, precision=HI).reshape(groups, lc * h, lc * h)
    w = pw[lc - 1::-1][:lc].transpose(1, 0, 2)[:, :, None, :] * b_bar.transpose(0, 2, 1)[:, None]
    bm = jnp.concatenate([w.real, w.imag], axis=-1).reshape(groups, lc * h, 2 * p)
    v = c[:, None] * pw[1:lc + 1].transpose(1, 0, 2)[:, :, None, :]
    v = v.transpose(0, 3, 1, 2)
    cm = jnp.concatenate([v.real, -v.imag], axis=1).reshape(groups, 2 * p, lc * h)
    a = jnp.concatenate([pw[lc].real, pw[lc].imag], axis=-1)
    return m, bm, cm, a


def _s5_powers(lam_re, lam_im, log_dt):
    lam_dt = lax.complex(lam_re, lam_im) * jnp.exp(log_dt)[:, None]
    pw = jnp.exp(lam_dt[None] * (S5_CHUNK * jnp.arange(1, 9, dtype=F32))[:, None, None])

    def c1(z):
        return jnp.concatenate([z.real, z.real], axis=-1).reshape(z.shape[0], -1)

    def c2(z):
        return jnp.concatenate([-z.imag, z.imag], axis=-1).reshape(z.shape[0], -1)

    p1, p2 = c1(pw), c2(pw)
    apw = jnp.stack([p1[0], p2[0], p1[1], p2[1], p1[3], p2[3], jnp.zeros_like(p1[0]), jnp.zeros_like(p1[0])])
    fwd = jnp.concatenate([p1, p2], axis=0)
    rev = jnp.concatenate([c1(pw[::-1]), c2(pw[::-1])], axis=0)
    return apw, fwd, rev


def _cmul(c1, c2, x, half, conj=False):
    sw = pltpu.roll(x, half, 1)
    return c1 * x - c2 * sw if conj else c1 * x + c2 * sw


def _gather_groups(u_ref, ug_ref, nc):
    h = S5_GROUP
    rows = min(S5_REGROUP_ROWS, nc)

    def step(r, _):
        base = pl.multiple_of(r * rows, rows)
        for t in range(S5_CHUNK):
            val = u_ref[pl.ds(base * S5_CHUNK + t, rows, stride=S5_CHUNK), :]
            for g in range(S5_OCTET):
                ug_ref[g, pl.ds(base, rows), t * h:(t + 1) * h] = val[:, g * h:(g + 1) * h]
        return 0

    lax.fori_loop(0, nc // rows, step, 0)


def _scatter_groups(yg_ref, y_ref, nc):
    h = S5_GROUP
    rows = min(S5_REGROUP_ROWS, nc)

    def step(r, _):
        base = pl.multiple_of(r * rows, rows)
        for t in range(S5_CHUNK):
            y_ref[pl.ds(base * S5_CHUNK + t, rows, stride=S5_CHUNK), :] = jnp.concatenate(
                [yg_ref[g, pl.ds(base, rows), t * h:(t + 1) * h] for g in range(S5_OCTET)], axis=1)
        return 0

    lax.fori_loop(0, nc // rows, step, 0)


def _s5_fwd(name, u, m, bm, cm, apw, arows):
    n_rows, width = u.shape
    nc = n_rows // S5_CHUNK
    groups, lh, _ = m.shape
    p2 = bm.shape[2]
    gb = S5_OCTET
    lanes = gb * S5_GROUP

    def body(u_ref, m_ref, b_ref, c_ref, apw_ref, ar_ref, y_ref, sp_ref, ug_ref, yg_ref, xs_ref):
        _gather_groups(u_ref, ug_ref, nc)
        for gi in range(gb):
            xs_ref[:, gi * p2:(gi + 1) * p2] = _dot(ug_ref[gi], b_ref[gi], prec=HI)
        row = lax.broadcasted_iota(jnp.int32, (SUBLANES, p2), 0)

        def step(n, carry):
            sl = pl.ds(pl.multiple_of(n * SUBLANES, SUBLANES), SUBLANES)
            new = []
            for gi in range(gb):
                ln = slice(gi * p2, (gi + 1) * p2)
                x = xs_ref[sl, ln]
                for q, k in enumerate((1, 2, 4)):
                    xs = jnp.where(row >= k, pltpu.roll(x, k, 0), 0.0)
                    x = x + _cmul(apw_ref[2 * q:2 * q + 1, ln], apw_ref[2 * q + 1:2 * q + 2, ln], xs, p2 // 2)
                cb = jnp.broadcast_to(carry[gi], (SUBLANES, p2))
                s8 = x + _cmul(ar_ref[0:8, ln], ar_ref[8:16, ln], cb, p2 // 2)
                sp_ref[sl, ln] = jnp.where(row >= 1, pltpu.roll(s8, 1, 0), cb)
                new.append(s8[SUBLANES - 1:SUBLANES])
            return tuple(new)

        lax.fori_loop(0, nc // SUBLANES, step, tuple(jnp.zeros((1, p2), F32) for _ in range(gb)))
        for gi in range(gb):
            yg_ref[gi] = (_dot(ug_ref[gi], m_ref[gi], prec=HI)
                          + _dot(sp_ref[:, gi * p2:(gi + 1) * p2], c_ref[gi], prec=HI))
        _scatter_groups(yg_ref, y_ref, nc)

    tok = pl.BlockSpec((n_rows, lanes), lambda g: (0, g), pipeline_mode=pl.Buffered(1))
    return pl.pallas_call(
        body, grid=(groups // gb,), name=name,
        in_specs=[tok, pl.BlockSpec((gb, lh, lh), lambda g: (g, 0, 0)),
                  pl.BlockSpec((gb, lh, p2), lambda g: (g, 0, 0)), pl.BlockSpec((gb, p2, lh), lambda g: (g, 0, 0)),
                  pl.BlockSpec((8, gb * p2), lambda g: (0, g)), pl.BlockSpec((16, gb * p2), lambda g: (0, g))],
        out_specs=[tok, pl.BlockSpec((nc, gb * p2), lambda g: (0, g))],
        out_shape=[jax.ShapeDtypeStruct((n_rows, width), F32), jax.ShapeDtypeStruct((nc, groups * p2), F32)],
        scratch_shapes=[pltpu.VMEM((gb, nc, lh), F32), pltpu.VMEM((gb, nc, lh), F32), pltpu.VMEM((nc, gb * p2), F32)],
        compiler_params=_params("parallel"),
    )(u, m, bm, cm, apw, arows)


def _s5_bwd(name, u, dy, sprev, m, bm, cm, apw, arows_rev):
    n_rows, width = u.shape
    nc = n_rows // S5_CHUNK
    groups, lh, _ = m.shape
    p2 = bm.shape[2]
    half = p2 // 2
    gb = S5_OCTET
    lanes = gb * S5_GROUP

    def body(u_ref, dy_ref, sp_ref, m_ref, b_ref, c_ref, apw_ref, ar_ref,
             du_ref, dm_ref, db_ref, dc_ref, da_ref, ug_ref, dyg_ref, ds_ref, gx_ref):
        _gather_groups(u_ref, ug_ref, nc)
        _gather_groups(dy_ref, dyg_ref, nc)
        for gi in range(gb):
            ds_ref[:, gi * p2:(gi + 1) * p2] = _dot(dyg_ref[gi], c_ref[gi], NT, prec=HI)
        row = lax.broadcasted_iota(jnp.int32, (SUBLANES, p2), 0)
        lane = lax.broadcasted_iota(jnp.int32, (SUBLANES, p2), 1)
        ngroups = nc // SUBLANES

        def step(n, state):
            carry, nxt, dacc = state
            sl = pl.ds(pl.multiple_of((ngroups - 1 - n) * SUBLANES, SUBLANES), SUBLANES)
            new_c, new_n, new_d = [], [], []
            for gi in range(gb):
                ln = slice(gi * p2, (gi + 1) * p2)
                d8 = ds_ref[sl, ln]
                x = jnp.where(row < SUBLANES - 1, pltpu.roll(d8, SUBLANES - 1, 0),
                              jnp.broadcast_to(nxt[gi], (SUBLANES, p2)))
                for q, k in enumerate((1, 2, 4)):
                    xs = jnp.where(row < SUBLANES - k, pltpu.roll(x, SUBLANES - k, 0), 0.0)
                    x = x + _cmul(apw_ref[2 * q:2 * q + 1, ln], apw_ref[2 * q + 1:2 * q + 2, ln], xs, half, conj=True)
                cb = jnp.broadcast_to(carry[gi], (SUBLANES, p2))
                g8 = x + _cmul(ar_ref[0:8, ln], ar_ref[8:16, ln], cb, half, conj=True)
                gx_ref[sl, ln] = g8
                s8 = sp_ref[sl, ln]
                p1 = g8 * s8
                pq = g8 * pltpu.roll(s8, half, 1)
                d_a = jnp.where(lane < half, p1 + pltpu.roll(p1, half, 1), pq - pltpu.roll(pq, half, 1))
                new_c.append(g8[0:1])
                new_n.append(d8[0:1])
                new_d.append(dacc[gi] + jnp.sum(d_a, axis=0, keepdims=True))
            return tuple(new_c), tuple(new_n), tuple(new_d)

        zeros = tuple(jnp.zeros((1, p2), F32) for _ in range(gb))
        _, _, dacc = lax.fori_loop(0, ngroups, step, (zeros, zeros, zeros))
        for gi in range(gb):
            ln = slice(gi * p2, (gi + 1) * p2)
            da_ref[:, ln] = dacc[gi]
            ug, dyg, gxg = ug_ref[gi], dyg_ref[gi], gx_ref[:, ln]
            dm_ref[gi] = _dot(ug, dyg, TN, prec=HI)
            dc_ref[gi] = _dot(sp_ref[:, ln], dyg, TN, prec=HI)
            db_ref[gi] = _dot(ug, gxg, TN, prec=HI)
            dyg_ref[gi] = _dot(dyg, m_ref[gi], NT, prec=HI) + _dot(gxg, b_ref[gi], NT, prec=HI)
        _scatter_groups(dyg_ref, du_ref, nc)

    tok = pl.BlockSpec((n_rows, lanes), lambda g: (0, g), pipeline_mode=pl.Buffered(1))
    tok_s = pl.BlockSpec((nc, gb * p2), lambda g: (0, g))
    mat_m = pl.BlockSpec((gb, lh, lh), lambda g: (g, 0, 0))
    mat_b = pl.BlockSpec((gb, lh, p2), lambda g: (g, 0, 0))
    mat_c = pl.BlockSpec((gb, p2, lh), lambda g: (g, 0, 0))
    return pl.pallas_call(
        body, grid=(groups // gb,), name=name,
        in_specs=[tok, tok, tok_s, mat_m, mat_b, mat_c,
                  pl.BlockSpec((8, gb * p2), lambda g: (0, g)), pl.BlockSpec((16, gb * p2), lambda g: (0, g))],
        out_specs=[tok, mat_m, mat_b, mat_c, pl.BlockSpec((1, gb * p2), lambda g: (0, g))],
        out_shape=[jax.ShapeDtypeStruct((n_rows, width), F32), jax.ShapeDtypeStruct(m.shape, F32),
                   jax.ShapeDtypeStruct(bm.shape, F32), jax.ShapeDtypeStruct(cm.shape, F32),
                   jax.ShapeDtypeStruct((1, groups * p2), F32)],
        scratch_shapes=[pltpu.VMEM((gb, nc, lh), F32), pltpu.VMEM((gb, nc, lh), F32),
                        pltpu.VMEM((nc, gb * p2), F32), pltpu.VMEM((nc, gb * p2), F32)],
        compiler_params=_params("parallel"),
    )(u, dy, sprev, m, bm, cm, apw, arows_rev)


def _split_dot(v, tri):
    hi = v.astype(BF16)
    lo = (v - hi.astype(F32)).astype(BF16)
    return _dot(hi, tri) + _dot(lo, tri)


def _sb_block(qb, kblk, causal):
    z = _dot(qb, kblk, NT)
    sp = _softplus(z)
    lk = -sp
    if causal is not None:
        lk = jnp.where(causal, lk, 0.0)
    return lk, z - sp


def _sb_more(kb, carry):
    return (kb >= 0) & (jnp.max(carry) > SB_UNDERFLOW)


def _sb_fwd(name, q, k, v):
    heads, n_rows, hd = q.shape
    tq = _tile(n_rows // 2, 256)
    scale = hd ** -0.5

    def body(q_ref, k_ref, v_ref, o_ref):
        qi = pl.program_id(1)
        qb = q_ref[0] * scale
        row = lax.broadcasted_iota(jnp.int32, (tq, tq), 0)
        col = lax.broadcasted_iota(jnp.int32, (tq, tq), 1)
        tri = (row > col).astype(BF16)

        def block(kb, carry, acc, causal):
            ks = pl.ds(pl.multiple_of(kb * tq, tq), tq)
            lk, lb = _sb_block(qb, k_ref[0, ks, :], causal)
            a = jnp.exp(lb + _split_dot(lk, tri) + carry)
            if causal is not None:
                a = jnp.where(causal, a, 0.0)
            acc = acc + _dot(a.astype(BF16), v_ref[0, ks, :])
            return carry + jnp.sum(lk, axis=1, keepdims=True), acc

        carry, acc = block(qi, jnp.zeros((tq, 1), F32), jnp.zeros((tq, hd), F32), col < row)
        _, _, acc = lax.while_loop(lambda st: _sb_more(st[0], st[1]),
                                   lambda st: (st[0] - 1,) + block(st[0], st[1], st[2], None), (qi - 1, carry, acc))
        o_ref[0] = acc

    kv = pl.BlockSpec((1, n_rows, hd), lambda h, i: (h, 0, 0))
    qs = pl.BlockSpec((1, tq, hd), lambda h, i: (h, i, 0))
    return pl.pallas_call(body, grid=(heads, n_rows // tq), in_specs=[qs, kv, kv], out_specs=qs, name=name,
                          out_shape=jax.ShapeDtypeStruct((heads, n_rows, hd), F32),
                          compiler_params=_params("parallel", "arbitrary"))(q, k, v)


def _sb_bwd(name, q, k, v, o, do):
    heads, n_rows, hd = q.shape
    tq = _tile(n_rows // 2, 256)
    scale = hd ** -0.5

    def body(q_ref, k_ref, v_ref, o_ref, do_ref, dq_ref, dk_ref, dv_ref):
        qi = pl.program_id(1)

        @pl.when(qi == 0)
        def _():
            dk_ref[...] = jnp.zeros(dk_ref.shape, F32)
            dv_ref[...] = jnp.zeros(dv_ref.shape, F32)

        qb = q_ref[0] * scale
        dob16 = do_ref[0].astype(BF16)
        delta = jnp.sum(dob16.astype(F32) * o_ref[0], axis=1, keepdims=True)
        row = lax.broadcasted_iota(jnp.int32, (tq, tq), 0)
        col = lax.broadcasted_iota(jnp.int32, (tq, tq), 1)
        tri = (row > col).astype(BF16)
        tri_incl = (row >= col).astype(BF16)

        def block(kb, carry, pcarry, dq, causal):
            ks = pl.ds(pl.multiple_of(kb * tq, tq), tq)
            kblk, vblk = k_ref[0, ks, :], v_ref[0, ks, :]
            lk, lb = _sb_block(qb, kblk, causal)
            a = jnp.exp(lb + _split_dot(lk, tri) + carry)
            if causal is not None:
                a = jnp.where(causal, a, 0.0)
            a16 = a.astype(BF16)
            p = _dot(dob16, vblk, NT) * a16.astype(F32)
            beta = jnp.exp(lb)
            dz = p * (1.0 - beta) - beta * (delta - pcarry - _split_dot(p, tri_incl))
            if causal is not None:
                dz = jnp.where(causal, dz, 0.0)
            dz16 = dz.astype(BF16)
            dk_ref[0, ks, :] += _dot(dz16, qb, TN)
            dv_ref[0, ks, :] += _dot(a16, dob16, TN)
            return (carry + jnp.sum(lk, axis=1, keepdims=True), pcarry + jnp.sum(p, axis=1, keepdims=True),
                    dq + _dot(dz16, kblk))

        zero = jnp.zeros((tq, 1), F32)
        st = block(qi, zero, zero, jnp.zeros((tq, hd), F32), col < row)
        st = lax.while_loop(lambda s: _sb_more(s[0], s[1]),
                            lambda s: (s[0] - 1,) + block(s[0], s[1], s[2], s[3], None), (qi - 1,) + st)
        dq_ref[0] = st[3] * scale

    kv = pl.BlockSpec((1, n_rows, hd), lambda h, i: (h, 0, 0))
    qs = pl.BlockSpec((1, tq, hd), lambda h, i: (h, i, 0))
    full = jax.ShapeDtypeStruct((heads, n_rows, hd), F32)
    return pl.pallas_call(body, grid=(heads, n_rows // tq), in_specs=[qs, kv, kv, qs, qs], out_specs=[qs, kv, kv],
                          out_shape=[full, full, full], name=name,
                          compiler_params=_params("parallel", "arbitrary"))(q, k, v, o, do)


def _block_diag(xb, w_ref_val, dims):
    nb = w_ref_val.shape[0]
    bw = xb.shape[1] // nb
    return jnp.concatenate([_dot(xb[:, n * bw:(n + 1) * bw], w_ref_val[n], dims) for n in range(nb)], axis=1)


def _lru_gates_fwd(name, gx, conv_w, conv_b, wa, ba, wx, bx, lam):
    n_rows, w2 = gx.shape
    w = w2 // 2
    tm = _tile(n_rows, 256)

    def fn(i, nt, br, prev, cw, cb, wa_v, ba_v, wx_v, bx_v, lam_v):
        xc = cb + sum(cw[k:k + 1] * _shift_down(br, prev, LRU_CONV - 1 - k, i == 0) for k in range(LRU_CONV))
        xb = xc.astype(BF16)
        r = _sigmoid(_block_diag(xb, wa_v, NN) + ba_v)
        ig = _sigmoid(_block_diag(xb, wx_v, NN) + bx_v)
        log_a = (-LRU_C * r) * _softplus(-lam_v)
        a = jnp.exp(log_a)
        gated = (ig * xc) * _one_minus_a2_sqrt(log_a)
        return (xc, r, ig, a, gated), ()

    return _rowwise(name, fn, [('t', gx, w, 1), ('p', gx, w, 1), ('b', conv_w), ('b', conv_b), ('b', wa), ('b', ba),
                               ('b', wx), ('b', bx), ('b', lam)], [(w, F32)] * 5, [], n_rows, tm)


def _lru_scan_fwd(name, a, gated, gx):
    n_rows, w = a.shape
    tm = _tile(n_rows, 256)

    def body(a_ref, x_ref, bg_ref, hs_ref, y_ref, carry_ref):
        @pl.when(pl.program_id(0) == 0)
        def _():
            carry_ref[...] = jnp.zeros(carry_ref.shape, F32)
        carry_ref[...] = _scan_tile(a_ref, x_ref, hs_ref, carry_ref[...], False, tm)
        y_ref[...] = (_gelu(bg_ref[...]) * hs_ref[...]).astype(BF16)

    tok = pl.BlockSpec((tm, w), lambda i: (i, 0))
    return pl.pallas_call(body, grid=(n_rows // tm,), in_specs=[tok, tok, tok], out_specs=[tok, tok], name=name,
                          out_shape=[jax.ShapeDtypeStruct((n_rows, w), F32), jax.ShapeDtypeStruct((n_rows, w), BF16)],
                          scratch_shapes=[pltpu.VMEM((1, w), F32)], compiler_params=_params("arbitrary"))(a, gated, gx)


def _lru_scan_bwd(name, a, dy, gx):
    n_rows, w = a.shape
    tm = _tile(n_rows, 256)
    nt = n_rows // tm
    per8 = tm // SUBLANES

    def body(a_ref, an_ref, dy_ref, bg_ref, lam_ref, carry_ref, aup_ref, dhs_ref):
        i = pl.program_id(0)

        @pl.when(i == 0)
        def _():
            carry_ref[...] = jnp.zeros(carry_ref.shape, F32)
        aup_ref[...] = _shift_up(a_ref[...], an_ref[...], 1, i == 0)
        dhs_ref[...] = dy_ref[...] * _gelu(bg_ref[...])
        carry_ref[...] = _scan_tile(aup_ref, dhs_ref, lam_ref, carry_ref[...], True, tm)

    tok = pl.BlockSpec((tm, w), lambda i: (nt - 1 - i, 0))
    nxt = pl.BlockSpec((SUBLANES, w), lambda i: (jnp.minimum((nt - i) * per8, n_rows // SUBLANES - 1), 0))
    return pl.pallas_call(body, grid=(nt,), in_specs=[tok, nxt, tok, tok], out_specs=tok, name=name,
                          out_shape=jax.ShapeDtypeStruct((n_rows, w), F32),
                          scratch_shapes=[pltpu.VMEM((1, w), F32), pltpu.VMEM((tm, w), F32), pltpu.VMEM((tm, w), F32)],
                          compiler_params=_params("arbitrary"))(a, a, dy, gx)


def _lru_gates_bwd(name, lam_t, hs, xc, r, ig, a, wa, wx, lam):
    n_rows, w = xc.shape
    nb, bw, _ = wa.shape
    tm = _tile(n_rows, 256)

    def fn(i, nt, lt, hs_v, hs_prev, xc_v, r_v, ig_v, a_v, wa_v, wx_v, lam_v):
        sp = _softplus(-lam_v)
        log_a = (-LRU_C * r_v) * sp
        mult = _one_minus_a2_sqrt(log_a)
        d_a = lt * _shift_down(hs_v, hs_prev, 1, i == 0)
        d_ig = lt * xc_v * mult
        d_mult = lt * ig_v * xc_v
        d_log_a = d_a * a_v - d_mult * (a_v * a_v) / mult
        d_ra = d_log_a * (-LRU_C * sp) * r_v * (1.0 - r_v)
        d_ia = d_ig * ig_v * (1.0 - ig_v)
        d_ra16, d_ia16, xb = d_ra.astype(BF16), d_ia.astype(BF16), xc_v.astype(BF16)
        dxc = lt * ig_v * mult + _block_diag(d_ra16, wa_v, NT) + _block_diag(d_ia16, wx_v, NT)
        dwa = jnp.concatenate([_dot(xb[:, n * bw:(n + 1) * bw], d_ra16[:, n * bw:(n + 1) * bw], TN)
                               for n in range(nb)], axis=0)
        dwx = jnp.concatenate([_dot(xb[:, n * bw:(n + 1) * bw], d_ia16[:, n * bw:(n + 1) * bw], TN)
                               for n in range(nb)], axis=0)
        col = lambda t: jnp.sum(t, axis=0, keepdims=True)
        return (dxc,), (dwa, dwx, col(d_ra), col(d_ia), col(d_log_a * (-LRU_C * r_v)))

    tiled = lambda arr: ('t', arr, w, 0)
    return _rowwise(name, fn, [tiled(lam_t), tiled(hs), ('p', hs, w, 0), tiled(xc), tiled(r), tiled(ig), tiled(a),
                               ('b', wa), ('b', wx), ('b', lam)],
                    [(w, F32)], [(nb * bw, bw), (nb * bw, bw), (1, w), (1, w), (1, w)], n_rows, tm)


def _lru_conv_bwd(name, dxc, gx, dy, hs, conv_w):
    n_rows, w = dxc.shape
    tm = _tile(n_rows, 256)

    def fn(i, nt, dxc_v, dxc_next, bg, br, br_prev, dy_v, hs_v, cw):
        dbr = sum(cw[k:k + 1] * _shift_up(dxc_v, dxc_next, LRU_CONV - 1 - k, i == nt - 1) for k in range(LRU_CONV))
        dbg = dy_v * hs_v * _gelu_grad(bg)
        dcw = [jnp.sum(dxc_v * _shift_down(br, br_prev, LRU_CONV - 1 - k, i == 0), axis=0, keepdims=True)
               for k in range(LRU_CONV)]
        dcw = jnp.concatenate(dcw + [jnp.zeros((SUBLANES - LRU_CONV, w), F32)], axis=0)
        return (jnp.concatenate([dbg, dbr], axis=1),), (dcw, jnp.sum(dxc_v, axis=0, keepdims=True))

    return _rowwise(name, fn, [('t', dxc, w, 0), ('n', dxc, w, 0), ('t', gx, w, 0), ('t', gx, w, 1), ('p', gx, w, 1),
                               ('t', dy, w, 0), ('t', hs, w, 0), ('b', conv_w)],
                    [(2 * w, BF16)], [(SUBLANES, w), (1, w)], n_rows, tm)


def _loss_head(name, h, gain, target):
    n_rows, dm = h.shape
    tm = _tile(n_rows, 512)

    def fn(i, nt, hv, tv, g):
        r, xhat = _rms(hv)
        err = xhat * g - tv
        dy = err * (1.0 / dm)
        return ((_rms_bwd(dy, xhat, r, g),),
                (jnp.sum(err * err, axis=0, keepdims=True), jnp.sum(dy * xhat, axis=0, keepdims=True)))

    return _rowwise(name, fn, [('t', h, dm, 0), ('t', target, dm, 0), ('b', gain)], [(dm, F32)], [(1, dm), (1, dm)],
                    n_rows, tm)


def _adamw(name, gparts, w, m, v):
    n_parts, n_rows, cols = gparts.shape
    tr = n_rows
    for cand in (256, 128, 64, 32, 16, 8):
        if n_rows % cand == 0:
            tr = cand
            break
    c1 = 1.0 - ADAM_B1 ** ADAM_STEP
    c2 = 1.0 - ADAM_B2 ** ADAM_STEP

    def body(gp_ref, w_ref, m_ref, v_ref, g_ref, d_ref, nm_ref, nv_ref):
        g = gp_ref[0].astype(F32)
        for p in range(1, n_parts):
            g = g + gp_ref[p].astype(F32)
        m_new = ADAM_B1 * m_ref[...] + (1.0 - ADAM_B1) * g
        v_new = ADAM_B2 * v_ref[...] + (1.0 - ADAM_B2) * (g * g)
        m_hat = m_new / c1
        v_hat = v_new / c2
        g_ref[...] = g
        d_ref[...] = -ADAM_LR * (m_hat / (jnp.sqrt(v_hat) + ADAM_EPS) + ADAM_WD * w_ref[...])
        nm_ref[...] = m_new
        nv_ref[...] = v_new

    blk = pl.BlockSpec((tr, cols), lambda i: (i, 0))
    shp = jax.ShapeDtypeStruct((n_rows, cols), F32)
    return pl.pallas_call(body, grid=(n_rows // tr,), name=name,
                          in_specs=[pl.BlockSpec((n_parts, tr, cols), lambda i: (0, i, 0)), blk, blk, blk],
                          out_specs=[blk, blk, blk, blk], out_shape=[shp, shp, shp, shp],
                          compiler_params=_params("parallel"))(gparts, w, m, v)


def _pack_rows(arrays, cols, lead=0):
    flat = [a.reshape(a.shape[:lead] + (-1,)) for a in arrays]
    cat = jnp.concatenate(flat, axis=lead) if len(flat) > 1 else flat[0]
    n = cat.shape[lead]
    pad = (-n) % (cols * PACK_ROWS)
    if pad:
        cat = jnp.pad(cat, [(0, 0)] * lead + [(0, pad)])
    return cat.reshape(cat.shape[:lead] + (-1, cols))


def _unpack_rows(packed, shapes, lead=0):
    flat = packed.reshape(packed.shape[:lead] + (-1,))
    out, off = [], 0
    for s in shapes:
        n = math.prod(s)
        out.append(lax.slice_in_dim(flat, off, off + n, axis=lead).reshape(flat.shape[:lead] + tuple(s)))
        off += n
    return out


def kernel(x, ffn1_norm, ffn1_w_in, ffn1_w_out, mix_norm, ffn2_norm, ffn2_w_in, ffn2_w_out, final_norm, s5_w_in, s5_lam_re, s5_lam_im, s5_log_dt, s5_b_re, s5_b_im, s5_c_re, s5_c_im, s5_d, s5_w_out, sb_w_qkv, sb_w_out, lru_w_in, lru_conv_w, lru_conv_b, lru_w_a, lru_b_a, lru_w_x, lru_b_x, lru_lambda, lru_w_out, loss_target, m_ffn1_norm, m_ffn1_w_in, m_ffn1_w_out, m_mix_norm, m_ffn2_norm, m_ffn2_w_in, m_ffn2_w_out, m_final_norm, m_s5_w_in, m_s5_lam_re, m_s5_lam_im, m_s5_log_dt, m_s5_b_re, m_s5_b_im, m_s5_c_re, m_s5_c_im, m_s5_d, m_s5_w_out, m_sb_w_qkv, m_sb_w_out, m_lru_w_in, m_lru_conv_w, m_lru_conv_b, m_lru_w_a, m_lru_b_a, m_lru_w_x, m_lru_b_x, m_lru_lambda, m_lru_w_out, v_ffn1_norm, v_ffn1_w_in, v_ffn1_w_out, v_mix_norm, v_ffn2_norm, v_ffn2_w_in, v_ffn2_w_out, v_final_norm, v_s5_w_in, v_s5_lam_re, v_s5_lam_im, v_s5_log_dt, v_s5_b_re, v_s5_b_im, v_s5_c_re, v_s5_c_im, v_s5_d, v_s5_w_out, v_sb_w_qkv, v_sb_w_out, v_lru_w_in, v_lru_conv_w, v_lru_conv_b, v_lru_w_a, v_lru_b_a, v_lru_w_x, v_lru_b_x, v_lru_lambda, v_lru_w_out):
    local = dict(locals())
    W = {n: local[n] for n in WEIGHTS}
    M = {n: local["m_" + n] for n in WEIGHTS}
    V = {n: local["v_" + n] for n in WEIGHTS}

    h0 = x[0]
    target = loss_target[0]
    n_rows, dm = h0.shape
    depth = ffn1_norm.shape[0]

    def ffn_weights(tag):
        wi = _all_gather("ag_%s_w_in" % tag, W[tag + "_w_in"].astype(BF16))
        wo = _all_gather("ag_%s_w_out" % tag, W[tag + "_w_out"].astype(BF16))
        return (wi.reshape((2, N_DEV // 2) + wi.shape[1:]), wo.reshape((N_DEV // 2, 2) + wo.shape[1:]))

    ffn_w = {"ffn1": ffn_weights("ffn1"), "ffn2": ffn_weights("ffn2")}
    mix_shapes = [W[n].shape for n in MIXER_BIG]
    mix_g = _all_gather("ag_mixers", _pack_rows([W[n].astype(BF16) for n in MIXER_BIG], dm))
    full = {n: _unshard(a, SHARD_AXIS[n]) for n, a in zip(MIXER_BIG, _unpack_rows(mix_g, mix_shapes, lead=1))}
    small_shapes = [W[n].shape for n in SMALL_SHARDED]
    small_g = _all_gather("ag_small", _pack_rows([W[n] for n in SMALL_SHARDED], 128))
    full.update({n: _unshard(a, SHARD_AXIS[n])
                 for n, a in zip(SMALL_SHARDED, _unpack_rows(small_g, small_shapes, lead=1))})

    n_s5 = s5_w_in.shape[0]
    s5_groups = s5_lam_re.shape[1]
    heads = dm // SB_HEAD_DIM

    grads = {}
    saved = []
    h = h0

    for layer in range(depth):
        kind, j = layer % 3, layer // 3
        rec = {"h0": h}
        h = _ffn_fwd("ffn1_fwd_%d" % layer, h, ffn1_norm[layer:layer + 1], *ffn_w["ffn1"], layer)
        rec["h1"] = h
        gain = mix_norm[layer:layer + 1]
        if kind == 0:
            (u,) = _mm_fwd("s5_in_%d" % layer, h, full["s5_w_in"][j], F32, gain=gain)
            pars = (s5_lam_re[j], s5_lam_im[j], s5_log_dt[j], s5_b_re[j], s5_b_im[j], s5_c_re[j], s5_c_im[j])
            mats, mats_vjp = jax.vjp(_s5_mats, *pars)
            apw, ar_fwd, ar_rev = _s5_powers(*pars[:3])
            ys, sprev = _s5_fwd("s5_core_%d" % layer, u, *mats[:3], apw, ar_fwd)
            d_skip = full["s5_d"][j:j + 1]
            (z,) = _rowwise("s5_gelu_%d" % layer, lambda i, nt, ys_v, u_v, d_v: ((_gelu(ys_v + d_v * u_v),), ()),
                            [('t', ys, dm, 0), ('t', u, dm, 0), ('b', d_skip)], [(dm, BF16)], [], n_rows,
                            _tile(n_rows, 512))
            h, vg = _mm_fwd("s5_out_%d" % layer, z, full["s5_w_out"][j], F32, resid=h, glu=True)
            rec.update(u=u, ys=ys, sprev=sprev, z=z, vg=vg, mats=mats, mats_vjp=mats_vjp, apw=apw,
                       ar_rev=ar_rev, d_skip=d_skip)
        elif kind == 1:
            (qkv,) = _mm_fwd("sb_in_%d" % layer, h, full["sb_w_qkv"][j], BF16, gain=gain)
            qkv_h = qkv.reshape(n_rows, 3, heads, SB_HEAD_DIM).transpose(1, 2, 0, 3)
            o = _sb_fwd("sb_attn_%d" % layer, qkv_h[0], qkv_h[1], qkv_h[2])
            o_flat = o.transpose(1, 0, 2).reshape(n_rows, dm).astype(BF16)
            (h,) = _mm_fwd("sb_out_%d" % layer, o_flat, full["sb_w_out"][j], F32, resid=h)
            rec.update(qkv_h=qkv_h, o=o, o_flat=o_flat)
        else:
            (gx,) = _mm_fwd("lru_in_%d" % layer, h, full["lru_w_in"][j], F32, gain=gain)
            wa, wx = full["lru_w_a"][j], full["lru_w_x"][j]
            ba, bx = full["lru_b_a"][j].reshape(1, dm), full["lru_b_x"][j].reshape(1, dm)
            lam_row = full["lru_lambda"][j:j + 1]
            xc, r, ig, a, gated = _lru_gates_fwd("lru_gates_%d" % layer, gx, full["lru_conv_w"][j],
                                                 full["lru_conv_b"][j:j + 1], wa, ba, wx, bx, lam_row)
            hs, y = _lru_scan_fwd("lru_scan_%d" % layer, a, gated, gx)
            (h,) = _mm_fwd("lru_out_%d" % layer, y, full["lru_w_out"][j], F32, resid=h)
            rec.update(gx=gx, xc=xc, r=r, ig=ig, a=a, hs=hs, y=y, wa=wa, wx=wx, lam_row=lam_row)
        rec["h2"] = h
        h = _ffn_fwd("ffn2_fwd_%d" % layer, h, ffn2_norm[layer:layer + 1], *ffn_w["ffn2"], layer)
        saved.append(rec)

    dh, err2, dgf = _loss_head("loss_head", h, final_norm.reshape(1, dm), target)
    loss = lax.psum(0.5 / dm * jnp.sum(err2), ("x", "y", "c"))
    grads["final_norm"] = dgf.reshape(final_norm.shape)

    per_layer = {n: [None] * depth for n in ("ffn1_norm", "mix_norm", "ffn2_norm", "ffn1_w_in", "ffn1_w_out",
                                             "ffn2_w_in", "ffn2_w_out")}
    mixer_grads = {}

    def put(name, j, value, count):
        mixer_grads.setdefault(name, [None] * count)[j] = value

    for layer in reversed(range(depth)):
        kind, j = layer % 3, layer // 3
        rec = saved[layer]
        dh, dwi, dwo, dg = _ffn_bwd("ffn2_bwd_%d" % layer, rec["h2"], dh, ffn2_norm[layer:layer + 1],
                                    *ffn_w["ffn2"], layer)
        per_layer["ffn2_w_in"][layer], per_layer["ffn2_w_out"][layer], per_layer["ffn2_norm"][layer] = dwi, dwo, dg
        gain = mix_norm[layer:layer + 1]
        if kind == 0:
            dvg, = _rowwise("s5_glu_bwd_%d" % layer,
                            lambda i, nt, d_v, vg_v: ((jnp.concatenate(
                                [d_v * _sigmoid(vg_v[:, dm:]),
                                 d_v * vg_v[:, :dm] * _sigmoid(vg_v[:, dm:]) * (1.0 - _sigmoid(vg_v[:, dm:]))],
                                axis=1),), ()),
                            [('t', dh, dm, 0), ('t', rec["vg"], 2 * dm, 0)], [(2 * dm, BF16)], [], n_rows,
                            _tile(n_rows, 256))
            dz, dw_out = _mm_bwd("s5_out_bwd_%d" % layer, rec["z"], dvg, full["s5_w_out"][j])

            def gelu_bwd(i, nt, dz_v, ys_v, u_v, d_v):
                dy_v = dz_v * _gelu_grad(ys_v + d_v * u_v)
                return (dy_v,), (jnp.sum(dy_v * u_v, axis=0, keepdims=True),)

            dys, dd = _rowwise("s5_gelu_bwd_%d" % layer, gelu_bwd,
                               [('t', dz, dm, 0), ('t', rec["ys"], dm, 0), ('t', rec["u"], dm, 0),
                                ('b', rec["d_skip"])], [(dm, F32)], [(1, dm)], n_rows, _tile(n_rows, 512))
            m_, bm_, cm_, _ = rec["mats"]
            du_core, dm_m, dm_b, dm_c, d_a = _s5_bwd("s5_core_bwd_%d" % layer, rec["u"], dys, rec["sprev"],
                                                    m_, bm_, cm_, rec["apw"], rec["ar_rev"])
            dpars = rec["mats_vjp"]((dm_m, dm_b, dm_c, d_a.reshape(s5_groups, -1)))
            for nme, val in zip(("s5_lam_re", "s5_lam_im", "s5_log_dt", "s5_b_re", "s5_b_im", "s5_c_re", "s5_c_im"),
                                dpars):
                put(nme, j, val, n_s5)
            (du,) = _rowwise("s5_du_%d" % layer, lambda i, nt, a_v, dy_v, d_v: ((a_v + dy_v * d_v,), ()),
                             [('t', du_core, dm, 0), ('t', dys, dm, 0), ('b', rec["d_skip"])],
                             [(dm, BF16)], [], n_rows, _tile(n_rows, 512))
            dh, dw_in, dgm = _mm_bwd("s5_in_bwd_%d" % layer, rec["h1"], du, full["s5_w_in"][j], gain=gain, dres=dh)
            put("s5_d", j, dd[0], n_s5)
            put("s5_w_out", j, dw_out, n_s5)
            put("s5_w_in", j, dw_in, n_s5)
        elif kind == 1:
            do_flat, dw_out = _mm_bwd("sb_out_bwd_%d" % layer, rec["o_flat"], dh, full["sb_w_out"][j])
            do = do_flat.reshape(n_rows, heads, SB_HEAD_DIM).transpose(1, 0, 2)
            qkv_h = rec["qkv_h"]
            dq, dk, dv = _sb_bwd("sb_attn_bwd_%d" % layer, qkv_h[0], qkv_h[1], qkv_h[2], rec["o"], do)
            dqkv = jnp.stack([dq, dk, dv]).transpose(2, 0, 1, 3).reshape(n_rows, 3 * dm).astype(BF16)
            dh, dw_in, dgm = _mm_bwd("sb_in_bwd_%d" % layer, rec["h1"], dqkv, full["sb_w_qkv"][j], gain=gain, dres=dh)
            put("sb_w_out", j, dw_out, 1)
            put("sb_w_qkv", j, dw_in, 1)
        else:
            dy, dw_out = _mm_bwd("lru_out_bwd_%d" % layer, rec["y"], dh, full["lru_w_out"][j])
            lam_t = _lru_scan_bwd("lru_scan_bwd_%d" % layer, rec["a"], dy, rec["gx"])
            dxc, dwa, dwx, dba, dbx, dsp = _lru_gates_bwd("lru_gates_bwd_%d" % layer, lam_t, rec["hs"], rec["xc"],
                                                          rec["r"], rec["ig"], rec["a"], rec["wa"], rec["wx"],
                                                          rec["lam_row"])
            dgx, dcw, dcb = _lru_conv_bwd("lru_conv_bwd_%d" % layer, dxc, rec["gx"], dy, rec["hs"],
                                          full["lru_conv_w"][j])
            dh, dw_in, dgm = _mm_bwd("lru_in_bwd_%d" % layer, rec["h1"], dgx, full["lru_w_in"][j], gain=gain, dres=dh)
            nb = rec["wa"].shape[0]
            put("lru_w_out", j, dw_out, 1)
            put("lru_w_in", j, dw_in, 1)
            put("lru_w_a", j, dwa.reshape(rec["wa"].shape), 1)
            put("lru_w_x", j, dwx.reshape(rec["wx"].shape), 1)
            put("lru_b_a", j, dba.reshape(nb, -1), 1)
            put("lru_b_x", j, dbx.reshape(nb, -1), 1)
            put("lru_conv_w", j, dcw[:LRU_CONV], 1)
            put("lru_conv_b", j, dcb[0], 1)
            put("lru_lambda", j, (dsp * -_sigmoid(-rec["lam_row"]))[0], 1)
        per_layer["mix_norm"][layer] = dgm
        dh, dwi, dwo, dg = _ffn_bwd("ffn1_bwd_%d" % layer, rec["h0"], dh, ffn1_norm[layer:layer + 1],
                                    *ffn_w["ffn1"], layer)
        per_layer["ffn1_w_in"][layer], per_layer["ffn1_w_out"][layer], per_layer["ffn1_norm"][layer] = dwi, dwo, dg

    grad_x = dh[None]
    for n in ("ffn1_norm", "mix_norm", "ffn2_norm"):
        grads[n] = jnp.concatenate(per_layer[n], axis=0)
    for n, parts in mixer_grads.items():
        grads[n] = jnp.stack(parts)

    out_g, out_d, out_m, out_v = {}, {}, {}, {}

    def finish(names, res, shapes):
        for n, g_, d_, m_, v_ in zip(names, *[_unpack_rows(t, shapes) for t in res]):
            out_g[n], out_d[n], out_m[n], out_v[n] = g_, d_, m_, v_

    for tag in ("ffn1", "ffn2"):
        n_in, n_out = tag + "_w_in", tag + "_w_out"
        dwi = jnp.stack(per_layer[n_in], axis=2)
        recv = _exchange("xchg_" + n_in, dwi.reshape((N_DEV, -1, dm)).astype(BF16))
        finish([n_in], _adamw("adamw_" + n_in, recv, *[t[n_in].reshape(-1, dm) for t in (W, M, V)]), [W[n_in].shape])
        dwo = jnp.stack([t.reshape(N_DEV // 2, 2, -1, dm) for t in per_layer[n_out]], axis=2)
        recv = _exchange("xchg_" + n_out, dwo.reshape((N_DEV, -1, dm)).astype(BF16))
        finish([n_out], _adamw("adamw_" + n_out, recv, *[t[n_out].reshape(-1, dm) for t in (W, M, V)]),
               [W[n_out].shape])

    send = _pack_rows([_shard_blocks(grads[n], SHARD_AXIS[n]).astype(BF16) for n in MIXER_BIG], dm, lead=1)
    recv = _exchange("xchg_mixers", send)
    finish(MIXER_BIG, _adamw("adamw_mixers", recv, *[_pack_rows([t[n] for n in MIXER_BIG], dm) for t in (W, M, V)]),
           mix_shapes)

    small_names = REPLICATED + SMALL_SHARDED
    small_full_shapes = [grads[n].shape for n in small_names]
    parts = _all_gather("ag_small_grads", _pack_rows([grads[n] for n in small_names], 128))
    zero = jnp.zeros(parts.shape[1:], F32)
    summed = _adamw("sum_small_grads", parts, zero, zero, zero)[0]
    small_sum = dict(zip(small_names, _unpack_rows(summed, small_full_shapes)))
    me = 4 * lax.axis_index("x") + 2 * lax.axis_index("y") + lax.axis_index("c")
    rep_shapes = [W[n].shape for n in REPLICATED]
    g_rep = _pack_rows([small_sum[n] for n in REPLICATED], 128)[None]
    finish(REPLICATED, _adamw("adamw_replicated", g_rep, *[_pack_rows([t[n] for n in REPLICATED], 128)
                                                           for t in (W, M, V)]), rep_shapes)
    g_loc = []
    for n in SMALL_SHARDED:
        ax = SHARD_AXIS[n]
        size = W[n].shape[ax]
        g_loc.append(lax.dynamic_slice_in_dim(small_sum[n], me * size, size, axis=ax))
    finish(SMALL_SHARDED, _adamw("adamw_small", _pack_rows(g_loc, 128)[None],
                                 *[_pack_rows([t[n] for n in SMALL_SHARDED], 128) for t in (W, M, V)]), small_shapes)

    return (loss, grad_x, *[out_g[n] for n in WEIGHTS], *[out_d[n] for n in WEIGHTS],
            *[out_m[n] for n in WEIGHTS], *[out_v[n] for n in WEIGHTS])
```

```python
import functools
import math

import jax
import jax.numpy as jnp
from jax import lax
from jax.experimental import pallas as pl
from jax.experimental.pallas import tpu as pltpu

F32 = jnp.float32
BF16 = jnp.bfloat16
HI = lax.Precision.HIGHEST
MESH = pl.DeviceIdType.MESH

N_DEV = 8
RMS_EPS = 1e-6
S5_GROUP = 16
S5_CHUNK = 16
S5_OCTET = 128 // S5_GROUP
S5_REGROUP_ROWS = 32
SB_HEAD_DIM = 64
SB_UNDERFLOW = -104.0
SB_HEADS_PER_STEP = 2
LRU_CONV = 4
LRU_C = 8.0
ADAM_LR, ADAM_B1, ADAM_B2, ADAM_EPS, ADAM_WD, ADAM_STEP = 0.001, 0.9, 0.999, 1e-08, 0.01, 10
VMEM_LIMIT_BYTES = 56 * 1024 * 1024
SUBLANES = 8
PACK_ROWS = 256

NN = (((1,), (0,)), ((), ()))
NT = (((1,), (1,)), ((), ()))
TN = (((0,), (0,)), ((), ()))

SHARD_AXIS = dict(
    ffn1_w_in=2, ffn1_w_out=1, ffn2_w_in=2, ffn2_w_out=1, s5_w_in=1, s5_d=1, s5_w_out=2, sb_w_qkv=2, sb_w_out=1,
    lru_w_in=2, lru_conv_w=2, lru_conv_b=1, lru_w_a=2, lru_b_a=2, lru_w_x=2, lru_b_x=2, lru_lambda=1, lru_w_out=1)
MIXER_BIG = ("s5_w_in", "s5_w_out", "sb_w_qkv", "sb_w_out", "lru_w_in", "lru_w_a", "lru_w_x", "lru_w_out")
SMALL_SHARDED = ("s5_d", "lru_conv_w", "lru_conv_b", "lru_b_a", "lru_b_x", "lru_lambda")
REPLICATED = ("ffn1_norm", "mix_norm", "ffn2_norm", "final_norm", "s5_lam_re", "s5_lam_im", "s5_log_dt",
              "s5_b_re", "s5_b_im", "s5_c_re", "s5_c_im")
WEIGHTS = ("ffn1_norm", "ffn1_w_in", "ffn1_w_out", "mix_norm", "ffn2_norm", "ffn2_w_in", "ffn2_w_out", "final_norm",
           "s5_w_in", "s5_lam_re", "s5_lam_im", "s5_log_dt", "s5_b_re", "s5_b_im", "s5_c_re", "s5_c_im", "s5_d",
           "s5_w_out", "sb_w_qkv", "sb_w_out", "lru_w_in", "lru_conv_w", "lru_conv_b", "lru_w_a", "lru_b_a",
           "lru_w_x", "lru_b_x", "lru_lambda", "lru_w_out")


def _dot(a, b, dims=NN, prec=None):
    return lax.dot_general(a, b, dims, precision=prec, preferred_element_type=F32)


def _params(*sem):
    return pltpu.CompilerParams(dimension_semantics=sem, vmem_limit_bytes=VMEM_LIMIT_BYTES)


def _tile(n, pref):
    return min(pref, n)


def _sigmoid(x):
    return jax.nn.sigmoid(x)


def _softplus(x):
    return jnp.maximum(x, 0.0) + jnp.log(1.0 + jnp.exp(-jnp.abs(x)))


_GELU_C = math.sqrt(2.0 / math.pi)


def _gelu(x):
    return 0.5 * x * (1.0 + jnp.tanh(_GELU_C * (x + 0.044715 * x * x * x)))


def _gelu_grad(x):
    t = jnp.tanh(_GELU_C * (x + 0.044715 * x * x * x))
    return 0.5 * (1.0 + t) + 0.5 * x * (1.0 - t * t) * _GELU_C * (1.0 + 3.0 * 0.044715 * x * x)


def _rms(x):
    r = lax.rsqrt(jnp.mean(x * x, axis=1, keepdims=True) + RMS_EPS)
    return r, x * r


def _rms_bwd(dhn, xhat, r, g):
    dxhat = dhn * g
    return r * (dxhat - xhat * jnp.mean(dxhat * xhat, axis=1, keepdims=True))


def _one_minus_a2_sqrt(log_a):
    t = jnp.tanh(log_a)
    return jnp.sqrt(-2.0 * t / (1.0 - t))


def _shift_down(cur, prev8, k, first):
    if k == 0:
        return cur
    row8 = lax.broadcasted_iota(jnp.int32, prev8.shape, 0)
    rolled = pltpu.roll(cur, k, 0)
    edge = jnp.where(first, 0.0, pltpu.roll(prev8, k, 0))
    top = jnp.where(row8 < k, edge, rolled[0:SUBLANES])
    return jnp.concatenate([top, rolled[SUBLANES:]], axis=0)


def _shift_up(cur, next8, k, last):
    if k == 0:
        return cur
    tm = cur.shape[0]
    row8 = lax.broadcasted_iota(jnp.int32, next8.shape, 0)
    rolled = pltpu.roll(cur, tm - k, 0)
    edge = jnp.where(last, 0.0, pltpu.roll(next8, SUBLANES - k, 0))
    bottom = jnp.where(row8 >= SUBLANES - k, edge, rolled[tm - SUBLANES:tm])
    return jnp.concatenate([rolled[:tm - SUBLANES], bottom], axis=0)


def _rowwise(name, fn, ins, out_tiled, out_acc, n_rows, tm, reverse=False):
    nt = n_rows // tm
    per8 = tm // SUBLANES
    n8 = n_rows // SUBLANES
    n_in, n_ot = len(ins), len(out_tiled)

    def pos(i):
        return nt - 1 - i if reverse else i

    in_specs, args = [], []
    for spec in ins:
        kind, arr = spec[0], spec[1]
        args.append(arr)
        if kind == 'b':
            in_specs.append(pl.BlockSpec(arr.shape, lambda i, nd=arr.ndim: (0,) * nd))
        elif kind == 't':
            in_specs.append(pl.BlockSpec((tm, spec[2]), lambda i, cb=spec[3]: (pos(i), cb)))
        elif kind == 'p':
            in_specs.append(pl.BlockSpec((SUBLANES, spec[2]),
                                         lambda i, cb=spec[3]: (jnp.maximum(pos(i) * per8 - 1, 0), cb)))
        else:
            in_specs.append(pl.BlockSpec((SUBLANES, spec[2]),
                                         lambda i, cb=spec[3]: (jnp.minimum((pos(i) + 1) * per8, n8 - 1), cb)))

    def body(*refs):
        i = pl.program_id(0)
        vals = [r[...] for r in refs[:n_in]]
        outs = refs[n_in:]
        touts, aouts = fn(pos(i), nt, *vals)
        for r, v in zip(outs[:n_ot], touts):
            r[...] = v.astype(r.dtype)
        if out_acc:
            @pl.when(i == 0)
            def _():
                for r in outs[n_ot:]:
                    r[...] = jnp.zeros(r.shape, r.dtype)
            for r, v in zip(outs[n_ot:], aouts):
                r[...] += v

    out_specs = [pl.BlockSpec((tm, n), lambda i: (pos(i), 0)) for n, _ in out_tiled]
    out_specs += [pl.BlockSpec((r, n), lambda i: (0, 0)) for r, n in out_acc]
    out_shape = [jax.ShapeDtypeStruct((n_rows, n), dt) for n, dt in out_tiled]
    out_shape += [jax.ShapeDtypeStruct((r, n), F32) for r, n in out_acc]
    return pl.pallas_call(body, grid=(nt,), in_specs=in_specs, out_specs=out_specs, out_shape=out_shape, name=name,
                          compiler_params=_params("arbitrary"))(*args)


def _all_gather(name, block):
    def body(x_ref, out_ref, send_sems, recv_sems, local_sem):
        x, y, c = lax.axis_index("x"), lax.axis_index("y"), lax.axis_index("c")
        me, sibling = (x, y, c), (x, y, 1 - c)
        chips = [(1 - x, y), (x, 1 - y), (1 - x, 1 - y)]

        def rows(px, py, pc):
            return out_ref.at[4 * px + 2 * py + pc]

        def copy(k, blk, to, src=None):
            return pltpu.make_async_remote_copy(
                src_ref=rows(*blk) if src is None else src, dst_ref=rows(*blk),
                send_sem=send_sems.at[k], recv_sem=recv_sems.at[k], device_id=to, device_id_type=MESH)

        mine = pltpu.make_async_copy(x_ref, rows(*me), local_sem)
        mine.start()
        first = [copy(0, me, sibling, src=x_ref)]
        first += [copy(1 + j, me, (*chip, c), src=x_ref) for j, chip in enumerate(chips)]
        for cp in first:
            cp.start()
        passed = [copy(4 + j, (*chip, c), sibling) for j, chip in enumerate(chips)]
        for j, chip in enumerate(chips):
            copy(1 + j, (*chip, c), me).wait_recv()
            passed[j].start()
        copy(0, sibling, me).wait_recv()
        for j, chip in enumerate(chips):
            copy(4 + j, (*chip, 1 - c), me).wait_recv()
        for cp in first + passed:
            cp.wait_send()
        mine.wait()

    return pl.pallas_call(
        body, name=name, out_shape=jax.ShapeDtypeStruct((N_DEV,) + block.shape, block.dtype),
        in_specs=[pl.BlockSpec(memory_space=pl.ANY)], out_specs=pl.BlockSpec(memory_space=pl.ANY),
        scratch_shapes=[pltpu.SemaphoreType.DMA((7,)), pltpu.SemaphoreType.DMA((7,)), pltpu.SemaphoreType.DMA(())],
    )(block)


def _exchange(name, send):
    def body(s_ref, r_ref, send_sems, recv_sems, local_sem):
        x, y, c = lax.axis_index("x"), lax.axis_index("y"), lax.axis_index("c")
        me = 4 * x + 2 * y + c
        mine = pltpu.make_async_copy(s_ref.at[me], r_ref.at[me], local_sem)
        mine.start()
        copies = []
        for k in range(1, N_DEV):
            dx, dy, dc = (k >> 2) & 1, (k >> 1) & 1, k & 1
            px = 1 - x if dx else x
            py = 1 - y if dy else y
            pc = 1 - c if dc else c
            peer = 4 * px + 2 * py + pc
            copies.append((pltpu.make_async_remote_copy(
                src_ref=s_ref.at[peer], dst_ref=r_ref.at[me], send_sem=send_sems.at[k - 1],
                recv_sem=recv_sems.at[k - 1], device_id=(px, py, pc), device_id_type=MESH), peer))
        for cp, _ in copies:
            cp.start()
        for k, (cp, peer) in enumerate(copies):
            pltpu.make_async_remote_copy(
                src_ref=s_ref.at[peer], dst_ref=r_ref.at[peer], send_sem=send_sems.at[k], recv_sem=recv_sems.at[k],
                device_id=(x, y, c), device_id_type=MESH).wait_recv()
        for cp, _ in copies:
            cp.wait_send()
        mine.wait()

    return pl.pallas_call(
        body, name=name, out_shape=jax.ShapeDtypeStruct(send.shape, send.dtype),
        in_specs=[pl.BlockSpec(memory_space=pl.ANY)], out_specs=pl.BlockSpec(memory_space=pl.ANY),
        scratch_shapes=[pltpu.SemaphoreType.DMA((7,)), pltpu.SemaphoreType.DMA((7,)), pltpu.SemaphoreType.DMA(())],
    )(send)


def _unshard(gathered, axis):
    local = gathered.shape[1:]
    full = jnp.moveaxis(gathered, 0, axis)
    return full.reshape(local[:axis] + (N_DEV * local[axis],) + local[axis + 1:])


def _shard_blocks(full, axis):
    s = full.shape
    cut = full.reshape(s[:axis] + (N_DEV, s[axis] // N_DEV) + s[axis + 1:])
    return jnp.moveaxis(cut, axis, 0)


def _mm_fwd(name, a, w, out_dtype, gain=None, resid=None, glu=False):
    n_rows, k = a.shape
    n = w.shape[1]
    tm = _tile(n_rows, 512)
    n_out = n // 2 if glu else n

    def body(*refs):
        it = iter(refs)
        a_ref, w_ref = next(it), next(it)
        g_ref = next(it) if gain is not None else None
        r_ref = next(it) if resid is not None else None
        outs = list(it)
        av = a_ref[...]
        if g_ref is not None:
            _, xhat = _rms(av)
            av = xhat * g_ref[...]
        res = _dot(av.astype(BF16), w_ref[...])
        if glu:
            outs[1][...] = res.astype(outs[1].dtype)
            res = res[:, :n_out] * _sigmoid(res[:, n_out:])
        if r_ref is not None:
            res = res + r_ref[...]
        outs[0][...] = res.astype(outs[0].dtype)

    args = [a, w]
    in_specs = [pl.BlockSpec((tm, k), lambda i: (i, 0)), pl.BlockSpec((k, n), lambda i: (0, 0))]
    if gain is not None:
        args.append(gain)
        in_specs.append(pl.BlockSpec((1, k), lambda i: (0, 0)))
    if resid is not None:
        args.append(resid)
        in_specs.append(pl.BlockSpec((tm, n_out), lambda i: (i, 0)))
    out_shape = [jax.ShapeDtypeStruct((n_rows, n_out), out_dtype)]
    out_specs = [pl.BlockSpec((tm, n_out), lambda i: (i, 0))]
    if glu:
        out_shape.append(jax.ShapeDtypeStruct((n_rows, n), F32))
        out_specs.append(pl.BlockSpec((tm, n), lambda i: (i, 0)))
    return pl.pallas_call(body, grid=(n_rows // tm,), in_specs=in_specs, out_specs=out_specs, out_shape=out_shape,
                          name=name, compiler_params=_params("parallel"))(*args)


def _mm_bwd(name, a, d, w, gain=None, dres=None):
    n_rows, k = a.shape
    n = w.shape[1]
    tm = _tile(n_rows, 512)

    def body(*refs):
        it = iter(refs)
        a_ref, d_ref, w_ref = next(it), next(it), next(it)
        g_ref = next(it) if gain is not None else None
        r_ref = next(it) if gain is not None else None
        da_ref, dw_ref = next(it), next(it)
        dg_ref = next(it) if gain is not None else None
        i = pl.program_id(0)

        @pl.when(i == 0)
        def _():
            dw_ref[...] = jnp.zeros(dw_ref.shape, F32)
            if dg_ref is not None:
                dg_ref[...] = jnp.zeros(dg_ref.shape, F32)

        av = a_ref[...]
        dv = d_ref[...].astype(BF16)
        if g_ref is not None:
            r, xhat = _rms(av)
            ab = (xhat * g_ref[...]).astype(BF16)
        else:
            ab = av.astype(BF16)
        dw_ref[...] += _dot(ab, dv, TN)
        da = _dot(dv, w_ref[...], NT)
        if g_ref is not None:
            dg_ref[...] += jnp.sum(da * xhat, axis=0, keepdims=True)
            da = r_ref[...] + _rms_bwd(da, xhat, r, g_ref[...])
        da_ref[...] = da.astype(da_ref.dtype)

    args = [a, d, w]
    in_specs = [pl.BlockSpec((tm, k), lambda i: (i, 0)), pl.BlockSpec((tm, n), lambda i: (i, 0)),
                pl.BlockSpec((k, n), lambda i: (0, 0))]
    out_shape = [jax.ShapeDtypeStruct((n_rows, k), F32), jax.ShapeDtypeStruct((k, n), F32)]
    out_specs = [pl.BlockSpec((tm, k), lambda i: (i, 0)), pl.BlockSpec((k, n), lambda i: (0, 0))]
    if gain is not None:
        args += [gain, dres]
        in_specs += [pl.BlockSpec((1, k), lambda i: (0, 0)), pl.BlockSpec((tm, k), lambda i: (i, 0))]
        out_shape.append(jax.ShapeDtypeStruct((1, k), F32))
        out_specs.append(pl.BlockSpec((1, k), lambda i: (0, 0)))
    return pl.pallas_call(body, grid=(n_rows // tm,), in_specs=in_specs, out_specs=out_specs, out_shape=out_shape,
                          name=name, compiler_params=_params("arbitrary"))(*args)


def _ffn_fwd(name, x, gain, wi, wo, layer):
    n_rows, dm = x.shape
    _, nj, _, _, fb = wi.shape
    tm = _tile(n_rows, 512)

    def body(x_ref, g_ref, wi_ref, wo_ref, y_ref):
        xv = x_ref[...]
        _, xhat = _rms(xv)
        hn = (xhat * g_ref[...]).astype(BF16)
        acc = jnp.zeros((tm, dm), F32)
        for j in range(nj):
            gate = _dot(hn, wi_ref[0, j])
            up = _dot(hn, wi_ref[1, j])
            act = (gate * _sigmoid(gate) * up).astype(BF16)
            acc = acc + _dot(act, wo_ref[j].reshape(fb, dm))
        y_ref[...] = xv + 0.5 * acc

    return pl.pallas_call(
        body, grid=(n_rows // tm,), name=name,
        in_specs=[pl.BlockSpec((tm, dm), lambda i: (i, 0)), pl.BlockSpec((1, dm), lambda i: (0, 0)),
                  pl.BlockSpec((2, nj, None, dm, fb), lambda i: (0, 0, layer, 0, 0)),
                  pl.BlockSpec((nj, 2, None, fb // 2, dm), lambda i: (0, 0, layer, 0, 0))],
        out_specs=pl.BlockSpec((tm, dm), lambda i: (i, 0)),
        out_shape=jax.ShapeDtypeStruct((n_rows, dm), F32), compiler_params=_params("parallel"))(x, gain, wi, wo)


def _ffn_bwd_block(name, x, dy, gain, wi, wo, layer, j, acc):
    n_rows, dm = x.shape
    _, nj, _, _, fb = wi.shape
    tm = _tile(n_rows, 512)
    last = j == nj - 1

    def body(*refs):
        it = iter(refs)
        x_ref, dy_ref, g_ref, wi_ref, wo_ref = next(it), next(it), next(it), next(it), next(it)
        acc_ref = next(it) if acc is not None else None
        out_ref, dwi_ref, dwo_ref = next(it), next(it), next(it)
        dg_ref = next(it) if last else None
        i = pl.program_id(0)

        @pl.when(i == 0)
        def _():
            dwi_ref[...] = jnp.zeros(dwi_ref.shape, F32)
            dwo_ref[...] = jnp.zeros(dwo_ref.shape, F32)
            if last:
                dg_ref[...] = jnp.zeros(dg_ref.shape, F32)

        xv, dyv, g = x_ref[...], dy_ref[...], g_ref[...]
        r, xhat = _rms(xv)
        hn = (xhat * g).astype(BF16)
        wg, wu, wob = wi_ref[0], wi_ref[1], wo_ref[...].reshape(fb, dm)
        gate = _dot(hn, wg)
        up = _dot(hn, wu)
        s = _sigmoid(gate)
        silu = gate * s
        act = (silu * up).astype(BF16)
        dout = (0.5 * dyv).astype(BF16)
        dact = _dot(dout, wob, NT)
        dwo_ref[...] += _dot(act, dout, TN)
        dgate = (dact * up * (s * (1.0 + gate * (1.0 - s)))).astype(BF16)
        dup = (dact * silu).astype(BF16)
        dwi_ref[0] += _dot(hn, dgate, TN)
        dwi_ref[1] += _dot(hn, dup, TN)
        tot = _dot(dgate, wg, NT) + _dot(dup, wu, NT)
        if acc_ref is not None:
            tot = tot + acc_ref[...]
        if last:
            out_ref[...] = dyv + _rms_bwd(tot, xhat, r, g)
            dg_ref[...] += jnp.sum(tot * xhat, axis=0, keepdims=True)
        else:
            out_ref[...] = tot

    tok = pl.BlockSpec((tm, dm), lambda i: (i, 0))
    args = [x, dy, gain, wi, wo]
    in_specs = [tok, tok, pl.BlockSpec((1, dm), lambda i: (0, 0)),
                pl.BlockSpec((2, None, None, dm, fb), lambda i: (0, j, layer, 0, 0)),
                pl.BlockSpec((None, 2, None, fb // 2, dm), lambda i: (j, 0, layer, 0, 0))]
    if acc is not None:
        args.append(acc)
        in_specs.append(tok)
    out_specs = [tok, pl.BlockSpec((2, dm, fb), lambda i: (0, 0, 0)), pl.BlockSpec((fb, dm), lambda i: (0, 0))]
    out_shape = [jax.ShapeDtypeStruct((n_rows, dm), F32), jax.ShapeDtypeStruct((2, dm, fb), F32),
                 jax.ShapeDtypeStruct((fb, dm), F32)]
    if last:
        out_specs.append(pl.BlockSpec((1, dm), lambda i: (0, 0)))
        out_shape.append(jax.ShapeDtypeStruct((1, dm), F32))
    return pl.pallas_call(body, grid=(n_rows // tm,), name=name, in_specs=in_specs, out_specs=out_specs,
                          out_shape=out_shape, compiler_params=_params("arbitrary"))(*args)


def _ffn_bwd(name, x, dy, gain, wi, wo, layer):
    nj = wi.shape[1]
    acc, dwi, dwo = None, [], []
    for j in range(nj):
        res = _ffn_bwd_block("%s_%d" % (name, j), x, dy, gain, wi, wo, layer, j, acc)
        acc = res[0]
        dwi.append(res[1])
        dwo.append(res[2])
    return acc, jnp.stack(dwi, axis=1), jnp.stack(dwo, axis=0), res[3]


def _scan8(a, x, reverse):
    row = lax.broadcasted_iota(jnp.int32, a.shape, 0)
    for k in (1, 2, 4):
        if reverse:
            keep = row < SUBLANES - k
            a_s, x_s = pltpu.roll(a, SUBLANES - k, 0), pltpu.roll(x, SUBLANES - k, 0)
        else:
            keep = row >= k
            a_s, x_s = pltpu.roll(a, k, 0), pltpu.roll(x, k, 0)
        x = a * jnp.where(keep, x_s, 0.0) + x
        a = a * jnp.where(keep, a_s, 1.0)
    return a, x


def _scan_tile(a_ref, x_ref, h_ref, carry, reverse, rows):
    groups = rows // SUBLANES

    def step(n, c):
        gidx = groups - 1 - n if reverse else n
        sl = pl.ds(pl.multiple_of(gidx * SUBLANES, SUBLANES), SUBLANES)
        a_cum, h0 = _scan8(a_ref[sl, :], x_ref[sl, :], reverse)
        h = a_cum * c + h0
        h_ref[sl, :] = h
        return h[0:1] if reverse else h[SUBLANES - 1:SUBLANES]

    return lax.fori_loop(0, groups, step, carry)


def _s5_mats(lam_re, lam_im, log_dt, b_re, b_im, c_re, c_im):
    lc = S5_CHUNK
    groups, p = lam_re.shape
    h = b_re.shape[-1]
    lam = lax.complex(lam_re, lam_im)
    lam_dt = lam * jnp.exp(log_dt)[:, None]
    lam_bar = jnp.exp(lam_dt)
    b_bar = ((lam_bar - 1.0) / lam)[:, :, None] * lax.complex(b_re, b_im)
    c = lax.complex(c_re, c_im)
    pw = jnp.exp(lam_dt[None] * jnp.arange(lc + 1, dtype=F32)[:, None, None])
    resp = jnp.einsum('ghp,tgp,gpk->tghk', c, pw[:lc], b_bar, precision=HI).real
    s_idx = jnp.arange(lc)[:, None]
    u_idx = jnp.arange(lc)[None, :]
    onehot = (jnp.arange(lc)[:, None, None] == (u_idx - s_idx)[None]).astype(F32)
    m = jnp.einsum('tghk,tsu->gskuh', resp, onehot, precision=HI).reshape(groups, lc * h, lc * h)
    w = pw[lc - 1::-1][:lc].transpose(1, 0, 2)[:, :, None, :] * b_bar.transpose(0, 2, 1)[:, None]
    bm = jnp.concatenate([w.real, w.imag], axis=-1).reshape(groups, lc * h, 2 * p)
    v = c[:, None] * pw[1:lc + 1].transpose(1, 0, 2)[:, :, None, :]
    v = v.transpose(0, 3, 1, 2)
    cm = jnp.concatenate([v.real, -v.imag], axis=1).reshape(groups, 2 * p, lc * h)
    a = jnp.concatenate([pw[lc].real, pw[lc].imag], axis=-1)
    return m, bm, cm, a


def _s5_powers(lam_re, lam_im, log_dt):
    lam_dt = lax.complex(lam_re, lam_im) * jnp.exp(log_dt)[:, None]
    pw = jnp.exp(lam_dt[None] * (S5_CHUNK * jnp.arange(1, 9, dtype=F32))[:, None, None])

    def c1(z):
        return jnp.concatenate([z.real, z.real], axis=-1).reshape(z.shape[0], -1)

    def c2(z):
        return jnp.concatenate([-z.imag, z.imag], axis=-1).reshape(z.shape[0], -1)

    p1, p2 = c1(pw), c2(pw)
    apw = jnp.stack([p1[0], p2[0], p1[1], p2[1], p1[3], p2[3], jnp.zeros_like(p1[0]), jnp.zeros_like(p1[0])])
    fwd = jnp.concatenate([p1, p2], axis=0)
    rev = jnp.concatenate([c1(pw[::-1]), c2(pw[::-1])], axis=0)
    return apw, fwd, rev


def _cmul(c1, c2, x, half, conj=False):
    sw = pltpu.roll(x, half, 1)
    return c1 * x - c2 * sw if conj else c1 * x + c2 * sw


def _gather_groups(u_ref, ug_ref, nc):
    h = S5_GROUP
    rows = min(S5_REGROUP_ROWS, nc)

    def step(r, _):
        base = pl.multiple_of(r * rows, rows)
        for t in range(S5_CHUNK):
            val = u_ref[pl.ds(base * S5_CHUNK + t, rows, stride=S5_CHUNK), :]
            for g in range(S5_OCTET):
                ug_ref[g, pl.ds(base, rows), t * h:(t + 1) * h] = val[:, g * h:(g + 1) * h]
        return 0

    lax.fori_loop(0, nc // rows, step, 0)


def _scatter_groups(yg_ref, y_ref, nc):
    h = S5_GROUP
    rows = min(S5_REGROUP_ROWS, nc)

    def step(r, _):
        base = pl.multiple_of(r * rows, rows)
        for t in range(S5_CHUNK):
            y_ref[pl.ds(base * S5_CHUNK + t, rows, stride=S5_CHUNK), :] = jnp.concatenate(
                [yg_ref[g, pl.ds(base, rows), t * h:(t + 1) * h] for g in range(S5_OCTET)], axis=1)
        return 0

    lax.fori_loop(0, nc // rows, step, 0)


def _s5_fwd(name, u, m, bm, cm, apw, arows):
    n_rows, width = u.shape
    nc = n_rows // S5_CHUNK
    groups, lh, _ = m.shape
    p2 = bm.shape[2]
    gb = S5_OCTET
    lanes = gb * S5_GROUP

    def body(u_ref, m_ref, b_ref, c_ref, apw_ref, ar_ref, y_ref, sp_ref, ug_ref, yg_ref, xs_ref, cin_ref):
        _gather_groups(u_ref, ug_ref, nc)
        for gi in range(gb):
            xs_ref[:, gi * p2:(gi + 1) * p2] = _dot(ug_ref[gi], b_ref[gi], prec=HI)
        row = lax.broadcasted_iota(jnp.int32, (SUBLANES, p2), 0)
        half = p2 // 2
        lns = [slice(gi * p2, (gi + 1) * p2) for gi in range(gb)]

        def within(n, _):
            sl = pl.ds(pl.multiple_of(n * SUBLANES, SUBLANES), SUBLANES)
            for ln in lns:
                x = xs_ref[sl, ln]
                for q, k in enumerate((1, 2, 4)):
                    xs = jnp.where(row >= k, pltpu.roll(x, k, 0), 0.0)
                    x = x + _cmul(apw_ref[2 * q:2 * q + 1, ln], apw_ref[2 * q + 1:2 * q + 2, ln], xs, half)
                xs_ref[sl, ln] = x
            return 0

        def across(n, carry):
            new = []
            for ln, (c, cs) in zip(lns, carry):
                cin_ref[pl.ds(pl.multiple_of(n * SUBLANES, SUBLANES), SUBLANES), ln] = jnp.broadcast_to(c, (SUBLANES, p2))
                x7 = xs_ref[pl.ds(pl.multiple_of(n * SUBLANES, SUBLANES), SUBLANES), ln][SUBLANES - 1:SUBLANES]
                c1, c2 = ar_ref[7:8, ln], ar_ref[15:16, ln]
                new.append((c1 * c + c2 * cs + x7, c1 * cs - c2 * c + pltpu.roll(x7, half, 1)))
            return tuple(new)

        def apply(n, _):
            sl = pl.ds(pl.multiple_of(n * SUBLANES, SUBLANES), SUBLANES)
            for ln in lns:
                cb = cin_ref[sl, ln]
                s8 = xs_ref[sl, ln] + _cmul(ar_ref[0:8, ln], ar_ref[8:16, ln], cb, half)
                sp_ref[sl, ln] = jnp.where(row >= 1, pltpu.roll(s8, 1, 0), cb)
            return 0

        ngroups = nc // SUBLANES
        zero = jnp.zeros((1, p2), F32)
        lax.fori_loop(0, ngroups, within, 0)
        lax.fori_loop(0, ngroups, across, tuple((zero, zero) for _ in range(gb)))
        lax.fori_loop(0, ngroups, apply, 0)
        for gi in range(gb):
            yg_ref[gi] = (_dot(ug_ref[gi], m_ref[gi], prec=HI)
                          + _dot(sp_ref[:, gi * p2:(gi + 1) * p2], c_ref[gi], prec=HI))
        _scatter_groups(yg_ref, y_ref, nc)

    tok = pl.BlockSpec((n_rows, lanes), lambda g: (0, g), pipeline_mode=pl.Buffered(1))
    return pl.pallas_call(
        body, grid=(groups // gb,), name=name,
        in_specs=[tok, pl.BlockSpec((gb, lh, lh), lambda g: (g, 0, 0)),
                  pl.BlockSpec((gb, lh, p2), lambda g: (g, 0, 0)), pl.BlockSpec((gb, p2, lh), lambda g: (g, 0, 0)),
                  pl.BlockSpec((8, gb * p2), lambda g: (0, g)), pl.BlockSpec((16, gb * p2), lambda g: (0, g))],
        out_specs=[tok, pl.BlockSpec((nc, gb * p2), lambda g: (0, g))],
        out_shape=[jax.ShapeDtypeStruct((n_rows, width), F32), jax.ShapeDtypeStruct((nc, groups * p2), F32)],
        scratch_shapes=[pltpu.VMEM((gb, nc, lh), F32), pltpu.VMEM((gb, nc, lh), F32), pltpu.VMEM((nc, gb * p2), F32),
                        pltpu.VMEM((nc, gb * p2), F32)],
        compiler_params=_params("parallel"),
    )(u, m, bm, cm, apw, arows)


def _s5_bwd(name, u, dy, sprev, m, bm, cm, apw, arows_rev):
    n_rows, width = u.shape
    nc = n_rows // S5_CHUNK
    groups, lh, _ = m.shape
    p2 = bm.shape[2]
    half = p2 // 2
    gb = S5_OCTET
    lanes = gb * S5_GROUP

    def body(u_ref, dy_ref, sp_ref, m_ref, b_ref, c_ref, apw_ref, ar_ref,
             du_ref, dm_ref, db_ref, dc_ref, da_ref, ug_ref, dyg_ref, ds_ref, gx_ref, cin_ref):
        _gather_groups(u_ref, ug_ref, nc)
        _gather_groups(dy_ref, dyg_ref, nc)
        for gi in range(gb):
            ds_ref[:, gi * p2:(gi + 1) * p2] = _dot(dyg_ref[gi], c_ref[gi], NT, prec=HI)
        row = lax.broadcasted_iota(jnp.int32, (SUBLANES, p2), 0)
        lane = lax.broadcasted_iota(jnp.int32, (SUBLANES, p2), 1)
        ngroups = nc // SUBLANES
        lns = [slice(gi * p2, (gi + 1) * p2) for gi in range(gb)]

        def within(n, _):
            sl = pl.ds(pl.multiple_of(n * SUBLANES, SUBLANES), SUBLANES)
            nxt_sl = pl.ds(pl.multiple_of(jnp.minimum(n + 1, ngroups - 1) * SUBLANES, SUBLANES), SUBLANES)
            for ln in lns:
                d8 = ds_ref[sl, ln]
                nxt = jnp.where(n == ngroups - 1, 0.0, ds_ref[nxt_sl, ln][0:1])
                x = jnp.where(row < SUBLANES - 1, pltpu.roll(d8, SUBLANES - 1, 0),
                              jnp.broadcast_to(nxt, (SUBLANES, p2)))
                for q, k in enumerate((1, 2, 4)):
                    xs = jnp.where(row < SUBLANES - k, pltpu.roll(x, SUBLANES - k, 0), 0.0)
                    x = x + _cmul(apw_ref[2 * q:2 * q + 1, ln], apw_ref[2 * q + 1:2 * q + 2, ln], xs, half, conj=True)
                gx_ref[sl, ln] = x
            return 0

        def across(i, carry):
            n = ngroups - 1 - i
            new = []
            for ln, (c, cs) in zip(lns, carry):
                cin_ref[pl.ds(pl.multiple_of(n * SUBLANES, SUBLANES), SUBLANES), ln] = jnp.broadcast_to(c, (SUBLANES, p2))
                x0 = gx_ref[pl.ds(pl.multiple_of(n * SUBLANES, SUBLANES), SUBLANES), ln][0:1]
                c1, c2 = ar_ref[0:1, ln], ar_ref[8:9, ln]
                new.append((c1 * c - c2 * cs + x0, c1 * cs + c2 * c + pltpu.roll(x0, half, 1)))
            return tuple(new)

        def apply(n, dacc):
            sl = pl.ds(pl.multiple_of(n * SUBLANES, SUBLANES), SUBLANES)
            new_d = []
            for ln, acc in zip(lns, dacc):
                cb = cin_ref[sl, ln]
                g8 = gx_ref[sl, ln] + _cmul(ar_ref[0:8, ln], ar_ref[8:16, ln], cb, half, conj=True)
                gx_ref[sl, ln] = g8
                s8 = sp_ref[sl, ln]
                p1 = g8 * s8
                pq = g8 * pltpu.roll(s8, half, 1)
                d_a = jnp.where(lane < half, p1 + pltpu.roll(p1, half, 1), pq - pltpu.roll(pq, half, 1))
                new_d.append(acc + jnp.sum(d_a, axis=0, keepdims=True))
            return tuple(new_d)

        zero = jnp.zeros((1, p2), F32)
        lax.fori_loop(0, ngroups, within, 0)
        lax.fori_loop(0, ngroups, across, tuple((zero, zero) for _ in range(gb)))
        dacc = lax.fori_loop(0, ngroups, apply, tuple(zero for _ in range(gb)))
        for gi in range(gb):
            ln = slice(gi * p2, (gi + 1) * p2)
            da_ref[:, ln] = dacc[gi]
            ug, dyg, gxg = ug_ref[gi], dyg_ref[gi], gx_ref[:, ln]
            dm_ref[gi] = _dot(ug, dyg, TN, prec=HI)
            dc_ref[gi] = _dot(sp_ref[:, ln], dyg, TN, prec=HI)
            db_ref[gi] = _dot(ug, gxg, TN, prec=HI)
            dyg_ref[gi] = _dot(dyg, m_ref[gi], NT, prec=HI) + _dot(gxg, b_ref[gi], NT, prec=HI)
        _scatter_groups(dyg_ref, du_ref, nc)

    tok = pl.BlockSpec((n_rows, lanes), lambda g: (0, g), pipeline_mode=pl.Buffered(1))
    tok_s = pl.BlockSpec((nc, gb * p2), lambda g: (0, g))
    mat_m = pl.BlockSpec((gb, lh, lh), lambda g: (g, 0, 0))
    mat_b = pl.BlockSpec((gb, lh, p2), lambda g: (g, 0, 0))
    mat_c = pl.BlockSpec((gb, p2, lh), lambda g: (g, 0, 0))
    return pl.pallas_call(
        body, grid=(groups // gb,), name=name,
        in_specs=[tok, tok, tok_s, mat_m, mat_b, mat_c,
                  pl.BlockSpec((8, gb * p2), lambda g: (0, g)), pl.BlockSpec((16, gb * p2), lambda g: (0, g))],
        out_specs=[tok, mat_m, mat_b, mat_c, pl.BlockSpec((1, gb * p2), lambda g: (0, g))],
        out_shape=[jax.ShapeDtypeStruct((n_rows, width), F32), jax.ShapeDtypeStruct(m.shape, F32),
                   jax.ShapeDtypeStruct(bm.shape, F32), jax.ShapeDtypeStruct(cm.shape, F32),
                   jax.ShapeDtypeStruct((1, groups * p2), F32)],
        scratch_shapes=[pltpu.VMEM((gb, nc, lh), F32), pltpu.VMEM((gb, nc, lh), F32),
                        pltpu.VMEM((nc, gb * p2), F32), pltpu.VMEM((nc, gb * p2), F32),
                        pltpu.VMEM((nc, gb * p2), F32)],
        compiler_params=_params("parallel"),
    )(u, dy, sprev, m, bm, cm, apw, arows_rev)


def _split_dot(v, tri):
    hi = v.astype(BF16)
    lo = (v - hi.astype(F32)).astype(BF16)
    return _dot(hi, tri) + _dot(lo, tri)


def _sb_block(qb, kblk, causal):
    z = _dot(qb, kblk, NT)
    sp = _softplus(z)
    lk = -sp
    if causal is not None:
        lk = jnp.where(causal, lk, 0.0)
    return lk, z - sp


def _sb_more(kb, carries):
    top = jnp.max(carries[0])
    for c in carries[1:]:
        top = jnp.maximum(top, jnp.max(c))
    return (kb >= 0) & (top > SB_UNDERFLOW)


def _sb_fwd(name, q, k, v):
    heads, n_rows, hd = q.shape
    tq = _tile(n_rows // 2, 256)
    hb = SB_HEADS_PER_STEP
    scale = hd ** -0.5

    def body(q_ref, k_ref, v_ref, o_ref):
        qi = pl.program_id(1)
        qb = [q_ref[h] * scale for h in range(hb)]
        row = lax.broadcasted_iota(jnp.int32, (tq, tq), 0)
        col = lax.broadcasted_iota(jnp.int32, (tq, tq), 1)
        tri = (row > col).astype(BF16)

        def block(kb, carries, accs, causal):
            ks = pl.ds(pl.multiple_of(kb * tq, tq), tq)
            new_c, new_a = [], []
            for h in range(hb):
                lk, lb = _sb_block(qb[h], k_ref[h, ks, :], causal)
                a = jnp.exp(lb + _split_dot(lk, tri) + carries[h])
                if causal is not None:
                    a = jnp.where(causal, a, 0.0)
                new_a.append(accs[h] + _dot(a.astype(BF16), v_ref[h, ks, :]))
                new_c.append(carries[h] + jnp.sum(lk, axis=1, keepdims=True))
            return tuple(new_c), tuple(new_a)

        zc = tuple(jnp.zeros((tq, 1), F32) for _ in range(hb))
        za = tuple(jnp.zeros((tq, hd), F32) for _ in range(hb))
        carries, accs = block(qi, zc, za, col < row)
        _, _, accs = lax.while_loop(lambda st: _sb_more(st[0], st[1]),
                                    lambda st: (st[0] - 1,) + block(st[0], st[1], st[2], None),
                                    (qi - 1, carries, accs))
        for h in range(hb):
            o_ref[h] = accs[h]

    kv = pl.BlockSpec((hb, n_rows, hd), lambda h, i: (h, 0, 0))
    qs = pl.BlockSpec((hb, tq, hd), lambda h, i: (h, i, 0))
    return pl.pallas_call(body, grid=(heads // hb, n_rows // tq), in_specs=[qs, kv, kv], out_specs=qs, name=name,
                          out_shape=jax.ShapeDtypeStruct((heads, n_rows, hd), F32),
                          compiler_params=_params("parallel", "arbitrary"))(q, k, v)


def _sb_bwd(name, q, k, v, o, do):
    heads, n_rows, hd = q.shape
    tq = _tile(n_rows // 2, 256)
    hb = SB_HEADS_PER_STEP
    scale = hd ** -0.5

    def body(q_ref, k_ref, v_ref, o_ref, do_ref, dq_ref, dk_ref, dv_ref):
        qi = pl.program_id(1)

        @pl.when(qi == 0)
        def _():
            dk_ref[...] = jnp.zeros(dk_ref.shape, F32)
            dv_ref[...] = jnp.zeros(dv_ref.shape, F32)

        qb = [q_ref[h] * scale for h in range(hb)]
        dob16 = [do_ref[h].astype(BF16) for h in range(hb)]
        delta = [jnp.sum(dob16[h].astype(F32) * o_ref[h], axis=1, keepdims=True) for h in range(hb)]
        row = lax.broadcasted_iota(jnp.int32, (tq, tq), 0)
        col = lax.broadcasted_iota(jnp.int32, (tq, tq), 1)
        tri = (row > col).astype(BF16)
        tri_incl = (row >= col).astype(BF16)

        def block(kb, carries, pcarries, dqs, causal):
            ks = pl.ds(pl.multiple_of(kb * tq, tq), tq)
            new_c, new_p, new_q = [], [], []
            for h in range(hb):
                kblk, vblk = k_ref[h, ks, :], v_ref[h, ks, :]
                lk, lb = _sb_block(qb[h], kblk, causal)
                a = jnp.exp(lb + _split_dot(lk, tri) + carries[h])
                if causal is not None:
                    a = jnp.where(causal, a, 0.0)
                a16 = a.astype(BF16)
                p = _dot(dob16[h], vblk, NT) * a16.astype(F32)
                beta = jnp.exp(lb)
                dz = p * (1.0 - beta) - beta * (delta[h] - pcarries[h] - _split_dot(p, tri_incl))
                if causal is not None:
                    dz = jnp.where(causal, dz, 0.0)
                dz16 = dz.astype(BF16)
                dk_ref[h, ks, :] += _dot(dz16, qb[h], TN)
                dv_ref[h, ks, :] += _dot(a16, dob16[h], TN)
                new_c.append(carries[h] + jnp.sum(lk, axis=1, keepdims=True))
                new_p.append(pcarries[h] + jnp.sum(p, axis=1, keepdims=True))
                new_q.append(dqs[h] + _dot(dz16, kblk))
            return tuple(new_c), tuple(new_p), tuple(new_q)

        zc = tuple(jnp.zeros((tq, 1), F32) for _ in range(hb))
        zq = tuple(jnp.zeros((tq, hd), F32) for _ in range(hb))
        st = block(qi, zc, zc, zq, col < row)
        st = lax.while_loop(lambda s: _sb_more(s[0], s[1]),
                            lambda s: (s[0] - 1,) + block(s[0], s[1], s[2], s[3], None), (qi - 1,) + st)
        for h in range(hb):
            dq_ref[h] = st[3][h] * scale

    kv = pl.BlockSpec((hb, n_rows, hd), lambda h, i: (h, 0, 0))
    qs = pl.BlockSpec((hb, tq, hd), lambda h, i: (h, i, 0))
    full = jax.ShapeDtypeStruct((heads, n_rows, hd), F32)
    return pl.pallas_call(body, grid=(heads // hb, n_rows // tq), in_specs=[qs, kv, kv, qs, qs],
                          out_specs=[qs, kv, kv], out_shape=[full, full, full], name=name,
                          compiler_params=_params("parallel", "arbitrary"))(q, k, v, o, do)


def _block_diag(xb, w_ref_val, dims):
    nb = w_ref_val.shape[0]
    bw = xb.shape[1] // nb
    return jnp.concatenate([_dot(xb[:, n * bw:(n + 1) * bw], w_ref_val[n], dims) for n in range(nb)], axis=1)


def _lru_gates_fwd(name, gx, conv_w, conv_b, wa, ba, wx, bx, lam):
    n_rows, w2 = gx.shape
    w = w2 // 2
    tm = _tile(n_rows, 256)

    def fn(i, nt, br, prev, cw, cb, wa_v, ba_v, wx_v, bx_v, lam_v):
        xc = cb + sum(cw[k:k + 1] * _shift_down(br, prev, LRU_CONV - 1 - k, i == 0) for k in range(LRU_CONV))
        xb = xc.astype(BF16)
        r = _sigmoid(_block_diag(xb, wa_v, NN) + ba_v)
        ig = _sigmoid(_block_diag(xb, wx_v, NN) + bx_v)
        log_a = (-LRU_C * r) * _softplus(-lam_v)
        a = jnp.exp(log_a)
        gated = (ig * xc) * _one_minus_a2_sqrt(log_a)
        return (xc, r, ig, a, gated), ()

    return _rowwise(name, fn, [('t', gx, w, 1), ('p', gx, w, 1), ('b', conv_w), ('b', conv_b), ('b', wa), ('b', ba),
                               ('b', wx), ('b', bx), ('b', lam)], [(w, F32)] * 5, [], n_rows, tm)


def _lru_scan_fwd(name, a, gated, gx):
    n_rows, w = a.shape
    tm = _tile(n_rows, 256)

    def body(a_ref, x_ref, bg_ref, hs_ref, y_ref, carry_ref):
        @pl.when(pl.program_id(0) == 0)
        def _():
            carry_ref[...] = jnp.zeros(carry_ref.shape, F32)
        carry_ref[...] = _scan_tile(a_ref, x_ref, hs_ref, carry_ref[...], False, tm)
        y_ref[...] = (_gelu(bg_ref[...]) * hs_ref[...]).astype(BF16)

    tok = pl.BlockSpec((tm, w), lambda i: (i, 0))
    return pl.pallas_call(body, grid=(n_rows // tm,), in_specs=[tok, tok, tok], out_specs=[tok, tok], name=name,
                          out_shape=[jax.ShapeDtypeStruct((n_rows, w), F32), jax.ShapeDtypeStruct((n_rows, w), BF16)],
                          scratch_shapes=[pltpu.VMEM((1, w), F32)], compiler_params=_params("arbitrary"))(a, gated, gx)


def _lru_scan_bwd(name, a, dy, gx):
    n_rows, w = a.shape
    tm = _tile(n_rows, 256)
    nt = n_rows // tm
    per8 = tm // SUBLANES

    def body(a_ref, an_ref, dy_ref, bg_ref, lam_ref, carry_ref, aup_ref, dhs_ref):
        i = pl.program_id(0)

        @pl.when(i == 0)
        def _():
            carry_ref[...] = jnp.zeros(carry_ref.shape, F32)
        aup_ref[...] = _shift_up(a_ref[...], an_ref[...], 1, i == 0)
        dhs_ref[...] = dy_ref[...] * _gelu(bg_ref[...])
        carry_ref[...] = _scan_tile(aup_ref, dhs_ref, lam_ref, carry_ref[...], True, tm)

    tok = pl.BlockSpec((tm, w), lambda i: (nt - 1 - i, 0))
    nxt = pl.BlockSpec((SUBLANES, w), lambda i: (jnp.minimum((nt - i) * per8, n_rows // SUBLANES - 1), 0))
    return pl.pallas_call(body, grid=(nt,), in_specs=[tok, nxt, tok, tok], out_specs=tok, name=name,
                          out_shape=jax.ShapeDtypeStruct((n_rows, w), F32),
                          scratch_shapes=[pltpu.VMEM((1, w), F32), pltpu.VMEM((tm, w), F32), pltpu.VMEM((tm, w), F32)],
                          compiler_params=_params("arbitrary"))(a, a, dy, gx)


def _lru_gates_bwd(name, lam_t, hs, xc, r, ig, a, wa, wx, lam):
    n_rows, w = xc.shape
    nb, bw, _ = wa.shape
    tm = _tile(n_rows, 256)

    def fn(i, nt, lt, hs_v, hs_prev, xc_v, r_v, ig_v, a_v, wa_v, wx_v, lam_v):
        sp = _softplus(-lam_v)
        log_a = (-LRU_C * r_v) * sp
        mult = _one_minus_a2_sqrt(log_a)
        d_a = lt * _shift_down(hs_v, hs_prev, 1, i == 0)
        d_ig = lt * xc_v * mult
        d_mult = lt * ig_v * xc_v
        d_log_a = d_a * a_v - d_mult * (a_v * a_v) / mult
        d_ra = d_log_a * (-LRU_C * sp) * r_v * (1.0 - r_v)
        d_ia = d_ig * ig_v * (1.0 - ig_v)
        d_ra16, d_ia16, xb = d_ra.astype(BF16), d_ia.astype(BF16), xc_v.astype(BF16)
        dxc = lt * ig_v * mult + _block_diag(d_ra16, wa_v, NT) + _block_diag(d_ia16, wx_v, NT)
        dwa = jnp.concatenate([_dot(xb[:, n * bw:(n + 1) * bw], d_ra16[:, n * bw:(n + 1) * bw], TN)
                               for n in range(nb)], axis=0)
        dwx = jnp.concatenate([_dot(xb[:, n * bw:(n + 1) * bw], d_ia16[:, n * bw:(n + 1) * bw], TN)
                               for n in range(nb)], axis=0)
        col = lambda t: jnp.sum(t, axis=0, keepdims=True)
        return (dxc,), (dwa, dwx, col(d_ra), col(d_ia), col(d_log_a * (-LRU_C * r_v)))

    tiled = lambda arr: ('t', arr, w, 0)
    return _rowwise(name, fn, [tiled(lam_t), tiled(hs), ('p', hs, w, 0), tiled(xc), tiled(r), tiled(ig), tiled(a),
                               ('b', wa), ('b', wx), ('b', lam)],
                    [(w, F32)], [(nb * bw, bw), (nb * bw, bw), (1, w), (1, w), (1, w)], n_rows, tm)


def _lru_conv_bwd(name, dxc, gx, dy, hs, conv_w):
    n_rows, w = dxc.shape
    tm = _tile(n_rows, 256)

    def fn(i, nt, dxc_v, dxc_next, bg, br, br_prev, dy_v, hs_v, cw):
        dbr = sum(cw[k:k + 1] * _shift_up(dxc_v, dxc_next, LRU_CONV - 1 - k, i == nt - 1) for k in range(LRU_CONV))
        dbg = dy_v * hs_v * _gelu_grad(bg)
        dcw = [jnp.sum(dxc_v * _shift_down(br, br_prev, LRU_CONV - 1 - k, i == 0), axis=0, keepdims=True)
               for k in range(LRU_CONV)]
        dcw = jnp.concatenate(dcw + [jnp.zeros((SUBLANES - LRU_CONV, w), F32)], axis=0)
        return (jnp.concatenate([dbg, dbr], axis=1),), (dcw, jnp.sum(dxc_v, axis=0, keepdims=True))

    return _rowwise(name, fn, [('t', dxc, w, 0), ('n', dxc, w, 0), ('t', gx, w, 0), ('t', gx, w, 1), ('p', gx, w, 1),
                               ('t', dy, w, 0), ('t', hs, w, 0), ('b', conv_w)],
                    [(2 * w, BF16)], [(SUBLANES, w), (1, w)], n_rows, tm)


def _loss_head(name, h, gain, target):
    n_rows, dm = h.shape
    tm = _tile(n_rows, 512)

    def fn(i, nt, hv, tv, g):
        r, xhat = _rms(hv)
        err = xhat * g - tv
        dy = err * (1.0 / dm)
        return ((_rms_bwd(dy, xhat, r, g),),
                (jnp.sum(err * err, axis=0, keepdims=True), jnp.sum(dy * xhat, axis=0, keepdims=True)))

    return _rowwise(name, fn, [('t', h, dm, 0), ('t', target, dm, 0), ('b', gain)], [(dm, F32)], [(1, dm), (1, dm)],
                    n_rows, tm)


def _adamw(name, gparts, w, m, v):
    n_parts, n_rows, cols = gparts.shape
    tr = n_rows
    for cand in (256, 128, 64, 32, 16, 8):
        if n_rows % cand == 0:
            tr = cand
            break
    c1 = 1.0 - ADAM_B1 ** ADAM_STEP
    c2 = 1.0 - ADAM_B2 ** ADAM_STEP

    def body(gp_ref, w_ref, m_ref, v_ref, g_ref, d_ref, nm_ref, nv_ref):
        g = gp_ref[0].astype(F32)
        for p in range(1, n_parts):
            g = g + gp_ref[p].astype(F32)
        m_new = ADAM_B1 * m_ref[...] + (1.0 - ADAM_B1) * g
        v_new = ADAM_B2 * v_ref[...] + (1.0 - ADAM_B2) * (g * g)
        m_hat = m_new / c1
        v_hat = v_new / c2
        g_ref[...] = g
        d_ref[...] = -ADAM_LR * (m_hat / (jnp.sqrt(v_hat) + ADAM_EPS) + ADAM_WD * w_ref[...])
        nm_ref[...] = m_new
        nv_ref[...] = v_new

    blk = pl.BlockSpec((tr, cols), lambda i: (i, 0))
    shp = jax.ShapeDtypeStruct((n_rows, cols), F32)
    return pl.pallas_call(body, grid=(n_rows // tr,), name=name,
                          in_specs=[pl.BlockSpec((n_parts, tr, cols), lambda i: (0, i, 0)), blk, blk, blk],
                          out_specs=[blk, blk, blk, blk], out_shape=[shp, shp, shp, shp],
                          compiler_params=_params("parallel"))(gparts, w, m, v)


def _pack_rows(arrays, cols, lead=0):
    flat = [a.reshape(a.shape[:lead] + (-1,)) for a in arrays]
    cat = jnp.concatenate(flat, axis=lead) if len(flat) > 1 else flat[0]
    n = cat.shape[lead]
    pad = (-n) % (cols * PACK_ROWS)
    if pad:
        cat = jnp.pad(cat, [(0, 0)] * lead + [(0, pad)])
    return cat.reshape(cat.shape[:lead] + (-1, cols))


def _unpack_rows(packed, shapes, lead=0):
    flat = packed.reshape(packed.shape[:lead] + (-1,))
    out, off = [], 0
    for s in shapes:
        n = math.prod(s)
        out.append(lax.slice_in_dim(flat, off, off + n, axis=lead).reshape(flat.shape[:lead] + tuple(s)))
        off += n
    return out


def kernel(x, ffn1_norm, ffn1_w_in, ffn1_w_out, mix_norm, ffn2_norm, ffn2_w_in, ffn2_w_out, final_norm, s5_w_in, s5_lam_re, s5_lam_im, s5_log_dt, s5_b_re, s5_b_im, s5_c_re, s5_c_im, s5_d, s5_w_out, sb_w_qkv, sb_w_out, lru_w_in, lru_conv_w, lru_conv_b, lru_w_a, lru_b_a, lru_w_x, lru_b_x, lru_lambda, lru_w_out, loss_target, m_ffn1_norm, m_ffn1_w_in, m_ffn1_w_out, m_mix_norm, m_ffn2_norm, m_ffn2_w_in, m_ffn2_w_out, m_final_norm, m_s5_w_in, m_s5_lam_re, m_s5_lam_im, m_s5_log_dt, m_s5_b_re, m_s5_b_im, m_s5_c_re, m_s5_c_im, m_s5_d, m_s5_w_out, m_sb_w_qkv, m_sb_w_out, m_lru_w_in, m_lru_conv_w, m_lru_conv_b, m_lru_w_a, m_lru_b_a, m_lru_w_x, m_lru_b_x, m_lru_lambda, m_lru_w_out, v_ffn1_norm, v_ffn1_w_in, v_ffn1_w_out, v_mix_norm, v_ffn2_norm, v_ffn2_w_in, v_ffn2_w_out, v_final_norm, v_s5_w_in, v_s5_lam_re, v_s5_lam_im, v_s5_log_dt, v_s5_b_re, v_s5_b_im, v_s5_c_re, v_s5_c_im, v_s5_d, v_s5_w_out, v_sb_w_qkv, v_sb_w_out, v_lru_w_in, v_lru_conv_w, v_lru_conv_b, v_lru_w_a, v_lru_b_a, v_lru_w_x, v_lru_b_x, v_lru_lambda, v_lru_w_out):
    local = dict(locals())
    W = {n: local[n] for n in WEIGHTS}
    M = {n: local["m_" + n] for n in WEIGHTS}
    V = {n: local["v_" + n] for n in WEIGHTS}

    h0 = x[0]
    target = loss_target[0]
    n_rows, dm = h0.shape
    depth = ffn1_norm.shape[0]

    def ffn_weights(tag):
        wi = _all_gather("ag_%s_w_in" % tag, W[tag + "_w_in"].astype(BF16))
        wo = _all_gather("ag_%s_w_out" % tag, W[tag + "_w_out"].astype(BF16))
        return (wi.reshape((2, N_DEV // 2) + wi.shape[1:]), wo.reshape((N_DEV // 2, 2) + wo.shape[1:]))

    ffn_w = {"ffn1": ffn_weights("ffn1"), "ffn2": ffn_weights("ffn2")}
    mix_shapes = [W[n].shape for n in MIXER_BIG]
    mix_g = _all_gather("ag_mixers", _pack_rows([W[n].astype(BF16) for n in MIXER_BIG], dm))
    full = {n: _unshard(a, SHARD_AXIS[n]) for n, a in zip(MIXER_BIG, _unpack_rows(mix_g, mix_shapes, lead=1))}
    small_shapes = [W[n].shape for n in SMALL_SHARDED]
    small_g = _all_gather("ag_small", _pack_rows([W[n] for n in SMALL_SHARDED], 128))
    full.update({n: _unshard(a, SHARD_AXIS[n])
                 for n, a in zip(SMALL_SHARDED, _unpack_rows(small_g, small_shapes, lead=1))})

    n_s5 = s5_w_in.shape[0]
    s5_groups = s5_lam_re.shape[1]
    heads = dm // SB_HEAD_DIM

    grads = {}
    saved = []
    h = h0

    for layer in range(depth):
        kind, j = layer % 3, layer // 3
        rec = {"h0": h}
        h = _ffn_fwd("ffn1_fwd_%d" % layer, h, ffn1_norm[layer:layer + 1], *ffn_w["ffn1"], layer)
        rec["h1"] = h
        gain = mix_norm[layer:layer + 1]
        if kind == 0:
            (u,) = _mm_fwd("s5_in_%d" % layer, h, full["s5_w_in"][j], F32, gain=gain)
            pars = (s5_lam_re[j], s5_lam_im[j], s5_log_dt[j], s5_b_re[j], s5_b_im[j], s5_c_re[j], s5_c_im[j])
            mats, mats_vjp = jax.vjp(_s5_mats, *pars)
            apw, ar_fwd, ar_rev = _s5_powers(*pars[:3])
            ys, sprev = _s5_fwd("s5_core_%d" % layer, u, *mats[:3], apw, ar_fwd)
            d_skip = full["s5_d"][j:j + 1]
            (z,) = _rowwise("s5_gelu_%d" % layer, lambda i, nt, ys_v, u_v, d_v: ((_gelu(ys_v + d_v * u_v),), ()),
                            [('t', ys, dm, 0), ('t', u, dm, 0), ('b', d_skip)], [(dm, BF16)], [], n_rows,
                            _tile(n_rows, 512))
            h, vg = _mm_fwd("s5_out_%d" % layer, z, full["s5_w_out"][j], F32, resid=h, glu=True)
            rec.update(u=u, ys=ys, sprev=sprev, z=z, vg=vg, mats=mats, mats_vjp=mats_vjp, apw=apw,
                       ar_rev=ar_rev, d_skip=d_skip)
        elif kind == 1:
            (qkv,) = _mm_fwd("sb_in_%d" % layer, h, full["sb_w_qkv"][j], BF16, gain=gain)
            qkv_h = qkv.reshape(n_rows, 3, heads, SB_HEAD_DIM).transpose(1, 2, 0, 3)
            o = _sb_fwd("sb_attn_%d" % layer, qkv_h[0], qkv_h[1], qkv_h[2])
            o_flat = o.transpose(1, 0, 2).reshape(n_rows, dm).astype(BF16)
            (h,) = _mm_fwd("sb_out_%d" % layer, o_flat, full["sb_w_out"][j], F32, resid=h)
            rec.update(qkv_h=qkv_h, o=o, o_flat=o_flat)
        else:
            (gx,) = _mm_fwd("lru_in_%d" % layer, h, full["lru_w_in"][j], F32, gain=gain)
            wa, wx = full["lru_w_a"][j], full["lru_w_x"][j]
            ba, bx = full["lru_b_a"][j].reshape(1, dm), full["lru_b_x"][j].reshape(1, dm)
            lam_row = full["lru_lambda"][j:j + 1]
            xc, r, ig, a, gated = _lru_gates_fwd("lru_gates_%d" % layer, gx, full["lru_conv_w"][j],
                                                 full["lru_conv_b"][j:j + 1], wa, ba, wx, bx, lam_row)
            hs, y = _lru_scan_fwd("lru_scan_%d" % layer, a, gated, gx)
            (h,) = _mm_fwd("lru_out_%d" % layer, y, full["lru_w_out"][j], F32, resid=h)
            rec.update(gx=gx, xc=xc, r=r, ig=ig, a=a, hs=hs, y=y, wa=wa, wx=wx, lam_row=lam_row)
        rec["h2"] = h
        h = _ffn_fwd("ffn2_fwd_%d" % layer, h, ffn2_norm[layer:layer + 1], *ffn_w["ffn2"], layer)
        saved.append(rec)

    dh, err2, dgf = _loss_head("loss_head", h, final_norm.reshape(1, dm), target)
    loss = lax.psum(0.5 / dm * jnp.sum(err2), ("x", "y", "c"))
    grads["final_norm"] = dgf.reshape(final_norm.shape)

    per_layer = {n: [None] * depth for n in ("ffn1_norm", "mix_norm", "ffn2_norm", "ffn1_w_in", "ffn1_w_out",
                                             "ffn2_w_in", "ffn2_w_out")}
    mixer_grads = {}

    def put(name, j, value, count):
        mixer_grads.setdefault(name, [None] * count)[j] = value

    for layer in reversed(range(depth)):
        kind, j = layer % 3, layer // 3
        rec = saved[layer]
        dh, dwi, dwo, dg = _ffn_bwd("ffn2_bwd_%d" % layer, rec["h2"], dh, ffn2_norm[layer:layer + 1],
                                    *ffn_w["ffn2"], layer)
        per_layer["ffn2_w_in"][layer], per_layer["ffn2_w_out"][layer], per_layer["ffn2_norm"][layer] = dwi, dwo, dg
        gain = mix_norm[layer:layer + 1]
        if kind == 0:
            dvg, = _rowwise("s5_glu_bwd_%d" % layer,
                            lambda i, nt, d_v, vg_v: ((jnp.concatenate(
                                [d_v * _sigmoid(vg_v[:, dm:]),
                                 d_v * vg_v[:, :dm] * _sigmoid(vg_v[:, dm:]) * (1.0 - _sigmoid(vg_v[:, dm:]))],
                                axis=1),), ()),
                            [('t', dh, dm, 0), ('t', rec["vg"], 2 * dm, 0)], [(2 * dm, BF16)], [], n_rows,
                            _tile(n_rows, 256))
            dz, dw_out = _mm_bwd("s5_out_bwd_%d" % layer, rec["z"], dvg, full["s5_w_out"][j])

            def gelu_bwd(i, nt, dz_v, ys_v, u_v, d_v):
                dy_v = dz_v * _gelu_grad(ys_v + d_v * u_v)
                return (dy_v,), (jnp.sum(dy_v * u_v, axis=0, keepdims=True),)

            dys, dd = _rowwise("s5_gelu_bwd_%d" % layer, gelu_bwd,
                               [('t', dz, dm, 0), ('t', rec["ys"], dm, 0), ('t', rec["u"], dm, 0),
                                ('b', rec["d_skip"])], [(dm, F32)], [(1, dm)], n_rows, _tile(n_rows, 512))
            m_, bm_, cm_, _ = rec["mats"]
            du_core, dm_m, dm_b, dm_c, d_a = _s5_bwd("s5_core_bwd_%d" % layer, rec["u"], dys, rec["sprev"],
                                                    m_, bm_, cm_, rec["apw"], rec["ar_rev"])
            dpars = rec["mats_vjp"]((dm_m, dm_b, dm_c, d_a.reshape(s5_groups, -1)))
            for nme, val in zip(("s5_lam_re", "s5_lam_im", "s5_log_dt", "s5_b_re", "s5_b_im", "s5_c_re", "s5_c_im"),
                                dpars):
                put(nme, j, val, n_s5)
            (du,) = _rowwise("s5_du_%d" % layer, lambda i, nt, a_v, dy_v, d_v: ((a_v + dy_v * d_v,), ()),
                             [('t', du_core, dm, 0), ('t', dys, dm, 0), ('b', rec["d_skip"])],
                             [(dm, BF16)], [], n_rows, _tile(n_rows, 512))
            dh, dw_in, dgm = _mm_bwd("s5_in_bwd_%d" % layer, rec["h1"], du, full["s5_w_in"][j], gain=gain, dres=dh)
            put("s5_d", j, dd[0], n_s5)
            put("s5_w_out", j, dw_out, n_s5)
            put("s5_w_in", j, dw_in, n_s5)
        elif kind == 1:
            do_flat, dw_out = _mm_bwd("sb_out_bwd_%d" % layer, rec["o_flat"], dh, full["sb_w_out"][j])
            do = do_flat.reshape(n_rows, heads, SB_HEAD_DIM).transpose(1, 0, 2)
            qkv_h = rec["qkv_h"]
            dq, dk, dv = _sb_bwd("sb_attn_bwd_%d" % layer, qkv_h[0], qkv_h[1], qkv_h[2], rec["o"], do)
            dqkv = jnp.stack([dq, dk, dv]).transpose(2, 0, 1, 3).reshape(n_rows, 3 * dm).astype(BF16)
            dh, dw_in, dgm = _mm_bwd("sb_in_bwd_%d" % layer, rec["h1"], dqkv, full["sb_w_qkv"][j], gain=gain, dres=dh)
            put("sb_w_out", j, dw_out, 1)
            put("sb_w_qkv", j, dw_in, 1)
        else:
            dy, dw_out = _mm_bwd("lru_out_bwd_%d" % layer, rec["y"], dh, full["lru_w_out"][j])
            lam_t = _lru_scan_bwd("lru_scan_bwd_%d" % layer, rec["a"], dy, rec["gx"])
            dxc, dwa, dwx, dba, dbx, dsp = _lru_gates_bwd("lru_gates_bwd_%d" % layer, lam_t, rec["hs"], rec["xc"],
                                                          rec["r"], rec["ig"], rec["a"], rec["wa"], rec["wx"],
                                                          rec["lam_row"])
            dgx, dcw, dcb = _lru_conv_bwd("lru_conv_bwd_%d" % layer, dxc, rec["gx"], dy, rec["hs"],
                                          full["lru_conv_w"][j])
            dh, dw_in, dgm = _mm_bwd("lru_in_bwd_%d" % layer, rec["h1"], dgx, full["lru_w_in"][j], gain=gain, dres=dh)
            nb = rec["wa"].shape[0]
            put("lru_w_out", j, dw_out, 1)
            put("lru_w_in", j, dw_in, 1)
            put("lru_w_a", j, dwa.reshape(rec["wa"].shape), 1)
            put("lru_w_x", j, dwx.reshape(rec["wx"].shape), 1)
            put("lru_b_a", j, dba.reshape(nb, -1), 1)
            put("lru_b_x", j, dbx.reshape(nb, -1), 1)
            put("lru_conv_w", j, dcw[:LRU_CONV], 1)
            put("lru_conv_b", j, dcb[0], 1)
            put("lru_lambda", j, (dsp * -_sigmoid(-rec["lam_row"]))[0], 1)
        per_layer["mix_norm"][layer] = dgm
        dh, dwi, dwo, dg = _ffn_bwd("ffn1_bwd_%d" % layer, rec["h0"], dh, ffn1_norm[layer:layer + 1],
                                    *ffn_w["ffn1"], layer)
        per_layer["ffn1_w_in"][layer], per_layer["ffn1_w_out"][layer], per_layer["ffn1_norm"][layer] = dwi, dwo, dg

    grad_x = dh[None]
    for n in ("ffn1_norm", "mix_norm", "ffn2_norm"):
        grads[n] = jnp.concatenate(per_layer[n], axis=0)
    for n, parts in mixer_grads.items():
        grads[n] = jnp.stack(parts)

    out_g, out_d, out_m, out_v = {}, {}, {}, {}

    def finish(names, res, shapes):
        for n, g_, d_, m_, v_ in zip(names, *[_unpack_rows(t, shapes) for t in res]):
            out_g[n], out_d[n], out_m[n], out_v[n] = g_, d_, m_, v_

    for tag in ("ffn1", "ffn2"):
        n_in, n_out = tag + "_w_in", tag + "_w_out"
        dwi = jnp.stack(per_layer[n_in], axis=2)
        recv = _exchange("xchg_" + n_in, dwi.reshape((N_DEV, -1, dm)).astype(BF16))
        finish([n_in], _adamw("adamw_" + n_in, recv, *[t[n_in].reshape(-1, dm) for t in (W, M, V)]), [W[n_in].shape])
        dwo = jnp.stack([t.reshape(N_DEV // 2, 2, -1, dm) for t in per_layer[n_out]], axis=2)
        recv = _exchange("xchg_" + n_out, dwo.reshape((N_DEV, -1, dm)).astype(BF16))
        finish([n_out], _adamw("adamw_" + n_out, recv, *[t[n_out].reshape(-1, dm) for t in (W, M, V)]),
               [W[n_out].shape])

    send = _pack_rows([_shard_blocks(grads[n], SHARD_AXIS[n]).astype(BF16) for n in MIXER_BIG], dm, lead=1)
    recv = _exchange("xchg_mixers", send)
    finish(MIXER_BIG, _adamw("adamw_mixers", recv, *[_pack_rows([t[n] for n in MIXER_BIG], dm) for t in (W, M, V)]),
           mix_shapes)

    small_names = REPLICATED + SMALL_SHARDED
    small_full_shapes = [grads[n].shape for n in small_names]
    parts = _all_gather("ag_small_grads", _pack_rows([grads[n] for n in small_names], 128))
    zero = jnp.zeros(parts.shape[1:], F32)
    summed = _adamw("sum_small_grads", parts, zero, zero, zero)[0]
    small_sum = dict(zip(small_names, _unpack_rows(summed, small_full_shapes)))
    me = 4 * lax.axis_index("x") + 2 * lax.axis_index("y") + lax.axis_index("c")
    rep_shapes = [W[n].shape for n in REPLICATED]
    g_rep = _pack_rows([small_sum[n] for n in REPLICATED], 128)[None]
    finish(REPLICATED, _adamw("adamw_replicated", g_rep, *[_pack_rows([t[n] for n in REPLICATED], 128)
                                                           for t in (W, M, V)]), rep_shapes)
    g_loc = []
    for n in SMALL_SHARDED:
        ax = SHARD_AXIS[n]
        size = W[n].shape[ax]
        g_loc.append(lax.dynamic_slice_in_dim(small_sum[n], me * size, size, axis=ax))
    finish(SMALL_SHARDED, _adamw("adamw_small", _pack_rows(g_loc, 128)[None],
                                 *[_pack_rows([t[n] for n in SMALL_SHARDED], 128) for t in (W, M, V)]), small_shapes)

    return (loss, grad_x, *[out_g[n] for n in WEIGHTS], *[out_d[n] for n in WEIGHTS],
            *[out_m[n] for n in WEIGHTS], *[out_v[n] for n in WEIGHTS])
```

```python
import functools
import math

import jax
import jax.numpy as jnp
from jax import lax
from jax.experimental import pallas as pl
from jax.experimental.pallas import tpu as pltpu

F32 = jnp.float32
BF16 = jnp.bfloat16
HI = lax.Precision.HIGHEST
MESH = pl.DeviceIdType.MESH

N_DEV = 8
RMS_EPS = 1e-6
S5_GROUP = 16
S5_CHUNK = 16
S5_OCTET = 128 // S5_GROUP
S5_REGROUP_ROWS = 32
SB_HEAD_DIM = 64
SB_UNDERFLOW = -104.0
SB_HEADS_PER_STEP = 2
LRU_CONV = 4
LRU_C = 8.0
ADAM_LR, ADAM_B1, ADAM_B2, ADAM_EPS, ADAM_WD, ADAM_STEP = 0.001, 0.9, 0.999, 1e-08, 0.01, 10
VMEM_LIMIT_BYTES = 56 * 1024 * 1024
SUBLANES = 8
PACK_ROWS = 256
ADAMW_LAYER_ROWS = 192

NN = (((1,), (0,)), ((), ()))
NT = (((1,), (1,)), ((), ()))
TN = (((0,), (0,)), ((), ()))

SHARD_AXIS = dict(
    ffn1_w_in=2, ffn1_w_out=1, ffn2_w_in=2, ffn2_w_out=1, s5_w_in=1, s5_d=1, s5_w_out=2, sb_w_qkv=2, sb_w_out=1,
    lru_w_in=2, lru_conv_w=2, lru_conv_b=1, lru_w_a=2, lru_b_a=2, lru_w_x=2, lru_b_x=2, lru_lambda=1, lru_w_out=1)
MIXER_BIG = ("s5_w_in", "s5_w_out", "sb_w_qkv", "sb_w_out", "lru_w_in", "lru_w_a", "lru_w_x", "lru_w_out")
SMALL_SHARDED = ("s5_d", "lru_conv_w", "lru_conv_b", "lru_b_a", "lru_b_x", "lru_lambda")
REPLICATED = ("ffn1_norm", "mix_norm", "ffn2_norm", "final_norm", "s5_lam_re", "s5_lam_im", "s5_log_dt",
              "s5_b_re", "s5_b_im", "s5_c_re", "s5_c_im")
WEIGHTS = ("ffn1_norm", "ffn1_w_in", "ffn1_w_out", "mix_norm", "ffn2_norm", "ffn2_w_in", "ffn2_w_out", "final_norm",
           "s5_w_in", "s5_lam_re", "s5_lam_im", "s5_log_dt", "s5_b_re", "s5_b_im", "s5_c_re", "s5_c_im", "s5_d",
           "s5_w_out", "sb_w_qkv", "sb_w_out", "lru_w_in", "lru_conv_w", "lru_conv_b", "lru_w_a", "lru_b_a",
           "lru_w_x", "lru_b_x", "lru_lambda", "lru_w_out")


def _dot(a, b, dims=NN, prec=None):
    return lax.dot_general(a, b, dims, precision=prec, preferred_element_type=F32)


def _params(*sem):
    return pltpu.CompilerParams(dimension_semantics=sem, vmem_limit_bytes=VMEM_LIMIT_BYTES)


def _tile(n, pref):
    return min(pref, n)


def _sigmoid(x):
    return jax.nn.sigmoid(x)


def _softplus(x):
    return jnp.maximum(x, 0.0) + jnp.log(1.0 + jnp.exp(-jnp.abs(x)))


_GELU_C = math.sqrt(2.0 / math.pi)


def _gelu(x):
    return 0.5 * x * (1.0 + jnp.tanh(_GELU_C * (x + 0.044715 * x * x * x)))


def _gelu_grad(x):
    t = jnp.tanh(_GELU_C * (x + 0.044715 * x * x * x))
    return 0.5 * (1.0 + t) + 0.5 * x * (1.0 - t * t) * _GELU_C * (1.0 + 3.0 * 0.044715 * x * x)


def _rms(x):
    r = lax.rsqrt(jnp.mean(x * x, axis=1, keepdims=True) + RMS_EPS)
    return r, x * r


def _rms_bwd(dhn, xhat, r, g):
    dxhat = dhn * g
    return r * (dxhat - xhat * jnp.mean(dxhat * xhat, axis=1, keepdims=True))


def _one_minus_a2_sqrt(log_a):
    t = jnp.tanh(log_a)
    return jnp.sqrt(-2.0 * t / (1.0 - t))


def _shift_down(cur, prev8, k, first):
    if k == 0:
        return cur
    row8 = lax.broadcasted_iota(jnp.int32, prev8.shape, 0)
    rolled = pltpu.roll(cur, k, 0)
    edge = jnp.where(first, 0.0, pltpu.roll(prev8, k, 0))
    top = jnp.where(row8 < k, edge, rolled[0:SUBLANES])
    return jnp.concatenate([top, rolled[SUBLANES:]], axis=0)


def _shift_up(cur, next8, k, last):
    if k == 0:
        return cur
    tm = cur.shape[0]
    row8 = lax.broadcasted_iota(jnp.int32, next8.shape, 0)
    rolled = pltpu.roll(cur, tm - k, 0)
    edge = jnp.where(last, 0.0, pltpu.roll(next8, SUBLANES - k, 0))
    bottom = jnp.where(row8 >= SUBLANES - k, edge, rolled[tm - SUBLANES:tm])
    return jnp.concatenate([rolled[:tm - SUBLANES], bottom], axis=0)


def _rowwise(name, fn, ins, out_tiled, out_acc, n_rows, tm, reverse=False):
    nt = n_rows // tm
    per8 = tm // SUBLANES
    n8 = n_rows // SUBLANES
    n_in, n_ot = len(ins), len(out_tiled)

    def pos(i):
        return nt - 1 - i if reverse else i

    in_specs, args = [], []
    for spec in ins:
        kind, arr = spec[0], spec[1]
        args.append(arr)
        if kind == 'b':
            in_specs.append(pl.BlockSpec(arr.shape, lambda i, nd=arr.ndim: (0,) * nd))
        elif kind == 't':
            in_specs.append(pl.BlockSpec((tm, spec[2]), lambda i, cb=spec[3]: (pos(i), cb)))
        elif kind == 'p':
            in_specs.append(pl.BlockSpec((SUBLANES, spec[2]),
                                         lambda i, cb=spec[3]: (jnp.maximum(pos(i) * per8 - 1, 0), cb)))
        else:
            in_specs.append(pl.BlockSpec((SUBLANES, spec[2]),
                                         lambda i, cb=spec[3]: (jnp.minimum((pos(i) + 1) * per8, n8 - 1), cb)))

    def body(*refs):
        i = pl.program_id(0)
        vals = [r[...] for r in refs[:n_in]]
        outs = refs[n_in:]
        touts, aouts = fn(pos(i), nt, *vals)
        for r, v in zip(outs[:n_ot], touts):
            r[...] = v.astype(r.dtype)
        if out_acc:
            @pl.when(i == 0)
            def _():
                for r in outs[n_ot:]:
                    r[...] = jnp.zeros(r.shape, r.dtype)
            for r, v in zip(outs[n_ot:], aouts):
                r[...] += v

    out_specs = [pl.BlockSpec((tm, n), lambda i: (pos(i), 0)) for n, _ in out_tiled]
    out_specs += [pl.BlockSpec((r, n), lambda i: (0, 0)) for r, n in out_acc]
    out_shape = [jax.ShapeDtypeStruct((n_rows, n), dt) for n, dt in out_tiled]
    out_shape += [jax.ShapeDtypeStruct((r, n), F32) for r, n in out_acc]
    return pl.pallas_call(body, grid=(nt,), in_specs=in_specs, out_specs=out_specs, out_shape=out_shape, name=name,
                          compiler_params=_params("arbitrary"))(*args)


def _all_gather(name, block):
    def body(x_ref, out_ref, send_sems, recv_sems, local_sem):
        x, y, c = lax.axis_index("x"), lax.axis_index("y"), lax.axis_index("c")
        me, sibling = (x, y, c), (x, y, 1 - c)
        chips = [(1 - x, y), (x, 1 - y), (1 - x, 1 - y)]

        def rows(px, py, pc):
            return out_ref.at[4 * px + 2 * py + pc]

        def copy(k, blk, to, src=None):
            return pltpu.make_async_remote_copy(
                src_ref=rows(*blk) if src is None else src, dst_ref=rows(*blk),
                send_sem=send_sems.at[k], recv_sem=recv_sems.at[k], device_id=to, device_id_type=MESH)

        mine = pltpu.make_async_copy(x_ref, rows(*me), local_sem)
        mine.start()
        first = [copy(0, me, sibling, src=x_ref)]
        first += [copy(1 + j, me, (*chip, c), src=x_ref) for j, chip in enumerate(chips)]
        for cp in first:
            cp.start()
        passed = [copy(4 + j, (*chip, c), sibling) for j, chip in enumerate(chips)]
        for j, chip in enumerate(chips):
            copy(1 + j, (*chip, c), me).wait_recv()
            passed[j].start()
        copy(0, sibling, me).wait_recv()
        for j, chip in enumerate(chips):
            copy(4 + j, (*chip, 1 - c), me).wait_recv()
        for cp in first + passed:
            cp.wait_send()
        mine.wait()

    return pl.pallas_call(
        body, name=name, out_shape=jax.ShapeDtypeStruct((N_DEV,) + block.shape, block.dtype),
        in_specs=[pl.BlockSpec(memory_space=pl.ANY)], out_specs=pl.BlockSpec(memory_space=pl.ANY),
        scratch_shapes=[pltpu.SemaphoreType.DMA((7,)), pltpu.SemaphoreType.DMA((7,)), pltpu.SemaphoreType.DMA(())],
    )(block)


def _exchange(name, send):
    def body(s_ref, r_ref, send_sems, recv_sems, local_sem):
        x, y, c = lax.axis_index("x"), lax.axis_index("y"), lax.axis_index("c")
        me = 4 * x + 2 * y + c
        mine = pltpu.make_async_copy(s_ref.at[me], r_ref.at[me], local_sem)
        mine.start()
        copies = []
        for k in range(1, N_DEV):
            dx, dy, dc = (k >> 2) & 1, (k >> 1) & 1, k & 1
            px = 1 - x if dx else x
            py = 1 - y if dy else y
            pc = 1 - c if dc else c
            peer = 4 * px + 2 * py + pc
            copies.append((pltpu.make_async_remote_copy(
                src_ref=s_ref.at[peer], dst_ref=r_ref.at[me], send_sem=send_sems.at[k - 1],
                recv_sem=recv_sems.at[k - 1], device_id=(px, py, pc), device_id_type=MESH), peer))
        for cp, _ in copies:
            cp.start()
        for k, (cp, peer) in enumerate(copies):
            pltpu.make_async_remote_copy(
                src_ref=s_ref.at[peer], dst_ref=r_ref.at[peer], send_sem=send_sems.at[k], recv_sem=recv_sems.at[k],
                device_id=(x, y, c), device_id_type=MESH).wait_recv()
        for cp, _ in copies:
            cp.wait_send()
        mine.wait()

    return pl.pallas_call(
        body, name=name, out_shape=jax.ShapeDtypeStruct(send.shape, send.dtype),
        in_specs=[pl.BlockSpec(memory_space=pl.ANY)], out_specs=pl.BlockSpec(memory_space=pl.ANY),
        scratch_shapes=[pltpu.SemaphoreType.DMA((7,)), pltpu.SemaphoreType.DMA((7,)), pltpu.SemaphoreType.DMA(())],
    )(send)


def _direct_copies(kind, s_ref, r_ref, send_sems, recv_sems, local_sem):
    x, y, c = lax.axis_index("x"), lax.axis_index("y"), lax.axis_index("c")
    me = 4 * x + 2 * y + c

    def src(p):
        return s_ref if kind == "gather" else s_ref.at[p]

    local = pltpu.make_async_copy(src(me), r_ref.at[me], local_sem)
    sends, recvs = [], []
    for k in range(1, N_DEV):
        px = 1 - x if (k >> 2) & 1 else x
        py = 1 - y if (k >> 1) & 1 else y
        pc = 1 - c if k & 1 else c
        peer = 4 * px + 2 * py + pc
        sends.append(pltpu.make_async_remote_copy(
            src_ref=src(peer), dst_ref=r_ref.at[me], send_sem=send_sems.at[k - 1], recv_sem=recv_sems.at[k - 1],
            device_id=(px, py, pc), device_id_type=MESH))
        recvs.append(pltpu.make_async_remote_copy(
            src_ref=src(peer), dst_ref=r_ref.at[peer], send_sem=send_sems.at[k - 1], recv_sem=recv_sems.at[k - 1],
            device_id=(x, y, c), device_id_type=MESH))
    return local, sends, recvs


def _call(body, *, grid, in_specs, out_specs, out_shape, name, args, scratch_shapes=(), semantics=None, comms=()):
    single = not isinstance(out_shape, (list, tuple))
    out_shape = [out_shape] if single else list(out_shape)
    out_specs = [out_specs] if single else list(out_specs)
    if not comms:
        res = pl.pallas_call(body, grid=grid, in_specs=in_specs, out_specs=out_specs, out_shape=out_shape, name=name,
                             scratch_shapes=list(scratch_shapes),
                             compiler_params=_params(*(semantics or ("arbitrary",) * len(grid))))(*args)
        return res[0] if single else res
    n_in, n_out, n_scr, n_c = len(args), len(out_shape), len(scratch_shapes), len(comms)

    def hosted(*refs):
        ins, srcs = refs[:n_in], refs[n_in:n_in + n_c]
        outs = refs[n_in + n_c:n_in + n_c + n_out]
        dsts = refs[n_in + n_c + n_out:n_in + 2 * n_c + n_out]
        scr = refs[n_in + 2 * n_c + n_out:n_in + 2 * n_c + n_out + n_scr]
        sems = refs[n_in + 2 * n_c + n_out + n_scr:]
        first = functools.reduce(jnp.logical_and, [pl.program_id(d) == 0 for d in range(len(grid))])
        last = functools.reduce(jnp.logical_and, [pl.program_id(d) == grid[d] - 1 for d in range(len(grid))])
        plans = [_direct_copies(comms[i][0], srcs[i], dsts[i], *sems[3 * i:3 * i + 3]) for i in range(n_c)]

        @pl.when(first)
        def _():
            for local, sends, _ in plans:
                local.start()
                for cp in sends:
                    cp.start()

        body(*ins, *outs, *scr)

        @pl.when(last)
        def _():
            for local, sends, recvs in plans:
                for cp in recvs:
                    cp.wait_recv()
                for cp in sends:
                    cp.wait_send()
                local.wait()

    any_spec = pl.BlockSpec(memory_space=pl.ANY)
    comm_shapes = [jax.ShapeDtypeStruct(((N_DEV,) + a.shape) if kind == "gather" else a.shape, a.dtype)
                   for kind, a in comms]
    sem_shapes = []
    for _ in comms:
        sem_shapes += [pltpu.SemaphoreType.DMA((7,)), pltpu.SemaphoreType.DMA((7,)), pltpu.SemaphoreType.DMA(())]
    res = pl.pallas_call(
        hosted, grid=grid, in_specs=list(in_specs) + [any_spec] * n_c, out_specs=out_specs + [any_spec] * n_c,
        out_shape=out_shape + comm_shapes, name=name, scratch_shapes=list(scratch_shapes) + sem_shapes,
        compiler_params=_params(*(("arbitrary",) * len(grid))))(*args, *[a for _, a in comms])
    return res


def _unshard(gathered, axis):
    local = gathered.shape[1:]
    full = jnp.moveaxis(gathered, 0, axis)
    return full.reshape(local[:axis] + (N_DEV * local[axis],) + local[axis + 1:])


def _shard_blocks(full, axis):
    s = full.shape
    cut = full.reshape(s[:axis] + (N_DEV, s[axis] // N_DEV) + s[axis + 1:])
    return jnp.moveaxis(cut, axis, 0)


def _mm_fwd(name, a, w, out_dtype, gain=None, resid=None, glu=False):
    n_rows, k = a.shape
    n = w.shape[1]
    tm = _tile(n_rows, 512)
    n_out = n // 2 if glu else n

    def body(*refs):
        it = iter(refs)
        a_ref, w_ref = next(it), next(it)
        g_ref = next(it) if gain is not None else None
        r_ref = next(it) if resid is not None else None
        outs = list(it)
        av = a_ref[...]
        if g_ref is not None:
            _, xhat = _rms(av)
            av = xhat * g_ref[...]
        res = _dot(av.astype(BF16), w_ref[...])
        if glu:
            outs[1][...] = res.astype(outs[1].dtype)
            res = res[:, :n_out] * _sigmoid(res[:, n_out:])
        if r_ref is not None:
            res = res + r_ref[...]
        outs[0][...] = res.astype(outs[0].dtype)

    args = [a, w]
    in_specs = [pl.BlockSpec((tm, k), lambda i: (i, 0)), pl.BlockSpec((k, n), lambda i: (0, 0))]
    if gain is not None:
        args.append(gain)
        in_specs.append(pl.BlockSpec((1, k), lambda i: (0, 0)))
    if resid is not None:
        args.append(resid)
        in_specs.append(pl.BlockSpec((tm, n_out), lambda i: (i, 0)))
    out_shape = [jax.ShapeDtypeStruct((n_rows, n_out), out_dtype)]
    out_specs = [pl.BlockSpec((tm, n_out), lambda i: (i, 0))]
    if glu:
        out_shape.append(jax.ShapeDtypeStruct((n_rows, n), F32))
        out_specs.append(pl.BlockSpec((tm, n), lambda i: (i, 0)))
    return pl.pallas_call(body, grid=(n_rows // tm,), in_specs=in_specs, out_specs=out_specs, out_shape=out_shape,
                          name=name, compiler_params=_params("parallel"))(*args)


def _mm_bwd(name, a, d, w, gain=None, dres=None):
    n_rows, k = a.shape
    n = w.shape[1]
    tm = _tile(n_rows, 512)

    def body(*refs):
        it = iter(refs)
        a_ref, d_ref, w_ref = next(it), next(it), next(it)
        g_ref = next(it) if gain is not None else None
        r_ref = next(it) if gain is not None else None
        da_ref, dw_ref = next(it), next(it)
        dg_ref = next(it) if gain is not None else None
        i = pl.program_id(0)

        @pl.when(i == 0)
        def _():
            dw_ref[...] = jnp.zeros(dw_ref.shape, F32)
            if dg_ref is not None:
                dg_ref[...] = jnp.zeros(dg_ref.shape, F32)

        av = a_ref[...]
        dv = d_ref[...].astype(BF16)
        if g_ref is not None:
            r, xhat = _rms(av)
            ab = (xhat * g_ref[...]).astype(BF16)
        else:
            ab = av.astype(BF16)
        dw_ref[...] += _dot(ab, dv, TN)
        da = _dot(dv, w_ref[...], NT)
        if g_ref is not None:
            dg_ref[...] += jnp.sum(da * xhat, axis=0, keepdims=True)
            da = r_ref[...] + _rms_bwd(da, xhat, r, g_ref[...])
        da_ref[...] = da.astype(da_ref.dtype)

    args = [a, d, w]
    in_specs = [pl.BlockSpec((tm, k), lambda i: (i, 0)), pl.BlockSpec((tm, n), lambda i: (i, 0)),
                pl.BlockSpec((k, n), lambda i: (0, 0))]
    out_shape = [jax.ShapeDtypeStruct((n_rows, k), F32), jax.ShapeDtypeStruct((k, n), F32)]
    out_specs = [pl.BlockSpec((tm, k), lambda i: (i, 0)), pl.BlockSpec((k, n), lambda i: (0, 0))]
    if gain is not None:
        args += [gain, dres]
        in_specs += [pl.BlockSpec((1, k), lambda i: (0, 0)), pl.BlockSpec((tm, k), lambda i: (i, 0))]
        out_shape.append(jax.ShapeDtypeStruct((1, k), F32))
        out_specs.append(pl.BlockSpec((1, k), lambda i: (0, 0)))
    return pl.pallas_call(body, grid=(n_rows // tm,), in_specs=in_specs, out_specs=out_specs, out_shape=out_shape,
                          name=name, compiler_params=_params("arbitrary"))(*args)


def _ffn_fwd(name, x, gain, wi, wo, comms=()):
    n_rows, dm = x.shape
    _, nj, _, fb = wi.shape
    tm = _tile(n_rows, 512)

    def body(x_ref, g_ref, wi_ref, wo_ref, y_ref):
        xv = x_ref[...]
        _, xhat = _rms(xv)
        hn = (xhat * g_ref[...]).astype(BF16)
        acc = jnp.zeros((tm, dm), F32)
        for j in range(nj):
            gate = _dot(hn, wi_ref[0, j])
            up = _dot(hn, wi_ref[1, j])
            act = (gate * _sigmoid(gate) * up).astype(BF16)
            acc = acc + _dot(act, wo_ref[j].reshape(fb, dm))
        y_ref[...] = xv + 0.5 * acc

    return _call(
        body, grid=(n_rows // tm,), name=name, args=[x, gain, wi, wo], comms=comms,
        in_specs=[pl.BlockSpec((tm, dm), lambda i: (i, 0)), pl.BlockSpec((1, dm), lambda i: (0, 0)),
                  pl.BlockSpec((2, nj, dm, fb), lambda i: (0, 0, 0, 0)),
                  pl.BlockSpec((nj, 2, fb // 2, dm), lambda i: (0, 0, 0, 0))],
        out_specs=[pl.BlockSpec((tm, dm), lambda i: (i, 0))],
        out_shape=[jax.ShapeDtypeStruct((n_rows, dm), F32)])


def _ffn_bwd_block(name, x, dy, gain, wi, wo, j, acc, comms=()):
    n_rows, dm = x.shape
    _, nj, _, fb = wi.shape
    tm = _tile(n_rows, 512)
    last = j == nj - 1

    def body(*refs):
        it = iter(refs)
        x_ref, dy_ref, g_ref, wi_ref, wo_ref = next(it), next(it), next(it), next(it), next(it)
        acc_ref = next(it) if acc is not None else None
        out_ref, dwi_ref, dwo_ref = next(it), next(it), next(it)
        dg_ref = next(it) if last else None
        i = pl.program_id(0)

        @pl.when(i == 0)
        def _():
            dwi_ref[...] = jnp.zeros(dwi_ref.shape, F32)
            dwo_ref[...] = jnp.zeros(dwo_ref.shape, F32)
            if last:
                dg_ref[...] = jnp.zeros(dg_ref.shape, F32)

        xv, dyv, g = x_ref[...], dy_ref[...], g_ref[...]
        r, xhat = _rms(xv)
        hn = (xhat * g).astype(BF16)
        wg, wu, wob = wi_ref[0], wi_ref[1], wo_ref[...].reshape(fb, dm)
        gate = _dot(hn, wg)
        up = _dot(hn, wu)
        s = _sigmoid(gate)
        silu = gate * s
        act = (silu * up).astype(BF16)
        dout = (0.5 * dyv).astype(BF16)
        dact = _dot(dout, wob, NT)
        dwo_ref[...] += _dot(act, dout, TN)
        dgate = (dact * up * (s * (1.0 + gate * (1.0 - s)))).astype(BF16)
        dup = (dact * silu).astype(BF16)
        dwi_ref[0] += _dot(hn, dgate, TN)
        dwi_ref[1] += _dot(hn, dup, TN)
        tot = _dot(dgate, wg, NT) + _dot(dup, wu, NT)
        if acc_ref is not None:
            tot = tot + acc_ref[...]
        if last:
            out_ref[...] = dyv + _rms_bwd(tot, xhat, r, g)
            dg_ref[...] += jnp.sum(tot * xhat, axis=0, keepdims=True)
        else:
            out_ref[...] = tot

    tok = pl.BlockSpec((tm, dm), lambda i: (i, 0))
    args = [x, dy, gain, wi, wo]
    in_specs = [tok, tok, pl.BlockSpec((1, dm), lambda i: (0, 0)),
                pl.BlockSpec((2, None, dm, fb), lambda i: (0, j, 0, 0)),
                pl.BlockSpec((None, 2, fb // 2, dm), lambda i: (j, 0, 0, 0))]
    if acc is not None:
        args.append(acc)
        in_specs.append(tok)
    out_specs = [tok, pl.BlockSpec((2, dm, fb), lambda i: (0, 0, 0)), pl.BlockSpec((fb, dm), lambda i: (0, 0))]
    out_shape = [jax.ShapeDtypeStruct((n_rows, dm), F32), jax.ShapeDtypeStruct((2, dm, fb), F32),
                 jax.ShapeDtypeStruct((fb, dm), F32)]
    if last:
        out_specs.append(pl.BlockSpec((1, dm), lambda i: (0, 0)))
        out_shape.append(jax.ShapeDtypeStruct((1, dm), F32))
    return _call(body, grid=(n_rows // tm,), name=name, in_specs=in_specs, out_specs=out_specs,
                 out_shape=out_shape, args=args, comms=comms)


def _ffn_bwd(name, x, dy, gain, wi, wo, comms_by_block=None):
    nj = wi.shape[1]
    acc, dwi, dwo, extra = None, [], [], []
    for j in range(nj):
        comms = (comms_by_block or {}).get(j, ())
        res = _ffn_bwd_block("%s_%d" % (name, j), x, dy, gain, wi, wo, j, acc, comms)
        n_own = 4 if j == nj - 1 else 3
        acc = res[0]
        dwi.append(res[1])
        dwo.append(res[2])
        extra += list(res[n_own:])
        dgain = res[3] if j == nj - 1 else None
    return acc, jnp.stack(dwi, axis=1), jnp.stack(dwo, axis=0), dgain, extra


def _scan8(a, x, reverse):
    row = lax.broadcasted_iota(jnp.int32, a.shape, 0)
    for k in (1, 2, 4):
        if reverse:
            keep = row < SUBLANES - k
            a_s, x_s = pltpu.roll(a, SUBLANES - k, 0), pltpu.roll(x, SUBLANES - k, 0)
        else:
            keep = row >= k
            a_s, x_s = pltpu.roll(a, k, 0), pltpu.roll(x, k, 0)
        x = a * jnp.where(keep, x_s, 0.0) + x
        a = a * jnp.where(keep, a_s, 1.0)
    return a, x


def _scan_tile(a_ref, x_ref, h_ref, carry, reverse, rows):
    groups = rows // SUBLANES

    def step(n, c):
        gidx = groups - 1 - n if reverse else n
        sl = pl.ds(pl.multiple_of(gidx * SUBLANES, SUBLANES), SUBLANES)
        a_cum, h0 = _scan8(a_ref[sl, :], x_ref[sl, :], reverse)
        h = a_cum * c + h0
        h_ref[sl, :] = h
        return h[0:1] if reverse else h[SUBLANES - 1:SUBLANES]

    return lax.fori_loop(0, groups, step, carry)


def _s5_mats(lam_re, lam_im, log_dt, b_re, b_im, c_re, c_im):
    lc = S5_CHUNK
    groups, p = lam_re.shape
    h = b_re.shape[-1]
    lam = lax.complex(lam_re, lam_im)
    lam_dt = lam * jnp.exp(log_dt)[:, None]
    lam_bar = jnp.exp(lam_dt)
    b_bar = ((lam_bar - 1.0) / lam)[:, :, None] * lax.complex(b_re, b_im)
    c = lax.complex(c_re, c_im)
    pw = jnp.exp(lam_dt[None] * jnp.arange(lc + 1, dtype=F32)[:, None, None])
    resp = jnp.einsum('ghp,tgp,gpk->tghk', c, pw[:lc], b_bar, precision=HI).real
    s_idx = jnp.arange(lc)[:, None]
    u_idx = jnp.arange(lc)[None, :]
    onehot = (jnp.arange(lc)[:, None, None] == (u_idx - s_idx)[None]).astype(F32)
    m = jnp.einsum('tghk,tsu->gskuh', resp, onehot, precision=HI).reshape(groups, lc * h, lc * h)
    w = pw[lc - 1::-1][:lc].transpose(1, 0, 2)[:, :, None, :] * b_bar.transpose(0, 2, 1)[:, None]
    bm = jnp.concatenate([w.real, w.imag], axis=-1).reshape(groups, lc * h, 2 * p)
    v = c[:, None] * pw[1:lc + 1].transpose(1, 0, 2)[:, :, None, :]
    v = v.transpose(0, 3, 1, 2)
    cm = jnp.concatenate([v.real, -v.imag], axis=1).reshape(groups, 2 * p, lc * h)
    a = jnp.concatenate([pw[lc].real, pw[lc].imag], axis=-1)
    return m, bm, cm, a


def _s5_powers(lam_re, lam_im, log_dt):
    lam_dt = lax.complex(lam_re, lam_im) * jnp.exp(log_dt)[:, None]
    pw = jnp.exp(lam_dt[None] * (S5_CHUNK * jnp.arange(1, 9, dtype=F32))[:, None, None])

    def c1(z):
        return jnp.concatenate([z.real, z.real], axis=-1).reshape(z.shape[0], -1)

    def c2(z):
        return jnp.concatenate([-z.imag, z.imag], axis=-1).reshape(z.shape[0], -1)

    p1, p2 = c1(pw), c2(pw)
    apw = jnp.stack([p1[0], p2[0], p1[1], p2[1], p1[3], p2[3], jnp.zeros_like(p1[0]), jnp.zeros_like(p1[0])])
    fwd = jnp.concatenate([p1, p2], axis=0)
    rev = jnp.concatenate([c1(pw[::-1]), c2(pw[::-1])], axis=0)
    return apw, fwd, rev


def _cmul(c1, c2, x, half, conj=False):
    sw = pltpu.roll(x, half, 1)
    return c1 * x - c2 * sw if conj else c1 * x + c2 * sw


def _gather_groups(u_ref, ug_ref, nc):
    h = S5_GROUP
    rows = min(S5_REGROUP_ROWS, nc)

    def step(r, _):
        base = pl.multiple_of(r * rows, rows)
        for t in range(S5_CHUNK):
            val = u_ref[pl.ds(base * S5_CHUNK + t, rows, stride=S5_CHUNK), :]
            for g in range(S5_OCTET):
                ug_ref[g, pl.ds(base, rows), t * h:(t + 1) * h] = val[:, g * h:(g + 1) * h]
        return 0

    lax.fori_loop(0, nc // rows, step, 0)


def _scatter_groups(yg_ref, y_ref, nc):
    h = S5_GROUP
    rows = min(S5_REGROUP_ROWS, nc)

    def step(r, _):
        base = pl.multiple_of(r * rows, rows)
        for t in range(S5_CHUNK):
            y_ref[pl.ds(base * S5_CHUNK + t, rows, stride=S5_CHUNK), :] = jnp.concatenate(
                [yg_ref[g, pl.ds(base, rows), t * h:(t + 1) * h] for g in range(S5_OCTET)], axis=1)
        return 0

    lax.fori_loop(0, nc // rows, step, 0)


def _s5_fwd(name, u, m, bm, cm, apw, arows):
    n_rows, width = u.shape
    nc = n_rows // S5_CHUNK
    groups, lh, _ = m.shape
    p2 = bm.shape[2]
    gb = S5_OCTET
    lanes = gb * S5_GROUP

    def body(u_ref, m_ref, b_ref, c_ref, apw_ref, ar_ref, y_ref, sp_ref, ug_ref, yg_ref, xs_ref):
        _gather_groups(u_ref, ug_ref, nc)
        for gi in range(gb):
            xs_ref[:, gi * p2:(gi + 1) * p2] = _dot(ug_ref[gi], b_ref[gi], prec=HI)
        row = lax.broadcasted_iota(jnp.int32, (SUBLANES, p2), 0)

        def step(n, carry):
            sl = pl.ds(pl.multiple_of(n * SUBLANES, SUBLANES), SUBLANES)
            new = []
            for gi in range(gb):
                ln = slice(gi * p2, (gi + 1) * p2)
                x = xs_ref[sl, ln]
                for q, k in enumerate((1, 2, 4)):
                    xs = jnp.where(row >= k, pltpu.roll(x, k, 0), 0.0)
                    x = x + _cmul(apw_ref[2 * q:2 * q + 1, ln], apw_ref[2 * q + 1:2 * q + 2, ln], xs, p2 // 2)
                cb = jnp.broadcast_to(carry[gi], (SUBLANES, p2))
                s8 = x + _cmul(ar_ref[0:8, ln], ar_ref[8:16, ln], cb, p2 // 2)
                sp_ref[sl, ln] = jnp.where(row >= 1, pltpu.roll(s8, 1, 0), cb)
                new.append(s8[SUBLANES - 1:SUBLANES])
            return tuple(new)

        lax.fori_loop(0, nc // SUBLANES, step, tuple(jnp.zeros((1, p2), F32) for _ in range(gb)))
        for gi in range(gb):
            yg_ref[gi] = (_dot(ug_ref[gi], m_ref[gi], prec=HI)
                          + _dot(sp_ref[:, gi * p2:(gi + 1) * p2], c_ref[gi], prec=HI))
        _scatter_groups(yg_ref, y_ref, nc)

    tok = pl.BlockSpec((n_rows, lanes), lambda g: (0, g), pipeline_mode=pl.Buffered(1))
    return pl.pallas_call(
        body, grid=(groups // gb,), name=name,
        in_specs=[tok, pl.BlockSpec((gb, lh, lh), lambda g: (g, 0, 0)),
                  pl.BlockSpec((gb, lh, p2), lambda g: (g, 0, 0)), pl.BlockSpec((gb, p2, lh), lambda g: (g, 0, 0)),
                  pl.BlockSpec((8, gb * p2), lambda g: (0, g)), pl.BlockSpec((16, gb * p2), lambda g: (0, g))],
        out_specs=[tok, pl.BlockSpec((nc, gb * p2), lambda g: (0, g))],
        out_shape=[jax.ShapeDtypeStruct((n_rows, width), F32), jax.ShapeDtypeStruct((nc, groups * p2), F32)],
        scratch_shapes=[pltpu.VMEM((gb, nc, lh), F32), pltpu.VMEM((gb, nc, lh), F32), pltpu.VMEM((nc, gb * p2), F32)],
        compiler_params=_params("parallel"),
    )(u, m, bm, cm, apw, arows)


def _s5_bwd(name, u, dy, sprev, m, bm, cm, apw, arows_rev):
    n_rows, width = u.shape
    nc = n_rows // S5_CHUNK
    groups, lh, _ = m.shape
    p2 = bm.shape[2]
    half = p2 // 2
    gb = S5_OCTET
    lanes = gb * S5_GROUP

    def body(u_ref, dy_ref, sp_ref, m_ref, b_ref, c_ref, apw_ref, ar_ref,
             du_ref, dm_ref, db_ref, dc_ref, da_ref, ug_ref, dyg_ref, ds_ref, gx_ref):
        _gather_groups(u_ref, ug_ref, nc)
        _gather_groups(dy_ref, dyg_ref, nc)
        for gi in range(gb):
            ds_ref[:, gi * p2:(gi + 1) * p2] = _dot(dyg_ref[gi], c_ref[gi], NT, prec=HI)
        row = lax.broadcasted_iota(jnp.int32, (SUBLANES, p2), 0)
        lane = lax.broadcasted_iota(jnp.int32, (SUBLANES, p2), 1)
        ngroups = nc // SUBLANES

        def step(n, state):
            carry, nxt, dacc = state
            sl = pl.ds(pl.multiple_of((ngroups - 1 - n) * SUBLANES, SUBLANES), SUBLANES)
            new_c, new_n, new_d = [], [], []
            for gi in range(gb):
                ln = slice(gi * p2, (gi + 1) * p2)
                d8 = ds_ref[sl, ln]
                x = jnp.where(row < SUBLANES - 1, pltpu.roll(d8, SUBLANES - 1, 0),
                              jnp.broadcast_to(nxt[gi], (SUBLANES, p2)))
                for q, k in enumerate((1, 2, 4)):
                    xs = jnp.where(row < SUBLANES - k, pltpu.roll(x, SUBLANES - k, 0), 0.0)
                    x = x + _cmul(apw_ref[2 * q:2 * q + 1, ln], apw_ref[2 * q + 1:2 * q + 2, ln], xs, half, conj=True)
                cb = jnp.broadcast_to(carry[gi], (SUBLANES, p2))
                g8 = x + _cmul(ar_ref[0:8, ln], ar_ref[8:16, ln], cb, half, conj=True)
                gx_ref[sl, ln] = g8
                s8 = sp_ref[sl, ln]
                p1 = g8 * s8
                pq = g8 * pltpu.roll(s8, half, 1)
                d_a = jnp.where(lane < half, p1 + pltpu.roll(p1, half, 1), pq - pltpu.roll(pq, half, 1))
                new_c.append(g8[0:1])
                new_n.append(d8[0:1])
                new_d.append(dacc[gi] + jnp.sum(d_a, axis=0, keepdims=True))
            return tuple(new_c), tuple(new_n), tuple(new_d)

        zeros = tuple(jnp.zeros((1, p2), F32) for _ in range(gb))
        _, _, dacc = lax.fori_loop(0, ngroups, step, (zeros, zeros, zeros))
        for gi in range(gb):
            ln = slice(gi * p2, (gi + 1) * p2)
            da_ref[:, ln] = dacc[gi]
            ug, dyg, gxg = ug_ref[gi], dyg_ref[gi], gx_ref[:, ln]
            dm_ref[gi] = _dot(ug, dyg, TN, prec=HI)
            dc_ref[gi] = _dot(sp_ref[:, ln], dyg, TN, prec=HI)
            db_ref[gi] = _dot(ug, gxg, TN, prec=HI)
            dyg_ref[gi] = _dot(dyg, m_ref[gi], NT, prec=HI) + _dot(gxg, b_ref[gi], NT, prec=HI)
        _scatter_groups(dyg_ref, du_ref, nc)

    tok = pl.BlockSpec((n_rows, lanes), lambda g: (0, g), pipeline_mode=pl.Buffered(1))
    tok_s = pl.BlockSpec((nc, gb * p2), lambda g: (0, g))
    mat_m = pl.BlockSpec((gb, lh, lh), lambda g: (g, 0, 0))
    mat_b = pl.BlockSpec((gb, lh, p2), lambda g: (g, 0, 0))
    mat_c = pl.BlockSpec((gb, p2, lh), lambda g: (g, 0, 0))
    return pl.pallas_call(
        body, grid=(groups // gb,), name=name,
        in_specs=[tok, tok, tok_s, mat_m, mat_b, mat_c,
                  pl.BlockSpec((8, gb * p2), lambda g: (0, g)), pl.BlockSpec((16, gb * p2), lambda g: (0, g))],
        out_specs=[tok, mat_m, mat_b, mat_c, pl.BlockSpec((1, gb * p2), lambda g: (0, g))],
        out_shape=[jax.ShapeDtypeStruct((n_rows, width), F32), jax.ShapeDtypeStruct(m.shape, F32),
                   jax.ShapeDtypeStruct(bm.shape, F32), jax.ShapeDtypeStruct(cm.shape, F32),
                   jax.ShapeDtypeStruct((1, groups * p2), F32)],
        scratch_shapes=[pltpu.VMEM((gb, nc, lh), F32), pltpu.VMEM((gb, nc, lh), F32),
                        pltpu.VMEM((nc, gb * p2), F32), pltpu.VMEM((nc, gb * p2), F32)],
        compiler_params=_params("parallel"),
    )(u, dy, sprev, m, bm, cm, apw, arows_rev)


def _split_dot(v, tri):
    hi = v.astype(BF16)
    lo = (v - hi.astype(F32)).astype(BF16)
    return _dot(hi, tri) + _dot(lo, tri)


def _sb_block(qb, kblk, causal):
    z = _dot(qb, kblk, NT)
    sp = _softplus(z)
    lk = -sp
    if causal is not None:
        lk = jnp.where(causal, lk, 0.0)
    return lk, z - sp


def _sb_more(kb, carries):
    top = jnp.max(carries[0])
    for c in carries[1:]:
        top = jnp.maximum(top, jnp.max(c))
    return (kb >= 0) & (top > SB_UNDERFLOW)


def _sb_fwd(name, q, k, v):
    heads, n_rows, hd = q.shape
    tq = _tile(n_rows // 2, 256)
    hb = SB_HEADS_PER_STEP
    scale = hd ** -0.5

    def body(q_ref, k_ref, v_ref, o_ref):
        qi = pl.program_id(1)
        qb = [q_ref[h] * scale for h in range(hb)]
        row = lax.broadcasted_iota(jnp.int32, (tq, tq), 0)
        col = lax.broadcasted_iota(jnp.int32, (tq, tq), 1)
        tri = (row > col).astype(BF16)

        def block(kb, carries, accs, causal):
            ks = pl.ds(pl.multiple_of(kb * tq, tq), tq)
            new_c, new_a = [], []
            for h in range(hb):
                lk, lb = _sb_block(qb[h], k_ref[h, ks, :], causal)
                a = jnp.exp(lb + _split_dot(lk, tri) + carries[h])
                if causal is not None:
                    a = jnp.where(causal, a, 0.0)
                new_a.append(accs[h] + _dot(a.astype(BF16), v_ref[h, ks, :]))
                new_c.append(carries[h] + jnp.sum(lk, axis=1, keepdims=True))
            return tuple(new_c), tuple(new_a)

        zc = tuple(jnp.zeros((tq, 1), F32) for _ in range(hb))
        za = tuple(jnp.zeros((tq, hd), F32) for _ in range(hb))
        carries, accs = block(qi, zc, za, col < row)
        _, _, accs = lax.while_loop(lambda st: _sb_more(st[0], st[1]),
                                    lambda st: (st[0] - 1,) + block(st[0], st[1], st[2], None),
                                    (qi - 1, carries, accs))
        for h in range(hb):
            o_ref[h] = accs[h]

    kv = pl.BlockSpec((hb, n_rows, hd), lambda h, i: (h, 0, 0))
    qs = pl.BlockSpec((hb, tq, hd), lambda h, i: (h, i, 0))
    return pl.pallas_call(body, grid=(heads // hb, n_rows // tq), in_specs=[qs, kv, kv], out_specs=qs, name=name,
                          out_shape=jax.ShapeDtypeStruct((heads, n_rows, hd), F32),
                          compiler_params=_params("parallel", "arbitrary"))(q, k, v)


def _sb_bwd(name, q, k, v, o, do):
    heads, n_rows, hd = q.shape
    tq = _tile(n_rows // 2, 256)
    hb = SB_HEADS_PER_STEP
    scale = hd ** -0.5

    def body(q_ref, k_ref, v_ref, o_ref, do_ref, dq_ref, dk_ref, dv_ref):
        qi = pl.program_id(1)

        @pl.when(qi == 0)
        def _():
            dk_ref[...] = jnp.zeros(dk_ref.shape, F32)
            dv_ref[...] = jnp.zeros(dv_ref.shape, F32)

        qb = [q_ref[h] * scale for h in range(hb)]
        dob16 = [do_ref[h].astype(BF16) for h in range(hb)]
        delta = [jnp.sum(dob16[h].astype(F32) * o_ref[h], axis=1, keepdims=True) for h in range(hb)]
        row = lax.broadcasted_iota(jnp.int32, (tq, tq), 0)
        col = lax.broadcasted_iota(jnp.int32, (tq, tq), 1)
        tri = (row > col).astype(BF16)
        tri_incl = (row >= col).astype(BF16)

        def block(kb, carries, pcarries, dqs, causal):
            ks = pl.ds(pl.multiple_of(kb * tq, tq), tq)
            new_c, new_p, new_q = [], [], []
            for h in range(hb):
                kblk, vblk = k_ref[h, ks, :], v_ref[h, ks, :]
                lk, lb = _sb_block(qb[h], kblk, causal)
                a = jnp.exp(lb + _split_dot(lk, tri) + carries[h])
                if causal is not None:
                    a = jnp.where(causal, a, 0.0)
                a16 = a.astype(BF16)
                p = _dot(dob16[h], vblk, NT) * a16.astype(F32)
                beta = jnp.exp(lb)
                dz = p * (1.0 - beta) - beta * (delta[h] - pcarries[h] - _split_dot(p, tri_incl))
                if causal is not None:
                    dz = jnp.where(causal, dz, 0.0)
                dz16 = dz.astype(BF16)
                dk_ref[h, ks, :] += _dot(dz16, qb[h], TN)
                dv_ref[h, ks, :] += _dot(a16, dob16[h], TN)
                new_c.append(carries[h] + jnp.sum(lk, axis=1, keepdims=True))
                new_p.append(pcarries[h] + jnp.sum(p, axis=1, keepdims=True))
                new_q.append(dqs[h] + _dot(dz16, kblk))
            return tuple(new_c), tuple(new_p), tuple(new_q)

        zc = tuple(jnp.zeros((tq, 1), F32) for _ in range(hb))
        zq = tuple(jnp.zeros((tq, hd), F32) for _ in range(hb))
        st = block(qi, zc, zc, zq, col < row)
        st = lax.while_loop(lambda s: _sb_more(s[0], s[1]),
                            lambda s: (s[0] - 1,) + block(s[0], s[1], s[2], s[3], None), (qi - 1,) + st)
        for h in range(hb):
            dq_ref[h] = st[3][h] * scale

    kv = pl.BlockSpec((hb, n_rows, hd), lambda h, i: (h, 0, 0))
    qs = pl.BlockSpec((hb, tq, hd), lambda h, i: (h, i, 0))
    full = jax.ShapeDtypeStruct((heads, n_rows, hd), F32)
    return pl.pallas_call(body, grid=(heads // hb, n_rows // tq), in_specs=[qs, kv, kv, qs, qs],
                          out_specs=[qs, kv, kv], out_shape=[full, full, full], name=name,
                          compiler_params=_params("parallel", "arbitrary"))(q, k, v, o, do)


def _block_diag(xb, w_ref_val, dims):
    nb = w_ref_val.shape[0]
    bw = xb.shape[1] // nb
    return jnp.concatenate([_dot(xb[:, n * bw:(n + 1) * bw], w_ref_val[n], dims) for n in range(nb)], axis=1)


def _lru_gates_fwd(name, gx, conv_w, conv_b, wa, ba, wx, bx, lam):
    n_rows, w2 = gx.shape
    w = w2 // 2
    tm = _tile(n_rows, 256)

    def fn(i, nt, br, prev, cw, cb, wa_v, ba_v, wx_v, bx_v, lam_v):
        xc = cb + sum(cw[k:k + 1] * _shift_down(br, prev, LRU_CONV - 1 - k, i == 0) for k in range(LRU_CONV))
        xb = xc.astype(BF16)
        r = _sigmoid(_block_diag(xb, wa_v, NN) + ba_v)
        ig = _sigmoid(_block_diag(xb, wx_v, NN) + bx_v)
        log_a = (-LRU_C * r) * _softplus(-lam_v)
        a = jnp.exp(log_a)
        gated = (ig * xc) * _one_minus_a2_sqrt(log_a)
        return (xc, r, ig, a, gated), ()

    return _rowwise(name, fn, [('t', gx, w, 1), ('p', gx, w, 1), ('b', conv_w), ('b', conv_b), ('b', wa), ('b', ba),
                               ('b', wx), ('b', bx), ('b', lam)], [(w, F32)] * 5, [], n_rows, tm)


def _lru_scan_fwd(name, a, gated, gx):
    n_rows, w = a.shape
    tm = _tile(n_rows, 256)

    def body(a_ref, x_ref, bg_ref, hs_ref, y_ref, carry_ref):
        @pl.when(pl.program_id(0) == 0)
        def _():
            carry_ref[...] = jnp.zeros(carry_ref.shape, F32)
        carry_ref[...] = _scan_tile(a_ref, x_ref, hs_ref, carry_ref[...], False, tm)
        y_ref[...] = (_gelu(bg_ref[...]) * hs_ref[...]).astype(BF16)

    tok = pl.BlockSpec((tm, w), lambda i: (i, 0))
    return pl.pallas_call(body, grid=(n_rows // tm,), in_specs=[tok, tok, tok], out_specs=[tok, tok], name=name,
                          out_shape=[jax.ShapeDtypeStruct((n_rows, w), F32), jax.ShapeDtypeStruct((n_rows, w), BF16)],
                          scratch_shapes=[pltpu.VMEM((1, w), F32)], compiler_params=_params("arbitrary"))(a, gated, gx)


def _lru_scan_bwd(name, a, dy, gx):
    n_rows, w = a.shape
    tm = _tile(n_rows, 256)
    nt = n_rows // tm
    per8 = tm // SUBLANES

    def body(a_ref, an_ref, dy_ref, bg_ref, lam_ref, carry_ref, aup_ref, dhs_ref):
        i = pl.program_id(0)

        @pl.when(i == 0)
        def _():
            carry_ref[...] = jnp.zeros(carry_ref.shape, F32)
        aup_ref[...] = _shift_up(a_ref[...], an_ref[...], 1, i == 0)
        dhs_ref[...] = dy_ref[...] * _gelu(bg_ref[...])
        carry_ref[...] = _scan_tile(aup_ref, dhs_ref, lam_ref, carry_ref[...], True, tm)

    tok = pl.BlockSpec((tm, w), lambda i: (nt - 1 - i, 0))
    nxt = pl.BlockSpec((SUBLANES, w), lambda i: (jnp.minimum((nt - i) * per8, n_rows // SUBLANES - 1), 0))
    return pl.pallas_call(body, grid=(nt,), in_specs=[tok, nxt, tok, tok], out_specs=tok, name=name,
                          out_shape=jax.ShapeDtypeStruct((n_rows, w), F32),
                          scratch_shapes=[pltpu.VMEM((1, w), F32), pltpu.VMEM((tm, w), F32), pltpu.VMEM((tm, w), F32)],
                          compiler_params=_params("arbitrary"))(a, a, dy, gx)


def _lru_gates_bwd(name, lam_t, hs, xc, r, ig, a, wa, wx, lam):
    n_rows, w = xc.shape
    nb, bw, _ = wa.shape
    tm = _tile(n_rows, 256)

    def fn(i, nt, lt, hs_v, hs_prev, xc_v, r_v, ig_v, a_v, wa_v, wx_v, lam_v):
        sp = _softplus(-lam_v)
        log_a = (-LRU_C * r_v) * sp
        mult = _one_minus_a2_sqrt(log_a)
        d_a = lt * _shift_down(hs_v, hs_prev, 1, i == 0)
        d_ig = lt * xc_v * mult
        d_mult = lt * ig_v * xc_v
        d_log_a = d_a * a_v - d_mult * (a_v * a_v) / mult
        d_ra = d_log_a * (-LRU_C * sp) * r_v * (1.0 - r_v)
        d_ia = d_ig * ig_v * (1.0 - ig_v)
        d_ra16, d_ia16, xb = d_ra.astype(BF16), d_ia.astype(BF16), xc_v.astype(BF16)
        dxc = lt * ig_v * mult + _block_diag(d_ra16, wa_v, NT) + _block_diag(d_ia16, wx_v, NT)
        dwa = jnp.concatenate([_dot(xb[:, n * bw:(n + 1) * bw], d_ra16[:, n * bw:(n + 1) * bw], TN)
                               for n in range(nb)], axis=0)
        dwx = jnp.concatenate([_dot(xb[:, n * bw:(n + 1) * bw], d_ia16[:, n * bw:(n + 1) * bw], TN)
                               for n in range(nb)], axis=0)
        col = lambda t: jnp.sum(t, axis=0, keepdims=True)
        return (dxc,), (dwa, dwx, col(d_ra), col(d_ia), col(d_log_a * (-LRU_C * r_v)))

    tiled = lambda arr: ('t', arr, w, 0)
    return _rowwise(name, fn, [tiled(lam_t), tiled(hs), ('p', hs, w, 0), tiled(xc), tiled(r), tiled(ig), tiled(a),
                               ('b', wa), ('b', wx), ('b', lam)],
                    [(w, F32)], [(nb * bw, bw), (nb * bw, bw), (1, w), (1, w), (1, w)], n_rows, tm)


def _lru_conv_bwd(name, dxc, gx, dy, hs, conv_w):
    n_rows, w = dxc.shape
    tm = _tile(n_rows, 256)

    def fn(i, nt, dxc_v, dxc_next, bg, br, br_prev, dy_v, hs_v, cw):
        dbr = sum(cw[k:k + 1] * _shift_up(dxc_v, dxc_next, LRU_CONV - 1 - k, i == nt - 1) for k in range(LRU_CONV))
        dbg = dy_v * hs_v * _gelu_grad(bg)
        dcw = [jnp.sum(dxc_v * _shift_down(br, br_prev, LRU_CONV - 1 - k, i == 0), axis=0, keepdims=True)
               for k in range(LRU_CONV)]
        dcw = jnp.concatenate(dcw + [jnp.zeros((SUBLANES - LRU_CONV, w), F32)], axis=0)
        return (jnp.concatenate([dbg, dbr], axis=1),), (dcw, jnp.sum(dxc_v, axis=0, keepdims=True))

    return _rowwise(name, fn, [('t', dxc, w, 0), ('n', dxc, w, 0), ('t', gx, w, 0), ('t', gx, w, 1), ('p', gx, w, 1),
                               ('t', dy, w, 0), ('t', hs, w, 0), ('b', conv_w)],
                    [(2 * w, BF16)], [(SUBLANES, w), (1, w)], n_rows, tm)


def _loss_head(name, h, gain, target):
    n_rows, dm = h.shape
    tm = _tile(n_rows, 512)

    def fn(i, nt, hv, tv, g):
        r, xhat = _rms(hv)
        err = xhat * g - tv
        dy = err * (1.0 / dm)
        return ((_rms_bwd(dy, xhat, r, g),),
                (jnp.sum(err * err, axis=0, keepdims=True), jnp.sum(dy * xhat, axis=0, keepdims=True)))

    return _rowwise(name, fn, [('t', h, dm, 0), ('t', target, dm, 0), ('b', gain)], [(dm, F32)], [(1, dm), (1, dm)],
                    n_rows, tm)


def _adamw(name, gparts, w, m, v):
    n_parts, n_rows, cols = gparts.shape
    tr = n_rows
    for cand in (256, 128, 64, 32, 16, 8):
        if n_rows % cand == 0:
            tr = cand
            break
    c1 = 1.0 - ADAM_B1 ** ADAM_STEP
    c2 = 1.0 - ADAM_B2 ** ADAM_STEP

    def body(gp_ref, w_ref, m_ref, v_ref, g_ref, d_ref, nm_ref, nv_ref):
        g = gp_ref[0].astype(F32)
        for p in range(1, n_parts):
            g = g + gp_ref[p].astype(F32)
        m_new = ADAM_B1 * m_ref[...] + (1.0 - ADAM_B1) * g
        v_new = ADAM_B2 * v_ref[...] + (1.0 - ADAM_B2) * (g * g)
        m_hat = m_new / c1
        v_hat = v_new / c2
        g_ref[...] = g
        d_ref[...] = -ADAM_LR * (m_hat / (jnp.sqrt(v_hat) + ADAM_EPS) + ADAM_WD * w_ref[...])
        nm_ref[...] = m_new
        nv_ref[...] = v_new

    blk = pl.BlockSpec((tr, cols), lambda i: (i, 0))
    shp = jax.ShapeDtypeStruct((n_rows, cols), F32)
    return pl.pallas_call(body, grid=(n_rows // tr,), name=name,
                          in_specs=[pl.BlockSpec((n_parts, tr, cols), lambda i: (0, i, 0)), blk, blk, blk],
                          out_specs=[blk, blk, blk, blk], out_shape=[shp, shp, shp, shp],
                          compiler_params=_params("parallel"))(gparts, w, m, v)


def _adamw_layers(name, recvs, w, m, v):
    n_layers, n_rows, cols = w.shape
    n_parts = recvs[0].shape[0]
    tr = max(t for t in range(16, ADAMW_LAYER_ROWS + 1, 16) if n_rows % t == 0)
    c1 = 1.0 - ADAM_B1 ** ADAM_STEP
    c2 = 1.0 - ADAM_B2 ** ADAM_STEP

    def body(*refs):
        gp_refs = refs[:n_layers]
        w_ref, m_ref, v_ref, g_ref, d_ref, nm_ref, nv_ref = refs[n_layers:]
        layer = pl.program_id(0)
        for k in range(n_layers):
            @pl.when(layer == k)
            def _(k=k):
                g = gp_refs[k][0].astype(F32)
                for p in range(1, n_parts):
                    g = g + gp_refs[k][p].astype(F32)
                m_new = ADAM_B1 * m_ref[...] + (1.0 - ADAM_B1) * g
                v_new = ADAM_B2 * v_ref[...] + (1.0 - ADAM_B2) * (g * g)
                g_ref[...] = g
                d_ref[...] = -ADAM_LR * ((m_new / c1) / (jnp.sqrt(v_new / c2) + ADAM_EPS) + ADAM_WD * w_ref[...])
                nm_ref[...] = m_new
                nv_ref[...] = v_new

    blk = pl.BlockSpec((None, tr, cols), lambda l, i: (l, i, 0))
    shp = jax.ShapeDtypeStruct((n_layers, n_rows, cols), F32)
    gp_specs = [pl.BlockSpec((n_parts, tr, cols), lambda l, i, k=k: (0, jnp.where(l == k, i, 0), 0))
                for k in range(n_layers)]
    return pl.pallas_call(body, grid=(n_layers, n_rows // tr), name=name, in_specs=gp_specs + [blk, blk, blk],
                          out_specs=[blk, blk, blk, blk], out_shape=[shp, shp, shp, shp],
                          compiler_params=_params("arbitrary", "arbitrary"))(*recvs, w, m, v)


def _pack_rows(arrays, cols, lead=0):
    flat = [a.reshape(a.shape[:lead] + (-1,)) for a in arrays]
    cat = jnp.concatenate(flat, axis=lead) if len(flat) > 1 else flat[0]
    n = cat.shape[lead]
    pad = (-n) % (cols * PACK_ROWS)
    if pad:
        cat = jnp.pad(cat, [(0, 0)] * lead + [(0, pad)])
    return cat.reshape(cat.shape[:lead] + (-1, cols))


def _unpack_rows(packed, shapes, lead=0):
    flat = packed.reshape(packed.shape[:lead] + (-1,))
    out, off = [], 0
    for s in shapes:
        n = math.prod(s)
        out.append(lax.slice_in_dim(flat, off, off + n, axis=lead).reshape(flat.shape[:lead] + tuple(s)))
        off += n
    return out


def kernel(x, ffn1_norm, ffn1_w_in, ffn1_w_out, mix_norm, ffn2_norm, ffn2_w_in, ffn2_w_out, final_norm, s5_w_in, s5_lam_re, s5_lam_im, s5_log_dt, s5_b_re, s5_b_im, s5_c_re, s5_c_im, s5_d, s5_w_out, sb_w_qkv, sb_w_out, lru_w_in, lru_conv_w, lru_conv_b, lru_w_a, lru_b_a, lru_w_x, lru_b_x, lru_lambda, lru_w_out, loss_target, m_ffn1_norm, m_ffn1_w_in, m_ffn1_w_out, m_mix_norm, m_ffn2_norm, m_ffn2_w_in, m_ffn2_w_out, m_final_norm, m_s5_w_in, m_s5_lam_re, m_s5_lam_im, m_s5_log_dt, m_s5_b_re, m_s5_b_im, m_s5_c_re, m_s5_c_im, m_s5_d, m_s5_w_out, m_sb_w_qkv, m_sb_w_out, m_lru_w_in, m_lru_conv_w, m_lru_conv_b, m_lru_w_a, m_lru_b_a, m_lru_w_x, m_lru_b_x, m_lru_lambda, m_lru_w_out, v_ffn1_norm, v_ffn1_w_in, v_ffn1_w_out, v_mix_norm, v_ffn2_norm, v_ffn2_w_in, v_ffn2_w_out, v_final_norm, v_s5_w_in, v_s5_lam_re, v_s5_lam_im, v_s5_log_dt, v_s5_b_re, v_s5_b_im, v_s5_c_re, v_s5_c_im, v_s5_d, v_s5_w_out, v_sb_w_qkv, v_sb_w_out, v_lru_w_in, v_lru_conv_w, v_lru_conv_b, v_lru_w_a, v_lru_b_a, v_lru_w_x, v_lru_b_x, v_lru_lambda, v_lru_w_out):
    local = dict(locals())
    W = {n: local[n] for n in WEIGHTS}
    M = {n: local["m_" + n] for n in WEIGHTS}
    V = {n: local["v_" + n] for n in WEIGHTS}

    h0 = x[0]
    target = loss_target[0]
    n_rows, dm = h0.shape
    depth = ffn1_norm.shape[0]

    ffn_seq = [(tag, layer) for layer in range(depth) for tag in ("ffn1", "ffn2")]

    def ffn_shards(tag, layer):
        return W[tag + "_w_in"][layer].astype(BF16), W[tag + "_w_out"][layer].astype(BF16)

    def ffn_views(wi, wo):
        return wi.reshape((2, N_DEV // 2) + wi.shape[1:]), wo.reshape((N_DEV // 2, 2) + wo.shape[1:])

    first_in, first_out = ffn_shards(*ffn_seq[0])
    ffn_w = {ffn_seq[0]: ffn_views(_all_gather("ag_first_w_in", first_in), _all_gather("ag_first_w_out", first_out))}

    def ffn_forward(pos, h_in):
        tag, layer = ffn_seq[pos]
        comms = [("gather", a) for a in ffn_shards(*ffn_seq[pos + 1])] if pos + 1 < len(ffn_seq) else []
        res = _ffn_fwd("%s_fwd_%d" % (tag, layer), h_in, W[tag + "_norm"][layer:layer + 1], *ffn_w[ffn_seq[pos]],
                       comms=comms)
        if comms:
            ffn_w[ffn_seq[pos + 1]] = ffn_views(res[1], res[2])
        return res[0]

    mix_shapes = [W[n].shape for n in MIXER_BIG]
    mix_g = _all_gather("ag_mixers", _pack_rows([W[n].astype(BF16) for n in MIXER_BIG], dm))
    full = {n: _unshard(a, SHARD_AXIS[n]) for n, a in zip(MIXER_BIG, _unpack_rows(mix_g, mix_shapes, lead=1))}
    small_shapes = [W[n].shape for n in SMALL_SHARDED]
    small_g = _all_gather("ag_small", _pack_rows([W[n] for n in SMALL_SHARDED], 128))
    full.update({n: _unshard(a, SHARD_AXIS[n])
                 for n, a in zip(SMALL_SHARDED, _unpack_rows(small_g, small_shapes, lead=1))})

    n_s5 = s5_w_in.shape[0]
    s5_groups = s5_lam_re.shape[1]
    heads = dm // SB_HEAD_DIM

    grads = {}
    saved = []
    h = h0

    for layer in range(depth):
        kind, j = layer % 3, layer // 3
        rec = {"h0": h}
        h = ffn_forward(2 * layer, h)
        rec["h1"] = h
        gain = mix_norm[layer:layer + 1]
        if kind == 0:
            (u,) = _mm_fwd("s5_in_%d" % layer, h, full["s5_w_in"][j], F32, gain=gain)
            pars = (s5_lam_re[j], s5_lam_im[j], s5_log_dt[j], s5_b_re[j], s5_b_im[j], s5_c_re[j], s5_c_im[j])
            mats, mats_vjp = jax.vjp(_s5_mats, *pars)
            apw, ar_fwd, ar_rev = _s5_powers(*pars[:3])
            ys, sprev = _s5_fwd("s5_core_%d" % layer, u, *mats[:3], apw, ar_fwd)
            d_skip = full["s5_d"][j:j + 1]
            (z,) = _rowwise("s5_gelu_%d" % layer, lambda i, nt, ys_v, u_v, d_v: ((_gelu(ys_v + d_v * u_v),), ()),
                            [('t', ys, dm, 0), ('t', u, dm, 0), ('b', d_skip)], [(dm, BF16)], [], n_rows,
                            _tile(n_rows, 512))
            h, vg = _mm_fwd("s5_out_%d" % layer, z, full["s5_w_out"][j], F32, resid=h, glu=True)
            rec.update(u=u, ys=ys, sprev=sprev, z=z, vg=vg, mats=mats, mats_vjp=mats_vjp, apw=apw,
                       ar_rev=ar_rev, d_skip=d_skip)
        elif kind == 1:
            (qkv,) = _mm_fwd("sb_in_%d" % layer, h, full["sb_w_qkv"][j], BF16, gain=gain)
            qkv_h = qkv.reshape(n_rows, 3, heads, SB_HEAD_DIM).transpose(1, 2, 0, 3)
            o = _sb_fwd("sb_attn_%d" % layer, qkv_h[0], qkv_h[1], qkv_h[2])
            o_flat = o.transpose(1, 0, 2).reshape(n_rows, dm).astype(BF16)
            (h,) = _mm_fwd("sb_out_%d" % layer, o_flat, full["sb_w_out"][j], F32, resid=h)
            rec.update(qkv_h=qkv_h, o=o, o_flat=o_flat)
        else:
            (gx,) = _mm_fwd("lru_in_%d" % layer, h, full["lru_w_in"][j], F32, gain=gain)
            wa, wx = full["lru_w_a"][j], full["lru_w_x"][j]
            ba, bx = full["lru_b_a"][j].reshape(1, dm), full["lru_b_x"][j].reshape(1, dm)
            lam_row = full["lru_lambda"][j:j + 1]
            xc, r, ig, a, gated = _lru_gates_fwd("lru_gates_%d" % layer, gx, full["lru_conv_w"][j],
                                                 full["lru_conv_b"][j:j + 1], wa, ba, wx, bx, lam_row)
            hs, y = _lru_scan_fwd("lru_scan_%d" % layer, a, gated, gx)
            (h,) = _mm_fwd("lru_out_%d" % layer, y, full["lru_w_out"][j], F32, resid=h)
            rec.update(gx=gx, xc=xc, r=r, ig=ig, a=a, hs=hs, y=y, wa=wa, wx=wx, lam_row=lam_row)
        rec["h2"] = h
        h = ffn_forward(2 * layer + 1, h)
        saved.append(rec)

    dh, err2, dgf = _loss_head("loss_head", h, final_norm.reshape(1, dm), target)
    loss = lax.psum(0.5 / dm * jnp.sum(err2), ("x", "y", "c"))
    grads["final_norm"] = dgf.reshape(final_norm.shape)

    per_layer = {n: [None] * depth for n in ("ffn1_norm", "mix_norm", "ffn2_norm")}
    mixer_grads = {}
    recv_ffn = {}
    pending = []

    def ffn_backward(tag, layer, x_in, dh_in):
        comms = {0: [("exchange", pending[1])], 1: [("exchange", pending[2])]} if pending else {}
        dx, dwi, dwo, dg, extra = _ffn_bwd("%s_bwd_%d" % (tag, layer), x_in, dh_in, W[tag + "_norm"][layer:layer + 1],
                                            *ffn_w[(tag, layer)], comms_by_block=comms)
        if pending:
            recv_ffn[pending[0]] = tuple(extra)
        pending[:] = [(tag, layer), dwi.reshape(N_DEV, -1, dm).astype(BF16), dwo.reshape(N_DEV, -1, dm).astype(BF16)]
        per_layer[tag + "_norm"][layer] = dg
        return dx

    def put(name, j, value, count):
        mixer_grads.setdefault(name, [None] * count)[j] = value

    for layer in reversed(range(depth)):
        kind, j = layer % 3, layer // 3
        rec = saved[layer]
        dh = ffn_backward("ffn2", layer, rec["h2"], dh)
        gain = mix_norm[layer:layer + 1]
        if kind == 0:
            dvg, = _rowwise("s5_glu_bwd_%d" % layer,
                            lambda i, nt, d_v, vg_v: ((jnp.concatenate(
                                [d_v * _sigmoid(vg_v[:, dm:]),
                                 d_v * vg_v[:, :dm] * _sigmoid(vg_v[:, dm:]) * (1.0 - _sigmoid(vg_v[:, dm:]))],
                                axis=1),), ()),
                            [('t', dh, dm, 0), ('t', rec["vg"], 2 * dm, 0)], [(2 * dm, BF16)], [], n_rows,
                            _tile(n_rows, 256))
            dz, dw_out = _mm_bwd("s5_out_bwd_%d" % layer, rec["z"], dvg, full["s5_w_out"][j])

            def gelu_bwd(i, nt, dz_v, ys_v, u_v, d_v):
                dy_v = dz_v * _gelu_grad(ys_v + d_v * u_v)
                return (dy_v,), (jnp.sum(dy_v * u_v, axis=0, keepdims=True),)

            dys, dd = _rowwise("s5_gelu_bwd_%d" % layer, gelu_bwd,
                               [('t', dz, dm, 0), ('t', rec["ys"], dm, 0), ('t', rec["u"], dm, 0),
                                ('b', rec["d_skip"])], [(dm, F32)], [(1, dm)], n_rows, _tile(n_rows, 512))
            m_, bm_, cm_, _ = rec["mats"]
            du_core, dm_m, dm_b, dm_c, d_a = _s5_bwd("s5_core_bwd_%d" % layer, rec["u"], dys, rec["sprev"],
                                                    m_, bm_, cm_, rec["apw"], rec["ar_rev"])
            dpars = rec["mats_vjp"]((dm_m, dm_b, dm_c, d_a.reshape(s5_groups, -1)))
            for nme, val in zip(("s5_lam_re", "s5_lam_im", "s5_log_dt", "s5_b_re", "s5_b_im", "s5_c_re", "s5_c_im"),
                                dpars):
                put(nme, j, val, n_s5)
            (du,) = _rowwise("s5_du_%d" % layer, lambda i, nt, a_v, dy_v, d_v: ((a_v + dy_v * d_v,), ()),
                             [('t', du_core, dm, 0), ('t', dys, dm, 0), ('b', rec["d_skip"])],
                             [(dm, BF16)], [], n_rows, _tile(n_rows, 512))
            dh, dw_in, dgm = _mm_bwd("s5_in_bwd_%d" % layer, rec["h1"], du, full["s5_w_in"][j], gain=gain, dres=dh)
            put("s5_d", j, dd[0], n_s5)
            put("s5_w_out", j, dw_out, n_s5)
            put("s5_w_in", j, dw_in, n_s5)
        elif kind == 1:
            do_flat, dw_out = _mm_bwd("sb_out_bwd_%d" % layer, rec["o_flat"], dh, full["sb_w_out"][j])
            do = do_flat.reshape(n_rows, heads, SB_HEAD_DIM).transpose(1, 0, 2)
            qkv_h = rec["qkv_h"]
            dq, dk, dv = _sb_bwd("sb_attn_bwd_%d" % layer, qkv_h[0], qkv_h[1], qkv_h[2], rec["o"], do)
            dqkv = jnp.stack([dq, dk, dv]).transpose(2, 0, 1, 3).reshape(n_rows, 3 * dm).astype(BF16)
            dh, dw_in, dgm = _mm_bwd("sb_in_bwd_%d" % layer, rec["h1"], dqkv, full["sb_w_qkv"][j], gain=gain, dres=dh)
            put("sb_w_out", j, dw_out, 1)
            put("sb_w_qkv", j, dw_in, 1)
        else:
            dy, dw_out = _mm_bwd("lru_out_bwd_%d" % layer, rec["y"], dh, full["lru_w_out"][j])
            lam_t = _lru_scan_bwd("lru_scan_bwd_%d" % layer, rec["a"], dy, rec["gx"])
            dxc, dwa, dwx, dba, dbx, dsp = _lru_gates_bwd("lru_gates_bwd_%d" % layer, lam_t, rec["hs"], rec["xc"],
                                                          rec["r"], rec["ig"], rec["a"], rec["wa"], rec["wx"],
                                                          rec["lam_row"])
            dgx, dcw, dcb = _lru_conv_bwd("lru_conv_bwd_%d" % layer, dxc, rec["gx"], dy, rec["hs"],
                                          full["lru_conv_w"][j])
            dh, dw_in, dgm = _mm_bwd("lru_in_bwd_%d" % layer, rec["h1"], dgx, full["lru_w_in"][j], gain=gain, dres=dh)
            nb = rec["wa"].shape[0]
            put("lru_w_out", j, dw_out, 1)
            put("lru_w_in", j, dw_in, 1)
            put("lru_w_a", j, dwa.reshape(rec["wa"].shape), 1)
            put("lru_w_x", j, dwx.reshape(rec["wx"].shape), 1)
            put("lru_b_a", j, dba.reshape(nb, -1), 1)
            put("lru_b_x", j, dbx.reshape(nb, -1), 1)
            put("lru_conv_w", j, dcw[:LRU_CONV], 1)
            put("lru_conv_b", j, dcb[0], 1)
            put("lru_lambda", j, (dsp * -_sigmoid(-rec["lam_row"]))[0], 1)
        per_layer["mix_norm"][layer] = dgm
        dh = ffn_backward("ffn1", layer, rec["h0"], dh)

    grad_x = dh[None]
    for n in ("ffn1_norm", "mix_norm", "ffn2_norm"):
        grads[n] = jnp.concatenate(per_layer[n], axis=0)
    for n, parts in mixer_grads.items():
        grads[n] = jnp.stack(parts)

    out_g, out_d, out_m, out_v = {}, {}, {}, {}

    def finish(names, res, shapes):
        for n, g_, d_, m_, v_ in zip(names, *[_unpack_rows(t, shapes) for t in res]):
            out_g[n], out_d[n], out_m[n], out_v[n] = g_, d_, m_, v_

    recv_ffn[pending[0]] = (_exchange("xchg_last_w_in", pending[1]), _exchange("xchg_last_w_out", pending[2]))
    for tag in ("ffn1", "ffn2"):
        for which, n in enumerate((tag + "_w_in", tag + "_w_out")):
            res = _adamw_layers("adamw_" + n, [recv_ffn[(tag, layer)][which] for layer in range(depth)],
                                *[t[n].reshape(depth, -1, dm) for t in (W, M, V)])
            out_g[n], out_d[n], out_m[n], out_v[n] = [r.reshape(W[n].shape) for r in res]

    send = _pack_rows([_shard_blocks(grads[n], SHARD_AXIS[n]).astype(BF16) for n in MIXER_BIG], dm, lead=1)
    recv = _exchange("xchg_mixers", send)
    finish(MIXER_BIG, _adamw("adamw_mixers", recv, *[_pack_rows([t[n] for n in MIXER_BIG], dm) for t in (W, M, V)]),
           mix_shapes)

    small_names = REPLICATED + SMALL_SHARDED
    small_full_shapes = [grads[n].shape for n in small_names]
    parts = _all_gather("ag_small_grads", _pack_rows([grads[n] for n in small_names], 128))
    zero = jnp.zeros(parts.shape[1:], F32)
    summed = _adamw("sum_small_grads", parts, zero, zero, zero)[0]
    small_sum = dict(zip(small_names, _unpack_rows(summed, small_full_shapes)))
    me = 4 * lax.axis_index("x") + 2 * lax.axis_index("y") + lax.axis_index("c")
    rep_shapes = [W[n].shape for n in REPLICATED]
    g_rep = _pack_rows([small_sum[n] for n in REPLICATED], 128)[None]
    finish(REPLICATED, _adamw("adamw_replicated", g_rep, *[_pack_rows([t[n] for n in REPLICATED], 128)
                                                           for t in (W, M, V)]), rep_shapes)
    g_loc = []
    for n in SMALL_SHARDED:
        ax = SHARD_AXIS[n]
        size = W[n].shape[ax]
        g_loc.append(lax.dynamic_slice_in_dim(small_sum[n], me * size, size, axis=ax))
    finish(SMALL_SHARDED, _adamw("adamw_small", _pack_rows(g_loc, 128)[None],
                                 *[_pack_rows([t[n] for n in SMALL_SHARDED], 128) for t in (W, M, V)]), small_shapes)

    return (loss, grad_x, *[out_g[n] for n in WEIGHTS], *[out_d[n] for n in WEIGHTS],
            *[out_m[n] for n in WEIGHTS], *[out_v[n] for n in WEIGHTS])
```

```python
import functools
import math

import jax
import jax.numpy as jnp
from jax import lax
from jax.experimental import pallas as pl
from jax.experimental.pallas import tpu as pltpu

F32 = jnp.float32
BF16 = jnp.bfloat16
HI = lax.Precision.HIGHEST
S5_PREC = lax.Precision.HIGH
MESH = pl.DeviceIdType.MESH

N_DEV = 8
RMS_EPS = 1e-6
S5_GROUP = 16
S5_CHUNK = 16
S5_OCTET = 128 // S5_GROUP
S5_REGROUP_ROWS = 32
SB_HEAD_DIM = 64
SB_UNDERFLOW = -104.0
SB_HEADS_PER_STEP = 2
SB_STRIP = 32
LRU_CONV = 4
LRU_C = 8.0
ADAM_LR, ADAM_B1, ADAM_B2, ADAM_EPS, ADAM_WD, ADAM_STEP = 0.001, 0.9, 0.999, 1e-08, 0.01, 10
VMEM_LIMIT_BYTES = 56 * 1024 * 1024
SUBLANES = 8
ROW_STRIP = 16
PACK_ROWS = 256
ADAMW_LAYER_ROWS = 192

NN = (((1,), (0,)), ((), ()))
NT = (((1,), (1,)), ((), ()))
TN = (((0,), (0,)), ((), ()))

SHARD_AXIS = dict(
    ffn1_w_in=2, ffn1_w_out=1, ffn2_w_in=2, ffn2_w_out=1, s5_w_in=1, s5_d=1, s5_w_out=2, sb_w_qkv=2, sb_w_out=1,
    lru_w_in=2, lru_conv_w=2, lru_conv_b=1, lru_w_a=2, lru_b_a=2, lru_w_x=2, lru_b_x=2, lru_lambda=1, lru_w_out=1)
MIXER_BIG = ("s5_w_in", "s5_w_out", "sb_w_qkv", "sb_w_out", "lru_w_in", "lru_w_a", "lru_w_x", "lru_w_out")
SMALL_SHARDED = ("s5_d", "lru_conv_w", "lru_conv_b", "lru_b_a", "lru_b_x", "lru_lambda")
REPLICATED = ("ffn1_norm", "mix_norm", "ffn2_norm", "final_norm", "s5_lam_re", "s5_lam_im", "s5_log_dt",
              "s5_b_re", "s5_b_im", "s5_c_re", "s5_c_im")
WEIGHTS = ("ffn1_norm", "ffn1_w_in", "ffn1_w_out", "mix_norm", "ffn2_norm", "ffn2_w_in", "ffn2_w_out", "final_norm",
           "s5_w_in", "s5_lam_re", "s5_lam_im", "s5_log_dt", "s5_b_re", "s5_b_im", "s5_c_re", "s5_c_im", "s5_d",
           "s5_w_out", "sb_w_qkv", "sb_w_out", "lru_w_in", "lru_conv_w", "lru_conv_b", "lru_w_a", "lru_b_a",
           "lru_w_x", "lru_b_x", "lru_lambda", "lru_w_out")


def _dot(a, b, dims=NN, prec=None):
    return lax.dot_general(a, b, dims, precision=prec, preferred_element_type=F32)


def _params(*sem):
    return pltpu.CompilerParams(dimension_semantics=sem, vmem_limit_bytes=VMEM_LIMIT_BYTES)


def _tile(n, pref):
    return min(pref, n)


def _sigmoid(x):
    return jax.nn.sigmoid(x)


def _softplus(x):
    return jnp.maximum(x, 0.0) + jnp.log(1.0 + jnp.exp(-jnp.abs(x)))


_GELU_C = math.sqrt(2.0 / math.pi)


def _gelu(x):
    return 0.5 * x * (1.0 + jnp.tanh(_GELU_C * (x + 0.044715 * x * x * x)))


def _gelu_grad(x):
    t = jnp.tanh(_GELU_C * (x + 0.044715 * x * x * x))
    return 0.5 * (1.0 + t) + 0.5 * x * (1.0 - t * t) * _GELU_C * (1.0 + 3.0 * 0.044715 * x * x)


def _rms(x):
    r = lax.rsqrt(jnp.mean(x * x, axis=1, keepdims=True) + RMS_EPS)
    return r, x * r


def _rms_bwd(dhn, xhat, r, g):
    dxhat = dhn * g
    return r * (dxhat - xhat * jnp.mean(dxhat * xhat, axis=1, keepdims=True))


def _one_minus_a2_sqrt(log_a):
    t = jnp.tanh(log_a)
    return jnp.sqrt(-2.0 * t / (1.0 - t))


def _shift_down(cur, prev8, k, first):
    if k == 0:
        return cur
    row8 = lax.broadcasted_iota(jnp.int32, prev8.shape, 0)
    rolled = pltpu.roll(cur, k, 0)
    edge = jnp.where(first, 0.0, pltpu.roll(prev8, k, 0))
    top = jnp.where(row8 < k, edge, rolled[0:SUBLANES])
    return jnp.concatenate([top, rolled[SUBLANES:]], axis=0)


def _shift_up(cur, next8, k, last):
    if k == 0:
        return cur
    tm = cur.shape[0]
    row8 = lax.broadcasted_iota(jnp.int32, next8.shape, 0)
    rolled = pltpu.roll(cur, tm - k, 0)
    edge = jnp.where(last, 0.0, pltpu.roll(next8, SUBLANES - k, 0))
    bottom = jnp.where(row8 >= SUBLANES - k, edge, rolled[tm - SUBLANES:tm])
    return jnp.concatenate([rolled[:tm - SUBLANES], bottom], axis=0)


def _rowwise(name, fn, ins, out_tiled, out_acc, n_rows, tm, reverse=False, strip=None):
    nt = n_rows // tm
    per8 = tm // SUBLANES
    n8 = n_rows // SUBLANES
    n_in, n_ot = len(ins), len(out_tiled)

    def pos(i):
        return nt - 1 - i if reverse else i

    in_specs, args = [], []
    for spec in ins:
        kind, arr = spec[0], spec[1]
        args.append(arr)
        if kind == 'b':
            in_specs.append(pl.BlockSpec(arr.shape, lambda i, nd=arr.ndim: (0,) * nd))
        elif kind == 't':
            in_specs.append(pl.BlockSpec((tm, spec[2]), lambda i, cb=spec[3]: (pos(i), cb)))
        elif kind == 'p':
            in_specs.append(pl.BlockSpec((SUBLANES, spec[2]),
                                         lambda i, cb=spec[3]: (jnp.maximum(pos(i) * per8 - 1, 0), cb)))
        else:
            in_specs.append(pl.BlockSpec((SUBLANES, spec[2]),
                                         lambda i, cb=spec[3]: (jnp.minimum((pos(i) + 1) * per8, n8 - 1), cb)))

    def body(*refs):
        i = pl.program_id(0)
        outs = refs[n_in:]
        if strip is None:
            touts, aouts = fn(pos(i), nt, *[r[...] for r in refs[:n_in]])
            for r, v in zip(outs[:n_ot], touts):
                r[...] = v.astype(r.dtype)
        else:
            def step(k, acc):
                rows = pl.ds(pl.multiple_of(k * strip, strip), strip)
                vals = [r[rows, :] if spec[0] == 't' else r[...] for r, spec in zip(refs[:n_in], ins)]
                touts, part = fn(pos(i), nt, *vals)
                for r, v in zip(outs[:n_ot], touts):
                    r[rows, :] = v.astype(r.dtype)
                return tuple(a + p for a, p in zip(acc, part))

            aouts = lax.fori_loop(0, tm // strip, step, tuple(jnp.zeros((r, n), F32) for r, n in out_acc))
        if out_acc:
            @pl.when(i == 0)
            def _():
                for r in outs[n_ot:]:
                    r[...] = jnp.zeros(r.shape, r.dtype)
            for r, v in zip(outs[n_ot:], aouts):
                r[...] += v

    out_specs = [pl.BlockSpec((tm, n), lambda i: (pos(i), 0)) for n, _ in out_tiled]
    out_specs += [pl.BlockSpec((r, n), lambda i: (0, 0)) for r, n in out_acc]
    out_shape = [jax.ShapeDtypeStruct((n_rows, n), dt) for n, dt in out_tiled]
    out_shape += [jax.ShapeDtypeStruct((r, n), F32) for r, n in out_acc]
    return pl.pallas_call(body, grid=(nt,), in_specs=in_specs, out_specs=out_specs, out_shape=out_shape, name=name,
                          compiler_params=_params("arbitrary"))(*args)


def _all_gather(name, block):
    def body(x_ref, out_ref, send_sems, recv_sems, local_sem):
        x, y, c = lax.axis_index("x"), lax.axis_index("y"), lax.axis_index("c")
        me, sibling = (x, y, c), (x, y, 1 - c)
        chips = [(1 - x, y), (x, 1 - y), (1 - x, 1 - y)]

        def rows(px, py, pc):
            return out_ref.at[4 * px + 2 * py + pc]

        def copy(k, blk, to, src=None):
            return pltpu.make_async_remote_copy(
                src_ref=rows(*blk) if src is None else src, dst_ref=rows(*blk),
                send_sem=send_sems.at[k], recv_sem=recv_sems.at[k], device_id=to, device_id_type=MESH)

        mine = pltpu.make_async_copy(x_ref, rows(*me), local_sem)
        mine.start()
        first = [copy(0, me, sibling, src=x_ref)]
        first += [copy(1 + j, me, (*chip, c), src=x_ref) for j, chip in enumerate(chips)]
        for cp in first:
            cp.start()
        passed = [copy(4 + j, (*chip, c), sibling) for j, chip in enumerate(chips)]
        for j, chip in enumerate(chips):
            copy(1 + j, (*chip, c), me).wait_recv()
            passed[j].start()
        copy(0, sibling, me).wait_recv()
        for j, chip in enumerate(chips):
            copy(4 + j, (*chip, 1 - c), me).wait_recv()
        for cp in first + passed:
            cp.wait_send()
        mine.wait()

    return pl.pallas_call(
        body, name=name, out_shape=jax.ShapeDtypeStruct((N_DEV,) + block.shape, block.dtype),
        in_specs=[pl.BlockSpec(memory_space=pl.ANY)], out_specs=pl.BlockSpec(memory_space=pl.ANY),
        scratch_shapes=[pltpu.SemaphoreType.DMA((7,)), pltpu.SemaphoreType.DMA((7,)), pltpu.SemaphoreType.DMA(())],
    )(block)


def _exchange(name, send):
    def body(s_ref, r_ref, send_sems, recv_sems, local_sem):
        x, y, c = lax.axis_index("x"), lax.axis_index("y"), lax.axis_index("c")
        me = 4 * x + 2 * y + c
        mine = pltpu.make_async_copy(s_ref.at[me], r_ref.at[me], local_sem)
        mine.start()
        copies = []
        for k in range(1, N_DEV):
            dx, dy, dc = (k >> 2) & 1, (k >> 1) & 1, k & 1
            px = 1 - x if dx else x
            py = 1 - y if dy else y
            pc = 1 - c if dc else c
            peer = 4 * px + 2 * py + pc
            copies.append((pltpu.make_async_remote_copy(
                src_ref=s_ref.at[peer], dst_ref=r_ref.at[me], send_sem=send_sems.at[k - 1],
                recv_sem=recv_sems.at[k - 1], device_id=(px, py, pc), device_id_type=MESH), peer))
        for cp, _ in copies:
            cp.start()
        for k, (cp, peer) in enumerate(copies):
            pltpu.make_async_remote_copy(
                src_ref=s_ref.at[peer], dst_ref=r_ref.at[peer], send_sem=send_sems.at[k], recv_sem=recv_sems.at[k],
                device_id=(x, y, c), device_id_type=MESH).wait_recv()
        for cp, _ in copies:
            cp.wait_send()
        mine.wait()

    return pl.pallas_call(
        body, name=name, out_shape=jax.ShapeDtypeStruct(send.shape, send.dtype),
        in_specs=[pl.BlockSpec(memory_space=pl.ANY)], out_specs=pl.BlockSpec(memory_space=pl.ANY),
        scratch_shapes=[pltpu.SemaphoreType.DMA((7,)), pltpu.SemaphoreType.DMA((7,)), pltpu.SemaphoreType.DMA(())],
    )(send)


def _direct_copies(kind, s_ref, r_ref, send_sems, recv_sems, local_sem):
    x, y, c = lax.axis_index("x"), lax.axis_index("y"), lax.axis_index("c")
    me = 4 * x + 2 * y + c

    def src(p):
        return s_ref if kind == "gather" else s_ref.at[p]

    local = pltpu.make_async_copy(src(me), r_ref.at[me], local_sem)
    sends, recvs = [], []
    for k in range(1, N_DEV):
        px = 1 - x if (k >> 2) & 1 else x
        py = 1 - y if (k >> 1) & 1 else y
        pc = 1 - c if k & 1 else c
        peer = 4 * px + 2 * py + pc
        sends.append(pltpu.make_async_remote_copy(
            src_ref=src(peer), dst_ref=r_ref.at[me], send_sem=send_sems.at[k - 1], recv_sem=recv_sems.at[k - 1],
            device_id=(px, py, pc), device_id_type=MESH))
        recvs.append(pltpu.make_async_remote_copy(
            src_ref=src(peer), dst_ref=r_ref.at[peer], send_sem=send_sems.at[k - 1], recv_sem=recv_sems.at[k - 1],
            device_id=(x, y, c), device_id_type=MESH))
    return local, sends, recvs


def _call(body, *, grid, in_specs, out_specs, out_shape, name, args, scratch_shapes=(), semantics=None, comms=()):
    single = not isinstance(out_shape, (list, tuple))
    out_shape = [out_shape] if single else list(out_shape)
    out_specs = [out_specs] if single else list(out_specs)
    if not comms:
        res = pl.pallas_call(body, grid=grid, in_specs=in_specs, out_specs=out_specs, out_shape=out_shape, name=name,
                             scratch_shapes=list(scratch_shapes),
                             compiler_params=_params(*(semantics or ("arbitrary",) * len(grid))))(*args)
        return res[0] if single else res
    n_in, n_out, n_scr, n_c = len(args), len(out_shape), len(scratch_shapes), len(comms)

    def hosted(*refs):
        ins, srcs = refs[:n_in], refs[n_in:n_in + n_c]
        outs = refs[n_in + n_c:n_in + n_c + n_out]
        dsts = refs[n_in + n_c + n_out:n_in + 2 * n_c + n_out]
        scr = refs[n_in + 2 * n_c + n_out:n_in + 2 * n_c + n_out + n_scr]
        sems = refs[n_in + 2 * n_c + n_out + n_scr:]
        first = functools.reduce(jnp.logical_and, [pl.program_id(d) == 0 for d in range(len(grid))])
        last = functools.reduce(jnp.logical_and, [pl.program_id(d) == grid[d] - 1 for d in range(len(grid))])
        plans = [_direct_copies(comms[i][0], srcs[i], dsts[i], *sems[3 * i:3 * i + 3]) for i in range(n_c)]

        @pl.when(first)
        def _():
            for local, sends, _ in plans:
                local.start()
                for cp in sends:
                    cp.start()

        body(*ins, *outs, *scr)

        @pl.when(last)
        def _():
            for local, sends, recvs in plans:
                for cp in recvs:
                    cp.wait_recv()
                for cp in sends:
                    cp.wait_send()
                local.wait()

    any_spec = pl.BlockSpec(memory_space=pl.ANY)
    comm_shapes = [jax.ShapeDtypeStruct(((N_DEV,) + a.shape) if kind == "gather" else a.shape, a.dtype)
                   for kind, a in comms]
    sem_shapes = []
    for _ in comms:
        sem_shapes += [pltpu.SemaphoreType.DMA((7,)), pltpu.SemaphoreType.DMA((7,)), pltpu.SemaphoreType.DMA(())]
    res = pl.pallas_call(
        hosted, grid=grid, in_specs=list(in_specs) + [any_spec] * n_c, out_specs=out_specs + [any_spec] * n_c,
        out_shape=out_shape + comm_shapes, name=name, scratch_shapes=list(scratch_shapes) + sem_shapes,
        compiler_params=_params(*(("arbitrary",) * len(grid))))(*args, *[a for _, a in comms])
    return res


def _unshard(gathered, axis):
    local = gathered.shape[1:]
    full = jnp.moveaxis(gathered, 0, axis)
    return full.reshape(local[:axis] + (N_DEV * local[axis],) + local[axis + 1:])


def _shard_blocks(full, axis):
    s = full.shape
    cut = full.reshape(s[:axis] + (N_DEV, s[axis] // N_DEV) + s[axis + 1:])
    return jnp.moveaxis(cut, axis, 0)


def _mm_fwd(name, a, w, out_dtype, gain=None, resid=None, glu=False):
    n_rows, k = a.shape
    n = w.shape[1]
    tm = _tile(n_rows, 512)
    n_out = n // 2 if glu else n

    def body(*refs):
        it = iter(refs)
        a_ref, w_ref = next(it), next(it)
        g_ref = next(it) if gain is not None else None
        r_ref = next(it) if resid is not None else None
        outs = list(it)
        av = a_ref[...]
        if g_ref is not None:
            _, xhat = _rms(av)
            av = xhat * g_ref[...]
        res = _dot(av.astype(BF16), w_ref[...])
        if glu:
            outs[1][...] = res.astype(outs[1].dtype)
            res = res[:, :n_out] * _sigmoid(res[:, n_out:])
        if r_ref is not None:
            res = res + r_ref[...]
        outs[0][...] = res.astype(outs[0].dtype)

    args = [a, w]
    in_specs = [pl.BlockSpec((tm, k), lambda i: (i, 0)), pl.BlockSpec((k, n), lambda i: (0, 0))]
    if gain is not None:
        args.append(gain)
        in_specs.append(pl.BlockSpec((1, k), lambda i: (0, 0)))
    if resid is not None:
        args.append(resid)
        in_specs.append(pl.BlockSpec((tm, n_out), lambda i: (i, 0)))
    out_shape = [jax.ShapeDtypeStruct((n_rows, n_out), out_dtype)]
    out_specs = [pl.BlockSpec((tm, n_out), lambda i: (i, 0))]
    if glu:
        out_shape.append(jax.ShapeDtypeStruct((n_rows, n), F32))
        out_specs.append(pl.BlockSpec((tm, n), lambda i: (i, 0)))
    return pl.pallas_call(body, grid=(n_rows // tm,), in_specs=in_specs, out_specs=out_specs, out_shape=out_shape,
                          name=name, compiler_params=_params("parallel"))(*args)


def _mm_bwd(name, a, d, w, gain=None, dres=None):
    n_rows, k = a.shape
    n = w.shape[1]
    tm = _tile(n_rows, 512)

    def body(*refs):
        it = iter(refs)
        a_ref, d_ref, w_ref = next(it), next(it), next(it)
        g_ref = next(it) if gain is not None else None
        r_ref = next(it) if gain is not None else None
        da_ref, dw_ref = next(it), next(it)
        dg_ref = next(it) if gain is not None else None
        i = pl.program_id(0)

        @pl.when(i == 0)
        def _():
            dw_ref[...] = jnp.zeros(dw_ref.shape, F32)
            if dg_ref is not None:
                dg_ref[...] = jnp.zeros(dg_ref.shape, F32)

        av = a_ref[...]
        dv = d_ref[...].astype(BF16)
        if g_ref is not None:
            r, xhat = _rms(av)
            ab = (xhat * g_ref[...]).astype(BF16)
        else:
            ab = av.astype(BF16)
        dw_ref[...] += _dot(ab, dv, TN)
        da = _dot(dv, w_ref[...], NT)
        if g_ref is not None:
            dg_ref[...] += jnp.sum(da * xhat, axis=0, keepdims=True)
            da = r_ref[...] + _rms_bwd(da, xhat, r, g_ref[...])
        da_ref[...] = da.astype(da_ref.dtype)

    args = [a, d, w]
    in_specs = [pl.BlockSpec((tm, k), lambda i: (i, 0)), pl.BlockSpec((tm, n), lambda i: (i, 0)),
                pl.BlockSpec((k, n), lambda i: (0, 0))]
    out_shape = [jax.ShapeDtypeStruct((n_rows, k), F32), jax.ShapeDtypeStruct((k, n), F32)]
    out_specs = [pl.BlockSpec((tm, k), lambda i: (i, 0)), pl.BlockSpec((k, n), lambda i: (0, 0))]
    if gain is not None:
        args += [gain, dres]
        in_specs += [pl.BlockSpec((1, k), lambda i: (0, 0)), pl.BlockSpec((tm, k), lambda i: (i, 0))]
        out_shape.append(jax.ShapeDtypeStruct((1, k), F32))
        out_specs.append(pl.BlockSpec((1, k), lambda i: (0, 0)))
    return pl.pallas_call(body, grid=(n_rows // tm,), in_specs=in_specs, out_specs=out_specs, out_shape=out_shape,
                          name=name, compiler_params=_params("arbitrary"))(*args)


def _ffn_fwd(name, x, gain, wi, wo, comms=()):
    n_rows, dm = x.shape
    _, nj, _, fb = wi.shape
    tm = _tile(n_rows, 512)

    def body(x_ref, g_ref, wi_ref, wo_ref, y_ref):
        xv = x_ref[...]
        _, xhat = _rms(xv)
        hn = (xhat * g_ref[...]).astype(BF16)
        acc = jnp.zeros((tm, dm), F32)
        for j in range(nj):
            gate = _dot(hn, wi_ref[0, j])
            up = _dot(hn, wi_ref[1, j])
            act = (gate * _sigmoid(gate) * up).astype(BF16)
            acc = acc + _dot(act, wo_ref[j].reshape(fb, dm))
        y_ref[...] = xv + 0.5 * acc

    return _call(
        body, grid=(n_rows // tm,), name=name, args=[x, gain, wi, wo], comms=comms,
        in_specs=[pl.BlockSpec((tm, dm), lambda i: (i, 0)), pl.BlockSpec((1, dm), lambda i: (0, 0)),
                  pl.BlockSpec((2, nj, dm, fb), lambda i: (0, 0, 0, 0)),
                  pl.BlockSpec((nj, 2, fb // 2, dm), lambda i: (0, 0, 0, 0))],
        out_specs=[pl.BlockSpec((tm, dm), lambda i: (i, 0))],
        out_shape=[jax.ShapeDtypeStruct((n_rows, dm), F32)])


def _ffn_bwd_block(name, x, dy, gain, wi, wo, j, acc, comms=()):
    n_rows, dm = x.shape
    _, nj, _, fb = wi.shape
    tm = _tile(n_rows, 512)
    last = j == nj - 1

    def body(*refs):
        it = iter(refs)
        x_ref, dy_ref, g_ref, wi_ref, wo_ref = next(it), next(it), next(it), next(it), next(it)
        acc_ref = next(it) if acc is not None else None
        out_ref, dwi_ref, dwo_ref = next(it), next(it), next(it)
        dg_ref = next(it) if last else None
        i = pl.program_id(0)

        @pl.when(i == 0)
        def _():
            dwi_ref[...] = jnp.zeros(dwi_ref.shape, F32)
            dwo_ref[...] = jnp.zeros(dwo_ref.shape, F32)
            if last:
                dg_ref[...] = jnp.zeros(dg_ref.shape, F32)

        xv, dyv, g = x_ref[...], dy_ref[...], g_ref[...]
        r, xhat = _rms(xv)
        hn = (xhat * g).astype(BF16)
        wg, wu, wob = wi_ref[0], wi_ref[1], wo_ref[...].reshape(fb, dm)
        gate = _dot(hn, wg)
        up = _dot(hn, wu)
        s = _sigmoid(gate)
        silu = gate * s
        act = (silu * up).astype(BF16)
        dout = (0.5 * dyv).astype(BF16)
        dact = _dot(dout, wob, NT)
        dwo_ref[...] += _dot(act, dout, TN)
        dgate = (dact * up * (s * (1.0 + gate * (1.0 - s)))).astype(BF16)
        dup = (dact * silu).astype(BF16)
        dwi_ref[0] += _dot(hn, dgate, TN)
        dwi_ref[1] += _dot(hn, dup, TN)
        tot = _dot(dgate, wg, NT) + _dot(dup, wu, NT)
        if acc_ref is not None:
            tot = tot + acc_ref[...]
        if last:
            out_ref[...] = dyv + _rms_bwd(tot, xhat, r, g)
            dg_ref[...] += jnp.sum(tot * xhat, axis=0, keepdims=True)
        else:
            out_ref[...] = tot

    tok = pl.BlockSpec((tm, dm), lambda i: (i, 0))
    args = [x, dy, gain, wi, wo]
    in_specs = [tok, tok, pl.BlockSpec((1, dm), lambda i: (0, 0)),
                pl.BlockSpec((2, None, dm, fb), lambda i: (0, j, 0, 0)),
                pl.BlockSpec((None, 2, fb // 2, dm), lambda i: (j, 0, 0, 0))]
    if acc is not None:
        args.append(acc)
        in_specs.append(tok)
    out_specs = [tok, pl.BlockSpec((2, dm, fb), lambda i: (0, 0, 0)), pl.BlockSpec((fb, dm), lambda i: (0, 0))]
    out_shape = [jax.ShapeDtypeStruct((n_rows, dm), F32), jax.ShapeDtypeStruct((2, dm, fb), F32),
                 jax.ShapeDtypeStruct((fb, dm), F32)]
    if last:
        out_specs.append(pl.BlockSpec((1, dm), lambda i: (0, 0)))
        out_shape.append(jax.ShapeDtypeStruct((1, dm), F32))
    return _call(body, grid=(n_rows // tm,), name=name, in_specs=in_specs, out_specs=out_specs,
                 out_shape=out_shape, args=args, comms=comms)


def _ffn_bwd(name, x, dy, gain, wi, wo, comms_by_block=None):
    nj = wi.shape[1]
    acc, dwi, dwo, extra = None, [], [], []
    for j in range(nj):
        comms = (comms_by_block or {}).get(j, ())
        res = _ffn_bwd_block("%s_%d" % (name, j), x, dy, gain, wi, wo, j, acc, comms)
        n_own = 4 if j == nj - 1 else 3
        acc = res[0]
        dwi.append(res[1])
        dwo.append(res[2])
        extra += list(res[n_own:])
        dgain = res[3] if j == nj - 1 else None
    return acc, jnp.stack(dwi, axis=1), jnp.stack(dwo, axis=0), dgain, extra


def _scan8(a, x, reverse):
    row = lax.broadcasted_iota(jnp.int32, a.shape, 0)
    for k in (1, 2, 4):
        if reverse:
            keep = row < SUBLANES - k
            a_s, x_s = pltpu.roll(a, SUBLANES - k, 0), pltpu.roll(x, SUBLANES - k, 0)
        else:
            keep = row >= k
            a_s, x_s = pltpu.roll(a, k, 0), pltpu.roll(x, k, 0)
        x = a * jnp.where(keep, x_s, 0.0) + x
        a = a * jnp.where(keep, a_s, 1.0)
    return a, x


def _scan_tile(a_ref, x_ref, h_ref, carry, reverse, rows):
    groups = rows // SUBLANES

    def step(n, c):
        gidx = groups - 1 - n if reverse else n
        sl = pl.ds(pl.multiple_of(gidx * SUBLANES, SUBLANES), SUBLANES)
        a_cum, h0 = _scan8(a_ref[sl, :], x_ref[sl, :], reverse)
        h = a_cum * c + h0
        h_ref[sl, :] = h
        return h[0:1] if reverse else h[SUBLANES - 1:SUBLANES]

    return lax.fori_loop(0, groups, step, carry)


def _s5_mats(lam_re, lam_im, log_dt, b_re, b_im, c_re, c_im):
    lc = S5_CHUNK
    groups, p = lam_re.shape
    h = b_re.shape[-1]
    lam = lax.complex(lam_re, lam_im)
    lam_dt = lam * jnp.exp(log_dt)[:, None]
    lam_bar = jnp.exp(lam_dt)
    b_bar = ((lam_bar - 1.0) / lam)[:, :, None] * lax.complex(b_re, b_im)
    c = lax.complex(c_re, c_im)
    pw = jnp.exp(lam_dt[None] * jnp.arange(lc + 1, dtype=F32)[:, None, None])
    resp = jnp.einsum('ghp,tgp,gpk->tghk', c, pw[:lc], b_bar, precision=HI).real
    s_idx = jnp.arange(lc)[:, None]
    u_idx = jnp.arange(lc)[None, :]
    onehot = (jnp.arange(lc)[:, None, None] == (u_idx - s_idx)[None]).astype(F32)
    m = jnp.einsum('tghk,tsu->gskuh', resp, onehot, precision=HI).reshape(groups, lc * h, lc * h)
    w = pw[lc - 1::-1][:lc].transpose(1, 0, 2)[:, :, None, :] * b_bar.transpose(0, 2, 1)[:, None]
    bm = jnp.concatenate([w.real, w.imag], axis=-1).reshape(groups, lc * h, 2 * p)
    v = c[:, None] * pw[1:lc + 1].transpose(1, 0, 2)[:, :, None, :]
    v = v.transpose(0, 3, 1, 2)
    cm = jnp.concatenate([v.real, -v.imag], axis=1).reshape(groups, 2 * p, lc * h)
    a = jnp.concatenate([pw[lc].real, pw[lc].imag], axis=-1)
    return m, bm, cm, a


def _s5_powers(lam_re, lam_im, log_dt):
    lam_dt = lax.complex(lam_re, lam_im) * jnp.exp(log_dt)[:, None]
    pw = jnp.exp(lam_dt[None] * (S5_CHUNK * jnp.arange(1, 9, dtype=F32))[:, None, None])

    def c1(z):
        return jnp.concatenate([z.real, z.real], axis=-1).reshape(z.shape[0], -1)

    def c2(z):
        return jnp.concatenate([-z.imag, z.imag], axis=-1).reshape(z.shape[0], -1)

    p1, p2 = c1(pw), c2(pw)
    apw = jnp.stack([p1[0], p2[0], p1[1], p2[1], p1[3], p2[3], jnp.zeros_like(p1[0]), jnp.zeros_like(p1[0])])
    fwd = jnp.concatenate([p1, p2], axis=0)
    rev = jnp.concatenate([c1(pw[::-1]), c2(pw[::-1])], axis=0)
    return apw, fwd, rev


def _cmul(c1, c2, x, half, conj=False):
    sw = pltpu.roll(x, half, 1)
    return c1 * x - c2 * sw if conj else c1 * x + c2 * sw


def _gather_groups(u_ref, ug_ref, nc):
    h = S5_GROUP
    rows = min(S5_REGROUP_ROWS, nc)

    def step(r, _):
        base = pl.multiple_of(r * rows, rows)
        for t in range(S5_CHUNK):
            val = u_ref[pl.ds(base * S5_CHUNK + t, rows, stride=S5_CHUNK), :]
            for g in range(S5_OCTET):
                ug_ref[g, pl.ds(base, rows), t * h:(t + 1) * h] = val[:, g * h:(g + 1) * h]
        return 0

    lax.fori_loop(0, nc // rows, step, 0)


def _scatter_groups(yg_ref, y_ref, nc):
    h = S5_GROUP
    rows = min(S5_REGROUP_ROWS, nc)

    def step(r, _):
        base = pl.multiple_of(r * rows, rows)
        for t in range(S5_CHUNK):
            y_ref[pl.ds(base * S5_CHUNK + t, rows, stride=S5_CHUNK), :] = jnp.concatenate(
                [yg_ref[g, pl.ds(base, rows), t * h:(t + 1) * h] for g in range(S5_OCTET)], axis=1)
        return 0

    lax.fori_loop(0, nc // rows, step, 0)


def _s5_fwd(name, u, m, bm, cm, apw, arows):
    n_rows, width = u.shape
    nc = n_rows // S5_CHUNK
    groups, lh, _ = m.shape
    p2 = bm.shape[2]
    gb = S5_OCTET
    lanes = gb * S5_GROUP

    def body(u_ref, m_ref, b_ref, c_ref, apw_ref, ar_ref, y_ref, sp_ref, ug_ref, yg_ref, xs_ref):
        _gather_groups(u_ref, ug_ref, nc)
        for gi in range(gb):
            xs_ref[:, gi * p2:(gi + 1) * p2] = _dot(ug_ref[gi], b_ref[gi], prec=S5_PREC)
        row = lax.broadcasted_iota(jnp.int32, (SUBLANES, p2), 0)

        def step(n, carry):
            sl = pl.ds(pl.multiple_of(n * SUBLANES, SUBLANES), SUBLANES)
            new = []
            for gi in range(gb):
                ln = slice(gi * p2, (gi + 1) * p2)
                x = xs_ref[sl, ln]
                for q, k in enumerate((1, 2, 4)):
                    xs = jnp.where(row >= k, pltpu.roll(x, k, 0), 0.0)
                    x = x + _cmul(apw_ref[2 * q:2 * q + 1, ln], apw_ref[2 * q + 1:2 * q + 2, ln], xs, p2 // 2)
                cb = jnp.broadcast_to(carry[gi], (SUBLANES, p2))
                s8 = x + _cmul(ar_ref[0:8, ln], ar_ref[8:16, ln], cb, p2 // 2)
                sp_ref[sl, ln] = jnp.where(row >= 1, pltpu.roll(s8, 1, 0), cb)
                new.append(s8[SUBLANES - 1:SUBLANES])
            return tuple(new)

        lax.fori_loop(0, nc // SUBLANES, step, tuple(jnp.zeros((1, p2), F32) for _ in range(gb)))
        for gi in range(gb):
            yg_ref[gi] = (_dot(ug_ref[gi], m_ref[gi], prec=S5_PREC)
                          + _dot(sp_ref[:, gi * p2:(gi + 1) * p2], c_ref[gi], prec=S5_PREC))
        _scatter_groups(yg_ref, y_ref, nc)

    tok = pl.BlockSpec((n_rows, lanes), lambda g: (0, g), pipeline_mode=pl.Buffered(1))
    return pl.pallas_call(
        body, grid=(groups // gb,), name=name,
        in_specs=[tok, pl.BlockSpec((gb, lh, lh), lambda g: (g, 0, 0)),
                  pl.BlockSpec((gb, lh, p2), lambda g: (g, 0, 0)), pl.BlockSpec((gb, p2, lh), lambda g: (g, 0, 0)),
                  pl.BlockSpec((8, gb * p2), lambda g: (0, g)), pl.BlockSpec((16, gb * p2), lambda g: (0, g))],
        out_specs=[tok, pl.BlockSpec((nc, gb * p2), lambda g: (0, g))],
        out_shape=[jax.ShapeDtypeStruct((n_rows, width), F32), jax.ShapeDtypeStruct((nc, groups * p2), F32)],
        scratch_shapes=[pltpu.VMEM((gb, nc, lh), F32), pltpu.VMEM((gb, nc, lh), F32), pltpu.VMEM((nc, gb * p2), F32)],
        compiler_params=_params("parallel"),
    )(u, m, bm, cm, apw, arows)


def _s5_bwd(name, u, dy, sprev, m, bm, cm, apw, arows_rev):
    n_rows, width = u.shape
    nc = n_rows // S5_CHUNK
    groups, lh, _ = m.shape
    p2 = bm.shape[2]
    half = p2 // 2
    gb = S5_OCTET
    lanes = gb * S5_GROUP

    def body(u_ref, dy_ref, sp_ref, m_ref, b_ref, c_ref, apw_ref, ar_ref,
             du_ref, dm_ref, db_ref, dc_ref, da_ref, ug_ref, dyg_ref, ds_ref, gx_ref):
        _gather_groups(u_ref, ug_ref, nc)
        _gather_groups(dy_ref, dyg_ref, nc)
        for gi in range(gb):
            ds_ref[:, gi * p2:(gi + 1) * p2] = _dot(dyg_ref[gi], c_ref[gi], NT, prec=S5_PREC)
        row = lax.broadcasted_iota(jnp.int32, (SUBLANES, p2), 0)
        lane = lax.broadcasted_iota(jnp.int32, (SUBLANES, p2), 1)
        ngroups = nc // SUBLANES

        def step(n, state):
            carry, nxt, dacc = state
            sl = pl.ds(pl.multiple_of((ngroups - 1 - n) * SUBLANES, SUBLANES), SUBLANES)
            new_c, new_n, new_d = [], [], []
            for gi in range(gb):
                ln = slice(gi * p2, (gi + 1) * p2)
                d8 = ds_ref[sl, ln]
                x = jnp.where(row < SUBLANES - 1, pltpu.roll(d8, SUBLANES - 1, 0),
                              jnp.broadcast_to(nxt[gi], (SUBLANES, p2)))
                for q, k in enumerate((1, 2, 4)):
                    xs = jnp.where(row < SUBLANES - k, pltpu.roll(x, SUBLANES - k, 0), 0.0)
                    x = x + _cmul(apw_ref[2 * q:2 * q + 1, ln], apw_ref[2 * q + 1:2 * q + 2, ln], xs, half, conj=True)
                cb = jnp.broadcast_to(carry[gi], (SUBLANES, p2))
                g8 = x + _cmul(ar_ref[0:8, ln], ar_ref[8:16, ln], cb, half, conj=True)
                gx_ref[sl, ln] = g8
                s8 = sp_ref[sl, ln]
                p1 = g8 * s8
                pq = g8 * pltpu.roll(s8, half, 1)
                d_a = jnp.where(lane < half, p1 + pltpu.roll(p1, half, 1), pq - pltpu.roll(pq, half, 1))
                new_c.append(g8[0:1])
                new_n.append(d8[0:1])
                new_d.append(dacc[gi] + jnp.sum(d_a, axis=0, keepdims=True))
            return tuple(new_c), tuple(new_n), tuple(new_d)

        zeros = tuple(jnp.zeros((1, p2), F32) for _ in range(gb))
        _, _, dacc = lax.fori_loop(0, ngroups, step, (zeros, zeros, zeros))
        for gi in range(gb):
            ln = slice(gi * p2, (gi + 1) * p2)
            da_ref[:, ln] = dacc[gi]
            ug, dyg, gxg = ug_ref[gi], dyg_ref[gi], gx_ref[:, ln]
            dm_ref[gi] = _dot(ug, dyg, TN, prec=S5_PREC)
            dc_ref[gi] = _dot(sp_ref[:, ln], dyg, TN, prec=S5_PREC)
            db_ref[gi] = _dot(ug, gxg, TN, prec=S5_PREC)
            dyg_ref[gi] = _dot(dyg, m_ref[gi], NT, prec=S5_PREC) + _dot(gxg, b_ref[gi], NT, prec=S5_PREC)
        _scatter_groups(dyg_ref, du_ref, nc)

    tok = pl.BlockSpec((n_rows, lanes), lambda g: (0, g), pipeline_mode=pl.Buffered(1))
    tok_s = pl.BlockSpec((nc, gb * p2), lambda g: (0, g))
    mat_m = pl.BlockSpec((gb, lh, lh), lambda g: (g, 0, 0))
    mat_b = pl.BlockSpec((gb, lh, p2), lambda g: (g, 0, 0))
    mat_c = pl.BlockSpec((gb, p2, lh), lambda g: (g, 0, 0))
    return pl.pallas_call(
        body, grid=(groups // gb,), name=name,
        in_specs=[tok, tok, tok_s, mat_m, mat_b, mat_c,
                  pl.BlockSpec((8, gb * p2), lambda g: (0, g)), pl.BlockSpec((16, gb * p2), lambda g: (0, g))],
        out_specs=[tok, mat_m, mat_b, mat_c, pl.BlockSpec((1, gb * p2), lambda g: (0, g))],
        out_shape=[jax.ShapeDtypeStruct((n_rows, width), F32), jax.ShapeDtypeStruct(m.shape, F32),
                   jax.ShapeDtypeStruct(bm.shape, F32), jax.ShapeDtypeStruct(cm.shape, F32),
                   jax.ShapeDtypeStruct((1, groups * p2), F32)],
        scratch_shapes=[pltpu.VMEM((gb, nc, lh), F32), pltpu.VMEM((gb, nc, lh), F32),
                        pltpu.VMEM((nc, gb * p2), F32), pltpu.VMEM((nc, gb * p2), F32)],
        compiler_params=_params("parallel"),
    )(u, dy, sprev, m, bm, cm, apw, arows_rev)


def _sb_more(kb, carries):
    top = jnp.max(carries[0][...])
    for c in carries[1:]:
        top = jnp.maximum(top, jnp.max(c[...]))
    return (kb >= 0) & (top > SB_UNDERFLOW)


def _strips(n_rows, fn):
    def step(s, _):
        fn(pl.ds(pl.multiple_of(s * SB_STRIP, SB_STRIP), SB_STRIP))
        return 0
    lax.fori_loop(0, n_rows // SB_STRIP, step, 0)


def _sb_logits(z_ref, lb_ref, hi_ref, lo_ref, ksum_ref, diagonal):
    tq, tk = z_ref.shape

    def strip(rows):
        z = z_ref[rows, :]
        sp = _softplus(z)
        lk = -sp
        if diagonal:
            row = lax.broadcasted_iota(jnp.int32, (SB_STRIP, tk), 0) + rows.start
            col = lax.broadcasted_iota(jnp.int32, (SB_STRIP, tk), 1)
            lk = jnp.where(col < row, lk, 0.0)
        lb_ref[rows, :] = z - sp
        hi = lk.astype(BF16)
        hi_ref[rows, :] = hi
        lo_ref[rows, :] = (lk - hi.astype(F32)).astype(BF16)
        ksum_ref[rows, :] = jnp.sum(lk, axis=1, keepdims=True)

    _strips(tq, strip)


def _sb_weights(lb_ref, r_ref, carry_ref, a_ref, diagonal):
    tq, tk = lb_ref.shape

    def strip(rows):
        a = jnp.exp(lb_ref[rows, :] + r_ref[rows, :] + carry_ref[rows, :])
        if diagonal:
            row = lax.broadcasted_iota(jnp.int32, (SB_STRIP, tk), 0) + rows.start
            col = lax.broadcasted_iota(jnp.int32, (SB_STRIP, tk), 1)
            a = jnp.where(col < row, a, 0.0)
        a_ref[rows, :] = a.astype(BF16)

    _strips(tq, strip)


def _sb_fwd(name, q, k, v):
    heads, n_rows, hd = q.shape
    tq = _tile(n_rows // 2, 256)
    hb = SB_HEADS_PER_STEP
    scale = hd ** -0.5

    def body(q_ref, k_ref, v_ref, o_ref, z_ref, lb_ref, hi_ref, lo_ref, r_ref, a_ref, ksum_ref, *carry_refs):
        qi = pl.program_id(1)
        row = lax.broadcasted_iota(jnp.int32, (tq, tq), 0)
        col = lax.broadcasted_iota(jnp.int32, (tq, tq), 1)
        tri = (row > col).astype(BF16)

        def block(kb, diagonal):
            ks = pl.ds(pl.multiple_of(kb * tq, tq), tq)
            for h in range(hb):
                z_ref[...] = _dot(q_ref[h] * scale, k_ref[h, ks, :], NT)
                _sb_logits(z_ref, lb_ref, hi_ref, lo_ref, ksum_ref, diagonal)
                r_ref[...] = _dot(hi_ref[...], tri) + _dot(lo_ref[...], tri)
                _sb_weights(lb_ref, r_ref, carry_refs[h], a_ref, diagonal)
                o_ref[h] += _dot(a_ref[...], v_ref[h, ks, :])
                carry_refs[h][...] += ksum_ref[...]

        for h in range(hb):
            carry_refs[h][...] = jnp.zeros((tq, 1), F32)
            o_ref[h] = jnp.zeros((tq, hd), F32)
        block(qi, True)

        def step(kb):
            block(kb, False)
            return kb - 1

        lax.while_loop(lambda kb: _sb_more(kb, carry_refs), step, qi - 1)

    kv = pl.BlockSpec((hb, n_rows, hd), lambda h, i: (h, 0, 0))
    qs = pl.BlockSpec((hb, tq, hd), lambda h, i: (h, i, 0))
    blk32 = pltpu.VMEM((tq, tq), F32)
    blk16 = pltpu.VMEM((tq, tq), BF16)
    return pl.pallas_call(body, grid=(heads // hb, n_rows // tq), in_specs=[qs, kv, kv], out_specs=qs, name=name,
                          out_shape=jax.ShapeDtypeStruct((heads, n_rows, hd), F32),
                          scratch_shapes=[blk32, blk32, blk16, blk16, blk32, blk16, pltpu.VMEM((tq, 1), F32)]
                          + [pltpu.VMEM((tq, 1), F32)] * hb,
                          compiler_params=_params("parallel", "arbitrary"))(q, k, v)


def _sb_bwd(name, q, k, v, o, do):
    heads, n_rows, hd = q.shape
    tq = _tile(n_rows // 2, 256)
    hb = SB_HEADS_PER_STEP
    scale = hd ** -0.5

    def body(q_ref, k_ref, v_ref, o_ref, do_ref, dq_ref, dk_ref, dv_ref,
             z_ref, lb_ref, hi_ref, lo_ref, r_ref, a_ref, ksum_ref, psum_ref, delta_ref, *state_refs):
        carry_refs, pcarry_refs = state_refs[:hb], state_refs[hb:]
        qi = pl.program_id(1)

        @pl.when(qi == 0)
        def _():
            dk_ref[...] = jnp.zeros(dk_ref.shape, F32)
            dv_ref[...] = jnp.zeros(dv_ref.shape, F32)

        row = lax.broadcasted_iota(jnp.int32, (tq, tq), 0)
        col = lax.broadcasted_iota(jnp.int32, (tq, tq), 1)
        tri = (row > col).astype(BF16)
        tri_incl = (row >= col).astype(BF16)

        def block(kb, diagonal):
            ks = pl.ds(pl.multiple_of(kb * tq, tq), tq)
            for h in range(hb):
                qb = q_ref[h] * scale
                dob16 = do_ref[h].astype(BF16)
                kblk, vblk = k_ref[h, ks, :], v_ref[h, ks, :]
                z_ref[...] = _dot(qb, kblk, NT)
                _sb_logits(z_ref, lb_ref, hi_ref, lo_ref, ksum_ref, diagonal)
                r_ref[...] = _dot(hi_ref[...], tri) + _dot(lo_ref[...], tri)
                _sb_weights(lb_ref, r_ref, carry_refs[h], a_ref, diagonal)
                dv_ref[h, ks, :] += _dot(a_ref[...], dob16, TN)
                z_ref[...] = _dot(dob16, vblk, NT)

                def p_strip(rows):
                    p = z_ref[rows, :] * a_ref[rows, :].astype(F32)
                    z_ref[rows, :] = p
                    hi = p.astype(BF16)
                    hi_ref[rows, :] = hi
                    lo_ref[rows, :] = (p - hi.astype(F32)).astype(BF16)
                    psum_ref[rows, :] = jnp.sum(p, axis=1, keepdims=True)

                _strips(tq, p_strip)
                r_ref[...] = _dot(hi_ref[...], tri_incl) + _dot(lo_ref[...], tri_incl)
                delta_ref[...] = jnp.sum(dob16.astype(F32) * o_ref[h], axis=1, keepdims=True) - pcarry_refs[h][...]

                def dz_strip(rows):
                    beta = jnp.exp(lb_ref[rows, :])
                    dz = z_ref[rows, :] * (1.0 - beta) - beta * (delta_ref[rows, :] - r_ref[rows, :])
                    if diagonal:
                        rr = lax.broadcasted_iota(jnp.int32, (SB_STRIP, tq), 0) + rows.start
                        cc = lax.broadcasted_iota(jnp.int32, (SB_STRIP, tq), 1)
                        dz = jnp.where(cc < rr, dz, 0.0)
                    a_ref[rows, :] = dz.astype(BF16)

                _strips(tq, dz_strip)
                dk_ref[h, ks, :] += _dot(a_ref[...], qb, TN)
                dq_ref[h] += _dot(a_ref[...], kblk)
                carry_refs[h][...] += ksum_ref[...]
                pcarry_refs[h][...] += psum_ref[...]

        for h in range(hb):
            carry_refs[h][...] = jnp.zeros((tq, 1), F32)
            pcarry_refs[h][...] = jnp.zeros((tq, 1), F32)
            dq_ref[h] = jnp.zeros((tq, hd), F32)
        block(qi, True)

        def step(kb):
            block(kb, False)
            return kb - 1

        lax.while_loop(lambda kb: _sb_more(kb, carry_refs), step, qi - 1)
        for h in range(hb):
            dq_ref[h] = dq_ref[h] * scale

    kv = pl.BlockSpec((hb, n_rows, hd), lambda h, i: (h, 0, 0))
    qs = pl.BlockSpec((hb, tq, hd), lambda h, i: (h, i, 0))
    full = jax.ShapeDtypeStruct((heads, n_rows, hd), F32)
    blk32 = pltpu.VMEM((tq, tq), F32)
    blk16 = pltpu.VMEM((tq, tq), BF16)
    col1 = pltpu.VMEM((tq, 1), F32)
    return pl.pallas_call(body, grid=(heads // hb, n_rows // tq), in_specs=[qs, kv, kv, qs, qs],
                          out_specs=[qs, kv, kv], out_shape=[full, full, full], name=name,
                          scratch_shapes=[blk32, blk32, blk16, blk16, blk32, blk16, col1, col1, col1]
                          + [col1] * (2 * hb),
                          compiler_params=_params("parallel", "arbitrary"))(q, k, v, o, do)


def _block_diag(xb, w_ref_val, dims):
    nb = w_ref_val.shape[0]
    bw = xb.shape[1] // nb
    return jnp.concatenate([_dot(xb[:, n * bw:(n + 1) * bw], w_ref_val[n], dims) for n in range(nb)], axis=1)


def _lru_gates_fwd(name, gx, conv_w, conv_b, wa, ba, wx, bx, lam):
    n_rows, w2 = gx.shape
    w = w2 // 2
    tm = _tile(n_rows, 256)

    def fn(i, nt, br, prev, cw, cb, wa_v, ba_v, wx_v, bx_v, lam_v):
        xc = cb + sum(cw[k:k + 1] * _shift_down(br, prev, LRU_CONV - 1 - k, i == 0) for k in range(LRU_CONV))
        xb = xc.astype(BF16)
        r = _sigmoid(_block_diag(xb, wa_v, NN) + ba_v)
        ig = _sigmoid(_block_diag(xb, wx_v, NN) + bx_v)
        log_a = (-LRU_C * r) * _softplus(-lam_v)
        a = jnp.exp(log_a)
        gated = (ig * xc) * _one_minus_a2_sqrt(log_a)
        return (xc, r, ig, a, gated), ()

    return _rowwise(name, fn, [('t', gx, w, 1), ('p', gx, w, 1), ('b', conv_w), ('b', conv_b), ('b', wa), ('b', ba),
                               ('b', wx), ('b', bx), ('b', lam)], [(w, F32)] * 5, [], n_rows, tm)


def _lru_scan_fwd(name, a, gated, gx):
    n_rows, w = a.shape
    tm = _tile(n_rows, 256)

    def body(a_ref, x_ref, bg_ref, hs_ref, y_ref, carry_ref):
        @pl.when(pl.program_id(0) == 0)
        def _():
            carry_ref[...] = jnp.zeros(carry_ref.shape, F32)
        carry_ref[...] = _scan_tile(a_ref, x_ref, hs_ref, carry_ref[...], False, tm)
        y_ref[...] = (_gelu(bg_ref[...]) * hs_ref[...]).astype(BF16)

    tok = pl.BlockSpec((tm, w), lambda i: (i, 0))
    return pl.pallas_call(body, grid=(n_rows // tm,), in_specs=[tok, tok, tok], out_specs=[tok, tok], name=name,
                          out_shape=[jax.ShapeDtypeStruct((n_rows, w), F32), jax.ShapeDtypeStruct((n_rows, w), BF16)],
                          scratch_shapes=[pltpu.VMEM((1, w), F32)], compiler_params=_params("arbitrary"))(a, gated, gx)


def _lru_scan_bwd(name, a, dy, gx):
    n_rows, w = a.shape
    tm = _tile(n_rows, 256)
    nt = n_rows // tm
    per8 = tm // SUBLANES

    def body(a_ref, an_ref, dy_ref, bg_ref, lam_ref, carry_ref, aup_ref, dhs_ref):
        i = pl.program_id(0)

        @pl.when(i == 0)
        def _():
            carry_ref[...] = jnp.zeros(carry_ref.shape, F32)
        aup_ref[...] = _shift_up(a_ref[...], an_ref[...], 1, i == 0)
        dhs_ref[...] = dy_ref[...] * _gelu(bg_ref[...])
        carry_ref[...] = _scan_tile(aup_ref, dhs_ref, lam_ref, carry_ref[...], True, tm)

    tok = pl.BlockSpec((tm, w), lambda i: (nt - 1 - i, 0))
    nxt = pl.BlockSpec((SUBLANES, w), lambda i: (jnp.minimum((nt - i) * per8, n_rows // SUBLANES - 1), 0))
    return pl.pallas_call(body, grid=(nt,), in_specs=[tok, nxt, tok, tok], out_specs=tok, name=name,
                          out_shape=jax.ShapeDtypeStruct((n_rows, w), F32),
                          scratch_shapes=[pltpu.VMEM((1, w), F32), pltpu.VMEM((tm, w), F32), pltpu.VMEM((tm, w), F32)],
                          compiler_params=_params("arbitrary"))(a, a, dy, gx)


def _lru_gates_bwd(name, lam_t, hs, xc, r, ig, a, wa, wx, lam):
    n_rows, w = xc.shape
    nb, bw, _ = wa.shape
    tm = _tile(n_rows, 256)

    def fn(i, nt, lt, hs_v, hs_prev, xc_v, r_v, ig_v, a_v, wa_v, wx_v, lam_v):
        sp = _softplus(-lam_v)
        log_a = (-LRU_C * r_v) * sp
        mult = _one_minus_a2_sqrt(log_a)
        d_a = lt * _shift_down(hs_v, hs_prev, 1, i == 0)
        d_ig = lt * xc_v * mult
        d_mult = lt * ig_v * xc_v
        d_log_a = d_a * a_v - d_mult * (a_v * a_v) / mult
        d_ra = d_log_a * (-LRU_C * sp) * r_v * (1.0 - r_v)
        d_ia = d_ig * ig_v * (1.0 - ig_v)
        d_ra16, d_ia16, xb = d_ra.astype(BF16), d_ia.astype(BF16), xc_v.astype(BF16)
        dxc = lt * ig_v * mult + _block_diag(d_ra16, wa_v, NT) + _block_diag(d_ia16, wx_v, NT)
        dwa = jnp.concatenate([_dot(xb[:, n * bw:(n + 1) * bw], d_ra16[:, n * bw:(n + 1) * bw], TN)
                               for n in range(nb)], axis=0)
        dwx = jnp.concatenate([_dot(xb[:, n * bw:(n + 1) * bw], d_ia16[:, n * bw:(n + 1) * bw], TN)
                               for n in range(nb)], axis=0)
        col = lambda t: jnp.sum(t, axis=0, keepdims=True)
        return (dxc,), (dwa, dwx, col(d_ra), col(d_ia), col(d_log_a * (-LRU_C * r_v)))

    tiled = lambda arr: ('t', arr, w, 0)
    return _rowwise(name, fn, [tiled(lam_t), tiled(hs), ('p', hs, w, 0), tiled(xc), tiled(r), tiled(ig), tiled(a),
                               ('b', wa), ('b', wx), ('b', lam)],
                    [(w, F32)], [(nb * bw, bw), (nb * bw, bw), (1, w), (1, w), (1, w)], n_rows, tm)


def _lru_conv_bwd(name, dxc, gx, dy, hs, conv_w):
    n_rows, w = dxc.shape
    tm = _tile(n_rows, 256)

    def fn(i, nt, dxc_v, dxc_next, bg, br, br_prev, dy_v, hs_v, cw):
        dbr = sum(cw[k:k + 1] * _shift_up(dxc_v, dxc_next, LRU_CONV - 1 - k, i == nt - 1) for k in range(LRU_CONV))
        dbg = dy_v * hs_v * _gelu_grad(bg)
        dcw = [jnp.sum(dxc_v * _shift_down(br, br_prev, LRU_CONV - 1 - k, i == 0), axis=0, keepdims=True)
               for k in range(LRU_CONV)]
        dcw = jnp.concatenate(dcw + [jnp.zeros((SUBLANES - LRU_CONV, w), F32)], axis=0)
        return (jnp.concatenate([dbg, dbr], axis=1),), (dcw, jnp.sum(dxc_v, axis=0, keepdims=True))

    return _rowwise(name, fn, [('t', dxc, w, 0), ('n', dxc, w, 0), ('t', gx, w, 0), ('t', gx, w, 1), ('p', gx, w, 1),
                               ('t', dy, w, 0), ('t', hs, w, 0), ('b', conv_w)],
                    [(2 * w, BF16)], [(SUBLANES, w), (1, w)], n_rows, tm)


def _loss_head(name, h, gain, target):
    n_rows, dm = h.shape
    tm = _tile(n_rows, 512)

    def fn(i, nt, hv, tv, g):
        r, xhat = _rms(hv)
        err = xhat * g - tv
        dy = err * (1.0 / dm)
        return ((_rms_bwd(dy, xhat, r, g),),
                (jnp.sum(err * err, axis=0, keepdims=True), jnp.sum(dy * xhat, axis=0, keepdims=True)))

    return _rowwise(name, fn, [('t', h, dm, 0), ('t', target, dm, 0), ('b', gain)], [(dm, F32)], [(1, dm), (1, dm)],
                    n_rows, tm, strip=ROW_STRIP)


def _adamw(name, gparts, w, m, v):
    n_parts, n_rows, cols = gparts.shape
    tr = n_rows
    for cand in (256, 128, 64, 32, 16, 8):
        if n_rows % cand == 0:
            tr = cand
            break
    c1 = 1.0 - ADAM_B1 ** ADAM_STEP
    c2 = 1.0 - ADAM_B2 ** ADAM_STEP

    def body(gp_ref, w_ref, m_ref, v_ref, g_ref, d_ref, nm_ref, nv_ref):
        g = gp_ref[0].astype(F32)
        for p in range(1, n_parts):
            g = g + gp_ref[p].astype(F32)
        m_new = ADAM_B1 * m_ref[...] + (1.0 - ADAM_B1) * g
        v_new = ADAM_B2 * v_ref[...] + (1.0 - ADAM_B2) * (g * g)
        m_hat = m_new / c1
        v_hat = v_new / c2
        g_ref[...] = g
        d_ref[...] = -ADAM_LR * (m_hat / (jnp.sqrt(v_hat) + ADAM_EPS) + ADAM_WD * w_ref[...])
        nm_ref[...] = m_new
        nv_ref[...] = v_new

    blk = pl.BlockSpec((tr, cols), lambda i: (i, 0))
    shp = jax.ShapeDtypeStruct((n_rows, cols), F32)
    return pl.pallas_call(body, grid=(n_rows // tr,), name=name,
                          in_specs=[pl.BlockSpec((n_parts, tr, cols), lambda i: (0, i, 0)), blk, blk, blk],
                          out_specs=[blk, blk, blk, blk], out_shape=[shp, shp, shp, shp],
                          compiler_params=_params("parallel"))(gparts, w, m, v)


def _adamw_layers(name, recvs, w, m, v):
    n_layers, n_rows, cols = w.shape
    n_parts = recvs[0].shape[0]
    tr = max(t for t in range(16, ADAMW_LAYER_ROWS + 1, 16) if n_rows % t == 0)
    c1 = 1.0 - ADAM_B1 ** ADAM_STEP
    c2 = 1.0 - ADAM_B2 ** ADAM_STEP

    def body(*refs):
        gp_refs = refs[:n_layers]
        w_ref, m_ref, v_ref, g_ref, d_ref, nm_ref, nv_ref = refs[n_layers:]
        layer = pl.program_id(0)
        for k in range(n_layers):
            @pl.when(layer == k)
            def _(k=k):
                g = gp_refs[k][0].astype(F32)
                for p in range(1, n_parts):
                    g = g + gp_refs[k][p].astype(F32)
                m_new = ADAM_B1 * m_ref[...] + (1.0 - ADAM_B1) * g
                v_new = ADAM_B2 * v_ref[...] + (1.0 - ADAM_B2) * (g * g)
                g_ref[...] = g
                d_ref[...] = -ADAM_LR * ((m_new / c1) / (jnp.sqrt(v_new / c2) + ADAM_EPS) + ADAM_WD * w_ref[...])
                nm_ref[...] = m_new
                nv_ref[...] = v_new

    blk = pl.BlockSpec((None, tr, cols), lambda l, i: (l, i, 0))
    shp = jax.ShapeDtypeStruct((n_layers, n_rows, cols), F32)
    gp_specs = [pl.BlockSpec((n_parts, tr, cols), lambda l, i, k=k: (0, jnp.where(l == k, i, 0), 0))
                for k in range(n_layers)]
    return pl.pallas_call(body, grid=(n_layers, n_rows // tr), name=name, in_specs=gp_specs + [blk, blk, blk],
                          out_specs=[blk, blk, blk, blk], out_shape=[shp, shp, shp, shp],
                          compiler_params=_params("arbitrary", "arbitrary"))(*recvs, w, m, v)


def _pack_rows(arrays, cols, lead=0):
    flat = [a.reshape(a.shape[:lead] + (-1,)) for a in arrays]
    cat = jnp.concatenate(flat, axis=lead) if len(flat) > 1 else flat[0]
    n = cat.shape[lead]
    pad = (-n) % (cols * PACK_ROWS)
    if pad:
        cat = jnp.pad(cat, [(0, 0)] * lead + [(0, pad)])
    return cat.reshape(cat.shape[:lead] + (-1, cols))


def _unpack_rows(packed, shapes, lead=0):
    flat = packed.reshape(packed.shape[:lead] + (-1,))
    out, off = [], 0
    for s in shapes:
        n = math.prod(s)
        out.append(lax.slice_in_dim(flat, off, off + n, axis=lead).reshape(flat.shape[:lead] + tuple(s)))
        off += n
    return out


def kernel(x, ffn1_norm, ffn1_w_in, ffn1_w_out, mix_norm, ffn2_norm, ffn2_w_in, ffn2_w_out, final_norm, s5_w_in, s5_lam_re, s5_lam_im, s5_log_dt, s5_b_re, s5_b_im, s5_c_re, s5_c_im, s5_d, s5_w_out, sb_w_qkv, sb_w_out, lru_w_in, lru_conv_w, lru_conv_b, lru_w_a, lru_b_a, lru_w_x, lru_b_x, lru_lambda, lru_w_out, loss_target, m_ffn1_norm, m_ffn1_w_in, m_ffn1_w_out, m_mix_norm, m_ffn2_norm, m_ffn2_w_in, m_ffn2_w_out, m_final_norm, m_s5_w_in, m_s5_lam_re, m_s5_lam_im, m_s5_log_dt, m_s5_b_re, m_s5_b_im, m_s5_c_re, m_s5_c_im, m_s5_d, m_s5_w_out, m_sb_w_qkv, m_sb_w_out, m_lru_w_in, m_lru_conv_w, m_lru_conv_b, m_lru_w_a, m_lru_b_a, m_lru_w_x, m_lru_b_x, m_lru_lambda, m_lru_w_out, v_ffn1_norm, v_ffn1_w_in, v_ffn1_w_out, v_mix_norm, v_ffn2_norm, v_ffn2_w_in, v_ffn2_w_out, v_final_norm, v_s5_w_in, v_s5_lam_re, v_s5_lam_im, v_s5_log_dt, v_s5_b_re, v_s5_b_im, v_s5_c_re, v_s5_c_im, v_s5_d, v_s5_w_out, v_sb_w_qkv, v_sb_w_out, v_lru_w_in, v_lru_conv_w, v_lru_conv_b, v_lru_w_a, v_lru_b_a, v_lru_w_x, v_lru_b_x, v_lru_lambda, v_lru_w_out):
    local = dict(locals())
    W = {n: local[n] for n in WEIGHTS}
    M = {n: local["m_" + n] for n in WEIGHTS}
    V = {n: local["v_" + n] for n in WEIGHTS}

    h0 = x[0]
    target = loss_target[0]
    n_rows, dm = h0.shape
    depth = ffn1_norm.shape[0]

    ffn_seq = [(tag, layer) for layer in range(depth) for tag in ("ffn1", "ffn2")]

    def ffn_shards(tag, layer):
        return W[tag + "_w_in"][layer].astype(BF16), W[tag + "_w_out"][layer].astype(BF16)

    def ffn_views(wi, wo):
        return wi.reshape((2, N_DEV // 2) + wi.shape[1:]), wo.reshape((N_DEV // 2, 2) + wo.shape[1:])

    first_in, first_out = ffn_shards(*ffn_seq[0])
    ffn_w = {ffn_seq[0]: ffn_views(_all_gather("ag_first_w_in", first_in), _all_gather("ag_first_w_out", first_out))}

    def ffn_forward(pos, h_in):
        tag, layer = ffn_seq[pos]
        comms = [("gather", a) for a in ffn_shards(*ffn_seq[pos + 1])] if pos + 1 < len(ffn_seq) else []
        res = _ffn_fwd("%s_fwd_%d" % (tag, layer), h_in, W[tag + "_norm"][layer:layer + 1], *ffn_w[ffn_seq[pos]],
                       comms=comms)
        if comms:
            ffn_w[ffn_seq[pos + 1]] = ffn_views(res[1], res[2])
        return res[0]

    mix_shapes = [W[n].shape for n in MIXER_BIG]
    mix_g = _all_gather("ag_mixers", _pack_rows([W[n].astype(BF16) for n in MIXER_BIG], dm))
    full = {n: _unshard(a, SHARD_AXIS[n]) for n, a in zip(MIXER_BIG, _unpack_rows(mix_g, mix_shapes, lead=1))}
    small_shapes = [W[n].shape for n in SMALL_SHARDED]
    small_g = _all_gather("ag_small", _pack_rows([W[n] for n in SMALL_SHARDED], 128))
    full.update({n: _unshard(a, SHARD_AXIS[n])
                 for n, a in zip(SMALL_SHARDED, _unpack_rows(small_g, small_shapes, lead=1))})

    n_s5 = s5_w_in.shape[0]
    s5_groups = s5_lam_re.shape[1]
    heads = dm // SB_HEAD_DIM

    grads = {}
    saved = []
    h = h0

    for layer in range(depth):
        kind, j = layer % 3, layer // 3
        rec = {"h0": h}
        h = ffn_forward(2 * layer, h)
        rec["h1"] = h
        gain = mix_norm[layer:layer + 1]
        if kind == 0:
            (u,) = _mm_fwd("s5_in_%d" % layer, h, full["s5_w_in"][j], F32, gain=gain)
            pars = (s5_lam_re[j], s5_lam_im[j], s5_log_dt[j], s5_b_re[j], s5_b_im[j], s5_c_re[j], s5_c_im[j])
            mats, mats_vjp = jax.vjp(_s5_mats, *pars)
            apw, ar_fwd, ar_rev = _s5_powers(*pars[:3])
            ys, sprev = _s5_fwd("s5_core_%d" % layer, u, *mats[:3], apw, ar_fwd)
            d_skip = full["s5_d"][j:j + 1]
            (z,) = _rowwise("s5_gelu_%d" % layer, lambda i, nt, ys_v, u_v, d_v: ((_gelu(ys_v + d_v * u_v),), ()),
                            [('t', ys, dm, 0), ('t', u, dm, 0), ('b', d_skip)], [(dm, BF16)], [], n_rows,
                            _tile(n_rows, 512), strip=ROW_STRIP)
            h, vg = _mm_fwd("s5_out_%d" % layer, z, full["s5_w_out"][j], F32, resid=h, glu=True)
            rec.update(u=u, ys=ys, sprev=sprev, z=z, vg=vg, mats=mats, mats_vjp=mats_vjp, apw=apw,
                       ar_rev=ar_rev, d_skip=d_skip)
        elif kind == 1:
            (qkv,) = _mm_fwd("sb_in_%d" % layer, h, full["sb_w_qkv"][j], BF16, gain=gain)
            qkv_h = qkv.reshape(n_rows, 3, heads, SB_HEAD_DIM).transpose(1, 2, 0, 3)
            o = _sb_fwd("sb_attn_%d" % layer, qkv_h[0], qkv_h[1], qkv_h[2])
            o_flat = o.transpose(1, 0, 2).reshape(n_rows, dm).astype(BF16)
            (h,) = _mm_fwd("sb_out_%d" % layer, o_flat, full["sb_w_out"][j], F32, resid=h)
            rec.update(qkv_h=qkv_h, o=o, o_flat=o_flat)
        else:
            (gx,) = _mm_fwd("lru_in_%d" % layer, h, full["lru_w_in"][j], F32, gain=gain)
            wa, wx = full["lru_w_a"][j], full["lru_w_x"][j]
            ba, bx = full["lru_b_a"][j].reshape(1, dm), full["lru_b_x"][j].reshape(1, dm)
            lam_row = full["lru_lambda"][j:j + 1]
            xc, r, ig, a, gated = _lru_gates_fwd("lru_gates_%d" % layer, gx, full["lru_conv_w"][j],
                                                 full["lru_conv_b"][j:j + 1], wa, ba, wx, bx, lam_row)
            hs, y = _lru_scan_fwd("lru_scan_%d" % layer, a, gated, gx)
            (h,) = _mm_fwd("lru_out_%d" % layer, y, full["lru_w_out"][j], F32, resid=h)
            rec.update(gx=gx, xc=xc, r=r, ig=ig, a=a, hs=hs, y=y, wa=wa, wx=wx, lam_row=lam_row)
        rec["h2"] = h
        h = ffn_forward(2 * layer + 1, h)
        saved.append(rec)

    dh, err2, dgf = _loss_head("loss_head", h, final_norm.reshape(1, dm), target)
    loss = lax.psum(0.5 / dm * jnp.sum(err2), ("x", "y", "c"))
    grads["final_norm"] = dgf.reshape(final_norm.shape)

    per_layer = {n: [None] * depth for n in ("ffn1_norm", "mix_norm", "ffn2_norm")}
    mixer_grads = {}
    recv_ffn = {}
    pending = []

    def ffn_backward(tag, layer, x_in, dh_in):
        comms = {0: [("exchange", pending[1])], 1: [("exchange", pending[2])]} if pending else {}
        dx, dwi, dwo, dg, extra = _ffn_bwd("%s_bwd_%d" % (tag, layer), x_in, dh_in, W[tag + "_norm"][layer:layer + 1],
                                            *ffn_w[(tag, layer)], comms_by_block=comms)
        if pending:
            recv_ffn[pending[0]] = tuple(extra)
        pending[:] = [(tag, layer), dwi.reshape((N_DEV,) + dwi.shape[2:]).astype(BF16),
                      dwo.reshape(N_DEV, -1, dm).astype(BF16)]
        per_layer[tag + "_norm"][layer] = dg
        return dx

    def put(name, j, value, count):
        mixer_grads.setdefault(name, [None] * count)[j] = value

    for layer in reversed(range(depth)):
        kind, j = layer % 3, layer // 3
        rec = saved[layer]
        dh = ffn_backward("ffn2", layer, rec["h2"], dh)
        gain = mix_norm[layer:layer + 1]
        if kind == 0:
            dvg, = _rowwise("s5_glu_bwd_%d" % layer,
                            lambda i, nt, d_v, vg_v: ((jnp.concatenate(
                                [d_v * _sigmoid(vg_v[:, dm:]),
                                 d_v * vg_v[:, :dm] * _sigmoid(vg_v[:, dm:]) * (1.0 - _sigmoid(vg_v[:, dm:]))],
                                axis=1),), ()),
                            [('t', dh, dm, 0), ('t', rec["vg"], 2 * dm, 0)], [(2 * dm, BF16)], [], n_rows,
                            _tile(n_rows, 256), strip=ROW_STRIP)
            dz, dw_out = _mm_bwd("s5_out_bwd_%d" % layer, rec["z"], dvg, full["s5_w_out"][j])

            def gelu_bwd(i, nt, dz_v, ys_v, u_v, d_v):
                dy_v = dz_v * _gelu_grad(ys_v + d_v * u_v)
                return (dy_v,), (jnp.sum(dy_v * u_v, axis=0, keepdims=True),)

            dys, dd = _rowwise("s5_gelu_bwd_%d" % layer, gelu_bwd,
                               [('t', dz, dm, 0), ('t', rec["ys"], dm, 0), ('t', rec["u"], dm, 0),
                                ('b', rec["d_skip"])], [(dm, F32)], [(1, dm)], n_rows, _tile(n_rows, 512),
                               strip=ROW_STRIP)
            m_, bm_, cm_, _ = rec["mats"]
            du_core, dm_m, dm_b, dm_c, d_a = _s5_bwd("s5_core_bwd_%d" % layer, rec["u"], dys, rec["sprev"],
                                                    m_, bm_, cm_, rec["apw"], rec["ar_rev"])
            dpars = rec["mats_vjp"]((dm_m, dm_b, dm_c, d_a.reshape(s5_groups, -1)))
            for nme, val in zip(("s5_lam_re", "s5_lam_im", "s5_log_dt", "s5_b_re", "s5_b_im", "s5_c_re", "s5_c_im"),
                                dpars):
                put(nme, j, val, n_s5)
            (du,) = _rowwise("s5_du_%d" % layer, lambda i, nt, a_v, dy_v, d_v: ((a_v + dy_v * d_v,), ()),
                             [('t', du_core, dm, 0), ('t', dys, dm, 0), ('b', rec["d_skip"])],
                             [(dm, BF16)], [], n_rows, _tile(n_rows, 512), strip=ROW_STRIP)
            dh, dw_in, dgm = _mm_bwd("s5_in_bwd_%d" % layer, rec["h1"], du, full["s5_w_in"][j], gain=gain, dres=dh)
            put("s5_d", j, dd[0], n_s5)
            put("s5_w_out", j, dw_out, n_s5)
            put("s5_w_in", j, dw_in, n_s5)
        elif kind == 1:
            do_flat, dw_out = _mm_bwd("sb_out_bwd_%d" % layer, rec["o_flat"], dh, full["sb_w_out"][j])
            do = do_flat.reshape(n_rows, heads, SB_HEAD_DIM).transpose(1, 0, 2)
            qkv_h = rec["qkv_h"]
            dq, dk, dv = _sb_bwd("sb_attn_bwd_%d" % layer, qkv_h[0], qkv_h[1], qkv_h[2], rec["o"], do)
            dqkv = jnp.stack([dq, dk, dv]).transpose(2, 0, 1, 3).reshape(n_rows, 3 * dm).astype(BF16)
            dh, dw_in, dgm = _mm_bwd("sb_in_bwd_%d" % layer, rec["h1"], dqkv, full["sb_w_qkv"][j], gain=gain, dres=dh)
            put("sb_w_out", j, dw_out, 1)
            put("sb_w_qkv", j, dw_in, 1)
        else:
            dy, dw_out = _mm_bwd("lru_out_bwd_%d" % layer, rec["y"], dh, full["lru_w_out"][j])
            lam_t = _lru_scan_bwd("lru_scan_bwd_%d" % layer, rec["a"], dy, rec["gx"])
            dxc, dwa, dwx, dba, dbx, dsp = _lru_gates_bwd("lru_gates_bwd_%d" % layer, lam_t, rec["hs"], rec["xc"],
                                                          rec["r"], rec["ig"], rec["a"], rec["wa"], rec["wx"],
                                                          rec["lam_row"])
            dgx, dcw, dcb = _lru_conv_bwd("lru_conv_bwd_%d" % layer, dxc, rec["gx"], dy, rec["hs"],
                                          full["lru_conv_w"][j])
            dh, dw_in, dgm = _mm_bwd("lru_in_bwd_%d" % layer, rec["h1"], dgx, full["lru_w_in"][j], gain=gain, dres=dh)
            nb = rec["wa"].shape[0]
            put("lru_w_out", j, dw_out, 1)
            put("lru_w_in", j, dw_in, 1)
            put("lru_w_a", j, dwa.reshape(rec["wa"].shape), 1)
            put("lru_w_x", j, dwx.reshape(rec["wx"].shape), 1)
            put("lru_b_a", j, dba.reshape(nb, -1), 1)
            put("lru_b_x", j, dbx.reshape(nb, -1), 1)
            put("lru_conv_w", j, dcw[:LRU_CONV], 1)
            put("lru_conv_b", j, dcb[0], 1)
            put("lru_lambda", j, (dsp * -_sigmoid(-rec["lam_row"]))[0], 1)
        per_layer["mix_norm"][layer] = dgm
        dh = ffn_backward("ffn1", layer, rec["h0"], dh)

    grad_x = dh[None]
    for n in ("ffn1_norm", "mix_norm", "ffn2_norm"):
        grads[n] = jnp.concatenate(per_layer[n], axis=0)
    for n, parts in mixer_grads.items():
        grads[n] = jnp.stack(parts)

    out_g, out_d, out_m, out_v = {}, {}, {}, {}

    def finish(names, res, shapes):
        for n, g_, d_, m_, v_ in zip(names, *[_unpack_rows(t, shapes) for t in res]):
            out_g[n], out_d[n], out_m[n], out_v[n] = g_, d_, m_, v_

    recv_ffn[pending[0]] = (_exchange("xchg_last_w_in", pending[1]), _exchange("xchg_last_w_out", pending[2]))
    for tag in ("ffn1", "ffn2"):
        for which, n in enumerate((tag + "_w_in", tag + "_w_out")):
            out_g[n], out_d[n], out_m[n], out_v[n] = _adamw_layers(
                "adamw_" + n, [recv_ffn[(tag, layer)][which] for layer in range(depth)], W[n], M[n], V[n])

    send = _pack_rows([_shard_blocks(grads[n], SHARD_AXIS[n]).astype(BF16) for n in MIXER_BIG], dm, lead=1)
    recv = _exchange("xchg_mixers", send)
    finish(MIXER_BIG, _adamw("adamw_mixers", recv, *[_pack_rows([t[n] for n in MIXER_BIG], dm) for t in (W, M, V)]),
           mix_shapes)

    small_names = REPLICATED + SMALL_SHARDED
    small_full_shapes = [grads[n].shape for n in small_names]
    parts = _all_gather("ag_small_grads", _pack_rows([grads[n] for n in small_names], 128))
    zero = jnp.zeros(parts.shape[1:], F32)
    summed = _adamw("sum_small_grads", parts, zero, zero, zero)[0]
    small_sum = dict(zip(small_names, _unpack_rows(summed, small_full_shapes)))
    me = 4 * lax.axis_index("x") + 2 * lax.axis_index("y") + lax.axis_index("c")
    rep_shapes = [W[n].shape for n in REPLICATED]
    g_rep = _pack_rows([small_sum[n] for n in REPLICATED], 128)[None]
    finish(REPLICATED, _adamw("adamw_replicated", g_rep, *[_pack_rows([t[n] for n in REPLICATED], 128)
                                                           for t in (W, M, V)]), rep_shapes)
    g_loc = []
    for n in SMALL_SHARDED:
        ax = SHARD_AXIS[n]
        size = W[n].shape[ax]
        g_loc.append(lax.dynamic_slice_in_dim(small_sum[n], me * size, size, axis=ax))
    finish(SMALL_SHARDED, _adamw("adamw_small", _pack_rows(g_loc, 128)[None],
                                 *[_pack_rows([t[n] for n in SMALL_SHARDED], 128) for t in (W, M, V)]), small_shapes)

    return (loss, grad_x, *[out_g[n] for n in WEIGHTS], *[out_d[n] for n in WEIGHTS],
            *[out_m[n] for n in WEIGHTS], *[out_v[n] for n in WEIGHTS])
```

```python
import functools
import math

import jax
import jax.numpy as jnp
from jax import lax
from jax.experimental import pallas as pl
from jax.experimental.pallas import tpu as pltpu

F32 = jnp.float32
BF16 = jnp.bfloat16
HI = lax.Precision.HIGHEST
S5_PREC = lax.Precision.HIGH
MESH = pl.DeviceIdType.MESH

N_DEV = 8
RMS_EPS = 1e-6
S5_GROUP = 16
S5_CHUNK = 16
S5_OCTET = 128 // S5_GROUP
S5_REGROUP_ROWS = 32
SB_HEAD_DIM = 64
SB_UNDERFLOW = -104.0
SB_HEADS_FWD = 4
SB_HEADS_BWD = 2
LRU_CONV = 4
LRU_C = 8.0
ADAM_LR, ADAM_B1, ADAM_B2, ADAM_EPS, ADAM_WD, ADAM_STEP = 0.001, 0.9, 0.999, 1e-08, 0.01, 10
VMEM_LIMIT_BYTES = 56 * 1024 * 1024
SUBLANES = 8
PACK_ROWS = 256
ADAMW_LAYER_ROWS = 192

NN = (((1,), (0,)), ((), ()))
NT = (((1,), (1,)), ((), ()))
TN = (((0,), (0,)), ((), ()))

SHARD_AXIS = dict(
    ffn1_w_in=2, ffn1_w_out=1, ffn2_w_in=2, ffn2_w_out=1, s5_w_in=1, s5_d=1, s5_w_out=2, sb_w_qkv=2, sb_w_out=1,
    lru_w_in=2, lru_conv_w=2, lru_conv_b=1, lru_w_a=2, lru_b_a=2, lru_w_x=2, lru_b_x=2, lru_lambda=1, lru_w_out=1)
MIXER_BIG = ("s5_w_in", "s5_w_out", "sb_w_qkv", "sb_w_out", "lru_w_in", "lru_w_a", "lru_w_x", "lru_w_out")
SMALL_SHARDED = ("s5_d", "lru_conv_w", "lru_conv_b", "lru_b_a", "lru_b_x", "lru_lambda")
REPLICATED = ("ffn1_norm", "mix_norm", "ffn2_norm", "final_norm", "s5_lam_re", "s5_lam_im", "s5_log_dt",
              "s5_b_re", "s5_b_im", "s5_c_re", "s5_c_im")
WEIGHTS = ("ffn1_norm", "ffn1_w_in", "ffn1_w_out", "mix_norm", "ffn2_norm", "ffn2_w_in", "ffn2_w_out", "final_norm",
           "s5_w_in", "s5_lam_re", "s5_lam_im", "s5_log_dt", "s5_b_re", "s5_b_im", "s5_c_re", "s5_c_im", "s5_d",
           "s5_w_out", "sb_w_qkv", "sb_w_out", "lru_w_in", "lru_conv_w", "lru_conv_b", "lru_w_a", "lru_b_a",
           "lru_w_x", "lru_b_x", "lru_lambda", "lru_w_out")


def _dot(a, b, dims=NN, prec=None):
    return lax.dot_general(a, b, dims, precision=prec, preferred_element_type=F32)


def _params(*sem):
    return pltpu.CompilerParams(dimension_semantics=sem, vmem_limit_bytes=VMEM_LIMIT_BYTES)


def _tile(n, pref):
    return min(pref, n)


def _sigmoid(x):
    return jax.nn.sigmoid(x)


def _softplus(x):
    return jnp.maximum(x, 0.0) + jnp.log(1.0 + jnp.exp(-jnp.abs(x)))


_GELU_C = math.sqrt(2.0 / math.pi)


def _gelu(x):
    return 0.5 * x * (1.0 + jnp.tanh(_GELU_C * (x + 0.044715 * x * x * x)))


def _gelu_grad(x):
    t = jnp.tanh(_GELU_C * (x + 0.044715 * x * x * x))
    return 0.5 * (1.0 + t) + 0.5 * x * (1.0 - t * t) * _GELU_C * (1.0 + 3.0 * 0.044715 * x * x)


def _rms(x):
    r = lax.rsqrt(jnp.mean(x * x, axis=1, keepdims=True) + RMS_EPS)
    return r, x * r


def _rms_bwd(dhn, xhat, r, g):
    dxhat = dhn * g
    return r * (dxhat - xhat * jnp.mean(dxhat * xhat, axis=1, keepdims=True))


def _one_minus_a2_sqrt(log_a):
    t = jnp.tanh(log_a)
    return jnp.sqrt(-2.0 * t / (1.0 - t))


def _shift_down(cur, prev8, k, first):
    if k == 0:
        return cur
    row8 = lax.broadcasted_iota(jnp.int32, prev8.shape, 0)
    rolled = pltpu.roll(cur, k, 0)
    edge = jnp.where(first, 0.0, pltpu.roll(prev8, k, 0))
    top = jnp.where(row8 < k, edge, rolled[0:SUBLANES])
    return jnp.concatenate([top, rolled[SUBLANES:]], axis=0)


def _shift_up(cur, next8, k, last):
    if k == 0:
        return cur
    tm = cur.shape[0]
    row8 = lax.broadcasted_iota(jnp.int32, next8.shape, 0)
    rolled = pltpu.roll(cur, tm - k, 0)
    edge = jnp.where(last, 0.0, pltpu.roll(next8, SUBLANES - k, 0))
    bottom = jnp.where(row8 >= SUBLANES - k, edge, rolled[tm - SUBLANES:tm])
    return jnp.concatenate([rolled[:tm - SUBLANES], bottom], axis=0)


def _rowwise(name, fn, ins, out_tiled, out_acc, n_rows, tm, reverse=False):
    nt = n_rows // tm
    per8 = tm // SUBLANES
    n8 = n_rows // SUBLANES
    n_in, n_ot = len(ins), len(out_tiled)

    def pos(i):
        return nt - 1 - i if reverse else i

    in_specs, args = [], []
    for spec in ins:
        kind, arr = spec[0], spec[1]
        args.append(arr)
        if kind == 'b':
            in_specs.append(pl.BlockSpec(arr.shape, lambda i, nd=arr.ndim: (0,) * nd))
        elif kind == 't':
            in_specs.append(pl.BlockSpec((tm, spec[2]), lambda i, cb=spec[3]: (pos(i), cb)))
        elif kind == 'p':
            in_specs.append(pl.BlockSpec((SUBLANES, spec[2]),
                                         lambda i, cb=spec[3]: (jnp.maximum(pos(i) * per8 - 1, 0), cb)))
        else:
            in_specs.append(pl.BlockSpec((SUBLANES, spec[2]),
                                         lambda i, cb=spec[3]: (jnp.minimum((pos(i) + 1) * per8, n8 - 1), cb)))

    def body(*refs):
        i = pl.program_id(0)
        outs = refs[n_in:]
        touts, aouts = fn(pos(i), nt, *[r[...] for r in refs[:n_in]])
        for r, v in zip(outs[:n_ot], touts):
            r[...] = v.astype(r.dtype)
        if out_acc:
            @pl.when(i == 0)
            def _():
                for r in outs[n_ot:]:
                    r[...] = jnp.zeros(r.shape, r.dtype)
            for r, v in zip(outs[n_ot:], aouts):
                r[...] += v

    out_specs = [pl.BlockSpec((tm, n), lambda i: (pos(i), 0)) for n, _ in out_tiled]
    out_specs += [pl.BlockSpec((r, n), lambda i: (0, 0)) for r, n in out_acc]
    out_shape = [jax.ShapeDtypeStruct((n_rows, n), dt) for n, dt in out_tiled]
    out_shape += [jax.ShapeDtypeStruct((r, n), F32) for r, n in out_acc]
    return pl.pallas_call(body, grid=(nt,), in_specs=in_specs, out_specs=out_specs, out_shape=out_shape, name=name,
                          compiler_params=_params("arbitrary"))(*args)


def _all_gather(name, block):
    def body(x_ref, out_ref, send_sems, recv_sems, local_sem):
        x, y, c = lax.axis_index("x"), lax.axis_index("y"), lax.axis_index("c")
        me, sibling = (x, y, c), (x, y, 1 - c)
        chips = [(1 - x, y), (x, 1 - y), (1 - x, 1 - y)]

        def rows(px, py, pc):
            return out_ref.at[4 * px + 2 * py + pc]

        def copy(k, blk, to, src=None):
            return pltpu.make_async_remote_copy(
                src_ref=rows(*blk) if src is None else src, dst_ref=rows(*blk),
                send_sem=send_sems.at[k], recv_sem=recv_sems.at[k], device_id=to, device_id_type=MESH)

        mine = pltpu.make_async_copy(x_ref, rows(*me), local_sem)
        mine.start()
        first = [copy(0, me, sibling, src=x_ref)]
        first += [copy(1 + j, me, (*chip, c), src=x_ref) for j, chip in enumerate(chips)]
        for cp in first:
            cp.start()
        passed = [copy(4 + j, (*chip, c), sibling) for j, chip in enumerate(chips)]
        for j, chip in enumerate(chips):
            copy(1 + j, (*chip, c), me).wait_recv()
            passed[j].start()
        copy(0, sibling, me).wait_recv()
        for j, chip in enumerate(chips):
            copy(4 + j, (*chip, 1 - c), me).wait_recv()
        for cp in first + passed:
            cp.wait_send()
        mine.wait()

    return pl.pallas_call(
        body, name=name, out_shape=jax.ShapeDtypeStruct((N_DEV,) + block.shape, block.dtype),
        in_specs=[pl.BlockSpec(memory_space=pl.ANY)], out_specs=pl.BlockSpec(memory_space=pl.ANY),
        scratch_shapes=[pltpu.SemaphoreType.DMA((7,)), pltpu.SemaphoreType.DMA((7,)), pltpu.SemaphoreType.DMA(())],
    )(block)


def _exchange(name, send):
    def body(s_ref, r_ref, send_sems, recv_sems, local_sem):
        x, y, c = lax.axis_index("x"), lax.axis_index("y"), lax.axis_index("c")
        me = 4 * x + 2 * y + c
        mine = pltpu.make_async_copy(s_ref.at[me], r_ref.at[me], local_sem)
        mine.start()
        copies = []
        for k in range(1, N_DEV):
            dx, dy, dc = (k >> 2) & 1, (k >> 1) & 1, k & 1
            px = 1 - x if dx else x
            py = 1 - y if dy else y
            pc = 1 - c if dc else c
            peer = 4 * px + 2 * py + pc
            copies.append((pltpu.make_async_remote_copy(
                src_ref=s_ref.at[peer], dst_ref=r_ref.at[me], send_sem=send_sems.at[k - 1],
                recv_sem=recv_sems.at[k - 1], device_id=(px, py, pc), device_id_type=MESH), peer))
        for cp, _ in copies:
            cp.start()
        for k, (cp, peer) in enumerate(copies):
            pltpu.make_async_remote_copy(
                src_ref=s_ref.at[peer], dst_ref=r_ref.at[peer], send_sem=send_sems.at[k], recv_sem=recv_sems.at[k],
                device_id=(x, y, c), device_id_type=MESH).wait_recv()
        for cp, _ in copies:
            cp.wait_send()
        mine.wait()

    return pl.pallas_call(
        body, name=name, out_shape=jax.ShapeDtypeStruct(send.shape, send.dtype),
        in_specs=[pl.BlockSpec(memory_space=pl.ANY)], out_specs=pl.BlockSpec(memory_space=pl.ANY),
        scratch_shapes=[pltpu.SemaphoreType.DMA((7,)), pltpu.SemaphoreType.DMA((7,)), pltpu.SemaphoreType.DMA(())],
    )(send)


def _direct_copies(kind, s_ref, r_ref, send_sems, recv_sems, local_sem):
    x, y, c = lax.axis_index("x"), lax.axis_index("y"), lax.axis_index("c")
    me = 4 * x + 2 * y + c

    def src(p):
        return s_ref if kind == "gather" else s_ref.at[p]

    local = pltpu.make_async_copy(src(me), r_ref.at[me], local_sem)
    sends, recvs = [], []
    for k in range(1, N_DEV):
        px = 1 - x if (k >> 2) & 1 else x
        py = 1 - y if (k >> 1) & 1 else y
        pc = 1 - c if k & 1 else c
        peer = 4 * px + 2 * py + pc
        sends.append(pltpu.make_async_remote_copy(
            src_ref=src(peer), dst_ref=r_ref.at[me], send_sem=send_sems.at[k - 1], recv_sem=recv_sems.at[k - 1],
            device_id=(px, py, pc), device_id_type=MESH))
        recvs.append(pltpu.make_async_remote_copy(
            src_ref=src(peer), dst_ref=r_ref.at[peer], send_sem=send_sems.at[k - 1], recv_sem=recv_sems.at[k - 1],
            device_id=(x, y, c), device_id_type=MESH))
    return local, sends, recvs


def _call(body, *, grid, in_specs, out_specs, out_shape, name, args, scratch_shapes=(), semantics=None, comms=()):
    single = not isinstance(out_shape, (list, tuple))
    out_shape = [out_shape] if single else list(out_shape)
    out_specs = [out_specs] if single else list(out_specs)
    if not comms:
        res = pl.pallas_call(body, grid=grid, in_specs=in_specs, out_specs=out_specs, out_shape=out_shape, name=name,
                             scratch_shapes=list(scratch_shapes),
                             compiler_params=_params(*(semantics or ("arbitrary",) * len(grid))))(*args)
        return res[0] if single else res
    n_in, n_out, n_scr, n_c = len(args), len(out_shape), len(scratch_shapes), len(comms)

    def hosted(*refs):
        ins, srcs = refs[:n_in], refs[n_in:n_in + n_c]
        outs = refs[n_in + n_c:n_in + n_c + n_out]
        dsts = refs[n_in + n_c + n_out:n_in + 2 * n_c + n_out]
        scr = refs[n_in + 2 * n_c + n_out:n_in + 2 * n_c + n_out + n_scr]
        sems = refs[n_in + 2 * n_c + n_out + n_scr:]
        first = functools.reduce(jnp.logical_and, [pl.program_id(d) == 0 for d in range(len(grid))])
        last = functools.reduce(jnp.logical_and, [pl.program_id(d) == grid[d] - 1 for d in range(len(grid))])
        plans = [_direct_copies(comms[i][0], srcs[i], dsts[i], *sems[3 * i:3 * i + 3]) for i in range(n_c)]

        @pl.when(first)
        def _():
            for local, sends, _ in plans:
                local.start()
                for cp in sends:
                    cp.start()

        body(*ins, *outs, *scr)

        @pl.when(last)
        def _():
            for local, sends, recvs in plans:
                for cp in recvs:
                    cp.wait_recv()
                for cp in sends:
                    cp.wait_send()
                local.wait()

    any_spec = pl.BlockSpec(memory_space=pl.ANY)
    comm_shapes = [jax.ShapeDtypeStruct(((N_DEV,) + a.shape) if kind == "gather" else a.shape, a.dtype)
                   for kind, a in comms]
    sem_shapes = []
    for _ in comms:
        sem_shapes += [pltpu.SemaphoreType.DMA((7,)), pltpu.SemaphoreType.DMA((7,)), pltpu.SemaphoreType.DMA(())]
    res = pl.pallas_call(
        hosted, grid=grid, in_specs=list(in_specs) + [any_spec] * n_c, out_specs=out_specs + [any_spec] * n_c,
        out_shape=out_shape + comm_shapes, name=name, scratch_shapes=list(scratch_shapes) + sem_shapes,
        compiler_params=_params(*(("arbitrary",) * len(grid))))(*args, *[a for _, a in comms])
    return res


def _unshard(gathered, axis):
    local = gathered.shape[1:]
    full = jnp.moveaxis(gathered, 0, axis)
    return full.reshape(local[:axis] + (N_DEV * local[axis],) + local[axis + 1:])


def _shard_blocks(full, axis):
    s = full.shape
    cut = full.reshape(s[:axis] + (N_DEV, s[axis] // N_DEV) + s[axis + 1:])
    return jnp.moveaxis(cut, axis, 0)


def _mm_fwd(name, a, w, out_dtype, gain=None, resid=None, glu=False):
    n_rows, k = a.shape
    n = w.shape[1]
    tm = _tile(n_rows, 512)
    n_out = n // 2 if glu else n

    def body(*refs):
        it = iter(refs)
        a_ref, w_ref = next(it), next(it)
        g_ref = next(it) if gain is not None else None
        r_ref = next(it) if resid is not None else None
        outs = list(it)
        av = a_ref[...]
        if g_ref is not None:
            _, xhat = _rms(av)
            av = xhat * g_ref[...]
        res = _dot(av.astype(BF16), w_ref[...])
        if glu:
            outs[1][...] = res.astype(outs[1].dtype)
            res = res[:, :n_out] * _sigmoid(res[:, n_out:])
        if r_ref is not None:
            res = res + r_ref[...]
        outs[0][...] = res.astype(outs[0].dtype)

    args = [a, w]
    in_specs = [pl.BlockSpec((tm, k), lambda i: (i, 0)), pl.BlockSpec((k, n), lambda i: (0, 0))]
    if gain is not None:
        args.append(gain)
        in_specs.append(pl.BlockSpec((1, k), lambda i: (0, 0)))
    if resid is not None:
        args.append(resid)
        in_specs.append(pl.BlockSpec((tm, n_out), lambda i: (i, 0)))
    out_shape = [jax.ShapeDtypeStruct((n_rows, n_out), out_dtype)]
    out_specs = [pl.BlockSpec((tm, n_out), lambda i: (i, 0))]
    if glu:
        out_shape.append(jax.ShapeDtypeStruct((n_rows, n), F32))
        out_specs.append(pl.BlockSpec((tm, n), lambda i: (i, 0)))
    return pl.pallas_call(body, grid=(n_rows // tm,), in_specs=in_specs, out_specs=out_specs, out_shape=out_shape,
                          name=name, compiler_params=_params("parallel"))(*args)


def _mm_bwd(name, a, d, w, gain=None, dres=None):
    n_rows, k = a.shape
    n = w.shape[1]
    tm = _tile(n_rows, 512)

    def body(*refs):
        it = iter(refs)
        a_ref, d_ref, w_ref = next(it), next(it), next(it)
        g_ref = next(it) if gain is not None else None
        r_ref = next(it) if gain is not None else None
        da_ref, dw_ref = next(it), next(it)
        dg_ref = next(it) if gain is not None else None
        i = pl.program_id(0)

        @pl.when(i == 0)
        def _():
            dw_ref[...] = jnp.zeros(dw_ref.shape, F32)
            if dg_ref is not None:
                dg_ref[...] = jnp.zeros(dg_ref.shape, F32)

        av = a_ref[...]
        dv = d_ref[...].astype(BF16)
        if g_ref is not None:
            r, xhat = _rms(av)
            ab = (xhat * g_ref[...]).astype(BF16)
        else:
            ab = av.astype(BF16)
        dw_ref[...] += _dot(ab, dv, TN)
        da = _dot(dv, w_ref[...], NT)
        if g_ref is not None:
            dg_ref[...] += jnp.sum(da * xhat, axis=0, keepdims=True)
            da = r_ref[...] + _rms_bwd(da, xhat, r, g_ref[...])
        da_ref[...] = da.astype(da_ref.dtype)

    args = [a, d, w]
    in_specs = [pl.BlockSpec((tm, k), lambda i: (i, 0)), pl.BlockSpec((tm, n), lambda i: (i, 0)),
                pl.BlockSpec((k, n), lambda i: (0, 0))]
    out_shape = [jax.ShapeDtypeStruct((n_rows, k), F32), jax.ShapeDtypeStruct((k, n), F32)]
    out_specs = [pl.BlockSpec((tm, k), lambda i: (i, 0)), pl.BlockSpec((k, n), lambda i: (0, 0))]
    if gain is not None:
        args += [gain, dres]
        in_specs += [pl.BlockSpec((1, k), lambda i: (0, 0)), pl.BlockSpec((tm, k), lambda i: (i, 0))]
        out_shape.append(jax.ShapeDtypeStruct((1, k), F32))
        out_specs.append(pl.BlockSpec((1, k), lambda i: (0, 0)))
    return pl.pallas_call(body, grid=(n_rows // tm,), in_specs=in_specs, out_specs=out_specs, out_shape=out_shape,
                          name=name, compiler_params=_params("arbitrary"))(*args)


def _ffn_fwd(name, x, gain, wi, wo, comms=()):
    n_rows, dm = x.shape
    _, nj, _, fb = wi.shape
    tm = _tile(n_rows, 512)

    def body(x_ref, g_ref, wi_ref, wo_ref, y_ref):
        xv = x_ref[...]
        _, xhat = _rms(xv)
        hn = (xhat * g_ref[...]).astype(BF16)
        acc = jnp.zeros((tm, dm), F32)
        for j in range(nj):
            gate = _dot(hn, wi_ref[0, j])
            up = _dot(hn, wi_ref[1, j])
            act = (gate * _sigmoid(gate) * up).astype(BF16)
            acc = acc + _dot(act, wo_ref[j].reshape(fb, dm))
        y_ref[...] = xv + 0.5 * acc

    return _call(
        body, grid=(n_rows // tm,), name=name, args=[x, gain, wi, wo], comms=comms,
        in_specs=[pl.BlockSpec((tm, dm), lambda i: (i, 0)), pl.BlockSpec((1, dm), lambda i: (0, 0)),
                  pl.BlockSpec((2, nj, dm, fb), lambda i: (0, 0, 0, 0)),
                  pl.BlockSpec((nj, 2, fb // 2, dm), lambda i: (0, 0, 0, 0))],
        out_specs=[pl.BlockSpec((tm, dm), lambda i: (i, 0))],
        out_shape=[jax.ShapeDtypeStruct((n_rows, dm), F32)])


def _ffn_bwd_block(name, x, dy, gain, wi, wo, j, acc, comms=()):
    n_rows, dm = x.shape
    _, nj, _, fb = wi.shape
    tm = _tile(n_rows, 512)
    last = j == nj - 1

    def body(*refs):
        it = iter(refs)
        x_ref, dy_ref, g_ref, wi_ref, wo_ref = next(it), next(it), next(it), next(it), next(it)
        acc_ref = next(it) if acc is not None else None
        out_ref, dwi_ref, dwo_ref = next(it), next(it), next(it)
        dg_ref = next(it) if last else None
        i = pl.program_id(0)

        @pl.when(i == 0)
        def _():
            dwi_ref[...] = jnp.zeros(dwi_ref.shape, F32)
            dwo_ref[...] = jnp.zeros(dwo_ref.shape, F32)
            if last:
                dg_ref[...] = jnp.zeros(dg_ref.shape, F32)

        xv, dyv, g = x_ref[...], dy_ref[...], g_ref[...]
        r, xhat = _rms(xv)
        hn = (xhat * g).astype(BF16)
        wg, wu, wob = wi_ref[0], wi_ref[1], wo_ref[...].reshape(fb, dm)
        gate = _dot(hn, wg)
        up = _dot(hn, wu)
        s = _sigmoid(gate)
        silu = gate * s
        act = (silu * up).astype(BF16)
        dout = (0.5 * dyv).astype(BF16)
        dact = _dot(dout, wob, NT)
        dwo_ref[...] += _dot(act, dout, TN)
        dgate = (dact * up * (s * (1.0 + gate * (1.0 - s)))).astype(BF16)
        dup = (dact * silu).astype(BF16)
        dwi_ref[0] += _dot(hn, dgate, TN)
        dwi_ref[1] += _dot(hn, dup, TN)
        tot = _dot(dgate, wg, NT) + _dot(dup, wu, NT)
        if acc_ref is not None:
            tot = tot + acc_ref[...]
        if last:
            out_ref[...] = dyv + _rms_bwd(tot, xhat, r, g)
            dg_ref[...] += jnp.sum(tot * xhat, axis=0, keepdims=True)
        else:
            out_ref[...] = tot

    tok = pl.BlockSpec((tm, dm), lambda i: (i, 0))
    args = [x, dy, gain, wi, wo]
    in_specs = [tok, tok, pl.BlockSpec((1, dm), lambda i: (0, 0)),
                pl.BlockSpec((2, None, dm, fb), lambda i: (0, j, 0, 0)),
                pl.BlockSpec((None, 2, fb // 2, dm), lambda i: (j, 0, 0, 0))]
    if acc is not None:
        args.append(acc)
        in_specs.append(tok)
    out_specs = [tok, pl.BlockSpec((2, dm, fb), lambda i: (0, 0, 0)), pl.BlockSpec((fb, dm), lambda i: (0, 0))]
    out_shape = [jax.ShapeDtypeStruct((n_rows, dm), F32), jax.ShapeDtypeStruct((2, dm, fb), F32),
                 jax.ShapeDtypeStruct((fb, dm), F32)]
    if last:
        out_specs.append(pl.BlockSpec((1, dm), lambda i: (0, 0)))
        out_shape.append(jax.ShapeDtypeStruct((1, dm), F32))
    return _call(body, grid=(n_rows // tm,), name=name, in_specs=in_specs, out_specs=out_specs,
                 out_shape=out_shape, args=args, comms=comms)


def _ffn_bwd(name, x, dy, gain, wi, wo, comms_by_block=None):
    nj = wi.shape[1]
    acc, dwi, dwo, extra = None, [], [], []
    for j in range(nj):
        comms = (comms_by_block or {}).get(j, ())
        res = _ffn_bwd_block("%s_%d" % (name, j), x, dy, gain, wi, wo, j, acc, comms)
        n_own = 4 if j == nj - 1 else 3
        acc = res[0]
        dwi.append(res[1])
        dwo.append(res[2])
        extra += list(res[n_own:])
        dgain = res[3] if j == nj - 1 else None
    return acc, jnp.stack(dwi, axis=1), jnp.stack(dwo, axis=0), dgain, extra


def _scan8(a, x, reverse):
    row = lax.broadcasted_iota(jnp.int32, a.shape, 0)
    for k in (1, 2, 4):
        if reverse:
            keep = row < SUBLANES - k
            a_s, x_s = pltpu.roll(a, SUBLANES - k, 0), pltpu.roll(x, SUBLANES - k, 0)
        else:
            keep = row >= k
            a_s, x_s = pltpu.roll(a, k, 0), pltpu.roll(x, k, 0)
        x = a * jnp.where(keep, x_s, 0.0) + x
        a = a * jnp.where(keep, a_s, 1.0)
    return a, x


def _scan_tile(a_ref, x_ref, h_ref, carry, reverse, rows):
    groups = rows // SUBLANES

    def step(n, c):
        gidx = groups - 1 - n if reverse else n
        sl = pl.ds(pl.multiple_of(gidx * SUBLANES, SUBLANES), SUBLANES)
        a_cum, h0 = _scan8(a_ref[sl, :], x_ref[sl, :], reverse)
        h = a_cum * c + h0
        h_ref[sl, :] = h
        return h[0:1] if reverse else h[SUBLANES - 1:SUBLANES]

    return lax.fori_loop(0, groups, step, carry)


def _s5_mats(lam_re, lam_im, log_dt, b_re, b_im, c_re, c_im):
    lc = S5_CHUNK
    groups, p = lam_re.shape
    h = b_re.shape[-1]
    lam = lax.complex(lam_re, lam_im)
    lam_dt = lam * jnp.exp(log_dt)[:, None]
    lam_bar = jnp.exp(lam_dt)
    b_bar = ((lam_bar - 1.0) / lam)[:, :, None] * lax.complex(b_re, b_im)
    c = lax.complex(c_re, c_im)
    pw = jnp.exp(lam_dt[None] * jnp.arange(lc + 1, dtype=F32)[:, None, None])
    resp = jnp.einsum('ghp,tgp,gpk->tghk', c, pw[:lc], b_bar, precision=HI).real
    s_idx = jnp.arange(lc)[:, None]
    u_idx = jnp.arange(lc)[None, :]
    onehot = (jnp.arange(lc)[:, None, None] == (u_idx - s_idx)[None]).astype(F32)
    m = jnp.einsum('tghk,tsu->gskuh', resp, onehot, precision=HI).reshape(groups, lc * h, lc * h)
    w = pw[lc - 1::-1][:lc].transpose(1, 0, 2)[:, :, None, :] * b_bar.transpose(0, 2, 1)[:, None]
    bm = jnp.concatenate([w.real, w.imag], axis=-1).reshape(groups, lc * h, 2 * p)
    v = c[:, None] * pw[1:lc + 1].transpose(1, 0, 2)[:, :, None, :]
    v = v.transpose(0, 3, 1, 2)
    cm = jnp.concatenate([v.real, -v.imag], axis=1).reshape(groups, 2 * p, lc * h)
    a = jnp.concatenate([pw[lc].real, pw[lc].imag], axis=-1)
    return m, bm, cm, a


def _s5_powers(lam_re, lam_im, log_dt):
    lam_dt = lax.complex(lam_re, lam_im) * jnp.exp(log_dt)[:, None]
    pw = jnp.exp(lam_dt[None] * (S5_CHUNK * jnp.arange(1, 9, dtype=F32))[:, None, None])

    def c1(z):
        return jnp.concatenate([z.real, z.real], axis=-1).reshape(z.shape[0], -1)

    def c2(z):
        return jnp.concatenate([-z.imag, z.imag], axis=-1).reshape(z.shape[0], -1)

    p1, p2 = c1(pw), c2(pw)
    apw = jnp.stack([p1[0], p2[0], p1[1], p2[1], p1[3], p2[3], jnp.zeros_like(p1[0]), jnp.zeros_like(p1[0])])
    fwd = jnp.concatenate([p1, p2], axis=0)
    rev = jnp.concatenate([c1(pw[::-1]), c2(pw[::-1])], axis=0)
    return apw, fwd, rev


def _cmul(c1, c2, x, half, conj=False):
    sw = pltpu.roll(x, half, 1)
    return c1 * x - c2 * sw if conj else c1 * x + c2 * sw


def _gather_groups(u_ref, ug_ref, nc):
    h = S5_GROUP
    rows = min(S5_REGROUP_ROWS, nc)

    def step(r, _):
        base = pl.multiple_of(r * rows, rows)
        for t in range(S5_CHUNK):
            val = u_ref[pl.ds(base * S5_CHUNK + t, rows, stride=S5_CHUNK), :]
            for g in range(S5_OCTET):
                ug_ref[g, pl.ds(base, rows), t * h:(t + 1) * h] = val[:, g * h:(g + 1) * h]
        return 0

    lax.fori_loop(0, nc // rows, step, 0)


def _scatter_groups(yg_ref, y_ref, nc):
    h = S5_GROUP
    rows = min(S5_REGROUP_ROWS, nc)

    def step(r, _):
        base = pl.multiple_of(r * rows, rows)
        for t in range(S5_CHUNK):
            y_ref[pl.ds(base * S5_CHUNK + t, rows, stride=S5_CHUNK), :] = jnp.concatenate(
                [yg_ref[g, pl.ds(base, rows), t * h:(t + 1) * h] for g in range(S5_OCTET)], axis=1)
        return 0

    lax.fori_loop(0, nc // rows, step, 0)


def _s5_fwd(name, u, m, bm, cm, apw, arows):
    n_rows, width = u.shape
    nc = n_rows // S5_CHUNK
    groups, lh, _ = m.shape
    p2 = bm.shape[2]
    gb = S5_OCTET
    lanes = gb * S5_GROUP

    def body(u_ref, m_ref, b_ref, c_ref, apw_ref, ar_ref, y_ref, sp_ref, ug_ref, yg_ref, xs_ref):
        _gather_groups(u_ref, ug_ref, nc)
        for gi in range(gb):
            xs_ref[:, gi * p2:(gi + 1) * p2] = _dot(ug_ref[gi], b_ref[gi], prec=S5_PREC)
        row = lax.broadcasted_iota(jnp.int32, (SUBLANES, p2), 0)

        def step(n, carry):
            sl = pl.ds(pl.multiple_of(n * SUBLANES, SUBLANES), SUBLANES)
            new = []
            for gi in range(gb):
                ln = slice(gi * p2, (gi + 1) * p2)
                x = xs_ref[sl, ln]
                for q, k in enumerate((1, 2, 4)):
                    xs = jnp.where(row >= k, pltpu.roll(x, k, 0), 0.0)
                    x = x + _cmul(apw_ref[2 * q:2 * q + 1, ln], apw_ref[2 * q + 1:2 * q + 2, ln], xs, p2 // 2)
                cb = jnp.broadcast_to(carry[gi], (SUBLANES, p2))
                s8 = x + _cmul(ar_ref[0:8, ln], ar_ref[8:16, ln], cb, p2 // 2)
                sp_ref[sl, ln] = jnp.where(row >= 1, pltpu.roll(s8, 1, 0), cb)
                new.append(s8[SUBLANES - 1:SUBLANES])
            return tuple(new)

        lax.fori_loop(0, nc // SUBLANES, step, tuple(jnp.zeros((1, p2), F32) for _ in range(gb)))
        for gi in range(gb):
            yg_ref[gi] = (_dot(ug_ref[gi], m_ref[gi], prec=S5_PREC)
                          + _dot(sp_ref[:, gi * p2:(gi + 1) * p2], c_ref[gi], prec=S5_PREC))
        _scatter_groups(yg_ref, y_ref, nc)

    tok = pl.BlockSpec((n_rows, lanes), lambda g: (0, g), pipeline_mode=pl.Buffered(1))
    return pl.pallas_call(
        body, grid=(groups // gb,), name=name,
        in_specs=[tok, pl.BlockSpec((gb, lh, lh), lambda g: (g, 0, 0)),
                  pl.BlockSpec((gb, lh, p2), lambda g: (g, 0, 0)), pl.BlockSpec((gb, p2, lh), lambda g: (g, 0, 0)),
                  pl.BlockSpec((8, gb * p2), lambda g: (0, g)), pl.BlockSpec((16, gb * p2), lambda g: (0, g))],
        out_specs=[tok, pl.BlockSpec((nc, gb * p2), lambda g: (0, g))],
        out_shape=[jax.ShapeDtypeStruct((n_rows, width), F32), jax.ShapeDtypeStruct((nc, groups * p2), F32)],
        scratch_shapes=[pltpu.VMEM((gb, nc, lh), F32), pltpu.VMEM((gb, nc, lh), F32), pltpu.VMEM((nc, gb * p2), F32)],
        compiler_params=_params("parallel"),
    )(u, m, bm, cm, apw, arows)


def _s5_bwd(name, u, dy, sprev, m, bm, cm, apw, arows_rev):
    n_rows, width = u.shape
    nc = n_rows // S5_CHUNK
    groups, lh, _ = m.shape
    p2 = bm.shape[2]
    half = p2 // 2
    gb = S5_OCTET
    lanes = gb * S5_GROUP

    def body(u_ref, dy_ref, sp_ref, m_ref, b_ref, c_ref, apw_ref, ar_ref,
             du_ref, dm_ref, db_ref, dc_ref, da_ref, ug_ref, dyg_ref, ds_ref, gx_ref):
        _gather_groups(u_ref, ug_ref, nc)
        _gather_groups(dy_ref, dyg_ref, nc)
        for gi in range(gb):
            ds_ref[:, gi * p2:(gi + 1) * p2] = _dot(dyg_ref[gi], c_ref[gi], NT, prec=S5_PREC)
        row = lax.broadcasted_iota(jnp.int32, (SUBLANES, p2), 0)
        lane = lax.broadcasted_iota(jnp.int32, (SUBLANES, p2), 1)
        ngroups = nc // SUBLANES

        def step(n, state):
            carry, nxt, dacc = state
            sl = pl.ds(pl.multiple_of((ngroups - 1 - n) * SUBLANES, SUBLANES), SUBLANES)
            new_c, new_n, new_d = [], [], []
            for gi in range(gb):
                ln = slice(gi * p2, (gi + 1) * p2)
                d8 = ds_ref[sl, ln]
                x = jnp.where(row < SUBLANES - 1, pltpu.roll(d8, SUBLANES - 1, 0),
                              jnp.broadcast_to(nxt[gi], (SUBLANES, p2)))
                for q, k in enumerate((1, 2, 4)):
                    xs = jnp.where(row < SUBLANES - k, pltpu.roll(x, SUBLANES - k, 0), 0.0)
                    x = x + _cmul(apw_ref[2 * q:2 * q + 1, ln], apw_ref[2 * q + 1:2 * q + 2, ln], xs, half, conj=True)
                cb = jnp.broadcast_to(carry[gi], (SUBLANES, p2))
                g8 = x + _cmul(ar_ref[0:8, ln], ar_ref[8:16, ln], cb, half, conj=True)
                gx_ref[sl, ln] = g8
                s8 = sp_ref[sl, ln]
                p1 = g8 * s8
                pq = g8 * pltpu.roll(s8, half, 1)
                d_a = jnp.where(lane < half, p1 + pltpu.roll(p1, half, 1), pq - pltpu.roll(pq, half, 1))
                new_c.append(g8[0:1])
                new_n.append(d8[0:1])
                new_d.append(dacc[gi] + jnp.sum(d_a, axis=0, keepdims=True))
            return tuple(new_c), tuple(new_n), tuple(new_d)

        zeros = tuple(jnp.zeros((1, p2), F32) for _ in range(gb))
        _, _, dacc = lax.fori_loop(0, ngroups, step, (zeros, zeros, zeros))
        for gi in range(gb):
            ln = slice(gi * p2, (gi + 1) * p2)
            da_ref[:, ln] = dacc[gi]
            ug, dyg, gxg = ug_ref[gi], dyg_ref[gi], gx_ref[:, ln]
            dm_ref[gi] = _dot(ug, dyg, TN, prec=S5_PREC)
            dc_ref[gi] = _dot(sp_ref[:, ln], dyg, TN, prec=S5_PREC)
            db_ref[gi] = _dot(ug, gxg, TN, prec=S5_PREC)
            dyg_ref[gi] = _dot(dyg, m_ref[gi], NT, prec=S5_PREC) + _dot(gxg, b_ref[gi], NT, prec=S5_PREC)
        _scatter_groups(dyg_ref, du_ref, nc)

    tok = pl.BlockSpec((n_rows, lanes), lambda g: (0, g), pipeline_mode=pl.Buffered(1))
    tok_s = pl.BlockSpec((nc, gb * p2), lambda g: (0, g))
    mat_m = pl.BlockSpec((gb, lh, lh), lambda g: (g, 0, 0))
    mat_b = pl.BlockSpec((gb, lh, p2), lambda g: (g, 0, 0))
    mat_c = pl.BlockSpec((gb, p2, lh), lambda g: (g, 0, 0))
    return pl.pallas_call(
        body, grid=(groups // gb,), name=name,
        in_specs=[tok, tok, tok_s, mat_m, mat_b, mat_c,
                  pl.BlockSpec((8, gb * p2), lambda g: (0, g)), pl.BlockSpec((16, gb * p2), lambda g: (0, g))],
        out_specs=[tok, mat_m, mat_b, mat_c, pl.BlockSpec((1, gb * p2), lambda g: (0, g))],
        out_shape=[jax.ShapeDtypeStruct((n_rows, width), F32), jax.ShapeDtypeStruct(m.shape, F32),
                   jax.ShapeDtypeStruct(bm.shape, F32), jax.ShapeDtypeStruct(cm.shape, F32),
                   jax.ShapeDtypeStruct((1, groups * p2), F32)],
        scratch_shapes=[pltpu.VMEM((gb, nc, lh), F32), pltpu.VMEM((gb, nc, lh), F32),
                        pltpu.VMEM((nc, gb * p2), F32), pltpu.VMEM((nc, gb * p2), F32)],
        compiler_params=_params("parallel"),
    )(u, dy, sprev, m, bm, cm, apw, arows_rev)


def _split(x):
    hi = x.astype(BF16)
    return hi, (x - hi.astype(F32)).astype(BF16)


def _sb_more(kb, carries):
    top = jnp.max(carries[0])
    for c in carries[1:]:
        top = jnp.maximum(top, jnp.max(c))
    return (kb >= 0) & (top > SB_UNDERFLOW)


def _sb_fwd(name, q, k, v):
    heads, n_rows, hd = q.shape
    tq = _tile(n_rows // 2, 256)
    hb = min(SB_HEADS_FWD, heads)
    scale = hd ** -0.5

    def body(q_ref, k_ref, v_ref, o_ref):
        qi = pl.program_id(1)
        hs = range(hb)
        row = lax.broadcasted_iota(jnp.int32, (tq, tq), 0)
        col = lax.broadcasted_iota(jnp.int32, (tq, tq), 1)
        tri = (row > col).astype(BF16)
        causal = col < row

        def block(kb, carries, accs, diagonal):
            ks = pl.ds(pl.multiple_of(kb * tq, tq), tq)
            z = [_dot(q_ref[h] * scale, k_ref[h, ks, :], NT) for h in hs]
            sp = [_softplus(z[h]) for h in hs]
            lk = [-sp[h] for h in hs]
            if diagonal:
                lk = [jnp.where(causal, lk[h], 0.0) for h in hs]
            parts = [_split(lk[h]) for h in hs]
            r = [_dot(parts[h][0], tri) + _dot(parts[h][1], tri) for h in hs]
            a = [jnp.exp(z[h] - sp[h] + r[h] + carries[h]) for h in hs]
            if diagonal:
                a = [jnp.where(causal, a[h], 0.0) for h in hs]
            new_a = tuple(accs[h] + _dot(a[h].astype(BF16), v_ref[h, ks, :]) for h in hs)
            new_c = tuple(carries[h] + jnp.sum(lk[h], axis=1, keepdims=True) for h in hs)
            return new_c, new_a

        zc = tuple(jnp.zeros((tq, 1), F32) for _ in hs)
        za = tuple(jnp.zeros((tq, hd), F32) for _ in hs)
        carries, accs = block(qi, zc, za, True)
        _, _, accs = lax.while_loop(lambda st: _sb_more(st[0], st[1]),
                                    lambda st: (st[0] - 1,) + block(st[0], st[1], st[2], False),
                                    (qi - 1, carries, accs))
        for h in hs:
            o_ref[h] = accs[h]

    kv = pl.BlockSpec((hb, n_rows, hd), lambda h, i: (h, 0, 0))
    qs = pl.BlockSpec((hb, tq, hd), lambda h, i: (h, i, 0))
    return pl.pallas_call(body, grid=(heads // hb, n_rows // tq), in_specs=[qs, kv, kv], out_specs=qs, name=name,
                          out_shape=jax.ShapeDtypeStruct((heads, n_rows, hd), F32),
                          compiler_params=_params("parallel", "arbitrary"))(q, k, v)


def _sb_bwd(name, q, k, v, o, do):
    heads, n_rows, hd = q.shape
    tq = _tile(n_rows // 2, 256)
    hb = min(SB_HEADS_BWD, heads)
    scale = hd ** -0.5

    def body(q_ref, k_ref, v_ref, o_ref, do_ref, dq_ref, dk_ref, dv_ref):
        qi = pl.program_id(1)

        @pl.when(qi == 0)
        def _():
            dk_ref[...] = jnp.zeros(dk_ref.shape, F32)
            dv_ref[...] = jnp.zeros(dv_ref.shape, F32)

        hs = range(hb)
        qb = [q_ref[h] * scale for h in hs]
        dob16 = [do_ref[h].astype(BF16) for h in hs]
        delta = [jnp.sum(dob16[h].astype(F32) * o_ref[h], axis=1, keepdims=True) for h in hs]
        row = lax.broadcasted_iota(jnp.int32, (tq, tq), 0)
        col = lax.broadcasted_iota(jnp.int32, (tq, tq), 1)
        tri = (row > col).astype(BF16)
        tri_incl = (row >= col).astype(BF16)
        causal = col < row

        def block(kb, carries, pcarries, dqs, diagonal):
            ks = pl.ds(pl.multiple_of(kb * tq, tq), tq)
            kblk = [k_ref[h, ks, :] for h in hs]
            vblk = [v_ref[h, ks, :] for h in hs]
            z = [_dot(qb[h], kblk[h], NT) for h in hs]
            da = [_dot(dob16[h], vblk[h], NT) for h in hs]
            sp = [_softplus(z[h]) for h in hs]
            lk = [-sp[h] for h in hs]
            if diagonal:
                lk = [jnp.where(causal, lk[h], 0.0) for h in hs]
            lb = [z[h] - sp[h] for h in hs]
            parts = [_split(lk[h]) for h in hs]
            r = [_dot(parts[h][0], tri) + _dot(parts[h][1], tri) for h in hs]
            a = [jnp.exp(lb[h] + r[h] + carries[h]) for h in hs]
            if diagonal:
                a = [jnp.where(causal, a[h], 0.0) for h in hs]
            a16 = [a[h].astype(BF16) for h in hs]
            p = [da[h] * a16[h].astype(F32) for h in hs]
            pparts = [_split(p[h]) for h in hs]
            pc = [_dot(pparts[h][0], tri_incl) + _dot(pparts[h][1], tri_incl) for h in hs]
            beta = [jnp.exp(lb[h]) for h in hs]
            dz = [p[h] * (1.0 - beta[h]) - beta[h] * (delta[h] - pcarries[h] - pc[h]) for h in hs]
            if diagonal:
                dz = [jnp.where(causal, dz[h], 0.0) for h in hs]
            dz16 = [dz[h].astype(BF16) for h in hs]
            for h in hs:
                dk_ref[h, ks, :] += _dot(dz16[h], qb[h], TN)
                dv_ref[h, ks, :] += _dot(a16[h], dob16[h], TN)
            return (tuple(carries[h] + jnp.sum(lk[h], axis=1, keepdims=True) for h in hs),
                    tuple(pcarries[h] + jnp.sum(p[h], axis=1, keepdims=True) for h in hs),
                    tuple(dqs[h] + _dot(dz16[h], kblk[h]) for h in hs))

        zc = tuple(jnp.zeros((tq, 1), F32) for _ in hs)
        zq = tuple(jnp.zeros((tq, hd), F32) for _ in hs)
        st = block(qi, zc, zc, zq, True)
        st = lax.while_loop(lambda s: _sb_more(s[0], s[1]),
                            lambda s: (s[0] - 1,) + block(s[0], s[1], s[2], s[3], False), (qi - 1,) + st)
        for h in hs:
            dq_ref[h] = st[3][h] * scale

    kv = pl.BlockSpec((hb, n_rows, hd), lambda h, i: (h, 0, 0))
    qs = pl.BlockSpec((hb, tq, hd), lambda h, i: (h, i, 0))
    full = jax.ShapeDtypeStruct((heads, n_rows, hd), F32)
    return pl.pallas_call(body, grid=(heads // hb, n_rows // tq), in_specs=[qs, kv, kv, qs, qs],
                          out_specs=[qs, kv, kv], out_shape=[full, full, full], name=name,
                          compiler_params=_params("parallel", "arbitrary"))(q, k, v, o, do)


def _block_diag(xb, w_ref_val, dims):
    nb = w_ref_val.shape[0]
    bw = xb.shape[1] // nb
    return jnp.concatenate([_dot(xb[:, n * bw:(n + 1) * bw], w_ref_val[n], dims) for n in range(nb)], axis=1)


def _lru_gates_fwd(name, gx, conv_w, conv_b, wa, ba, wx, bx, lam):
    n_rows, w2 = gx.shape
    w = w2 // 2
    tm = _tile(n_rows, 256)

    def fn(i, nt, br, prev, cw, cb, wa_v, ba_v, wx_v, bx_v, lam_v):
        xc = cb + sum(cw[k:k + 1] * _shift_down(br, prev, LRU_CONV - 1 - k, i == 0) for k in range(LRU_CONV))
        xb = xc.astype(BF16)
        r = _sigmoid(_block_diag(xb, wa_v, NN) + ba_v)
        ig = _sigmoid(_block_diag(xb, wx_v, NN) + bx_v)
        log_a = (-LRU_C * r) * _softplus(-lam_v)
        a = jnp.exp(log_a)
        gated = (ig * xc) * _one_minus_a2_sqrt(log_a)
        return (xc, r, ig, a, gated), ()

    return _rowwise(name, fn, [('t', gx, w, 1), ('p', gx, w, 1), ('b', conv_w), ('b', conv_b), ('b', wa), ('b', ba),
                               ('b', wx), ('b', bx), ('b', lam)], [(w, F32)] * 5, [], n_rows, tm)


def _lru_scan_fwd(name, a, gated, gx):
    n_rows, w = a.shape
    tm = _tile(n_rows, 256)

    def body(a_ref, x_ref, bg_ref, hs_ref, y_ref, carry_ref):
        @pl.when(pl.program_id(0) == 0)
        def _():
            carry_ref[...] = jnp.zeros(carry_ref.shape, F32)
        carry_ref[...] = _scan_tile(a_ref, x_ref, hs_ref, carry_ref[...], False, tm)
        y_ref[...] = (_gelu(bg_ref[...]) * hs_ref[...]).astype(BF16)

    tok = pl.BlockSpec((tm, w), lambda i: (i, 0))
    return pl.pallas_call(body, grid=(n_rows // tm,), in_specs=[tok, tok, tok], out_specs=[tok, tok], name=name,
                          out_shape=[jax.ShapeDtypeStruct((n_rows, w), F32), jax.ShapeDtypeStruct((n_rows, w), BF16)],
                          scratch_shapes=[pltpu.VMEM((1, w), F32)], compiler_params=_params("arbitrary"))(a, gated, gx)


def _lru_scan_bwd(name, a, dy, gx):
    n_rows, w = a.shape
    tm = _tile(n_rows, 256)
    nt = n_rows // tm
    per8 = tm // SUBLANES

    def body(a_ref, an_ref, dy_ref, bg_ref, lam_ref, carry_ref, aup_ref, dhs_ref):
        i = pl.program_id(0)

        @pl.when(i == 0)
        def _():
            carry_ref[...] = jnp.zeros(carry_ref.shape, F32)
        aup_ref[...] = _shift_up(a_ref[...], an_ref[...], 1, i == 0)
        dhs_ref[...] = dy_ref[...] * _gelu(bg_ref[...])
        carry_ref[...] = _scan_tile(aup_ref, dhs_ref, lam_ref, carry_ref[...], True, tm)

    tok = pl.BlockSpec((tm, w), lambda i: (nt - 1 - i, 0))
    nxt = pl.BlockSpec((SUBLANES, w), lambda i: (jnp.minimum((nt - i) * per8, n_rows // SUBLANES - 1), 0))
    return pl.pallas_call(body, grid=(nt,), in_specs=[tok, nxt, tok, tok], out_specs=tok, name=name,
                          out_shape=jax.ShapeDtypeStruct((n_rows, w), F32),
                          scratch_shapes=[pltpu.VMEM((1, w), F32), pltpu.VMEM((tm, w), F32), pltpu.VMEM((tm, w), F32)],
                          compiler_params=_params("arbitrary"))(a, a, dy, gx)


def _lru_gates_bwd(name, lam_t, hs, xc, r, ig, a, wa, wx, lam):
    n_rows, w = xc.shape
    nb, bw, _ = wa.shape
    tm = _tile(n_rows, 256)

    def fn(i, nt, lt, hs_v, hs_prev, xc_v, r_v, ig_v, a_v, wa_v, wx_v, lam_v):
        sp = _softplus(-lam_v)
        log_a = (-LRU_C * r_v) * sp
        mult = _one_minus_a2_sqrt(log_a)
        d_a = lt * _shift_down(hs_v, hs_prev, 1, i == 0)
        d_ig = lt * xc_v * mult
        d_mult = lt * ig_v * xc_v
        d_log_a = d_a * a_v - d_mult * (a_v * a_v) / mult
        d_ra = d_log_a * (-LRU_C * sp) * r_v * (1.0 - r_v)
        d_ia = d_ig * ig_v * (1.0 - ig_v)
        d_ra16, d_ia16, xb = d_ra.astype(BF16), d_ia.astype(BF16), xc_v.astype(BF16)
        dxc = lt * ig_v * mult + _block_diag(d_ra16, wa_v, NT) + _block_diag(d_ia16, wx_v, NT)
        dwa = jnp.concatenate([_dot(xb[:, n * bw:(n + 1) * bw], d_ra16[:, n * bw:(n + 1) * bw], TN)
                               for n in range(nb)], axis=0)
        dwx = jnp.concatenate([_dot(xb[:, n * bw:(n + 1) * bw], d_ia16[:, n * bw:(n + 1) * bw], TN)
                               for n in range(nb)], axis=0)
        col = lambda t: jnp.sum(t, axis=0, keepdims=True)
        return (dxc,), (dwa, dwx, col(d_ra), col(d_ia), col(d_log_a * (-LRU_C * r_v)))

    tiled = lambda arr: ('t', arr, w, 0)
    return _rowwise(name, fn, [tiled(lam_t), tiled(hs), ('p', hs, w, 0), tiled(xc), tiled(r), tiled(ig), tiled(a),
                               ('b', wa), ('b', wx), ('b', lam)],
                    [(w, F32)], [(nb * bw, bw), (nb * bw, bw), (1, w), (1, w), (1, w)], n_rows, tm)


def _lru_conv_bwd(name, dxc, gx, dy, hs, conv_w):
    n_rows, w = dxc.shape
    tm = _tile(n_rows, 256)

    def fn(i, nt, dxc_v, dxc_next, bg, br, br_prev, dy_v, hs_v, cw):
        dbr = sum(cw[k:k + 1] * _shift_up(dxc_v, dxc_next, LRU_CONV - 1 - k, i == nt - 1) for k in range(LRU_CONV))
        dbg = dy_v * hs_v * _gelu_grad(bg)
        dcw = [jnp.sum(dxc_v * _shift_down(br, br_prev, LRU_CONV - 1 - k, i == 0), axis=0, keepdims=True)
               for k in range(LRU_CONV)]
        dcw = jnp.concatenate(dcw + [jnp.zeros((SUBLANES - LRU_CONV, w), F32)], axis=0)
        return (jnp.concatenate([dbg, dbr], axis=1),), (dcw, jnp.sum(dxc_v, axis=0, keepdims=True))

    return _rowwise(name, fn, [('t', dxc, w, 0), ('n', dxc, w, 0), ('t', gx, w, 0), ('t', gx, w, 1), ('p', gx, w, 1),
                               ('t', dy, w, 0), ('t', hs, w, 0), ('b', conv_w)],
                    [(2 * w, BF16)], [(SUBLANES, w), (1, w)], n_rows, tm)


def _loss_head(name, h, gain, target):
    n_rows, dm = h.shape
    tm = _tile(n_rows, 512)

    def fn(i, nt, hv, tv, g):
        r, xhat = _rms(hv)
        err = xhat * g - tv
        dy = err * (1.0 / dm)
        return ((_rms_bwd(dy, xhat, r, g),),
                (jnp.sum(err * err, axis=0, keepdims=True), jnp.sum(dy * xhat, axis=0, keepdims=True)))

    return _rowwise(name, fn, [('t', h, dm, 0), ('t', target, dm, 0), ('b', gain)], [(dm, F32)], [(1, dm), (1, dm)],
                    n_rows, tm)


def _adamw(name, gparts, w, m, v):
    n_parts, n_rows, cols = gparts.shape
    tr = n_rows
    for cand in (256, 128, 64, 32, 16, 8):
        if n_rows % cand == 0:
            tr = cand
            break
    c1 = 1.0 - ADAM_B1 ** ADAM_STEP
    c2 = 1.0 - ADAM_B2 ** ADAM_STEP

    def body(gp_ref, w_ref, m_ref, v_ref, g_ref, d_ref, nm_ref, nv_ref):
        g = gp_ref[0].astype(F32)
        for p in range(1, n_parts):
            g = g + gp_ref[p].astype(F32)
        m_new = ADAM_B1 * m_ref[...] + (1.0 - ADAM_B1) * g
        v_new = ADAM_B2 * v_ref[...] + (1.0 - ADAM_B2) * (g * g)
        m_hat = m_new / c1
        v_hat = v_new / c2
        g_ref[...] = g
        d_ref[...] = -ADAM_LR * (m_hat / (jnp.sqrt(v_hat) + ADAM_EPS) + ADAM_WD * w_ref[...])
        nm_ref[...] = m_new
        nv_ref[...] = v_new

    blk = pl.BlockSpec((tr, cols), lambda i: (i, 0))
    shp = jax.ShapeDtypeStruct((n_rows, cols), F32)
    return pl.pallas_call(body, grid=(n_rows // tr,), name=name,
                          in_specs=[pl.BlockSpec((n_parts, tr, cols), lambda i: (0, i, 0)), blk, blk, blk],
                          out_specs=[blk, blk, blk, blk], out_shape=[shp, shp, shp, shp],
                          compiler_params=_params("parallel"))(gparts, w, m, v)


def _adamw_layers(name, recvs, w, m, v):
    n_layers, n_rows, cols = w.shape
    n_parts = recvs[0].shape[0]
    tr = max(t for t in range(16, ADAMW_LAYER_ROWS + 1, 16) if n_rows % t == 0)
    c1 = 1.0 - ADAM_B1 ** ADAM_STEP
    c2 = 1.0 - ADAM_B2 ** ADAM_STEP

    def body(*refs):
        gp_refs = refs[:n_layers]
        w_ref, m_ref, v_ref, g_ref, d_ref, nm_ref, nv_ref = refs[n_layers:]
        layer = pl.program_id(0)
        for k in range(n_layers):
            @pl.when(layer == k)
            def _(k=k):
                g = gp_refs[k][0].astype(F32)
                for p in range(1, n_parts):
                    g = g + gp_refs[k][p].astype(F32)
                m_new = ADAM_B1 * m_ref[...] + (1.0 - ADAM_B1) * g
                v_new = ADAM_B2 * v_ref[...] + (1.0 - ADAM_B2) * (g * g)
                g_ref[...] = g
                d_ref[...] = -ADAM_LR * ((m_new / c1) / (jnp.sqrt(v_new / c2) + ADAM_EPS) + ADAM_WD * w_ref[...])
                nm_ref[...] = m_new
                nv_ref[...] = v_new

    blk = pl.BlockSpec((None, tr, cols), lambda l, i: (l, i, 0))
    shp = jax.ShapeDtypeStruct((n_layers, n_rows, cols), F32)
    gp_specs = [pl.BlockSpec((n_parts, tr, cols), lambda l, i, k=k: (0, jnp.where(l == k, i, 0), 0))
                for k in range(n_layers)]
    return pl.pallas_call(body, grid=(n_layers, n_rows // tr), name=name, in_specs=gp_specs + [blk, blk, blk],
                          out_specs=[blk, blk, blk, blk], out_shape=[shp, shp, shp, shp],
                          compiler_params=_params("arbitrary", "arbitrary"))(*recvs, w, m, v)


def _pack_rows(arrays, cols, lead=0):
    flat = [a.reshape(a.shape[:lead] + (-1,)) for a in arrays]
    cat = jnp.concatenate(flat, axis=lead) if len(flat) > 1 else flat[0]
    n = cat.shape[lead]
    pad = (-n) % (cols * PACK_ROWS)
    if pad:
        cat = jnp.pad(cat, [(0, 0)] * lead + [(0, pad)])
    return cat.reshape(cat.shape[:lead] + (-1, cols))


def _unpack_rows(packed, shapes, lead=0):
    flat = packed.reshape(packed.shape[:lead] + (-1,))
    out, off = [], 0
    for s in shapes:
        n = math.prod(s)
        out.append(lax.slice_in_dim(flat, off, off + n, axis=lead).reshape(flat.shape[:lead] + tuple(s)))
        off += n
    return out


def kernel(x, ffn1_norm, ffn1_w_in, ffn1_w_out, mix_norm, ffn2_norm, ffn2_w_in, ffn2_w_out, final_norm, s5_w_in, s5_lam_re, s5_lam_im, s5_log_dt, s5_b_re, s5_b_im, s5_c_re, s5_c_im, s5_d, s5_w_out, sb_w_qkv, sb_w_out, lru_w_in, lru_conv_w, lru_conv_b, lru_w_a, lru_b_a, lru_w_x, lru_b_x, lru_lambda, lru_w_out, loss_target, m_ffn1_norm, m_ffn1_w_in, m_ffn1_w_out, m_mix_norm, m_ffn2_norm, m_ffn2_w_in, m_ffn2_w_out, m_final_norm, m_s5_w_in, m_s5_lam_re, m_s5_lam_im, m_s5_log_dt, m_s5_b_re, m_s5_b_im, m_s5_c_re, m_s5_c_im, m_s5_d, m_s5_w_out, m_sb_w_qkv, m_sb_w_out, m_lru_w_in, m_lru_conv_w, m_lru_conv_b, m_lru_w_a, m_lru_b_a, m_lru_w_x, m_lru_b_x, m_lru_lambda, m_lru_w_out, v_ffn1_norm, v_ffn1_w_in, v_ffn1_w_out, v_mix_norm, v_ffn2_norm, v_ffn2_w_in, v_ffn2_w_out, v_final_norm, v_s5_w_in, v_s5_lam_re, v_s5_lam_im, v_s5_log_dt, v_s5_b_re, v_s5_b_im, v_s5_c_re, v_s5_c_im, v_s5_d, v_s5_w_out, v_sb_w_qkv, v_sb_w_out, v_lru_w_in, v_lru_conv_w, v_lru_conv_b, v_lru_w_a, v_lru_b_a, v_lru_w_x, v_lru_b_x, v_lru_lambda, v_lru_w_out):
    local = dict(locals())
    W = {n: local[n] for n in WEIGHTS}
    M = {n: local["m_" + n] for n in WEIGHTS}
    V = {n: local["v_" + n] for n in WEIGHTS}

    h0 = x[0]
    target = loss_target[0]
    n_rows, dm = h0.shape
    depth = ffn1_norm.shape[0]

    ffn_seq = [(tag, layer) for layer in range(depth) for tag in ("ffn1", "ffn2")]

    def ffn_shards(tag, layer):
        return W[tag + "_w_in"][layer].astype(BF16), W[tag + "_w_out"][layer].astype(BF16)

    def ffn_views(wi, wo):
        return wi.reshape((2, N_DEV // 2) + wi.shape[1:]), wo.reshape((N_DEV // 2, 2) + wo.shape[1:])

    first_in, first_out = ffn_shards(*ffn_seq[0])
    ffn_w = {ffn_seq[0]: ffn_views(_all_gather("ag_first_w_in", first_in), _all_gather("ag_first_w_out", first_out))}

    def ffn_forward(pos, h_in):
        tag, layer = ffn_seq[pos]
        comms = [("gather", a) for a in ffn_shards(*ffn_seq[pos + 1])] if pos + 1 < len(ffn_seq) else []
        res = _ffn_fwd("%s_fwd_%d" % (tag, layer), h_in, W[tag + "_norm"][layer:layer + 1], *ffn_w[ffn_seq[pos]],
                       comms=comms)
        if comms:
            ffn_w[ffn_seq[pos + 1]] = ffn_views(res[1], res[2])
        return res[0]

    mix_shapes = [W[n].shape for n in MIXER_BIG]
    mix_g = _all_gather("ag_mixers", _pack_rows([W[n].astype(BF16) for n in MIXER_BIG], dm))
    full = {n: _unshard(a, SHARD_AXIS[n]) for n, a in zip(MIXER_BIG, _unpack_rows(mix_g, mix_shapes, lead=1))}
    small_shapes = [W[n].shape for n in SMALL_SHARDED]
    small_g = _all_gather("ag_small", _pack_rows([W[n] for n in SMALL_SHARDED], 128))
    full.update({n: _unshard(a, SHARD_AXIS[n])
                 for n, a in zip(SMALL_SHARDED, _unpack_rows(small_g, small_shapes, lead=1))})

    n_s5 = s5_w_in.shape[0]
    s5_groups = s5_lam_re.shape[1]
    heads = dm // SB_HEAD_DIM

    grads = {}
    saved = []
    h = h0

    for layer in range(depth):
        kind, j = layer % 3, layer // 3
        rec = {"h0": h}
        h = ffn_forward(2 * layer, h)
        rec["h1"] = h
        gain = mix_norm[layer:layer + 1]
        if kind == 0:
            (u,) = _mm_fwd("s5_in_%d" % layer, h, full["s5_w_in"][j], F32, gain=gain)
            pars = (s5_lam_re[j], s5_lam_im[j], s5_log_dt[j], s5_b_re[j], s5_b_im[j], s5_c_re[j], s5_c_im[j])
            mats, mats_vjp = jax.vjp(_s5_mats, *pars)
            apw, ar_fwd, ar_rev = _s5_powers(*pars[:3])
            ys, sprev = _s5_fwd("s5_core_%d" % layer, u, *mats[:3], apw, ar_fwd)
            d_skip = full["s5_d"][j:j + 1]
            (z,) = _rowwise("s5_gelu_%d" % layer, lambda i, nt, ys_v, u_v, d_v: ((_gelu(ys_v + d_v * u_v),), ()),
                            [('t', ys, dm, 0), ('t', u, dm, 0), ('b', d_skip)], [(dm, BF16)], [], n_rows,
                            _tile(n_rows, 512))
            h, vg = _mm_fwd("s5_out_%d" % layer, z, full["s5_w_out"][j], F32, resid=h, glu=True)
            rec.update(u=u, ys=ys, sprev=sprev, z=z, vg=vg, mats=mats, mats_vjp=mats_vjp, apw=apw,
                       ar_rev=ar_rev, d_skip=d_skip)
        elif kind == 1:
            (qkv,) = _mm_fwd("sb_in_%d" % layer, h, full["sb_w_qkv"][j], BF16, gain=gain)
            qkv_h = qkv.reshape(n_rows, 3, heads, SB_HEAD_DIM).transpose(1, 2, 0, 3)
            o = _sb_fwd("sb_attn_%d" % layer, qkv_h[0], qkv_h[1], qkv_h[2])
            o_flat = o.transpose(1, 0, 2).reshape(n_rows, dm).astype(BF16)
            (h,) = _mm_fwd("sb_out_%d" % layer, o_flat, full["sb_w_out"][j], F32, resid=h)
            rec.update(qkv_h=qkv_h, o=o, o_flat=o_flat)
        else:
            (gx,) = _mm_fwd("lru_in_%d" % layer, h, full["lru_w_in"][j], F32, gain=gain)
            wa, wx = full["lru_w_a"][j], full["lru_w_x"][j]
            ba, bx = full["lru_b_a"][j].reshape(1, dm), full["lru_b_x"][j].reshape(1, dm)
            lam_row = full["lru_lambda"][j:j + 1]
            xc, r, ig, a, gated = _lru_gates_fwd("lru_gates_%d" % layer, gx, full["lru_conv_w"][j],
                                                 full["lru_conv_b"][j:j + 1], wa, ba, wx, bx, lam_row)
            hs, y = _lru_scan_fwd("lru_scan_%d" % layer, a, gated, gx)
            (h,) = _mm_fwd("lru_out_%d" % layer, y, full["lru_w_out"][j], F32, resid=h)
            rec.update(gx=gx, xc=xc, r=r, ig=ig, a=a, hs=hs, y=y, wa=wa, wx=wx, lam_row=lam_row)
        rec["h2"] = h
        h = ffn_forward(2 * layer + 1, h)
        saved.append(rec)

    dh, err2, dgf = _loss_head("loss_head", h, final_norm.reshape(1, dm), target)
    loss = lax.psum(0.5 / dm * jnp.sum(err2), ("x", "y", "c"))
    grads["final_norm"] = dgf.reshape(final_norm.shape)

    per_layer = {n: [None] * depth for n in ("ffn1_norm", "mix_norm", "ffn2_norm")}
    mixer_grads = {}
    recv_ffn = {}
    pending = []

    def ffn_backward(tag, layer, x_in, dh_in, more=None):
        comms = {0: [("exchange", pending[1])], 1: [("exchange", pending[2])]} if pending else {}
        comms.update(more or {})
        dx, dwi, dwo, dg, extra = _ffn_bwd("%s_bwd_%d" % (tag, layer), x_in, dh_in, W[tag + "_norm"][layer:layer + 1],
                                            *ffn_w[(tag, layer)], comms_by_block=comms)
        if pending:
            recv_ffn[pending[0]] = tuple(extra[:2])
            extra = extra[2:]
        pending[:] = [(tag, layer), dwi.reshape((N_DEV,) + dwi.shape[2:]).astype(BF16),
                      dwo.reshape(N_DEV, -1, dm).astype(BF16)]
        per_layer[tag + "_norm"][layer] = dg
        return dx, extra

    def put(name, j, value, count):
        mixer_grads.setdefault(name, [None] * count)[j] = value

    for layer in reversed(range(depth)):
        kind, j = layer % 3, layer // 3
        rec = saved[layer]
        dh, _ = ffn_backward("ffn2", layer, rec["h2"], dh)
        gain = mix_norm[layer:layer + 1]
        if kind == 0:
            dvg, = _rowwise("s5_glu_bwd_%d" % layer,
                            lambda i, nt, d_v, vg_v: ((jnp.concatenate(
                                [d_v * _sigmoid(vg_v[:, dm:]),
                                 d_v * vg_v[:, :dm] * _sigmoid(vg_v[:, dm:]) * (1.0 - _sigmoid(vg_v[:, dm:]))],
                                axis=1),), ()),
                            [('t', dh, dm, 0), ('t', rec["vg"], 2 * dm, 0)], [(2 * dm, BF16)], [], n_rows,
                            _tile(n_rows, 256))
            dz, dw_out = _mm_bwd("s5_out_bwd_%d" % layer, rec["z"], dvg, full["s5_w_out"][j])

            def gelu_bwd(i, nt, dz_v, ys_v, u_v, d_v):
                dy_v = dz_v * _gelu_grad(ys_v + d_v * u_v)
                return (dy_v,), (jnp.sum(dy_v * u_v, axis=0, keepdims=True),)

            dys, dd = _rowwise("s5_gelu_bwd_%d" % layer, gelu_bwd,
                               [('t', dz, dm, 0), ('t', rec["ys"], dm, 0), ('t', rec["u"], dm, 0),
                                ('b', rec["d_skip"])], [(dm, F32)], [(1, dm)], n_rows, _tile(n_rows, 512))
            m_, bm_, cm_, _ = rec["mats"]
            du_core, dm_m, dm_b, dm_c, d_a = _s5_bwd("s5_core_bwd_%d" % layer, rec["u"], dys, rec["sprev"],
                                                    m_, bm_, cm_, rec["apw"], rec["ar_rev"])
            dpars = rec["mats_vjp"]((dm_m, dm_b, dm_c, d_a.reshape(s5_groups, -1)))
            for nme, val in zip(("s5_lam_re", "s5_lam_im", "s5_log_dt", "s5_b_re", "s5_b_im", "s5_c_re", "s5_c_im"),
                                dpars):
                put(nme, j, val, n_s5)
            (du,) = _rowwise("s5_du_%d" % layer, lambda i, nt, a_v, dy_v, d_v: ((a_v + dy_v * d_v,), ()),
                             [('t', du_core, dm, 0), ('t', dys, dm, 0), ('b', rec["d_skip"])],
                             [(dm, BF16)], [], n_rows, _tile(n_rows, 512))
            dh, dw_in, dgm = _mm_bwd("s5_in_bwd_%d" % layer, rec["h1"], du, full["s5_w_in"][j], gain=gain, dres=dh)
            put("s5_d", j, dd[0], n_s5)
            put("s5_w_out", j, dw_out, n_s5)
            put("s5_w_in", j, dw_in, n_s5)
        elif kind == 1:
            do_flat, dw_out = _mm_bwd("sb_out_bwd_%d" % layer, rec["o_flat"], dh, full["sb_w_out"][j])
            do = do_flat.reshape(n_rows, heads, SB_HEAD_DIM).transpose(1, 0, 2)
            qkv_h = rec["qkv_h"]
            dq, dk, dv = _sb_bwd("sb_attn_bwd_%d" % layer, qkv_h[0], qkv_h[1], qkv_h[2], rec["o"], do)
            dqkv = jnp.stack([dq, dk, dv]).transpose(2, 0, 1, 3).reshape(n_rows, 3 * dm).astype(BF16)
            dh, dw_in, dgm = _mm_bwd("sb_in_bwd_%d" % layer, rec["h1"], dqkv, full["sb_w_qkv"][j], gain=gain, dres=dh)
            put("sb_w_out", j, dw_out, 1)
            put("sb_w_qkv", j, dw_in, 1)
        else:
            dy, dw_out = _mm_bwd("lru_out_bwd_%d" % layer, rec["y"], dh, full["lru_w_out"][j])
            lam_t = _lru_scan_bwd("lru_scan_bwd_%d" % layer, rec["a"], dy, rec["gx"])
            dxc, dwa, dwx, dba, dbx, dsp = _lru_gates_bwd("lru_gates_bwd_%d" % layer, lam_t, rec["hs"], rec["xc"],
                                                          rec["r"], rec["ig"], rec["a"], rec["wa"], rec["wx"],
                                                          rec["lam_row"])
            dgx, dcw, dcb = _lru_conv_bwd("lru_conv_bwd_%d" % layer, dxc, rec["gx"], dy, rec["hs"],
                                          full["lru_conv_w"][j])
            dh, dw_in, dgm = _mm_bwd("lru_in_bwd_%d" % layer, rec["h1"], dgx, full["lru_w_in"][j], gain=gain, dres=dh)
            nb = rec["wa"].shape[0]
            put("lru_w_out", j, dw_out, 1)
            put("lru_w_in", j, dw_in, 1)
            put("lru_w_a", j, dwa.reshape(rec["wa"].shape), 1)
            put("lru_w_x", j, dwx.reshape(rec["wx"].shape), 1)
            put("lru_b_a", j, dba.reshape(nb, -1), 1)
            put("lru_b_x", j, dbx.reshape(nb, -1), 1)
            put("lru_conv_w", j, dcw[:LRU_CONV], 1)
            put("lru_conv_b", j, dcb[0], 1)
            put("lru_lambda", j, (dsp * -_sigmoid(-rec["lam_row"]))[0], 1)
        per_layer["mix_norm"][layer] = dgm
        more = None
        if layer == 0:
            for n, parts in mixer_grads.items():
                grads[n] = jnp.stack(parts)
            send = _pack_rows([_shard_blocks(grads[n], SHARD_AXIS[n]).astype(BF16) for n in MIXER_BIG], dm, lead=1)
            half = send.shape[1] // 2
            more = {2: [("exchange", send[:, :half])], 3: [("exchange", send[:, half:])]}
        dh, got = ffn_backward("ffn1", layer, rec["h0"], dh, more)
        if layer == 0:
            recv_mixers = jnp.concatenate(got, axis=1)

    grad_x = dh[None]
    for n in ("ffn1_norm", "mix_norm", "ffn2_norm"):
        grads[n] = jnp.concatenate(per_layer[n], axis=0)

    out_g, out_d, out_m, out_v = {}, {}, {}, {}

    def finish(names, res, shapes):
        for n, g_, d_, m_, v_ in zip(names, *[_unpack_rows(t, shapes) for t in res]):
            out_g[n], out_d[n], out_m[n], out_v[n] = g_, d_, m_, v_

    recv_ffn[pending[0]] = (_exchange("xchg_last_w_in", pending[1]), _exchange("xchg_last_w_out", pending[2]))
    for tag in ("ffn1", "ffn2"):
        for which, n in enumerate((tag + "_w_in", tag + "_w_out")):
            out_g[n], out_d[n], out_m[n], out_v[n] = _adamw_layers(
                "adamw_" + n, [recv_ffn[(tag, layer)][which] for layer in range(depth)], W[n], M[n], V[n])

    finish(MIXER_BIG, _adamw("adamw_mixers", recv_mixers,
                             *[_pack_rows([t[n] for n in MIXER_BIG], dm) for t in (W, M, V)]), mix_shapes)

    small_names = REPLICATED + SMALL_SHARDED
    small_full_shapes = [grads[n].shape for n in small_names]
    parts = _all_gather("ag_small_grads", _pack_rows([grads[n] for n in small_names], 128))
    zero = jnp.zeros(parts.shape[1:], F32)
    summed = _adamw("sum_small_grads", parts, zero, zero, zero)[0]
    small_sum = dict(zip(small_names, _unpack_rows(summed, small_full_shapes)))
    me = 4 * lax.axis_index("x") + 2 * lax.axis_index("y") + lax.axis_index("c")
    rep_shapes = [W[n].shape for n in REPLICATED]
    g_rep = _pack_rows([small_sum[n] for n in REPLICATED], 128)[None]
    finish(REPLICATED, _adamw("adamw_replicated", g_rep, *[_pack_rows([t[n] for n in REPLICATED], 128)
                                                           for t in (W, M, V)]), rep_shapes)
    g_loc = []
    for n in SMALL_SHARDED:
        ax = SHARD_AXIS[n]
        size = W[n].shape[ax]
        g_loc.append(lax.dynamic_slice_in_dim(small_sum[n], me * size, size, axis=ax))
    finish(SMALL_SHARDED, _adamw("adamw_small", _pack_rows(g_loc, 128)[None],
                                 *[_pack_rows([t[n] for n in SMALL_SHARDED], 128) for t in (W, M, V)]), small_shapes)

    return (loss, grad_x, *[out_g[n] for n in WEIGHTS], *[out_d[n] for n in WEIGHTS],
            *[out_m[n] for n in WEIGHTS], *[out_v[n] for n in WEIGHTS])
```

```python
import functools
import math

import jax
import jax.numpy as jnp
from jax import lax
from jax.experimental import pallas as pl
from jax.experimental.pallas import tpu as pltpu

F32 = jnp.float32
BF16 = jnp.bfloat16
HI = lax.Precision.HIGHEST
S5_PREC = lax.Precision.HIGH
MESH = pl.DeviceIdType.MESH

N_DEV = 8
RMS_EPS = 1e-6
S5_GROUP = 16
S5_CHUNK = 16
S5_OCTET = 128 // S5_GROUP
S5_REGROUP_ROWS = 32
S5_SCAN_UNROLL = 8
SB_HEAD_DIM = 64
SB_UNDERFLOW = -104.0
SB_HEADS_FWD = 4
SB_HEADS_BWD = 2
LRU_CONV = 4
LRU_C = 8.0
ADAM_LR, ADAM_B1, ADAM_B2, ADAM_EPS, ADAM_WD, ADAM_STEP = 0.001, 0.9, 0.999, 1e-08, 0.01, 10
VMEM_LIMIT_BYTES = 56 * 1024 * 1024
SUBLANES = 8
PACK_ROWS = 256
ADAMW_LAYER_ROWS = 192

NN = (((1,), (0,)), ((), ()))
NT = (((1,), (1,)), ((), ()))
TN = (((0,), (0,)), ((), ()))

SHARD_AXIS = dict(
    ffn1_w_in=2, ffn1_w_out=1, ffn2_w_in=2, ffn2_w_out=1, s5_w_in=1, s5_d=1, s5_w_out=2, sb_w_qkv=2, sb_w_out=1,
    lru_w_in=2, lru_conv_w=2, lru_conv_b=1, lru_w_a=2, lru_b_a=2, lru_w_x=2, lru_b_x=2, lru_lambda=1, lru_w_out=1)
MIXER_BIG = ("s5_w_in", "s5_w_out", "sb_w_qkv", "sb_w_out", "lru_w_in", "lru_w_a", "lru_w_x", "lru_w_out")
SMALL_SHARDED = ("s5_d", "lru_conv_w", "lru_conv_b", "lru_b_a", "lru_b_x", "lru_lambda")
REPLICATED = ("ffn1_norm", "mix_norm", "ffn2_norm", "final_norm", "s5_lam_re", "s5_lam_im", "s5_log_dt",
              "s5_b_re", "s5_b_im", "s5_c_re", "s5_c_im")
WEIGHTS = ("ffn1_norm", "ffn1_w_in", "ffn1_w_out", "mix_norm", "ffn2_norm", "ffn2_w_in", "ffn2_w_out", "final_norm",
           "s5_w_in", "s5_lam_re", "s5_lam_im", "s5_log_dt", "s5_b_re", "s5_b_im", "s5_c_re", "s5_c_im", "s5_d",
           "s5_w_out", "sb_w_qkv", "sb_w_out", "lru_w_in", "lru_conv_w", "lru_conv_b", "lru_w_a", "lru_b_a",
           "lru_w_x", "lru_b_x", "lru_lambda", "lru_w_out")


def _dot(a, b, dims=NN, prec=None):
    return lax.dot_general(a, b, dims, precision=prec, preferred_element_type=F32)


def _params(*sem):
    return pltpu.CompilerParams(dimension_semantics=sem, vmem_limit_bytes=VMEM_LIMIT_BYTES)


def _tile(n, pref):
    return min(pref, n)


def _sigmoid(x):
    return jax.nn.sigmoid(x)


def _softplus(x):
    return jnp.maximum(x, 0.0) + jnp.log(1.0 + jnp.exp(-jnp.abs(x)))


_GELU_C = math.sqrt(2.0 / math.pi)


def _gelu(x):
    return 0.5 * x * (1.0 + jnp.tanh(_GELU_C * (x + 0.044715 * x * x * x)))


def _gelu_grad(x):
    t = jnp.tanh(_GELU_C * (x + 0.044715 * x * x * x))
    return 0.5 * (1.0 + t) + 0.5 * x * (1.0 - t * t) * _GELU_C * (1.0 + 3.0 * 0.044715 * x * x)


def _rms(x):
    r = lax.rsqrt(jnp.mean(x * x, axis=1, keepdims=True) + RMS_EPS)
    return r, x * r


def _rms_bwd(dhn, xhat, r, g):
    dxhat = dhn * g
    return r * (dxhat - xhat * jnp.mean(dxhat * xhat, axis=1, keepdims=True))


def _one_minus_a2_sqrt(log_a):
    t = jnp.tanh(log_a)
    return jnp.sqrt(-2.0 * t / (1.0 - t))


def _shift_down(cur, prev8, k, first):
    if k == 0:
        return cur
    row8 = lax.broadcasted_iota(jnp.int32, prev8.shape, 0)
    rolled = pltpu.roll(cur, k, 0)
    edge = jnp.where(first, 0.0, pltpu.roll(prev8, k, 0))
    top = jnp.where(row8 < k, edge, rolled[0:SUBLANES])
    return jnp.concatenate([top, rolled[SUBLANES:]], axis=0)


def _shift_up(cur, next8, k, last):
    if k == 0:
        return cur
    tm = cur.shape[0]
    row8 = lax.broadcasted_iota(jnp.int32, next8.shape, 0)
    rolled = pltpu.roll(cur, tm - k, 0)
    edge = jnp.where(last, 0.0, pltpu.roll(next8, SUBLANES - k, 0))
    bottom = jnp.where(row8 >= SUBLANES - k, edge, rolled[tm - SUBLANES:tm])
    return jnp.concatenate([rolled[:tm - SUBLANES], bottom], axis=0)


def _rowwise(name, fn, ins, out_tiled, out_acc, n_rows, tm, reverse=False):
    nt = n_rows // tm
    per8 = tm // SUBLANES
    n8 = n_rows // SUBLANES
    n_in, n_ot = len(ins), len(out_tiled)

    def pos(i):
        return nt - 1 - i if reverse else i

    in_specs, args = [], []
    for spec in ins:
        kind, arr = spec[0], spec[1]
        args.append(arr)
        if kind == 'b':
            in_specs.append(pl.BlockSpec(arr.shape, lambda i, nd=arr.ndim: (0,) * nd))
        elif kind == 't':
            in_specs.append(pl.BlockSpec((tm, spec[2]), lambda i, cb=spec[3]: (pos(i), cb)))
        elif kind == 'p':
            in_specs.append(pl.BlockSpec((SUBLANES, spec[2]),
                                         lambda i, cb=spec[3]: (jnp.maximum(pos(i) * per8 - 1, 0), cb)))
        else:
            in_specs.append(pl.BlockSpec((SUBLANES, spec[2]),
                                         lambda i, cb=spec[3]: (jnp.minimum((pos(i) + 1) * per8, n8 - 1), cb)))

    def body(*refs):
        i = pl.program_id(0)
        outs = refs[n_in:]
        touts, aouts = fn(pos(i), nt, *[r[...] for r in refs[:n_in]])
        for r, v in zip(outs[:n_ot], touts):
            r[...] = v.astype(r.dtype)
        if out_acc:
            @pl.when(i == 0)
            def _():
                for r in outs[n_ot:]:
                    r[...] = jnp.zeros(r.shape, r.dtype)
            for r, v in zip(outs[n_ot:], aouts):
                r[...] += v

    out_specs = [pl.BlockSpec((tm, n), lambda i: (pos(i), 0)) for n, _ in out_tiled]
    out_specs += [pl.BlockSpec((r, n), lambda i: (0, 0)) for r, n in out_acc]
    out_shape = [jax.ShapeDtypeStruct((n_rows, n), dt) for n, dt in out_tiled]
    out_shape += [jax.ShapeDtypeStruct((r, n), F32) for r, n in out_acc]
    return pl.pallas_call(body, grid=(nt,), in_specs=in_specs, out_specs=out_specs, out_shape=out_shape, name=name,
                          compiler_params=_params("arbitrary"))(*args)


def _all_gather(name, block):
    def body(x_ref, out_ref, send_sems, recv_sems, local_sem):
        x, y, c = lax.axis_index("x"), lax.axis_index("y"), lax.axis_index("c")
        me, sibling = (x, y, c), (x, y, 1 - c)
        chips = [(1 - x, y), (x, 1 - y), (1 - x, 1 - y)]

        def rows(px, py, pc):
            return out_ref.at[4 * px + 2 * py + pc]

        def copy(k, blk, to, src=None):
            return pltpu.make_async_remote_copy(
                src_ref=rows(*blk) if src is None else src, dst_ref=rows(*blk),
                send_sem=send_sems.at[k], recv_sem=recv_sems.at[k], device_id=to, device_id_type=MESH)

        mine = pltpu.make_async_copy(x_ref, rows(*me), local_sem)
        mine.start()
        first = [copy(0, me, sibling, src=x_ref)]
        first += [copy(1 + j, me, (*chip, c), src=x_ref) for j, chip in enumerate(chips)]
        for cp in first:
            cp.start()
        passed = [copy(4 + j, (*chip, c), sibling) for j, chip in enumerate(chips)]
        for j, chip in enumerate(chips):
            copy(1 + j, (*chip, c), me).wait_recv()
            passed[j].start()
        copy(0, sibling, me).wait_recv()
        for j, chip in enumerate(chips):
            copy(4 + j, (*chip, 1 - c), me).wait_recv()
        for cp in first + passed:
            cp.wait_send()
        mine.wait()

    return pl.pallas_call(
        body, name=name, out_shape=jax.ShapeDtypeStruct((N_DEV,) + block.shape, block.dtype),
        in_specs=[pl.BlockSpec(memory_space=pl.ANY)], out_specs=pl.BlockSpec(memory_space=pl.ANY),
        scratch_shapes=[pltpu.SemaphoreType.DMA((7,)), pltpu.SemaphoreType.DMA((7,)), pltpu.SemaphoreType.DMA(())],
    )(block)


def _exchange(name, send):
    def body(s_ref, r_ref, send_sems, recv_sems, local_sem):
        x, y, c = lax.axis_index("x"), lax.axis_index("y"), lax.axis_index("c")
        me = 4 * x + 2 * y + c
        mine = pltpu.make_async_copy(s_ref.at[me], r_ref.at[me], local_sem)
        mine.start()
        copies = []
        for k in range(1, N_DEV):
            dx, dy, dc = (k >> 2) & 1, (k >> 1) & 1, k & 1
            px = 1 - x if dx else x
            py = 1 - y if dy else y
            pc = 1 - c if dc else c
            peer = 4 * px + 2 * py + pc
            copies.append((pltpu.make_async_remote_copy(
                src_ref=s_ref.at[peer], dst_ref=r_ref.at[me], send_sem=send_sems.at[k - 1],
                recv_sem=recv_sems.at[k - 1], device_id=(px, py, pc), device_id_type=MESH), peer))
        for cp, _ in copies:
            cp.start()
        for k, (cp, peer) in enumerate(copies):
            pltpu.make_async_remote_copy(
                src_ref=s_ref.at[peer], dst_ref=r_ref.at[peer], send_sem=send_sems.at[k], recv_sem=recv_sems.at[k],
                device_id=(x, y, c), device_id_type=MESH).wait_recv()
        for cp, _ in copies:
            cp.wait_send()
        mine.wait()

    return pl.pallas_call(
        body, name=name, out_shape=jax.ShapeDtypeStruct(send.shape, send.dtype),
        in_specs=[pl.BlockSpec(memory_space=pl.ANY)], out_specs=pl.BlockSpec(memory_space=pl.ANY),
        scratch_shapes=[pltpu.SemaphoreType.DMA((7,)), pltpu.SemaphoreType.DMA((7,)), pltpu.SemaphoreType.DMA(())],
    )(send)


def _direct_copies(kind, s_ref, r_ref, send_sems, recv_sems, local_sem):
    x, y, c = lax.axis_index("x"), lax.axis_index("y"), lax.axis_index("c")
    me = 4 * x + 2 * y + c

    def src(p):
        return s_ref if kind == "gather" else s_ref.at[p]

    local = pltpu.make_async_copy(src(me), r_ref.at[me], local_sem)
    sends, recvs = [], []
    for k in range(1, N_DEV):
        px = 1 - x if (k >> 2) & 1 else x
        py = 1 - y if (k >> 1) & 1 else y
        pc = 1 - c if k & 1 else c
        peer = 4 * px + 2 * py + pc
        sends.append(pltpu.make_async_remote_copy(
            src_ref=src(peer), dst_ref=r_ref.at[me], send_sem=send_sems.at[k - 1], recv_sem=recv_sems.at[k - 1],
            device_id=(px, py, pc), device_id_type=MESH))
        recvs.append(pltpu.make_async_remote_copy(
            src_ref=src(peer), dst_ref=r_ref.at[peer], send_sem=send_sems.at[k - 1], recv_sem=recv_sems.at[k - 1],
            device_id=(x, y, c), device_id_type=MESH))
    return local, sends, recvs


def _call(body, *, grid, in_specs, out_specs, out_shape, name, args, scratch_shapes=(), semantics=None, comms=()):
    single = not isinstance(out_shape, (list, tuple))
    out_shape = [out_shape] if single else list(out_shape)
    out_specs = [out_specs] if single else list(out_specs)
    if not comms:
        res = pl.pallas_call(body, grid=grid, in_specs=in_specs, out_specs=out_specs, out_shape=out_shape, name=name,
                             scratch_shapes=list(scratch_shapes),
                             compiler_params=_params(*(semantics or ("arbitrary",) * len(grid))))(*args)
        return res[0] if single else res
    n_in, n_out, n_scr, n_c = len(args), len(out_shape), len(scratch_shapes), len(comms)

    def hosted(*refs):
        ins, srcs = refs[:n_in], refs[n_in:n_in + n_c]
        outs = refs[n_in + n_c:n_in + n_c + n_out]
        dsts = refs[n_in + n_c + n_out:n_in + 2 * n_c + n_out]
        scr = refs[n_in + 2 * n_c + n_out:n_in + 2 * n_c + n_out + n_scr]
        sems = refs[n_in + 2 * n_c + n_out + n_scr:]
        first = functools.reduce(jnp.logical_and, [pl.program_id(d) == 0 for d in range(len(grid))])
        last = functools.reduce(jnp.logical_and, [pl.program_id(d) == grid[d] - 1 for d in range(len(grid))])
        plans = [_direct_copies(comms[i][0], srcs[i], dsts[i], *sems[3 * i:3 * i + 3]) for i in range(n_c)]

        @pl.when(first)
        def _():
            for local, sends, _ in plans:
                local.start()
                for cp in sends:
                    cp.start()

        body(*ins, *outs, *scr)

        @pl.when(last)
        def _():
            for local, sends, recvs in plans:
                for cp in recvs:
                    cp.wait_recv()
                for cp in sends:
                    cp.wait_send()
                local.wait()

    any_spec = pl.BlockSpec(memory_space=pl.ANY)
    comm_shapes = [jax.ShapeDtypeStruct(((N_DEV,) + a.shape) if kind == "gather" else a.shape, a.dtype)
                   for kind, a in comms]
    sem_shapes = []
    for _ in comms:
        sem_shapes += [pltpu.SemaphoreType.DMA((7,)), pltpu.SemaphoreType.DMA((7,)), pltpu.SemaphoreType.DMA(())]
    res = pl.pallas_call(
        hosted, grid=grid, in_specs=list(in_specs) + [any_spec] * n_c, out_specs=out_specs + [any_spec] * n_c,
        out_shape=out_shape + comm_shapes, name=name, scratch_shapes=list(scratch_shapes) + sem_shapes,
        compiler_params=_params(*(("arbitrary",) * len(grid))))(*args, *[a for _, a in comms])
    return res


def _unshard(gathered, axis):
    local = gathered.shape[1:]
    full = jnp.moveaxis(gathered, 0, axis)
    return full.reshape(local[:axis] + (N_DEV * local[axis],) + local[axis + 1:])


def _shard_blocks(full, axis):
    s = full.shape
    cut = full.reshape(s[:axis] + (N_DEV, s[axis] // N_DEV) + s[axis + 1:])
    return jnp.moveaxis(cut, axis, 0)


def _mm_fwd(name, a, w, out_dtype, gain=None, resid=None, glu=False):
    n_rows, k = a.shape
    n = w.shape[1]
    tm = _tile(n_rows, 512)
    n_out = n // 2 if glu else n

    def body(*refs):
        it = iter(refs)
        a_ref, w_ref = next(it), next(it)
        g_ref = next(it) if gain is not None else None
        r_ref = next(it) if resid is not None else None
        outs = list(it)
        av = a_ref[...]
        if g_ref is not None:
            _, xhat = _rms(av)
            av = xhat * g_ref[...]
        res = _dot(av.astype(BF16), w_ref[...])
        if glu:
            outs[1][...] = res.astype(outs[1].dtype)
            res = res[:, :n_out] * _sigmoid(res[:, n_out:])
        if r_ref is not None:
            res = res + r_ref[...]
        outs[0][...] = res.astype(outs[0].dtype)

    args = [a, w]
    in_specs = [pl.BlockSpec((tm, k), lambda i: (i, 0)), pl.BlockSpec((k, n), lambda i: (0, 0))]
    if gain is not None:
        args.append(gain)
        in_specs.append(pl.BlockSpec((1, k), lambda i: (0, 0)))
    if resid is not None:
        args.append(resid)
        in_specs.append(pl.BlockSpec((tm, n_out), lambda i: (i, 0)))
    out_shape = [jax.ShapeDtypeStruct((n_rows, n_out), out_dtype)]
    out_specs = [pl.BlockSpec((tm, n_out), lambda i: (i, 0))]
    if glu:
        out_shape.append(jax.ShapeDtypeStruct((n_rows, n), F32))
        out_specs.append(pl.BlockSpec((tm, n), lambda i: (i, 0)))
    return pl.pallas_call(body, grid=(n_rows // tm,), in_specs=in_specs, out_specs=out_specs, out_shape=out_shape,
                          name=name, compiler_params=_params("parallel"))(*args)


def _mm_bwd(name, a, d, w, gain=None, dres=None):
    n_rows, k = a.shape
    n = w.shape[1]
    tm = _tile(n_rows, 512)

    def body(*refs):
        it = iter(refs)
        a_ref, d_ref, w_ref = next(it), next(it), next(it)
        g_ref = next(it) if gain is not None else None
        r_ref = next(it) if gain is not None else None
        da_ref, dw_ref = next(it), next(it)
        dg_ref = next(it) if gain is not None else None
        i = pl.program_id(0)

        @pl.when(i == 0)
        def _():
            dw_ref[...] = jnp.zeros(dw_ref.shape, F32)
            if dg_ref is not None:
                dg_ref[...] = jnp.zeros(dg_ref.shape, F32)

        av = a_ref[...]
        dv = d_ref[...].astype(BF16)
        if g_ref is not None:
            r, xhat = _rms(av)
            ab = (xhat * g_ref[...]).astype(BF16)
        else:
            ab = av.astype(BF16)
        dw_ref[...] += _dot(ab, dv, TN)
        da = _dot(dv, w_ref[...], NT)
        if g_ref is not None:
            dg_ref[...] += jnp.sum(da * xhat, axis=0, keepdims=True)
            da = r_ref[...] + _rms_bwd(da, xhat, r, g_ref[...])
        da_ref[...] = da.astype(da_ref.dtype)

    args = [a, d, w]
    in_specs = [pl.BlockSpec((tm, k), lambda i: (i, 0)), pl.BlockSpec((tm, n), lambda i: (i, 0)),
                pl.BlockSpec((k, n), lambda i: (0, 0))]
    out_shape = [jax.ShapeDtypeStruct((n_rows, k), F32), jax.ShapeDtypeStruct((k, n), F32)]
    out_specs = [pl.BlockSpec((tm, k), lambda i: (i, 0)), pl.BlockSpec((k, n), lambda i: (0, 0))]
    if gain is not None:
        args += [gain, dres]
        in_specs += [pl.BlockSpec((1, k), lambda i: (0, 0)), pl.BlockSpec((tm, k), lambda i: (i, 0))]
        out_shape.append(jax.ShapeDtypeStruct((1, k), F32))
        out_specs.append(pl.BlockSpec((1, k), lambda i: (0, 0)))
    return pl.pallas_call(body, grid=(n_rows // tm,), in_specs=in_specs, out_specs=out_specs, out_shape=out_shape,
                          name=name, compiler_params=_params("arbitrary"))(*args)


def _ffn_fwd(name, x, gain, wi, wo, comms=()):
    n_rows, dm = x.shape
    _, nj, _, fb = wi.shape
    tm = _tile(n_rows, 512)

    def body(x_ref, g_ref, wi_ref, wo_ref, y_ref, gate_ref, up_ref):
        xv = x_ref[...]
        _, xhat = _rms(xv)
        hn = (xhat * g_ref[...]).astype(BF16)
        acc = jnp.zeros((tm, dm), F32)
        for j in range(nj):
            gate = _dot(hn, wi_ref[0, j])
            up = _dot(hn, wi_ref[1, j])
            gate_ref[j] = gate.astype(BF16)
            up_ref[j] = up.astype(BF16)
            act = (gate * _sigmoid(gate) * up).astype(BF16)
            acc = acc + _dot(act, wo_ref[j].reshape(fb, dm))
        y_ref[...] = xv + 0.5 * acc

    return _call(
        body, grid=(n_rows // tm,), name=name, args=[x, gain, wi, wo], comms=comms,
        in_specs=[pl.BlockSpec((tm, dm), lambda i: (i, 0)), pl.BlockSpec((1, dm), lambda i: (0, 0)),
                  pl.BlockSpec((2, nj, dm, fb), lambda i: (0, 0, 0, 0)),
                  pl.BlockSpec((nj, 2, fb // 2, dm), lambda i: (0, 0, 0, 0))],
        out_specs=[pl.BlockSpec((tm, dm), lambda i: (i, 0)), pl.BlockSpec((nj, tm, fb), lambda i: (0, i, 0)),
                   pl.BlockSpec((nj, tm, fb), lambda i: (0, i, 0))],
        out_shape=[jax.ShapeDtypeStruct((n_rows, dm), F32), jax.ShapeDtypeStruct((nj, n_rows, fb), BF16),
                   jax.ShapeDtypeStruct((nj, n_rows, fb), BF16)])


def _ffn_bwd_block(name, x, dy, gain, wi, wo, gate_s, up_s, j, acc, comms=()):
    n_rows, dm = x.shape
    _, nj, _, fb = wi.shape
    tm = _tile(n_rows, 512)
    last = j == nj - 1

    def body(*refs):
        it = iter(refs)
        x_ref, dy_ref, g_ref, wi_ref, wo_ref = next(it), next(it), next(it), next(it), next(it)
        gate_ref, up_ref = next(it), next(it)
        acc_ref = next(it) if acc is not None else None
        out_ref, dwi_ref, dwo_ref = next(it), next(it), next(it)
        dg_ref = next(it) if last else None
        i = pl.program_id(0)

        @pl.when(i == 0)
        def _():
            dwi_ref[...] = jnp.zeros(dwi_ref.shape, F32)
            dwo_ref[...] = jnp.zeros(dwo_ref.shape, F32)
            if last:
                dg_ref[...] = jnp.zeros(dg_ref.shape, F32)

        xv, dyv, g = x_ref[...], dy_ref[...], g_ref[...]
        r, xhat = _rms(xv)
        hn = (xhat * g).astype(BF16)
        wg, wu, wob = wi_ref[0], wi_ref[1], wo_ref[...].reshape(fb, dm)
        gate = gate_ref[...].astype(F32)
        up = up_ref[...].astype(F32)
        s = _sigmoid(gate)
        silu = gate * s
        act = (silu * up).astype(BF16)
        dout = (0.5 * dyv).astype(BF16)
        dact = _dot(dout, wob, NT)
        dwo_ref[...] += _dot(act, dout, TN)
        dgate = (dact * up * (s * (1.0 + gate * (1.0 - s)))).astype(BF16)
        dup = (dact * silu).astype(BF16)
        dwi_ref[0] += _dot(hn, dgate, TN)
        dwi_ref[1] += _dot(hn, dup, TN)
        tot = _dot(dgate, wg, NT) + _dot(dup, wu, NT)
        if acc_ref is not None:
            tot = tot + acc_ref[...]
        if last:
            out_ref[...] = dyv + _rms_bwd(tot, xhat, r, g)
            dg_ref[...] += jnp.sum(tot * xhat, axis=0, keepdims=True)
        else:
            out_ref[...] = tot

    tok = pl.BlockSpec((tm, dm), lambda i: (i, 0))
    args = [x, dy, gain, wi, wo, gate_s, up_s]
    saved = pl.BlockSpec((None, tm, fb), lambda i: (j, i, 0))
    in_specs = [tok, tok, pl.BlockSpec((1, dm), lambda i: (0, 0)),
                pl.BlockSpec((2, None, dm, fb), lambda i: (0, j, 0, 0)),
                pl.BlockSpec((None, 2, fb // 2, dm), lambda i: (j, 0, 0, 0)), saved, saved]
    if acc is not None:
        args.append(acc)
        in_specs.append(tok)
    out_specs = [tok, pl.BlockSpec((2, dm, fb), lambda i: (0, 0, 0)), pl.BlockSpec((fb, dm), lambda i: (0, 0))]
    out_shape = [jax.ShapeDtypeStruct((n_rows, dm), F32), jax.ShapeDtypeStruct((2, dm, fb), F32),
                 jax.ShapeDtypeStruct((fb, dm), F32)]
    if last:
        out_specs.append(pl.BlockSpec((1, dm), lambda i: (0, 0)))
        out_shape.append(jax.ShapeDtypeStruct((1, dm), F32))
    return _call(body, grid=(n_rows // tm,), name=name, in_specs=in_specs, out_specs=out_specs,
                 out_shape=out_shape, args=args, comms=comms)


def _ffn_bwd(name, x, dy, gain, wi, wo, gate_s, up_s, comms_by_block=None):
    nj = wi.shape[1]
    acc, dwi, dwo, extra = None, [], [], []
    for j in range(nj):
        comms = (comms_by_block or {}).get(j, ())
        res = _ffn_bwd_block("%s_%d" % (name, j), x, dy, gain, wi, wo, gate_s, up_s, j, acc, comms)
        n_own = 4 if j == nj - 1 else 3
        acc = res[0]
        dwi.append(res[1])
        dwo.append(res[2])
        extra += list(res[n_own:])
        dgain = res[3] if j == nj - 1 else None
    return acc, jnp.stack(dwi, axis=1), jnp.stack(dwo, axis=0), dgain, extra


def _scan8(a, x, reverse):
    row = lax.broadcasted_iota(jnp.int32, a.shape, 0)
    for k in (1, 2, 4):
        if reverse:
            keep = row < SUBLANES - k
            a_s, x_s = pltpu.roll(a, SUBLANES - k, 0), pltpu.roll(x, SUBLANES - k, 0)
        else:
            keep = row >= k
            a_s, x_s = pltpu.roll(a, k, 0), pltpu.roll(x, k, 0)
        x = a * jnp.where(keep, x_s, 0.0) + x
        a = a * jnp.where(keep, a_s, 1.0)
    return a, x


def _scan_tile(a_ref, x_ref, h_ref, carry, reverse, rows):
    groups = rows // SUBLANES

    def step(n, c):
        gidx = groups - 1 - n if reverse else n
        sl = pl.ds(pl.multiple_of(gidx * SUBLANES, SUBLANES), SUBLANES)
        a_cum, h0 = _scan8(a_ref[sl, :], x_ref[sl, :], reverse)
        h = a_cum * c + h0
        h_ref[sl, :] = h
        return h[0:1] if reverse else h[SUBLANES - 1:SUBLANES]

    return lax.fori_loop(0, groups, step, carry)


def _s5_mats(lam_re, lam_im, log_dt, b_re, b_im, c_re, c_im):
    lc = S5_CHUNK
    groups, p = lam_re.shape
    h = b_re.shape[-1]
    lam = lax.complex(lam_re, lam_im)
    lam_dt = lam * jnp.exp(log_dt)[:, None]
    lam_bar = jnp.exp(lam_dt)
    b_bar = ((lam_bar - 1.0) / lam)[:, :, None] * lax.complex(b_re, b_im)
    c = lax.complex(c_re, c_im)
    pw = jnp.exp(lam_dt[None] * jnp.arange(lc + 1, dtype=F32)[:, None, None])
    resp = jnp.einsum('ghp,tgp,gpk->tghk', c, pw[:lc], b_bar, precision=HI).real
    s_idx = jnp.arange(lc)[:, None]
    u_idx = jnp.arange(lc)[None, :]
    onehot = (jnp.arange(lc)[:, None, None] == (u_idx - s_idx)[None]).astype(F32)
    m = jnp.einsum('tghk,tsu->gskuh', resp, onehot, precision=HI).reshape(groups, lc * h, lc * h)
    w = pw[lc - 1::-1][:lc].transpose(1, 0, 2)[:, :, None, :] * b_bar.transpose(0, 2, 1)[:, None]
    bm = jnp.concatenate([w.real, w.imag], axis=-1).reshape(groups, lc * h, 2 * p)
    v = c[:, None] * pw[1:lc + 1].transpose(1, 0, 2)[:, :, None, :]
    v = v.transpose(0, 3, 1, 2)
    cm = jnp.concatenate([v.real, -v.imag], axis=1).reshape(groups, 2 * p, lc * h)
    a = jnp.concatenate([pw[lc].real, pw[lc].imag], axis=-1)
    return m, bm, cm, a


def _s5_powers(lam_re, lam_im, log_dt):
    lam_dt = lax.complex(lam_re, lam_im) * jnp.exp(log_dt)[:, None]
    pw = jnp.exp(lam_dt[None] * (S5_CHUNK * jnp.arange(1, 9, dtype=F32))[:, None, None])

    def c1(z):
        return jnp.concatenate([z.real, z.real], axis=-1).reshape(z.shape[0], -1)

    def c2(z):
        return jnp.concatenate([-z.imag, z.imag], axis=-1).reshape(z.shape[0], -1)

    p1, p2 = c1(pw), c2(pw)
    apw = jnp.stack([p1[0], p2[0], p1[1], p2[1], p1[3], p2[3], jnp.zeros_like(p1[0]), jnp.zeros_like(p1[0])])
    fwd = jnp.concatenate([p1, p2], axis=0)
    rev = jnp.concatenate([c1(pw[::-1]), c2(pw[::-1])], axis=0)
    return apw, fwd, rev


def _cmul(c1, c2, x, half, conj=False):
    sw = pltpu.roll(x, half, 1)
    return c1 * x - c2 * sw if conj else c1 * x + c2 * sw


def _gather_groups(u_ref, ug_ref, nc):
    h = S5_GROUP
    rows = min(S5_REGROUP_ROWS, nc)

    def step(r, _):
        base = pl.multiple_of(r * rows, rows)
        for t in range(S5_CHUNK):
            val = u_ref[pl.ds(base * S5_CHUNK + t, rows, stride=S5_CHUNK), :]
            for g in range(S5_OCTET):
                ug_ref[g, pl.ds(base, rows), t * h:(t + 1) * h] = val[:, g * h:(g + 1) * h]
        return 0

    lax.fori_loop(0, nc // rows, step, 0)


def _scatter_groups(yg_ref, y_ref, nc):
    h = S5_GROUP
    rows = min(S5_REGROUP_ROWS, nc)

    def step(r, _):
        base = pl.multiple_of(r * rows, rows)
        for t in range(S5_CHUNK):
            y_ref[pl.ds(base * S5_CHUNK + t, rows, stride=S5_CHUNK), :] = jnp.concatenate(
                [yg_ref[g, pl.ds(base, rows), t * h:(t + 1) * h] for g in range(S5_OCTET)], axis=1)
        return 0

    lax.fori_loop(0, nc // rows, step, 0)


def _s5_fwd(name, u, m, bm, cm, apw, arows):
    n_rows, width = u.shape
    nc = n_rows // S5_CHUNK
    groups, lh, _ = m.shape
    p2 = bm.shape[2]
    gb = S5_OCTET
    lanes = gb * S5_GROUP

    def body(u_ref, m_ref, b_ref, c_ref, apw_ref, ar_ref, y_ref, sp_ref, ug_ref, yg_ref, xs_ref):
        _gather_groups(u_ref, ug_ref, nc)
        for gi in range(gb):
            xs_ref[:, gi * p2:(gi + 1) * p2] = _dot(ug_ref[gi], b_ref[gi], prec=S5_PREC)
        row = lax.broadcasted_iota(jnp.int32, (SUBLANES, p2), 0)

        def group(n, carry):
            sl = pl.ds(pl.multiple_of(n * SUBLANES, SUBLANES), SUBLANES)
            new = []
            for gi in range(gb):
                ln = slice(gi * p2, (gi + 1) * p2)
                x = xs_ref[sl, ln]
                for q, k in enumerate((1, 2, 4)):
                    xs = jnp.where(row >= k, pltpu.roll(x, k, 0), 0.0)
                    x = x + _cmul(apw_ref[2 * q:2 * q + 1, ln], apw_ref[2 * q + 1:2 * q + 2, ln], xs, p2 // 2)
                cb = jnp.broadcast_to(carry[gi], (SUBLANES, p2))
                s8 = x + _cmul(ar_ref[0:8, ln], ar_ref[8:16, ln], cb, p2 // 2)
                sp_ref[sl, ln] = jnp.where(row >= 1, pltpu.roll(s8, 1, 0), cb)
                new.append(s8[SUBLANES - 1:SUBLANES])
            return tuple(new)

        unroll = min(S5_SCAN_UNROLL, nc // SUBLANES)

        def step(n, carry):
            for k in range(unroll):
                carry = group(n * unroll + k, carry)
            return carry

        lax.fori_loop(0, nc // (SUBLANES * unroll), step, tuple(jnp.zeros((1, p2), F32) for _ in range(gb)))
        for gi in range(gb):
            yg_ref[gi] = (_dot(ug_ref[gi], m_ref[gi], prec=S5_PREC)
                          + _dot(sp_ref[:, gi * p2:(gi + 1) * p2], c_ref[gi], prec=S5_PREC))
        _scatter_groups(yg_ref, y_ref, nc)

    tok = pl.BlockSpec((n_rows, lanes), lambda g: (0, g), pipeline_mode=pl.Buffered(1))
    return pl.pallas_call(
        body, grid=(groups // gb,), name=name,
        in_specs=[tok, pl.BlockSpec((gb, lh, lh), lambda g: (g, 0, 0)),
                  pl.BlockSpec((gb, lh, p2), lambda g: (g, 0, 0)), pl.BlockSpec((gb, p2, lh), lambda g: (g, 0, 0)),
                  pl.BlockSpec((8, gb * p2), lambda g: (0, g)), pl.BlockSpec((16, gb * p2), lambda g: (0, g))],
        out_specs=[tok, pl.BlockSpec((nc, gb * p2), lambda g: (0, g))],
        out_shape=[jax.ShapeDtypeStruct((n_rows, width), F32), jax.ShapeDtypeStruct((nc, groups * p2), F32)],
        scratch_shapes=[pltpu.VMEM((gb, nc, lh), F32), pltpu.VMEM((gb, nc, lh), F32), pltpu.VMEM((nc, gb * p2), F32)],
        compiler_params=_params("parallel"),
    )(u, m, bm, cm, apw, arows)


def _s5_bwd(name, u, dy, sprev, m, bm, cm, apw, arows_rev):
    n_rows, width = u.shape
    nc = n_rows // S5_CHUNK
    groups, lh, _ = m.shape
    p2 = bm.shape[2]
    half = p2 // 2
    gb = S5_OCTET
    lanes = gb * S5_GROUP

    def body(u_ref, dy_ref, sp_ref, m_ref, b_ref, c_ref, apw_ref, ar_ref,
             du_ref, dm_ref, db_ref, dc_ref, da_ref, ug_ref, dyg_ref, ds_ref, gx_ref):
        _gather_groups(u_ref, ug_ref, nc)
        _gather_groups(dy_ref, dyg_ref, nc)
        for gi in range(gb):
            ds_ref[:, gi * p2:(gi + 1) * p2] = _dot(dyg_ref[gi], c_ref[gi], NT, prec=S5_PREC)
        row = lax.broadcasted_iota(jnp.int32, (SUBLANES, p2), 0)
        lane = lax.broadcasted_iota(jnp.int32, (SUBLANES, p2), 1)
        ngroups = nc // SUBLANES

        def group(n, state):
            carry, nxt, dacc = state
            sl = pl.ds(pl.multiple_of((ngroups - 1 - n) * SUBLANES, SUBLANES), SUBLANES)
            new_c, new_n, new_d = [], [], []
            for gi in range(gb):
                ln = slice(gi * p2, (gi + 1) * p2)
                d8 = ds_ref[sl, ln]
                x = jnp.where(row < SUBLANES - 1, pltpu.roll(d8, SUBLANES - 1, 0),
                              jnp.broadcast_to(nxt[gi], (SUBLANES, p2)))
                for q, k in enumerate((1, 2, 4)):
                    xs = jnp.where(row < SUBLANES - k, pltpu.roll(x, SUBLANES - k, 0), 0.0)
                    x = x + _cmul(apw_ref[2 * q:2 * q + 1, ln], apw_ref[2 * q + 1:2 * q + 2, ln], xs, half, conj=True)
                cb = jnp.broadcast_to(carry[gi], (SUBLANES, p2))
                g8 = x + _cmul(ar_ref[0:8, ln], ar_ref[8:16, ln], cb, half, conj=True)
                gx_ref[sl, ln] = g8
                s8 = sp_ref[sl, ln]
                p1 = g8 * s8
                pq = g8 * pltpu.roll(s8, half, 1)
                d_a = jnp.where(lane < half, p1 + pltpu.roll(p1, half, 1), pq - pltpu.roll(pq, half, 1))
                new_c.append(g8[0:1])
                new_n.append(d8[0:1])
                new_d.append(dacc[gi] + jnp.sum(d_a, axis=0, keepdims=True))
            return tuple(new_c), tuple(new_n), tuple(new_d)

        unroll = min(S5_SCAN_UNROLL, ngroups)

        def step(n, state):
            for k in range(unroll):
                state = group(n * unroll + k, state)
            return state

        zeros = tuple(jnp.zeros((1, p2), F32) for _ in range(gb))
        _, _, dacc = lax.fori_loop(0, ngroups // unroll, step, (zeros, zeros, zeros))
        for gi in range(gb):
            ln = slice(gi * p2, (gi + 1) * p2)
            da_ref[:, ln] = dacc[gi]
            ug, dyg, gxg = ug_ref[gi], dyg_ref[gi], gx_ref[:, ln]
            dm_ref[gi] = _dot(ug, dyg, TN, prec=S5_PREC)
            dc_ref[gi] = _dot(sp_ref[:, ln], dyg, TN, prec=S5_PREC)
            db_ref[gi] = _dot(ug, gxg, TN, prec=S5_PREC)
            dyg_ref[gi] = _dot(dyg, m_ref[gi], NT, prec=S5_PREC) + _dot(gxg, b_ref[gi], NT, prec=S5_PREC)
        _scatter_groups(dyg_ref, du_ref, nc)

    tok = pl.BlockSpec((n_rows, lanes), lambda g: (0, g), pipeline_mode=pl.Buffered(1))
    tok_s = pl.BlockSpec((nc, gb * p2), lambda g: (0, g))
    mat_m = pl.BlockSpec((gb, lh, lh), lambda g: (g, 0, 0))
    mat_b = pl.BlockSpec((gb, lh, p2), lambda g: (g, 0, 0))
    mat_c = pl.BlockSpec((gb, p2, lh), lambda g: (g, 0, 0))
    return pl.pallas_call(
        body, grid=(groups // gb,), name=name,
        in_specs=[tok, tok, tok_s, mat_m, mat_b, mat_c,
                  pl.BlockSpec((8, gb * p2), lambda g: (0, g)), pl.BlockSpec((16, gb * p2), lambda g: (0, g))],
        out_specs=[tok, mat_m, mat_b, mat_c, pl.BlockSpec((1, gb * p2), lambda g: (0, g))],
        out_shape=[jax.ShapeDtypeStruct((n_rows, width), F32), jax.ShapeDtypeStruct(m.shape, F32),
                   jax.ShapeDtypeStruct(bm.shape, F32), jax.ShapeDtypeStruct(cm.shape, F32),
                   jax.ShapeDtypeStruct((1, groups * p2), F32)],
        scratch_shapes=[pltpu.VMEM((gb, nc, lh), F32), pltpu.VMEM((gb, nc, lh), F32),
                        pltpu.VMEM((nc, gb * p2), F32), pltpu.VMEM((nc, gb * p2), F32)],
        compiler_params=_params("parallel"),
    )(u, dy, sprev, m, bm, cm, apw, arows_rev)


def _split(x):
    hi = x.astype(BF16)
    return hi, (x - hi.astype(F32)).astype(BF16)


def _sb_more(kb, carries):
    top = jnp.max(carries[0])
    for c in carries[1:]:
        top = jnp.maximum(top, jnp.max(c))
    return (kb >= 0) & (top > SB_UNDERFLOW)


def _sb_fwd(name, q, k, v):
    heads, n_rows, hd = q.shape
    tq = _tile(n_rows // 2, 256)
    hb = min(SB_HEADS_FWD, heads)
    scale = hd ** -0.5

    def body(q_ref, k_ref, v_ref, o_ref):
        qi = pl.program_id(1)
        hs = range(hb)
        row = lax.broadcasted_iota(jnp.int32, (tq, tq), 0)
        col = lax.broadcasted_iota(jnp.int32, (tq, tq), 1)
        tri = (row > col).astype(BF16)
        causal = col < row

        def block(kb, carries, accs, diagonal):
            ks = pl.ds(pl.multiple_of(kb * tq, tq), tq)
            z = [_dot(q_ref[h] * scale, k_ref[h, ks, :], NT) for h in hs]
            sp = [_softplus(z[h]) for h in hs]
            lk = [-sp[h] for h in hs]
            if diagonal:
                lk = [jnp.where(causal, lk[h], 0.0) for h in hs]
            parts = [_split(lk[h]) for h in hs]
            r = [_dot(parts[h][0], tri) + _dot(parts[h][1], tri) for h in hs]
            a = [jnp.exp(z[h] - sp[h] + r[h] + carries[h]) for h in hs]
            if diagonal:
                a = [jnp.where(causal, a[h], 0.0) for h in hs]
            new_a = tuple(accs[h] + _dot(a[h].astype(BF16), v_ref[h, ks, :]) for h in hs)
            new_c = tuple(carries[h] + jnp.sum(lk[h], axis=1, keepdims=True) for h in hs)
            return new_c, new_a

        zc = tuple(jnp.zeros((tq, 1), F32) for _ in hs)
        za = tuple(jnp.zeros((tq, hd), F32) for _ in hs)
        carries, accs = block(qi, zc, za, True)
        _, _, accs = lax.while_loop(lambda st: _sb_more(st[0], st[1]),
                                    lambda st: (st[0] - 1,) + block(st[0], st[1], st[2], False),
                                    (qi - 1, carries, accs))
        for h in hs:
            o_ref[h] = accs[h]

    kv = pl.BlockSpec((hb, n_rows, hd), lambda h, i: (h, 0, 0))
    qs = pl.BlockSpec((hb, tq, hd), lambda h, i: (h, i, 0))
    return pl.pallas_call(body, grid=(heads // hb, n_rows // tq), in_specs=[qs, kv, kv], out_specs=qs, name=name,
                          out_shape=jax.ShapeDtypeStruct((heads, n_rows, hd), F32),
                          compiler_params=_params("parallel", "arbitrary"))(q, k, v)


def _sb_bwd(name, q, k, v, o, do):
    heads, n_rows, hd = q.shape
    tq = _tile(n_rows // 2, 256)
    hb = min(SB_HEADS_BWD, heads)
    scale = hd ** -0.5

    def body(q_ref, k_ref, v_ref, o_ref, do_ref, dq_ref, dk_ref, dv_ref):
        qi = pl.program_id(1)

        @pl.when(qi == 0)
        def _():
            dk_ref[...] = jnp.zeros(dk_ref.shape, F32)
            dv_ref[...] = jnp.zeros(dv_ref.shape, F32)

        hs = range(hb)
        qb = [q_ref[h] * scale for h in hs]
        dob16 = [do_ref[h].astype(BF16) for h in hs]
        delta = [jnp.sum(dob16[h].astype(F32) * o_ref[h], axis=1, keepdims=True) for h in hs]
        row = lax.broadcasted_iota(jnp.int32, (tq, tq), 0)
        col = lax.broadcasted_iota(jnp.int32, (tq, tq), 1)
        tri = (row > col).astype(BF16)
        tri_incl = (row >= col).astype(BF16)
        causal = col < row

        def block(kb, carries, pcarries, dqs, diagonal):
            ks = pl.ds(pl.multiple_of(kb * tq, tq), tq)
            kblk = [k_ref[h, ks, :] for h in hs]
            vblk = [v_ref[h, ks, :] for h in hs]
            z = [_dot(qb[h], kblk[h], NT) for h in hs]
            da = [_dot(dob16[h], vblk[h], NT) for h in hs]
            sp = [_softplus(z[h]) for h in hs]
            lk = [-sp[h] for h in hs]
            if diagonal:
                lk = [jnp.where(causal, lk[h], 0.0) for h in hs]
            lb = [z[h] - sp[h] for h in hs]
            parts = [_split(lk[h]) for h in hs]
            r = [_dot(parts[h][0], tri) + _dot(parts[h][1], tri) for h in hs]
            a = [jnp.exp(lb[h] + r[h] + carries[h]) for h in hs]
            if diagonal:
                a = [jnp.where(causal, a[h], 0.0) for h in hs]
            a16 = [a[h].astype(BF16) for h in hs]
            p = [da[h] * a16[h].astype(F32) for h in hs]
            pparts = [_split(p[h]) for h in hs]
            pc = [_dot(pparts[h][0], tri_incl) + _dot(pparts[h][1], tri_incl) for h in hs]
            beta = [jnp.exp(lb[h]) for h in hs]
            dz = [p[h] * (1.0 - beta[h]) - beta[h] * (delta[h] - pcarries[h] - pc[h]) for h in hs]
            if diagonal:
                dz = [jnp.where(causal, dz[h], 0.0) for h in hs]
            dz16 = [dz[h].astype(BF16) for h in hs]
            for h in hs:
                dk_ref[h, ks, :] += _dot(dz16[h], qb[h], TN)
                dv_ref[h, ks, :] += _dot(a16[h], dob16[h], TN)
            return (tuple(carries[h] + jnp.sum(lk[h], axis=1, keepdims=True) for h in hs),
                    tuple(pcarries[h] + jnp.sum(p[h], axis=1, keepdims=True) for h in hs),
                    tuple(dqs[h] + _dot(dz16[h], kblk[h]) for h in hs))

        zc = tuple(jnp.zeros((tq, 1), F32) for _ in hs)
        zq = tuple(jnp.zeros((tq, hd), F32) for _ in hs)
        st = block(qi, zc, zc, zq, True)
        st = lax.while_loop(lambda s: _sb_more(s[0], s[1]),
                            lambda s: (s[0] - 1,) + block(s[0], s[1], s[2], s[3], False), (qi - 1,) + st)
        for h in hs:
            dq_ref[h] = st[3][h] * scale

    kv = pl.BlockSpec((hb, n_rows, hd), lambda h, i: (h, 0, 0))
    qs = pl.BlockSpec((hb, tq, hd), lambda h, i: (h, i, 0))
    full = jax.ShapeDtypeStruct((heads, n_rows, hd), F32)
    return pl.pallas_call(body, grid=(heads // hb, n_rows // tq), in_specs=[qs, kv, kv, qs, qs],
                          out_specs=[qs, kv, kv], out_shape=[full, full, full], name=name,
                          compiler_params=_params("parallel", "arbitrary"))(q, k, v, o, do)


def _block_diag(xb, w_ref_val, dims):
    nb = w_ref_val.shape[0]
    bw = xb.shape[1] // nb
    return jnp.concatenate([_dot(xb[:, n * bw:(n + 1) * bw], w_ref_val[n], dims) for n in range(nb)], axis=1)


def _lru_gates_fwd(name, gx, conv_w, conv_b, wa, ba, wx, bx, lam):
    n_rows, w2 = gx.shape
    w = w2 // 2
    tm = _tile(n_rows, 256)

    def fn(i, nt, br, prev, cw, cb, wa_v, ba_v, wx_v, bx_v, lam_v):
        xc = cb + sum(cw[k:k + 1] * _shift_down(br, prev, LRU_CONV - 1 - k, i == 0) for k in range(LRU_CONV))
        xb = xc.astype(BF16)
        r = _sigmoid(_block_diag(xb, wa_v, NN) + ba_v)
        ig = _sigmoid(_block_diag(xb, wx_v, NN) + bx_v)
        log_a = (-LRU_C * r) * _softplus(-lam_v)
        a = jnp.exp(log_a)
        gated = (ig * xc) * _one_minus_a2_sqrt(log_a)
        return (xc, r, ig, a, gated), ()

    return _rowwise(name, fn, [('t', gx, w, 1), ('p', gx, w, 1), ('b', conv_w), ('b', conv_b), ('b', wa), ('b', ba),
                               ('b', wx), ('b', bx), ('b', lam)], [(w, F32)] * 5, [], n_rows, tm)


def _lru_scan_fwd(name, a, gated, gx):
    n_rows, w = a.shape
    tm = _tile(n_rows, 256)

    def body(a_ref, x_ref, bg_ref, hs_ref, y_ref, carry_ref):
        @pl.when(pl.program_id(0) == 0)
        def _():
            carry_ref[...] = jnp.zeros(carry_ref.shape, F32)
        carry_ref[...] = _scan_tile(a_ref, x_ref, hs_ref, carry_ref[...], False, tm)
        y_ref[...] = (_gelu(bg_ref[...]) * hs_ref[...]).astype(BF16)

    tok = pl.BlockSpec((tm, w), lambda i: (i, 0))
    return pl.pallas_call(body, grid=(n_rows // tm,), in_specs=[tok, tok, tok], out_specs=[tok, tok], name=name,
                          out_shape=[jax.ShapeDtypeStruct((n_rows, w), F32), jax.ShapeDtypeStruct((n_rows, w), BF16)],
                          scratch_shapes=[pltpu.VMEM((1, w), F32)], compiler_params=_params("arbitrary"))(a, gated, gx)


def _lru_scan_bwd(name, a, dy, gx):
    n_rows, w = a.shape
    tm = _tile(n_rows, 256)
    nt = n_rows // tm
    per8 = tm // SUBLANES

    def body(a_ref, an_ref, dy_ref, bg_ref, lam_ref, carry_ref, aup_ref, dhs_ref):
        i = pl.program_id(0)

        @pl.when(i == 0)
        def _():
            carry_ref[...] = jnp.zeros(carry_ref.shape, F32)
        aup_ref[...] = _shift_up(a_ref[...], an_ref[...], 1, i == 0)
        dhs_ref[...] = dy_ref[...] * _gelu(bg_ref[...])
        carry_ref[...] = _scan_tile(aup_ref, dhs_ref, lam_ref, carry_ref[...], True, tm)

    tok = pl.BlockSpec((tm, w), lambda i: (nt - 1 - i, 0))
    nxt = pl.BlockSpec((SUBLANES, w), lambda i: (jnp.minimum((nt - i) * per8, n_rows // SUBLANES - 1), 0))
    return pl.pallas_call(body, grid=(nt,), in_specs=[tok, nxt, tok, tok], out_specs=tok, name=name,
                          out_shape=jax.ShapeDtypeStruct((n_rows, w), F32),
                          scratch_shapes=[pltpu.VMEM((1, w), F32), pltpu.VMEM((tm, w), F32), pltpu.VMEM((tm, w), F32)],
                          compiler_params=_params("arbitrary"))(a, a, dy, gx)


def _lru_gates_bwd(name, lam_t, hs, xc, r, ig, a, wa, wx, lam):
    n_rows, w = xc.shape
    nb, bw, _ = wa.shape
    tm = _tile(n_rows, 256)

    def fn(i, nt, lt, hs_v, hs_prev, xc_v, r_v, ig_v, a_v, wa_v, wx_v, lam_v):
        sp = _softplus(-lam_v)
        log_a = (-LRU_C * r_v) * sp
        mult = _one_minus_a2_sqrt(log_a)
        d_a = lt * _shift_down(hs_v, hs_prev, 1, i == 0)
        d_ig = lt * xc_v * mult
        d_mult = lt * ig_v * xc_v
        d_log_a = d_a * a_v - d_mult * (a_v * a_v) / mult
        d_ra = d_log_a * (-LRU_C * sp) * r_v * (1.0 - r_v)
        d_ia = d_ig * ig_v * (1.0 - ig_v)
        d_ra16, d_ia16, xb = d_ra.astype(BF16), d_ia.astype(BF16), xc_v.astype(BF16)
        dxc = lt * ig_v * mult + _block_diag(d_ra16, wa_v, NT) + _block_diag(d_ia16, wx_v, NT)
        dwa = jnp.concatenate([_dot(xb[:, n * bw:(n + 1) * bw], d_ra16[:, n * bw:(n + 1) * bw], TN)
                               for n in range(nb)], axis=0)
        dwx = jnp.concatenate([_dot(xb[:, n * bw:(n + 1) * bw], d_ia16[:, n * bw:(n + 1) * bw], TN)
                               for n in range(nb)], axis=0)
        col = lambda t: jnp.sum(t, axis=0, keepdims=True)
        return (dxc,), (dwa, dwx, col(d_ra), col(d_ia), col(d_log_a * (-LRU_C * r_v)))

    tiled = lambda arr: ('t', arr, w, 0)
    return _rowwise(name, fn, [tiled(lam_t), tiled(hs), ('p', hs, w, 0), tiled(xc), tiled(r), tiled(ig), tiled(a),
                               ('b', wa), ('b', wx), ('b', lam)],
                    [(w, F32)], [(nb * bw, bw), (nb * bw, bw), (1, w), (1, w), (1, w)], n_rows, tm)


def _lru_conv_bwd(name, dxc, gx, dy, hs, conv_w):
    n_rows, w = dxc.shape
    tm = _tile(n_rows, 256)

    def fn(i, nt, dxc_v, dxc_next, bg, br, br_prev, dy_v, hs_v, cw):
        dbr = sum(cw[k:k + 1] * _shift_up(dxc_v, dxc_next, LRU_CONV - 1 - k, i == nt - 1) for k in range(LRU_CONV))
        dbg = dy_v * hs_v * _gelu_grad(bg)
        dcw = [jnp.sum(dxc_v * _shift_down(br, br_prev, LRU_CONV - 1 - k, i == 0), axis=0, keepdims=True)
               for k in range(LRU_CONV)]
        dcw = jnp.concatenate(dcw + [jnp.zeros((SUBLANES - LRU_CONV, w), F32)], axis=0)
        return (jnp.concatenate([dbg, dbr], axis=1),), (dcw, jnp.sum(dxc_v, axis=0, keepdims=True))

    return _rowwise(name, fn, [('t', dxc, w, 0), ('n', dxc, w, 0), ('t', gx, w, 0), ('t', gx, w, 1), ('p', gx, w, 1),
                               ('t', dy, w, 0), ('t', hs, w, 0), ('b', conv_w)],
                    [(2 * w, BF16)], [(SUBLANES, w), (1, w)], n_rows, tm)


def _loss_head(name, h, gain, target):
    n_rows, dm = h.shape
    tm = _tile(n_rows, 512)

    def fn(i, nt, hv, tv, g):
        r, xhat = _rms(hv)
        err = xhat * g - tv
        dy = err * (1.0 / dm)
        return ((_rms_bwd(dy, xhat, r, g),),
                (jnp.sum(err * err, axis=0, keepdims=True), jnp.sum(dy * xhat, axis=0, keepdims=True)))

    return _rowwise(name, fn, [('t', h, dm, 0), ('t', target, dm, 0), ('b', gain)], [(dm, F32)], [(1, dm), (1, dm)],
                    n_rows, tm)


def _adamw(name, gparts, w, m, v):
    n_parts, n_rows, cols = gparts.shape
    tr = n_rows
    for cand in (256, 128, 64, 32, 16, 8):
        if n_rows % cand == 0:
            tr = cand
            break
    c1 = 1.0 - ADAM_B1 ** ADAM_STEP
    c2 = 1.0 - ADAM_B2 ** ADAM_STEP

    def body(gp_ref, w_ref, m_ref, v_ref, g_ref, d_ref, nm_ref, nv_ref):
        g = gp_ref[0].astype(F32)
        for p in range(1, n_parts):
            g = g + gp_ref[p].astype(F32)
        m_new = ADAM_B1 * m_ref[...] + (1.0 - ADAM_B1) * g
        v_new = ADAM_B2 * v_ref[...] + (1.0 - ADAM_B2) * (g * g)
        m_hat = m_new / c1
        v_hat = v_new / c2
        g_ref[...] = g
        d_ref[...] = -ADAM_LR * (m_hat / (jnp.sqrt(v_hat) + ADAM_EPS) + ADAM_WD * w_ref[...])
        nm_ref[...] = m_new
        nv_ref[...] = v_new

    blk = pl.BlockSpec((tr, cols), lambda i: (i, 0))
    shp = jax.ShapeDtypeStruct((n_rows, cols), F32)
    return pl.pallas_call(body, grid=(n_rows // tr,), name=name,
                          in_specs=[pl.BlockSpec((n_parts, tr, cols), lambda i: (0, i, 0)), blk, blk, blk],
                          out_specs=[blk, blk, blk, blk], out_shape=[shp, shp, shp, shp],
                          compiler_params=_params("parallel"))(gparts, w, m, v)


def _adamw_layers(name, recvs, w, m, v):
    n_layers, n_rows, cols = w.shape
    n_parts = recvs[0].shape[0]
    tr = max(t for t in range(16, ADAMW_LAYER_ROWS + 1, 16) if n_rows % t == 0)
    c1 = 1.0 - ADAM_B1 ** ADAM_STEP
    c2 = 1.0 - ADAM_B2 ** ADAM_STEP

    def body(*refs):
        gp_refs = refs[:n_layers]
        w_ref, m_ref, v_ref, g_ref, d_ref, nm_ref, nv_ref = refs[n_layers:]
        layer = pl.program_id(0)
        for k in range(n_layers):
            @pl.when(layer == k)
            def _(k=k):
                g = gp_refs[k][0].astype(F32)
                for p in range(1, n_parts):
                    g = g + gp_refs[k][p].astype(F32)
                m_new = ADAM_B1 * m_ref[...] + (1.0 - ADAM_B1) * g
                v_new = ADAM_B2 * v_ref[...] + (1.0 - ADAM_B2) * (g * g)
                g_ref[...] = g
                d_ref[...] = -ADAM_LR * ((m_new / c1) / (jnp.sqrt(v_new / c2) + ADAM_EPS) + ADAM_WD * w_ref[...])
                nm_ref[...] = m_new
                nv_ref[...] = v_new

    blk = pl.BlockSpec((None, tr, cols), lambda l, i: (l, i, 0))
    shp = jax.ShapeDtypeStruct((n_layers, n_rows, cols), F32)
    gp_specs = [pl.BlockSpec((n_parts, tr, cols), lambda l, i, k=k: (0, jnp.where(l == k, i, 0), 0))
                for k in range(n_layers)]
    return pl.pallas_call(body, grid=(n_layers, n_rows // tr), name=name, in_specs=gp_specs + [blk, blk, blk],
                          out_specs=[blk, blk, blk, blk], out_shape=[shp, shp, shp, shp],
                          compiler_params=_params("arbitrary", "arbitrary"))(*recvs, w, m, v)


def _pack_rows(arrays, cols, lead=0):
    flat = [a.reshape(a.shape[:lead] + (-1,)) for a in arrays]
    cat = jnp.concatenate(flat, axis=lead) if len(flat) > 1 else flat[0]
    n = cat.shape[lead]
    pad = (-n) % (cols * PACK_ROWS)
    if pad:
        cat = jnp.pad(cat, [(0, 0)] * lead + [(0, pad)])
    return cat.reshape(cat.shape[:lead] + (-1, cols))


def _unpack_rows(packed, shapes, lead=0):
    flat = packed.reshape(packed.shape[:lead] + (-1,))
    out, off = [], 0
    for s in shapes:
        n = math.prod(s)
        out.append(lax.slice_in_dim(flat, off, off + n, axis=lead).reshape(flat.shape[:lead] + tuple(s)))
        off += n
    return out


def kernel(x, ffn1_norm, ffn1_w_in, ffn1_w_out, mix_norm, ffn2_norm, ffn2_w_in, ffn2_w_out, final_norm, s5_w_in, s5_lam_re, s5_lam_im, s5_log_dt, s5_b_re, s5_b_im, s5_c_re, s5_c_im, s5_d, s5_w_out, sb_w_qkv, sb_w_out, lru_w_in, lru_conv_w, lru_conv_b, lru_w_a, lru_b_a, lru_w_x, lru_b_x, lru_lambda, lru_w_out, loss_target, m_ffn1_norm, m_ffn1_w_in, m_ffn1_w_out, m_mix_norm, m_ffn2_norm, m_ffn2_w_in, m_ffn2_w_out, m_final_norm, m_s5_w_in, m_s5_lam_re, m_s5_lam_im, m_s5_log_dt, m_s5_b_re, m_s5_b_im, m_s5_c_re, m_s5_c_im, m_s5_d, m_s5_w_out, m_sb_w_qkv, m_sb_w_out, m_lru_w_in, m_lru_conv_w, m_lru_conv_b, m_lru_w_a, m_lru_b_a, m_lru_w_x, m_lru_b_x, m_lru_lambda, m_lru_w_out, v_ffn1_norm, v_ffn1_w_in, v_ffn1_w_out, v_mix_norm, v_ffn2_norm, v_ffn2_w_in, v_ffn2_w_out, v_final_norm, v_s5_w_in, v_s5_lam_re, v_s5_lam_im, v_s5_log_dt, v_s5_b_re, v_s5_b_im, v_s5_c_re, v_s5_c_im, v_s5_d, v_s5_w_out, v_sb_w_qkv, v_sb_w_out, v_lru_w_in, v_lru_conv_w, v_lru_conv_b, v_lru_w_a, v_lru_b_a, v_lru_w_x, v_lru_b_x, v_lru_lambda, v_lru_w_out):
    local = dict(locals())
    W = {n: local[n] for n in WEIGHTS}
    M = {n: local["m_" + n] for n in WEIGHTS}
    V = {n: local["v_" + n] for n in WEIGHTS}

    h0 = x[0]
    target = loss_target[0]
    n_rows, dm = h0.shape
    depth = ffn1_norm.shape[0]

    ffn_seq = [(tag, layer) for layer in range(depth) for tag in ("ffn1", "ffn2")]

    def ffn_shards(tag, layer):
        return W[tag + "_w_in"][layer].astype(BF16), W[tag + "_w_out"][layer].astype(BF16)

    def ffn_views(wi, wo):
        return wi.reshape((2, N_DEV // 2) + wi.shape[1:]), wo.reshape((N_DEV // 2, 2) + wo.shape[1:])

    first_in, first_out = ffn_shards(*ffn_seq[0])
    ffn_w = {ffn_seq[0]: ffn_views(_all_gather("ag_first_w_in", first_in), _all_gather("ag_first_w_out", first_out))}

    ffn_saved = {}

    def ffn_forward(pos, h_in):
        tag, layer = ffn_seq[pos]
        comms = [("gather", a) for a in ffn_shards(*ffn_seq[pos + 1])] if pos + 1 < len(ffn_seq) else []
        res = _ffn_fwd("%s_fwd_%d" % (tag, layer), h_in, W[tag + "_norm"][layer:layer + 1], *ffn_w[ffn_seq[pos]],
                       comms=comms)
        ffn_saved[ffn_seq[pos]] = (res[1], res[2])
        if comms:
            ffn_w[ffn_seq[pos + 1]] = ffn_views(res[3], res[4])
        return res[0]

    mix_shapes = [W[n].shape for n in MIXER_BIG]
    mix_g = _all_gather("ag_mixers", _pack_rows([W[n].astype(BF16) for n in MIXER_BIG], dm))
    full = {n: _unshard(a, SHARD_AXIS[n]) for n, a in zip(MIXER_BIG, _unpack_rows(mix_g, mix_shapes, lead=1))}
    small_shapes = [W[n].shape for n in SMALL_SHARDED]
    small_g = _all_gather("ag_small", _pack_rows([W[n] for n in SMALL_SHARDED], 128))
    full.update({n: _unshard(a, SHARD_AXIS[n])
                 for n, a in zip(SMALL_SHARDED, _unpack_rows(small_g, small_shapes, lead=1))})

    n_s5 = s5_w_in.shape[0]
    s5_groups = s5_lam_re.shape[1]
    heads = dm // SB_HEAD_DIM

    grads = {}
    saved = []
    h = h0

    for layer in range(depth):
        kind, j = layer % 3, layer // 3
        rec = {"h0": h}
        h = ffn_forward(2 * layer, h)
        rec["h1"] = h
        gain = mix_norm[layer:layer + 1]
        if kind == 0:
            (u,) = _mm_fwd("s5_in_%d" % layer, h, full["s5_w_in"][j], F32, gain=gain)
            pars = (s5_lam_re[j], s5_lam_im[j], s5_log_dt[j], s5_b_re[j], s5_b_im[j], s5_c_re[j], s5_c_im[j])
            mats, mats_vjp = jax.vjp(_s5_mats, *pars)
            apw, ar_fwd, ar_rev = _s5_powers(*pars[:3])
            ys, sprev = _s5_fwd("s5_core_%d" % layer, u, *mats[:3], apw, ar_fwd)
            d_skip = full["s5_d"][j:j + 1]
            (z,) = _rowwise("s5_gelu_%d" % layer, lambda i, nt, ys_v, u_v, d_v: ((_gelu(ys_v + d_v * u_v),), ()),
                            [('t', ys, dm, 0), ('t', u, dm, 0), ('b', d_skip)], [(dm, BF16)], [], n_rows,
                            _tile(n_rows, 512))
            h, vg = _mm_fwd("s5_out_%d" % layer, z, full["s5_w_out"][j], F32, resid=h, glu=True)
            rec.update(u=u, ys=ys, sprev=sprev, z=z, vg=vg, mats=mats, mats_vjp=mats_vjp, apw=apw,
                       ar_rev=ar_rev, d_skip=d_skip)
        elif kind == 1:
            (qkv,) = _mm_fwd("sb_in_%d" % layer, h, full["sb_w_qkv"][j], BF16, gain=gain)
            qkv_h = qkv.reshape(n_rows, 3, heads, SB_HEAD_DIM).transpose(1, 2, 0, 3)
            o = _sb_fwd("sb_attn_%d" % layer, qkv_h[0], qkv_h[1], qkv_h[2])
            o_flat = o.transpose(1, 0, 2).reshape(n_rows, dm).astype(BF16)
            (h,) = _mm_fwd("sb_out_%d" % layer, o_flat, full["sb_w_out"][j], F32, resid=h)
            rec.update(qkv_h=qkv_h, o=o, o_flat=o_flat)
        else:
            (gx,) = _mm_fwd("lru_in_%d" % layer, h, full["lru_w_in"][j], F32, gain=gain)
            wa, wx = full["lru_w_a"][j], full["lru_w_x"][j]
            ba, bx = full["lru_b_a"][j].reshape(1, dm), full["lru_b_x"][j].reshape(1, dm)
            lam_row = full["lru_lambda"][j:j + 1]
            xc, r, ig, a, gated = _lru_gates_fwd("lru_gates_%d" % layer, gx, full["lru_conv_w"][j],
                                                 full["lru_conv_b"][j:j + 1], wa, ba, wx, bx, lam_row)
            hs, y = _lru_scan_fwd("lru_scan_%d" % layer, a, gated, gx)
            (h,) = _mm_fwd("lru_out_%d" % layer, y, full["lru_w_out"][j], F32, resid=h)
            rec.update(gx=gx, xc=xc, r=r, ig=ig, a=a, hs=hs, y=y, wa=wa, wx=wx, lam_row=lam_row)
        rec["h2"] = h
        h = ffn_forward(2 * layer + 1, h)
        saved.append(rec)

    dh, err2, dgf = _loss_head("loss_head", h, final_norm.reshape(1, dm), target)
    loss = lax.psum(0.5 / dm * jnp.sum(err2), ("x", "y", "c"))
    grads["final_norm"] = dgf.reshape(final_norm.shape)

    per_layer = {n: [None] * depth for n in ("ffn1_norm", "mix_norm", "ffn2_norm")}
    mixer_grads = {}
    recv_ffn = {}
    pending = []

    def ffn_backward(tag, layer, x_in, dh_in, more=None):
        comms = {0: [("exchange", pending[1])], 1: [("exchange", pending[2])]} if pending else {}
        comms.update(more or {})
        dx, dwi, dwo, dg, extra = _ffn_bwd("%s_bwd_%d" % (tag, layer), x_in, dh_in, W[tag + "_norm"][layer:layer + 1],
                                            *ffn_w[(tag, layer)], *ffn_saved[(tag, layer)], comms_by_block=comms)
        if pending:
            recv_ffn[pending[0]] = tuple(extra[:2])
            extra = extra[2:]
        pending[:] = [(tag, layer), dwi.reshape((N_DEV,) + dwi.shape[2:]).astype(BF16),
                      dwo.reshape(N_DEV, -1, dm).astype(BF16)]
        per_layer[tag + "_norm"][layer] = dg
        return dx, extra

    def put(name, j, value, count):
        mixer_grads.setdefault(name, [None] * count)[j] = value

    for layer in reversed(range(depth)):
        kind, j = layer % 3, layer // 3
        rec = saved[layer]
        dh, _ = ffn_backward("ffn2", layer, rec["h2"], dh)
        gain = mix_norm[layer:layer + 1]
        if kind == 0:
            dvg, = _rowwise("s5_glu_bwd_%d" % layer,
                            lambda i, nt, d_v, vg_v: ((jnp.concatenate(
                                [d_v * _sigmoid(vg_v[:, dm:]),
                                 d_v * vg_v[:, :dm] * _sigmoid(vg_v[:, dm:]) * (1.0 - _sigmoid(vg_v[:, dm:]))],
                                axis=1),), ()),
                            [('t', dh, dm, 0), ('t', rec["vg"], 2 * dm, 0)], [(2 * dm, BF16)], [], n_rows,
                            _tile(n_rows, 256))
            dz, dw_out = _mm_bwd("s5_out_bwd_%d" % layer, rec["z"], dvg, full["s5_w_out"][j])

            def gelu_bwd(i, nt, dz_v, ys_v, u_v, d_v):
                dy_v = dz_v * _gelu_grad(ys_v + d_v * u_v)
                return (dy_v,), (jnp.sum(dy_v * u_v, axis=0, keepdims=True),)

            dys, dd = _rowwise("s5_gelu_bwd_%d" % layer, gelu_bwd,
                               [('t', dz, dm, 0), ('t', rec["ys"], dm, 0), ('t', rec["u"], dm, 0),
                                ('b', rec["d_skip"])], [(dm, F32)], [(1, dm)], n_rows, _tile(n_rows, 512))
            m_, bm_, cm_, _ = rec["mats"]
            du_core, dm_m, dm_b, dm_c, d_a = _s5_bwd("s5_core_bwd_%d" % layer, rec["u"], dys, rec["sprev"],
                                                    m_, bm_, cm_, rec["apw"], rec["ar_rev"])
            dpars = rec["mats_vjp"]((dm_m, dm_b, dm_c, d_a.reshape(s5_groups, -1)))
            for nme, val in zip(("s5_lam_re", "s5_lam_im", "s5_log_dt", "s5_b_re", "s5_b_im", "s5_c_re", "s5_c_im"),
                                dpars):
                put(nme, j, val, n_s5)
            (du,) = _rowwise("s5_du_%d" % layer, lambda i, nt, a_v, dy_v, d_v: ((a_v + dy_v * d_v,), ()),
                             [('t', du_core, dm, 0), ('t', dys, dm, 0), ('b', rec["d_skip"])],
                             [(dm, BF16)], [], n_rows, _tile(n_rows, 512))
            dh, dw_in, dgm = _mm_bwd("s5_in_bwd_%d" % layer, rec["h1"], du, full["s5_w_in"][j], gain=gain, dres=dh)
            put("s5_d", j, dd[0], n_s5)
            put("s5_w_out", j, dw_out, n_s5)
            put("s5_w_in", j, dw_in, n_s5)
        elif kind == 1:
            do_flat, dw_out = _mm_bwd("sb_out_bwd_%d" % layer, rec["o_flat"], dh, full["sb_w_out"][j])
            do = do_flat.reshape(n_rows, heads, SB_HEAD_DIM).transpose(1, 0, 2)
            qkv_h = rec["qkv_h"]
            dq, dk, dv = _sb_bwd("sb_attn_bwd_%d" % layer, qkv_h[0], qkv_h[1], qkv_h[2], rec["o"], do)
            dqkv = jnp.stack([dq, dk, dv]).transpose(2, 0, 1, 3).reshape(n_rows, 3 * dm).astype(BF16)
            dh, dw_in, dgm = _mm_bwd("sb_in_bwd_%d" % layer, rec["h1"], dqkv, full["sb_w_qkv"][j], gain=gain, dres=dh)
            put("sb_w_out", j, dw_out, 1)
            put("sb_w_qkv", j, dw_in, 1)
        else:
            dy, dw_out = _mm_bwd("lru_out_bwd_%d" % layer, rec["y"], dh, full["lru_w_out"][j])
            lam_t = _lru_scan_bwd("lru_scan_bwd_%d" % layer, rec["a"], dy, rec["gx"])
            dxc, dwa, dwx, dba, dbx, dsp = _lru_gates_bwd("lru_gates_bwd_%d" % layer, lam_t, rec["hs"], rec["xc"],
                                                          rec["r"], rec["ig"], rec["a"], rec["wa"], rec["wx"],
                                                          rec["lam_row"])
            dgx, dcw, dcb = _lru_conv_bwd("lru_conv_bwd_%d" % layer, dxc, rec["gx"], dy, rec["hs"],
                                          full["lru_conv_w"][j])
            dh, dw_in, dgm = _mm_bwd("lru_in_bwd_%d" % layer, rec["h1"], dgx, full["lru_w_in"][j], gain=gain, dres=dh)
            nb = rec["wa"].shape[0]
            put("lru_w_out", j, dw_out, 1)
            put("lru_w_in", j, dw_in, 1)
            put("lru_w_a", j, dwa.reshape(rec["wa"].shape), 1)
            put("lru_w_x", j, dwx.reshape(rec["wx"].shape), 1)
            put("lru_b_a", j, dba.reshape(nb, -1), 1)
            put("lru_b_x", j, dbx.reshape(nb, -1), 1)
            put("lru_conv_w", j, dcw[:LRU_CONV], 1)
            put("lru_conv_b", j, dcb[0], 1)
            put("lru_lambda", j, (dsp * -_sigmoid(-rec["lam_row"]))[0], 1)
        per_layer["mix_norm"][layer] = dgm
        more = None
        if layer == 0:
            for n, parts in mixer_grads.items():
                grads[n] = jnp.stack(parts)
            send = _pack_rows([_shard_blocks(grads[n], SHARD_AXIS[n]).astype(BF16) for n in MIXER_BIG], dm, lead=1)
            half = send.shape[1] // 2
            more = {2: [("exchange", send[:, :half])], 3: [("exchange", send[:, half:])]}
        dh, got = ffn_backward("ffn1", layer, rec["h0"], dh, more)
        if layer == 0:
            recv_mixers = jnp.concatenate(got, axis=1)

    grad_x = dh[None]
    for n in ("ffn1_norm", "mix_norm", "ffn2_norm"):
        grads[n] = jnp.concatenate(per_layer[n], axis=0)

    out_g, out_d, out_m, out_v = {}, {}, {}, {}

    def finish(names, res, shapes):
        for n, g_, d_, m_, v_ in zip(names, *[_unpack_rows(t, shapes) for t in res]):
            out_g[n], out_d[n], out_m[n], out_v[n] = g_, d_, m_, v_

    recv_ffn[pending[0]] = (_exchange("xchg_last_w_in", pending[1]), _exchange("xchg_last_w_out", pending[2]))
    for tag in ("ffn1", "ffn2"):
        for which, n in enumerate((tag + "_w_in", tag + "_w_out")):
            out_g[n], out_d[n], out_m[n], out_v[n] = _adamw_layers(
                "adamw_" + n, [recv_ffn[(tag, layer)][which] for layer in range(depth)], W[n], M[n], V[n])

    finish(MIXER_BIG, _adamw("adamw_mixers", recv_mixers,
                             *[_pack_rows([t[n] for n in MIXER_BIG], dm) for t in (W, M, V)]), mix_shapes)

    small_names = REPLICATED + SMALL_SHARDED
    small_full_shapes = [grads[n].shape for n in small_names]
    parts = _all_gather("ag_small_grads", _pack_rows([grads[n] for n in small_names], 128))
    zero = jnp.zeros(parts.shape[1:], F32)
    summed = _adamw("sum_small_grads", parts, zero, zero, zero)[0]
    small_sum = dict(zip(small_names, _unpack_rows(summed, small_full_shapes)))
    me = 4 * lax.axis_index("x") + 2 * lax.axis_index("y") + lax.axis_index("c")
    rep_shapes = [W[n].shape for n in REPLICATED]
    g_rep = _pack_rows([small_sum[n] for n in REPLICATED], 128)[None]
    finish(REPLICATED, _adamw("adamw_replicated", g_rep, *[_pack_rows([t[n] for n in REPLICATED], 128)
                                                           for t in (W, M, V)]), rep_shapes)
    g_loc = []
    for n in SMALL_SHARDED:
        ax = SHARD_AXIS[n]
        size = W[n].shape[ax]
        g_loc.append(lax.dynamic_slice_in_dim(small_sum[n], me * size, size, axis=ax))
    finish(SMALL_SHARDED, _adamw("adamw_small", _pack_rows(g_loc, 128)[None],
                                 *[_pack_rows([t[n] for n in SMALL_SHARDED], 128) for t in (W, M, V)]), small_shapes)

    return (loss, grad_x, *[out_g[n] for n in WEIGHTS], *[out_d[n] for n in WEIGHTS],
            *[out_m[n] for n in WEIGHTS], *[out_v[n] for n in WEIGHTS])
```

```python
import functools
import math

import jax
import jax.numpy as jnp
from jax import lax
from jax.experimental import pallas as pl
from jax.experimental.pallas import tpu as pltpu

F32 = jnp.float32
BF16 = jnp.bfloat16
HI = lax.Precision.HIGHEST
S5_PREC = lax.Precision.HIGH
MESH = pl.DeviceIdType.MESH

N_DEV = 8
RMS_EPS = 1e-6
S5_GROUP = 16
S5_CHUNK = 16
S5_OCTET = 128 // S5_GROUP
S5_REGROUP_ROWS = 32
FFN_ROW_PARTS = 2
S5_SCAN_UNROLL = 8
SB_HEAD_DIM = 64
SB_UNDERFLOW = -104.0
SB_HEADS_FWD = 4
SB_HEADS_BWD = 2
LRU_CONV = 4
LRU_C = 8.0
ADAM_LR, ADAM_B1, ADAM_B2, ADAM_EPS, ADAM_WD, ADAM_STEP = 0.001, 0.9, 0.999, 1e-08, 0.01, 10
VMEM_LIMIT_BYTES = 56 * 1024 * 1024
SUBLANES = 8
PACK_ROWS = 256
ADAMW_LAYER_ROWS = 192

NN = (((1,), (0,)), ((), ()))
NT = (((1,), (1,)), ((), ()))
TN = (((0,), (0,)), ((), ()))

SHARD_AXIS = dict(
    ffn1_w_in=2, ffn1_w_out=1, ffn2_w_in=2, ffn2_w_out=1, s5_w_in=1, s5_d=1, s5_w_out=2, sb_w_qkv=2, sb_w_out=1,
    lru_w_in=2, lru_conv_w=2, lru_conv_b=1, lru_w_a=2, lru_b_a=2, lru_w_x=2, lru_b_x=2, lru_lambda=1, lru_w_out=1)
MIXER_BIG = ("s5_w_in", "s5_w_out", "sb_w_qkv", "sb_w_out", "lru_w_in", "lru_w_a", "lru_w_x", "lru_w_out")
SMALL_SHARDED = ("s5_d", "lru_conv_w", "lru_conv_b", "lru_b_a", "lru_b_x", "lru_lambda")
REPLICATED = ("ffn1_norm", "mix_norm", "ffn2_norm", "final_norm", "s5_lam_re", "s5_lam_im", "s5_log_dt",
              "s5_b_re", "s5_b_im", "s5_c_re", "s5_c_im")
WEIGHTS = ("ffn1_norm", "ffn1_w_in", "ffn1_w_out", "mix_norm", "ffn2_norm", "ffn2_w_in", "ffn2_w_out", "final_norm",
           "s5_w_in", "s5_lam_re", "s5_lam_im", "s5_log_dt", "s5_b_re", "s5_b_im", "s5_c_re", "s5_c_im", "s5_d",
           "s5_w_out", "sb_w_qkv", "sb_w_out", "lru_w_in", "lru_conv_w", "lru_conv_b", "lru_w_a", "lru_b_a",
           "lru_w_x", "lru_b_x", "lru_lambda", "lru_w_out")


def _dot(a, b, dims=NN, prec=None):
    return lax.dot_general(a, b, dims, precision=prec, preferred_element_type=F32)


def _params(*sem):
    return pltpu.CompilerParams(dimension_semantics=sem, vmem_limit_bytes=VMEM_LIMIT_BYTES)


def _tile(n, pref):
    return min(pref, n)


def _sigmoid(x):
    return jax.nn.sigmoid(x)


def _softplus(x):
    return jnp.maximum(x, 0.0) + jnp.log(1.0 + jnp.exp(-jnp.abs(x)))


_GELU_C = math.sqrt(2.0 / math.pi)


def _gelu(x):
    return 0.5 * x * (1.0 + jnp.tanh(_GELU_C * (x + 0.044715 * x * x * x)))


def _gelu_grad(x):
    t = jnp.tanh(_GELU_C * (x + 0.044715 * x * x * x))
    return 0.5 * (1.0 + t) + 0.5 * x * (1.0 - t * t) * _GELU_C * (1.0 + 3.0 * 0.044715 * x * x)


def _rms(x):
    r = lax.rsqrt(jnp.mean(x * x, axis=1, keepdims=True) + RMS_EPS)
    return r, x * r


def _rms_bwd(dhn, xhat, r, g):
    dxhat = dhn * g
    return r * (dxhat - xhat * jnp.mean(dxhat * xhat, axis=1, keepdims=True))


def _one_minus_a2_sqrt(log_a):
    t = jnp.tanh(log_a)
    return jnp.sqrt(-2.0 * t / (1.0 - t))


def _shift_down(cur, prev8, k, first):
    if k == 0:
        return cur
    row8 = lax.broadcasted_iota(jnp.int32, prev8.shape, 0)
    rolled = pltpu.roll(cur, k, 0)
    edge = jnp.where(first, 0.0, pltpu.roll(prev8, k, 0))
    top = jnp.where(row8 < k, edge, rolled[0:SUBLANES])
    return jnp.concatenate([top, rolled[SUBLANES:]], axis=0)


def _shift_up(cur, next8, k, last):
    if k == 0:
        return cur
    tm = cur.shape[0]
    row8 = lax.broadcasted_iota(jnp.int32, next8.shape, 0)
    rolled = pltpu.roll(cur, tm - k, 0)
    edge = jnp.where(last, 0.0, pltpu.roll(next8, SUBLANES - k, 0))
    bottom = jnp.where(row8 >= SUBLANES - k, edge, rolled[tm - SUBLANES:tm])
    return jnp.concatenate([rolled[:tm - SUBLANES], bottom], axis=0)


def _rowwise(name, fn, ins, out_tiled, out_acc, n_rows, tm, reverse=False):
    nt = n_rows // tm
    per8 = tm // SUBLANES
    n8 = n_rows // SUBLANES
    n_in, n_ot = len(ins), len(out_tiled)

    def pos(i):
        return nt - 1 - i if reverse else i

    in_specs, args = [], []
    for spec in ins:
        kind, arr = spec[0], spec[1]
        args.append(arr)
        if kind == 'b':
            in_specs.append(pl.BlockSpec(arr.shape, lambda i, nd=arr.ndim: (0,) * nd))
        elif kind == 't':
            in_specs.append(pl.BlockSpec((tm, spec[2]), lambda i, cb=spec[3]: (pos(i), cb)))
        elif kind == 'p':
            in_specs.append(pl.BlockSpec((SUBLANES, spec[2]),
                                         lambda i, cb=spec[3]: (jnp.maximum(pos(i) * per8 - 1, 0), cb)))
        else:
            in_specs.append(pl.BlockSpec((SUBLANES, spec[2]),
                                         lambda i, cb=spec[3]: (jnp.minimum((pos(i) + 1) * per8, n8 - 1), cb)))

    def body(*refs):
        i = pl.program_id(0)
        outs = refs[n_in:]
        touts, aouts = fn(pos(i), nt, *[r[...] for r in refs[:n_in]])
        for r, v in zip(outs[:n_ot], touts):
            r[...] = v.astype(r.dtype)
        if out_acc:
            @pl.when(i == 0)
            def _():
                for r in outs[n_ot:]:
                    r[...] = jnp.zeros(r.shape, r.dtype)
            for r, v in zip(outs[n_ot:], aouts):
                r[...] += v

    out_specs = [pl.BlockSpec((tm, n), lambda i: (pos(i), 0)) for n, _ in out_tiled]
    out_specs += [pl.BlockSpec((r, n), lambda i: (0, 0)) for r, n in out_acc]
    out_shape = [jax.ShapeDtypeStruct((n_rows, n), dt) for n, dt in out_tiled]
    out_shape += [jax.ShapeDtypeStruct((r, n), F32) for r, n in out_acc]
    return pl.pallas_call(body, grid=(nt,), in_specs=in_specs, out_specs=out_specs, out_shape=out_shape, name=name,
                          compiler_params=_params("arbitrary"))(*args)


def _all_gather(name, block):
    def body(x_ref, out_ref, send_sems, recv_sems, local_sem):
        x, y, c = lax.axis_index("x"), lax.axis_index("y"), lax.axis_index("c")
        me, sibling = (x, y, c), (x, y, 1 - c)
        chips = [(1 - x, y), (x, 1 - y), (1 - x, 1 - y)]

        def rows(px, py, pc):
            return out_ref.at[4 * px + 2 * py + pc]

        def copy(k, blk, to, src=None):
            return pltpu.make_async_remote_copy(
                src_ref=rows(*blk) if src is None else src, dst_ref=rows(*blk),
                send_sem=send_sems.at[k], recv_sem=recv_sems.at[k], device_id=to, device_id_type=MESH)

        mine = pltpu.make_async_copy(x_ref, rows(*me), local_sem)
        mine.start()
        first = [copy(0, me, sibling, src=x_ref)]
        first += [copy(1 + j, me, (*chip, c), src=x_ref) for j, chip in enumerate(chips)]
        for cp in first:
            cp.start()
        passed = [copy(4 + j, (*chip, c), sibling) for j, chip in enumerate(chips)]
        for j, chip in enumerate(chips):
            copy(1 + j, (*chip, c), me).wait_recv()
            passed[j].start()
        copy(0, sibling, me).wait_recv()
        for j, chip in enumerate(chips):
            copy(4 + j, (*chip, 1 - c), me).wait_recv()
        for cp in first + passed:
            cp.wait_send()
        mine.wait()

    return pl.pallas_call(
        body, name=name, out_shape=jax.ShapeDtypeStruct((N_DEV,) + block.shape, block.dtype),
        in_specs=[pl.BlockSpec(memory_space=pl.ANY)], out_specs=pl.BlockSpec(memory_space=pl.ANY),
        scratch_shapes=[pltpu.SemaphoreType.DMA((7,)), pltpu.SemaphoreType.DMA((7,)), pltpu.SemaphoreType.DMA(())],
    )(block)


def _exchange(name, send):
    def body(s_ref, r_ref, send_sems, recv_sems, local_sem):
        x, y, c = lax.axis_index("x"), lax.axis_index("y"), lax.axis_index("c")
        me = 4 * x + 2 * y + c
        mine = pltpu.make_async_copy(s_ref.at[me], r_ref.at[me], local_sem)
        mine.start()
        copies = []
        for k in range(1, N_DEV):
            dx, dy, dc = (k >> 2) & 1, (k >> 1) & 1, k & 1
            px = 1 - x if dx else x
            py = 1 - y if dy else y
            pc = 1 - c if dc else c
            peer = 4 * px + 2 * py + pc
            copies.append((pltpu.make_async_remote_copy(
                src_ref=s_ref.at[peer], dst_ref=r_ref.at[me], send_sem=send_sems.at[k - 1],
                recv_sem=recv_sems.at[k - 1], device_id=(px, py, pc), device_id_type=MESH), peer))
        for cp, _ in copies:
            cp.start()
        for k, (cp, peer) in enumerate(copies):
            pltpu.make_async_remote_copy(
                src_ref=s_ref.at[peer], dst_ref=r_ref.at[peer], send_sem=send_sems.at[k], recv_sem=recv_sems.at[k],
                device_id=(x, y, c), device_id_type=MESH).wait_recv()
        for cp, _ in copies:
            cp.wait_send()
        mine.wait()

    return pl.pallas_call(
        body, name=name, out_shape=jax.ShapeDtypeStruct(send.shape, send.dtype),
        in_specs=[pl.BlockSpec(memory_space=pl.ANY)], out_specs=pl.BlockSpec(memory_space=pl.ANY),
        scratch_shapes=[pltpu.SemaphoreType.DMA((7,)), pltpu.SemaphoreType.DMA((7,)), pltpu.SemaphoreType.DMA(())],
    )(send)


def _direct_copies(kind, s_ref, r_ref, send_sems, recv_sems, local_sem):
    x, y, c = lax.axis_index("x"), lax.axis_index("y"), lax.axis_index("c")
    me = 4 * x + 2 * y + c

    def src(p):
        return s_ref if kind == "gather" else s_ref.at[p]

    local = pltpu.make_async_copy(src(me), r_ref.at[me], local_sem)
    sends, recvs = [], []
    for k in range(1, N_DEV):
        px = 1 - x if (k >> 2) & 1 else x
        py = 1 - y if (k >> 1) & 1 else y
        pc = 1 - c if k & 1 else c
        peer = 4 * px + 2 * py + pc
        sends.append(pltpu.make_async_remote_copy(
            src_ref=src(peer), dst_ref=r_ref.at[me], send_sem=send_sems.at[k - 1], recv_sem=recv_sems.at[k - 1],
            device_id=(px, py, pc), device_id_type=MESH))
        recvs.append(pltpu.make_async_remote_copy(
            src_ref=src(peer), dst_ref=r_ref.at[peer], send_sem=send_sems.at[k - 1], recv_sem=recv_sems.at[k - 1],
            device_id=(x, y, c), device_id_type=MESH))
    return local, sends, recvs


def _call(body, *, grid, in_specs, out_specs, out_shape, name, args, scratch_shapes=(), semantics=None, comms=()):
    single = not isinstance(out_shape, (list, tuple))
    out_shape = [out_shape] if single else list(out_shape)
    out_specs = [out_specs] if single else list(out_specs)
    if not comms:
        res = pl.pallas_call(body, grid=grid, in_specs=in_specs, out_specs=out_specs, out_shape=out_shape, name=name,
                             scratch_shapes=list(scratch_shapes),
                             compiler_params=_params(*(semantics or ("arbitrary",) * len(grid))))(*args)
        return res[0] if single else res
    n_in, n_out, n_scr, n_c = len(args), len(out_shape), len(scratch_shapes), len(comms)

    def hosted(*refs):
        ins, srcs = refs[:n_in], refs[n_in:n_in + n_c]
        outs = refs[n_in + n_c:n_in + n_c + n_out]
        dsts = refs[n_in + n_c + n_out:n_in + 2 * n_c + n_out]
        scr = refs[n_in + 2 * n_c + n_out:n_in + 2 * n_c + n_out + n_scr]
        sems = refs[n_in + 2 * n_c + n_out + n_scr:]
        first = functools.reduce(jnp.logical_and, [pl.program_id(d) == 0 for d in range(len(grid))])
        last = functools.reduce(jnp.logical_and, [pl.program_id(d) == grid[d] - 1 for d in range(len(grid))])
        plans = [_direct_copies(comms[i][0], srcs[i], dsts[i], *sems[3 * i:3 * i + 3]) for i in range(n_c)]

        @pl.when(first)
        def _():
            for local, sends, _ in plans:
                local.start()
                for cp in sends:
                    cp.start()

        body(*ins, *outs, *scr)

        @pl.when(last)
        def _():
            for local, sends, recvs in plans:
                for cp in recvs:
                    cp.wait_recv()
                for cp in sends:
                    cp.wait_send()
                local.wait()

    any_spec = pl.BlockSpec(memory_space=pl.ANY)
    comm_shapes = [jax.ShapeDtypeStruct(((N_DEV,) + a.shape) if kind == "gather" else a.shape, a.dtype)
                   for kind, a in comms]
    sem_shapes = []
    for _ in comms:
        sem_shapes += [pltpu.SemaphoreType.DMA((7,)), pltpu.SemaphoreType.DMA((7,)), pltpu.SemaphoreType.DMA(())]
    res = pl.pallas_call(
        hosted, grid=grid, in_specs=list(in_specs) + [any_spec] * n_c, out_specs=out_specs + [any_spec] * n_c,
        out_shape=out_shape + comm_shapes, name=name, scratch_shapes=list(scratch_shapes) + sem_shapes,
        compiler_params=_params(*(("arbitrary",) * len(grid))))(*args, *[a for _, a in comms])
    return res


def _unshard(gathered, axis):
    local = gathered.shape[1:]
    full = jnp.moveaxis(gathered, 0, axis)
    return full.reshape(local[:axis] + (N_DEV * local[axis],) + local[axis + 1:])


def _shard_blocks(full, axis):
    s = full.shape
    cut = full.reshape(s[:axis] + (N_DEV, s[axis] // N_DEV) + s[axis + 1:])
    return jnp.moveaxis(cut, axis, 0)


def _mm_fwd(name, a, w, out_dtype, gain=None, resid=None, glu=False):
    n_rows, k = a.shape
    n = w.shape[1]
    tm = _tile(n_rows, 512)
    n_out = n // 2 if glu else n

    def body(*refs):
        it = iter(refs)
        a_ref, w_ref = next(it), next(it)
        g_ref = next(it) if gain is not None else None
        r_ref = next(it) if resid is not None else None
        outs = list(it)
        av = a_ref[...]
        if g_ref is not None:
            _, xhat = _rms(av)
            av = xhat * g_ref[...]
        res = _dot(av.astype(BF16), w_ref[...])
        if glu:
            outs[1][...] = res.astype(outs[1].dtype)
            res = res[:, :n_out] * _sigmoid(res[:, n_out:])
        if r_ref is not None:
            res = res + r_ref[...]
        outs[0][...] = res.astype(outs[0].dtype)

    args = [a, w]
    in_specs = [pl.BlockSpec((tm, k), lambda i: (i, 0)), pl.BlockSpec((k, n), lambda i: (0, 0))]
    if gain is not None:
        args.append(gain)
        in_specs.append(pl.BlockSpec((1, k), lambda i: (0, 0)))
    if resid is not None:
        args.append(resid)
        in_specs.append(pl.BlockSpec((tm, n_out), lambda i: (i, 0)))
    out_shape = [jax.ShapeDtypeStruct((n_rows, n_out), out_dtype)]
    out_specs = [pl.BlockSpec((tm, n_out), lambda i: (i, 0))]
    if glu:
        out_shape.append(jax.ShapeDtypeStruct((n_rows, n), F32))
        out_specs.append(pl.BlockSpec((tm, n), lambda i: (i, 0)))
    return pl.pallas_call(body, grid=(n_rows // tm,), in_specs=in_specs, out_specs=out_specs, out_shape=out_shape,
                          name=name, compiler_params=_params("parallel"))(*args)


def _mm_bwd(name, a, d, w, gain=None, dres=None):
    n_rows, k = a.shape
    n = w.shape[1]
    tm = _tile(n_rows, 512)

    def body(*refs):
        it = iter(refs)
        a_ref, d_ref, w_ref = next(it), next(it), next(it)
        g_ref = next(it) if gain is not None else None
        r_ref = next(it) if gain is not None else None
        da_ref, dw_ref = next(it), next(it)
        dg_ref = next(it) if gain is not None else None
        i = pl.program_id(0)

        @pl.when(i == 0)
        def _():
            dw_ref[...] = jnp.zeros(dw_ref.shape, F32)
            if dg_ref is not None:
                dg_ref[...] = jnp.zeros(dg_ref.shape, F32)

        av = a_ref[...]
        dv = d_ref[...].astype(BF16)
        if g_ref is not None:
            r, xhat = _rms(av)
            ab = (xhat * g_ref[...]).astype(BF16)
        else:
            ab = av.astype(BF16)
        dw_ref[...] += _dot(ab, dv, TN)
        da = _dot(dv, w_ref[...], NT)
        if g_ref is not None:
            dg_ref[...] += jnp.sum(da * xhat, axis=0, keepdims=True)
            da = r_ref[...] + _rms_bwd(da, xhat, r, g_ref[...])
        da_ref[...] = da.astype(da_ref.dtype)

    args = [a, d, w]
    in_specs = [pl.BlockSpec((tm, k), lambda i: (i, 0)), pl.BlockSpec((tm, n), lambda i: (i, 0)),
                pl.BlockSpec((k, n), lambda i: (0, 0))]
    out_shape = [jax.ShapeDtypeStruct((n_rows, k), F32), jax.ShapeDtypeStruct((k, n), F32)]
    out_specs = [pl.BlockSpec((tm, k), lambda i: (i, 0)), pl.BlockSpec((k, n), lambda i: (0, 0))]
    if gain is not None:
        args += [gain, dres]
        in_specs += [pl.BlockSpec((1, k), lambda i: (0, 0)), pl.BlockSpec((tm, k), lambda i: (i, 0))]
        out_shape.append(jax.ShapeDtypeStruct((1, k), F32))
        out_specs.append(pl.BlockSpec((1, k), lambda i: (0, 0)))
    return pl.pallas_call(body, grid=(n_rows // tm,), in_specs=in_specs, out_specs=out_specs, out_shape=out_shape,
                          name=name, compiler_params=_params("arbitrary"))(*args)


def _ffn_fwd(name, x, gain, wi, wo, comms=()):
    n_rows, dm = x.shape
    _, nj, _, fb = wi.shape
    tm = _tile(n_rows, 512)

    def body(x_ref, g_ref, wi_ref, wo_ref, y_ref, gate_ref, up_ref):
        xv = x_ref[...]
        _, xhat = _rms(xv)
        hn = (xhat * g_ref[...]).astype(BF16)
        acc = jnp.zeros((tm, dm), F32)
        for j in range(nj):
            gate = _dot(hn, wi_ref[0, j])
            up = _dot(hn, wi_ref[1, j])
            gate_ref[j] = gate.astype(BF16)
            up_ref[j] = up.astype(BF16)
            act = (gate * _sigmoid(gate) * up).astype(BF16)
            acc = acc + _dot(act, wo_ref[j].reshape(fb, dm))
        y_ref[...] = xv + 0.5 * acc

    return _call(
        body, grid=(n_rows // tm,), name=name, args=[x, gain, wi, wo], comms=comms,
        in_specs=[pl.BlockSpec((tm, dm), lambda i: (i, 0)), pl.BlockSpec((1, dm), lambda i: (0, 0)),
                  pl.BlockSpec((2, nj, dm, fb), lambda i: (0, 0, 0, 0)),
                  pl.BlockSpec((nj, 2, fb // 2, dm), lambda i: (0, 0, 0, 0))],
        out_specs=[pl.BlockSpec((tm, dm), lambda i: (i, 0)), pl.BlockSpec((nj, tm, fb), lambda i: (0, i, 0)),
                   pl.BlockSpec((nj, tm, fb), lambda i: (0, i, 0))],
        out_shape=[jax.ShapeDtypeStruct((n_rows, dm), F32), jax.ShapeDtypeStruct((nj, n_rows, fb), BF16),
                   jax.ShapeDtypeStruct((nj, n_rows, fb), BF16)])


def _ffn_bwd_block(name, x, dy, gain, wi, wo, gate_s, up_s, j, acc, comms=()):
    n_rows, dm = x.shape
    _, nj, _, fb = wi.shape
    tm = _tile(n_rows, 512)
    last = j == nj - 1

    def body(*refs):
        it = iter(refs)
        x_ref, dy_ref, g_ref, wi_ref, wo_ref = next(it), next(it), next(it), next(it), next(it)
        gate_ref, up_ref = next(it), next(it)
        acc_ref = next(it) if acc is not None else None
        out_ref, dwi_ref, dwo_ref = next(it), next(it), next(it)
        dg_ref = next(it) if last else None
        i = pl.program_id(0)

        @pl.when(i == 0)
        def _():
            dwi_ref[...] = jnp.zeros(dwi_ref.shape, F32)
            dwo_ref[...] = jnp.zeros(dwo_ref.shape, F32)
            if last:
                dg_ref[...] = jnp.zeros(dg_ref.shape, F32)

        g = g_ref[...]
        wg, wu, wob = wi_ref[0], wi_ref[1], wo_ref[...].reshape(fb, dm)
        parts = range(FFN_ROW_PARTS)
        rp = tm // FFN_ROW_PARTS
        rows = [slice(k * rp, (k + 1) * rp) for k in parts]
        xv = [x_ref[rows[k], :] for k in parts]
        dyv = [dy_ref[rows[k], :] for k in parts]
        rx = [_rms(xv[k]) for k in parts]
        hn = [(rx[k][1] * g).astype(BF16) for k in parts]
        dout = [(0.5 * dyv[k]).astype(BF16) for k in parts]
        dact = [_dot(dout[k], wob, NT) for k in parts]
        gate = [gate_ref[rows[k], :].astype(F32) for k in parts]
        up = [up_ref[rows[k], :].astype(F32) for k in parts]
        s = [_sigmoid(gate[k]) for k in parts]
        silu = [gate[k] * s[k] for k in parts]
        act = [(silu[k] * up[k]).astype(BF16) for k in parts]
        dgate = [(dact[k] * up[k] * (s[k] * (1.0 + gate[k] * (1.0 - s[k])))).astype(BF16) for k in parts]
        dup = [(dact[k] * silu[k]).astype(BF16) for k in parts]
        for k in parts:
            dwo_ref[...] += _dot(act[k], dout[k], TN)
            dwi_ref[0] += _dot(hn[k], dgate[k], TN)
            dwi_ref[1] += _dot(hn[k], dup[k], TN)
        tot = [_dot(dgate[k], wg, NT) + _dot(dup[k], wu, NT) for k in parts]
        for k in parts:
            t = tot[k] + acc_ref[rows[k], :] if acc_ref is not None else tot[k]
            if last:
                out_ref[rows[k], :] = dyv[k] + _rms_bwd(t, rx[k][1], rx[k][0], g)
                dg_ref[...] += jnp.sum(t * rx[k][1], axis=0, keepdims=True)
            else:
                out_ref[rows[k], :] = t

    tok = pl.BlockSpec((tm, dm), lambda i: (i, 0))
    args = [x, dy, gain, wi, wo, gate_s, up_s]
    saved = pl.BlockSpec((None, tm, fb), lambda i: (j, i, 0))
    in_specs = [tok, tok, pl.BlockSpec((1, dm), lambda i: (0, 0)),
                pl.BlockSpec((2, None, dm, fb), lambda i: (0, j, 0, 0)),
                pl.BlockSpec((None, 2, fb // 2, dm), lambda i: (j, 0, 0, 0)), saved, saved]
    if acc is not None:
        args.append(acc)
        in_specs.append(tok)
    out_specs = [tok, pl.BlockSpec((2, dm, fb), lambda i: (0, 0, 0)), pl.BlockSpec((fb, dm), lambda i: (0, 0))]
    out_shape = [jax.ShapeDtypeStruct((n_rows, dm), F32), jax.ShapeDtypeStruct((2, dm, fb), F32),
                 jax.ShapeDtypeStruct((fb, dm), F32)]
    if last:
        out_specs.append(pl.BlockSpec((1, dm), lambda i: (0, 0)))
        out_shape.append(jax.ShapeDtypeStruct((1, dm), F32))
    return _call(body, grid=(n_rows // tm,), name=name, in_specs=in_specs, out_specs=out_specs,
                 out_shape=out_shape, args=args, comms=comms)


def _ffn_bwd(name, x, dy, gain, wi, wo, gate_s, up_s, comms_by_block=None):
    nj = wi.shape[1]
    acc, dwi, dwo, extra = None, [], [], []
    for j in range(nj):
        comms = (comms_by_block or {}).get(j, ())
        res = _ffn_bwd_block("%s_%d" % (name, j), x, dy, gain, wi, wo, gate_s, up_s, j, acc, comms)
        n_own = 4 if j == nj - 1 else 3
        acc = res[0]
        dwi.append(res[1])
        dwo.append(res[2])
        extra += list(res[n_own:])
        dgain = res[3] if j == nj - 1 else None
    return acc, jnp.stack(dwi, axis=1), jnp.stack(dwo, axis=0), dgain, extra


def _scan8(a, x, reverse):
    row = lax.broadcasted_iota(jnp.int32, a.shape, 0)
    for k in (1, 2, 4):
        if reverse:
            keep = row < SUBLANES - k
            a_s, x_s = pltpu.roll(a, SUBLANES - k, 0), pltpu.roll(x, SUBLANES - k, 0)
        else:
            keep = row >= k
            a_s, x_s = pltpu.roll(a, k, 0), pltpu.roll(x, k, 0)
        x = a * jnp.where(keep, x_s, 0.0) + x
        a = a * jnp.where(keep, a_s, 1.0)
    return a, x


def _scan_tile(a_ref, x_ref, h_ref, carry, reverse, rows):
    groups = rows // SUBLANES

    def step(n, c):
        gidx = groups - 1 - n if reverse else n
        sl = pl.ds(pl.multiple_of(gidx * SUBLANES, SUBLANES), SUBLANES)
        a_cum, h0 = _scan8(a_ref[sl, :], x_ref[sl, :], reverse)
        h = a_cum * c + h0
        h_ref[sl, :] = h
        return h[0:1] if reverse else h[SUBLANES - 1:SUBLANES]

    return lax.fori_loop(0, groups, step, carry)


def _s5_mats(lam_re, lam_im, log_dt, b_re, b_im, c_re, c_im):
    lc = S5_CHUNK
    groups, p = lam_re.shape
    h = b_re.shape[-1]
    lam = lax.complex(lam_re, lam_im)
    lam_dt = lam * jnp.exp(log_dt)[:, None]
    lam_bar = jnp.exp(lam_dt)
    b_bar = ((lam_bar - 1.0) / lam)[:, :, None] * lax.complex(b_re, b_im)
    c = lax.complex(c_re, c_im)
    pw = jnp.exp(lam_dt[None] * jnp.arange(lc + 1, dtype=F32)[:, None, None])
    resp = jnp.einsum('ghp,tgp,gpk->tghk', c, pw[:lc], b_bar, precision=HI).real
    s_idx = jnp.arange(lc)[:, None]
    u_idx = jnp.arange(lc)[None, :]
    onehot = (jnp.arange(lc)[:, None, None] == (u_idx - s_idx)[None]).astype(F32)
    m = jnp.einsum('tghk,tsu->gskuh', resp, onehot, precision=HI).reshape(groups, lc * h, lc * h)
    w = pw[lc - 1::-1][:lc].transpose(1, 0, 2)[:, :, None, :] * b_bar.transpose(0, 2, 1)[:, None]
    bm = jnp.concatenate([w.real, w.imag], axis=-1).reshape(groups, lc * h, 2 * p)
    v = c[:, None] * pw[1:lc + 1].transpose(1, 0, 2)[:, :, None, :]
    v = v.transpose(0, 3, 1, 2)
    cm = jnp.concatenate([v.real, -v.imag], axis=1).reshape(groups, 2 * p, lc * h)
    a = jnp.concatenate([pw[lc].real, pw[lc].imag], axis=-1)
    return m, bm, cm, a


def _s5_powers(lam_re, lam_im, log_dt):
    lam_dt = lax.complex(lam_re, lam_im) * jnp.exp(log_dt)[:, None]
    pw = jnp.exp(lam_dt[None] * (S5_CHUNK * jnp.arange(1, 9, dtype=F32))[:, None, None])

    def c1(z):
        return jnp.concatenate([z.real, z.real], axis=-1).reshape(z.shape[0], -1)

    def c2(z):
        return jnp.concatenate([-z.imag, z.imag], axis=-1).reshape(z.shape[0], -1)

    p1, p2 = c1(pw), c2(pw)
    apw = jnp.stack([p1[0], p2[0], p1[1], p2[1], p1[3], p2[3], jnp.zeros_like(p1[0]), jnp.zeros_like(p1[0])])
    fwd = jnp.concatenate([p1, p2], axis=0)
    rev = jnp.concatenate([c1(pw[::-1]), c2(pw[::-1])], axis=0)
    return apw, fwd, rev


def _cmul(c1, c2, x, half, conj=False):
    sw = pltpu.roll(x, half, 1)
    return c1 * x - c2 * sw if conj else c1 * x + c2 * sw


def _gather_groups(u_ref, ug_ref, nc):
    h = S5_GROUP
    rows = min(S5_REGROUP_ROWS, nc)

    def step(r, _):
        base = pl.multiple_of(r * rows, rows)
        for t in range(S5_CHUNK):
            val = u_ref[pl.ds(base * S5_CHUNK + t, rows, stride=S5_CHUNK), :]
            for g in range(S5_OCTET):
                ug_ref[g, pl.ds(base, rows), t * h:(t + 1) * h] = val[:, g * h:(g + 1) * h]
        return 0

    lax.fori_loop(0, nc // rows, step, 0)


def _scatter_groups(yg_ref, y_ref, nc):
    h = S5_GROUP
    rows = min(S5_REGROUP_ROWS, nc)

    def step(r, _):
        base = pl.multiple_of(r * rows, rows)
        for t in range(S5_CHUNK):
            y_ref[pl.ds(base * S5_CHUNK + t, rows, stride=S5_CHUNK), :] = jnp.concatenate(
                [yg_ref[g, pl.ds(base, rows), t * h:(t + 1) * h] for g in range(S5_OCTET)], axis=1)
        return 0

    lax.fori_loop(0, nc // rows, step, 0)


def _s5_fwd(name, u, m, bm, cm, apw, arows):
    n_rows, width = u.shape
    nc = n_rows // S5_CHUNK
    groups, lh, _ = m.shape
    p2 = bm.shape[2]
    gb = S5_OCTET
    lanes = gb * S5_GROUP

    def body(u_ref, m_ref, b_ref, c_ref, apw_ref, ar_ref, y_ref, sp_ref, ug_ref, yg_ref, xs_ref):
        _gather_groups(u_ref, ug_ref, nc)
        for gi in range(gb):
            xs_ref[:, gi * p2:(gi + 1) * p2] = _dot(ug_ref[gi], b_ref[gi], prec=S5_PREC)
        row = lax.broadcasted_iota(jnp.int32, (SUBLANES, p2), 0)

        def group(n, carry):
            sl = pl.ds(pl.multiple_of(n * SUBLANES, SUBLANES), SUBLANES)
            new = []
            for gi in range(gb):
                ln = slice(gi * p2, (gi + 1) * p2)
                x = xs_ref[sl, ln]
                for q, k in enumerate((1, 2, 4)):
                    xs = jnp.where(row >= k, pltpu.roll(x, k, 0), 0.0)
                    x = x + _cmul(apw_ref[2 * q:2 * q + 1, ln], apw_ref[2 * q + 1:2 * q + 2, ln], xs, p2 // 2)
                cb = jnp.broadcast_to(carry[gi], (SUBLANES, p2))
                s8 = x + _cmul(ar_ref[0:8, ln], ar_ref[8:16, ln], cb, p2 // 2)
                sp_ref[sl, ln] = jnp.where(row >= 1, pltpu.roll(s8, 1, 0), cb)
                new.append(s8[SUBLANES - 1:SUBLANES])
            return tuple(new)

        unroll = min(S5_SCAN_UNROLL, nc // SUBLANES)

        def step(n, carry):
            for k in range(unroll):
                carry = group(n * unroll + k, carry)
            return carry

        lax.fori_loop(0, nc // (SUBLANES * unroll), step, tuple(jnp.zeros((1, p2), F32) for _ in range(gb)))
        for gi in range(gb):
            yg_ref[gi] = (_dot(ug_ref[gi], m_ref[gi], prec=S5_PREC)
                          + _dot(sp_ref[:, gi * p2:(gi + 1) * p2], c_ref[gi], prec=S5_PREC))
        _scatter_groups(yg_ref, y_ref, nc)

    tok = pl.BlockSpec((n_rows, lanes), lambda g: (0, g), pipeline_mode=pl.Buffered(1))
    return pl.pallas_call(
        body, grid=(groups // gb,), name=name,
        in_specs=[tok, pl.BlockSpec((gb, lh, lh), lambda g: (g, 0, 0)),
                  pl.BlockSpec((gb, lh, p2), lambda g: (g, 0, 0)), pl.BlockSpec((gb, p2, lh), lambda g: (g, 0, 0)),
                  pl.BlockSpec((8, gb * p2), lambda g: (0, g)), pl.BlockSpec((16, gb * p2), lambda g: (0, g))],
        out_specs=[tok, pl.BlockSpec((nc, gb * p2), lambda g: (0, g))],
        out_shape=[jax.ShapeDtypeStruct((n_rows, width), F32), jax.ShapeDtypeStruct((nc, groups * p2), F32)],
        scratch_shapes=[pltpu.VMEM((gb, nc, lh), F32), pltpu.VMEM((gb, nc, lh), F32), pltpu.VMEM((nc, gb * p2), F32)],
        compiler_params=_params("parallel"),
    )(u, m, bm, cm, apw, arows)


def _s5_bwd(name, u, dy, sprev, m, bm, cm, apw, arows_rev):
    n_rows, width = u.shape
    nc = n_rows // S5_CHUNK
    groups, lh, _ = m.shape
    p2 = bm.shape[2]
    half = p2 // 2
    gb = S5_OCTET
    lanes = gb * S5_GROUP

    def body(u_ref, dy_ref, sp_ref, m_ref, b_ref, c_ref, apw_ref, ar_ref,
             du_ref, dm_ref, db_ref, dc_ref, da_ref, ug_ref, dyg_ref, ds_ref, gx_ref):
        _gather_groups(u_ref, ug_ref, nc)
        _gather_groups(dy_ref, dyg_ref, nc)
        for gi in range(gb):
            ds_ref[:, gi * p2:(gi + 1) * p2] = _dot(dyg_ref[gi], c_ref[gi], NT, prec=S5_PREC)
        row = lax.broadcasted_iota(jnp.int32, (SUBLANES, p2), 0)
        lane = lax.broadcasted_iota(jnp.int32, (SUBLANES, p2), 1)
        ngroups = nc // SUBLANES

        def group(n, state):
            carry, nxt, dacc = state
            sl = pl.ds(pl.multiple_of((ngroups - 1 - n) * SUBLANES, SUBLANES), SUBLANES)
            new_c, new_n, new_d = [], [], []
            for gi in range(gb):
                ln = slice(gi * p2, (gi + 1) * p2)
                d8 = ds_ref[sl, ln]
                x = jnp.where(row < SUBLANES - 1, pltpu.roll(d8, SUBLANES - 1, 0),
                              jnp.broadcast_to(nxt[gi], (SUBLANES, p2)))
                for q, k in enumerate((1, 2, 4)):
                    xs = jnp.where(row < SUBLANES - k, pltpu.roll(x, SUBLANES - k, 0), 0.0)
                    x = x + _cmul(apw_ref[2 * q:2 * q + 1, ln], apw_ref[2 * q + 1:2 * q + 2, ln], xs, half, conj=True)
                cb = jnp.broadcast_to(carry[gi], (SUBLANES, p2))
                g8 = x + _cmul(ar_ref[0:8, ln], ar_ref[8:16, ln], cb, half, conj=True)
                gx_ref[sl, ln] = g8
                s8 = sp_ref[sl, ln]
                p1 = g8 * s8
                pq = g8 * pltpu.roll(s8, half, 1)
                d_a = jnp.where(lane < half, p1 + pltpu.roll(p1, half, 1), pq - pltpu.roll(pq, half, 1))
                new_c.append(g8[0:1])
                new_n.append(d8[0:1])
                new_d.append(dacc[gi] + jnp.sum(d_a, axis=0, keepdims=True))
            return tuple(new_c), tuple(new_n), tuple(new_d)

        unroll = min(S5_SCAN_UNROLL, ngroups)

        def step(n, state):
            for k in range(unroll):
                state = group(n * unroll + k, state)
            return state

        zeros = tuple(jnp.zeros((1, p2), F32) for _ in range(gb))
        _, _, dacc = lax.fori_loop(0, ngroups // unroll, step, (zeros, zeros, zeros))
        for gi in range(gb):
            ln = slice(gi * p2, (gi + 1) * p2)
            da_ref[:, ln] = dacc[gi]
            ug, dyg, gxg = ug_ref[gi], dyg_ref[gi], gx_ref[:, ln]
            dm_ref[gi] = _dot(ug, dyg, TN, prec=S5_PREC)
            dc_ref[gi] = _dot(sp_ref[:, ln], dyg, TN, prec=S5_PREC)
            db_ref[gi] = _dot(ug, gxg, TN, prec=S5_PREC)
            dyg_ref[gi] = _dot(dyg, m_ref[gi], NT, prec=S5_PREC) + _dot(gxg, b_ref[gi], NT, prec=S5_PREC)
        _scatter_groups(dyg_ref, du_ref, nc)

    tok = pl.BlockSpec((n_rows, lanes), lambda g: (0, g), pipeline_mode=pl.Buffered(1))
    tok_s = pl.BlockSpec((nc, gb * p2), lambda g: (0, g))
    mat_m = pl.BlockSpec((gb, lh, lh), lambda g: (g, 0, 0))
    mat_b = pl.BlockSpec((gb, lh, p2), lambda g: (g, 0, 0))
    mat_c = pl.BlockSpec((gb, p2, lh), lambda g: (g, 0, 0))
    return pl.pallas_call(
        body, grid=(groups // gb,), name=name,
        in_specs=[tok, tok, tok_s, mat_m, mat_b, mat_c,
                  pl.BlockSpec((8, gb * p2), lambda g: (0, g)), pl.BlockSpec((16, gb * p2), lambda g: (0, g))],
        out_specs=[tok, mat_m, mat_b, mat_c, pl.BlockSpec((1, gb * p2), lambda g: (0, g))],
        out_shape=[jax.ShapeDtypeStruct((n_rows, width), F32), jax.ShapeDtypeStruct(m.shape, F32),
                   jax.ShapeDtypeStruct(bm.shape, F32), jax.ShapeDtypeStruct(cm.shape, F32),
                   jax.ShapeDtypeStruct((1, groups * p2), F32)],
        scratch_shapes=[pltpu.VMEM((gb, nc, lh), F32), pltpu.VMEM((gb, nc, lh), F32),
                        pltpu.VMEM((nc, gb * p2), F32), pltpu.VMEM((nc, gb * p2), F32)],
        compiler_params=_params("parallel"),
    )(u, dy, sprev, m, bm, cm, apw, arows_rev)


def _split(x):
    hi = x.astype(BF16)
    return hi, (x - hi.astype(F32)).astype(BF16)


def _sb_more(kb, carries):
    top = jnp.max(carries[0])
    for c in carries[1:]:
        top = jnp.maximum(top, jnp.max(c))
    return (kb >= 0) & (top > SB_UNDERFLOW)


def _sb_fwd(name, qkv, heads):
    n_rows, dm3 = qkv.shape
    dm = dm3 // 3
    hd = dm // heads
    tq = _tile(n_rows // 2, 256)
    hb = min(SB_HEADS_FWD, heads)
    groups = heads // hb
    scale = hd ** -0.5

    def body(q_ref, k_ref, v_ref, o_ref):
        qi = pl.program_id(1)
        hs = range(hb)
        row = lax.broadcasted_iota(jnp.int32, (tq, tq), 0)
        col = lax.broadcasted_iota(jnp.int32, (tq, tq), 1)
        tri = (row > col).astype(BF16)
        causal = col < row
        qall = q_ref[...] * scale
        qb = [qall[:, h * hd:(h + 1) * hd] for h in hs]

        def block(kb, carries, accs, diagonal):
            ks = pl.ds(pl.multiple_of(kb * tq, tq), tq)
            kblk, vblk = k_ref[ks, :], v_ref[ks, :]
            z = [_dot(qb[h], kblk[:, h * hd:(h + 1) * hd], NT) for h in hs]
            sp = [_softplus(z[h]) for h in hs]
            lk = [-sp[h] for h in hs]
            if diagonal:
                lk = [jnp.where(causal, lk[h], 0.0) for h in hs]
            parts = [_split(lk[h]) for h in hs]
            r = [_dot(parts[h][0], tri) + _dot(parts[h][1], tri) for h in hs]
            a = [jnp.exp(z[h] - sp[h] + r[h] + carries[h]) for h in hs]
            if diagonal:
                a = [jnp.where(causal, a[h], 0.0) for h in hs]
            new_a = tuple(accs[h] + _dot(a[h].astype(BF16), vblk[:, h * hd:(h + 1) * hd]) for h in hs)
            new_c = tuple(carries[h] + jnp.sum(lk[h], axis=1, keepdims=True) for h in hs)
            return new_c, new_a

        zc = tuple(jnp.zeros((tq, 1), F32) for _ in hs)
        za = tuple(jnp.zeros((tq, hd), F32) for _ in hs)
        carries, accs = block(qi, zc, za, True)
        _, _, accs = lax.while_loop(lambda st: _sb_more(st[0], st[1]),
                                    lambda st: (st[0] - 1,) + block(st[0], st[1], st[2], False),
                                    (qi - 1, carries, accs))
        o_ref[...] = jnp.concatenate(accs, axis=1)

    lanes = hb * hd
    return pl.pallas_call(
        body, grid=(groups, n_rows // tq), name=name,
        in_specs=[pl.BlockSpec((tq, lanes), lambda g, i: (i, g)),
                  pl.BlockSpec((n_rows, lanes), lambda g, i: (0, groups + g)),
                  pl.BlockSpec((n_rows, lanes), lambda g, i: (0, 2 * groups + g))],
        out_specs=pl.BlockSpec((tq, lanes), lambda g, i: (i, g)),
        out_shape=jax.ShapeDtypeStruct((n_rows, dm), F32),
        compiler_params=_params("parallel", "arbitrary"))(qkv, qkv, qkv)


def _sb_bwd(name, qkv, o, do, heads):
    n_rows, dm3 = qkv.shape
    dm = dm3 // 3
    hd = dm // heads
    tq = _tile(n_rows // 2, 256)
    hb = min(SB_HEADS_BWD, heads)
    groups = heads // hb
    scale = hd ** -0.5

    def body(q_ref, k_ref, v_ref, o_ref, do_ref, dq_ref, dk_ref, dv_ref):
        qi = pl.program_id(1)

        @pl.when(qi == 0)
        def _():
            dk_ref[...] = jnp.zeros(dk_ref.shape, F32)
            dv_ref[...] = jnp.zeros(dv_ref.shape, F32)

        hs = range(hb)
        cols = [slice(h * hd, (h + 1) * hd) for h in hs]
        qall = q_ref[...] * scale
        doall = do_ref[...].astype(BF16)
        prod = doall.astype(F32) * o_ref[...]
        qb = [qall[:, cols[h]] for h in hs]
        dob16 = [doall[:, cols[h]] for h in hs]
        delta = [jnp.sum(prod[:, cols[h]], axis=1, keepdims=True) for h in hs]
        row = lax.broadcasted_iota(jnp.int32, (tq, tq), 0)
        col = lax.broadcasted_iota(jnp.int32, (tq, tq), 1)
        tri = (row > col).astype(BF16)
        tri_incl = (row >= col).astype(BF16)
        causal = col < row

        def block(kb, carries, pcarries, dqs, diagonal):
            ks = pl.ds(pl.multiple_of(kb * tq, tq), tq)
            kall, vall = k_ref[ks, :], v_ref[ks, :]
            kblk = [kall[:, cols[h]] for h in hs]
            vblk = [vall[:, cols[h]] for h in hs]
            z = [_dot(qb[h], kblk[h], NT) for h in hs]
            da = [_dot(dob16[h], vblk[h], NT) for h in hs]
            sp = [_softplus(z[h]) for h in hs]
            lk = [-sp[h] for h in hs]
            if diagonal:
                lk = [jnp.where(causal, lk[h], 0.0) for h in hs]
            lb = [z[h] - sp[h] for h in hs]
            parts = [_split(lk[h]) for h in hs]
            r = [_dot(parts[h][0], tri) + _dot(parts[h][1], tri) for h in hs]
            a = [jnp.exp(lb[h] + r[h] + carries[h]) for h in hs]
            if diagonal:
                a = [jnp.where(causal, a[h], 0.0) for h in hs]
            a16 = [a[h].astype(BF16) for h in hs]
            p = [da[h] * a16[h].astype(F32) for h in hs]
            pparts = [_split(p[h]) for h in hs]
            pc = [_dot(pparts[h][0], tri_incl) + _dot(pparts[h][1], tri_incl) for h in hs]
            beta = [jnp.exp(lb[h]) for h in hs]
            dz = [p[h] * (1.0 - beta[h]) - beta[h] * (delta[h] - pcarries[h] - pc[h]) for h in hs]
            if diagonal:
                dz = [jnp.where(causal, dz[h], 0.0) for h in hs]
            dz16 = [dz[h].astype(BF16) for h in hs]
            dk_ref[ks, :] += jnp.concatenate([_dot(dz16[h], qb[h], TN) for h in hs], axis=1)
            dv_ref[ks, :] += jnp.concatenate([_dot(a16[h], dob16[h], TN) for h in hs], axis=1)
            return (tuple(carries[h] + jnp.sum(lk[h], axis=1, keepdims=True) for h in hs),
                    tuple(pcarries[h] + jnp.sum(p[h], axis=1, keepdims=True) for h in hs),
                    tuple(dqs[h] + _dot(dz16[h], kblk[h]) for h in hs))

        zc = tuple(jnp.zeros((tq, 1), F32) for _ in hs)
        zq = tuple(jnp.zeros((tq, hd), F32) for _ in hs)
        st = block(qi, zc, zc, zq, True)
        st = lax.while_loop(lambda s: _sb_more(s[0], s[1]),
                            lambda s: (s[0] - 1,) + block(s[0], s[1], s[2], s[3], False), (qi - 1,) + st)
        dq_ref[...] = jnp.concatenate(st[3], axis=1) * scale

    lanes = hb * hd
    tile = pl.BlockSpec((tq, lanes), lambda g, i: (i, g))
    whole = pl.BlockSpec((n_rows, lanes), lambda g, i: (0, g))
    full = jax.ShapeDtypeStruct((n_rows, dm), F32)
    return pl.pallas_call(
        body, grid=(groups, n_rows // tq), name=name,
        in_specs=[tile, pl.BlockSpec((n_rows, lanes), lambda g, i: (0, groups + g)),
                  pl.BlockSpec((n_rows, lanes), lambda g, i: (0, 2 * groups + g)), tile, tile],
        out_specs=[tile, whole, whole], out_shape=[full, full, full],
        compiler_params=_params("parallel", "arbitrary"))(qkv, qkv, qkv, o, do)


def _block_diag(xb, w_ref_val, dims):
    nb = w_ref_val.shape[0]
    bw = xb.shape[1] // nb
    return jnp.concatenate([_dot(xb[:, n * bw:(n + 1) * bw], w_ref_val[n], dims) for n in range(nb)], axis=1)


def _lru_gates_fwd(name, gx, conv_w, conv_b, wa, ba, wx, bx, lam):
    n_rows, w2 = gx.shape
    w = w2 // 2
    tm = _tile(n_rows, 256)

    def fn(i, nt, br, prev, cw, cb, wa_v, ba_v, wx_v, bx_v, lam_v):
        xc = cb + sum(cw[k:k + 1] * _shift_down(br, prev, LRU_CONV - 1 - k, i == 0) for k in range(LRU_CONV))
        xb = xc.astype(BF16)
        r = _sigmoid(_block_diag(xb, wa_v, NN) + ba_v)
        ig = _sigmoid(_block_diag(xb, wx_v, NN) + bx_v)
        log_a = (-LRU_C * r) * _softplus(-lam_v)
        a = jnp.exp(log_a)
        gated = (ig * xc) * _one_minus_a2_sqrt(log_a)
        return (xc, r, ig, a, gated), ()

    return _rowwise(name, fn, [('t', gx, w, 1), ('p', gx, w, 1), ('b', conv_w), ('b', conv_b), ('b', wa), ('b', ba),
                               ('b', wx), ('b', bx), ('b', lam)], [(w, F32)] * 5, [], n_rows, tm)


def _lru_scan_fwd(name, a, gated, gx):
    n_rows, w = a.shape
    tm = _tile(n_rows, 256)

    def body(a_ref, x_ref, bg_ref, hs_ref, y_ref, carry_ref):
        @pl.when(pl.program_id(0) == 0)
        def _():
            carry_ref[...] = jnp.zeros(carry_ref.shape, F32)
        carry_ref[...] = _scan_tile(a_ref, x_ref, hs_ref, carry_ref[...], False, tm)
        y_ref[...] = (_gelu(bg_ref[...]) * hs_ref[...]).astype(BF16)

    tok = pl.BlockSpec((tm, w), lambda i: (i, 0))
    return pl.pallas_call(body, grid=(n_rows // tm,), in_specs=[tok, tok, tok], out_specs=[tok, tok], name=name,
                          out_shape=[jax.ShapeDtypeStruct((n_rows, w), F32), jax.ShapeDtypeStruct((n_rows, w), BF16)],
                          scratch_shapes=[pltpu.VMEM((1, w), F32)], compiler_params=_params("arbitrary"))(a, gated, gx)


def _lru_scan_bwd(name, a, dy, gx):
    n_rows, w = a.shape
    tm = _tile(n_rows, 256)
    nt = n_rows // tm
    per8 = tm // SUBLANES

    def body(a_ref, an_ref, dy_ref, bg_ref, lam_ref, carry_ref, aup_ref, dhs_ref):
        i = pl.program_id(0)

        @pl.when(i == 0)
        def _():
            carry_ref[...] = jnp.zeros(carry_ref.shape, F32)
        aup_ref[...] = _shift_up(a_ref[...], an_ref[...], 1, i == 0)
        dhs_ref[...] = dy_ref[...] * _gelu(bg_ref[...])
        carry_ref[...] = _scan_tile(aup_ref, dhs_ref, lam_ref, carry_ref[...], True, tm)

    tok = pl.BlockSpec((tm, w), lambda i: (nt - 1 - i, 0))
    nxt = pl.BlockSpec((SUBLANES, w), lambda i: (jnp.minimum((nt - i) * per8, n_rows // SUBLANES - 1), 0))
    return pl.pallas_call(body, grid=(nt,), in_specs=[tok, nxt, tok, tok], out_specs=tok, name=name,
                          out_shape=jax.ShapeDtypeStruct((n_rows, w), F32),
                          scratch_shapes=[pltpu.VMEM((1, w), F32), pltpu.VMEM((tm, w), F32), pltpu.VMEM((tm, w), F32)],
                          compiler_params=_params("arbitrary"))(a, a, dy, gx)


def _lru_gates_bwd(name, lam_t, hs, xc, r, ig, a, wa, wx, lam):
    n_rows, w = xc.shape
    nb, bw, _ = wa.shape
    tm = _tile(n_rows, 256)

    def fn(i, nt, lt, hs_v, hs_prev, xc_v, r_v, ig_v, a_v, wa_v, wx_v, lam_v):
        sp = _softplus(-lam_v)
        log_a = (-LRU_C * r_v) * sp
        mult = _one_minus_a2_sqrt(log_a)
        d_a = lt * _shift_down(hs_v, hs_prev, 1, i == 0)
        d_ig = lt * xc_v * mult
        d_mult = lt * ig_v * xc_v
        d_log_a = d_a * a_v - d_mult * (a_v * a_v) / mult
        d_ra = d_log_a * (-LRU_C * sp) * r_v * (1.0 - r_v)
        d_ia = d_ig * ig_v * (1.0 - ig_v)
        d_ra16, d_ia16, xb = d_ra.astype(BF16), d_ia.astype(BF16), xc_v.astype(BF16)
        dxc = lt * ig_v * mult + _block_diag(d_ra16, wa_v, NT) + _block_diag(d_ia16, wx_v, NT)
        dwa = jnp.concatenate([_dot(xb[:, n * bw:(n + 1) * bw], d_ra16[:, n * bw:(n + 1) * bw], TN)
                               for n in range(nb)], axis=0)
        dwx = jnp.concatenate([_dot(xb[:, n * bw:(n + 1) * bw], d_ia16[:, n * bw:(n + 1) * bw], TN)
                               for n in range(nb)], axis=0)
        col = lambda t: jnp.sum(t, axis=0, keepdims=True)
        return (dxc,), (dwa, dwx, col(d_ra), col(d_ia), col(d_log_a * (-LRU_C * r_v)))

    tiled = lambda arr: ('t', arr, w, 0)
    return _rowwise(name, fn, [tiled(lam_t), tiled(hs), ('p', hs, w, 0), tiled(xc), tiled(r), tiled(ig), tiled(a),
                               ('b', wa), ('b', wx), ('b', lam)],
                    [(w, F32)], [(nb * bw, bw), (nb * bw, bw), (1, w), (1, w), (1, w)], n_rows, tm)


def _lru_conv_bwd(name, dxc, gx, dy, hs, conv_w):
    n_rows, w = dxc.shape
    tm = _tile(n_rows, 256)

    def fn(i, nt, dxc_v, dxc_next, bg, br, br_prev, dy_v, hs_v, cw):
        dbr = sum(cw[k:k + 1] * _shift_up(dxc_v, dxc_next, LRU_CONV - 1 - k, i == nt - 1) for k in range(LRU_CONV))
        dbg = dy_v * hs_v * _gelu_grad(bg)
        dcw = [jnp.sum(dxc_v * _shift_down(br, br_prev, LRU_CONV - 1 - k, i == 0), axis=0, keepdims=True)
               for k in range(LRU_CONV)]
        dcw = jnp.concatenate(dcw + [jnp.zeros((SUBLANES - LRU_CONV, w), F32)], axis=0)
        return (jnp.concatenate([dbg, dbr], axis=1),), (dcw, jnp.sum(dxc_v, axis=0, keepdims=True))

    return _rowwise(name, fn, [('t', dxc, w, 0), ('n', dxc, w, 0), ('t', gx, w, 0), ('t', gx, w, 1), ('p', gx, w, 1),
                               ('t', dy, w, 0), ('t', hs, w, 0), ('b', conv_w)],
                    [(2 * w, BF16)], [(SUBLANES, w), (1, w)], n_rows, tm)


def _loss_head(name, h, gain, target):
    n_rows, dm = h.shape
    tm = _tile(n_rows, 512)

    def fn(i, nt, hv, tv, g):
        r, xhat = _rms(hv)
        err = xhat * g - tv
        dy = err * (1.0 / dm)
        return ((_rms_bwd(dy, xhat, r, g),),
                (jnp.sum(err * err, axis=0, keepdims=True), jnp.sum(dy * xhat, axis=0, keepdims=True)))

    return _rowwise(name, fn, [('t', h, dm, 0), ('t', target, dm, 0), ('b', gain)], [(dm, F32)], [(1, dm), (1, dm)],
                    n_rows, tm)


def _adamw(name, gparts, w, m, v):
    n_parts, n_rows, cols = gparts.shape
    tr = n_rows
    for cand in (256, 128, 64, 32, 16, 8):
        if n_rows % cand == 0:
            tr = cand
            break
    c1 = 1.0 - ADAM_B1 ** ADAM_STEP
    c2 = 1.0 - ADAM_B2 ** ADAM_STEP

    def body(gp_ref, w_ref, m_ref, v_ref, g_ref, d_ref, nm_ref, nv_ref):
        g = gp_ref[0].astype(F32)
        for p in range(1, n_parts):
            g = g + gp_ref[p].astype(F32)
        m_new = ADAM_B1 * m_ref[...] + (1.0 - ADAM_B1) * g
        v_new = ADAM_B2 * v_ref[...] + (1.0 - ADAM_B2) * (g * g)
        m_hat = m_new / c1
        v_hat = v_new / c2
        g_ref[...] = g
        d_ref[...] = -ADAM_LR * (m_hat / (jnp.sqrt(v_hat) + ADAM_EPS) + ADAM_WD * w_ref[...])
        nm_ref[...] = m_new
        nv_ref[...] = v_new

    blk = pl.BlockSpec((tr, cols), lambda i: (i, 0))
    shp = jax.ShapeDtypeStruct((n_rows, cols), F32)
    return pl.pallas_call(body, grid=(n_rows // tr,), name=name,
                          in_specs=[pl.BlockSpec((n_parts, tr, cols), lambda i: (0, i, 0)), blk, blk, blk],
                          out_specs=[blk, blk, blk, blk], out_shape=[shp, shp, shp, shp],
                          compiler_params=_params("parallel"))(gparts, w, m, v)


def _adamw_layers(name, recvs, w, m, v):
    n_layers, n_rows, cols = w.shape
    n_parts = recvs[0].shape[0]
    tr = max(t for t in range(16, ADAMW_LAYER_ROWS + 1, 16) if n_rows % t == 0)
    c1 = 1.0 - ADAM_B1 ** ADAM_STEP
    c2 = 1.0 - ADAM_B2 ** ADAM_STEP

    def body(*refs):
        gp_refs = refs[:n_layers]
        w_ref, m_ref, v_ref, g_ref, d_ref, nm_ref, nv_ref = refs[n_layers:]
        layer = pl.program_id(0)
        for k in range(n_layers):
            @pl.when(layer == k)
            def _(k=k):
                g = gp_refs[k][0].astype(F32)
                for p in range(1, n_parts):
                    g = g + gp_refs[k][p].astype(F32)
                m_new = ADAM_B1 * m_ref[...] + (1.0 - ADAM_B1) * g
                v_new = ADAM_B2 * v_ref[...] + (1.0 - ADAM_B2) * (g * g)
                g_ref[...] = g
                d_ref[...] = -ADAM_LR * ((m_new / c1) / (jnp.sqrt(v_new / c2) + ADAM_EPS) + ADAM_WD * w_ref[...])
                nm_ref[...] = m_new
                nv_ref[...] = v_new

    blk = pl.BlockSpec((None, tr, cols), lambda l, i: (l, i, 0))
    shp = jax.ShapeDtypeStruct((n_layers, n_rows, cols), F32)
    gp_specs = [pl.BlockSpec((n_parts, tr, cols), lambda l, i, k=k: (0, jnp.where(l == k, i, 0), 0))
                for k in range(n_layers)]
    return pl.pallas_call(body, grid=(n_layers, n_rows // tr), name=name, in_specs=gp_specs + [blk, blk, blk],
                          out_specs=[blk, blk, blk, blk], out_shape=[shp, shp, shp, shp],
                          compiler_params=_params("arbitrary", "arbitrary"))(*recvs, w, m, v)


def _pack_rows(arrays, cols, lead=0):
    flat = [a.reshape(a.shape[:lead] + (-1,)) for a in arrays]
    cat = jnp.concatenate(flat, axis=lead) if len(flat) > 1 else flat[0]
    n = cat.shape[lead]
    pad = (-n) % (cols * PACK_ROWS)
    if pad:
        cat = jnp.pad(cat, [(0, 0)] * lead + [(0, pad)])
    return cat.reshape(cat.shape[:lead] + (-1, cols))


def _unpack_rows(packed, shapes, lead=0):
    flat = packed.reshape(packed.shape[:lead] + (-1,))
    out, off = [], 0
    for s in shapes:
        n = math.prod(s)
        out.append(lax.slice_in_dim(flat, off, off + n, axis=lead).reshape(flat.shape[:lead] + tuple(s)))
        off += n
    return out


def kernel(x, ffn1_norm, ffn1_w_in, ffn1_w_out, mix_norm, ffn2_norm, ffn2_w_in, ffn2_w_out, final_norm, s5_w_in, s5_lam_re, s5_lam_im, s5_log_dt, s5_b_re, s5_b_im, s5_c_re, s5_c_im, s5_d, s5_w_out, sb_w_qkv, sb_w_out, lru_w_in, lru_conv_w, lru_conv_b, lru_w_a, lru_b_a, lru_w_x, lru_b_x, lru_lambda, lru_w_out, loss_target, m_ffn1_norm, m_ffn1_w_in, m_ffn1_w_out, m_mix_norm, m_ffn2_norm, m_ffn2_w_in, m_ffn2_w_out, m_final_norm, m_s5_w_in, m_s5_lam_re, m_s5_lam_im, m_s5_log_dt, m_s5_b_re, m_s5_b_im, m_s5_c_re, m_s5_c_im, m_s5_d, m_s5_w_out, m_sb_w_qkv, m_sb_w_out, m_lru_w_in, m_lru_conv_w, m_lru_conv_b, m_lru_w_a, m_lru_b_a, m_lru_w_x, m_lru_b_x, m_lru_lambda, m_lru_w_out, v_ffn1_norm, v_ffn1_w_in, v_ffn1_w_out, v_mix_norm, v_ffn2_norm, v_ffn2_w_in, v_ffn2_w_out, v_final_norm, v_s5_w_in, v_s5_lam_re, v_s5_lam_im, v_s5_log_dt, v_s5_b_re, v_s5_b_im, v_s5_c_re, v_s5_c_im, v_s5_d, v_s5_w_out, v_sb_w_qkv, v_sb_w_out, v_lru_w_in, v_lru_conv_w, v_lru_conv_b, v_lru_w_a, v_lru_b_a, v_lru_w_x, v_lru_b_x, v_lru_lambda, v_lru_w_out):
    local = dict(locals())
    W = {n: local[n] for n in WEIGHTS}
    M = {n: local["m_" + n] for n in WEIGHTS}
    V = {n: local["v_" + n] for n in WEIGHTS}

    h0 = x[0]
    target = loss_target[0]
    n_rows, dm = h0.shape
    depth = ffn1_norm.shape[0]

    ffn_seq = [(tag, layer) for layer in range(depth) for tag in ("ffn1", "ffn2")]

    def ffn_shards(tag, layer):
        return W[tag + "_w_in"][layer].astype(BF16), W[tag + "_w_out"][layer].astype(BF16)

    def ffn_views(wi, wo):
        return wi.reshape((2, N_DEV // 2) + wi.shape[1:]), wo.reshape((N_DEV // 2, 2) + wo.shape[1:])

    first_in, first_out = ffn_shards(*ffn_seq[0])
    ffn_w = {ffn_seq[0]: ffn_views(_all_gather("ag_first_w_in", first_in), _all_gather("ag_first_w_out", first_out))}

    ffn_saved = {}

    def ffn_forward(pos, h_in):
        tag, layer = ffn_seq[pos]
        comms = [("gather", a) for a in ffn_shards(*ffn_seq[pos + 1])] if pos + 1 < len(ffn_seq) else []
        res = _ffn_fwd("%s_fwd_%d" % (tag, layer), h_in, W[tag + "_norm"][layer:layer + 1], *ffn_w[ffn_seq[pos]],
                       comms=comms)
        ffn_saved[ffn_seq[pos]] = (res[1], res[2])
        if comms:
            ffn_w[ffn_seq[pos + 1]] = ffn_views(res[3], res[4])
        return res[0]

    mix_shapes = [W[n].shape for n in MIXER_BIG]
    mix_g = _all_gather("ag_mixers", _pack_rows([W[n].astype(BF16) for n in MIXER_BIG], dm))
    full = {n: _unshard(a, SHARD_AXIS[n]) for n, a in zip(MIXER_BIG, _unpack_rows(mix_g, mix_shapes, lead=1))}
    small_shapes = [W[n].shape for n in SMALL_SHARDED]
    small_g = _all_gather("ag_small", _pack_rows([W[n] for n in SMALL_SHARDED], 128))
    full.update({n: _unshard(a, SHARD_AXIS[n])
                 for n, a in zip(SMALL_SHARDED, _unpack_rows(small_g, small_shapes, lead=1))})

    n_s5 = s5_w_in.shape[0]
    s5_groups = s5_lam_re.shape[1]
    heads = dm // SB_HEAD_DIM

    grads = {}
    saved = []
    h = h0

    for layer in range(depth):
        kind, j = layer % 3, layer // 3
        rec = {"h0": h}
        h = ffn_forward(2 * layer, h)
        rec["h1"] = h
        gain = mix_norm[layer:layer + 1]
        if kind == 0:
            (u,) = _mm_fwd("s5_in_%d" % layer, h, full["s5_w_in"][j], F32, gain=gain)
            pars = (s5_lam_re[j], s5_lam_im[j], s5_log_dt[j], s5_b_re[j], s5_b_im[j], s5_c_re[j], s5_c_im[j])
            mats, mats_vjp = jax.vjp(_s5_mats, *pars)
            apw, ar_fwd, ar_rev = _s5_powers(*pars[:3])
            ys, sprev = _s5_fwd("s5_core_%d" % layer, u, *mats[:3], apw, ar_fwd)
            d_skip = full["s5_d"][j:j + 1]
            (z,) = _rowwise("s5_gelu_%d" % layer, lambda i, nt, ys_v, u_v, d_v: ((_gelu(ys_v + d_v * u_v),), ()),
                            [('t', ys, dm, 0), ('t', u, dm, 0), ('b', d_skip)], [(dm, BF16)], [], n_rows,
                            _tile(n_rows, 512))
            h, vg = _mm_fwd("s5_out_%d" % layer, z, full["s5_w_out"][j], F32, resid=h, glu=True)
            rec.update(u=u, ys=ys, sprev=sprev, z=z, vg=vg, mats=mats, mats_vjp=mats_vjp, apw=apw,
                       ar_rev=ar_rev, d_skip=d_skip)
        elif kind == 1:
            (qkv,) = _mm_fwd("sb_in_%d" % layer, h, full["sb_w_qkv"][j], BF16, gain=gain)
            o = _sb_fwd("sb_attn_%d" % layer, qkv, heads)
            (h,) = _mm_fwd("sb_out_%d" % layer, o, full["sb_w_out"][j], F32, resid=h)
            rec.update(qkv=qkv, o=o)
        else:
            (gx,) = _mm_fwd("lru_in_%d" % layer, h, full["lru_w_in"][j], F32, gain=gain)
            wa, wx = full["lru_w_a"][j], full["lru_w_x"][j]
            ba, bx = full["lru_b_a"][j].reshape(1, dm), full["lru_b_x"][j].reshape(1, dm)
            lam_row = full["lru_lambda"][j:j + 1]
            xc, r, ig, a, gated = _lru_gates_fwd("lru_gates_%d" % layer, gx, full["lru_conv_w"][j],
                                                 full["lru_conv_b"][j:j + 1], wa, ba, wx, bx, lam_row)
            hs, y = _lru_scan_fwd("lru_scan_%d" % layer, a, gated, gx)
            (h,) = _mm_fwd("lru_out_%d" % layer, y, full["lru_w_out"][j], F32, resid=h)
            rec.update(gx=gx, xc=xc, r=r, ig=ig, a=a, hs=hs, y=y, wa=wa, wx=wx, lam_row=lam_row)
        rec["h2"] = h
        h = ffn_forward(2 * layer + 1, h)
        saved.append(rec)

    dh, err2, dgf = _loss_head("loss_head", h, final_norm.reshape(1, dm), target)
    loss = lax.psum(0.5 / dm * jnp.sum(err2), ("x", "y", "c"))
    grads["final_norm"] = dgf.reshape(final_norm.shape)

    per_layer = {n: [None] * depth for n in ("ffn1_norm", "mix_norm", "ffn2_norm")}
    mixer_grads = {}
    recv_ffn = {}
    pending = []

    def ffn_backward(tag, layer, x_in, dh_in, more=None):
        comms = {0: [("exchange", pending[1])], 1: [("exchange", pending[2])]} if pending else {}
        comms.update(more or {})
        dx, dwi, dwo, dg, extra = _ffn_bwd("%s_bwd_%d" % (tag, layer), x_in, dh_in, W[tag + "_norm"][layer:layer + 1],
                                            *ffn_w[(tag, layer)], *ffn_saved[(tag, layer)], comms_by_block=comms)
        if pending:
            recv_ffn[pending[0]] = tuple(extra[:2])
            extra = extra[2:]
        pending[:] = [(tag, layer), dwi.reshape((N_DEV,) + dwi.shape[2:]).astype(BF16),
                      dwo.reshape(N_DEV, -1, dm).astype(BF16)]
        per_layer[tag + "_norm"][layer] = dg
        return dx, extra

    def put(name, j, value, count):
        mixer_grads.setdefault(name, [None] * count)[j] = value

    for layer in reversed(range(depth)):
        kind, j = layer % 3, layer // 3
        rec = saved[layer]
        dh, _ = ffn_backward("ffn2", layer, rec["h2"], dh)
        gain = mix_norm[layer:layer + 1]
        if kind == 0:
            dvg, = _rowwise("s5_glu_bwd_%d" % layer,
                            lambda i, nt, d_v, vg_v: ((jnp.concatenate(
                                [d_v * _sigmoid(vg_v[:, dm:]),
                                 d_v * vg_v[:, :dm] * _sigmoid(vg_v[:, dm:]) * (1.0 - _sigmoid(vg_v[:, dm:]))],
                                axis=1),), ()),
                            [('t', dh, dm, 0), ('t', rec["vg"], 2 * dm, 0)], [(2 * dm, BF16)], [], n_rows,
                            _tile(n_rows, 256))
            dz, dw_out = _mm_bwd("s5_out_bwd_%d" % layer, rec["z"], dvg, full["s5_w_out"][j])

            def gelu_bwd(i, nt, dz_v, ys_v, u_v, d_v):
                dy_v = dz_v * _gelu_grad(ys_v + d_v * u_v)
                return (dy_v,), (jnp.sum(dy_v * u_v, axis=0, keepdims=True),)

            dys, dd = _rowwise("s5_gelu_bwd_%d" % layer, gelu_bwd,
                               [('t', dz, dm, 0), ('t', rec["ys"], dm, 0), ('t', rec["u"], dm, 0),
                                ('b', rec["d_skip"])], [(dm, F32)], [(1, dm)], n_rows, _tile(n_rows, 512))
            m_, bm_, cm_, _ = rec["mats"]
            du_core, dm_m, dm_b, dm_c, d_a = _s5_bwd("s5_core_bwd_%d" % layer, rec["u"], dys, rec["sprev"],
                                                    m_, bm_, cm_, rec["apw"], rec["ar_rev"])
            dpars = rec["mats_vjp"]((dm_m, dm_b, dm_c, d_a.reshape(s5_groups, -1)))
            for nme, val in zip(("s5_lam_re", "s5_lam_im", "s5_log_dt", "s5_b_re", "s5_b_im", "s5_c_re", "s5_c_im"),
                                dpars):
                put(nme, j, val, n_s5)
            (du,) = _rowwise("s5_du_%d" % layer, lambda i, nt, a_v, dy_v, d_v: ((a_v + dy_v * d_v,), ()),
                             [('t', du_core, dm, 0), ('t', dys, dm, 0), ('b', rec["d_skip"])],
                             [(dm, BF16)], [], n_rows, _tile(n_rows, 512))
            dh, dw_in, dgm = _mm_bwd("s5_in_bwd_%d" % layer, rec["h1"], du, full["s5_w_in"][j], gain=gain, dres=dh)
            put("s5_d", j, dd[0], n_s5)
            put("s5_w_out", j, dw_out, n_s5)
            put("s5_w_in", j, dw_in, n_s5)
        elif kind == 1:
            do, dw_out = _mm_bwd("sb_out_bwd_%d" % layer, rec["o"], dh, full["sb_w_out"][j])
            dq, dk, dv = _sb_bwd("sb_attn_bwd_%d" % layer, rec["qkv"], rec["o"], do, heads)
            dqkv = jnp.concatenate([dq, dk, dv], axis=1).astype(BF16)
            dh, dw_in, dgm = _mm_bwd("sb_in_bwd_%d" % layer, rec["h1"], dqkv, full["sb_w_qkv"][j], gain=gain, dres=dh)
            put("sb_w_out", j, dw_out, 1)
            put("sb_w_qkv", j, dw_in, 1)
        else:
            dy, dw_out = _mm_bwd("lru_out_bwd_%d" % layer, rec["y"], dh, full["lru_w_out"][j])
            lam_t = _lru_scan_bwd("lru_scan_bwd_%d" % layer, rec["a"], dy, rec["gx"])
            dxc, dwa, dwx, dba, dbx, dsp = _lru_gates_bwd("lru_gates_bwd_%d" % layer, lam_t, rec["hs"], rec["xc"],
                                                          rec["r"], rec["ig"], rec["a"], rec["wa"], rec["wx"],
                                                          rec["lam_row"])
            dgx, dcw, dcb = _lru_conv_bwd("lru_conv_bwd_%d" % layer, dxc, rec["gx"], dy, rec["hs"],
                                          full["lru_conv_w"][j])
            dh, dw_in, dgm = _mm_bwd("lru_in_bwd_%d" % layer, rec["h1"], dgx, full["lru_w_in"][j], gain=gain, dres=dh)
            nb = rec["wa"].shape[0]
            put("lru_w_out", j, dw_out, 1)
            put("lru_w_in", j, dw_in, 1)
            put("lru_w_a", j, dwa.reshape(rec["wa"].shape), 1)
            put("lru_w_x", j, dwx.reshape(rec["wx"].shape), 1)
            put("lru_b_a", j, dba.reshape(nb, -1), 1)
            put("lru_b_x", j, dbx.reshape(nb, -1), 1)
            put("lru_conv_w", j, dcw[:LRU_CONV], 1)
            put("lru_conv_b", j, dcb[0], 1)
            put("lru_lambda", j, (dsp * -_sigmoid(-rec["lam_row"]))[0], 1)
        per_layer["mix_norm"][layer] = dgm
        more = None
        if layer == 0:
            for n, parts in mixer_grads.items():
                grads[n] = jnp.stack(parts)
            send = _pack_rows([_shard_blocks(grads[n], SHARD_AXIS[n]).astype(BF16) for n in MIXER_BIG], dm, lead=1)
            half = send.shape[1] // 2
            more = {2: [("exchange", send[:, :half])], 3: [("exchange", send[:, half:])]}
        dh, got = ffn_backward("ffn1", layer, rec["h0"], dh, more)
        if layer == 0:
            recv_mixers = jnp.concatenate(got, axis=1)

    grad_x = dh[None]
    for n in ("ffn1_norm", "mix_norm", "ffn2_norm"):
        grads[n] = jnp.concatenate(per_layer[n], axis=0)

    out_g, out_d, out_m, out_v = {}, {}, {}, {}

    def finish(names, res, shapes):
        for n, g_, d_, m_, v_ in zip(names, *[_unpack_rows(t, shapes) for t in res]):
            out_g[n], out_d[n], out_m[n], out_v[n] = g_, d_, m_, v_

    recv_ffn[pending[0]] = (_exchange("xchg_last_w_in", pending[1]), _exchange("xchg_last_w_out", pending[2]))
    for tag in ("ffn1", "ffn2"):
        for which, n in enumerate((tag + "_w_in", tag + "_w_out")):
            out_g[n], out_d[n], out_m[n], out_v[n] = _adamw_layers(
                "adamw_" + n, [recv_ffn[(tag, layer)][which] for layer in range(depth)], W[n], M[n], V[n])

    finish(MIXER_BIG, _adamw("adamw_mixers", recv_mixers,
                             *[_pack_rows([t[n] for n in MIXER_BIG], dm) for t in (W, M, V)]), mix_shapes)

    small_names = REPLICATED + SMALL_SHARDED
    small_full_shapes = [grads[n].shape for n in small_names]
    parts = _all_gather("ag_small_grads", _pack_rows([grads[n] for n in small_names], 128))
    zero = jnp.zeros(parts.shape[1:], F32)
    summed = _adamw("sum_small_grads", parts, zero, zero, zero)[0]
    small_sum = dict(zip(small_names, _unpack_rows(summed, small_full_shapes)))
    me = 4 * lax.axis_index("x") + 2 * lax.axis_index("y") + lax.axis_index("c")
    rep_shapes = [W[n].shape for n in REPLICATED]
    g_rep = _pack_rows([small_sum[n] for n in REPLICATED], 128)[None]
    finish(REPLICATED, _adamw("adamw_replicated", g_rep, *[_pack_rows([t[n] for n in REPLICATED], 128)
                                                           for t in (W, M, V)]), rep_shapes)
    g_loc = []
    for n in SMALL_SHARDED:
        ax = SHARD_AXIS[n]
        size = W[n].shape[ax]
        g_loc.append(lax.dynamic_slice_in_dim(small_sum[n], me * size, size, axis=ax))
    finish(SMALL_SHARDED, _adamw("adamw_small", _pack_rows(g_loc, 128)[None],
                                 *[_pack_rows([t[n] for n in SMALL_SHARDED], 128) for t in (W, M, V)]), small_shapes)

    return (loss, grad_x, *[out_g[n] for n in WEIGHTS], *[out_d[n] for n in WEIGHTS],
            *[out_m[n] for n in WEIGHTS], *[out_v[n] for n in WEIGHTS])
```

```python
import functools
import math

import jax
import jax.numpy as jnp
from jax import lax
from jax.experimental import pallas as pl
from jax.experimental.pallas import tpu as pltpu

F32 = jnp.float32
BF16 = jnp.bfloat16
HI = lax.Precision.HIGHEST
S5_PREC = lax.Precision.HIGH
MESH = pl.DeviceIdType.MESH

N_DEV = 8
RMS_EPS = 1e-6
S5_GROUP = 16
S5_CHUNK = 16
S5_OCTET = 128 // S5_GROUP
S5_REGROUP_ROWS = 32
FFN_ROW_PARTS = 2
S5_SCAN_UNROLL = 8
SB_HEAD_DIM = 64
SB_UNDERFLOW = -104.0
SB_HEADS_FWD = 4
SB_HEADS_BWD = 4
LRU_CONV = 4
LRU_C = 8.0
ADAM_LR, ADAM_B1, ADAM_B2, ADAM_EPS, ADAM_WD, ADAM_STEP = 0.001, 0.9, 0.999, 1e-08, 0.01, 10
VMEM_LIMIT_BYTES = 56 * 1024 * 1024
SUBLANES = 8
PACK_ROWS = 256
ADAMW_LAYER_ROWS = 192

NN = (((1,), (0,)), ((), ()))
NT = (((1,), (1,)), ((), ()))
TN = (((0,), (0,)), ((), ()))

SHARD_AXIS = dict(
    ffn1_w_in=2, ffn1_w_out=1, ffn2_w_in=2, ffn2_w_out=1, s5_w_in=1, s5_d=1, s5_w_out=2, sb_w_qkv=2, sb_w_out=1,
    lru_w_in=2, lru_conv_w=2, lru_conv_b=1, lru_w_a=2, lru_b_a=2, lru_w_x=2, lru_b_x=2, lru_lambda=1, lru_w_out=1)
MIXER_BIG = ("s5_w_in", "s5_w_out", "sb_w_qkv", "sb_w_out", "lru_w_in", "lru_w_a", "lru_w_x", "lru_w_out")
MIXER_EARLY = MIXER_BIG[:2]
MIXER_LATE = MIXER_BIG[2:]
SMALL_SHARDED = ("s5_d", "lru_conv_w", "lru_conv_b", "lru_b_a", "lru_b_x", "lru_lambda")
REPLICATED = ("ffn1_norm", "mix_norm", "ffn2_norm", "final_norm", "s5_lam_re", "s5_lam_im", "s5_log_dt",
              "s5_b_re", "s5_b_im", "s5_c_re", "s5_c_im")
WEIGHTS = ("ffn1_norm", "ffn1_w_in", "ffn1_w_out", "mix_norm", "ffn2_norm", "ffn2_w_in", "ffn2_w_out", "final_norm",
           "s5_w_in", "s5_lam_re", "s5_lam_im", "s5_log_dt", "s5_b_re", "s5_b_im", "s5_c_re", "s5_c_im", "s5_d",
           "s5_w_out", "sb_w_qkv", "sb_w_out", "lru_w_in", "lru_conv_w", "lru_conv_b", "lru_w_a", "lru_b_a",
           "lru_w_x", "lru_b_x", "lru_lambda", "lru_w_out")


def _dot(a, b, dims=NN, prec=None):
    return lax.dot_general(a, b, dims, precision=prec, preferred_element_type=F32)


def _params(*sem):
    return pltpu.CompilerParams(dimension_semantics=sem, vmem_limit_bytes=VMEM_LIMIT_BYTES)


def _tile(n, pref):
    return min(pref, n)


def _sigmoid(x):
    return jax.nn.sigmoid(x)


def _softplus(x):
    return jnp.maximum(x, 0.0) + jnp.log(1.0 + jnp.exp(-jnp.abs(x)))


_GELU_C = math.sqrt(2.0 / math.pi)


def _gelu(x):
    return 0.5 * x * (1.0 + jnp.tanh(_GELU_C * (x + 0.044715 * x * x * x)))


def _gelu_grad(x):
    t = jnp.tanh(_GELU_C * (x + 0.044715 * x * x * x))
    return 0.5 * (1.0 + t) + 0.5 * x * (1.0 - t * t) * _GELU_C * (1.0 + 3.0 * 0.044715 * x * x)


def _rms(x):
    r = lax.rsqrt(jnp.mean(x * x, axis=1, keepdims=True) + RMS_EPS)
    return r, x * r


def _rms_bwd(dhn, xhat, r, g):
    dxhat = dhn * g
    return r * (dxhat - xhat * jnp.mean(dxhat * xhat, axis=1, keepdims=True))


def _one_minus_a2_sqrt(log_a):
    t = jnp.tanh(log_a)
    return jnp.sqrt(-2.0 * t / (1.0 - t))


def _shift_down(cur, prev8, k, first):
    if k == 0:
        return cur
    row8 = lax.broadcasted_iota(jnp.int32, prev8.shape, 0)
    rolled = pltpu.roll(cur, k, 0)
    edge = jnp.where(first, 0.0, pltpu.roll(prev8, k, 0))
    top = jnp.where(row8 < k, edge, rolled[0:SUBLANES])
    return jnp.concatenate([top, rolled[SUBLANES:]], axis=0)


def _shift_up(cur, next8, k, last):
    if k == 0:
        return cur
    tm = cur.shape[0]
    row8 = lax.broadcasted_iota(jnp.int32, next8.shape, 0)
    rolled = pltpu.roll(cur, tm - k, 0)
    edge = jnp.where(last, 0.0, pltpu.roll(next8, SUBLANES - k, 0))
    bottom = jnp.where(row8 >= SUBLANES - k, edge, rolled[tm - SUBLANES:tm])
    return jnp.concatenate([rolled[:tm - SUBLANES], bottom], axis=0)


def _rowwise(name, fn, ins, out_tiled, out_acc, n_rows, tm, reverse=False):
    nt = n_rows // tm
    per8 = tm // SUBLANES
    n8 = n_rows // SUBLANES
    n_in, n_ot = len(ins), len(out_tiled)

    def pos(i):
        return nt - 1 - i if reverse else i

    in_specs, args = [], []
    for spec in ins:
        kind, arr = spec[0], spec[1]
        args.append(arr)
        if kind == 'b':
            in_specs.append(pl.BlockSpec(arr.shape, lambda i, nd=arr.ndim: (0,) * nd))
        elif kind == 't':
            in_specs.append(pl.BlockSpec((tm, spec[2]), lambda i, cb=spec[3]: (pos(i), cb)))
        elif kind == 'p':
            in_specs.append(pl.BlockSpec((SUBLANES, spec[2]),
                                         lambda i, cb=spec[3]: (jnp.maximum(pos(i) * per8 - 1, 0), cb)))
        else:
            in_specs.append(pl.BlockSpec((SUBLANES, spec[2]),
                                         lambda i, cb=spec[3]: (jnp.minimum((pos(i) + 1) * per8, n8 - 1), cb)))

    def body(*refs):
        i = pl.program_id(0)
        outs = refs[n_in:]
        touts, aouts = fn(pos(i), nt, *[r[...] for r in refs[:n_in]])
        for r, v in zip(outs[:n_ot], touts):
            r[...] = v.astype(r.dtype)
        if out_acc:
            @pl.when(i == 0)
            def _():
                for r in outs[n_ot:]:
                    r[...] = jnp.zeros(r.shape, r.dtype)
            for r, v in zip(outs[n_ot:], aouts):
                r[...] += v

    out_specs = [pl.BlockSpec((tm, n), lambda i: (pos(i), 0)) for n, _ in out_tiled]
    out_specs += [pl.BlockSpec((r, n), lambda i: (0, 0)) for r, n in out_acc]
    out_shape = [jax.ShapeDtypeStruct((n_rows, n), dt) for n, dt in out_tiled]
    out_shape += [jax.ShapeDtypeStruct((r, n), F32) for r, n in out_acc]
    return pl.pallas_call(body, grid=(nt,), in_specs=in_specs, out_specs=out_specs, out_shape=out_shape, name=name,
                          compiler_params=_params("arbitrary"))(*args)


def _all_gather(name, block):
    def body(x_ref, out_ref, send_sems, recv_sems, local_sem):
        x, y, c = lax.axis_index("x"), lax.axis_index("y"), lax.axis_index("c")
        me, sibling = (x, y, c), (x, y, 1 - c)
        chips = [(1 - x, y), (x, 1 - y), (1 - x, 1 - y)]

        def rows(px, py, pc):
            return out_ref.at[4 * px + 2 * py + pc]

        def copy(k, blk, to, src=None):
            return pltpu.make_async_remote_copy(
                src_ref=rows(*blk) if src is None else src, dst_ref=rows(*blk),
                send_sem=send_sems.at[k], recv_sem=recv_sems.at[k], device_id=to, device_id_type=MESH)

        mine = pltpu.make_async_copy(x_ref, rows(*me), local_sem)
        mine.start()
        first = [copy(0, me, sibling, src=x_ref)]
        first += [copy(1 + j, me, (*chip, c), src=x_ref) for j, chip in enumerate(chips)]
        for cp in first:
            cp.start()
        passed = [copy(4 + j, (*chip, c), sibling) for j, chip in enumerate(chips)]
        for j, chip in enumerate(chips):
            copy(1 + j, (*chip, c), me).wait_recv()
            passed[j].start()
        copy(0, sibling, me).wait_recv()
        for j, chip in enumerate(chips):
            copy(4 + j, (*chip, 1 - c), me).wait_recv()
        for cp in first + passed:
            cp.wait_send()
        mine.wait()

    return pl.pallas_call(
        body, name=name, out_shape=jax.ShapeDtypeStruct((N_DEV,) + block.shape, block.dtype),
        in_specs=[pl.BlockSpec(memory_space=pl.ANY)], out_specs=pl.BlockSpec(memory_space=pl.ANY),
        scratch_shapes=[pltpu.SemaphoreType.DMA((7,)), pltpu.SemaphoreType.DMA((7,)), pltpu.SemaphoreType.DMA(())],
    )(block)


def _exchange(name, send):
    def body(s_ref, r_ref, send_sems, recv_sems, local_sem):
        x, y, c = lax.axis_index("x"), lax.axis_index("y"), lax.axis_index("c")
        me = 4 * x + 2 * y + c
        mine = pltpu.make_async_copy(s_ref.at[me], r_ref.at[me], local_sem)
        mine.start()
        copies = []
        for k in range(1, N_DEV):
            dx, dy, dc = (k >> 2) & 1, (k >> 1) & 1, k & 1
            px = 1 - x if dx else x
            py = 1 - y if dy else y
            pc = 1 - c if dc else c
            peer = 4 * px + 2 * py + pc
            copies.append((pltpu.make_async_remote_copy(
                src_ref=s_ref.at[peer], dst_ref=r_ref.at[me], send_sem=send_sems.at[k - 1],
                recv_sem=recv_sems.at[k - 1], device_id=(px, py, pc), device_id_type=MESH), peer))
        for cp, _ in copies:
            cp.start()
        for k, (cp, peer) in enumerate(copies):
            pltpu.make_async_remote_copy(
                src_ref=s_ref.at[peer], dst_ref=r_ref.at[peer], send_sem=send_sems.at[k], recv_sem=recv_sems.at[k],
                device_id=(x, y, c), device_id_type=MESH).wait_recv()
        for cp, _ in copies:
            cp.wait_send()
        mine.wait()

    return pl.pallas_call(
        body, name=name, out_shape=jax.ShapeDtypeStruct(send.shape, send.dtype),
        in_specs=[pl.BlockSpec(memory_space=pl.ANY)], out_specs=pl.BlockSpec(memory_space=pl.ANY),
        scratch_shapes=[pltpu.SemaphoreType.DMA((7,)), pltpu.SemaphoreType.DMA((7,)), pltpu.SemaphoreType.DMA(())],
    )(send)


def _direct_copies(kind, s_ref, r_ref, send_sems, recv_sems, local_sem):
    x, y, c = lax.axis_index("x"), lax.axis_index("y"), lax.axis_index("c")
    me = 4 * x + 2 * y + c

    def src(p):
        return s_ref if kind == "gather" else s_ref.at[p]

    local = pltpu.make_async_copy(src(me), r_ref.at[me], local_sem)
    sends, recvs = [], []
    for k in range(1, N_DEV):
        px = 1 - x if (k >> 2) & 1 else x
        py = 1 - y if (k >> 1) & 1 else y
        pc = 1 - c if k & 1 else c
        peer = 4 * px + 2 * py + pc
        sends.append(pltpu.make_async_remote_copy(
            src_ref=src(peer), dst_ref=r_ref.at[me], send_sem=send_sems.at[k - 1], recv_sem=recv_sems.at[k - 1],
            device_id=(px, py, pc), device_id_type=MESH))
        recvs.append(pltpu.make_async_remote_copy(
            src_ref=src(peer), dst_ref=r_ref.at[peer], send_sem=send_sems.at[k - 1], recv_sem=recv_sems.at[k - 1],
            device_id=(x, y, c), device_id_type=MESH))
    return local, sends, recvs


def _call(body, *, grid, in_specs, out_specs, out_shape, name, args, scratch_shapes=(), semantics=None, comms=()):
    single = not isinstance(out_shape, (list, tuple))
    out_shape = [out_shape] if single else list(out_shape)
    out_specs = [out_specs] if single else list(out_specs)
    if not comms:
        res = pl.pallas_call(body, grid=grid, in_specs=in_specs, out_specs=out_specs, out_shape=out_shape, name=name,
                             scratch_shapes=list(scratch_shapes),
                             compiler_params=_params(*(semantics or ("arbitrary",) * len(grid))))(*args)
        return res[0] if single else res
    n_in, n_out, n_scr, n_c = len(args), len(out_shape), len(scratch_shapes), len(comms)

    def hosted(*refs):
        ins, srcs = refs[:n_in], refs[n_in:n_in + n_c]
        outs = refs[n_in + n_c:n_in + n_c + n_out]
        dsts = refs[n_in + n_c + n_out:n_in + 2 * n_c + n_out]
        scr = refs[n_in + 2 * n_c + n_out:n_in + 2 * n_c + n_out + n_scr]
        sems = refs[n_in + 2 * n_c + n_out + n_scr:]
        first = functools.reduce(jnp.logical_and, [pl.program_id(d) == 0 for d in range(len(grid))])
        last = functools.reduce(jnp.logical_and, [pl.program_id(d) == grid[d] - 1 for d in range(len(grid))])
        plans = [_direct_copies(comms[i][0], srcs[i], dsts[i], *sems[3 * i:3 * i + 3]) for i in range(n_c)]

        @pl.when(first)
        def _():
            for local, sends, _ in plans:
                local.start()
                for cp in sends:
                    cp.start()

        body(*ins, *outs, *scr)

        @pl.when(last)
        def _():
            for local, sends, recvs in plans:
                for cp in recvs:
                    cp.wait_recv()
                for cp in sends:
                    cp.wait_send()
                local.wait()

    any_spec = pl.BlockSpec(memory_space=pl.ANY)
    comm_shapes = [jax.ShapeDtypeStruct(((N_DEV,) + a.shape) if kind == "gather" else a.shape, a.dtype)
                   for kind, a in comms]
    sem_shapes = []
    for _ in comms:
        sem_shapes += [pltpu.SemaphoreType.DMA((7,)), pltpu.SemaphoreType.DMA((7,)), pltpu.SemaphoreType.DMA(())]
    res = pl.pallas_call(
        hosted, grid=grid, in_specs=list(in_specs) + [any_spec] * n_c, out_specs=out_specs + [any_spec] * n_c,
        out_shape=out_shape + comm_shapes, name=name, scratch_shapes=list(scratch_shapes) + sem_shapes,
        compiler_params=_params(*(("arbitrary",) * len(grid))))(*args, *[a for _, a in comms])
    return res


def _unshard(gathered, axis):
    local = gathered.shape[1:]
    full = jnp.moveaxis(gathered, 0, axis)
    return full.reshape(local[:axis] + (N_DEV * local[axis],) + local[axis + 1:])


def _shard_blocks(full, axis):
    s = full.shape
    cut = full.reshape(s[:axis] + (N_DEV, s[axis] // N_DEV) + s[axis + 1:])
    return jnp.moveaxis(cut, axis, 0)


def _mm_fwd(name, a, w, out_dtype, gain=None, resid=None, glu=False):
    n_rows, k = a.shape
    n = w.shape[1]
    tm = _tile(n_rows, 512)
    n_out = n // 2 if glu else n

    def body(*refs):
        it = iter(refs)
        a_ref, w_ref = next(it), next(it)
        g_ref = next(it) if gain is not None else None
        r_ref = next(it) if resid is not None else None
        outs = list(it)
        av = a_ref[...]
        if g_ref is not None:
            _, xhat = _rms(av)
            av = xhat * g_ref[...]
        res = _dot(av.astype(BF16), w_ref[...])
        if glu:
            outs[1][...] = res.astype(outs[1].dtype)
            res = res[:, :n_out] * _sigmoid(res[:, n_out:])
        if r_ref is not None:
            res = res + r_ref[...]
        outs[0][...] = res.astype(outs[0].dtype)

    args = [a, w]
    in_specs = [pl.BlockSpec((tm, k), lambda i: (i, 0)), pl.BlockSpec((k, n), lambda i: (0, 0))]
    if gain is not None:
        args.append(gain)
        in_specs.append(pl.BlockSpec((1, k), lambda i: (0, 0)))
    if resid is not None:
        args.append(resid)
        in_specs.append(pl.BlockSpec((tm, n_out), lambda i: (i, 0)))
    out_shape = [jax.ShapeDtypeStruct((n_rows, n_out), out_dtype)]
    out_specs = [pl.BlockSpec((tm, n_out), lambda i: (i, 0))]
    if glu:
        out_shape.append(jax.ShapeDtypeStruct((n_rows, n), F32))
        out_specs.append(pl.BlockSpec((tm, n), lambda i: (i, 0)))
    return pl.pallas_call(body, grid=(n_rows // tm,), in_specs=in_specs, out_specs=out_specs, out_shape=out_shape,
                          name=name, compiler_params=_params("parallel"))(*args)


def _mm_bwd(name, a, d, w, gain=None, dres=None):
    n_rows, k = a.shape
    n = w.shape[1]
    tm = _tile(n_rows, 512)

    def body(*refs):
        it = iter(refs)
        a_ref, d_ref, w_ref = next(it), next(it), next(it)
        g_ref = next(it) if gain is not None else None
        r_ref = next(it) if gain is not None else None
        da_ref, dw_ref = next(it), next(it)
        dg_ref = next(it) if gain is not None else None
        i = pl.program_id(0)

        @pl.when(i == 0)
        def _():
            dw_ref[...] = jnp.zeros(dw_ref.shape, F32)
            if dg_ref is not None:
                dg_ref[...] = jnp.zeros(dg_ref.shape, F32)

        av = a_ref[...]
        dv = d_ref[...].astype(BF16)
        if g_ref is not None:
            r, xhat = _rms(av)
            ab = (xhat * g_ref[...]).astype(BF16)
        else:
            ab = av.astype(BF16)
        dw_ref[...] += _dot(ab, dv, TN)
        da = _dot(dv, w_ref[...], NT)
        if g_ref is not None:
            dg_ref[...] += jnp.sum(da * xhat, axis=0, keepdims=True)
            da = r_ref[...] + _rms_bwd(da, xhat, r, g_ref[...])
        da_ref[...] = da.astype(da_ref.dtype)

    args = [a, d, w]
    in_specs = [pl.BlockSpec((tm, k), lambda i: (i, 0)), pl.BlockSpec((tm, n), lambda i: (i, 0)),
                pl.BlockSpec((k, n), lambda i: (0, 0))]
    out_shape = [jax.ShapeDtypeStruct((n_rows, k), F32), jax.ShapeDtypeStruct((k, n), F32)]
    out_specs = [pl.BlockSpec((tm, k), lambda i: (i, 0)), pl.BlockSpec((k, n), lambda i: (0, 0))]
    if gain is not None:
        args += [gain, dres]
        in_specs += [pl.BlockSpec((1, k), lambda i: (0, 0)), pl.BlockSpec((tm, k), lambda i: (i, 0))]
        out_shape.append(jax.ShapeDtypeStruct((1, k), F32))
        out_specs.append(pl.BlockSpec((1, k), lambda i: (0, 0)))
    return pl.pallas_call(body, grid=(n_rows // tm,), in_specs=in_specs, out_specs=out_specs, out_shape=out_shape,
                          name=name, compiler_params=_params("arbitrary"))(*args)


def _ffn_fwd(name, x, gain, wi, wo, comms=()):
    n_rows, dm = x.shape
    _, nj, _, fb = wi.shape
    tm = _tile(n_rows, 512)

    def body(x_ref, g_ref, wi_ref, wo_ref, y_ref, gate_ref, up_ref):
        xv = x_ref[...]
        _, xhat = _rms(xv)
        hn = (xhat * g_ref[...]).astype(BF16)
        acc = jnp.zeros((tm, dm), F32)
        for j in range(nj):
            gate = _dot(hn, wi_ref[0, j])
            up = _dot(hn, wi_ref[1, j])
            gate_ref[j] = gate.astype(BF16)
            up_ref[j] = up.astype(BF16)
            act = (gate * _sigmoid(gate) * up).astype(BF16)
            acc = acc + _dot(act, wo_ref[j].reshape(fb, dm))
        y_ref[...] = xv + 0.5 * acc

    return _call(
        body, grid=(n_rows // tm,), name=name, args=[x, gain, wi, wo], comms=comms,
        in_specs=[pl.BlockSpec((tm, dm), lambda i: (i, 0)), pl.BlockSpec((1, dm), lambda i: (0, 0)),
                  pl.BlockSpec((2, nj, dm, fb), lambda i: (0, 0, 0, 0)),
                  pl.BlockSpec((nj, 2, fb // 2, dm), lambda i: (0, 0, 0, 0))],
        out_specs=[pl.BlockSpec((tm, dm), lambda i: (i, 0)), pl.BlockSpec((nj, tm, fb), lambda i: (0, i, 0)),
                   pl.BlockSpec((nj, tm, fb), lambda i: (0, i, 0))],
        out_shape=[jax.ShapeDtypeStruct((n_rows, dm), F32), jax.ShapeDtypeStruct((nj, n_rows, fb), BF16),
                   jax.ShapeDtypeStruct((nj, n_rows, fb), BF16)])


def _ffn_bwd_block(name, x, dy, gain, wi, wo, gate_s, up_s, j, acc, comms=()):
    n_rows, dm = x.shape
    _, nj, _, fb = wi.shape
    tm = _tile(n_rows, 512)
    last = j == nj - 1

    def body(*refs):
        it = iter(refs)
        x_ref, dy_ref, g_ref, wi_ref, wo_ref = next(it), next(it), next(it), next(it), next(it)
        gate_ref, up_ref = next(it), next(it)
        acc_ref = next(it) if acc is not None else None
        out_ref, dwi_ref, dwo_ref = next(it), next(it), next(it)
        dg_ref = next(it) if last else None
        i = pl.program_id(0)

        @pl.when(i == 0)
        def _():
            dwi_ref[...] = jnp.zeros(dwi_ref.shape, F32)
            dwo_ref[...] = jnp.zeros(dwo_ref.shape, F32)
            if last:
                dg_ref[...] = jnp.zeros(dg_ref.shape, F32)

        g = g_ref[...]
        wg, wu, wob = wi_ref[0], wi_ref[1], wo_ref[...].reshape(fb, dm)
        parts = range(FFN_ROW_PARTS)
        rp = tm // FFN_ROW_PARTS
        rows = [slice(k * rp, (k + 1) * rp) for k in parts]
        xv = [x_ref[rows[k], :] for k in parts]
        dyv = [dy_ref[rows[k], :] for k in parts]
        rx = [_rms(xv[k]) for k in parts]
        hn = [(rx[k][1] * g).astype(BF16) for k in parts]
        dout = [(0.5 * dyv[k]).astype(BF16) for k in parts]
        dact = [_dot(dout[k], wob, NT) for k in parts]
        gate = [gate_ref[rows[k], :].astype(F32) for k in parts]
        up = [up_ref[rows[k], :].astype(F32) for k in parts]
        s = [_sigmoid(gate[k]) for k in parts]
        silu = [gate[k] * s[k] for k in parts]
        act = [(silu[k] * up[k]).astype(BF16) for k in parts]
        dgate = [(dact[k] * up[k] * (s[k] * (1.0 + gate[k] * (1.0 - s[k])))).astype(BF16) for k in parts]
        dup = [(dact[k] * silu[k]).astype(BF16) for k in parts]
        for k in parts:
            dwo_ref[...] += _dot(act[k], dout[k], TN)
            dwi_ref[0] += _dot(hn[k], dgate[k], TN)
            dwi_ref[1] += _dot(hn[k], dup[k], TN)
        tot = [_dot(dgate[k], wg, NT) + _dot(dup[k], wu, NT) for k in parts]
        for k in parts:
            t = tot[k] + acc_ref[rows[k], :] if acc_ref is not None else tot[k]
            if last:
                out_ref[rows[k], :] = dyv[k] + _rms_bwd(t, rx[k][1], rx[k][0], g)
                dg_ref[...] += jnp.sum(t * rx[k][1], axis=0, keepdims=True)
            else:
                out_ref[rows[k], :] = t

    tok = pl.BlockSpec((tm, dm), lambda i: (i, 0))
    args = [x, dy, gain, wi, wo, gate_s, up_s]
    saved = pl.BlockSpec((None, tm, fb), lambda i: (j, i, 0))
    in_specs = [tok, tok, pl.BlockSpec((1, dm), lambda i: (0, 0)),
                pl.BlockSpec((2, None, dm, fb), lambda i: (0, j, 0, 0)),
                pl.BlockSpec((None, 2, fb // 2, dm), lambda i: (j, 0, 0, 0)), saved, saved]
    if acc is not None:
        args.append(acc)
        in_specs.append(tok)
    out_specs = [tok, pl.BlockSpec((2, dm, fb), lambda i: (0, 0, 0)), pl.BlockSpec((fb, dm), lambda i: (0, 0))]
    out_shape = [jax.ShapeDtypeStruct((n_rows, dm), F32), jax.ShapeDtypeStruct((2, dm, fb), F32),
                 jax.ShapeDtypeStruct((fb, dm), F32)]
    if last:
        out_specs.append(pl.BlockSpec((1, dm), lambda i: (0, 0)))
        out_shape.append(jax.ShapeDtypeStruct((1, dm), F32))
    return _call(body, grid=(n_rows // tm,), name=name, in_specs=in_specs, out_specs=out_specs,
                 out_shape=out_shape, args=args, comms=comms)


def _ffn_bwd(name, x, dy, gain, wi, wo, gate_s, up_s, comms_by_block=None):
    nj = wi.shape[1]
    acc, dwi, dwo, extra = None, [], [], []
    for j in range(nj):
        comms = (comms_by_block or {}).get(j, ())
        res = _ffn_bwd_block("%s_%d" % (name, j), x, dy, gain, wi, wo, gate_s, up_s, j, acc, comms)
        n_own = 4 if j == nj - 1 else 3
        acc = res[0]
        dwi.append(res[1])
        dwo.append(res[2])
        extra += list(res[n_own:])
        dgain = res[3] if j == nj - 1 else None
    return acc, jnp.stack(dwi, axis=1), jnp.stack(dwo, axis=0), dgain, extra


def _scan8(a, x, reverse):
    row = lax.broadcasted_iota(jnp.int32, a.shape, 0)
    for k in (1, 2, 4):
        if reverse:
            keep = row < SUBLANES - k
            a_s, x_s = pltpu.roll(a, SUBLANES - k, 0), pltpu.roll(x, SUBLANES - k, 0)
        else:
            keep = row >= k
            a_s, x_s = pltpu.roll(a, k, 0), pltpu.roll(x, k, 0)
        x = a * jnp.where(keep, x_s, 0.0) + x
        a = a * jnp.where(keep, a_s, 1.0)
    return a, x


def _scan_tile(a_ref, x_ref, h_ref, carry, reverse, rows):
    groups = rows // SUBLANES

    def step(n, c):
        gidx = groups - 1 - n if reverse else n
        sl = pl.ds(pl.multiple_of(gidx * SUBLANES, SUBLANES), SUBLANES)
        a_cum, h0 = _scan8(a_ref[sl, :], x_ref[sl, :], reverse)
        h = a_cum * c + h0
        h_ref[sl, :] = h
        return h[0:1] if reverse else h[SUBLANES - 1:SUBLANES]

    return lax.fori_loop(0, groups, step, carry)


def _s5_mats(lam_re, lam_im, log_dt, b_re, b_im, c_re, c_im):
    lc = S5_CHUNK
    groups, p = lam_re.shape
    h = b_re.shape[-1]
    lam = lax.complex(lam_re, lam_im)
    lam_dt = lam * jnp.exp(log_dt)[:, None]
    lam_bar = jnp.exp(lam_dt)
    b_bar = ((lam_bar - 1.0) / lam)[:, :, None] * lax.complex(b_re, b_im)
    c = lax.complex(c_re, c_im)
    pw = jnp.exp(lam_dt[None] * jnp.arange(lc + 1, dtype=F32)[:, None, None])
    resp = jnp.einsum('ghp,tgp,gpk->tghk', c, pw[:lc], b_bar, precision=HI).real
    s_idx = jnp.arange(lc)[:, None]
    u_idx = jnp.arange(lc)[None, :]
    onehot = (jnp.arange(lc)[:, None, None] == (u_idx - s_idx)[None]).astype(F32)
    m = jnp.einsum('tghk,tsu->gskuh', resp, onehot, precision=HI).reshape(groups, lc * h, lc * h)
    w = pw[lc - 1::-1][:lc].transpose(1, 0, 2)[:, :, None, :] * b_bar.transpose(0, 2, 1)[:, None]
    bm = jnp.concatenate([w.real, w.imag], axis=-1).reshape(groups, lc * h, 2 * p)
    v = c[:, None] * pw[1:lc + 1].transpose(1, 0, 2)[:, :, None, :]
    v = v.transpose(0, 3, 1, 2)
    cm = jnp.concatenate([v.real, -v.imag], axis=1).reshape(groups, 2 * p, lc * h)
    a = jnp.concatenate([pw[lc].real, pw[lc].imag], axis=-1)
    return m, bm, cm, a


def _s5_powers(lam_re, lam_im, log_dt):
    lam_dt = lax.complex(lam_re, lam_im) * jnp.exp(log_dt)[:, None]
    pw = jnp.exp(lam_dt[None] * (S5_CHUNK * jnp.arange(1, 9, dtype=F32))[:, None, None])

    def c1(z):
        return jnp.concatenate([z.real, z.real], axis=-1).reshape(z.shape[0], -1)

    def c2(z):
        return jnp.concatenate([-z.imag, z.imag], axis=-1).reshape(z.shape[0], -1)

    p1, p2 = c1(pw), c2(pw)
    apw = jnp.stack([p1[0], p2[0], p1[1], p2[1], p1[3], p2[3], jnp.zeros_like(p1[0]), jnp.zeros_like(p1[0])])
    fwd = jnp.concatenate([p1, p2], axis=0)
    rev = jnp.concatenate([c1(pw[::-1]), c2(pw[::-1])], axis=0)
    return apw, fwd, rev


def _cmul(c1, c2, x, half, conj=False):
    sw = pltpu.roll(x, half, 1)
    return c1 * x - c2 * sw if conj else c1 * x + c2 * sw


def _gather_groups(u_ref, ug_ref, nc):
    h = S5_GROUP
    rows = min(S5_REGROUP_ROWS, nc)

    def step(r, _):
        base = pl.multiple_of(r * rows, rows)
        for t in range(S5_CHUNK):
            val = u_ref[pl.ds(base * S5_CHUNK + t, rows, stride=S5_CHUNK), :]
            for g in range(S5_OCTET):
                ug_ref[g, pl.ds(base, rows), t * h:(t + 1) * h] = val[:, g * h:(g + 1) * h]
        return 0

    lax.fori_loop(0, nc // rows, step, 0)


def _scatter_groups(yg_ref, y_ref, nc):
    h = S5_GROUP
    rows = min(S5_REGROUP_ROWS, nc)

    def step(r, _):
        base = pl.multiple_of(r * rows, rows)
        for t in range(S5_CHUNK):
            y_ref[pl.ds(base * S5_CHUNK + t, rows, stride=S5_CHUNK), :] = jnp.concatenate(
                [yg_ref[g, pl.ds(base, rows), t * h:(t + 1) * h] for g in range(S5_OCTET)], axis=1)
        return 0

    lax.fori_loop(0, nc // rows, step, 0)


def _s5_fwd(name, u, m, bm, cm, apw, arows, comms=()):
    n_rows, width = u.shape
    nc = n_rows // S5_CHUNK
    groups, lh, _ = m.shape
    p2 = bm.shape[2]
    gb = S5_OCTET
    lanes = gb * S5_GROUP

    def body(u_ref, m_ref, b_ref, c_ref, apw_ref, ar_ref, y_ref, sp_ref, ug_ref, yg_ref, xs_ref):
        _gather_groups(u_ref, ug_ref, nc)
        for gi in range(gb):
            xs_ref[:, gi * p2:(gi + 1) * p2] = _dot(ug_ref[gi], b_ref[gi], prec=S5_PREC)
        row = lax.broadcasted_iota(jnp.int32, (SUBLANES, p2), 0)

        def group(n, carry):
            sl = pl.ds(pl.multiple_of(n * SUBLANES, SUBLANES), SUBLANES)
            new = []
            for gi in range(gb):
                ln = slice(gi * p2, (gi + 1) * p2)
                x = xs_ref[sl, ln]
                for q, k in enumerate((1, 2, 4)):
                    xs = jnp.where(row >= k, pltpu.roll(x, k, 0), 0.0)
                    x = x + _cmul(apw_ref[2 * q:2 * q + 1, ln], apw_ref[2 * q + 1:2 * q + 2, ln], xs, p2 // 2)
                cb = jnp.broadcast_to(carry[gi], (SUBLANES, p2))
                s8 = x + _cmul(ar_ref[0:8, ln], ar_ref[8:16, ln], cb, p2 // 2)
                sp_ref[sl, ln] = jnp.where(row >= 1, pltpu.roll(s8, 1, 0), cb)
                new.append(s8[SUBLANES - 1:SUBLANES])
            return tuple(new)

        unroll = min(S5_SCAN_UNROLL, nc // SUBLANES)

        def step(n, carry):
            for k in range(unroll):
                carry = group(n * unroll + k, carry)
            return carry

        lax.fori_loop(0, nc // (SUBLANES * unroll), step, tuple(jnp.zeros((1, p2), F32) for _ in range(gb)))
        for gi in range(gb):
            yg_ref[gi] = (_dot(ug_ref[gi], m_ref[gi], prec=S5_PREC)
                          + _dot(sp_ref[:, gi * p2:(gi + 1) * p2], c_ref[gi], prec=S5_PREC))
        _scatter_groups(yg_ref, y_ref, nc)

    tok = pl.BlockSpec((n_rows, lanes), lambda g: (0, g), pipeline_mode=pl.Buffered(1))
    return _call(
        body, grid=(groups // gb,), name=name, args=[u, m, bm, cm, apw, arows], comms=comms, semantics=("parallel",),
        in_specs=[tok, pl.BlockSpec((gb, lh, lh), lambda g: (g, 0, 0)),
                  pl.BlockSpec((gb, lh, p2), lambda g: (g, 0, 0)), pl.BlockSpec((gb, p2, lh), lambda g: (g, 0, 0)),
                  pl.BlockSpec((8, gb * p2), lambda g: (0, g)), pl.BlockSpec((16, gb * p2), lambda g: (0, g))],
        out_specs=[tok, pl.BlockSpec((nc, gb * p2), lambda g: (0, g))],
        out_shape=[jax.ShapeDtypeStruct((n_rows, width), F32), jax.ShapeDtypeStruct((nc, groups * p2), F32)],
        scratch_shapes=[pltpu.VMEM((gb, nc, lh), F32), pltpu.VMEM((gb, nc, lh), F32), pltpu.VMEM((nc, gb * p2), F32)])


def _s5_bwd(name, u, dy, sprev, m, bm, cm, apw, arows_rev):
    n_rows, width = u.shape
    nc = n_rows // S5_CHUNK
    groups, lh, _ = m.shape
    p2 = bm.shape[2]
    half = p2 // 2
    gb = S5_OCTET
    lanes = gb * S5_GROUP

    def body(u_ref, dy_ref, sp_ref, m_ref, b_ref, c_ref, apw_ref, ar_ref,
             du_ref, dm_ref, db_ref, dc_ref, da_ref, ug_ref, dyg_ref, ds_ref, gx_ref):
        _gather_groups(u_ref, ug_ref, nc)
        _gather_groups(dy_ref, dyg_ref, nc)
        for gi in range(gb):
            ds_ref[:, gi * p2:(gi + 1) * p2] = _dot(dyg_ref[gi], c_ref[gi], NT, prec=S5_PREC)
        row = lax.broadcasted_iota(jnp.int32, (SUBLANES, p2), 0)
        lane = lax.broadcasted_iota(jnp.int32, (SUBLANES, p2), 1)
        ngroups = nc // SUBLANES

        def group(n, state):
            carry, nxt, dacc = state
            sl = pl.ds(pl.multiple_of((ngroups - 1 - n) * SUBLANES, SUBLANES), SUBLANES)
            new_c, new_n, new_d = [], [], []
            for gi in range(gb):
                ln = slice(gi * p2, (gi + 1) * p2)
                d8 = ds_ref[sl, ln]
                x = jnp.where(row < SUBLANES - 1, pltpu.roll(d8, SUBLANES - 1, 0),
                              jnp.broadcast_to(nxt[gi], (SUBLANES, p2)))
                for q, k in enumerate((1, 2, 4)):
                    xs = jnp.where(row < SUBLANES - k, pltpu.roll(x, SUBLANES - k, 0), 0.0)
                    x = x + _cmul(apw_ref[2 * q:2 * q + 1, ln], apw_ref[2 * q + 1:2 * q + 2, ln], xs, half, conj=True)
                cb = jnp.broadcast_to(carry[gi], (SUBLANES, p2))
                g8 = x + _cmul(ar_ref[0:8, ln], ar_ref[8:16, ln], cb, half, conj=True)
                gx_ref[sl, ln] = g8
                s8 = sp_ref[sl, ln]
                p1 = g8 * s8
                pq = g8 * pltpu.roll(s8, half, 1)
                d_a = jnp.where(lane < half, p1 + pltpu.roll(p1, half, 1), pq - pltpu.roll(pq, half, 1))
                new_c.append(g8[0:1])
                new_n.append(d8[0:1])
                new_d.append(dacc[gi] + jnp.sum(d_a, axis=0, keepdims=True))
            return tuple(new_c), tuple(new_n), tuple(new_d)

        unroll = min(S5_SCAN_UNROLL, ngroups)

        def step(n, state):
            for k in range(unroll):
                state = group(n * unroll + k, state)
            return state

        zeros = tuple(jnp.zeros((1, p2), F32) for _ in range(gb))
        _, _, dacc = lax.fori_loop(0, ngroups // unroll, step, (zeros, zeros, zeros))
        for gi in range(gb):
            ln = slice(gi * p2, (gi + 1) * p2)
            da_ref[:, ln] = dacc[gi]
            ug, dyg, gxg = ug_ref[gi], dyg_ref[gi], gx_ref[:, ln]
            dm_ref[gi] = _dot(ug, dyg, TN, prec=S5_PREC)
            dc_ref[gi] = _dot(sp_ref[:, ln], dyg, TN, prec=S5_PREC)
            db_ref[gi] = _dot(ug, gxg, TN, prec=S5_PREC)
            dyg_ref[gi] = _dot(dyg, m_ref[gi], NT, prec=S5_PREC) + _dot(gxg, b_ref[gi], NT, prec=S5_PREC)
        _scatter_groups(dyg_ref, du_ref, nc)

    tok = pl.BlockSpec((n_rows, lanes), lambda g: (0, g), pipeline_mode=pl.Buffered(1))
    tok_s = pl.BlockSpec((nc, gb * p2), lambda g: (0, g))
    mat_m = pl.BlockSpec((gb, lh, lh), lambda g: (g, 0, 0))
    mat_b = pl.BlockSpec((gb, lh, p2), lambda g: (g, 0, 0))
    mat_c = pl.BlockSpec((gb, p2, lh), lambda g: (g, 0, 0))
    return pl.pallas_call(
        body, grid=(groups // gb,), name=name,
        in_specs=[tok, tok, tok_s, mat_m, mat_b, mat_c,
                  pl.BlockSpec((8, gb * p2), lambda g: (0, g)), pl.BlockSpec((16, gb * p2), lambda g: (0, g))],
        out_specs=[tok, mat_m, mat_b, mat_c, pl.BlockSpec((1, gb * p2), lambda g: (0, g))],
        out_shape=[jax.ShapeDtypeStruct((n_rows, width), F32), jax.ShapeDtypeStruct(m.shape, F32),
                   jax.ShapeDtypeStruct(bm.shape, F32), jax.ShapeDtypeStruct(cm.shape, F32),
                   jax.ShapeDtypeStruct((1, groups * p2), F32)],
        scratch_shapes=[pltpu.VMEM((gb, nc, lh), F32), pltpu.VMEM((gb, nc, lh), F32),
                        pltpu.VMEM((nc, gb * p2), F32), pltpu.VMEM((nc, gb * p2), F32)],
        compiler_params=_params("parallel"),
    )(u, dy, sprev, m, bm, cm, apw, arows_rev)


def _split(x):
    hi = x.astype(BF16)
    return hi, (x - hi.astype(F32)).astype(BF16)


def _sb_more(kb, carries):
    top = jnp.max(carries[0])
    for c in carries[1:]:
        top = jnp.maximum(top, jnp.max(c))
    return (kb >= 0) & (top > SB_UNDERFLOW)


def _sb_fwd(name, qkv, heads):
    n_rows, dm3 = qkv.shape
    dm = dm3 // 3
    hd = dm // heads
    tq = _tile(n_rows // 2, 256)
    hb = min(SB_HEADS_FWD, heads)
    groups = heads // hb
    scale = hd ** -0.5

    def body(q_ref, k_ref, v_ref, o_ref):
        qi = pl.program_id(1)
        hs = range(hb)
        row = lax.broadcasted_iota(jnp.int32, (tq, tq), 0)
        col = lax.broadcasted_iota(jnp.int32, (tq, tq), 1)
        tri = (row > col).astype(BF16)
        causal = col < row
        qall = q_ref[...] * scale
        qb = [qall[:, h * hd:(h + 1) * hd] for h in hs]

        def block(kb, carries, accs, diagonal):
            ks = pl.ds(pl.multiple_of(kb * tq, tq), tq)
            kblk, vblk = k_ref[ks, :], v_ref[ks, :]
            z = [_dot(qb[h], kblk[:, h * hd:(h + 1) * hd], NT) for h in hs]
            sp = [_softplus(z[h]) for h in hs]
            lk = [-sp[h] for h in hs]
            if diagonal:
                lk = [jnp.where(causal, lk[h], 0.0) for h in hs]
            parts = [_split(lk[h]) for h in hs]
            r = [_dot(parts[h][0], tri) + _dot(parts[h][1], tri) for h in hs]
            a = [jnp.exp(z[h] - sp[h] + r[h] + carries[h]) for h in hs]
            if diagonal:
                a = [jnp.where(causal, a[h], 0.0) for h in hs]
            new_a = tuple(accs[h] + _dot(a[h].astype(BF16), vblk[:, h * hd:(h + 1) * hd]) for h in hs)
            new_c = tuple(carries[h] + jnp.sum(lk[h], axis=1, keepdims=True) for h in hs)
            return new_c, new_a

        zc = tuple(jnp.zeros((tq, 1), F32) for _ in hs)
        za = tuple(jnp.zeros((tq, hd), F32) for _ in hs)
        carries, accs = block(qi, zc, za, True)
        _, _, accs = lax.while_loop(lambda st: _sb_more(st[0], st[1]),
                                    lambda st: (st[0] - 1,) + block(st[0], st[1], st[2], False),
                                    (qi - 1, carries, accs))
        o_ref[...] = jnp.concatenate(accs, axis=1)

    lanes = hb * hd
    return pl.pallas_call(
        body, grid=(groups, n_rows // tq), name=name,
        in_specs=[pl.BlockSpec((tq, lanes), lambda g, i: (i, g)),
                  pl.BlockSpec((n_rows, lanes), lambda g, i: (0, groups + g)),
                  pl.BlockSpec((n_rows, lanes), lambda g, i: (0, 2 * groups + g))],
        out_specs=pl.BlockSpec((tq, lanes), lambda g, i: (i, g)),
        out_shape=jax.ShapeDtypeStruct((n_rows, dm), F32),
        compiler_params=_params("parallel", "arbitrary"))(qkv, qkv, qkv)


def _sb_bwd(name, qkv, o, do, heads):
    n_rows, dm3 = qkv.shape
    dm = dm3 // 3
    hd = dm // heads
    tq = _tile(n_rows // 2, 256)
    hb = min(SB_HEADS_BWD, heads)
    groups = heads // hb
    scale = hd ** -0.5

    def body(q_ref, k_ref, v_ref, o_ref, do_ref, dq_ref, dk_ref, dv_ref):
        qi = pl.program_id(1)

        @pl.when(qi == 0)
        def _():
            dk_ref[...] = jnp.zeros(dk_ref.shape, F32)
            dv_ref[...] = jnp.zeros(dv_ref.shape, F32)

        hs = range(hb)
        cols = [slice(h * hd, (h + 1) * hd) for h in hs]
        qall = q_ref[...] * scale
        doall = do_ref[...].astype(BF16)
        prod = doall.astype(F32) * o_ref[...]
        qb = [qall[:, cols[h]] for h in hs]
        dob16 = [doall[:, cols[h]] for h in hs]
        delta = [jnp.sum(prod[:, cols[h]], axis=1, keepdims=True) for h in hs]
        row = lax.broadcasted_iota(jnp.int32, (tq, tq), 0)
        col = lax.broadcasted_iota(jnp.int32, (tq, tq), 1)
        tri = (row > col).astype(BF16)
        tri_incl = (row >= col).astype(BF16)
        causal = col < row

        def block(kb, carries, pcarries, dqs, diagonal):
            ks = pl.ds(pl.multiple_of(kb * tq, tq), tq)
            kall, vall = k_ref[ks, :], v_ref[ks, :]
            kblk = [kall[:, cols[h]] for h in hs]
            vblk = [vall[:, cols[h]] for h in hs]
            z = [_dot(qb[h], kblk[h], NT) for h in hs]
            da = [_dot(dob16[h], vblk[h], NT) for h in hs]
            sp = [_softplus(z[h]) for h in hs]
            lk = [-sp[h] for h in hs]
            if diagonal:
                lk = [jnp.where(causal, lk[h], 0.0) for h in hs]
            lb = [z[h] - sp[h] for h in hs]
            parts = [_split(lk[h]) for h in hs]
            r = [_dot(parts[h][0], tri) + _dot(parts[h][1], tri) for h in hs]
            a = [jnp.exp(lb[h] + r[h] + carries[h]) for h in hs]
            if diagonal:
                a = [jnp.where(causal, a[h], 0.0) for h in hs]
            a16 = [a[h].astype(BF16) for h in hs]
            p = [da[h] * a16[h].astype(F32) for h in hs]
            pparts = [_split(p[h]) for h in hs]
            pc = [_dot(pparts[h][0], tri_incl) + _dot(pparts[h][1], tri_incl) for h in hs]
            beta = [jnp.exp(lb[h]) for h in hs]
            dz = [p[h] * (1.0 - beta[h]) - beta[h] * (delta[h] - pcarries[h] - pc[h]) for h in hs]
            if diagonal:
                dz = [jnp.where(causal, dz[h], 0.0) for h in hs]
            dz16 = [dz[h].astype(BF16) for h in hs]
            dk_ref[ks, :] += jnp.concatenate([_dot(dz16[h], qb[h], TN) for h in hs], axis=1)
            dv_ref[ks, :] += jnp.concatenate([_dot(a16[h], dob16[h], TN) for h in hs], axis=1)
            return (tuple(carries[h] + jnp.sum(lk[h], axis=1, keepdims=True) for h in hs),
                    tuple(pcarries[h] + jnp.sum(p[h], axis=1, keepdims=True) for h in hs),
                    tuple(dqs[h] + _dot(dz16[h], kblk[h]) for h in hs))

        zc = tuple(jnp.zeros((tq, 1), F32) for _ in hs)
        zq = tuple(jnp.zeros((tq, hd), F32) for _ in hs)
        st = block(qi, zc, zc, zq, True)
        st = lax.while_loop(lambda s: _sb_more(s[0], s[1]),
                            lambda s: (s[0] - 1,) + block(s[0], s[1], s[2], s[3], False), (qi - 1,) + st)
        dq_ref[...] = jnp.concatenate(st[3], axis=1) * scale

    lanes = hb * hd
    tile = pl.BlockSpec((tq, lanes), lambda g, i: (i, g))
    whole = pl.BlockSpec((n_rows, lanes), lambda g, i: (0, g), pipeline_mode=pl.Buffered(1))
    full = jax.ShapeDtypeStruct((n_rows, dm), F32)
    return pl.pallas_call(
        body, grid=(groups, n_rows // tq), name=name,
        in_specs=[tile, pl.BlockSpec((n_rows, lanes), lambda g, i: (0, groups + g), pipeline_mode=pl.Buffered(1)),
                  pl.BlockSpec((n_rows, lanes), lambda g, i: (0, 2 * groups + g), pipeline_mode=pl.Buffered(1)),
                  tile, tile],
        out_specs=[tile, whole, whole], out_shape=[full, full, full],
        compiler_params=_params("parallel", "arbitrary"))(qkv, qkv, qkv, o, do)


def _block_diag(xb, w_ref_val, dims):
    nb = w_ref_val.shape[0]
    bw = xb.shape[1] // nb
    return jnp.concatenate([_dot(xb[:, n * bw:(n + 1) * bw], w_ref_val[n], dims) for n in range(nb)], axis=1)


def _lru_gates_fwd(name, gx, conv_w, conv_b, wa, ba, wx, bx, lam):
    n_rows, w2 = gx.shape
    w = w2 // 2
    tm = _tile(n_rows, 256)

    def fn(i, nt, br, prev, cw, cb, wa_v, ba_v, wx_v, bx_v, lam_v):
        xc = cb + sum(cw[k:k + 1] * _shift_down(br, prev, LRU_CONV - 1 - k, i == 0) for k in range(LRU_CONV))
        xb = xc.astype(BF16)
        r = _sigmoid(_block_diag(xb, wa_v, NN) + ba_v)
        ig = _sigmoid(_block_diag(xb, wx_v, NN) + bx_v)
        log_a = (-LRU_C * r) * _softplus(-lam_v)
        a = jnp.exp(log_a)
        gated = (ig * xc) * _one_minus_a2_sqrt(log_a)
        return (xc, r, ig, a, gated), ()

    return _rowwise(name, fn, [('t', gx, w, 1), ('p', gx, w, 1), ('b', conv_w), ('b', conv_b), ('b', wa), ('b', ba),
                               ('b', wx), ('b', bx), ('b', lam)], [(w, F32)] * 5, [], n_rows, tm)


def _lru_scan_fwd(name, a, gated, gx):
    n_rows, w = a.shape
    tm = _tile(n_rows, 256)

    def body(a_ref, x_ref, bg_ref, hs_ref, y_ref, carry_ref):
        @pl.when(pl.program_id(0) == 0)
        def _():
            carry_ref[...] = jnp.zeros(carry_ref.shape, F32)
        carry_ref[...] = _scan_tile(a_ref, x_ref, hs_ref, carry_ref[...], False, tm)
        y_ref[...] = (_gelu(bg_ref[...]) * hs_ref[...]).astype(BF16)

    tok = pl.BlockSpec((tm, w), lambda i: (i, 0))
    return pl.pallas_call(body, grid=(n_rows // tm,), in_specs=[tok, tok, tok], out_specs=[tok, tok], name=name,
                          out_shape=[jax.ShapeDtypeStruct((n_rows, w), F32), jax.ShapeDtypeStruct((n_rows, w), BF16)],
                          scratch_shapes=[pltpu.VMEM((1, w), F32)], compiler_params=_params("arbitrary"))(a, gated, gx)


def _lru_scan_bwd(name, a, dy, gx):
    n_rows, w = a.shape
    tm = _tile(n_rows, 256)
    nt = n_rows // tm
    per8 = tm // SUBLANES

    def body(a_ref, an_ref, dy_ref, bg_ref, lam_ref, carry_ref, aup_ref, dhs_ref):
        i = pl.program_id(0)

        @pl.when(i == 0)
        def _():
            carry_ref[...] = jnp.zeros(carry_ref.shape, F32)
        aup_ref[...] = _shift_up(a_ref[...], an_ref[...], 1, i == 0)
        dhs_ref[...] = dy_ref[...] * _gelu(bg_ref[...])
        carry_ref[...] = _scan_tile(aup_ref, dhs_ref, lam_ref, carry_ref[...], True, tm)

    tok = pl.BlockSpec((tm, w), lambda i: (nt - 1 - i, 0))
    nxt = pl.BlockSpec((SUBLANES, w), lambda i: (jnp.minimum((nt - i) * per8, n_rows // SUBLANES - 1), 0))
    return pl.pallas_call(body, grid=(nt,), in_specs=[tok, nxt, tok, tok], out_specs=tok, name=name,
                          out_shape=jax.ShapeDtypeStruct((n_rows, w), F32),
                          scratch_shapes=[pltpu.VMEM((1, w), F32), pltpu.VMEM((tm, w), F32), pltpu.VMEM((tm, w), F32)],
                          compiler_params=_params("arbitrary"))(a, a, dy, gx)


def _lru_gates_bwd(name, lam_t, hs, xc, r, ig, a, wa, wx, lam):
    n_rows, w = xc.shape
    nb, bw, _ = wa.shape
    tm = _tile(n_rows, 256)

    def fn(i, nt, lt, hs_v, hs_prev, xc_v, r_v, ig_v, a_v, wa_v, wx_v, lam_v):
        sp = _softplus(-lam_v)
        log_a = (-LRU_C * r_v) * sp
        mult = _one_minus_a2_sqrt(log_a)
        d_a = lt * _shift_down(hs_v, hs_prev, 1, i == 0)
        d_ig = lt * xc_v * mult
        d_mult = lt * ig_v * xc_v
        d_log_a = d_a * a_v - d_mult * (a_v * a_v) / mult
        d_ra = d_log_a * (-LRU_C * sp) * r_v * (1.0 - r_v)
        d_ia = d_ig * ig_v * (1.0 - ig_v)
        d_ra16, d_ia16, xb = d_ra.astype(BF16), d_ia.astype(BF16), xc_v.astype(BF16)
        dxc = lt * ig_v * mult + _block_diag(d_ra16, wa_v, NT) + _block_diag(d_ia16, wx_v, NT)
        dwa = jnp.concatenate([_dot(xb[:, n * bw:(n + 1) * bw], d_ra16[:, n * bw:(n + 1) * bw], TN)
                               for n in range(nb)], axis=0)
        dwx = jnp.concatenate([_dot(xb[:, n * bw:(n + 1) * bw], d_ia16[:, n * bw:(n + 1) * bw], TN)
                               for n in range(nb)], axis=0)
        col = lambda t: jnp.sum(t, axis=0, keepdims=True)
        return (dxc,), (dwa, dwx, col(d_ra), col(d_ia), col(d_log_a * (-LRU_C * r_v)))

    tiled = lambda arr: ('t', arr, w, 0)
    return _rowwise(name, fn, [tiled(lam_t), tiled(hs), ('p', hs, w, 0), tiled(xc), tiled(r), tiled(ig), tiled(a),
                               ('b', wa), ('b', wx), ('b', lam)],
                    [(w, F32)], [(nb * bw, bw), (nb * bw, bw), (1, w), (1, w), (1, w)], n_rows, tm)


def _lru_conv_bwd(name, dxc, gx, dy, hs, conv_w):
    n_rows, w = dxc.shape
    tm = _tile(n_rows, 256)

    def fn(i, nt, dxc_v, dxc_next, bg, br, br_prev, dy_v, hs_v, cw):
        dbr = sum(cw[k:k + 1] * _shift_up(dxc_v, dxc_next, LRU_CONV - 1 - k, i == nt - 1) for k in range(LRU_CONV))
        dbg = dy_v * hs_v * _gelu_grad(bg)
        dcw = [jnp.sum(dxc_v * _shift_down(br, br_prev, LRU_CONV - 1 - k, i == 0), axis=0, keepdims=True)
               for k in range(LRU_CONV)]
        dcw = jnp.concatenate(dcw + [jnp.zeros((SUBLANES - LRU_CONV, w), F32)], axis=0)
        return (jnp.concatenate([dbg, dbr], axis=1),), (dcw, jnp.sum(dxc_v, axis=0, keepdims=True))

    return _rowwise(name, fn, [('t', dxc, w, 0), ('n', dxc, w, 0), ('t', gx, w, 0), ('t', gx, w, 1), ('p', gx, w, 1),
                               ('t', dy, w, 0), ('t', hs, w, 0), ('b', conv_w)],
                    [(2 * w, BF16)], [(SUBLANES, w), (1, w)], n_rows, tm)


def _loss_head(name, h, gain, target):
    n_rows, dm = h.shape
    tm = _tile(n_rows, 512)

    def fn(i, nt, hv, tv, g):
        r, xhat = _rms(hv)
        err = xhat * g - tv
        dy = err * (1.0 / dm)
        return ((_rms_bwd(dy, xhat, r, g),),
                (jnp.sum(err * err, axis=0, keepdims=True), jnp.sum(dy * xhat, axis=0, keepdims=True)))

    return _rowwise(name, fn, [('t', h, dm, 0), ('t', target, dm, 0), ('b', gain)], [(dm, F32)], [(1, dm), (1, dm)],
                    n_rows, tm)


def _adamw(name, gparts, w, m, v):
    n_parts, n_rows, cols = gparts.shape
    tr = n_rows
    for cand in (256, 128, 64, 32, 16, 8):
        if n_rows % cand == 0:
            tr = cand
            break
    c1 = 1.0 - ADAM_B1 ** ADAM_STEP
    c2 = 1.0 - ADAM_B2 ** ADAM_STEP

    def body(gp_ref, w_ref, m_ref, v_ref, g_ref, d_ref, nm_ref, nv_ref):
        g = gp_ref[0].astype(F32)
        for p in range(1, n_parts):
            g = g + gp_ref[p].astype(F32)
        m_new = ADAM_B1 * m_ref[...] + (1.0 - ADAM_B1) * g
        v_new = ADAM_B2 * v_ref[...] + (1.0 - ADAM_B2) * (g * g)
        m_hat = m_new / c1
        v_hat = v_new / c2
        g_ref[...] = g
        d_ref[...] = -ADAM_LR * (m_hat / (jnp.sqrt(v_hat) + ADAM_EPS) + ADAM_WD * w_ref[...])
        nm_ref[...] = m_new
        nv_ref[...] = v_new

    blk = pl.BlockSpec((tr, cols), lambda i: (i, 0))
    shp = jax.ShapeDtypeStruct((n_rows, cols), F32)
    return pl.pallas_call(body, grid=(n_rows // tr,), name=name,
                          in_specs=[pl.BlockSpec((n_parts, tr, cols), lambda i: (0, i, 0)), blk, blk, blk],
                          out_specs=[blk, blk, blk, blk], out_shape=[shp, shp, shp, shp],
                          compiler_params=_params("parallel"))(gparts, w, m, v)


def _adamw_layers(name, recvs, w, m, v):
    n_layers, n_rows, cols = w.shape
    n_parts = recvs[0].shape[0]
    tr = max(t for t in range(16, ADAMW_LAYER_ROWS + 1, 16) if n_rows % t == 0)
    c1 = 1.0 - ADAM_B1 ** ADAM_STEP
    c2 = 1.0 - ADAM_B2 ** ADAM_STEP

    def body(*refs):
        gp_refs = refs[:n_layers]
        w_ref, m_ref, v_ref, g_ref, d_ref, nm_ref, nv_ref = refs[n_layers:]
        layer = pl.program_id(0)
        for k in range(n_layers):
            @pl.when(layer == k)
            def _(k=k):
                g = gp_refs[k][0].astype(F32)
                for p in range(1, n_parts):
                    g = g + gp_refs[k][p].astype(F32)
                m_new = ADAM_B1 * m_ref[...] + (1.0 - ADAM_B1) * g
                v_new = ADAM_B2 * v_ref[...] + (1.0 - ADAM_B2) * (g * g)
                g_ref[...] = g
                d_ref[...] = -ADAM_LR * ((m_new / c1) / (jnp.sqrt(v_new / c2) + ADAM_EPS) + ADAM_WD * w_ref[...])
                nm_ref[...] = m_new
                nv_ref[...] = v_new

    blk = pl.BlockSpec((None, tr, cols), lambda l, i: (l, i, 0))
    shp = jax.ShapeDtypeStruct((n_layers, n_rows, cols), F32)
    gp_specs = [pl.BlockSpec((n_parts, tr, cols), lambda l, i, k=k: (0, jnp.where(l == k, i, 0), 0))
                for k in range(n_layers)]
    return pl.pallas_call(body, grid=(n_layers, n_rows // tr), name=name, in_specs=gp_specs + [blk, blk, blk],
                          out_specs=[blk, blk, blk, blk], out_shape=[shp, shp, shp, shp],
                          compiler_params=_params("arbitrary", "arbitrary"))(*recvs, w, m, v)


def _pack_rows(arrays, cols, lead=0):
    flat = [a.reshape(a.shape[:lead] + (-1,)) for a in arrays]
    cat = jnp.concatenate(flat, axis=lead) if len(flat) > 1 else flat[0]
    n = cat.shape[lead]
    pad = (-n) % (cols * PACK_ROWS)
    if pad:
        cat = jnp.pad(cat, [(0, 0)] * lead + [(0, pad)])
    return cat.reshape(cat.shape[:lead] + (-1, cols))


def _unpack_rows(packed, shapes, lead=0):
    flat = packed.reshape(packed.shape[:lead] + (-1,))
    out, off = [], 0
    for s in shapes:
        n = math.prod(s)
        out.append(lax.slice_in_dim(flat, off, off + n, axis=lead).reshape(flat.shape[:lead] + tuple(s)))
        off += n
    return out


def kernel(x, ffn1_norm, ffn1_w_in, ffn1_w_out, mix_norm, ffn2_norm, ffn2_w_in, ffn2_w_out, final_norm, s5_w_in, s5_lam_re, s5_lam_im, s5_log_dt, s5_b_re, s5_b_im, s5_c_re, s5_c_im, s5_d, s5_w_out, sb_w_qkv, sb_w_out, lru_w_in, lru_conv_w, lru_conv_b, lru_w_a, lru_b_a, lru_w_x, lru_b_x, lru_lambda, lru_w_out, loss_target, m_ffn1_norm, m_ffn1_w_in, m_ffn1_w_out, m_mix_norm, m_ffn2_norm, m_ffn2_w_in, m_ffn2_w_out, m_final_norm, m_s5_w_in, m_s5_lam_re, m_s5_lam_im, m_s5_log_dt, m_s5_b_re, m_s5_b_im, m_s5_c_re, m_s5_c_im, m_s5_d, m_s5_w_out, m_sb_w_qkv, m_sb_w_out, m_lru_w_in, m_lru_conv_w, m_lru_conv_b, m_lru_w_a, m_lru_b_a, m_lru_w_x, m_lru_b_x, m_lru_lambda, m_lru_w_out, v_ffn1_norm, v_ffn1_w_in, v_ffn1_w_out, v_mix_norm, v_ffn2_norm, v_ffn2_w_in, v_ffn2_w_out, v_final_norm, v_s5_w_in, v_s5_lam_re, v_s5_lam_im, v_s5_log_dt, v_s5_b_re, v_s5_b_im, v_s5_c_re, v_s5_c_im, v_s5_d, v_s5_w_out, v_sb_w_qkv, v_sb_w_out, v_lru_w_in, v_lru_conv_w, v_lru_conv_b, v_lru_w_a, v_lru_b_a, v_lru_w_x, v_lru_b_x, v_lru_lambda, v_lru_w_out):
    local = dict(locals())
    W = {n: local[n] for n in WEIGHTS}
    M = {n: local["m_" + n] for n in WEIGHTS}
    V = {n: local["v_" + n] for n in WEIGHTS}

    h0 = x[0]
    target = loss_target[0]
    n_rows, dm = h0.shape
    depth = ffn1_norm.shape[0]

    ffn_seq = [(tag, layer) for layer in range(depth) for tag in ("ffn1", "ffn2")]

    def ffn_shards(tag, layer):
        return W[tag + "_w_in"][layer].astype(BF16), W[tag + "_w_out"][layer].astype(BF16)

    def ffn_views(wi, wo):
        return wi.reshape((2, N_DEV // 2) + wi.shape[1:]), wo.reshape((N_DEV // 2, 2) + wo.shape[1:])

    first_in, first_out = ffn_shards(*ffn_seq[0])
    ffn_w = {ffn_seq[0]: ffn_views(_all_gather("ag_first_w_in", first_in), _all_gather("ag_first_w_out", first_out))}

    ffn_saved = {}

    def ffn_forward(pos, h_in):
        tag, layer = ffn_seq[pos]
        comms = [("gather", a) for a in ffn_shards(*ffn_seq[pos + 1])] if pos + 1 < len(ffn_seq) else []
        res = _ffn_fwd("%s_fwd_%d" % (tag, layer), h_in, W[tag + "_norm"][layer:layer + 1], *ffn_w[ffn_seq[pos]],
                       comms=comms)
        ffn_saved[ffn_seq[pos]] = (res[1], res[2])
        if comms:
            ffn_w[ffn_seq[pos + 1]] = ffn_views(res[3], res[4])
        return res[0]

    mix_shapes = [W[n].shape for n in MIXER_BIG]

    def unpack_mixers(names, gathered):
        parts = _unpack_rows(gathered, [W[n].shape for n in names], lead=1)
        return {n: _unshard(a, SHARD_AXIS[n]) for n, a in zip(names, parts)}

    full = unpack_mixers(MIXER_EARLY, _all_gather("ag_mixers", _pack_rows([W[n].astype(BF16) for n in MIXER_EARLY], dm)))
    late_mixers = _pack_rows([W[n].astype(BF16) for n in MIXER_LATE], dm)
    small_shapes = [W[n].shape for n in SMALL_SHARDED]
    small_g = _all_gather("ag_small", _pack_rows([W[n] for n in SMALL_SHARDED], 128))
    full.update({n: _unshard(a, SHARD_AXIS[n])
                 for n, a in zip(SMALL_SHARDED, _unpack_rows(small_g, small_shapes, lead=1))})

    n_s5 = s5_w_in.shape[0]
    s5_groups = s5_lam_re.shape[1]
    heads = dm // SB_HEAD_DIM

    grads = {}
    saved = []
    h = h0

    for layer in range(depth):
        kind, j = layer % 3, layer // 3
        rec = {"h0": h}
        h = ffn_forward(2 * layer, h)
        rec["h1"] = h
        gain = mix_norm[layer:layer + 1]
        if kind == 0:
            (u,) = _mm_fwd("s5_in_%d" % layer, h, full["s5_w_in"][j], F32, gain=gain)
            pars = (s5_lam_re[j], s5_lam_im[j], s5_log_dt[j], s5_b_re[j], s5_b_im[j], s5_c_re[j], s5_c_im[j])
            mats, mats_vjp = jax.vjp(_s5_mats, *pars)
            apw, ar_fwd, ar_rev = _s5_powers(*pars[:3])
            res = _s5_fwd("s5_core_%d" % layer, u, *mats[:3], apw, ar_fwd,
                          comms=[("gather", late_mixers)] if layer == 0 else [])
            ys, sprev = res[0], res[1]
            if layer == 0:
                full.update(unpack_mixers(MIXER_LATE, res[2]))
            d_skip = full["s5_d"][j:j + 1]
            (z,) = _rowwise("s5_gelu_%d" % layer, lambda i, nt, ys_v, u_v, d_v: ((_gelu(ys_v + d_v * u_v),), ()),
                            [('t', ys, dm, 0), ('t', u, dm, 0), ('b', d_skip)], [(dm, BF16)], [], n_rows,
                            _tile(n_rows, 512))
            h, vg = _mm_fwd("s5_out_%d" % layer, z, full["s5_w_out"][j], F32, resid=h, glu=True)
            rec.update(u=u, ys=ys, sprev=sprev, z=z, vg=vg, mats=mats, mats_vjp=mats_vjp, apw=apw,
                       ar_rev=ar_rev, d_skip=d_skip)
        elif kind == 1:
            (qkv,) = _mm_fwd("sb_in_%d" % layer, h, full["sb_w_qkv"][j], BF16, gain=gain)
            o = _sb_fwd("sb_attn_%d" % layer, qkv, heads)
            (h,) = _mm_fwd("sb_out_%d" % layer, o, full["sb_w_out"][j], F32, resid=h)
            rec.update(qkv=qkv, o=o)
        else:
            (gx,) = _mm_fwd("lru_in_%d" % layer, h, full["lru_w_in"][j], F32, gain=gain)
            wa, wx = full["lru_w_a"][j], full["lru_w_x"][j]
            ba, bx = full["lru_b_a"][j].reshape(1, dm), full["lru_b_x"][j].reshape(1, dm)
            lam_row = full["lru_lambda"][j:j + 1]
            xc, r, ig, a, gated = _lru_gates_fwd("lru_gates_%d" % layer, gx, full["lru_conv_w"][j],
                                                 full["lru_conv_b"][j:j + 1], wa, ba, wx, bx, lam_row)
            hs, y = _lru_scan_fwd("lru_scan_%d" % layer, a, gated, gx)
            (h,) = _mm_fwd("lru_out_%d" % layer, y, full["lru_w_out"][j], F32, resid=h)
            rec.update(gx=gx, xc=xc, r=r, ig=ig, a=a, hs=hs, y=y, wa=wa, wx=wx, lam_row=lam_row)
        rec["h2"] = h
        h = ffn_forward(2 * layer + 1, h)
        saved.append(rec)

    dh, err2, dgf = _loss_head("loss_head", h, final_norm.reshape(1, dm), target)
    loss = lax.psum(0.5 / dm * jnp.sum(err2), ("x", "y", "c"))
    grads["final_norm"] = dgf.reshape(final_norm.shape)

    per_layer = {n: [None] * depth for n in ("ffn1_norm", "mix_norm", "ffn2_norm")}
    mixer_grads = {}
    recv_ffn = {}
    pending = []

    def ffn_backward(tag, layer, x_in, dh_in, more=None):
        comms = {0: [("exchange", pending[1])], 1: [("exchange", pending[2])]} if pending else {}
        comms.update(more or {})
        dx, dwi, dwo, dg, extra = _ffn_bwd("%s_bwd_%d" % (tag, layer), x_in, dh_in, W[tag + "_norm"][layer:layer + 1],
                                            *ffn_w[(tag, layer)], *ffn_saved[(tag, layer)], comms_by_block=comms)
        if pending:
            recv_ffn[pending[0]] = tuple(extra[:2])
            extra = extra[2:]
        pending[:] = [(tag, layer), dwi.reshape((N_DEV,) + dwi.shape[2:]).astype(BF16),
                      dwo.reshape(N_DEV, -1, dm).astype(BF16)]
        per_layer[tag + "_norm"][layer] = dg
        return dx, extra

    def put(name, j, value, count):
        mixer_grads.setdefault(name, [None] * count)[j] = value

    for layer in reversed(range(depth)):
        kind, j = layer % 3, layer // 3
        rec = saved[layer]
        dh, _ = ffn_backward("ffn2", layer, rec["h2"], dh)
        gain = mix_norm[layer:layer + 1]
        if kind == 0:
            dvg, = _rowwise("s5_glu_bwd_%d" % layer,
                            lambda i, nt, d_v, vg_v: ((jnp.concatenate(
                                [d_v * _sigmoid(vg_v[:, dm:]),
                                 d_v * vg_v[:, :dm] * _sigmoid(vg_v[:, dm:]) * (1.0 - _sigmoid(vg_v[:, dm:]))],
                                axis=1),), ()),
                            [('t', dh, dm, 0), ('t', rec["vg"], 2 * dm, 0)], [(2 * dm, BF16)], [], n_rows,
                            _tile(n_rows, 256))
            dz, dw_out = _mm_bwd("s5_out_bwd_%d" % layer, rec["z"], dvg, full["s5_w_out"][j])

            def gelu_bwd(i, nt, dz_v, ys_v, u_v, d_v):
                dy_v = dz_v * _gelu_grad(ys_v + d_v * u_v)
                return (dy_v,), (jnp.sum(dy_v * u_v, axis=0, keepdims=True),)

            dys, dd = _rowwise("s5_gelu_bwd_%d" % layer, gelu_bwd,
                               [('t', dz, dm, 0), ('t', rec["ys"], dm, 0), ('t', rec["u"], dm, 0),
                                ('b', rec["d_skip"])], [(dm, F32)], [(1, dm)], n_rows, _tile(n_rows, 512))
            m_, bm_, cm_, _ = rec["mats"]
            du_core, dm_m, dm_b, dm_c, d_a = _s5_bwd("s5_core_bwd_%d" % layer, rec["u"], dys, rec["sprev"],
                                                    m_, bm_, cm_, rec["apw"], rec["ar_rev"])
            dpars = rec["mats_vjp"]((dm_m, dm_b, dm_c, d_a.reshape(s5_groups, -1)))
            for nme, val in zip(("s5_lam_re", "s5_lam_im", "s5_log_dt", "s5_b_re", "s5_b_im", "s5_c_re", "s5_c_im"),
                                dpars):
                put(nme, j, val, n_s5)
            (du,) = _rowwise("s5_du_%d" % layer, lambda i, nt, a_v, dy_v, d_v: ((a_v + dy_v * d_v,), ()),
                             [('t', du_core, dm, 0), ('t', dys, dm, 0), ('b', rec["d_skip"])],
                             [(dm, BF16)], [], n_rows, _tile(n_rows, 512))
            dh, dw_in, dgm = _mm_bwd("s5_in_bwd_%d" % layer, rec["h1"], du, full["s5_w_in"][j], gain=gain, dres=dh)
            put("s5_d", j, dd[0], n_s5)
            put("s5_w_out", j, dw_out, n_s5)
            put("s5_w_in", j, dw_in, n_s5)
        elif kind == 1:
            do, dw_out = _mm_bwd("sb_out_bwd_%d" % layer, rec["o"], dh, full["sb_w_out"][j])
            dq, dk, dv = _sb_bwd("sb_attn_bwd_%d" % layer, rec["qkv"], rec["o"], do, heads)
            dqkv = jnp.concatenate([dq, dk, dv], axis=1).astype(BF16)
            dh, dw_in, dgm = _mm_bwd("sb_in_bwd_%d" % layer, rec["h1"], dqkv, full["sb_w_qkv"][j], gain=gain, dres=dh)
            put("sb_w_out", j, dw_out, 1)
            put("sb_w_qkv", j, dw_in, 1)
        else:
            dy, dw_out = _mm_bwd("lru_out_bwd_%d" % layer, rec["y"], dh, full["lru_w_out"][j])
            lam_t = _lru_scan_bwd("lru_scan_bwd_%d" % layer, rec["a"], dy, rec["gx"])
            dxc, dwa, dwx, dba, dbx, dsp = _lru_gates_bwd("lru_gates_bwd_%d" % layer, lam_t, rec["hs"], rec["xc"],
                                                          rec["r"], rec["ig"], rec["a"], rec["wa"], rec["wx"],
                                                          rec["lam_row"])
            dgx, dcw, dcb = _lru_conv_bwd("lru_conv_bwd_%d" % layer, dxc, rec["gx"], dy, rec["hs"],
                                          full["lru_conv_w"][j])
            dh, dw_in, dgm = _mm_bwd("lru_in_bwd_%d" % layer, rec["h1"], dgx, full["lru_w_in"][j], gain=gain, dres=dh)
            nb = rec["wa"].shape[0]
            put("lru_w_out", j, dw_out, 1)
            put("lru_w_in", j, dw_in, 1)
            put("lru_w_a", j, dwa.reshape(rec["wa"].shape), 1)
            put("lru_w_x", j, dwx.reshape(rec["wx"].shape), 1)
            put("lru_b_a", j, dba.reshape(nb, -1), 1)
            put("lru_b_x", j, dbx.reshape(nb, -1), 1)
            put("lru_conv_w", j, dcw[:LRU_CONV], 1)
            put("lru_conv_b", j, dcb[0], 1)
            put("lru_lambda", j, (dsp * -_sigmoid(-rec["lam_row"]))[0], 1)
        per_layer["mix_norm"][layer] = dgm
        more = None
        if layer == 0:
            for n, parts in mixer_grads.items():
                grads[n] = jnp.stack(parts)
            send = _pack_rows([_shard_blocks(grads[n], SHARD_AXIS[n]).astype(BF16) for n in MIXER_BIG], dm, lead=1)
            half = send.shape[1] // 2
            more = {2: [("exchange", send[:, :half])], 3: [("exchange", send[:, half:])]}
        dh, got = ffn_backward("ffn1", layer, rec["h0"], dh, more)
        if layer == 0:
            recv_mixers = jnp.concatenate(got, axis=1)

    grad_x = dh[None]
    for n in ("ffn1_norm", "mix_norm", "ffn2_norm"):
        grads[n] = jnp.concatenate(per_layer[n], axis=0)

    out_g, out_d, out_m, out_v = {}, {}, {}, {}

    def finish(names, res, shapes):
        for n, g_, d_, m_, v_ in zip(names, *[_unpack_rows(t, shapes) for t in res]):
            out_g[n], out_d[n], out_m[n], out_v[n] = g_, d_, m_, v_

    recv_ffn[pending[0]] = (_exchange("xchg_last_w_in", pending[1]), _exchange("xchg_last_w_out", pending[2]))
    for tag in ("ffn1", "ffn2"):
        for which, n in enumerate((tag + "_w_in", tag + "_w_out")):
            out_g[n], out_d[n], out_m[n], out_v[n] = _adamw_layers(
                "adamw_" + n, [recv_ffn[(tag, layer)][which] for layer in range(depth)], W[n], M[n], V[n])

    finish(MIXER_BIG, _adamw("adamw_mixers", recv_mixers,
                             *[_pack_rows([t[n] for n in MIXER_BIG], dm) for t in (W, M, V)]), mix_shapes)

    small_names = REPLICATED + SMALL_SHARDED
    small_full_shapes = [grads[n].shape for n in small_names]
    parts = _all_gather("ag_small_grads", _pack_rows([grads[n] for n in small_names], 128))
    zero = jnp.zeros(parts.shape[1:], F32)
    summed = _adamw("sum_small_grads", parts, zero, zero, zero)[0]
    small_sum = dict(zip(small_names, _unpack_rows(summed, small_full_shapes)))
    me = 4 * lax.axis_index("x") + 2 * lax.axis_index("y") + lax.axis_index("c")
    rep_shapes = [W[n].shape for n in REPLICATED]
    g_rep = _pack_rows([small_sum[n] for n in REPLICATED], 128)[None]
    finish(REPLICATED, _adamw("adamw_replicated", g_rep, *[_pack_rows([t[n] for n in REPLICATED], 128)
                                                           for t in (W, M, V)]), rep_shapes)
    g_loc = []
    for n in SMALL_SHARDED:
        ax = SHARD_AXIS[n]
        size = W[n].shape[ax]
        g_loc.append(lax.dynamic_slice_in_dim(small_sum[n], me * size, size, axis=ax))
    finish(SMALL_SHARDED, _adamw("adamw_small", _pack_rows(g_loc, 128)[None],
                                 *[_pack_rows([t[n] for n in SMALL_SHARDED], 128) for t in (W, M, V)]), small_shapes)

    return (loss, grad_x, *[out_g[n] for n in WEIGHTS], *[out_d[n] for n in WEIGHTS],
            *[out_m[n] for n in WEIGHTS], *[out_v[n] for n in WEIGHTS])
```

```python
import functools
import math

import jax
import jax.numpy as jnp
from jax import lax
from jax.experimental import pallas as pl
from jax.experimental.pallas import tpu as pltpu

F32 = jnp.float32
BF16 = jnp.bfloat16
HI = lax.Precision.HIGHEST
S5_PREC = lax.Precision.HIGH
MESH = pl.DeviceIdType.MESH

N_DEV = 8
RMS_EPS = 1e-6
S5_GROUP = 16
S5_CHUNK = 16
S5_OCTET = 128 // S5_GROUP
S5_REGROUP_ROWS = 32
FFN_ROW_PARTS = 2
S5_SCAN_UNROLL = 8
SB_HEAD_DIM = 64
SB_UNDERFLOW = -104.0
SB_HEADS_FWD = 4
SB_HEADS_BWD = 4
LRU_CONV = 4
LRU_C = 8.0
ADAM_LR, ADAM_B1, ADAM_B2, ADAM_EPS, ADAM_WD, ADAM_STEP = 0.001, 0.9, 0.999, 1e-08, 0.01, 10
VMEM_LIMIT_BYTES = 56 * 1024 * 1024
SUBLANES = 8
PACK_ROWS = 256
ADAMW_LAYER_ROWS = 192

NN = (((1,), (0,)), ((), ()))
NT = (((1,), (1,)), ((), ()))
TN = (((0,), (0,)), ((), ()))

SHARD_AXIS = dict(
    ffn1_w_in=2, ffn1_w_out=1, ffn2_w_in=2, ffn2_w_out=1, s5_w_in=1, s5_d=1, s5_w_out=2, sb_w_qkv=2, sb_w_out=1,
    lru_w_in=2, lru_conv_w=2, lru_conv_b=1, lru_w_a=2, lru_b_a=2, lru_w_x=2, lru_b_x=2, lru_lambda=1, lru_w_out=1)
MIXER_BIG = ("s5_w_in", "s5_w_out", "sb_w_qkv", "sb_w_out", "lru_w_in", "lru_w_a", "lru_w_x", "lru_w_out")
MIXER_EARLY = MIXER_BIG[:2]
MIXER_LATE = MIXER_BIG[2:]
SMALL_SHARDED = ("s5_d", "lru_conv_w", "lru_conv_b", "lru_b_a", "lru_b_x", "lru_lambda")
REPLICATED = ("ffn1_norm", "mix_norm", "ffn2_norm", "final_norm", "s5_lam_re", "s5_lam_im", "s5_log_dt",
              "s5_b_re", "s5_b_im", "s5_c_re", "s5_c_im")
WEIGHTS = ("ffn1_norm", "ffn1_w_in", "ffn1_w_out", "mix_norm", "ffn2_norm", "ffn2_w_in", "ffn2_w_out", "final_norm",
           "s5_w_in", "s5_lam_re", "s5_lam_im", "s5_log_dt", "s5_b_re", "s5_b_im", "s5_c_re", "s5_c_im", "s5_d",
           "s5_w_out", "sb_w_qkv", "sb_w_out", "lru_w_in", "lru_conv_w", "lru_conv_b", "lru_w_a", "lru_b_a",
           "lru_w_x", "lru_b_x", "lru_lambda", "lru_w_out")


def _dot(a, b, dims=NN, prec=None):
    return lax.dot_general(a, b, dims, precision=prec, preferred_element_type=F32)


def _params(*sem):
    return pltpu.CompilerParams(dimension_semantics=sem, vmem_limit_bytes=VMEM_LIMIT_BYTES)


def _tile(n, pref):
    return min(pref, n)


def _sigmoid(x):
    return jax.nn.sigmoid(x)


def _softplus(x):
    return jnp.maximum(x, 0.0) + jnp.log(1.0 + jnp.exp(-jnp.abs(x)))


_GELU_C = math.sqrt(2.0 / math.pi)


def _gelu(x):
    return 0.5 * x * (1.0 + jnp.tanh(_GELU_C * (x + 0.044715 * x * x * x)))


def _gelu_grad(x):
    t = jnp.tanh(_GELU_C * (x + 0.044715 * x * x * x))
    return 0.5 * (1.0 + t) + 0.5 * x * (1.0 - t * t) * _GELU_C * (1.0 + 3.0 * 0.044715 * x * x)


def _rms(x):
    r = lax.rsqrt(jnp.mean(x * x, axis=1, keepdims=True) + RMS_EPS)
    return r, x * r


def _rms_bwd(dhn, xhat, r, g):
    dxhat = dhn * g
    return r * (dxhat - xhat * jnp.mean(dxhat * xhat, axis=1, keepdims=True))


def _one_minus_a2_sqrt(log_a):
    t = jnp.tanh(log_a)
    return jnp.sqrt(-2.0 * t / (1.0 - t))


def _shift_down(cur, prev8, k, first):
    if k == 0:
        return cur
    row8 = lax.broadcasted_iota(jnp.int32, prev8.shape, 0)
    rolled = pltpu.roll(cur, k, 0)
    edge = jnp.where(first, 0.0, pltpu.roll(prev8, k, 0))
    top = jnp.where(row8 < k, edge, rolled[0:SUBLANES])
    return jnp.concatenate([top, rolled[SUBLANES:]], axis=0)


def _shift_up(cur, next8, k, last):
    if k == 0:
        return cur
    tm = cur.shape[0]
    row8 = lax.broadcasted_iota(jnp.int32, next8.shape, 0)
    rolled = pltpu.roll(cur, tm - k, 0)
    edge = jnp.where(last, 0.0, pltpu.roll(next8, SUBLANES - k, 0))
    bottom = jnp.where(row8 >= SUBLANES - k, edge, rolled[tm - SUBLANES:tm])
    return jnp.concatenate([rolled[:tm - SUBLANES], bottom], axis=0)


def _rowwise(name, fn, ins, out_tiled, out_acc, n_rows, tm, reverse=False):
    nt = n_rows // tm
    per8 = tm // SUBLANES
    n8 = n_rows // SUBLANES
    n_in, n_ot = len(ins), len(out_tiled)

    def pos(i):
        return nt - 1 - i if reverse else i

    in_specs, args = [], []
    for spec in ins:
        kind, arr = spec[0], spec[1]
        args.append(arr)
        if kind == 'b':
            in_specs.append(pl.BlockSpec(arr.shape, lambda i, nd=arr.ndim: (0,) * nd))
        elif kind == 't':
            in_specs.append(pl.BlockSpec((tm, spec[2]), lambda i, cb=spec[3]: (pos(i), cb)))
        elif kind == 'p':
            in_specs.append(pl.BlockSpec((SUBLANES, spec[2]),
                                         lambda i, cb=spec[3]: (jnp.maximum(pos(i) * per8 - 1, 0), cb)))
        else:
            in_specs.append(pl.BlockSpec((SUBLANES, spec[2]),
                                         lambda i, cb=spec[3]: (jnp.minimum((pos(i) + 1) * per8, n8 - 1), cb)))

    def body(*refs):
        i = pl.program_id(0)
        outs = refs[n_in:]
        touts, aouts = fn(pos(i), nt, *[r[...] for r in refs[:n_in]])
        for r, v in zip(outs[:n_ot], touts):
            r[...] = v.astype(r.dtype)
        if out_acc:
            @pl.when(i == 0)
            def _():
                for r in outs[n_ot:]:
                    r[...] = jnp.zeros(r.shape, r.dtype)
            for r, v in zip(outs[n_ot:], aouts):
                r[...] += v

    out_specs = [pl.BlockSpec((tm, n), lambda i: (pos(i), 0)) for n, _ in out_tiled]
    out_specs += [pl.BlockSpec((r, n), lambda i: (0, 0)) for r, n in out_acc]
    out_shape = [jax.ShapeDtypeStruct((n_rows, n), dt) for n, dt in out_tiled]
    out_shape += [jax.ShapeDtypeStruct((r, n), F32) for r, n in out_acc]
    return pl.pallas_call(body, grid=(nt,), in_specs=in_specs, out_specs=out_specs, out_shape=out_shape, name=name,
                          compiler_params=_params("arbitrary"))(*args)


def _all_gather(name, block):
    def body(x_ref, out_ref, send_sems, recv_sems, local_sem):
        x, y, c = lax.axis_index("x"), lax.axis_index("y"), lax.axis_index("c")
        me, sibling = (x, y, c), (x, y, 1 - c)
        chips = [(1 - x, y), (x, 1 - y), (1 - x, 1 - y)]

        def rows(px, py, pc):
            return out_ref.at[4 * px + 2 * py + pc]

        def copy(k, blk, to, src=None):
            return pltpu.make_async_remote_copy(
                src_ref=rows(*blk) if src is None else src, dst_ref=rows(*blk),
                send_sem=send_sems.at[k], recv_sem=recv_sems.at[k], device_id=to, device_id_type=MESH)

        mine = pltpu.make_async_copy(x_ref, rows(*me), local_sem)
        mine.start()
        first = [copy(0, me, sibling, src=x_ref)]
        first += [copy(1 + j, me, (*chip, c), src=x_ref) for j, chip in enumerate(chips)]
        for cp in first:
            cp.start()
        passed = [copy(4 + j, (*chip, c), sibling) for j, chip in enumerate(chips)]
        for j, chip in enumerate(chips):
            copy(1 + j, (*chip, c), me).wait_recv()
            passed[j].start()
        copy(0, sibling, me).wait_recv()
        for j, chip in enumerate(chips):
            copy(4 + j, (*chip, 1 - c), me).wait_recv()
        for cp in first + passed:
            cp.wait_send()
        mine.wait()

    return pl.pallas_call(
        body, name=name, out_shape=jax.ShapeDtypeStruct((N_DEV,) + block.shape, block.dtype),
        in_specs=[pl.BlockSpec(memory_space=pl.ANY)], out_specs=pl.BlockSpec(memory_space=pl.ANY),
        scratch_shapes=[pltpu.SemaphoreType.DMA((7,)), pltpu.SemaphoreType.DMA((7,)), pltpu.SemaphoreType.DMA(())],
    )(block)


def _exchange(name, send):
    def body(s_ref, r_ref, send_sems, recv_sems, local_sem):
        x, y, c = lax.axis_index("x"), lax.axis_index("y"), lax.axis_index("c")
        me = 4 * x + 2 * y + c
        mine = pltpu.make_async_copy(s_ref.at[me], r_ref.at[me], local_sem)
        mine.start()
        copies = []
        for k in range(1, N_DEV):
            dx, dy, dc = (k >> 2) & 1, (k >> 1) & 1, k & 1
            px = 1 - x if dx else x
            py = 1 - y if dy else y
            pc = 1 - c if dc else c
            peer = 4 * px + 2 * py + pc
            copies.append((pltpu.make_async_remote_copy(
                src_ref=s_ref.at[peer], dst_ref=r_ref.at[me], send_sem=send_sems.at[k - 1],
                recv_sem=recv_sems.at[k - 1], device_id=(px, py, pc), device_id_type=MESH), peer))
        for cp, _ in copies:
            cp.start()
        for k, (cp, peer) in enumerate(copies):
            pltpu.make_async_remote_copy(
                src_ref=s_ref.at[peer], dst_ref=r_ref.at[peer], send_sem=send_sems.at[k], recv_sem=recv_sems.at[k],
                device_id=(x, y, c), device_id_type=MESH).wait_recv()
        for cp, _ in copies:
            cp.wait_send()
        mine.wait()

    return pl.pallas_call(
        body, name=name, out_shape=jax.ShapeDtypeStruct(send.shape, send.dtype),
        in_specs=[pl.BlockSpec(memory_space=pl.ANY)], out_specs=pl.BlockSpec(memory_space=pl.ANY),
        scratch_shapes=[pltpu.SemaphoreType.DMA((7,)), pltpu.SemaphoreType.DMA((7,)), pltpu.SemaphoreType.DMA(())],
    )(send)


def _direct_copies(kind, s_ref, r_ref, send_sems, recv_sems, local_sem):
    x, y, c = lax.axis_index("x"), lax.axis_index("y"), lax.axis_index("c")
    me = 4 * x + 2 * y + c

    def src(p):
        return s_ref if kind == "gather" else s_ref.at[p]

    local = pltpu.make_async_copy(src(me), r_ref.at[me], local_sem)
    sends, recvs = [], []
    for k in range(1, N_DEV):
        px = 1 - x if (k >> 2) & 1 else x
        py = 1 - y if (k >> 1) & 1 else y
        pc = 1 - c if k & 1 else c
        peer = 4 * px + 2 * py + pc
        sends.append(pltpu.make_async_remote_copy(
            src_ref=src(peer), dst_ref=r_ref.at[me], send_sem=send_sems.at[k - 1], recv_sem=recv_sems.at[k - 1],
            device_id=(px, py, pc), device_id_type=MESH))
        recvs.append(pltpu.make_async_remote_copy(
            src_ref=src(peer), dst_ref=r_ref.at[peer], send_sem=send_sems.at[k - 1], recv_sem=recv_sems.at[k - 1],
            device_id=(x, y, c), device_id_type=MESH))
    return local, sends, recvs


def _call(body, *, grid, in_specs, out_specs, out_shape, name, args, scratch_shapes=(), semantics=None, comms=()):
    single = not isinstance(out_shape, (list, tuple))
    out_shape = [out_shape] if single else list(out_shape)
    out_specs = [out_specs] if single else list(out_specs)
    if not comms:
        res = pl.pallas_call(body, grid=grid, in_specs=in_specs, out_specs=out_specs, out_shape=out_shape, name=name,
                             scratch_shapes=list(scratch_shapes),
                             compiler_params=_params(*(semantics or ("arbitrary",) * len(grid))))(*args)
        return res[0] if single else res
    n_in, n_out, n_scr, n_c = len(args), len(out_shape), len(scratch_shapes), len(comms)

    def hosted(*refs):
        ins, srcs = refs[:n_in], refs[n_in:n_in + n_c]
        outs = refs[n_in + n_c:n_in + n_c + n_out]
        dsts = refs[n_in + n_c + n_out:n_in + 2 * n_c + n_out]
        scr = refs[n_in + 2 * n_c + n_out:n_in + 2 * n_c + n_out + n_scr]
        sems = refs[n_in + 2 * n_c + n_out + n_scr:]
        first = functools.reduce(jnp.logical_and, [pl.program_id(d) == 0 for d in range(len(grid))])
        last = functools.reduce(jnp.logical_and, [pl.program_id(d) == grid[d] - 1 for d in range(len(grid))])
        plans = [_direct_copies(comms[i][0], srcs[i], dsts[i], *sems[3 * i:3 * i + 3]) for i in range(n_c)]

        @pl.when(first)
        def _():
            for local, sends, _ in plans:
                local.start()
                for cp in sends:
                    cp.start()

        body(*ins, *outs, *scr)

        @pl.when(last)
        def _():
            for local, sends, recvs in plans:
                for cp in recvs:
                    cp.wait_recv()
                for cp in sends:
                    cp.wait_send()
                local.wait()

    any_spec = pl.BlockSpec(memory_space=pl.ANY)
    comm_shapes = [jax.ShapeDtypeStruct(((N_DEV,) + a.shape) if kind == "gather" else a.shape, a.dtype)
                   for kind, a in comms]
    sem_shapes = []
    for _ in comms:
        sem_shapes += [pltpu.SemaphoreType.DMA((7,)), pltpu.SemaphoreType.DMA((7,)), pltpu.SemaphoreType.DMA(())]
    res = pl.pallas_call(
        hosted, grid=grid, in_specs=list(in_specs) + [any_spec] * n_c, out_specs=out_specs + [any_spec] * n_c,
        out_shape=out_shape + comm_shapes, name=name, scratch_shapes=list(scratch_shapes) + sem_shapes,
        compiler_params=_params(*(("arbitrary",) * len(grid))))(*args, *[a for _, a in comms])
    return res


def _unshard(gathered, axis):
    local = gathered.shape[1:]
    full = jnp.moveaxis(gathered, 0, axis)
    return full.reshape(local[:axis] + (N_DEV * local[axis],) + local[axis + 1:])


def _shard_blocks(full, axis):
    s = full.shape
    cut = full.reshape(s[:axis] + (N_DEV, s[axis] // N_DEV) + s[axis + 1:])
    return jnp.moveaxis(cut, axis, 0)


def _mm_fwd(name, a, w, out_dtype, gain=None, resid=None, glu=False):
    n_rows, k = a.shape
    n = w.shape[1]
    tm = _tile(n_rows, 512)
    n_out = n // 2 if glu else n

    def body(*refs):
        it = iter(refs)
        a_ref, w_ref = next(it), next(it)
        g_ref = next(it) if gain is not None else None
        r_ref = next(it) if resid is not None else None
        outs = list(it)
        av = a_ref[...]
        if g_ref is not None:
            _, xhat = _rms(av)
            av = xhat * g_ref[...]
        res = _dot(av.astype(BF16), w_ref[...])
        if glu:
            outs[1][...] = res.astype(outs[1].dtype)
            res = res[:, :n_out] * _sigmoid(res[:, n_out:])
        if r_ref is not None:
            res = res + r_ref[...]
        outs[0][...] = res.astype(outs[0].dtype)

    args = [a, w]
    in_specs = [pl.BlockSpec((tm, k), lambda i: (i, 0)), pl.BlockSpec((k, n), lambda i: (0, 0))]
    if gain is not None:
        args.append(gain)
        in_specs.append(pl.BlockSpec((1, k), lambda i: (0, 0)))
    if resid is not None:
        args.append(resid)
        in_specs.append(pl.BlockSpec((tm, n_out), lambda i: (i, 0)))
    out_shape = [jax.ShapeDtypeStruct((n_rows, n_out), out_dtype)]
    out_specs = [pl.BlockSpec((tm, n_out), lambda i: (i, 0))]
    if glu:
        out_shape.append(jax.ShapeDtypeStruct((n_rows, n), F32))
        out_specs.append(pl.BlockSpec((tm, n), lambda i: (i, 0)))
    return pl.pallas_call(body, grid=(n_rows // tm,), in_specs=in_specs, out_specs=out_specs, out_shape=out_shape,
                          name=name, compiler_params=_params("parallel"))(*args)


def _mm_bwd(name, a, d, w, gain=None, dres=None):
    n_rows, k = a.shape
    n = w.shape[1]
    tm = _tile(n_rows, 512)

    def body(*refs):
        it = iter(refs)
        a_ref, d_ref, w_ref = next(it), next(it), next(it)
        g_ref = next(it) if gain is not None else None
        r_ref = next(it) if gain is not None else None
        da_ref, dw_ref = next(it), next(it)
        dg_ref = next(it) if gain is not None else None
        i = pl.program_id(0)

        @pl.when(i == 0)
        def _():
            dw_ref[...] = jnp.zeros(dw_ref.shape, F32)
            if dg_ref is not None:
                dg_ref[...] = jnp.zeros(dg_ref.shape, F32)

        av = a_ref[...]
        dv = d_ref[...].astype(BF16)
        if g_ref is not None:
            r, xhat = _rms(av)
            ab = (xhat * g_ref[...]).astype(BF16)
        else:
            ab = av.astype(BF16)
        dw_ref[...] += _dot(ab, dv, TN)
        da = _dot(dv, w_ref[...], NT)
        if g_ref is not None:
            dg_ref[...] += jnp.sum(da * xhat, axis=0, keepdims=True)
            da = r_ref[...] + _rms_bwd(da, xhat, r, g_ref[...])
        da_ref[...] = da.astype(da_ref.dtype)

    args = [a, d, w]
    in_specs = [pl.BlockSpec((tm, k), lambda i: (i, 0)), pl.BlockSpec((tm, n), lambda i: (i, 0)),
                pl.BlockSpec((k, n), lambda i: (0, 0))]
    out_shape = [jax.ShapeDtypeStruct((n_rows, k), F32), jax.ShapeDtypeStruct((k, n), F32)]
    out_specs = [pl.BlockSpec((tm, k), lambda i: (i, 0)), pl.BlockSpec((k, n), lambda i: (0, 0))]
    if gain is not None:
        args += [gain, dres]
        in_specs += [pl.BlockSpec((1, k), lambda i: (0, 0)), pl.BlockSpec((tm, k), lambda i: (i, 0))]
        out_shape.append(jax.ShapeDtypeStruct((1, k), F32))
        out_specs.append(pl.BlockSpec((1, k), lambda i: (0, 0)))
    return pl.pallas_call(body, grid=(n_rows // tm,), in_specs=in_specs, out_specs=out_specs, out_shape=out_shape,
                          name=name, compiler_params=_params("arbitrary"))(*args)


def _ffn_fwd(name, x, gain, wi, wo, comms=()):
    n_rows, dm = x.shape
    _, nj, _, fb = wi.shape
    tm = _tile(n_rows, 512)

    def body(x_ref, g_ref, wi_ref, wo_ref, y_ref, gate_ref, up_ref):
        xv = x_ref[...]
        _, xhat = _rms(xv)
        hn = (xhat * g_ref[...]).astype(BF16)
        acc = jnp.zeros((tm, dm), F32)
        for j in range(nj):
            gate = _dot(hn, wi_ref[0, j])
            up = _dot(hn, wi_ref[1, j])
            gate_ref[j] = gate.astype(BF16)
            up_ref[j] = up.astype(BF16)
            act = (gate * _sigmoid(gate) * up).astype(BF16)
            acc = acc + _dot(act, wo_ref[j].reshape(fb, dm))
        y_ref[...] = xv + 0.5 * acc

    return _call(
        body, grid=(n_rows // tm,), name=name, args=[x, gain, wi, wo], comms=comms,
        in_specs=[pl.BlockSpec((tm, dm), lambda i: (i, 0)), pl.BlockSpec((1, dm), lambda i: (0, 0)),
                  pl.BlockSpec((2, nj, dm, fb), lambda i: (0, 0, 0, 0)),
                  pl.BlockSpec((nj, 2, fb // 2, dm), lambda i: (0, 0, 0, 0))],
        out_specs=[pl.BlockSpec((tm, dm), lambda i: (i, 0)), pl.BlockSpec((nj, tm, fb), lambda i: (0, i, 0)),
                   pl.BlockSpec((nj, tm, fb), lambda i: (0, i, 0))],
        out_shape=[jax.ShapeDtypeStruct((n_rows, dm), F32), jax.ShapeDtypeStruct((nj, n_rows, fb), BF16),
                   jax.ShapeDtypeStruct((nj, n_rows, fb), BF16)])


def _ffn_bwd_block(name, x, dy, gain, wi, wo, gate_s, up_s, j, acc, comms=()):
    n_rows, dm = x.shape
    _, nj, _, fb = wi.shape
    tm = _tile(n_rows, 512)
    last = j == nj - 1

    def body(*refs):
        it = iter(refs)
        x_ref, dy_ref, g_ref, wi_ref, wo_ref = next(it), next(it), next(it), next(it), next(it)
        gate_ref, up_ref = next(it), next(it)
        acc_ref = next(it) if acc is not None else None
        out_ref, dwi16_ref, dwo16_ref = next(it), next(it), next(it)
        dg_ref = next(it) if last else None
        dwi_ref, dwo_ref = next(it), next(it)
        i = pl.program_id(0)

        @pl.when(i == 0)
        def _():
            dwi_ref[...] = jnp.zeros(dwi_ref.shape, F32)
            dwo_ref[...] = jnp.zeros(dwo_ref.shape, F32)
            if last:
                dg_ref[...] = jnp.zeros(dg_ref.shape, F32)

        g = g_ref[...]
        wg, wu, wob = wi_ref[0], wi_ref[1], wo_ref[...].reshape(fb, dm)
        parts = range(FFN_ROW_PARTS)
        rp = tm // FFN_ROW_PARTS
        rows = [slice(k * rp, (k + 1) * rp) for k in parts]
        xv = [x_ref[rows[k], :] for k in parts]
        dyv = [dy_ref[rows[k], :] for k in parts]
        rx = [_rms(xv[k]) for k in parts]
        hn = [(rx[k][1] * g).astype(BF16) for k in parts]
        dout = [(0.5 * dyv[k]).astype(BF16) for k in parts]
        dact = [_dot(dout[k], wob, NT) for k in parts]
        gate = [gate_ref[rows[k], :].astype(F32) for k in parts]
        up = [up_ref[rows[k], :].astype(F32) for k in parts]
        s = [_sigmoid(gate[k]) for k in parts]
        silu = [gate[k] * s[k] for k in parts]
        act = [(silu[k] * up[k]).astype(BF16) for k in parts]
        dgate = [(dact[k] * up[k] * (s[k] * (1.0 + gate[k] * (1.0 - s[k])))).astype(BF16) for k in parts]
        dup = [(dact[k] * silu[k]).astype(BF16) for k in parts]
        for k in parts:
            dwo_ref[...] += _dot(act[k], dout[k], TN)
            dwi_ref[0] += _dot(dgate[k], hn[k], TN)
            dwi_ref[1] += _dot(dup[k], hn[k], TN)
        tot = [_dot(dgate[k], wg, NT) + _dot(dup[k], wu, NT) for k in parts]
        for k in parts:
            t = tot[k] + acc_ref[rows[k], :] if acc_ref is not None else tot[k]
            if last:
                out_ref[rows[k], :] = dyv[k] + _rms_bwd(t, rx[k][1], rx[k][0], g)
                dg_ref[...] += jnp.sum(t * rx[k][1], axis=0, keepdims=True)
            else:
                out_ref[rows[k], :] = t

        @pl.when(i == n_rows // tm - 1)
        def _():
            dwi16_ref[...] = dwi_ref[...].astype(BF16)
            dwo16_ref[...] = dwo_ref[...].astype(BF16)

    tok = pl.BlockSpec((tm, dm), lambda i: (i, 0))
    args = [x, dy, gain, wi, wo, gate_s, up_s]
    saved = pl.BlockSpec((None, tm, fb), lambda i: (j, i, 0))
    in_specs = [tok, tok, pl.BlockSpec((1, dm), lambda i: (0, 0)),
                pl.BlockSpec((2, None, dm, fb), lambda i: (0, j, 0, 0)),
                pl.BlockSpec((None, 2, fb // 2, dm), lambda i: (j, 0, 0, 0)), saved, saved]
    if acc is not None:
        args.append(acc)
        in_specs.append(tok)
    out_specs = [tok, pl.BlockSpec((2, fb, dm), lambda i: (0, 0, 0)), pl.BlockSpec((fb, dm), lambda i: (0, 0))]
    out_shape = [jax.ShapeDtypeStruct((n_rows, dm), F32), jax.ShapeDtypeStruct((2, fb, dm), BF16),
                 jax.ShapeDtypeStruct((fb, dm), BF16)]
    if last:
        out_specs.append(pl.BlockSpec((1, dm), lambda i: (0, 0)))
        out_shape.append(jax.ShapeDtypeStruct((1, dm), F32))
    return _call(body, grid=(n_rows // tm,), name=name, in_specs=in_specs, out_specs=out_specs,
                 out_shape=out_shape, args=args, comms=comms,
                 scratch_shapes=[pltpu.VMEM((2, fb, dm), F32), pltpu.VMEM((fb, dm), F32)])


def _ffn_bwd(name, x, dy, gain, wi, wo, gate_s, up_s, comms_by_block=None):
    nj = wi.shape[1]
    acc, dwi, dwo, extra = None, [], [], []
    for j in range(nj):
        comms = (comms_by_block or {}).get(j, ())
        res = _ffn_bwd_block("%s_%d" % (name, j), x, dy, gain, wi, wo, gate_s, up_s, j, acc, comms)
        n_own = 4 if j == nj - 1 else 3
        acc = res[0]
        dwi.append(res[1])
        dwo.append(res[2])
        extra += list(res[n_own:])
        dgain = res[3] if j == nj - 1 else None
    return acc, jnp.stack(dwi, axis=1), jnp.stack(dwo, axis=0), dgain, extra


def _scan8(a, x, reverse):
    row = lax.broadcasted_iota(jnp.int32, a.shape, 0)
    for k in (1, 2, 4):
        if reverse:
            keep = row < SUBLANES - k
            a_s, x_s = pltpu.roll(a, SUBLANES - k, 0), pltpu.roll(x, SUBLANES - k, 0)
        else:
            keep = row >= k
            a_s, x_s = pltpu.roll(a, k, 0), pltpu.roll(x, k, 0)
        x = a * jnp.where(keep, x_s, 0.0) + x
        a = a * jnp.where(keep, a_s, 1.0)
    return a, x


def _scan_tile(a_ref, x_ref, h_ref, carry, reverse, rows):
    groups = rows // SUBLANES

    def step(n, c):
        gidx = groups - 1 - n if reverse else n
        sl = pl.ds(pl.multiple_of(gidx * SUBLANES, SUBLANES), SUBLANES)
        a_cum, h0 = _scan8(a_ref[sl, :], x_ref[sl, :], reverse)
        h = a_cum * c + h0
        h_ref[sl, :] = h
        return h[0:1] if reverse else h[SUBLANES - 1:SUBLANES]

    return lax.fori_loop(0, groups, step, carry)


def _s5_mats(lam_re, lam_im, log_dt, b_re, b_im, c_re, c_im):
    lc = S5_CHUNK
    groups, p = lam_re.shape
    h = b_re.shape[-1]
    lam = lax.complex(lam_re, lam_im)
    lam_dt = lam * jnp.exp(log_dt)[:, None]
    lam_bar = jnp.exp(lam_dt)
    b_bar = ((lam_bar - 1.0) / lam)[:, :, None] * lax.complex(b_re, b_im)
    c = lax.complex(c_re, c_im)
    pw = jnp.exp(lam_dt[None] * jnp.arange(lc + 1, dtype=F32)[:, None, None])
    resp = jnp.einsum('ghp,tgp,gpk->tghk', c, pw[:lc], b_bar, precision=HI).real
    s_idx = jnp.arange(lc)[:, None]
    u_idx = jnp.arange(lc)[None, :]
    onehot = (jnp.arange(lc)[:, None, None] == (u_idx - s_idx)[None]).astype(F32)
    m = jnp.einsum('tghk,tsu->gskuh', resp, onehot, precision=HI).reshape(groups, lc * h, lc * h)
    w = pw[lc - 1::-1][:lc].transpose(1, 0, 2)[:, :, None, :] * b_bar.transpose(0, 2, 1)[:, None]
    bm = jnp.concatenate([w.real, w.imag], axis=-1).reshape(groups, lc * h, 2 * p)
    v = c[:, None] * pw[1:lc + 1].transpose(1, 0, 2)[:, :, None, :]
    v = v.transpose(0, 3, 1, 2)
    cm = jnp.concatenate([v.real, -v.imag], axis=1).reshape(groups, 2 * p, lc * h)
    a = jnp.concatenate([pw[lc].real, pw[lc].imag], axis=-1)
    return m, bm, cm, a


def _s5_powers(lam_re, lam_im, log_dt):
    lam_dt = lax.complex(lam_re, lam_im) * jnp.exp(log_dt)[:, None]
    pw = jnp.exp(lam_dt[None] * (S5_CHUNK * jnp.arange(1, 9, dtype=F32))[:, None, None])

    def c1(z):
        return jnp.concatenate([z.real, z.real], axis=-1).reshape(z.shape[0], -1)

    def c2(z):
        return jnp.concatenate([-z.imag, z.imag], axis=-1).reshape(z.shape[0], -1)

    p1, p2 = c1(pw), c2(pw)
    apw = jnp.stack([p1[0], p2[0], p1[1], p2[1], p1[3], p2[3], jnp.zeros_like(p1[0]), jnp.zeros_like(p1[0])])
    fwd = jnp.concatenate([p1, p2], axis=0)
    rev = jnp.concatenate([c1(pw[::-1]), c2(pw[::-1])], axis=0)
    return apw, fwd, rev


def _cmul(c1, c2, x, half, conj=False):
    sw = pltpu.roll(x, half, 1)
    return c1 * x - c2 * sw if conj else c1 * x + c2 * sw


def _gather_groups(u_ref, ug_ref, nc):
    h = S5_GROUP
    rows = min(S5_REGROUP_ROWS, nc)

    def step(r, _):
        base = pl.multiple_of(r * rows, rows)
        for t in range(S5_CHUNK):
            val = u_ref[pl.ds(base * S5_CHUNK + t, rows, stride=S5_CHUNK), :]
            for g in range(S5_OCTET):
                ug_ref[g, pl.ds(base, rows), t * h:(t + 1) * h] = val[:, g * h:(g + 1) * h]
        return 0

    lax.fori_loop(0, nc // rows, step, 0)


def _scatter_groups(yg_ref, y_ref, nc):
    h = S5_GROUP
    rows = min(S5_REGROUP_ROWS, nc)

    def step(r, _):
        base = pl.multiple_of(r * rows, rows)
        for t in range(S5_CHUNK):
            y_ref[pl.ds(base * S5_CHUNK + t, rows, stride=S5_CHUNK), :] = jnp.concatenate(
                [yg_ref[g, pl.ds(base, rows), t * h:(t + 1) * h] for g in range(S5_OCTET)], axis=1)
        return 0

    lax.fori_loop(0, nc // rows, step, 0)


def _s5_fwd(name, u, m, bm, cm, apw, arows, comms=()):
    n_rows, width = u.shape
    nc = n_rows // S5_CHUNK
    groups, lh, _ = m.shape
    p2 = bm.shape[2]
    gb = S5_OCTET
    lanes = gb * S5_GROUP

    def body(u_ref, m_ref, b_ref, c_ref, apw_ref, ar_ref, y_ref, sp_ref, ug_ref, yg_ref, xs_ref):
        _gather_groups(u_ref, ug_ref, nc)
        for gi in range(gb):
            xs_ref[:, gi * p2:(gi + 1) * p2] = _dot(ug_ref[gi], b_ref[gi], prec=S5_PREC)
        row = lax.broadcasted_iota(jnp.int32, (SUBLANES, p2), 0)

        def group(n, carry):
            sl = pl.ds(pl.multiple_of(n * SUBLANES, SUBLANES), SUBLANES)
            new = []
            for gi in range(gb):
                ln = slice(gi * p2, (gi + 1) * p2)
                x = xs_ref[sl, ln]
                for q, k in enumerate((1, 2, 4)):
                    xs = jnp.where(row >= k, pltpu.roll(x, k, 0), 0.0)
                    x = x + _cmul(apw_ref[2 * q:2 * q + 1, ln], apw_ref[2 * q + 1:2 * q + 2, ln], xs, p2 // 2)
                cb = jnp.broadcast_to(carry[gi], (SUBLANES, p2))
                s8 = x + _cmul(ar_ref[0:8, ln], ar_ref[8:16, ln], cb, p2 // 2)
                sp_ref[sl, ln] = jnp.where(row >= 1, pltpu.roll(s8, 1, 0), cb)
                new.append(s8[SUBLANES - 1:SUBLANES])
            return tuple(new)

        unroll = min(S5_SCAN_UNROLL, nc // SUBLANES)

        def step(n, carry):
            for k in range(unroll):
                carry = group(n * unroll + k, carry)
            return carry

        lax.fori_loop(0, nc // (SUBLANES * unroll), step, tuple(jnp.zeros((1, p2), F32) for _ in range(gb)))
        for gi in range(gb):
            yg_ref[gi] = (_dot(ug_ref[gi], m_ref[gi], prec=S5_PREC)
                          + _dot(sp_ref[:, gi * p2:(gi + 1) * p2], c_ref[gi], prec=S5_PREC))
        _scatter_groups(yg_ref, y_ref, nc)

    tok = pl.BlockSpec((n_rows, lanes), lambda g: (0, g), pipeline_mode=pl.Buffered(1))
    return _call(
        body, grid=(groups // gb,), name=name, args=[u, m, bm, cm, apw, arows], comms=comms, semantics=("parallel",),
        in_specs=[tok, pl.BlockSpec((gb, lh, lh), lambda g: (g, 0, 0)),
                  pl.BlockSpec((gb, lh, p2), lambda g: (g, 0, 0)), pl.BlockSpec((gb, p2, lh), lambda g: (g, 0, 0)),
                  pl.BlockSpec((8, gb * p2), lambda g: (0, g)), pl.BlockSpec((16, gb * p2), lambda g: (0, g))],
        out_specs=[tok, pl.BlockSpec((nc, gb * p2), lambda g: (0, g))],
        out_shape=[jax.ShapeDtypeStruct((n_rows, width), F32), jax.ShapeDtypeStruct((nc, groups * p2), F32)],
        scratch_shapes=[pltpu.VMEM((gb, nc, lh), F32), pltpu.VMEM((gb, nc, lh), F32), pltpu.VMEM((nc, gb * p2), F32)])


def _s5_bwd(name, u, dy, sprev, m, bm, cm, apw, arows_rev):
    n_rows, width = u.shape
    nc = n_rows // S5_CHUNK
    groups, lh, _ = m.shape
    p2 = bm.shape[2]
    half = p2 // 2
    gb = S5_OCTET
    lanes = gb * S5_GROUP

    def body(u_ref, dy_ref, sp_ref, m_ref, b_ref, c_ref, apw_ref, ar_ref,
             du_ref, dm_ref, db_ref, dc_ref, da_ref, ug_ref, dyg_ref, ds_ref, gx_ref):
        _gather_groups(u_ref, ug_ref, nc)
        _gather_groups(dy_ref, dyg_ref, nc)
        for gi in range(gb):
            ds_ref[:, gi * p2:(gi + 1) * p2] = _dot(dyg_ref[gi], c_ref[gi], NT, prec=S5_PREC)
        row = lax.broadcasted_iota(jnp.int32, (SUBLANES, p2), 0)
        lane = lax.broadcasted_iota(jnp.int32, (SUBLANES, p2), 1)
        ngroups = nc // SUBLANES

        def group(n, state):
            carry, nxt, dacc = state
            sl = pl.ds(pl.multiple_of((ngroups - 1 - n) * SUBLANES, SUBLANES), SUBLANES)
            new_c, new_n, new_d = [], [], []
            for gi in range(gb):
                ln = slice(gi * p2, (gi + 1) * p2)
                d8 = ds_ref[sl, ln]
                x = jnp.where(row < SUBLANES - 1, pltpu.roll(d8, SUBLANES - 1, 0),
                              jnp.broadcast_to(nxt[gi], (SUBLANES, p2)))
                for q, k in enumerate((1, 2, 4)):
                    xs = jnp.where(row < SUBLANES - k, pltpu.roll(x, SUBLANES - k, 0), 0.0)
                    x = x + _cmul(apw_ref[2 * q:2 * q + 1, ln], apw_ref[2 * q + 1:2 * q + 2, ln], xs, half, conj=True)
                cb = jnp.broadcast_to(carry[gi], (SUBLANES, p2))
                g8 = x + _cmul(ar_ref[0:8, ln], ar_ref[8:16, ln], cb, half, conj=True)
                gx_ref[sl, ln] = g8
                s8 = sp_ref[sl, ln]
                p1 = g8 * s8
                pq = g8 * pltpu.roll(s8, half, 1)
                d_a = jnp.where(lane < half, p1 + pltpu.roll(p1, half, 1), pq - pltpu.roll(pq, half, 1))
                new_c.append(g8[0:1])
                new_n.append(d8[0:1])
                new_d.append(dacc[gi] + jnp.sum(d_a, axis=0, keepdims=True))
            return tuple(new_c), tuple(new_n), tuple(new_d)

        unroll = min(S5_SCAN_UNROLL, ngroups)

        def step(n, state):
            for k in range(unroll):
                state = group(n * unroll + k, state)
            return state

        zeros = tuple(jnp.zeros((1, p2), F32) for _ in range(gb))
        _, _, dacc = lax.fori_loop(0, ngroups // unroll, step, (zeros, zeros, zeros))
        for gi in range(gb):
            ln = slice(gi * p2, (gi + 1) * p2)
            da_ref[:, ln] = dacc[gi]
            ug, dyg, gxg = ug_ref[gi], dyg_ref[gi], gx_ref[:, ln]
            dm_ref[gi] = _dot(ug, dyg, TN, prec=S5_PREC)
            dc_ref[gi] = _dot(sp_ref[:, ln], dyg, TN, prec=S5_PREC)
            db_ref[gi] = _dot(ug, gxg, TN, prec=S5_PREC)
            dyg_ref[gi] = _dot(dyg, m_ref[gi], NT, prec=S5_PREC) + _dot(gxg, b_ref[gi], NT, prec=S5_PREC)
        _scatter_groups(dyg_ref, du_ref, nc)

    tok = pl.BlockSpec((n_rows, lanes), lambda g: (0, g), pipeline_mode=pl.Buffered(1))
    tok_s = pl.BlockSpec((nc, gb * p2), lambda g: (0, g))
    mat_m = pl.BlockSpec((gb, lh, lh), lambda g: (g, 0, 0))
    mat_b = pl.BlockSpec((gb, lh, p2), lambda g: (g, 0, 0))
    mat_c = pl.BlockSpec((gb, p2, lh), lambda g: (g, 0, 0))
    return pl.pallas_call(
        body, grid=(groups // gb,), name=name,
        in_specs=[tok, tok, tok_s, mat_m, mat_b, mat_c,
                  pl.BlockSpec((8, gb * p2), lambda g: (0, g)), pl.BlockSpec((16, gb * p2), lambda g: (0, g))],
        out_specs=[tok, mat_m, mat_b, mat_c, pl.BlockSpec((1, gb * p2), lambda g: (0, g))],
        out_shape=[jax.ShapeDtypeStruct((n_rows, width), F32), jax.ShapeDtypeStruct(m.shape, F32),
                   jax.ShapeDtypeStruct(bm.shape, F32), jax.ShapeDtypeStruct(cm.shape, F32),
                   jax.ShapeDtypeStruct((1, groups * p2), F32)],
        scratch_shapes=[pltpu.VMEM((gb, nc, lh), F32), pltpu.VMEM((gb, nc, lh), F32),
                        pltpu.VMEM((nc, gb * p2), F32), pltpu.VMEM((nc, gb * p2), F32)],
        compiler_params=_params("parallel"),
    )(u, dy, sprev, m, bm, cm, apw, arows_rev)


def _split(x):
    hi = x.astype(BF16)
    return hi, (x - hi.astype(F32)).astype(BF16)


def _sb_more(kb, carries):
    top = jnp.max(carries[0])
    for c in carries[1:]:
        top = jnp.maximum(top, jnp.max(c))
    return (kb >= 0) & (top > SB_UNDERFLOW)


def _sb_fwd(name, qkv, heads):
    n_rows, dm3 = qkv.shape
    dm = dm3 // 3
    hd = dm // heads
    tq = _tile(n_rows // 2, 256)
    hb = min(SB_HEADS_FWD, heads)
    groups = heads // hb
    scale = hd ** -0.5

    def body(q_ref, k_ref, v_ref, o_ref):
        qi = pl.program_id(1)
        hs = range(hb)
        row = lax.broadcasted_iota(jnp.int32, (tq, tq), 0)
        col = lax.broadcasted_iota(jnp.int32, (tq, tq), 1)
        tri = (row > col).astype(BF16)
        causal = col < row
        qall = q_ref[...] * scale
        qb = [qall[:, h * hd:(h + 1) * hd] for h in hs]

        def block(kb, carries, accs, diagonal):
            ks = pl.ds(pl.multiple_of(kb * tq, tq), tq)
            kblk, vblk = k_ref[ks, :], v_ref[ks, :]
            z = [_dot(qb[h], kblk[:, h * hd:(h + 1) * hd], NT) for h in hs]
            sp = [_softplus(z[h]) for h in hs]
            lk = [-sp[h] for h in hs]
            if diagonal:
                lk = [jnp.where(causal, lk[h], 0.0) for h in hs]
            parts = [_split(lk[h]) for h in hs]
            r = [_dot(parts[h][0], tri) + _dot(parts[h][1], tri) for h in hs]
            a = [jnp.exp(z[h] - sp[h] + r[h] + carries[h]) for h in hs]
            if diagonal:
                a = [jnp.where(causal, a[h], 0.0) for h in hs]
            new_a = tuple(accs[h] + _dot(a[h].astype(BF16), vblk[:, h * hd:(h + 1) * hd]) for h in hs)
            new_c = tuple(carries[h] + jnp.sum(lk[h], axis=1, keepdims=True) for h in hs)
            return new_c, new_a

        zc = tuple(jnp.zeros((tq, 1), F32) for _ in hs)
        za = tuple(jnp.zeros((tq, hd), F32) for _ in hs)
        carries, accs = block(qi, zc, za, True)
        _, _, accs = lax.while_loop(lambda st: _sb_more(st[0], st[1]),
                                    lambda st: (st[0] - 1,) + block(st[0], st[1], st[2], False),
                                    (qi - 1, carries, accs))
        o_ref[...] = jnp.concatenate(accs, axis=1)

    lanes = hb * hd
    return pl.pallas_call(
        body, grid=(groups, n_rows // tq), name=name,
        in_specs=[pl.BlockSpec((tq, lanes), lambda g, i: (i, g)),
                  pl.BlockSpec((n_rows, lanes), lambda g, i: (0, groups + g)),
                  pl.BlockSpec((n_rows, lanes), lambda g, i: (0, 2 * groups + g))],
        out_specs=pl.BlockSpec((tq, lanes), lambda g, i: (i, g)),
        out_shape=jax.ShapeDtypeStruct((n_rows, dm), F32),
        compiler_params=_params("parallel", "arbitrary"))(qkv, qkv, qkv)


def _sb_bwd(name, qkv, o, do, heads):
    n_rows, dm3 = qkv.shape
    dm = dm3 // 3
    hd = dm // heads
    tq = _tile(n_rows // 2, 256)
    hb = min(SB_HEADS_BWD, heads)
    groups = heads // hb
    scale = hd ** -0.5

    def body(q_ref, k_ref, v_ref, o_ref, do_ref, dq_ref, dk_ref, dv_ref):
        qi = pl.program_id(1)

        @pl.when(qi == 0)
        def _():
            dk_ref[...] = jnp.zeros(dk_ref.shape, F32)
            dv_ref[...] = jnp.zeros(dv_ref.shape, F32)

        hs = range(hb)
        cols = [slice(h * hd, (h + 1) * hd) for h in hs]
        qall = q_ref[...] * scale
        doall = do_ref[...].astype(BF16)
        prod = doall.astype(F32) * o_ref[...]
        qb = [qall[:, cols[h]] for h in hs]
        dob16 = [doall[:, cols[h]] for h in hs]
        delta = [jnp.sum(prod[:, cols[h]], axis=1, keepdims=True) for h in hs]
        row = lax.broadcasted_iota(jnp.int32, (tq, tq), 0)
        col = lax.broadcasted_iota(jnp.int32, (tq, tq), 1)
        tri = (row > col).astype(BF16)
        tri_incl = (row >= col).astype(BF16)
        causal = col < row

        def block(kb, carries, pcarries, dqs, diagonal):
            ks = pl.ds(pl.multiple_of(kb * tq, tq), tq)
            kall, vall = k_ref[ks, :], v_ref[ks, :]
            kblk = [kall[:, cols[h]] for h in hs]
            vblk = [vall[:, cols[h]] for h in hs]
            z = [_dot(qb[h], kblk[h], NT) for h in hs]
            da = [_dot(dob16[h], vblk[h], NT) for h in hs]
            sp = [_softplus(z[h]) for h in hs]
            lk = [-sp[h] for h in hs]
            if diagonal:
                lk = [jnp.where(causal, lk[h], 0.0) for h in hs]
            lb = [z[h] - sp[h] for h in hs]
            parts = [_split(lk[h]) for h in hs]
            r = [_dot(parts[h][0], tri) + _dot(parts[h][1], tri) for h in hs]
            a = [jnp.exp(lb[h] + r[h] + carries[h]) for h in hs]
            if diagonal:
                a = [jnp.where(causal, a[h], 0.0) for h in hs]
            a16 = [a[h].astype(BF16) for h in hs]
            p = [da[h] * a16[h].astype(F32) for h in hs]
            pparts = [_split(p[h]) for h in hs]
            pc = [_dot(pparts[h][0], tri_incl) + _dot(pparts[h][1], tri_incl) for h in hs]
            beta = [jnp.exp(lb[h]) for h in hs]
            dz = [p[h] * (1.0 - beta[h]) - beta[h] * (delta[h] - pcarries[h] - pc[h]) for h in hs]
            if diagonal:
                dz = [jnp.where(causal, dz[h], 0.0) for h in hs]
            dz16 = [dz[h].astype(BF16) for h in hs]
            dk_ref[ks, :] += jnp.concatenate([_dot(dz16[h], qb[h], TN) for h in hs], axis=1)
            dv_ref[ks, :] += jnp.concatenate([_dot(a16[h], dob16[h], TN) for h in hs], axis=1)
            return (tuple(carries[h] + jnp.sum(lk[h], axis=1, keepdims=True) for h in hs),
                    tuple(pcarries[h] + jnp.sum(p[h], axis=1, keepdims=True) for h in hs),
                    tuple(dqs[h] + _dot(dz16[h], kblk[h]) for h in hs))

        zc = tuple(jnp.zeros((tq, 1), F32) for _ in hs)
        zq = tuple(jnp.zeros((tq, hd), F32) for _ in hs)
        st = block(qi, zc, zc, zq, True)
        st = lax.while_loop(lambda s: _sb_more(s[0], s[1]),
                            lambda s: (s[0] - 1,) + block(s[0], s[1], s[2], s[3], False), (qi - 1,) + st)
        dq_ref[...] = jnp.concatenate(st[3], axis=1) * scale

    lanes = hb * hd
    tile = pl.BlockSpec((tq, lanes), lambda g, i: (i, g))
    whole = pl.BlockSpec((n_rows, lanes), lambda g, i: (0, g), pipeline_mode=pl.Buffered(1))
    full = jax.ShapeDtypeStruct((n_rows, dm), F32)
    return pl.pallas_call(
        body, grid=(groups, n_rows // tq), name=name,
        in_specs=[tile, pl.BlockSpec((n_rows, lanes), lambda g, i: (0, groups + g), pipeline_mode=pl.Buffered(1)),
                  pl.BlockSpec((n_rows, lanes), lambda g, i: (0, 2 * groups + g), pipeline_mode=pl.Buffered(1)),
                  tile, tile],
        out_specs=[tile, whole, whole], out_shape=[full, full, full],
        compiler_params=_params("parallel", "arbitrary"))(qkv, qkv, qkv, o, do)


def _block_diag(xb, w_ref_val, dims):
    nb = w_ref_val.shape[0]
    bw = xb.shape[1] // nb
    return jnp.concatenate([_dot(xb[:, n * bw:(n + 1) * bw], w_ref_val[n], dims) for n in range(nb)], axis=1)


def _lru_gates_fwd(name, gx, conv_w, conv_b, wa, ba, wx, bx, lam):
    n_rows, w2 = gx.shape
    w = w2 // 2
    tm = _tile(n_rows, 256)

    def fn(i, nt, br, prev, cw, cb, wa_v, ba_v, wx_v, bx_v, lam_v):
        xc = cb + sum(cw[k:k + 1] * _shift_down(br, prev, LRU_CONV - 1 - k, i == 0) for k in range(LRU_CONV))
        xb = xc.astype(BF16)
        r = _sigmoid(_block_diag(xb, wa_v, NN) + ba_v)
        ig = _sigmoid(_block_diag(xb, wx_v, NN) + bx_v)
        log_a = (-LRU_C * r) * _softplus(-lam_v)
        a = jnp.exp(log_a)
        gated = (ig * xc) * _one_minus_a2_sqrt(log_a)
        return (xc, r, ig, a, gated), ()

    return _rowwise(name, fn, [('t', gx, w, 1), ('p', gx, w, 1), ('b', conv_w), ('b', conv_b), ('b', wa), ('b', ba),
                               ('b', wx), ('b', bx), ('b', lam)], [(w, F32)] * 5, [], n_rows, tm)


def _lru_scan_fwd(name, a, gated, gx):
    n_rows, w = a.shape
    tm = _tile(n_rows, 256)

    def body(a_ref, x_ref, bg_ref, hs_ref, y_ref, carry_ref):
        @pl.when(pl.program_id(0) == 0)
        def _():
            carry_ref[...] = jnp.zeros(carry_ref.shape, F32)
        carry_ref[...] = _scan_tile(a_ref, x_ref, hs_ref, carry_ref[...], False, tm)
        y_ref[...] = (_gelu(bg_ref[...]) * hs_ref[...]).astype(BF16)

    tok = pl.BlockSpec((tm, w), lambda i: (i, 0))
    return pl.pallas_call(body, grid=(n_rows // tm,), in_specs=[tok, tok, tok], out_specs=[tok, tok], name=name,
                          out_shape=[jax.ShapeDtypeStruct((n_rows, w), F32), jax.ShapeDtypeStruct((n_rows, w), BF16)],
                          scratch_shapes=[pltpu.VMEM((1, w), F32)], compiler_params=_params("arbitrary"))(a, gated, gx)


def _lru_scan_bwd(name, a, dy, gx):
    n_rows, w = a.shape
    tm = _tile(n_rows, 256)
    nt = n_rows // tm
    per8 = tm // SUBLANES

    def body(a_ref, an_ref, dy_ref, bg_ref, lam_ref, carry_ref, aup_ref, dhs_ref):
        i = pl.program_id(0)

        @pl.when(i == 0)
        def _():
            carry_ref[...] = jnp.zeros(carry_ref.shape, F32)
        aup_ref[...] = _shift_up(a_ref[...], an_ref[...], 1, i == 0)
        dhs_ref[...] = dy_ref[...] * _gelu(bg_ref[...])
        carry_ref[...] = _scan_tile(aup_ref, dhs_ref, lam_ref, carry_ref[...], True, tm)

    tok = pl.BlockSpec((tm, w), lambda i: (nt - 1 - i, 0))
    nxt = pl.BlockSpec((SUBLANES, w), lambda i: (jnp.minimum((nt - i) * per8, n_rows // SUBLANES - 1), 0))
    return pl.pallas_call(body, grid=(nt,), in_specs=[tok, nxt, tok, tok], out_specs=tok, name=name,
                          out_shape=jax.ShapeDtypeStruct((n_rows, w), F32),
                          scratch_shapes=[pltpu.VMEM((1, w), F32), pltpu.VMEM((tm, w), F32), pltpu.VMEM((tm, w), F32)],
                          compiler_params=_params("arbitrary"))(a, a, dy, gx)


def _lru_gates_bwd(name, lam_t, hs, xc, r, ig, a, wa, wx, lam):
    n_rows, w = xc.shape
    nb, bw, _ = wa.shape
    tm = _tile(n_rows, 256)

    def fn(i, nt, lt, hs_v, hs_prev, xc_v, r_v, ig_v, a_v, wa_v, wx_v, lam_v):
        sp = _softplus(-lam_v)
        log_a = (-LRU_C * r_v) * sp
        mult = _one_minus_a2_sqrt(log_a)
        d_a = lt * _shift_down(hs_v, hs_prev, 1, i == 0)
        d_ig = lt * xc_v * mult
        d_mult = lt * ig_v * xc_v
        d_log_a = d_a * a_v - d_mult * (a_v * a_v) / mult
        d_ra = d_log_a * (-LRU_C * sp) * r_v * (1.0 - r_v)
        d_ia = d_ig * ig_v * (1.0 - ig_v)
        d_ra16, d_ia16, xb = d_ra.astype(BF16), d_ia.astype(BF16), xc_v.astype(BF16)
        dxc = lt * ig_v * mult + _block_diag(d_ra16, wa_v, NT) + _block_diag(d_ia16, wx_v, NT)
        dwa = jnp.concatenate([_dot(xb[:, n * bw:(n + 1) * bw], d_ra16[:, n * bw:(n + 1) * bw], TN)
                               for n in range(nb)], axis=0)
        dwx = jnp.concatenate([_dot(xb[:, n * bw:(n + 1) * bw], d_ia16[:, n * bw:(n + 1) * bw], TN)
                               for n in range(nb)], axis=0)
        col = lambda t: jnp.sum(t, axis=0, keepdims=True)
        return (dxc,), (dwa, dwx, col(d_ra), col(d_ia), col(d_log_a * (-LRU_C * r_v)))

    tiled = lambda arr: ('t', arr, w, 0)
    return _rowwise(name, fn, [tiled(lam_t), tiled(hs), ('p', hs, w, 0), tiled(xc), tiled(r), tiled(ig), tiled(a),
                               ('b', wa), ('b', wx), ('b', lam)],
                    [(w, F32)], [(nb * bw, bw), (nb * bw, bw), (1, w), (1, w), (1, w)], n_rows, tm)


def _lru_conv_bwd(name, dxc, gx, dy, hs, conv_w):
    n_rows, w = dxc.shape
    tm = _tile(n_rows, 256)

    def fn(i, nt, dxc_v, dxc_next, bg, br, br_prev, dy_v, hs_v, cw):
        dbr = sum(cw[k:k + 1] * _shift_up(dxc_v, dxc_next, LRU_CONV - 1 - k, i == nt - 1) for k in range(LRU_CONV))
        dbg = dy_v * hs_v * _gelu_grad(bg)
        dcw = [jnp.sum(dxc_v * _shift_down(br, br_prev, LRU_CONV - 1 - k, i == 0), axis=0, keepdims=True)
               for k in range(LRU_CONV)]
        dcw = jnp.concatenate(dcw + [jnp.zeros((SUBLANES - LRU_CONV, w), F32)], axis=0)
        return (jnp.concatenate([dbg, dbr], axis=1),), (dcw, jnp.sum(dxc_v, axis=0, keepdims=True))

    return _rowwise(name, fn, [('t', dxc, w, 0), ('n', dxc, w, 0), ('t', gx, w, 0), ('t', gx, w, 1), ('p', gx, w, 1),
                               ('t', dy, w, 0), ('t', hs, w, 0), ('b', conv_w)],
                    [(2 * w, BF16)], [(SUBLANES, w), (1, w)], n_rows, tm)


def _loss_head(name, h, gain, target):
    n_rows, dm = h.shape
    tm = _tile(n_rows, 512)

    def fn(i, nt, hv, tv, g):
        r, xhat = _rms(hv)
        err = xhat * g - tv
        dy = err * (1.0 / dm)
        return ((_rms_bwd(dy, xhat, r, g),),
                (jnp.sum(err * err, axis=0, keepdims=True), jnp.sum(dy * xhat, axis=0, keepdims=True)))

    return _rowwise(name, fn, [('t', h, dm, 0), ('t', target, dm, 0), ('b', gain)], [(dm, F32)], [(1, dm), (1, dm)],
                    n_rows, tm)


def _adamw(name, gparts, w, m, v):
    n_parts, n_rows, cols = gparts.shape
    tr = n_rows
    for cand in (256, 128, 64, 32, 16, 8):
        if n_rows % cand == 0:
            tr = cand
            break
    c1 = 1.0 - ADAM_B1 ** ADAM_STEP
    c2 = 1.0 - ADAM_B2 ** ADAM_STEP

    def body(gp_ref, w_ref, m_ref, v_ref, g_ref, d_ref, nm_ref, nv_ref):
        g = gp_ref[0].astype(F32)
        for p in range(1, n_parts):
            g = g + gp_ref[p].astype(F32)
        m_new = ADAM_B1 * m_ref[...] + (1.0 - ADAM_B1) * g
        v_new = ADAM_B2 * v_ref[...] + (1.0 - ADAM_B2) * (g * g)
        m_hat = m_new / c1
        v_hat = v_new / c2
        g_ref[...] = g
        d_ref[...] = -ADAM_LR * (m_hat / (jnp.sqrt(v_hat) + ADAM_EPS) + ADAM_WD * w_ref[...])
        nm_ref[...] = m_new
        nv_ref[...] = v_new

    blk = pl.BlockSpec((tr, cols), lambda i: (i, 0))
    shp = jax.ShapeDtypeStruct((n_rows, cols), F32)
    return pl.pallas_call(body, grid=(n_rows // tr,), name=name,
                          in_specs=[pl.BlockSpec((n_parts, tr, cols), lambda i: (0, i, 0)), blk, blk, blk],
                          out_specs=[blk, blk, blk, blk], out_shape=[shp, shp, shp, shp],
                          compiler_params=_params("parallel"))(gparts, w, m, v)


def _adamw_layers(name, recvs, w, m, v):
    n_layers, n_rows, cols = w.shape
    n_parts = recvs[0].shape[0]
    tr = max(t for t in range(16, ADAMW_LAYER_ROWS + 1, 16) if n_rows % t == 0)
    c1 = 1.0 - ADAM_B1 ** ADAM_STEP
    c2 = 1.0 - ADAM_B2 ** ADAM_STEP

    def body(*refs):
        gp_refs = refs[:n_layers]
        w_ref, m_ref, v_ref, g_ref, d_ref, nm_ref, nv_ref = refs[n_layers:]
        layer = pl.program_id(0)
        for k in range(n_layers):
            @pl.when(layer == k)
            def _(k=k):
                g = gp_refs[k][0].astype(F32)
                for p in range(1, n_parts):
                    g = g + gp_refs[k][p].astype(F32)
                m_new = ADAM_B1 * m_ref[...] + (1.0 - ADAM_B1) * g
                v_new = ADAM_B2 * v_ref[...] + (1.0 - ADAM_B2) * (g * g)
                g_ref[...] = g
                d_ref[...] = -ADAM_LR * ((m_new / c1) / (jnp.sqrt(v_new / c2) + ADAM_EPS) + ADAM_WD * w_ref[...])
                nm_ref[...] = m_new
                nv_ref[...] = v_new

    blk = pl.BlockSpec((None, tr, cols), lambda l, i: (l, i, 0))
    shp = jax.ShapeDtypeStruct((n_layers, n_rows, cols), F32)
    gp_specs = [pl.BlockSpec((n_parts, tr, cols), lambda l, i, k=k: (0, jnp.where(l == k, i, 0), 0))
                for k in range(n_layers)]
    return pl.pallas_call(body, grid=(n_layers, n_rows // tr), name=name, in_specs=gp_specs + [blk, blk, blk],
                          out_specs=[blk, blk, blk, blk], out_shape=[shp, shp, shp, shp],
                          compiler_params=_params("arbitrary", "arbitrary"))(*recvs, w, m, v)


def _pack_rows(arrays, cols, lead=0):
    flat = [a.reshape(a.shape[:lead] + (-1,)) for a in arrays]
    cat = jnp.concatenate(flat, axis=lead) if len(flat) > 1 else flat[0]
    n = cat.shape[lead]
    pad = (-n) % (cols * PACK_ROWS)
    if pad:
        cat = jnp.pad(cat, [(0, 0)] * lead + [(0, pad)])
    return cat.reshape(cat.shape[:lead] + (-1, cols))


def _unpack_rows(packed, shapes, lead=0):
    flat = packed.reshape(packed.shape[:lead] + (-1,))
    out, off = [], 0
    for s in shapes:
        n = math.prod(s)
        out.append(lax.slice_in_dim(flat, off, off + n, axis=lead).reshape(flat.shape[:lead] + tuple(s)))
        off += n
    return out


def kernel(x, ffn1_norm, ffn1_w_in, ffn1_w_out, mix_norm, ffn2_norm, ffn2_w_in, ffn2_w_out, final_norm, s5_w_in, s5_lam_re, s5_lam_im, s5_log_dt, s5_b_re, s5_b_im, s5_c_re, s5_c_im, s5_d, s5_w_out, sb_w_qkv, sb_w_out, lru_w_in, lru_conv_w, lru_conv_b, lru_w_a, lru_b_a, lru_w_x, lru_b_x, lru_lambda, lru_w_out, loss_target, m_ffn1_norm, m_ffn1_w_in, m_ffn1_w_out, m_mix_norm, m_ffn2_norm, m_ffn2_w_in, m_ffn2_w_out, m_final_norm, m_s5_w_in, m_s5_lam_re, m_s5_lam_im, m_s5_log_dt, m_s5_b_re, m_s5_b_im, m_s5_c_re, m_s5_c_im, m_s5_d, m_s5_w_out, m_sb_w_qkv, m_sb_w_out, m_lru_w_in, m_lru_conv_w, m_lru_conv_b, m_lru_w_a, m_lru_b_a, m_lru_w_x, m_lru_b_x, m_lru_lambda, m_lru_w_out, v_ffn1_norm, v_ffn1_w_in, v_ffn1_w_out, v_mix_norm, v_ffn2_norm, v_ffn2_w_in, v_ffn2_w_out, v_final_norm, v_s5_w_in, v_s5_lam_re, v_s5_lam_im, v_s5_log_dt, v_s5_b_re, v_s5_b_im, v_s5_c_re, v_s5_c_im, v_s5_d, v_s5_w_out, v_sb_w_qkv, v_sb_w_out, v_lru_w_in, v_lru_conv_w, v_lru_conv_b, v_lru_w_a, v_lru_b_a, v_lru_w_x, v_lru_b_x, v_lru_lambda, v_lru_w_out):
    local = dict(locals())
    W = {n: local[n] for n in WEIGHTS}
    M = {n: local["m_" + n] for n in WEIGHTS}
    V = {n: local["v_" + n] for n in WEIGHTS}

    h0 = x[0]
    target = loss_target[0]
    n_rows, dm = h0.shape
    depth = ffn1_norm.shape[0]

    ffn_seq = [(tag, layer) for layer in range(depth) for tag in ("ffn1", "ffn2")]

    def ffn_shards(tag, layer):
        return W[tag + "_w_in"][layer].astype(BF16), W[tag + "_w_out"][layer].astype(BF16)

    def ffn_views(wi, wo):
        return wi.reshape((2, N_DEV // 2) + wi.shape[1:]), wo.reshape((N_DEV // 2, 2) + wo.shape[1:])

    first_in, first_out = ffn_shards(*ffn_seq[0])
    ffn_w = {ffn_seq[0]: ffn_views(_all_gather("ag_first_w_in", first_in), _all_gather("ag_first_w_out", first_out))}

    ffn_saved = {}

    def ffn_forward(pos, h_in):
        tag, layer = ffn_seq[pos]
        comms = [("gather", a) for a in ffn_shards(*ffn_seq[pos + 1])] if pos + 1 < len(ffn_seq) else []
        res = _ffn_fwd("%s_fwd_%d" % (tag, layer), h_in, W[tag + "_norm"][layer:layer + 1], *ffn_w[ffn_seq[pos]],
                       comms=comms)
        ffn_saved[ffn_seq[pos]] = (res[1], res[2])
        if comms:
            ffn_w[ffn_seq[pos + 1]] = ffn_views(res[3], res[4])
        return res[0]

    mix_shapes = [W[n].shape for n in MIXER_BIG]

    def unpack_mixers(names, gathered):
        parts = _unpack_rows(gathered, [W[n].shape for n in names], lead=1)
        return {n: _unshard(a, SHARD_AXIS[n]) for n, a in zip(names, parts)}

    full = unpack_mixers(MIXER_EARLY, _all_gather("ag_mixers", _pack_rows([W[n].astype(BF16) for n in MIXER_EARLY], dm)))
    late_mixers = _pack_rows([W[n].astype(BF16) for n in MIXER_LATE], dm)
    small_shapes = [W[n].shape for n in SMALL_SHARDED]
    small_g = _all_gather("ag_small", _pack_rows([W[n] for n in SMALL_SHARDED], 128))
    full.update({n: _unshard(a, SHARD_AXIS[n])
                 for n, a in zip(SMALL_SHARDED, _unpack_rows(small_g, small_shapes, lead=1))})

    n_s5 = s5_w_in.shape[0]
    s5_groups = s5_lam_re.shape[1]
    heads = dm // SB_HEAD_DIM

    grads = {}
    saved = []
    h = h0

    for layer in range(depth):
        kind, j = layer % 3, layer // 3
        rec = {"h0": h}
        h = ffn_forward(2 * layer, h)
        rec["h1"] = h
        gain = mix_norm[layer:layer + 1]
        if kind == 0:
            (u,) = _mm_fwd("s5_in_%d" % layer, h, full["s5_w_in"][j], F32, gain=gain)
            pars = (s5_lam_re[j], s5_lam_im[j], s5_log_dt[j], s5_b_re[j], s5_b_im[j], s5_c_re[j], s5_c_im[j])
            mats, mats_vjp = jax.vjp(_s5_mats, *pars)
            apw, ar_fwd, ar_rev = _s5_powers(*pars[:3])
            res = _s5_fwd("s5_core_%d" % layer, u, *mats[:3], apw, ar_fwd,
                          comms=[("gather", late_mixers)] if layer == 0 else [])
            ys, sprev = res[0], res[1]
            if layer == 0:
                full.update(unpack_mixers(MIXER_LATE, res[2]))
            d_skip = full["s5_d"][j:j + 1]
            (z,) = _rowwise("s5_gelu_%d" % layer, lambda i, nt, ys_v, u_v, d_v: ((_gelu(ys_v + d_v * u_v),), ()),
                            [('t', ys, dm, 0), ('t', u, dm, 0), ('b', d_skip)], [(dm, BF16)], [], n_rows,
                            _tile(n_rows, 512))
            h, vg = _mm_fwd("s5_out_%d" % layer, z, full["s5_w_out"][j], F32, resid=h, glu=True)
            rec.update(u=u, ys=ys, sprev=sprev, z=z, vg=vg, mats=mats, mats_vjp=mats_vjp, apw=apw,
                       ar_rev=ar_rev, d_skip=d_skip)
        elif kind == 1:
            (qkv,) = _mm_fwd("sb_in_%d" % layer, h, full["sb_w_qkv"][j], BF16, gain=gain)
            o = _sb_fwd("sb_attn_%d" % layer, qkv, heads)
            (h,) = _mm_fwd("sb_out_%d" % layer, o, full["sb_w_out"][j], F32, resid=h)
            rec.update(qkv=qkv, o=o)
        else:
            (gx,) = _mm_fwd("lru_in_%d" % layer, h, full["lru_w_in"][j], F32, gain=gain)
            wa, wx = full["lru_w_a"][j], full["lru_w_x"][j]
            ba, bx = full["lru_b_a"][j].reshape(1, dm), full["lru_b_x"][j].reshape(1, dm)
            lam_row = full["lru_lambda"][j:j + 1]
            xc, r, ig, a, gated = _lru_gates_fwd("lru_gates_%d" % layer, gx, full["lru_conv_w"][j],
                                                 full["lru_conv_b"][j:j + 1], wa, ba, wx, bx, lam_row)
            hs, y = _lru_scan_fwd("lru_scan_%d" % layer, a, gated, gx)
            (h,) = _mm_fwd("lru_out_%d" % layer, y, full["lru_w_out"][j], F32, resid=h)
            rec.update(gx=gx, xc=xc, r=r, ig=ig, a=a, hs=hs, y=y, wa=wa, wx=wx, lam_row=lam_row)
        rec["h2"] = h
        h = ffn_forward(2 * layer + 1, h)
        saved.append(rec)

    dh, err2, dgf = _loss_head("loss_head", h, final_norm.reshape(1, dm), target)
    loss = lax.psum(0.5 / dm * jnp.sum(err2), ("x", "y", "c"))
    grads["final_norm"] = dgf.reshape(final_norm.shape)

    per_layer = {n: [None] * depth for n in ("ffn1_norm", "mix_norm", "ffn2_norm")}
    mixer_grads = {}
    recv_ffn = {}
    pending = []

    def ffn_backward(tag, layer, x_in, dh_in, more=None):
        comms = {0: [("exchange", pending[1])], 1: [("exchange", pending[2])]} if pending else {}
        comms.update(more or {})
        dx, dwi, dwo, dg, extra = _ffn_bwd("%s_bwd_%d" % (tag, layer), x_in, dh_in, W[tag + "_norm"][layer:layer + 1],
                                            *ffn_w[(tag, layer)], *ffn_saved[(tag, layer)], comms_by_block=comms)
        if pending:
            recv_ffn[pending[0]] = tuple(extra[:2])
            extra = extra[2:]
        pending[:] = [(tag, layer), dwi.reshape((N_DEV,) + dwi.shape[2:]), dwo.reshape(N_DEV, -1, dm)]
        per_layer[tag + "_norm"][layer] = dg
        return dx, extra

    def put(name, j, value, count):
        mixer_grads.setdefault(name, [None] * count)[j] = value

    for layer in reversed(range(depth)):
        kind, j = layer % 3, layer // 3
        rec = saved[layer]
        dh, _ = ffn_backward("ffn2", layer, rec["h2"], dh)
        gain = mix_norm[layer:layer + 1]
        if kind == 0:
            dvg, = _rowwise("s5_glu_bwd_%d" % layer,
                            lambda i, nt, d_v, vg_v: ((jnp.concatenate(
                                [d_v * _sigmoid(vg_v[:, dm:]),
                                 d_v * vg_v[:, :dm] * _sigmoid(vg_v[:, dm:]) * (1.0 - _sigmoid(vg_v[:, dm:]))],
                                axis=1),), ()),
                            [('t', dh, dm, 0), ('t', rec["vg"], 2 * dm, 0)], [(2 * dm, BF16)], [], n_rows,
                            _tile(n_rows, 256))
            dz, dw_out = _mm_bwd("s5_out_bwd_%d" % layer, rec["z"], dvg, full["s5_w_out"][j])

            def gelu_bwd(i, nt, dz_v, ys_v, u_v, d_v):
                dy_v = dz_v * _gelu_grad(ys_v + d_v * u_v)
                return (dy_v,), (jnp.sum(dy_v * u_v, axis=0, keepdims=True),)

            dys, dd = _rowwise("s5_gelu_bwd_%d" % layer, gelu_bwd,
                               [('t', dz, dm, 0), ('t', rec["ys"], dm, 0), ('t', rec["u"], dm, 0),
                                ('b', rec["d_skip"])], [(dm, F32)], [(1, dm)], n_rows, _tile(n_rows, 512))
            m_, bm_, cm_, _ = rec["mats"]
            du_core, dm_m, dm_b, dm_c, d_a = _s5_bwd("s5_core_bwd_%d" % layer, rec["u"], dys, rec["sprev"],
                                                    m_, bm_, cm_, rec["apw"], rec["ar_rev"])
            dpars = rec["mats_vjp"]((dm_m, dm_b, dm_c, d_a.reshape(s5_groups, -1)))
            for nme, val in zip(("s5_lam_re", "s5_lam_im", "s5_log_dt", "s5_b_re", "s5_b_im", "s5_c_re", "s5_c_im"),
                                dpars):
                put(nme, j, val, n_s5)
            (du,) = _rowwise("s5_du_%d" % layer, lambda i, nt, a_v, dy_v, d_v: ((a_v + dy_v * d_v,), ()),
                             [('t', du_core, dm, 0), ('t', dys, dm, 0), ('b', rec["d_skip"])],
                             [(dm, BF16)], [], n_rows, _tile(n_rows, 512))
            dh, dw_in, dgm = _mm_bwd("s5_in_bwd_%d" % layer, rec["h1"], du, full["s5_w_in"][j], gain=gain, dres=dh)
            put("s5_d", j, dd[0], n_s5)
            put("s5_w_out", j, dw_out, n_s5)
            put("s5_w_in", j, dw_in, n_s5)
        elif kind == 1:
            do, dw_out = _mm_bwd("sb_out_bwd_%d" % layer, rec["o"], dh, full["sb_w_out"][j])
            dq, dk, dv = _sb_bwd("sb_attn_bwd_%d" % layer, rec["qkv"], rec["o"], do, heads)
            dqkv = jnp.concatenate([dq, dk, dv], axis=1).astype(BF16)
            dh, dw_in, dgm = _mm_bwd("sb_in_bwd_%d" % layer, rec["h1"], dqkv, full["sb_w_qkv"][j], gain=gain, dres=dh)
            put("sb_w_out", j, dw_out, 1)
            put("sb_w_qkv", j, dw_in, 1)
        else:
            dy, dw_out = _mm_bwd("lru_out_bwd_%d" % layer, rec["y"], dh, full["lru_w_out"][j])
            lam_t = _lru_scan_bwd("lru_scan_bwd_%d" % layer, rec["a"], dy, rec["gx"])
            dxc, dwa, dwx, dba, dbx, dsp = _lru_gates_bwd("lru_gates_bwd_%d" % layer, lam_t, rec["hs"], rec["xc"],
                                                          rec["r"], rec["ig"], rec["a"], rec["wa"], rec["wx"],
                                                          rec["lam_row"])
            dgx, dcw, dcb = _lru_conv_bwd("lru_conv_bwd_%d" % layer, dxc, rec["gx"], dy, rec["hs"],
                                          full["lru_conv_w"][j])
            dh, dw_in, dgm = _mm_bwd("lru_in_bwd_%d" % layer, rec["h1"], dgx, full["lru_w_in"][j], gain=gain, dres=dh)
            nb = rec["wa"].shape[0]
            put("lru_w_out", j, dw_out, 1)
            put("lru_w_in", j, dw_in, 1)
            put("lru_w_a", j, dwa.reshape(rec["wa"].shape), 1)
            put("lru_w_x", j, dwx.reshape(rec["wx"].shape), 1)
            put("lru_b_a", j, dba.reshape(nb, -1), 1)
            put("lru_b_x", j, dbx.reshape(nb, -1), 1)
            put("lru_conv_w", j, dcw[:LRU_CONV], 1)
            put("lru_conv_b", j, dcb[0], 1)
            put("lru_lambda", j, (dsp * -_sigmoid(-rec["lam_row"]))[0], 1)
        per_layer["mix_norm"][layer] = dgm
        more = None
        if layer == 0:
            for n, parts in mixer_grads.items():
                grads[n] = jnp.stack(parts)
            send = _pack_rows([_shard_blocks(grads[n], SHARD_AXIS[n]).astype(BF16) for n in MIXER_BIG], dm, lead=1)
            half = send.shape[1] // 2
            more = {2: [("exchange", send[:, :half])], 3: [("exchange", send[:, half:])]}
        dh, got = ffn_backward("ffn1", layer, rec["h0"], dh, more)
        if layer == 0:
            recv_mixers = jnp.concatenate(got, axis=1)

    grad_x = dh[None]
    for n in ("ffn1_norm", "mix_norm", "ffn2_norm"):
        grads[n] = jnp.concatenate(per_layer[n], axis=0)

    out_g, out_d, out_m, out_v = {}, {}, {}, {}

    def finish(names, res, shapes):
        for n, g_, d_, m_, v_ in zip(names, *[_unpack_rows(t, shapes) for t in res]):
            out_g[n], out_d[n], out_m[n], out_v[n] = g_, d_, m_, v_

    recv_ffn[pending[0]] = (_exchange("xchg_last_w_in", pending[1]), _exchange("xchg_last_w_out", pending[2]))
    for tag in ("ffn1", "ffn2"):
        for which, n in enumerate((tag + "_w_in", tag + "_w_out")):
            turn = (lambda t: jnp.swapaxes(t, 1, 2)) if which == 0 else (lambda t: t)
            res = _adamw_layers("adamw_" + n, [recv_ffn[(tag, layer)][which] for layer in range(depth)],
                                turn(W[n]), turn(M[n]), turn(V[n]))
            out_g[n], out_d[n], out_m[n], out_v[n] = [turn(r) for r in res]

    finish(MIXER_BIG, _adamw("adamw_mixers", recv_mixers,
                             *[_pack_rows([t[n] for n in MIXER_BIG], dm) for t in (W, M, V)]), mix_shapes)

    small_names = REPLICATED + SMALL_SHARDED
    small_full_shapes = [grads[n].shape for n in small_names]
    parts = _all_gather("ag_small_grads", _pack_rows([grads[n] for n in small_names], 128))
    zero = jnp.zeros(parts.shape[1:], F32)
    summed = _adamw("sum_small_grads", parts, zero, zero, zero)[0]
    small_sum = dict(zip(small_names, _unpack_rows(summed, small_full_shapes)))
    me = 4 * lax.axis_index("x") + 2 * lax.axis_index("y") + lax.axis_index("c")
    rep_shapes = [W[n].shape for n in REPLICATED]
    g_rep = _pack_rows([small_sum[n] for n in REPLICATED], 128)[None]
    finish(REPLICATED, _adamw("adamw_replicated", g_rep, *[_pack_rows([t[n] for n in REPLICATED], 128)
                                                           for t in (W, M, V)]), rep_shapes)
    g_loc = []
    for n in SMALL_SHARDED:
        ax = SHARD_AXIS[n]
        size = W[n].shape[ax]
        g_loc.append(lax.dynamic_slice_in_dim(small_sum[n], me * size, size, axis=ax))
    finish(SMALL_SHARDED, _adamw("adamw_small", _pack_rows(g_loc, 128)[None],
                                 *[_pack_rows([t[n] for n in SMALL_SHARDED], 128) for t in (W, M, V)]), small_shapes)

    return (loss, grad_x, *[out_g[n] for n in WEIGHTS], *[out_d[n] for n in WEIGHTS],
            *[out_m[n] for n in WEIGHTS], *[out_v[n] for n in WEIGHTS])
```

```python
import functools
import math

import jax
import jax.numpy as jnp
from jax import lax
from jax.experimental import pallas as pl
from jax.experimental.pallas import tpu as pltpu

F32 = jnp.float32
BF16 = jnp.bfloat16
HI = lax.Precision.HIGHEST
S5_PREC = lax.Precision.HIGH
MESH = pl.DeviceIdType.MESH

N_DEV = 8
RMS_EPS = 1e-6
S5_GROUP = 16
S5_CHUNK = 16
S5_OCTET = 128 // S5_GROUP
S5_REGROUP_ROWS = 32
FFN_ROW_PARTS = 2
S5_SCAN_UNROLL = 8
SB_HEAD_DIM = 64
SB_UNDERFLOW = -104.0
SB_HEADS_FWD = 4
SB_HEADS_BWD = 4
LRU_CONV = 4
LRU_C = 8.0
ADAM_LR, ADAM_B1, ADAM_B2, ADAM_EPS, ADAM_WD, ADAM_STEP = 0.001, 0.9, 0.999, 1e-08, 0.01, 10
VMEM_LIMIT_BYTES = 56 * 1024 * 1024
SUBLANES = 8
PACK_ROWS = 256
ADAMW_LAYER_ROWS = 192

NN = (((1,), (0,)), ((), ()))
NT = (((1,), (1,)), ((), ()))
TN = (((0,), (0,)), ((), ()))

SHARD_AXIS = dict(
    ffn1_w_in=2, ffn1_w_out=1, ffn2_w_in=2, ffn2_w_out=1, s5_w_in=1, s5_d=1, s5_w_out=2, sb_w_qkv=2, sb_w_out=1,
    lru_w_in=2, lru_conv_w=2, lru_conv_b=1, lru_w_a=2, lru_b_a=2, lru_w_x=2, lru_b_x=2, lru_lambda=1, lru_w_out=1)
MIXER_BIG = ("s5_w_in", "s5_w_out", "sb_w_qkv", "sb_w_out", "lru_w_in", "lru_w_a", "lru_w_x", "lru_w_out")
MIXER_EARLY = MIXER_BIG[:2]
MIXER_LATE = MIXER_BIG[2:]
SMALL_SHARDED = ("s5_d", "lru_conv_w", "lru_conv_b", "lru_b_a", "lru_b_x", "lru_lambda")
REPLICATED = ("ffn1_norm", "mix_norm", "ffn2_norm", "final_norm", "s5_lam_re", "s5_lam_im", "s5_log_dt",
              "s5_b_re", "s5_b_im", "s5_c_re", "s5_c_im")
WEIGHTS = ("ffn1_norm", "ffn1_w_in", "ffn1_w_out", "mix_norm", "ffn2_norm", "ffn2_w_in", "ffn2_w_out", "final_norm",
           "s5_w_in", "s5_lam_re", "s5_lam_im", "s5_log_dt", "s5_b_re", "s5_b_im", "s5_c_re", "s5_c_im", "s5_d",
           "s5_w_out", "sb_w_qkv", "sb_w_out", "lru_w_in", "lru_conv_w", "lru_conv_b", "lru_w_a", "lru_b_a",
           "lru_w_x", "lru_b_x", "lru_lambda", "lru_w_out")


def _dot(a, b, dims=NN, prec=None):
    return lax.dot_general(a, b, dims, precision=prec, preferred_element_type=F32)


def _params(*sem):
    return pltpu.CompilerParams(dimension_semantics=sem, vmem_limit_bytes=VMEM_LIMIT_BYTES)


def _tile(n, pref):
    return min(pref, n)


def _sigmoid(x):
    return jax.nn.sigmoid(x)


def _softplus(x):
    return jnp.maximum(x, 0.0) + jnp.log(1.0 + jnp.exp(-jnp.abs(x)))


_GELU_C = math.sqrt(2.0 / math.pi)


def _gelu(x):
    return 0.5 * x * (1.0 + jnp.tanh(_GELU_C * (x + 0.044715 * x * x * x)))


def _gelu_grad(x):
    t = jnp.tanh(_GELU_C * (x + 0.044715 * x * x * x))
    return 0.5 * (1.0 + t) + 0.5 * x * (1.0 - t * t) * _GELU_C * (1.0 + 3.0 * 0.044715 * x * x)


def _rms(x):
    r = lax.rsqrt(jnp.mean(x * x, axis=1, keepdims=True) + RMS_EPS)
    return r, x * r


def _rms_bwd(dhn, xhat, r, g):
    dxhat = dhn * g
    return r * (dxhat - xhat * jnp.mean(dxhat * xhat, axis=1, keepdims=True))


def _one_minus_a2_sqrt(log_a):
    t = jnp.tanh(log_a)
    return jnp.sqrt(-2.0 * t / (1.0 - t))


def _shift_down(cur, prev8, k, first):
    if k == 0:
        return cur
    row8 = lax.broadcasted_iota(jnp.int32, prev8.shape, 0)
    rolled = pltpu.roll(cur, k, 0)
    edge = jnp.where(first, 0.0, pltpu.roll(prev8, k, 0))
    top = jnp.where(row8 < k, edge, rolled[0:SUBLANES])
    return jnp.concatenate([top, rolled[SUBLANES:]], axis=0)


def _shift_up(cur, next8, k, last):
    if k == 0:
        return cur
    tm = cur.shape[0]
    row8 = lax.broadcasted_iota(jnp.int32, next8.shape, 0)
    rolled = pltpu.roll(cur, tm - k, 0)
    edge = jnp.where(last, 0.0, pltpu.roll(next8, SUBLANES - k, 0))
    bottom = jnp.where(row8 >= SUBLANES - k, edge, rolled[tm - SUBLANES:tm])
    return jnp.concatenate([rolled[:tm - SUBLANES], bottom], axis=0)


def _rowwise(name, fn, ins, out_tiled, out_acc, n_rows, tm, reverse=False):
    nt = n_rows // tm
    per8 = tm // SUBLANES
    n8 = n_rows // SUBLANES
    n_in, n_ot = len(ins), len(out_tiled)

    def pos(i):
        return nt - 1 - i if reverse else i

    in_specs, args = [], []
    for spec in ins:
        kind, arr = spec[0], spec[1]
        args.append(arr)
        if kind == 'b':
            in_specs.append(pl.BlockSpec(arr.shape, lambda i, nd=arr.ndim: (0,) * nd))
        elif kind == 't':
            in_specs.append(pl.BlockSpec((tm, spec[2]), lambda i, cb=spec[3]: (pos(i), cb)))
        elif kind == 'p':
            in_specs.append(pl.BlockSpec((SUBLANES, spec[2]),
                                         lambda i, cb=spec[3]: (jnp.maximum(pos(i) * per8 - 1, 0), cb)))
        else:
            in_specs.append(pl.BlockSpec((SUBLANES, spec[2]),
                                         lambda i, cb=spec[3]: (jnp.minimum((pos(i) + 1) * per8, n8 - 1), cb)))

    def body(*refs):
        i = pl.program_id(0)
        outs = refs[n_in:]
        touts, aouts = fn(pos(i), nt, *[r[...] for r in refs[:n_in]])
        for r, v in zip(outs[:n_ot], touts):
            r[...] = v.astype(r.dtype)
        if out_acc:
            @pl.when(i == 0)
            def _():
                for r in outs[n_ot:]:
                    r[...] = jnp.zeros(r.shape, r.dtype)
            for r, v in zip(outs[n_ot:], aouts):
                r[...] += v

    out_specs = [pl.BlockSpec((tm, n), lambda i: (pos(i), 0)) for n, _ in out_tiled]
    out_specs += [pl.BlockSpec((r, n), lambda i: (0, 0)) for r, n in out_acc]
    out_shape = [jax.ShapeDtypeStruct((n_rows, n), dt) for n, dt in out_tiled]
    out_shape += [jax.ShapeDtypeStruct((r, n), F32) for r, n in out_acc]
    return pl.pallas_call(body, grid=(nt,), in_specs=in_specs, out_specs=out_specs, out_shape=out_shape, name=name,
                          compiler_params=_params("arbitrary"))(*args)


def _all_gather(name, block):
    def body(x_ref, out_ref, send_sems, recv_sems, local_sem):
        x, y, c = lax.axis_index("x"), lax.axis_index("y"), lax.axis_index("c")
        me, sibling = (x, y, c), (x, y, 1 - c)
        chips = [(1 - x, y), (x, 1 - y), (1 - x, 1 - y)]

        def rows(px, py, pc):
            return out_ref.at[4 * px + 2 * py + pc]

        def copy(k, blk, to, src=None):
            return pltpu.make_async_remote_copy(
                src_ref=rows(*blk) if src is None else src, dst_ref=rows(*blk),
                send_sem=send_sems.at[k], recv_sem=recv_sems.at[k], device_id=to, device_id_type=MESH)

        mine = pltpu.make_async_copy(x_ref, rows(*me), local_sem)
        mine.start()
        first = [copy(0, me, sibling, src=x_ref)]
        first += [copy(1 + j, me, (*chip, c), src=x_ref) for j, chip in enumerate(chips)]
        for cp in first:
            cp.start()
        passed = [copy(4 + j, (*chip, c), sibling) for j, chip in enumerate(chips)]
        for j, chip in enumerate(chips):
            copy(1 + j, (*chip, c), me).wait_recv()
            passed[j].start()
        copy(0, sibling, me).wait_recv()
        for j, chip in enumerate(chips):
            copy(4 + j, (*chip, 1 - c), me).wait_recv()
        for cp in first + passed:
            cp.wait_send()
        mine.wait()

    return pl.pallas_call(
        body, name=name, out_shape=jax.ShapeDtypeStruct((N_DEV,) + block.shape, block.dtype),
        in_specs=[pl.BlockSpec(memory_space=pl.ANY)], out_specs=pl.BlockSpec(memory_space=pl.ANY),
        scratch_shapes=[pltpu.SemaphoreType.DMA((7,)), pltpu.SemaphoreType.DMA((7,)), pltpu.SemaphoreType.DMA(())],
    )(block)


def _exchange(name, send):
    def body(s_ref, r_ref, send_sems, recv_sems, local_sem):
        x, y, c = lax.axis_index("x"), lax.axis_index("y"), lax.axis_index("c")
        me = 4 * x + 2 * y + c
        mine = pltpu.make_async_copy(s_ref.at[me], r_ref.at[me], local_sem)
        mine.start()
        copies = []
        for k in range(1, N_DEV):
            dx, dy, dc = (k >> 2) & 1, (k >> 1) & 1, k & 1
            px = 1 - x if dx else x
            py = 1 - y if dy else y
            pc = 1 - c if dc else c
            peer = 4 * px + 2 * py + pc
            copies.append((pltpu.make_async_remote_copy(
                src_ref=s_ref.at[peer], dst_ref=r_ref.at[me], send_sem=send_sems.at[k - 1],
                recv_sem=recv_sems.at[k - 1], device_id=(px, py, pc), device_id_type=MESH), peer))
        for cp, _ in copies:
            cp.start()
        for k, (cp, peer) in enumerate(copies):
            pltpu.make_async_remote_copy(
                src_ref=s_ref.at[peer], dst_ref=r_ref.at[peer], send_sem=send_sems.at[k], recv_sem=recv_sems.at[k],
                device_id=(x, y, c), device_id_type=MESH).wait_recv()
        for cp, _ in copies:
            cp.wait_send()
        mine.wait()

    return pl.pallas_call(
        body, name=name, out_shape=jax.ShapeDtypeStruct(send.shape, send.dtype),
        in_specs=[pl.BlockSpec(memory_space=pl.ANY)], out_specs=pl.BlockSpec(memory_space=pl.ANY),
        scratch_shapes=[pltpu.SemaphoreType.DMA((7,)), pltpu.SemaphoreType.DMA((7,)), pltpu.SemaphoreType.DMA(())],
    )(send)


def _direct_copies(kind, s_ref, r_ref, send_sems, recv_sems, local_sem):
    x, y, c = lax.axis_index("x"), lax.axis_index("y"), lax.axis_index("c")
    me = 4 * x + 2 * y + c

    def src(p):
        return s_ref if kind == "gather" else s_ref.at[p]

    local = pltpu.make_async_copy(src(me), r_ref.at[me], local_sem)
    sends, recvs = [], []
    for k in range(1, N_DEV):
        px = 1 - x if (k >> 2) & 1 else x
        py = 1 - y if (k >> 1) & 1 else y
        pc = 1 - c if k & 1 else c
        peer = 4 * px + 2 * py + pc
        sends.append(pltpu.make_async_remote_copy(
            src_ref=src(peer), dst_ref=r_ref.at[me], send_sem=send_sems.at[k - 1], recv_sem=recv_sems.at[k - 1],
            device_id=(px, py, pc), device_id_type=MESH))
        recvs.append(pltpu.make_async_remote_copy(
            src_ref=src(peer), dst_ref=r_ref.at[peer], send_sem=send_sems.at[k - 1], recv_sem=recv_sems.at[k - 1],
            device_id=(x, y, c), device_id_type=MESH))
    return local, sends, recvs


def _call(body, *, grid, in_specs, out_specs, out_shape, name, args, scratch_shapes=(), semantics=None, comms=()):
    single = not isinstance(out_shape, (list, tuple))
    out_shape = [out_shape] if single else list(out_shape)
    out_specs = [out_specs] if single else list(out_specs)
    if not comms:
        res = pl.pallas_call(body, grid=grid, in_specs=in_specs, out_specs=out_specs, out_shape=out_shape, name=name,
                             scratch_shapes=list(scratch_shapes),
                             compiler_params=_params(*(semantics or ("arbitrary",) * len(grid))))(*args)
        return res[0] if single else res
    n_in, n_out, n_scr, n_c = len(args), len(out_shape), len(scratch_shapes), len(comms)

    def hosted(*refs):
        ins, srcs = refs[:n_in], refs[n_in:n_in + n_c]
        outs = refs[n_in + n_c:n_in + n_c + n_out]
        dsts = refs[n_in + n_c + n_out:n_in + 2 * n_c + n_out]
        scr = refs[n_in + 2 * n_c + n_out:n_in + 2 * n_c + n_out + n_scr]
        sems = refs[n_in + 2 * n_c + n_out + n_scr:]
        first = functools.reduce(jnp.logical_and, [pl.program_id(d) == 0 for d in range(len(grid))])
        last = functools.reduce(jnp.logical_and, [pl.program_id(d) == grid[d] - 1 for d in range(len(grid))])
        plans = [_direct_copies(comms[i][0], srcs[i], dsts[i], *sems[3 * i:3 * i + 3]) for i in range(n_c)]

        @pl.when(first)
        def _():
            for local, sends, _ in plans:
                local.start()
                for cp in sends:
                    cp.start()

        body(*ins, *outs, *scr)

        @pl.when(last)
        def _():
            for local, sends, recvs in plans:
                for cp in recvs:
                    cp.wait_recv()
                for cp in sends:
                    cp.wait_send()
                local.wait()

    any_spec = pl.BlockSpec(memory_space=pl.ANY)
    comm_shapes = [jax.ShapeDtypeStruct(((N_DEV,) + a.shape) if kind == "gather" else a.shape, a.dtype)
                   for kind, a in comms]
    sem_shapes = []
    for _ in comms:
        sem_shapes += [pltpu.SemaphoreType.DMA((7,)), pltpu.SemaphoreType.DMA((7,)), pltpu.SemaphoreType.DMA(())]
    res = pl.pallas_call(
        hosted, grid=grid, in_specs=list(in_specs) + [any_spec] * n_c, out_specs=out_specs + [any_spec] * n_c,
        out_shape=out_shape + comm_shapes, name=name, scratch_shapes=list(scratch_shapes) + sem_shapes,
        compiler_params=_params(*(("arbitrary",) * len(grid))))(*args, *[a for _, a in comms])
    return res


def _unshard(gathered, axis):
    local = gathered.shape[1:]
    full = jnp.moveaxis(gathered, 0, axis)
    return full.reshape(local[:axis] + (N_DEV * local[axis],) + local[axis + 1:])


def _shard_blocks(full, axis):
    s = full.shape
    cut = full.reshape(s[:axis] + (N_DEV, s[axis] // N_DEV) + s[axis + 1:])
    return jnp.moveaxis(cut, axis, 0)


def _mm_fwd(name, a, w, out_dtype, gain=None, resid=None, glu=False):
    n_rows, k = a.shape
    n = w.shape[1]
    tm = _tile(n_rows, 512)
    n_out = n // 2 if glu else n

    def body(*refs):
        it = iter(refs)
        a_ref, w_ref = next(it), next(it)
        g_ref = next(it) if gain is not None else None
        r_ref = next(it) if resid is not None else None
        outs = list(it)
        av = a_ref[...]
        if g_ref is not None:
            _, xhat = _rms(av)
            av = xhat * g_ref[...]
        res = _dot(av.astype(BF16), w_ref[...])
        if glu:
            outs[1][...] = res.astype(outs[1].dtype)
            res = res[:, :n_out] * _sigmoid(res[:, n_out:])
        if r_ref is not None:
            res = res + r_ref[...]
        outs[0][...] = res.astype(outs[0].dtype)

    args = [a, w]
    in_specs = [pl.BlockSpec((tm, k), lambda i: (i, 0)), pl.BlockSpec((k, n), lambda i: (0, 0))]
    if gain is not None:
        args.append(gain)
        in_specs.append(pl.BlockSpec((1, k), lambda i: (0, 0)))
    if resid is not None:
        args.append(resid)
        in_specs.append(pl.BlockSpec((tm, n_out), lambda i: (i, 0)))
    out_shape = [jax.ShapeDtypeStruct((n_rows, n_out), out_dtype)]
    out_specs = [pl.BlockSpec((tm, n_out), lambda i: (i, 0))]
    if glu:
        out_shape.append(jax.ShapeDtypeStruct((n_rows, n), F32))
        out_specs.append(pl.BlockSpec((tm, n), lambda i: (i, 0)))
    return pl.pallas_call(body, grid=(n_rows // tm,), in_specs=in_specs, out_specs=out_specs, out_shape=out_shape,
                          name=name, compiler_params=_params("parallel"))(*args)


def _mm_bwd(name, a, d, w, gain=None, dres=None):
    n_rows, k = a.shape
    n = w.shape[1]
    tm = _tile(n_rows, 512)

    def body(*refs):
        it = iter(refs)
        a_ref, d_ref, w_ref = next(it), next(it), next(it)
        g_ref = next(it) if gain is not None else None
        r_ref = next(it) if gain is not None else None
        da_ref, dw_ref = next(it), next(it)
        dg_ref = next(it) if gain is not None else None
        i = pl.program_id(0)

        @pl.when(i == 0)
        def _():
            dw_ref[...] = jnp.zeros(dw_ref.shape, F32)
            if dg_ref is not None:
                dg_ref[...] = jnp.zeros(dg_ref.shape, F32)

        av = a_ref[...]
        dv = d_ref[...].astype(BF16)
        if g_ref is not None:
            r, xhat = _rms(av)
            ab = (xhat * g_ref[...]).astype(BF16)
        else:
            ab = av.astype(BF16)
        dw_ref[...] += _dot(ab, dv, TN)
        da = _dot(dv, w_ref[...], NT)
        if g_ref is not None:
            dg_ref[...] += jnp.sum(da * xhat, axis=0, keepdims=True)
            da = r_ref[...] + _rms_bwd(da, xhat, r, g_ref[...])
        da_ref[...] = da.astype(da_ref.dtype)

    args = [a, d, w]
    in_specs = [pl.BlockSpec((tm, k), lambda i: (i, 0)), pl.BlockSpec((tm, n), lambda i: (i, 0)),
                pl.BlockSpec((k, n), lambda i: (0, 0))]
    out_shape = [jax.ShapeDtypeStruct((n_rows, k), F32), jax.ShapeDtypeStruct((k, n), F32)]
    out_specs = [pl.BlockSpec((tm, k), lambda i: (i, 0)), pl.BlockSpec((k, n), lambda i: (0, 0))]
    if gain is not None:
        args += [gain, dres]
        in_specs += [pl.BlockSpec((1, k), lambda i: (0, 0)), pl.BlockSpec((tm, k), lambda i: (i, 0))]
        out_shape.append(jax.ShapeDtypeStruct((1, k), F32))
        out_specs.append(pl.BlockSpec((1, k), lambda i: (0, 0)))
    return pl.pallas_call(body, grid=(n_rows // tm,), in_specs=in_specs, out_specs=out_specs, out_shape=out_shape,
                          name=name, compiler_params=_params("arbitrary"))(*args)


def _ffn_fwd(name, x, gain, wi, wo, comms=()):
    n_rows, dm = x.shape
    _, nj, _, fb = wi.shape
    tm = _tile(n_rows, 512)

    def body(x_ref, g_ref, wi_ref, wo_ref, y_ref, gate_ref, up_ref):
        xv = x_ref[...]
        _, xhat = _rms(xv)
        hn = (xhat * g_ref[...]).astype(BF16)
        acc = jnp.zeros((tm, dm), F32)
        for j in range(nj):
            gate = _dot(hn, wi_ref[0, j])
            up = _dot(hn, wi_ref[1, j])
            gate_ref[j] = gate.astype(BF16)
            up_ref[j] = up.astype(BF16)
            act = (gate * _sigmoid(gate) * up).astype(BF16)
            acc = acc + _dot(act, wo_ref[j].reshape(fb, dm))
        y_ref[...] = xv + 0.5 * acc

    return _call(
        body, grid=(n_rows // tm,), name=name, args=[x, gain, wi, wo], comms=comms,
        in_specs=[pl.BlockSpec((tm, dm), lambda i: (i, 0)), pl.BlockSpec((1, dm), lambda i: (0, 0)),
                  pl.BlockSpec((2, nj, dm, fb), lambda i: (0, 0, 0, 0)),
                  pl.BlockSpec((nj, 2, fb // 2, dm), lambda i: (0, 0, 0, 0))],
        out_specs=[pl.BlockSpec((tm, dm), lambda i: (i, 0)), pl.BlockSpec((nj, tm, fb), lambda i: (0, i, 0)),
                   pl.BlockSpec((nj, tm, fb), lambda i: (0, i, 0))],
        out_shape=[jax.ShapeDtypeStruct((n_rows, dm), F32), jax.ShapeDtypeStruct((nj, n_rows, fb), BF16),
                   jax.ShapeDtypeStruct((nj, n_rows, fb), BF16)])


def _ffn_bwd_block(name, x, dy, gain, wi, wo, gate_s, up_s, j, acc, comms=()):
    n_rows, dm = x.shape
    _, nj, _, fb = wi.shape
    tm = _tile(n_rows, 512)
    last = j == nj - 1

    def body(*refs):
        it = iter(refs)
        x_ref, dy_ref, g_ref, wi_ref, wo_ref = next(it), next(it), next(it), next(it), next(it)
        gate_ref, up_ref = next(it), next(it)
        acc_ref = next(it) if acc is not None else None
        out_ref, dwi16_ref, dwo16_ref = next(it), next(it), next(it)
        dg_ref = next(it) if last else None
        dwi_ref, dwo_ref = next(it), next(it)
        i = pl.program_id(0)

        @pl.when(i == 0)
        def _():
            dwi_ref[...] = jnp.zeros(dwi_ref.shape, F32)
            dwo_ref[...] = jnp.zeros(dwo_ref.shape, F32)
            if last:
                dg_ref[...] = jnp.zeros(dg_ref.shape, F32)

        g = g_ref[...]
        wg, wu, wob = wi_ref[0], wi_ref[1], wo_ref[...].reshape(fb, dm)
        parts = range(FFN_ROW_PARTS)
        rp = tm // FFN_ROW_PARTS
        rows = [slice(k * rp, (k + 1) * rp) for k in parts]
        xv = [x_ref[rows[k], :] for k in parts]
        dyv = [dy_ref[rows[k], :] for k in parts]
        rx = [_rms(xv[k]) for k in parts]
        hn = [(rx[k][1] * g).astype(BF16) for k in parts]
        dout = [(0.5 * dyv[k]).astype(BF16) for k in parts]
        dact = [_dot(dout[k], wob, NT) for k in parts]
        gate = [gate_ref[rows[k], :].astype(F32) for k in parts]
        up = [up_ref[rows[k], :].astype(F32) for k in parts]
        s = [_sigmoid(gate[k]) for k in parts]
        silu = [gate[k] * s[k] for k in parts]
        act = [(silu[k] * up[k]).astype(BF16) for k in parts]
        dgate = [(dact[k] * up[k] * (s[k] * (1.0 + gate[k] * (1.0 - s[k])))).astype(BF16) for k in parts]
        dup = [(dact[k] * silu[k]).astype(BF16) for k in parts]
        for k in parts:
            dwo_ref[...] += _dot(act[k], dout[k], TN)
            dwi_ref[0] += _dot(dgate[k], hn[k], TN)
            dwi_ref[1] += _dot(dup[k], hn[k], TN)
        tot = [_dot(dgate[k], wg, NT) + _dot(dup[k], wu, NT) for k in parts]
        for k in parts:
            t = tot[k] + acc_ref[rows[k], :] if acc_ref is not None else tot[k]
            if last:
                out_ref[rows[k], :] = dyv[k] + _rms_bwd(t, rx[k][1], rx[k][0], g)
                dg_ref[...] += jnp.sum(t * rx[k][1], axis=0, keepdims=True)
            else:
                out_ref[rows[k], :] = t

        @pl.when(i == n_rows // tm - 1)
        def _():
            dwi16_ref[...] = dwi_ref[...].astype(BF16)
            dwo16_ref[...] = dwo_ref[...].astype(BF16)

    tok = pl.BlockSpec((tm, dm), lambda i: (i, 0))
    args = [x, dy, gain, wi, wo, gate_s, up_s]
    saved = pl.BlockSpec((None, tm, fb), lambda i: (j, i, 0))
    in_specs = [tok, tok, pl.BlockSpec((1, dm), lambda i: (0, 0)),
                pl.BlockSpec((2, None, dm, fb), lambda i: (0, j, 0, 0)),
                pl.BlockSpec((None, 2, fb // 2, dm), lambda i: (j, 0, 0, 0)), saved, saved]
    if acc is not None:
        args.append(acc)
        in_specs.append(tok)
    out_specs = [tok, pl.BlockSpec((2, fb, dm), lambda i: (0, 0, 0)), pl.BlockSpec((fb, dm), lambda i: (0, 0))]
    out_shape = [jax.ShapeDtypeStruct((n_rows, dm), F32), jax.ShapeDtypeStruct((2, fb, dm), BF16),
                 jax.ShapeDtypeStruct((fb, dm), BF16)]
    if last:
        out_specs.append(pl.BlockSpec((1, dm), lambda i: (0, 0)))
        out_shape.append(jax.ShapeDtypeStruct((1, dm), F32))
    return _call(body, grid=(n_rows // tm,), name=name, in_specs=in_specs, out_specs=out_specs,
                 out_shape=out_shape, args=args, comms=comms,
                 scratch_shapes=[pltpu.VMEM((2, fb, dm), F32), pltpu.VMEM((fb, dm), F32)])


def _ffn_bwd(name, x, dy, gain, wi, wo, gate_s, up_s, comms_by_block=None):
    nj = wi.shape[1]
    acc, dwi, dwo, extra = None, [], [], []
    for j in range(nj):
        comms = (comms_by_block or {}).get(j, ())
        res = _ffn_bwd_block("%s_%d" % (name, j), x, dy, gain, wi, wo, gate_s, up_s, j, acc, comms)
        n_own = 4 if j == nj - 1 else 3
        acc = res[0]
        dwi.append(res[1])
        dwo.append(res[2])
        extra += list(res[n_own:])
        dgain = res[3] if j == nj - 1 else None
    return acc, jnp.stack(dwi, axis=1), jnp.stack(dwo, axis=0), dgain, extra


def _scan8(a, x, reverse):
    row = lax.broadcasted_iota(jnp.int32, a.shape, 0)
    for k in (1, 2, 4):
        if reverse:
            keep = row < SUBLANES - k
            a_s, x_s = pltpu.roll(a, SUBLANES - k, 0), pltpu.roll(x, SUBLANES - k, 0)
        else:
            keep = row >= k
            a_s, x_s = pltpu.roll(a, k, 0), pltpu.roll(x, k, 0)
        x = a * jnp.where(keep, x_s, 0.0) + x
        a = a * jnp.where(keep, a_s, 1.0)
    return a, x


def _scan_tile(a_ref, x_ref, h_ref, carry, reverse, rows):
    groups = rows // SUBLANES

    def step(n, c):
        gidx = groups - 1 - n if reverse else n
        sl = pl.ds(pl.multiple_of(gidx * SUBLANES, SUBLANES), SUBLANES)
        a_cum, h0 = _scan8(a_ref[sl, :], x_ref[sl, :], reverse)
        h = a_cum * c + h0
        h_ref[sl, :] = h
        return h[0:1] if reverse else h[SUBLANES - 1:SUBLANES]

    return lax.fori_loop(0, groups, step, carry)


def _s5_mats(lam_re, lam_im, log_dt, b_re, b_im, c_re, c_im):
    lc = S5_CHUNK
    groups, p = lam_re.shape
    h = b_re.shape[-1]
    lam = lax.complex(lam_re, lam_im)
    lam_dt = lam * jnp.exp(log_dt)[:, None]
    lam_bar = jnp.exp(lam_dt)
    b_bar = ((lam_bar - 1.0) / lam)[:, :, None] * lax.complex(b_re, b_im)
    c = lax.complex(c_re, c_im)
    pw = jnp.exp(lam_dt[None] * jnp.arange(lc + 1, dtype=F32)[:, None, None])
    resp = jnp.einsum('ghp,tgp,gpk->tghk', c, pw[:lc], b_bar, precision=HI).real
    s_idx = jnp.arange(lc)[:, None]
    u_idx = jnp.arange(lc)[None, :]
    onehot = (jnp.arange(lc)[:, None, None] == (u_idx - s_idx)[None]).astype(F32)
    m = jnp.einsum('tghk,tsu->gskuh', resp, onehot, precision=HI).reshape(groups, lc * h, lc * h)
    w = pw[lc - 1::-1][:lc].transpose(1, 0, 2)[:, :, None, :] * b_bar.transpose(0, 2, 1)[:, None]
    bm = jnp.concatenate([w.real, w.imag], axis=-1).reshape(groups, lc * h, 2 * p)
    v = c[:, None] * pw[1:lc + 1].transpose(1, 0, 2)[:, :, None, :]
    v = v.transpose(0, 3, 1, 2)
    cm = jnp.concatenate([v.real, -v.imag], axis=1).reshape(groups, 2 * p, lc * h)
    a = jnp.concatenate([pw[lc].real, pw[lc].imag], axis=-1)
    return m, bm, cm, a


def _s5_powers(lam_re, lam_im, log_dt):
    lam_dt = lax.complex(lam_re, lam_im) * jnp.exp(log_dt)[:, None]
    pw = jnp.exp(lam_dt[None] * (S5_CHUNK * jnp.arange(1, 9, dtype=F32))[:, None, None])

    def c1(z):
        return jnp.concatenate([z.real, z.real], axis=-1).reshape(z.shape[0], -1)

    def c2(z):
        return jnp.concatenate([-z.imag, z.imag], axis=-1).reshape(z.shape[0], -1)

    p1, p2 = c1(pw), c2(pw)
    apw = jnp.stack([p1[0], p2[0], p1[1], p2[1], p1[3], p2[3], jnp.zeros_like(p1[0]), jnp.zeros_like(p1[0])])
    fwd = jnp.concatenate([p1, p2], axis=0)
    rev = jnp.concatenate([c1(pw[::-1]), c2(pw[::-1])], axis=0)
    return apw, fwd, rev


def _cmul(c1, c2, x, half, conj=False):
    sw = pltpu.roll(x, half, 1)
    return c1 * x - c2 * sw if conj else c1 * x + c2 * sw


def _gather_groups(u_ref, ug_ref, nc):
    h = S5_GROUP
    rows = min(S5_REGROUP_ROWS, nc)

    def step(r, _):
        base = pl.multiple_of(r * rows, rows)
        for t in range(S5_CHUNK):
            val = u_ref[pl.ds(base * S5_CHUNK + t, rows, stride=S5_CHUNK), :]
            for g in range(S5_OCTET):
                ug_ref[g, pl.ds(base, rows), t * h:(t + 1) * h] = val[:, g * h:(g + 1) * h]
        return 0

    lax.fori_loop(0, nc // rows, step, 0)


def _scatter_groups(yg_ref, y_ref, nc):
    h = S5_GROUP
    rows = min(S5_REGROUP_ROWS, nc)

    def step(r, _):
        base = pl.multiple_of(r * rows, rows)
        for t in range(S5_CHUNK):
            y_ref[pl.ds(base * S5_CHUNK + t, rows, stride=S5_CHUNK), :] = jnp.concatenate(
                [yg_ref[g, pl.ds(base, rows), t * h:(t + 1) * h] for g in range(S5_OCTET)], axis=1)
        return 0

    lax.fori_loop(0, nc // rows, step, 0)


def _s5_fwd(name, u, m, bm, cm, apw, arows, comms=()):
    n_rows, width = u.shape
    nc = n_rows // S5_CHUNK
    groups, lh, _ = m.shape
    p2 = bm.shape[2]
    gb = S5_OCTET
    lanes = gb * S5_GROUP

    def body(u_ref, m_ref, b_ref, c_ref, apw_ref, ar_ref, y_ref, sp_ref, ug_ref, yg_ref, xs_ref):
        _gather_groups(u_ref, ug_ref, nc)
        for gi in range(gb):
            xs_ref[:, gi * p2:(gi + 1) * p2] = _dot(ug_ref[gi], b_ref[gi], prec=S5_PREC)
        row = lax.broadcasted_iota(jnp.int32, (SUBLANES, p2), 0)

        def group(n, carry):
            sl = pl.ds(pl.multiple_of(n * SUBLANES, SUBLANES), SUBLANES)
            new = []
            for gi in range(gb):
                ln = slice(gi * p2, (gi + 1) * p2)
                x = xs_ref[sl, ln]
                for q, k in enumerate((1, 2, 4)):
                    xs = jnp.where(row >= k, pltpu.roll(x, k, 0), 0.0)
                    x = x + _cmul(apw_ref[2 * q:2 * q + 1, ln], apw_ref[2 * q + 1:2 * q + 2, ln], xs, p2 // 2)
                cb = jnp.broadcast_to(carry[gi], (SUBLANES, p2))
                s8 = x + _cmul(ar_ref[0:8, ln], ar_ref[8:16, ln], cb, p2 // 2)
                sp_ref[sl, ln] = jnp.where(row >= 1, pltpu.roll(s8, 1, 0), cb)
                new.append(s8[SUBLANES - 1:SUBLANES])
            return tuple(new)

        unroll = min(S5_SCAN_UNROLL, nc // SUBLANES)

        def step(n, carry):
            for k in range(unroll):
                carry = group(n * unroll + k, carry)
            return carry

        lax.fori_loop(0, nc // (SUBLANES * unroll), step, tuple(jnp.zeros((1, p2), F32) for _ in range(gb)))
        for gi in range(gb):
            yg_ref[gi] = (_dot(ug_ref[gi], m_ref[gi], prec=S5_PREC)
                          + _dot(sp_ref[:, gi * p2:(gi + 1) * p2], c_ref[gi], prec=S5_PREC))
        _scatter_groups(yg_ref, y_ref, nc)

    tok = pl.BlockSpec((n_rows, lanes), lambda g: (0, g), pipeline_mode=pl.Buffered(1))
    return _call(
        body, grid=(groups // gb,), name=name, args=[u, m, bm, cm, apw, arows], comms=comms, semantics=("parallel",),
        in_specs=[tok, pl.BlockSpec((gb, lh, lh), lambda g: (g, 0, 0)),
                  pl.BlockSpec((gb, lh, p2), lambda g: (g, 0, 0)), pl.BlockSpec((gb, p2, lh), lambda g: (g, 0, 0)),
                  pl.BlockSpec((8, gb * p2), lambda g: (0, g)), pl.BlockSpec((16, gb * p2), lambda g: (0, g))],
        out_specs=[tok, pl.BlockSpec((nc, gb * p2), lambda g: (0, g))],
        out_shape=[jax.ShapeDtypeStruct((n_rows, width), F32), jax.ShapeDtypeStruct((nc, groups * p2), F32)],
        scratch_shapes=[pltpu.VMEM((gb, nc, lh), F32), pltpu.VMEM((gb, nc, lh), F32), pltpu.VMEM((nc, gb * p2), F32)])


def _s5_bwd(name, u, dy, sprev, m, bm, cm, apw, arows_rev):
    n_rows, width = u.shape
    nc = n_rows // S5_CHUNK
    groups, lh, _ = m.shape
    p2 = bm.shape[2]
    half = p2 // 2
    gb = S5_OCTET
    lanes = gb * S5_GROUP

    def body(u_ref, dy_ref, sp_ref, m_ref, b_ref, c_ref, apw_ref, ar_ref,
             du_ref, dm_ref, db_ref, dc_ref, da_ref, ug_ref, dyg_ref, ds_ref, gx_ref):
        _gather_groups(u_ref, ug_ref, nc)
        _gather_groups(dy_ref, dyg_ref, nc)
        for gi in range(gb):
            ds_ref[:, gi * p2:(gi + 1) * p2] = _dot(dyg_ref[gi], c_ref[gi], NT, prec=S5_PREC)
        row = lax.broadcasted_iota(jnp.int32, (SUBLANES, p2), 0)
        lane = lax.broadcasted_iota(jnp.int32, (SUBLANES, p2), 1)
        ngroups = nc // SUBLANES

        def group(n, state):
            carry, nxt, dacc = state
            sl = pl.ds(pl.multiple_of((ngroups - 1 - n) * SUBLANES, SUBLANES), SUBLANES)
            new_c, new_n, new_d = [], [], []
            for gi in range(gb):
                ln = slice(gi * p2, (gi + 1) * p2)
                d8 = ds_ref[sl, ln]
                x = jnp.where(row < SUBLANES - 1, pltpu.roll(d8, SUBLANES - 1, 0),
                              jnp.broadcast_to(nxt[gi], (SUBLANES, p2)))
                for q, k in enumerate((1, 2, 4)):
                    xs = jnp.where(row < SUBLANES - k, pltpu.roll(x, SUBLANES - k, 0), 0.0)
                    x = x + _cmul(apw_ref[2 * q:2 * q + 1, ln], apw_ref[2 * q + 1:2 * q + 2, ln], xs, half, conj=True)
                cb = jnp.broadcast_to(carry[gi], (SUBLANES, p2))
                g8 = x + _cmul(ar_ref[0:8, ln], ar_ref[8:16, ln], cb, half, conj=True)
                gx_ref[sl, ln] = g8
                s8 = sp_ref[sl, ln]
                p1 = g8 * s8
                pq = g8 * pltpu.roll(s8, half, 1)
                d_a = jnp.where(lane < half, p1 + pltpu.roll(p1, half, 1), pq - pltpu.roll(pq, half, 1))
                new_c.append(g8[0:1])
                new_n.append(d8[0:1])
                new_d.append(dacc[gi] + jnp.sum(d_a, axis=0, keepdims=True))
            return tuple(new_c), tuple(new_n), tuple(new_d)

        unroll = min(S5_SCAN_UNROLL, ngroups)

        def step(n, state):
            for k in range(unroll):
                state = group(n * unroll + k, state)
            return state

        zeros = tuple(jnp.zeros((1, p2), F32) for _ in range(gb))
        _, _, dacc = lax.fori_loop(0, ngroups // unroll, step, (zeros, zeros, zeros))
        for gi in range(gb):
            ln = slice(gi * p2, (gi + 1) * p2)
            da_ref[:, ln] = dacc[gi]
            ug, dyg, gxg = ug_ref[gi], dyg_ref[gi], gx_ref[:, ln]
            dm_ref[gi] = _dot(ug, dyg, TN, prec=S5_PREC)
            dc_ref[gi] = _dot(sp_ref[:, ln], dyg, TN, prec=S5_PREC)
            db_ref[gi] = _dot(ug, gxg, TN, prec=S5_PREC)
            dyg_ref[gi] = _dot(dyg, m_ref[gi], NT, prec=S5_PREC) + _dot(gxg, b_ref[gi], NT, prec=S5_PREC)
        _scatter_groups(dyg_ref, du_ref, nc)

    tok = pl.BlockSpec((n_rows, lanes), lambda g: (0, g), pipeline_mode=pl.Buffered(1))
    tok_s = pl.BlockSpec((nc, gb * p2), lambda g: (0, g))
    mat_m = pl.BlockSpec((gb, lh, lh), lambda g: (g, 0, 0))
    mat_b = pl.BlockSpec((gb, lh, p2), lambda g: (g, 0, 0))
    mat_c = pl.BlockSpec((gb, p2, lh), lambda g: (g, 0, 0))
    return pl.pallas_call(
        body, grid=(groups // gb,), name=name,
        in_specs=[tok, tok, tok_s, mat_m, mat_b, mat_c,
                  pl.BlockSpec((8, gb * p2), lambda g: (0, g)), pl.BlockSpec((16, gb * p2), lambda g: (0, g))],
        out_specs=[tok, mat_m, mat_b, mat_c, pl.BlockSpec((1, gb * p2), lambda g: (0, g))],
        out_shape=[jax.ShapeDtypeStruct((n_rows, width), F32), jax.ShapeDtypeStruct(m.shape, F32),
                   jax.ShapeDtypeStruct(bm.shape, F32), jax.ShapeDtypeStruct(cm.shape, F32),
                   jax.ShapeDtypeStruct((1, groups * p2), F32)],
        scratch_shapes=[pltpu.VMEM((gb, nc, lh), F32), pltpu.VMEM((gb, nc, lh), F32),
                        pltpu.VMEM((nc, gb * p2), F32), pltpu.VMEM((nc, gb * p2), F32)],
        compiler_params=_params("parallel"),
    )(u, dy, sprev, m, bm, cm, apw, arows_rev)


def _split(x):
    hi = x.astype(BF16)
    return hi, (x - hi.astype(F32)).astype(BF16)


def _sb_more(kb, carries):
    top = jnp.max(carries[0])
    for c in carries[1:]:
        top = jnp.maximum(top, jnp.max(c))
    return (kb >= 0) & (top > SB_UNDERFLOW)


def _sb_fwd(name, qkv, heads):
    n_rows, dm3 = qkv.shape
    dm = dm3 // 3
    hd = dm // heads
    tq = _tile(n_rows // 2, 256)
    hb = min(SB_HEADS_FWD, heads)
    groups = heads // hb
    scale = hd ** -0.5

    def body(q_ref, k_ref, v_ref, o_ref):
        qi = pl.program_id(1)
        hs = range(hb)
        row = lax.broadcasted_iota(jnp.int32, (tq, tq), 0)
        col = lax.broadcasted_iota(jnp.int32, (tq, tq), 1)
        tri = (row > col).astype(BF16)
        causal = col < row
        qall = q_ref[...] * scale
        qb = [qall[:, h * hd:(h + 1) * hd] for h in hs]

        def block(kb, carries, accs, diagonal):
            ks = pl.ds(pl.multiple_of(kb * tq, tq), tq)
            kblk, vblk = k_ref[ks, :], v_ref[ks, :]
            z = [_dot(qb[h], kblk[:, h * hd:(h + 1) * hd], NT) for h in hs]
            sp = [_softplus(z[h]) for h in hs]
            lk = [-sp[h] for h in hs]
            if diagonal:
                lk = [jnp.where(causal, lk[h], 0.0) for h in hs]
            parts = [_split(lk[h]) for h in hs]
            r = [_dot(parts[h][0], tri) + _dot(parts[h][1], tri) for h in hs]
            a = [jnp.exp(z[h] - sp[h] + r[h] + carries[h]) for h in hs]
            if diagonal:
                a = [jnp.where(causal, a[h], 0.0) for h in hs]
            new_a = tuple(accs[h] + _dot(a[h].astype(BF16), vblk[:, h * hd:(h + 1) * hd]) for h in hs)
            new_c = tuple(carries[h] + jnp.sum(lk[h], axis=1, keepdims=True) for h in hs)
            return new_c, new_a

        zc = tuple(jnp.zeros((tq, 1), F32) for _ in hs)
        za = tuple(jnp.zeros((tq, hd), F32) for _ in hs)
        carries, accs = block(qi, zc, za, True)
        _, _, accs = lax.while_loop(lambda st: _sb_more(st[0], st[1]),
                                    lambda st: (st[0] - 1,) + block(st[0], st[1], st[2], False),
                                    (qi - 1, carries, accs))
        o_ref[...] = jnp.concatenate(accs, axis=1)

    lanes = hb * hd
    return pl.pallas_call(
        body, grid=(groups, n_rows // tq), name=name,
        in_specs=[pl.BlockSpec((tq, lanes), lambda g, i: (i, g)),
                  pl.BlockSpec((n_rows, lanes), lambda g, i: (0, groups + g)),
                  pl.BlockSpec((n_rows, lanes), lambda g, i: (0, 2 * groups + g))],
        out_specs=pl.BlockSpec((tq, lanes), lambda g, i: (i, g)),
        out_shape=jax.ShapeDtypeStruct((n_rows, dm), F32),
        compiler_params=_params("parallel", "arbitrary"))(qkv, qkv, qkv)


def _sb_bwd(name, qkv, o, do, heads):
    n_rows, dm3 = qkv.shape
    dm = dm3 // 3
    hd = dm // heads
    tq = _tile(n_rows // 2, 256)
    hb = min(SB_HEADS_BWD, heads)
    groups = heads // hb
    scale = hd ** -0.5

    def body(q_ref, k_ref, v_ref, o_ref, do_ref, dq_ref, dk_ref, dv_ref):
        qi = pl.program_id(1)

        @pl.when(qi == 0)
        def _():
            dk_ref[...] = jnp.zeros(dk_ref.shape, F32)
            dv_ref[...] = jnp.zeros(dv_ref.shape, F32)

        hs = range(hb)
        cols = [slice(h * hd, (h + 1) * hd) for h in hs]
        qall = q_ref[...] * scale
        doall = do_ref[...].astype(BF16)
        prod = doall.astype(F32) * o_ref[...]
        qb = [qall[:, cols[h]] for h in hs]
        dob16 = [doall[:, cols[h]] for h in hs]
        delta = [jnp.sum(prod[:, cols[h]], axis=1, keepdims=True) for h in hs]
        row = lax.broadcasted_iota(jnp.int32, (tq, tq), 0)
        col = lax.broadcasted_iota(jnp.int32, (tq, tq), 1)
        tri = (row > col).astype(BF16)
        tri_incl = (row >= col).astype(BF16)
        causal = col < row

        def block(kb, carries, pcarries, dqs, diagonal):
            ks = pl.ds(pl.multiple_of(kb * tq, tq), tq)
            kall, vall = k_ref[ks, :], v_ref[ks, :]
            kblk = [kall[:, cols[h]] for h in hs]
            vblk = [vall[:, cols[h]] for h in hs]
            z = [_dot(qb[h], kblk[h], NT) for h in hs]
            da = [_dot(dob16[h], vblk[h], NT) for h in hs]
            sp = [_softplus(z[h]) for h in hs]
            lk = [-sp[h] for h in hs]
            if diagonal:
                lk = [jnp.where(causal, lk[h], 0.0) for h in hs]
            lb = [z[h] - sp[h] for h in hs]
            parts = [_split(lk[h]) for h in hs]
            r = [_dot(parts[h][0], tri) + _dot(parts[h][1], tri) for h in hs]
            a = [jnp.exp(lb[h] + r[h] + carries[h]) for h in hs]
            if diagonal:
                a = [jnp.where(causal, a[h], 0.0) for h in hs]
            a16 = [a[h].astype(BF16) for h in hs]
            p = [da[h] * a16[h].astype(F32) for h in hs]
            pparts = [_split(p[h]) for h in hs]
            pc = [_dot(pparts[h][0], tri_incl) + _dot(pparts[h][1], tri_incl) for h in hs]
            beta = [jnp.exp(lb[h]) for h in hs]
            dz = [p[h] * (1.0 - beta[h]) - beta[h] * (delta[h] - pcarries[h] - pc[h]) for h in hs]
            if diagonal:
                dz = [jnp.where(causal, dz[h], 0.0) for h in hs]
            dz16 = [dz[h].astype(BF16) for h in hs]
            dk_ref[ks, :] += jnp.concatenate([_dot(dz16[h], qb[h], TN) for h in hs], axis=1)
            dv_ref[ks, :] += jnp.concatenate([_dot(a16[h], dob16[h], TN) for h in hs], axis=1)
            return (tuple(carries[h] + jnp.sum(lk[h], axis=1, keepdims=True) for h in hs),
                    tuple(pcarries[h] + jnp.sum(p[h], axis=1, keepdims=True) for h in hs),
                    tuple(dqs[h] + _dot(dz16[h], kblk[h]) for h in hs))

        zc = tuple(jnp.zeros((tq, 1), F32) for _ in hs)
        zq = tuple(jnp.zeros((tq, hd), F32) for _ in hs)
        st = block(qi, zc, zc, zq, True)
        st = lax.while_loop(lambda s: _sb_more(s[0], s[1]),
                            lambda s: (s[0] - 1,) + block(s[0], s[1], s[2], s[3], False), (qi - 1,) + st)
        dq_ref[...] = jnp.concatenate(st[3], axis=1) * scale

    lanes = hb * hd
    tile = pl.BlockSpec((tq, lanes), lambda g, i: (i, g))
    whole = pl.BlockSpec((n_rows, lanes), lambda g, i: (0, g), pipeline_mode=pl.Buffered(1))
    full = jax.ShapeDtypeStruct((n_rows, dm), F32)
    return pl.pallas_call(
        body, grid=(groups, n_rows // tq), name=name,
        in_specs=[tile, pl.BlockSpec((n_rows, lanes), lambda g, i: (0, groups + g), pipeline_mode=pl.Buffered(1)),
                  pl.BlockSpec((n_rows, lanes), lambda g, i: (0, 2 * groups + g), pipeline_mode=pl.Buffered(1)),
                  tile, tile],
        out_specs=[tile, whole, whole], out_shape=[full, full, full],
        compiler_params=_params("parallel", "arbitrary"))(qkv, qkv, qkv, o, do)


def _block_diag(xb, w_ref_val, dims):
    nb = w_ref_val.shape[0]
    bw = xb.shape[1] // nb
    return jnp.concatenate([_dot(xb[:, n * bw:(n + 1) * bw], w_ref_val[n], dims) for n in range(nb)], axis=1)


def _lru_gates_fwd(name, gx, conv_w, conv_b, wa, ba, wx, bx, lam):
    n_rows, w2 = gx.shape
    w = w2 // 2
    tm = _tile(n_rows, 256)

    def fn(i, nt, br, prev, cw, cb, wa_v, ba_v, wx_v, bx_v, lam_v):
        xc = cb + sum(cw[k:k + 1] * _shift_down(br, prev, LRU_CONV - 1 - k, i == 0) for k in range(LRU_CONV))
        xb = xc.astype(BF16)
        r = _sigmoid(_block_diag(xb, wa_v, NN) + ba_v)
        ig = _sigmoid(_block_diag(xb, wx_v, NN) + bx_v)
        log_a = (-LRU_C * r) * _softplus(-lam_v)
        a = jnp.exp(log_a)
        gated = (ig * xc) * _one_minus_a2_sqrt(log_a)
        return (xc, r, ig, a, gated), ()

    return _rowwise(name, fn, [('t', gx, w, 1), ('p', gx, w, 1), ('b', conv_w), ('b', conv_b), ('b', wa), ('b', ba),
                               ('b', wx), ('b', bx), ('b', lam)], [(w, F32)] * 5, [], n_rows, tm)


def _lru_scan_fwd(name, a, gated, gx):
    n_rows, w = a.shape
    tm = _tile(n_rows, 256)

    def body(a_ref, x_ref, bg_ref, hs_ref, y_ref, carry_ref):
        @pl.when(pl.program_id(0) == 0)
        def _():
            carry_ref[...] = jnp.zeros(carry_ref.shape, F32)
        carry_ref[...] = _scan_tile(a_ref, x_ref, hs_ref, carry_ref[...], False, tm)
        y_ref[...] = (_gelu(bg_ref[...]) * hs_ref[...]).astype(BF16)

    tok = pl.BlockSpec((tm, w), lambda i: (i, 0))
    return pl.pallas_call(body, grid=(n_rows // tm,), in_specs=[tok, tok, tok], out_specs=[tok, tok], name=name,
                          out_shape=[jax.ShapeDtypeStruct((n_rows, w), F32), jax.ShapeDtypeStruct((n_rows, w), BF16)],
                          scratch_shapes=[pltpu.VMEM((1, w), F32)], compiler_params=_params("arbitrary"))(a, gated, gx)


def _lru_scan_bwd(name, a, dy, gx):
    n_rows, w = a.shape
    tm = _tile(n_rows, 256)
    nt = n_rows // tm
    per8 = tm // SUBLANES

    def body(a_ref, an_ref, dy_ref, bg_ref, lam_ref, carry_ref, aup_ref, dhs_ref):
        i = pl.program_id(0)

        @pl.when(i == 0)
        def _():
            carry_ref[...] = jnp.zeros(carry_ref.shape, F32)
        aup_ref[...] = _shift_up(a_ref[...], an_ref[...], 1, i == 0)
        dhs_ref[...] = dy_ref[...] * _gelu(bg_ref[...])
        carry_ref[...] = _scan_tile(aup_ref, dhs_ref, lam_ref, carry_ref[...], True, tm)

    tok = pl.BlockSpec((tm, w), lambda i: (nt - 1 - i, 0))
    nxt = pl.BlockSpec((SUBLANES, w), lambda i: (jnp.minimum((nt - i) * per8, n_rows // SUBLANES - 1), 0))
    return pl.pallas_call(body, grid=(nt,), in_specs=[tok, nxt, tok, tok], out_specs=tok, name=name,
                          out_shape=jax.ShapeDtypeStruct((n_rows, w), F32),
                          scratch_shapes=[pltpu.VMEM((1, w), F32), pltpu.VMEM((tm, w), F32), pltpu.VMEM((tm, w), F32)],
                          compiler_params=_params("arbitrary"))(a, a, dy, gx)


def _lru_gates_bwd(name, lam_t, hs, xc, r, ig, a, wa, wx, lam):
    n_rows, w = xc.shape
    nb, bw, _ = wa.shape
    tm = _tile(n_rows, 256)

    def fn(i, nt, lt, hs_v, hs_prev, xc_v, r_v, ig_v, a_v, wa_v, wx_v, lam_v):
        sp = _softplus(-lam_v)
        log_a = (-LRU_C * r_v) * sp
        mult = _one_minus_a2_sqrt(log_a)
        d_a = lt * _shift_down(hs_v, hs_prev, 1, i == 0)
        d_ig = lt * xc_v * mult
        d_mult = lt * ig_v * xc_v
        d_log_a = d_a * a_v - d_mult * (a_v * a_v) / mult
        d_ra = d_log_a * (-LRU_C * sp) * r_v * (1.0 - r_v)
        d_ia = d_ig * ig_v * (1.0 - ig_v)
        d_ra16, d_ia16, xb = d_ra.astype(BF16), d_ia.astype(BF16), xc_v.astype(BF16)
        dxc = lt * ig_v * mult + _block_diag(d_ra16, wa_v, NT) + _block_diag(d_ia16, wx_v, NT)
        dwa = jnp.concatenate([_dot(xb[:, n * bw:(n + 1) * bw], d_ra16[:, n * bw:(n + 1) * bw], TN)
                               for n in range(nb)], axis=0)
        dwx = jnp.concatenate([_dot(xb[:, n * bw:(n + 1) * bw], d_ia16[:, n * bw:(n + 1) * bw], TN)
                               for n in range(nb)], axis=0)
        col = lambda t: jnp.sum(t, axis=0, keepdims=True)
        return (dxc,), (dwa, dwx, col(d_ra), col(d_ia), col(d_log_a * (-LRU_C * r_v)))

    tiled = lambda arr: ('t', arr, w, 0)
    return _rowwise(name, fn, [tiled(lam_t), tiled(hs), ('p', hs, w, 0), tiled(xc), tiled(r), tiled(ig), tiled(a),
                               ('b', wa), ('b', wx), ('b', lam)],
                    [(w, F32)], [(nb * bw, bw), (nb * bw, bw), (1, w), (1, w), (1, w)], n_rows, tm)


def _lru_conv_bwd(name, dxc, gx, dy, hs, conv_w):
    n_rows, w = dxc.shape
    tm = _tile(n_rows, 256)

    def fn(i, nt, dxc_v, dxc_next, bg, br, br_prev, dy_v, hs_v, cw):
        dbr = sum(cw[k:k + 1] * _shift_up(dxc_v, dxc_next, LRU_CONV - 1 - k, i == nt - 1) for k in range(LRU_CONV))
        dbg = dy_v * hs_v * _gelu_grad(bg)
        dcw = [jnp.sum(dxc_v * _shift_down(br, br_prev, LRU_CONV - 1 - k, i == 0), axis=0, keepdims=True)
               for k in range(LRU_CONV)]
        dcw = jnp.concatenate(dcw + [jnp.zeros((SUBLANES - LRU_CONV, w), F32)], axis=0)
        return (jnp.concatenate([dbg, dbr], axis=1),), (dcw, jnp.sum(dxc_v, axis=0, keepdims=True))

    return _rowwise(name, fn, [('t', dxc, w, 0), ('n', dxc, w, 0), ('t', gx, w, 0), ('t', gx, w, 1), ('p', gx, w, 1),
                               ('t', dy, w, 0), ('t', hs, w, 0), ('b', conv_w)],
                    [(2 * w, BF16)], [(SUBLANES, w), (1, w)], n_rows, tm)


def _loss_head(name, h, gain, target):
    n_rows, dm = h.shape
    tm = _tile(n_rows, 512)

    def fn(i, nt, hv, tv, g):
        r, xhat = _rms(hv)
        err = xhat * g - tv
        dy = err * (1.0 / dm)
        return ((_rms_bwd(dy, xhat, r, g),),
                (jnp.sum(err * err, axis=0, keepdims=True), jnp.sum(dy * xhat, axis=0, keepdims=True)))

    return _rowwise(name, fn, [('t', h, dm, 0), ('t', target, dm, 0), ('b', gain)], [(dm, F32)], [(1, dm), (1, dm)],
                    n_rows, tm)


def _adamw(name, gparts, w, m, v):
    n_parts, n_rows, cols = gparts.shape
    tr = n_rows
    for cand in (256, 128, 64, 32, 16, 8):
        if n_rows % cand == 0:
            tr = cand
            break
    c1 = 1.0 - ADAM_B1 ** ADAM_STEP
    c2 = 1.0 - ADAM_B2 ** ADAM_STEP

    def body(gp_ref, w_ref, m_ref, v_ref, g_ref, d_ref, nm_ref, nv_ref):
        g = gp_ref[0].astype(F32)
        for p in range(1, n_parts):
            g = g + gp_ref[p].astype(F32)
        m_new = ADAM_B1 * m_ref[...] + (1.0 - ADAM_B1) * g
        v_new = ADAM_B2 * v_ref[...] + (1.0 - ADAM_B2) * (g * g)
        m_hat = m_new / c1
        v_hat = v_new / c2
        g_ref[...] = g
        d_ref[...] = -ADAM_LR * (m_hat / (jnp.sqrt(v_hat) + ADAM_EPS) + ADAM_WD * w_ref[...])
        nm_ref[...] = m_new
        nv_ref[...] = v_new

    blk = pl.BlockSpec((tr, cols), lambda i: (i, 0))
    shp = jax.ShapeDtypeStruct((n_rows, cols), F32)
    return pl.pallas_call(body, grid=(n_rows // tr,), name=name,
                          in_specs=[pl.BlockSpec((n_parts, tr, cols), lambda i: (0, i, 0)), blk, blk, blk],
                          out_specs=[blk, blk, blk, blk], out_shape=[shp, shp, shp, shp],
                          compiler_params=_params("parallel"))(gparts, w, m, v)


def _adamw_layers(name, recvs, w, m, v, comms=()):
    n_layers, n_rows, cols = w.shape
    n_parts = recvs[0].shape[0]
    tr = max(t for t in range(16, ADAMW_LAYER_ROWS + 1, 16) if n_rows % t == 0)
    c1 = 1.0 - ADAM_B1 ** ADAM_STEP
    c2 = 1.0 - ADAM_B2 ** ADAM_STEP

    def body(*refs):
        gp_refs = refs[:n_layers]
        w_ref, m_ref, v_ref, g_ref, d_ref, nm_ref, nv_ref = refs[n_layers:]
        layer = pl.program_id(0)
        for k in range(n_layers):
            @pl.when(layer == k)
            def _(k=k):
                g = gp_refs[k][0].astype(F32)
                for p in range(1, n_parts):
                    g = g + gp_refs[k][p].astype(F32)
                m_new = ADAM_B1 * m_ref[...] + (1.0 - ADAM_B1) * g
                v_new = ADAM_B2 * v_ref[...] + (1.0 - ADAM_B2) * (g * g)
                g_ref[...] = g
                d_ref[...] = -ADAM_LR * ((m_new / c1) / (jnp.sqrt(v_new / c2) + ADAM_EPS) + ADAM_WD * w_ref[...])
                nm_ref[...] = m_new
                nv_ref[...] = v_new

    blk = pl.BlockSpec((None, tr, cols), lambda l, i: (l, i, 0))
    shp = jax.ShapeDtypeStruct((n_layers, n_rows, cols), F32)
    gp_specs = [pl.BlockSpec((n_parts, tr, cols), lambda l, i, k=k: (0, jnp.where(l == k, i, 0), 0))
                for k in range(n_layers)]
    return _call(body, grid=(n_layers, n_rows // tr), name=name, in_specs=gp_specs + [blk, blk, blk],
                 out_specs=[blk, blk, blk, blk], out_shape=[shp, shp, shp, shp], args=[*recvs, w, m, v], comms=comms)


def _pack_rows(arrays, cols, lead=0):
    flat = [a.reshape(a.shape[:lead] + (-1,)) for a in arrays]
    cat = jnp.concatenate(flat, axis=lead) if len(flat) > 1 else flat[0]
    n = cat.shape[lead]
    pad = (-n) % (cols * PACK_ROWS)
    if pad:
        cat = jnp.pad(cat, [(0, 0)] * lead + [(0, pad)])
    return cat.reshape(cat.shape[:lead] + (-1, cols))


def _unpack_rows(packed, shapes, lead=0):
    flat = packed.reshape(packed.shape[:lead] + (-1,))
    out, off = [], 0
    for s in shapes:
        n = math.prod(s)
        out.append(lax.slice_in_dim(flat, off, off + n, axis=lead).reshape(flat.shape[:lead] + tuple(s)))
        off += n
    return out


def kernel(x, ffn1_norm, ffn1_w_in, ffn1_w_out, mix_norm, ffn2_norm, ffn2_w_in, ffn2_w_out, final_norm, s5_w_in, s5_lam_re, s5_lam_im, s5_log_dt, s5_b_re, s5_b_im, s5_c_re, s5_c_im, s5_d, s5_w_out, sb_w_qkv, sb_w_out, lru_w_in, lru_conv_w, lru_conv_b, lru_w_a, lru_b_a, lru_w_x, lru_b_x, lru_lambda, lru_w_out, loss_target, m_ffn1_norm, m_ffn1_w_in, m_ffn1_w_out, m_mix_norm, m_ffn2_norm, m_ffn2_w_in, m_ffn2_w_out, m_final_norm, m_s5_w_in, m_s5_lam_re, m_s5_lam_im, m_s5_log_dt, m_s5_b_re, m_s5_b_im, m_s5_c_re, m_s5_c_im, m_s5_d, m_s5_w_out, m_sb_w_qkv, m_sb_w_out, m_lru_w_in, m_lru_conv_w, m_lru_conv_b, m_lru_w_a, m_lru_b_a, m_lru_w_x, m_lru_b_x, m_lru_lambda, m_lru_w_out, v_ffn1_norm, v_ffn1_w_in, v_ffn1_w_out, v_mix_norm, v_ffn2_norm, v_ffn2_w_in, v_ffn2_w_out, v_final_norm, v_s5_w_in, v_s5_lam_re, v_s5_lam_im, v_s5_log_dt, v_s5_b_re, v_s5_b_im, v_s5_c_re, v_s5_c_im, v_s5_d, v_s5_w_out, v_sb_w_qkv, v_sb_w_out, v_lru_w_in, v_lru_conv_w, v_lru_conv_b, v_lru_w_a, v_lru_b_a, v_lru_w_x, v_lru_b_x, v_lru_lambda, v_lru_w_out):
    local = dict(locals())
    W = {n: local[n] for n in WEIGHTS}
    M = {n: local["m_" + n] for n in WEIGHTS}
    V = {n: local["v_" + n] for n in WEIGHTS}

    h0 = x[0]
    target = loss_target[0]
    n_rows, dm = h0.shape
    depth = ffn1_norm.shape[0]

    ffn_seq = [(tag, layer) for layer in range(depth) for tag in ("ffn1", "ffn2")]

    def ffn_shards(tag, layer):
        return W[tag + "_w_in"][layer].astype(BF16), W[tag + "_w_out"][layer].astype(BF16)

    def ffn_views(wi, wo):
        return wi.reshape((2, N_DEV // 2) + wi.shape[1:]), wo.reshape((N_DEV // 2, 2) + wo.shape[1:])

    first_in, first_out = ffn_shards(*ffn_seq[0])
    ffn_w = {ffn_seq[0]: ffn_views(_all_gather("ag_first_w_in", first_in), _all_gather("ag_first_w_out", first_out))}

    ffn_saved = {}

    def ffn_forward(pos, h_in):
        tag, layer = ffn_seq[pos]
        comms = [("gather", a) for a in ffn_shards(*ffn_seq[pos + 1])] if pos + 1 < len(ffn_seq) else []
        res = _ffn_fwd("%s_fwd_%d" % (tag, layer), h_in, W[tag + "_norm"][layer:layer + 1], *ffn_w[ffn_seq[pos]],
                       comms=comms)
        ffn_saved[ffn_seq[pos]] = (res[1], res[2])
        if comms:
            ffn_w[ffn_seq[pos + 1]] = ffn_views(res[3], res[4])
        return res[0]

    mix_shapes = [W[n].shape for n in MIXER_BIG]

    def unpack_mixers(names, gathered):
        parts = _unpack_rows(gathered, [W[n].shape for n in names], lead=1)
        return {n: _unshard(a, SHARD_AXIS[n]) for n, a in zip(names, parts)}

    full = unpack_mixers(MIXER_EARLY, _all_gather("ag_mixers", _pack_rows([W[n].astype(BF16) for n in MIXER_EARLY], dm)))
    late_mixers = _pack_rows([W[n].astype(BF16) for n in MIXER_LATE], dm)
    small_shapes = [W[n].shape for n in SMALL_SHARDED]
    small_g = _all_gather("ag_small", _pack_rows([W[n] for n in SMALL_SHARDED], 128))
    full.update({n: _unshard(a, SHARD_AXIS[n])
                 for n, a in zip(SMALL_SHARDED, _unpack_rows(small_g, small_shapes, lead=1))})

    n_s5 = s5_w_in.shape[0]
    s5_groups = s5_lam_re.shape[1]
    heads = dm // SB_HEAD_DIM

    grads = {}
    saved = []
    h = h0

    for layer in range(depth):
        kind, j = layer % 3, layer // 3
        rec = {"h0": h}
        h = ffn_forward(2 * layer, h)
        rec["h1"] = h
        gain = mix_norm[layer:layer + 1]
        if kind == 0:
            (u,) = _mm_fwd("s5_in_%d" % layer, h, full["s5_w_in"][j], F32, gain=gain)
            pars = (s5_lam_re[j], s5_lam_im[j], s5_log_dt[j], s5_b_re[j], s5_b_im[j], s5_c_re[j], s5_c_im[j])
            mats, mats_vjp = jax.vjp(_s5_mats, *pars)
            apw, ar_fwd, ar_rev = _s5_powers(*pars[:3])
            res = _s5_fwd("s5_core_%d" % layer, u, *mats[:3], apw, ar_fwd,
                          comms=[("gather", late_mixers)] if layer == 0 else [])
            ys, sprev = res[0], res[1]
            if layer == 0:
                full.update(unpack_mixers(MIXER_LATE, res[2]))
            d_skip = full["s5_d"][j:j + 1]
            (z,) = _rowwise("s5_gelu_%d" % layer, lambda i, nt, ys_v, u_v, d_v: ((_gelu(ys_v + d_v * u_v),), ()),
                            [('t', ys, dm, 0), ('t', u, dm, 0), ('b', d_skip)], [(dm, BF16)], [], n_rows,
                            _tile(n_rows, 512))
            h, vg = _mm_fwd("s5_out_%d" % layer, z, full["s5_w_out"][j], F32, resid=h, glu=True)
            rec.update(u=u, ys=ys, sprev=sprev, z=z, vg=vg, mats=mats, mats_vjp=mats_vjp, apw=apw,
                       ar_rev=ar_rev, d_skip=d_skip)
        elif kind == 1:
            (qkv,) = _mm_fwd("sb_in_%d" % layer, h, full["sb_w_qkv"][j], BF16, gain=gain)
            o = _sb_fwd("sb_attn_%d" % layer, qkv, heads)
            (h,) = _mm_fwd("sb_out_%d" % layer, o, full["sb_w_out"][j], F32, resid=h)
            rec.update(qkv=qkv, o=o)
        else:
            (gx,) = _mm_fwd("lru_in_%d" % layer, h, full["lru_w_in"][j], F32, gain=gain)
            wa, wx = full["lru_w_a"][j], full["lru_w_x"][j]
            ba, bx = full["lru_b_a"][j].reshape(1, dm), full["lru_b_x"][j].reshape(1, dm)
            lam_row = full["lru_lambda"][j:j + 1]
            xc, r, ig, a, gated = _lru_gates_fwd("lru_gates_%d" % layer, gx, full["lru_conv_w"][j],
                                                 full["lru_conv_b"][j:j + 1], wa, ba, wx, bx, lam_row)
            hs, y = _lru_scan_fwd("lru_scan_%d" % layer, a, gated, gx)
            (h,) = _mm_fwd("lru_out_%d" % layer, y, full["lru_w_out"][j], F32, resid=h)
            rec.update(gx=gx, xc=xc, r=r, ig=ig, a=a, hs=hs, y=y, wa=wa, wx=wx, lam_row=lam_row)
        rec["h2"] = h
        h = ffn_forward(2 * layer + 1, h)
        saved.append(rec)

    dh, err2, dgf = _loss_head("loss_head", h, final_norm.reshape(1, dm), target)
    loss = lax.psum(0.5 / dm * jnp.sum(err2), ("x", "y", "c"))
    grads["final_norm"] = dgf.reshape(final_norm.shape)

    per_layer = {n: [None] * depth for n in ("ffn1_norm", "mix_norm", "ffn2_norm")}
    mixer_grads = {}
    recv_ffn = {}
    pending = []

    def ffn_backward(tag, layer, x_in, dh_in, more=None):
        comms = {0: [("exchange", pending[1])], 1: [("exchange", pending[2])]} if pending else {}
        comms.update(more or {})
        dx, dwi, dwo, dg, extra = _ffn_bwd("%s_bwd_%d" % (tag, layer), x_in, dh_in, W[tag + "_norm"][layer:layer + 1],
                                            *ffn_w[(tag, layer)], *ffn_saved[(tag, layer)], comms_by_block=comms)
        if pending:
            recv_ffn[pending[0]] = tuple(extra[:2])
            extra = extra[2:]
        pending[:] = [(tag, layer), dwi.reshape((N_DEV,) + dwi.shape[2:]), dwo.reshape(N_DEV, -1, dm)]
        per_layer[tag + "_norm"][layer] = dg
        return dx, extra

    def put(name, j, value, count):
        mixer_grads.setdefault(name, [None] * count)[j] = value

    for layer in reversed(range(depth)):
        kind, j = layer % 3, layer // 3
        rec = saved[layer]
        dh, _ = ffn_backward("ffn2", layer, rec["h2"], dh)
        gain = mix_norm[layer:layer + 1]
        if kind == 0:
            dvg, = _rowwise("s5_glu_bwd_%d" % layer,
                            lambda i, nt, d_v, vg_v: ((jnp.concatenate(
                                [d_v * _sigmoid(vg_v[:, dm:]),
                                 d_v * vg_v[:, :dm] * _sigmoid(vg_v[:, dm:]) * (1.0 - _sigmoid(vg_v[:, dm:]))],
                                axis=1),), ()),
                            [('t', dh, dm, 0), ('t', rec["vg"], 2 * dm, 0)], [(2 * dm, BF16)], [], n_rows,
                            _tile(n_rows, 256))
            dz, dw_out = _mm_bwd("s5_out_bwd_%d" % layer, rec["z"], dvg, full["s5_w_out"][j])

            def gelu_bwd(i, nt, dz_v, ys_v, u_v, d_v):
                dy_v = dz_v * _gelu_grad(ys_v + d_v * u_v)
                return (dy_v,), (jnp.sum(dy_v * u_v, axis=0, keepdims=True),)

            dys, dd = _rowwise("s5_gelu_bwd_%d" % layer, gelu_bwd,
                               [('t', dz, dm, 0), ('t', rec["ys"], dm, 0), ('t', rec["u"], dm, 0),
                                ('b', rec["d_skip"])], [(dm, F32)], [(1, dm)], n_rows, _tile(n_rows, 512))
            m_, bm_, cm_, _ = rec["mats"]
            du_core, dm_m, dm_b, dm_c, d_a = _s5_bwd("s5_core_bwd_%d" % layer, rec["u"], dys, rec["sprev"],
                                                    m_, bm_, cm_, rec["apw"], rec["ar_rev"])
            dpars = rec["mats_vjp"]((dm_m, dm_b, dm_c, d_a.reshape(s5_groups, -1)))
            for nme, val in zip(("s5_lam_re", "s5_lam_im", "s5_log_dt", "s5_b_re", "s5_b_im", "s5_c_re", "s5_c_im"),
                                dpars):
                put(nme, j, val, n_s5)
            (du,) = _rowwise("s5_du_%d" % layer, lambda i, nt, a_v, dy_v, d_v: ((a_v + dy_v * d_v,), ()),
                             [('t', du_core, dm, 0), ('t', dys, dm, 0), ('b', rec["d_skip"])],
                             [(dm, BF16)], [], n_rows, _tile(n_rows, 512))
            dh, dw_in, dgm = _mm_bwd("s5_in_bwd_%d" % layer, rec["h1"], du, full["s5_w_in"][j], gain=gain, dres=dh)
            put("s5_d", j, dd[0], n_s5)
            put("s5_w_out", j, dw_out, n_s5)
            put("s5_w_in", j, dw_in, n_s5)
        elif kind == 1:
            do, dw_out = _mm_bwd("sb_out_bwd_%d" % layer, rec["o"], dh, full["sb_w_out"][j])
            dq, dk, dv = _sb_bwd("sb_attn_bwd_%d" % layer, rec["qkv"], rec["o"], do, heads)
            dqkv = jnp.concatenate([dq, dk, dv], axis=1).astype(BF16)
            dh, dw_in, dgm = _mm_bwd("sb_in_bwd_%d" % layer, rec["h1"], dqkv, full["sb_w_qkv"][j], gain=gain, dres=dh)
            put("sb_w_out", j, dw_out, 1)
            put("sb_w_qkv", j, dw_in, 1)
        else:
            dy, dw_out = _mm_bwd("lru_out_bwd_%d" % layer, rec["y"], dh, full["lru_w_out"][j])
            lam_t = _lru_scan_bwd("lru_scan_bwd_%d" % layer, rec["a"], dy, rec["gx"])
            dxc, dwa, dwx, dba, dbx, dsp = _lru_gates_bwd("lru_gates_bwd_%d" % layer, lam_t, rec["hs"], rec["xc"],
                                                          rec["r"], rec["ig"], rec["a"], rec["wa"], rec["wx"],
                                                          rec["lam_row"])
            dgx, dcw, dcb = _lru_conv_bwd("lru_conv_bwd_%d" % layer, dxc, rec["gx"], dy, rec["hs"],
                                          full["lru_conv_w"][j])
            dh, dw_in, dgm = _mm_bwd("lru_in_bwd_%d" % layer, rec["h1"], dgx, full["lru_w_in"][j], gain=gain, dres=dh)
            nb = rec["wa"].shape[0]
            put("lru_w_out", j, dw_out, 1)
            put("lru_w_in", j, dw_in, 1)
            put("lru_w_a", j, dwa.reshape(rec["wa"].shape), 1)
            put("lru_w_x", j, dwx.reshape(rec["wx"].shape), 1)
            put("lru_b_a", j, dba.reshape(nb, -1), 1)
            put("lru_b_x", j, dbx.reshape(nb, -1), 1)
            put("lru_conv_w", j, dcw[:LRU_CONV], 1)
            put("lru_conv_b", j, dcb[0], 1)
            put("lru_lambda", j, (dsp * -_sigmoid(-rec["lam_row"]))[0], 1)
        per_layer["mix_norm"][layer] = dgm
        more = None
        if layer == 0:
            for n, parts in mixer_grads.items():
                grads[n] = jnp.stack(parts)
            send = _pack_rows([_shard_blocks(grads[n], SHARD_AXIS[n]).astype(BF16) for n in MIXER_BIG], dm, lead=1)
            half = send.shape[1] // 2
            more = {2: [("exchange", send[:, :half])], 3: [("exchange", send[:, half:])]}
        dh, got = ffn_backward("ffn1", layer, rec["h0"], dh, more)
        if layer == 0:
            recv_mixers = jnp.concatenate(got, axis=1)

    grad_x = dh[None]
    for n in ("ffn1_norm", "mix_norm", "ffn2_norm"):
        grads[n] = jnp.concatenate(per_layer[n], axis=0)

    out_g, out_d, out_m, out_v = {}, {}, {}, {}

    def finish(names, res, shapes):
        for n, g_, d_, m_, v_ in zip(names, *[_unpack_rows(t, shapes) for t in res]):
            out_g[n], out_d[n], out_m[n], out_v[n] = g_, d_, m_, v_

    last_recv = []
    for tag in ("ffn2", "ffn1"):
        for which, n in enumerate((tag + "_w_in", tag + "_w_out")):
            turn = (lambda t: jnp.swapaxes(t, 1, 2)) if which == 0 else (lambda t: t)
            if tag == "ffn1":
                recv_ffn[pending[0]] = tuple(last_recv)
            res = _adamw_layers("adamw_" + n, [recv_ffn[(tag, layer)][which] for layer in range(depth)],
                                turn(W[n]), turn(M[n]), turn(V[n]),
                                comms=[("exchange", pending[1 + which])] if tag == "ffn2" else ())
            if tag == "ffn2":
                last_recv.append(res[4])
            out_g[n], out_d[n], out_m[n], out_v[n] = [turn(r) for r in res[:4]]

    finish(MIXER_BIG, _adamw("adamw_mixers", recv_mixers,
                             *[_pack_rows([t[n] for n in MIXER_BIG], dm) for t in (W, M, V)]), mix_shapes)

    small_names = REPLICATED + SMALL_SHARDED
    small_full_shapes = [grads[n].shape for n in small_names]
    parts = _all_gather("ag_small_grads", _pack_rows([grads[n] for n in small_names], 128))
    zero = jnp.zeros(parts.shape[1:], F32)
    summed = _adamw("sum_small_grads", parts, zero, zero, zero)[0]
    small_sum = dict(zip(small_names, _unpack_rows(summed, small_full_shapes)))
    me = 4 * lax.axis_index("x") + 2 * lax.axis_index("y") + lax.axis_index("c")
    rep_shapes = [W[n].shape for n in REPLICATED]
    g_rep = _pack_rows([small_sum[n] for n in REPLICATED], 128)[None]
    finish(REPLICATED, _adamw("adamw_replicated", g_rep, *[_pack_rows([t[n] for n in REPLICATED], 128)
                                                           for t in (W, M, V)]), rep_shapes)
    g_loc = []
    for n in SMALL_SHARDED:
        ax = SHARD_AXIS[n]
        size = W[n].shape[ax]
        g_loc.append(lax.dynamic_slice_in_dim(small_sum[n], me * size, size, axis=ax))
    finish(SMALL_SHARDED, _adamw("adamw_small", _pack_rows(g_loc, 128)[None],
                                 *[_pack_rows([t[n] for n in SMALL_SHARDED], 128) for t in (W, M, V)]), small_shapes)

    return (loss, grad_x, *[out_g[n] for n in WEIGHTS], *[out_d[n] for n in WEIGHTS],
            *[out_m[n] for n in WEIGHTS], *[out_v[n] for n in WEIGHTS])
```

```python
import functools
import math

import jax
import jax.numpy as jnp
from jax import lax
from jax.experimental import pallas as pl
from jax.experimental.pallas import tpu as pltpu

F32 = jnp.float32
BF16 = jnp.bfloat16
HI = lax.Precision.HIGHEST
S5_PREC = lax.Precision.HIGH
MESH = pl.DeviceIdType.MESH

N_DEV = 8
RMS_EPS = 1e-6
S5_GROUP = 16
S5_CHUNK = 16
S5_OCTET = 128 // S5_GROUP
S5_REGROUP_ROWS = 32
FFN_ROW_PARTS = 2
S5_SCAN_UNROLL = 8
SB_HEAD_DIM = 64
SB_UNDERFLOW = -104.0
SB_HEADS_FWD = 8
SB_HEADS_BWD = 4
LRU_CONV = 4
LRU_C = 8.0
ADAM_LR, ADAM_B1, ADAM_B2, ADAM_EPS, ADAM_WD, ADAM_STEP = 0.001, 0.9, 0.999, 1e-08, 0.01, 10
VMEM_LIMIT_BYTES = 56 * 1024 * 1024
SUBLANES = 8
PACK_ROWS = 256
ADAMW_LAYER_ROWS = 192

NN = (((1,), (0,)), ((), ()))
NT = (((1,), (1,)), ((), ()))
TN = (((0,), (0,)), ((), ()))

SHARD_AXIS = dict(
    ffn1_w_in=2, ffn1_w_out=1, ffn2_w_in=2, ffn2_w_out=1, s5_w_in=1, s5_d=1, s5_w_out=2, sb_w_qkv=2, sb_w_out=1,
    lru_w_in=2, lru_conv_w=2, lru_conv_b=1, lru_w_a=2, lru_b_a=2, lru_w_x=2, lru_b_x=2, lru_lambda=1, lru_w_out=1)
MIXER_BIG = ("s5_w_in", "s5_w_out", "sb_w_qkv", "sb_w_out", "lru_w_in", "lru_w_a", "lru_w_x", "lru_w_out")
MIXER_EARLY = MIXER_BIG[:2]
MIXER_LATE = MIXER_BIG[2:]
SMALL_SHARDED = ("s5_d", "lru_conv_w", "lru_conv_b", "lru_b_a", "lru_b_x", "lru_lambda")
REPLICATED = ("ffn1_norm", "mix_norm", "ffn2_norm", "final_norm", "s5_lam_re", "s5_lam_im", "s5_log_dt",
              "s5_b_re", "s5_b_im", "s5_c_re", "s5_c_im")
WEIGHTS = ("ffn1_norm", "ffn1_w_in", "ffn1_w_out", "mix_norm", "ffn2_norm", "ffn2_w_in", "ffn2_w_out", "final_norm",
           "s5_w_in", "s5_lam_re", "s5_lam_im", "s5_log_dt", "s5_b_re", "s5_b_im", "s5_c_re", "s5_c_im", "s5_d",
           "s5_w_out", "sb_w_qkv", "sb_w_out", "lru_w_in", "lru_conv_w", "lru_conv_b", "lru_w_a", "lru_b_a",
           "lru_w_x", "lru_b_x", "lru_lambda", "lru_w_out")


def _dot(a, b, dims=NN, prec=None):
    return lax.dot_general(a, b, dims, precision=prec, preferred_element_type=F32)


def _params(*sem):
    return pltpu.CompilerParams(dimension_semantics=sem, vmem_limit_bytes=VMEM_LIMIT_BYTES)


def _tile(n, pref):
    return min(pref, n)


def _sigmoid(x):
    return jax.nn.sigmoid(x)


def _softplus(x):
    return jnp.maximum(x, 0.0) + jnp.log(1.0 + jnp.exp(-jnp.abs(x)))


_GELU_C = math.sqrt(2.0 / math.pi)


def _gelu(x):
    return 0.5 * x * (1.0 + jnp.tanh(_GELU_C * (x + 0.044715 * x * x * x)))


def _gelu_grad(x):
    t = jnp.tanh(_GELU_C * (x + 0.044715 * x * x * x))
    return 0.5 * (1.0 + t) + 0.5 * x * (1.0 - t * t) * _GELU_C * (1.0 + 3.0 * 0.044715 * x * x)


def _rms(x):
    r = lax.rsqrt(jnp.mean(x * x, axis=1, keepdims=True) + RMS_EPS)
    return r, x * r


def _rms_bwd(dhn, xhat, r, g):
    dxhat = dhn * g
    return r * (dxhat - xhat * jnp.mean(dxhat * xhat, axis=1, keepdims=True))


def _one_minus_a2_sqrt(log_a):
    t = jnp.tanh(log_a)
    return jnp.sqrt(-2.0 * t / (1.0 - t))


def _shift_down(cur, prev8, k, first):
    if k == 0:
        return cur
    row8 = lax.broadcasted_iota(jnp.int32, prev8.shape, 0)
    rolled = pltpu.roll(cur, k, 0)
    edge = jnp.where(first, 0.0, pltpu.roll(prev8, k, 0))
    top = jnp.where(row8 < k, edge, rolled[0:SUBLANES])
    return jnp.concatenate([top, rolled[SUBLANES:]], axis=0)


def _shift_up(cur, next8, k, last):
    if k == 0:
        return cur
    tm = cur.shape[0]
    row8 = lax.broadcasted_iota(jnp.int32, next8.shape, 0)
    rolled = pltpu.roll(cur, tm - k, 0)
    edge = jnp.where(last, 0.0, pltpu.roll(next8, SUBLANES - k, 0))
    bottom = jnp.where(row8 >= SUBLANES - k, edge, rolled[tm - SUBLANES:tm])
    return jnp.concatenate([rolled[:tm - SUBLANES], bottom], axis=0)


def _rowwise(name, fn, ins, out_tiled, out_acc, n_rows, tm, reverse=False):
    nt = n_rows // tm
    per8 = tm // SUBLANES
    n8 = n_rows // SUBLANES
    n_in, n_ot = len(ins), len(out_tiled)

    def pos(i):
        return nt - 1 - i if reverse else i

    in_specs, args = [], []
    for spec in ins:
        kind, arr = spec[0], spec[1]
        args.append(arr)
        if kind == 'b':
            in_specs.append(pl.BlockSpec(arr.shape, lambda i, nd=arr.ndim: (0,) * nd))
        elif kind == 't':
            in_specs.append(pl.BlockSpec((tm, spec[2]), lambda i, cb=spec[3]: (pos(i), cb)))
        elif kind == 'p':
            in_specs.append(pl.BlockSpec((SUBLANES, spec[2]),
                                         lambda i, cb=spec[3]: (jnp.maximum(pos(i) * per8 - 1, 0), cb)))
        else:
            in_specs.append(pl.BlockSpec((SUBLANES, spec[2]),
                                         lambda i, cb=spec[3]: (jnp.minimum((pos(i) + 1) * per8, n8 - 1), cb)))

    def body(*refs):
        i = pl.program_id(0)
        outs = refs[n_in:]
        touts, aouts = fn(pos(i), nt, *[r[...] for r in refs[:n_in]])
        for r, v in zip(outs[:n_ot], touts):
            r[...] = v.astype(r.dtype)
        if out_acc:
            @pl.when(i == 0)
            def _():
                for r in outs[n_ot:]:
                    r[...] = jnp.zeros(r.shape, r.dtype)
            for r, v in zip(outs[n_ot:], aouts):
                r[...] += v

    out_specs = [pl.BlockSpec((tm, n), lambda i: (pos(i), 0)) for n, _ in out_tiled]
    out_specs += [pl.BlockSpec((r, n), lambda i: (0, 0)) for r, n in out_acc]
    out_shape = [jax.ShapeDtypeStruct((n_rows, n), dt) for n, dt in out_tiled]
    out_shape += [jax.ShapeDtypeStruct((r, n), F32) for r, n in out_acc]
    return pl.pallas_call(body, grid=(nt,), in_specs=in_specs, out_specs=out_specs, out_shape=out_shape, name=name,
                          compiler_params=_params("arbitrary"))(*args)


def _all_gather(name, block):
    def body(x_ref, out_ref, send_sems, recv_sems, local_sem):
        x, y, c = lax.axis_index("x"), lax.axis_index("y"), lax.axis_index("c")
        me, sibling = (x, y, c), (x, y, 1 - c)
        chips = [(1 - x, y), (x, 1 - y), (1 - x, 1 - y)]

        def rows(px, py, pc):
            return out_ref.at[4 * px + 2 * py + pc]

        def copy(k, blk, to, src=None):
            return pltpu.make_async_remote_copy(
                src_ref=rows(*blk) if src is None else src, dst_ref=rows(*blk),
                send_sem=send_sems.at[k], recv_sem=recv_sems.at[k], device_id=to, device_id_type=MESH)

        mine = pltpu.make_async_copy(x_ref, rows(*me), local_sem)
        mine.start()
        first = [copy(0, me, sibling, src=x_ref)]
        first += [copy(1 + j, me, (*chip, c), src=x_ref) for j, chip in enumerate(chips)]
        for cp in first:
            cp.start()
        passed = [copy(4 + j, (*chip, c), sibling) for j, chip in enumerate(chips)]
        for j, chip in enumerate(chips):
            copy(1 + j, (*chip, c), me).wait_recv()
            passed[j].start()
        copy(0, sibling, me).wait_recv()
        for j, chip in enumerate(chips):
            copy(4 + j, (*chip, 1 - c), me).wait_recv()
        for cp in first + passed:
            cp.wait_send()
        mine.wait()

    return pl.pallas_call(
        body, name=name, out_shape=jax.ShapeDtypeStruct((N_DEV,) + block.shape, block.dtype),
        in_specs=[pl.BlockSpec(memory_space=pl.ANY)], out_specs=pl.BlockSpec(memory_space=pl.ANY),
        scratch_shapes=[pltpu.SemaphoreType.DMA((7,)), pltpu.SemaphoreType.DMA((7,)), pltpu.SemaphoreType.DMA(())],
    )(block)


def _exchange(name, send):
    def body(s_ref, r_ref, send_sems, recv_sems, local_sem):
        x, y, c = lax.axis_index("x"), lax.axis_index("y"), lax.axis_index("c")
        me = 4 * x + 2 * y + c
        mine = pltpu.make_async_copy(s_ref.at[me], r_ref.at[me], local_sem)
        mine.start()
        copies = []
        for k in range(1, N_DEV):
            dx, dy, dc = (k >> 2) & 1, (k >> 1) & 1, k & 1
            px = 1 - x if dx else x
            py = 1 - y if dy else y
            pc = 1 - c if dc else c
            peer = 4 * px + 2 * py + pc
            copies.append((pltpu.make_async_remote_copy(
                src_ref=s_ref.at[peer], dst_ref=r_ref.at[me], send_sem=send_sems.at[k - 1],
                recv_sem=recv_sems.at[k - 1], device_id=(px, py, pc), device_id_type=MESH), peer))
        for cp, _ in copies:
            cp.start()
        for k, (cp, peer) in enumerate(copies):
            pltpu.make_async_remote_copy(
                src_ref=s_ref.at[peer], dst_ref=r_ref.at[peer], send_sem=send_sems.at[k], recv_sem=recv_sems.at[k],
                device_id=(x, y, c), device_id_type=MESH).wait_recv()
        for cp, _ in copies:
            cp.wait_send()
        mine.wait()

    return pl.pallas_call(
        body, name=name, out_shape=jax.ShapeDtypeStruct(send.shape, send.dtype),
        in_specs=[pl.BlockSpec(memory_space=pl.ANY)], out_specs=pl.BlockSpec(memory_space=pl.ANY),
        scratch_shapes=[pltpu.SemaphoreType.DMA((7,)), pltpu.SemaphoreType.DMA((7,)), pltpu.SemaphoreType.DMA(())],
    )(send)


def _direct_copies(kind, s_ref, r_ref, send_sems, recv_sems, local_sem):
    x, y, c = lax.axis_index("x"), lax.axis_index("y"), lax.axis_index("c")
    me = 4 * x + 2 * y + c

    def src(p):
        return s_ref if kind == "gather" else s_ref.at[p]

    local = pltpu.make_async_copy(src(me), r_ref.at[me], local_sem)
    sends, recvs = [], []
    for k in range(1, N_DEV):
        px = 1 - x if (k >> 2) & 1 else x
        py = 1 - y if (k >> 1) & 1 else y
        pc = 1 - c if k & 1 else c
        peer = 4 * px + 2 * py + pc
        sends.append(pltpu.make_async_remote_copy(
            src_ref=src(peer), dst_ref=r_ref.at[me], send_sem=send_sems.at[k - 1], recv_sem=recv_sems.at[k - 1],
            device_id=(px, py, pc), device_id_type=MESH))
        recvs.append(pltpu.make_async_remote_copy(
            src_ref=src(peer), dst_ref=r_ref.at[peer], send_sem=send_sems.at[k - 1], recv_sem=recv_sems.at[k - 1],
            device_id=(x, y, c), device_id_type=MESH))
    return local, sends, recvs


def _call(body, *, grid, in_specs, out_specs, out_shape, name, args, scratch_shapes=(), semantics=None, comms=()):
    single = not isinstance(out_shape, (list, tuple))
    out_shape = [out_shape] if single else list(out_shape)
    out_specs = [out_specs] if single else list(out_specs)
    if not comms:
        res = pl.pallas_call(body, grid=grid, in_specs=in_specs, out_specs=out_specs, out_shape=out_shape, name=name,
                             scratch_shapes=list(scratch_shapes),
                             compiler_params=_params(*(semantics or ("arbitrary",) * len(grid))))(*args)
        return res[0] if single else res
    n_in, n_out, n_scr, n_c = len(args), len(out_shape), len(scratch_shapes), len(comms)

    def hosted(*refs):
        ins, srcs = refs[:n_in], refs[n_in:n_in + n_c]
        outs = refs[n_in + n_c:n_in + n_c + n_out]
        dsts = refs[n_in + n_c + n_out:n_in + 2 * n_c + n_out]
        scr = refs[n_in + 2 * n_c + n_out:n_in + 2 * n_c + n_out + n_scr]
        sems = refs[n_in + 2 * n_c + n_out + n_scr:]
        first = functools.reduce(jnp.logical_and, [pl.program_id(d) == 0 for d in range(len(grid))])
        last = functools.reduce(jnp.logical_and, [pl.program_id(d) == grid[d] - 1 for d in range(len(grid))])
        plans = [_direct_copies(comms[i][0], srcs[i], dsts[i], *sems[3 * i:3 * i + 3]) for i in range(n_c)]

        @pl.when(first)
        def _():
            for local, sends, _ in plans:
                local.start()
                for cp in sends:
                    cp.start()

        body(*ins, *outs, *scr)

        @pl.when(last)
        def _():
            for local, sends, recvs in plans:
                for cp in recvs:
                    cp.wait_recv()
                for cp in sends:
                    cp.wait_send()
                local.wait()

    any_spec = pl.BlockSpec(memory_space=pl.ANY)
    comm_shapes = [jax.ShapeDtypeStruct(((N_DEV,) + a.shape) if kind == "gather" else a.shape, a.dtype)
                   for kind, a in comms]
    sem_shapes = []
    for _ in comms:
        sem_shapes += [pltpu.SemaphoreType.DMA((7,)), pltpu.SemaphoreType.DMA((7,)), pltpu.SemaphoreType.DMA(())]
    res = pl.pallas_call(
        hosted, grid=grid, in_specs=list(in_specs) + [any_spec] * n_c, out_specs=out_specs + [any_spec] * n_c,
        out_shape=out_shape + comm_shapes, name=name, scratch_shapes=list(scratch_shapes) + sem_shapes,
        compiler_params=_params(*(("arbitrary",) * len(grid))))(*args, *[a for _, a in comms])
    return res


def _unshard(gathered, axis):
    local = gathered.shape[1:]
    full = jnp.moveaxis(gathered, 0, axis)
    return full.reshape(local[:axis] + (N_DEV * local[axis],) + local[axis + 1:])


def _shard_blocks(full, axis):
    s = full.shape
    cut = full.reshape(s[:axis] + (N_DEV, s[axis] // N_DEV) + s[axis + 1:])
    return jnp.moveaxis(cut, axis, 0)


def _mm_fwd(name, a, w, out_dtype, gain=None, resid=None, glu=False):
    n_rows, k = a.shape
    n = w.shape[1]
    tm = _tile(n_rows, 512)
    n_out = n // 2 if glu else n

    def body(*refs):
        it = iter(refs)
        a_ref, w_ref = next(it), next(it)
        g_ref = next(it) if gain is not None else None
        r_ref = next(it) if resid is not None else None
        outs = list(it)
        av = a_ref[...]
        if g_ref is not None:
            _, xhat = _rms(av)
            av = xhat * g_ref[...]
        res = _dot(av.astype(BF16), w_ref[...])
        if glu:
            outs[1][...] = res.astype(outs[1].dtype)
            res = res[:, :n_out] * _sigmoid(res[:, n_out:])
        if r_ref is not None:
            res = res + r_ref[...]
        outs[0][...] = res.astype(outs[0].dtype)

    args = [a, w]
    in_specs = [pl.BlockSpec((tm, k), lambda i: (i, 0)), pl.BlockSpec((k, n), lambda i: (0, 0))]
    if gain is not None:
        args.append(gain)
        in_specs.append(pl.BlockSpec((1, k), lambda i: (0, 0)))
    if resid is not None:
        args.append(resid)
        in_specs.append(pl.BlockSpec((tm, n_out), lambda i: (i, 0)))
    out_shape = [jax.ShapeDtypeStruct((n_rows, n_out), out_dtype)]
    out_specs = [pl.BlockSpec((tm, n_out), lambda i: (i, 0))]
    if glu:
        out_shape.append(jax.ShapeDtypeStruct((n_rows, n), F32))
        out_specs.append(pl.BlockSpec((tm, n), lambda i: (i, 0)))
    return pl.pallas_call(body, grid=(n_rows // tm,), in_specs=in_specs, out_specs=out_specs, out_shape=out_shape,
                          name=name, compiler_params=_params("parallel"))(*args)


def _mm_bwd(name, a, d, w, gain=None, dres=None, d_add=None):
    n_rows, k = a.shape
    n = w.shape[1]
    tm = _tile(n_rows, 512)

    def body(*refs):
        it = iter(refs)
        a_ref, d_ref, w_ref = next(it), next(it), next(it)
        g_ref = next(it) if gain is not None else None
        r_ref = next(it) if gain is not None else None
        e_ref, s_ref = (next(it), next(it)) if d_add is not None else (None, None)
        da_ref, dw_ref = next(it), next(it)
        dg_ref = next(it) if gain is not None else None
        i = pl.program_id(0)

        @pl.when(i == 0)
        def _():
            dw_ref[...] = jnp.zeros(dw_ref.shape, F32)
            if dg_ref is not None:
                dg_ref[...] = jnp.zeros(dg_ref.shape, F32)

        av = a_ref[...]
        dv = d_ref[...]
        if e_ref is not None:
            dv = dv + e_ref[...] * s_ref[...]
        dv = dv.astype(BF16)
        if g_ref is not None:
            r, xhat = _rms(av)
            ab = (xhat * g_ref[...]).astype(BF16)
        else:
            ab = av.astype(BF16)
        dw_ref[...] += _dot(ab, dv, TN)
        da = _dot(dv, w_ref[...], NT)
        if g_ref is not None:
            dg_ref[...] += jnp.sum(da * xhat, axis=0, keepdims=True)
            da = r_ref[...] + _rms_bwd(da, xhat, r, g_ref[...])
        da_ref[...] = da.astype(da_ref.dtype)

    args = [a, d, w]
    in_specs = [pl.BlockSpec((tm, k), lambda i: (i, 0)), pl.BlockSpec((tm, n), lambda i: (i, 0)),
                pl.BlockSpec((k, n), lambda i: (0, 0))]
    out_shape = [jax.ShapeDtypeStruct((n_rows, k), F32), jax.ShapeDtypeStruct((k, n), F32)]
    out_specs = [pl.BlockSpec((tm, k), lambda i: (i, 0)), pl.BlockSpec((k, n), lambda i: (0, 0))]
    if gain is not None:
        args += [gain, dres]
        in_specs += [pl.BlockSpec((1, k), lambda i: (0, 0)), pl.BlockSpec((tm, k), lambda i: (i, 0))]
        out_shape.append(jax.ShapeDtypeStruct((1, k), F32))
        out_specs.append(pl.BlockSpec((1, k), lambda i: (0, 0)))
    if d_add is not None:
        args += list(d_add)
        in_specs += [pl.BlockSpec((tm, n), lambda i: (i, 0)), pl.BlockSpec((1, n), lambda i: (0, 0))]
    return pl.pallas_call(body, grid=(n_rows // tm,), in_specs=in_specs, out_specs=out_specs, out_shape=out_shape,
                          name=name, compiler_params=_params("arbitrary"))(*args)


def _ffn_fwd(name, x, gain, wi, wo, comms=()):
    n_rows, dm = x.shape
    _, nj, _, fb = wi.shape
    tm = _tile(n_rows, 512)

    def body(x_ref, g_ref, wi_ref, wo_ref, y_ref, gate_ref, up_ref):
        xv = x_ref[...]
        _, xhat = _rms(xv)
        hn = (xhat * g_ref[...]).astype(BF16)
        acc = jnp.zeros((tm, dm), F32)
        for j in range(nj):
            gate = _dot(hn, wi_ref[0, j])
            up = _dot(hn, wi_ref[1, j])
            gate_ref[j] = gate.astype(BF16)
            up_ref[j] = up.astype(BF16)
            act = (gate * _sigmoid(gate) * up).astype(BF16)
            acc = acc + _dot(act, wo_ref[j].reshape(fb, dm))
        y_ref[...] = xv + 0.5 * acc

    return _call(
        body, grid=(n_rows // tm,), name=name, args=[x, gain, wi, wo], comms=comms,
        in_specs=[pl.BlockSpec((tm, dm), lambda i: (i, 0)), pl.BlockSpec((1, dm), lambda i: (0, 0)),
                  pl.BlockSpec((2, nj, dm, fb), lambda i: (0, 0, 0, 0)),
                  pl.BlockSpec((nj, 2, fb // 2, dm), lambda i: (0, 0, 0, 0))],
        out_specs=[pl.BlockSpec((tm, dm), lambda i: (i, 0)), pl.BlockSpec((nj, tm, fb), lambda i: (0, i, 0)),
                   pl.BlockSpec((nj, tm, fb), lambda i: (0, i, 0))],
        out_shape=[jax.ShapeDtypeStruct((n_rows, dm), F32), jax.ShapeDtypeStruct((nj, n_rows, fb), BF16),
                   jax.ShapeDtypeStruct((nj, n_rows, fb), BF16)])


def _ffn_bwd_block(name, x, dy, gain, wi, wo, gate_s, up_s, j, acc, comms=()):
    n_rows, dm = x.shape
    _, nj, _, fb = wi.shape
    tm = _tile(n_rows, 512)
    last = j == nj - 1

    def body(*refs):
        it = iter(refs)
        x_ref, dy_ref, g_ref, wi_ref, wo_ref = next(it), next(it), next(it), next(it), next(it)
        gate_ref, up_ref = next(it), next(it)
        acc_ref = next(it) if acc is not None else None
        out_ref, dwi16_ref, dwo16_ref = next(it), next(it), next(it)
        dg_ref = next(it) if last else None
        dwi_ref, dwo_ref = next(it), next(it)
        i = pl.program_id(0)

        @pl.when(i == 0)
        def _():
            dwi_ref[...] = jnp.zeros(dwi_ref.shape, F32)
            dwo_ref[...] = jnp.zeros(dwo_ref.shape, F32)
            if last:
                dg_ref[...] = jnp.zeros(dg_ref.shape, F32)

        g = g_ref[...]
        wg, wu, wob = wi_ref[0], wi_ref[1], wo_ref[...].reshape(fb, dm)
        parts = range(FFN_ROW_PARTS)
        rp = tm // FFN_ROW_PARTS
        rows = [slice(k * rp, (k + 1) * rp) for k in parts]
        xv = [x_ref[rows[k], :] for k in parts]
        dyv = [dy_ref[rows[k], :] for k in parts]
        rx = [_rms(xv[k]) for k in parts]
        hn = [(rx[k][1] * g).astype(BF16) for k in parts]
        dout = [(0.5 * dyv[k]).astype(BF16) for k in parts]
        dact = [_dot(dout[k], wob, NT) for k in parts]
        gate = [gate_ref[rows[k], :].astype(F32) for k in parts]
        up = [up_ref[rows[k], :].astype(F32) for k in parts]
        s = [_sigmoid(gate[k]) for k in parts]
        silu = [gate[k] * s[k] for k in parts]
        act = [(silu[k] * up[k]).astype(BF16) for k in parts]
        dgate = [(dact[k] * up[k] * (s[k] * (1.0 + gate[k] * (1.0 - s[k])))).astype(BF16) for k in parts]
        dup = [(dact[k] * silu[k]).astype(BF16) for k in parts]
        for k in parts:
            dwo_ref[...] += _dot(act[k], dout[k], TN)
            dwi_ref[0] += _dot(dgate[k], hn[k], TN)
            dwi_ref[1] += _dot(dup[k], hn[k], TN)
        tot = [_dot(dgate[k], wg, NT) + _dot(dup[k], wu, NT) for k in parts]
        for k in parts:
            t = tot[k] + acc_ref[rows[k], :] if acc_ref is not None else tot[k]
            if last:
                out_ref[rows[k], :] = dyv[k] + _rms_bwd(t, rx[k][1], rx[k][0], g)
                dg_ref[...] += jnp.sum(t * rx[k][1], axis=0, keepdims=True)
            else:
                out_ref[rows[k], :] = t

        @pl.when(i == n_rows // tm - 1)
        def _():
            dwi16_ref[...] = dwi_ref[...].astype(BF16)
            dwo16_ref[...] = dwo_ref[...].astype(BF16)

    tok = pl.BlockSpec((tm, dm), lambda i: (i, 0))
    args = [x, dy, gain, wi, wo, gate_s, up_s]
    saved = pl.BlockSpec((None, tm, fb), lambda i: (j, i, 0))
    in_specs = [tok, tok, pl.BlockSpec((1, dm), lambda i: (0, 0)),
                pl.BlockSpec((2, None, dm, fb), lambda i: (0, j, 0, 0)),
                pl.BlockSpec((None, 2, fb // 2, dm), lambda i: (j, 0, 0, 0)), saved, saved]
    if acc is not None:
        args.append(acc)
        in_specs.append(tok)
    out_specs = [tok, pl.BlockSpec((2, fb, dm), lambda i: (0, 0, 0)), pl.BlockSpec((fb, dm), lambda i: (0, 0))]
    out_shape = [jax.ShapeDtypeStruct((n_rows, dm), F32), jax.ShapeDtypeStruct((2, fb, dm), BF16),
                 jax.ShapeDtypeStruct((fb, dm), BF16)]
    if last:
        out_specs.append(pl.BlockSpec((1, dm), lambda i: (0, 0)))
        out_shape.append(jax.ShapeDtypeStruct((1, dm), F32))
    return _call(body, grid=(n_rows // tm,), name=name, in_specs=in_specs, out_specs=out_specs,
                 out_shape=out_shape, args=args, comms=comms,
                 scratch_shapes=[pltpu.VMEM((2, fb, dm), F32), pltpu.VMEM((fb, dm), F32)])


def _ffn_bwd(name, x, dy, gain, wi, wo, gate_s, up_s, comms_by_block=None):
    nj = wi.shape[1]
    acc, dwi, dwo, extra = None, [], [], []
    for j in range(nj):
        comms = (comms_by_block or {}).get(j, ())
        res = _ffn_bwd_block("%s_%d" % (name, j), x, dy, gain, wi, wo, gate_s, up_s, j, acc, comms)
        n_own = 4 if j == nj - 1 else 3
        acc = res[0]
        dwi.append(res[1])
        dwo.append(res[2])
        extra += list(res[n_own:])
        dgain = res[3] if j == nj - 1 else None
    return acc, jnp.stack(dwi, axis=1), jnp.stack(dwo, axis=0), dgain, extra


def _scan8(a, x, reverse):
    row = lax.broadcasted_iota(jnp.int32, a.shape, 0)
    for k in (1, 2, 4):
        if reverse:
            keep = row < SUBLANES - k
            a_s, x_s = pltpu.roll(a, SUBLANES - k, 0), pltpu.roll(x, SUBLANES - k, 0)
        else:
            keep = row >= k
            a_s, x_s = pltpu.roll(a, k, 0), pltpu.roll(x, k, 0)
        x = a * jnp.where(keep, x_s, 0.0) + x
        a = a * jnp.where(keep, a_s, 1.0)
    return a, x


def _scan_tile(a_ref, x_ref, h_ref, carry, reverse, rows):
    groups = rows // SUBLANES

    def step(n, c):
        gidx = groups - 1 - n if reverse else n
        sl = pl.ds(pl.multiple_of(gidx * SUBLANES, SUBLANES), SUBLANES)
        a_cum, h0 = _scan8(a_ref[sl, :], x_ref[sl, :], reverse)
        h = a_cum * c + h0
        h_ref[sl, :] = h
        return h[0:1] if reverse else h[SUBLANES - 1:SUBLANES]

    return lax.fori_loop(0, groups, step, carry)


def _s5_mats(lam_re, lam_im, log_dt, b_re, b_im, c_re, c_im):
    lc = S5_CHUNK
    groups, p = lam_re.shape
    h = b_re.shape[-1]
    lam = lax.complex(lam_re, lam_im)
    lam_dt = lam * jnp.exp(log_dt)[:, None]
    lam_bar = jnp.exp(lam_dt)
    b_bar = ((lam_bar - 1.0) / lam)[:, :, None] * lax.complex(b_re, b_im)
    c = lax.complex(c_re, c_im)
    pw = jnp.exp(lam_dt[None] * jnp.arange(lc + 1, dtype=F32)[:, None, None])
    resp = jnp.einsum('ghp,tgp,gpk->tghk', c, pw[:lc], b_bar, precision=HI).real
    s_idx = jnp.arange(lc)[:, None]
    u_idx = jnp.arange(lc)[None, :]
    onehot = (jnp.arange(lc)[:, None, None] == (u_idx - s_idx)[None]).astype(F32)
    m = jnp.einsum('tghk,tsu->gskuh', resp, onehot, precision=HI).reshape(groups, lc * h, lc * h)
    w = pw[lc - 1::-1][:lc].transpose(1, 0, 2)[:, :, None, :] * b_bar.transpose(0, 2, 1)[:, None]
    bm = jnp.concatenate([w.real, w.imag], axis=-1).reshape(groups, lc * h, 2 * p)
    v = c[:, None] * pw[1:lc + 1].transpose(1, 0, 2)[:, :, None, :]
    v = v.transpose(0, 3, 1, 2)
    cm = jnp.concatenate([v.real, -v.imag], axis=1).reshape(groups, 2 * p, lc * h)
    a = jnp.concatenate([pw[lc].real, pw[lc].imag], axis=-1)
    return m, bm, cm, a


def _s5_powers(lam_re, lam_im, log_dt):
    lam_dt = lax.complex(lam_re, lam_im) * jnp.exp(log_dt)[:, None]
    pw = jnp.exp(lam_dt[None] * (S5_CHUNK * jnp.arange(1, 9, dtype=F32))[:, None, None])

    def c1(z):
        return jnp.concatenate([z.real, z.real], axis=-1).reshape(z.shape[0], -1)

    def c2(z):
        return jnp.concatenate([-z.imag, z.imag], axis=-1).reshape(z.shape[0], -1)

    p1, p2 = c1(pw), c2(pw)
    apw = jnp.stack([p1[0], p2[0], p1[1], p2[1], p1[3], p2[3], jnp.zeros_like(p1[0]), jnp.zeros_like(p1[0])])
    fwd = jnp.concatenate([p1, p2], axis=0)
    rev = jnp.concatenate([c1(pw[::-1]), c2(pw[::-1])], axis=0)
    return apw, fwd, rev


def _cmul(c1, c2, x, half, conj=False):
    sw = pltpu.roll(x, half, 1)
    return c1 * x - c2 * sw if conj else c1 * x + c2 * sw


def _gather_groups(u_ref, ug_ref, nc):
    h = S5_GROUP
    rows = min(S5_REGROUP_ROWS, nc)

    def step(r, _):
        base = pl.multiple_of(r * rows, rows)
        for t in range(S5_CHUNK):
            val = u_ref[pl.ds(base * S5_CHUNK + t, rows, stride=S5_CHUNK), :]
            for g in range(S5_OCTET):
                ug_ref[g, pl.ds(base, rows), t * h:(t + 1) * h] = val[:, g * h:(g + 1) * h]
        return 0

    lax.fori_loop(0, nc // rows, step, 0)


def _scatter_groups(yg_ref, y_ref, nc):
    h = S5_GROUP
    rows = min(S5_REGROUP_ROWS, nc)

    def step(r, _):
        base = pl.multiple_of(r * rows, rows)
        for t in range(S5_CHUNK):
            y_ref[pl.ds(base * S5_CHUNK + t, rows, stride=S5_CHUNK), :] = jnp.concatenate(
                [yg_ref[g, pl.ds(base, rows), t * h:(t + 1) * h] for g in range(S5_OCTET)], axis=1)
        return 0

    lax.fori_loop(0, nc // rows, step, 0)


def _s5_fwd(name, u, m, bm, cm, apw, arows, comms=()):
    n_rows, width = u.shape
    nc = n_rows // S5_CHUNK
    groups, lh, _ = m.shape
    p2 = bm.shape[2]
    gb = S5_OCTET
    lanes = gb * S5_GROUP

    def body(u_ref, m_ref, b_ref, c_ref, apw_ref, ar_ref, y_ref, sp_ref, ug_ref, yg_ref, xs_ref):
        _gather_groups(u_ref, ug_ref, nc)
        for gi in range(gb):
            xs_ref[:, gi * p2:(gi + 1) * p2] = _dot(ug_ref[gi], b_ref[gi], prec=S5_PREC)
        row = lax.broadcasted_iota(jnp.int32, (SUBLANES, p2), 0)

        def group(n, carry):
            sl = pl.ds(pl.multiple_of(n * SUBLANES, SUBLANES), SUBLANES)
            new = []
            for gi in range(gb):
                ln = slice(gi * p2, (gi + 1) * p2)
                x = xs_ref[sl, ln]
                for q, k in enumerate((1, 2, 4)):
                    xs = jnp.where(row >= k, pltpu.roll(x, k, 0), 0.0)
                    x = x + _cmul(apw_ref[2 * q:2 * q + 1, ln], apw_ref[2 * q + 1:2 * q + 2, ln], xs, p2 // 2)
                cb = jnp.broadcast_to(carry[gi], (SUBLANES, p2))
                s8 = x + _cmul(ar_ref[0:8, ln], ar_ref[8:16, ln], cb, p2 // 2)
                sp_ref[sl, ln] = jnp.where(row >= 1, pltpu.roll(s8, 1, 0), cb)
                new.append(s8[SUBLANES - 1:SUBLANES])
            return tuple(new)

        unroll = min(S5_SCAN_UNROLL, nc // SUBLANES)

        def step(n, carry):
            for k in range(unroll):
                carry = group(n * unroll + k, carry)
            return carry

        lax.fori_loop(0, nc // (SUBLANES * unroll), step, tuple(jnp.zeros((1, p2), F32) for _ in range(gb)))
        for gi in range(gb):
            yg_ref[gi] = (_dot(ug_ref[gi], m_ref[gi], prec=S5_PREC)
                          + _dot(sp_ref[:, gi * p2:(gi + 1) * p2], c_ref[gi], prec=S5_PREC))
        _scatter_groups(yg_ref, y_ref, nc)

    tok = pl.BlockSpec((n_rows, lanes), lambda g: (0, g), pipeline_mode=pl.Buffered(1))
    return _call(
        body, grid=(groups // gb,), name=name, args=[u, m, bm, cm, apw, arows], comms=comms, semantics=("parallel",),
        in_specs=[tok, pl.BlockSpec((gb, lh, lh), lambda g: (g, 0, 0)),
                  pl.BlockSpec((gb, lh, p2), lambda g: (g, 0, 0)), pl.BlockSpec((gb, p2, lh), lambda g: (g, 0, 0)),
                  pl.BlockSpec((8, gb * p2), lambda g: (0, g)), pl.BlockSpec((16, gb * p2), lambda g: (0, g))],
        out_specs=[tok, pl.BlockSpec((nc, gb * p2), lambda g: (0, g))],
        out_shape=[jax.ShapeDtypeStruct((n_rows, width), F32), jax.ShapeDtypeStruct((nc, groups * p2), F32)],
        scratch_shapes=[pltpu.VMEM((gb, nc, lh), F32), pltpu.VMEM((gb, nc, lh), F32), pltpu.VMEM((nc, gb * p2), F32)])


def _s5_bwd(name, u, dy, sprev, m, bm, cm, apw, arows_rev):
    n_rows, width = u.shape
    nc = n_rows // S5_CHUNK
    groups, lh, _ = m.shape
    p2 = bm.shape[2]
    half = p2 // 2
    gb = S5_OCTET
    lanes = gb * S5_GROUP

    def body(u_ref, dy_ref, sp_ref, m_ref, b_ref, c_ref, apw_ref, ar_ref,
             du_ref, dm_ref, db_ref, dc_ref, da_ref, ug_ref, dyg_ref, ds_ref, gx_ref):
        _gather_groups(u_ref, ug_ref, nc)
        _gather_groups(dy_ref, dyg_ref, nc)
        for gi in range(gb):
            ds_ref[:, gi * p2:(gi + 1) * p2] = _dot(dyg_ref[gi], c_ref[gi], NT, prec=S5_PREC)
        row = lax.broadcasted_iota(jnp.int32, (SUBLANES, p2), 0)
        lane = lax.broadcasted_iota(jnp.int32, (SUBLANES, p2), 1)
        ngroups = nc // SUBLANES

        def group(n, state):
            carry, nxt, dacc = state
            sl = pl.ds(pl.multiple_of((ngroups - 1 - n) * SUBLANES, SUBLANES), SUBLANES)
            new_c, new_n, new_d = [], [], []
            for gi in range(gb):
                ln = slice(gi * p2, (gi + 1) * p2)
                d8 = ds_ref[sl, ln]
                x = jnp.where(row < SUBLANES - 1, pltpu.roll(d8, SUBLANES - 1, 0),
                              jnp.broadcast_to(nxt[gi], (SUBLANES, p2)))
                for q, k in enumerate((1, 2, 4)):
                    xs = jnp.where(row < SUBLANES - k, pltpu.roll(x, SUBLANES - k, 0), 0.0)
                    x = x + _cmul(apw_ref[2 * q:2 * q + 1, ln], apw_ref[2 * q + 1:2 * q + 2, ln], xs, half, conj=True)
                cb = jnp.broadcast_to(carry[gi], (SUBLANES, p2))
                g8 = x + _cmul(ar_ref[0:8, ln], ar_ref[8:16, ln], cb, half, conj=True)
                gx_ref[sl, ln] = g8
                s8 = sp_ref[sl, ln]
                p1 = g8 * s8
                pq = g8 * pltpu.roll(s8, half, 1)
                d_a = jnp.where(lane < half, p1 + pltpu.roll(p1, half, 1), pq - pltpu.roll(pq, half, 1))
                new_c.append(g8[0:1])
                new_n.append(d8[0:1])
                new_d.append(dacc[gi] + jnp.sum(d_a, axis=0, keepdims=True))
            return tuple(new_c), tuple(new_n), tuple(new_d)

        unroll = min(S5_SCAN_UNROLL, ngroups)

        def step(n, state):
            for k in range(unroll):
                state = group(n * unroll + k, state)
            return state

        zeros = tuple(jnp.zeros((1, p2), F32) for _ in range(gb))
        _, _, dacc = lax.fori_loop(0, ngroups // unroll, step, (zeros, zeros, zeros))
        for gi in range(gb):
            ln = slice(gi * p2, (gi + 1) * p2)
            da_ref[:, ln] = dacc[gi]
            ug, dyg, gxg = ug_ref[gi], dyg_ref[gi], gx_ref[:, ln]
            dm_ref[gi] = _dot(ug, dyg, TN, prec=S5_PREC)
            dc_ref[gi] = _dot(sp_ref[:, ln], dyg, TN, prec=S5_PREC)
            db_ref[gi] = _dot(ug, gxg, TN, prec=S5_PREC)
            dyg_ref[gi] = _dot(dyg, m_ref[gi], NT, prec=S5_PREC) + _dot(gxg, b_ref[gi], NT, prec=S5_PREC)
        _scatter_groups(dyg_ref, du_ref, nc)

    tok = pl.BlockSpec((n_rows, lanes), lambda g: (0, g), pipeline_mode=pl.Buffered(1))
    tok_s = pl.BlockSpec((nc, gb * p2), lambda g: (0, g))
    mat_m = pl.BlockSpec((gb, lh, lh), lambda g: (g, 0, 0))
    mat_b = pl.BlockSpec((gb, lh, p2), lambda g: (g, 0, 0))
    mat_c = pl.BlockSpec((gb, p2, lh), lambda g: (g, 0, 0))
    return pl.pallas_call(
        body, grid=(groups // gb,), name=name,
        in_specs=[tok, tok, tok_s, mat_m, mat_b, mat_c,
                  pl.BlockSpec((8, gb * p2), lambda g: (0, g)), pl.BlockSpec((16, gb * p2), lambda g: (0, g))],
        out_specs=[tok, mat_m, mat_b, mat_c, pl.BlockSpec((1, gb * p2), lambda g: (0, g))],
        out_shape=[jax.ShapeDtypeStruct((n_rows, width), F32), jax.ShapeDtypeStruct(m.shape, F32),
                   jax.ShapeDtypeStruct(bm.shape, F32), jax.ShapeDtypeStruct(cm.shape, F32),
                   jax.ShapeDtypeStruct((1, groups * p2), F32)],
        scratch_shapes=[pltpu.VMEM((gb, nc, lh), F32), pltpu.VMEM((gb, nc, lh), F32),
                        pltpu.VMEM((nc, gb * p2), F32), pltpu.VMEM((nc, gb * p2), F32)],
        compiler_params=_params("parallel"),
    )(u, dy, sprev, m, bm, cm, apw, arows_rev)


def _split(x):
    hi = x.astype(BF16)
    return hi, (x - hi.astype(F32)).astype(BF16)


def _sb_more(kb, carries):
    top = jnp.max(carries[0])
    for c in carries[1:]:
        top = jnp.maximum(top, jnp.max(c))
    return (kb >= 0) & (top > SB_UNDERFLOW)


def _sb_fwd(name, qkv, heads):
    n_rows, dm3 = qkv.shape
    dm = dm3 // 3
    hd = dm // heads
    tq = _tile(n_rows // 2, 256)
    hb = min(SB_HEADS_FWD, heads)
    groups = heads // hb
    scale = hd ** -0.5

    def body(q_ref, k_ref, v_ref, o_ref):
        qi = pl.program_id(1)
        hs = range(hb)
        row = lax.broadcasted_iota(jnp.int32, (tq, tq), 0)
        col = lax.broadcasted_iota(jnp.int32, (tq, tq), 1)
        tri = (row > col).astype(BF16)
        causal = col < row
        qall = q_ref[...] * scale
        qb = [qall[:, h * hd:(h + 1) * hd] for h in hs]

        def block(kb, carries, accs, diagonal):
            ks = pl.ds(pl.multiple_of(kb * tq, tq), tq)
            kblk, vblk = k_ref[ks, :], v_ref[ks, :]
            z = [_dot(qb[h], kblk[:, h * hd:(h + 1) * hd], NT) for h in hs]
            sp = [_softplus(z[h]) for h in hs]
            lk = [-sp[h] for h in hs]
            if diagonal:
                lk = [jnp.where(causal, lk[h], 0.0) for h in hs]
            parts = [_split(lk[h]) for h in hs]
            r = [_dot(parts[h][0], tri) + _dot(parts[h][1], tri) for h in hs]
            a = [jnp.exp(z[h] - sp[h] + r[h] + carries[h]) for h in hs]
            if diagonal:
                a = [jnp.where(causal, a[h], 0.0) for h in hs]
            new_a = tuple(accs[h] + _dot(a[h].astype(BF16), vblk[:, h * hd:(h + 1) * hd]) for h in hs)
            new_c = tuple(carries[h] + jnp.sum(lk[h], axis=1, keepdims=True) for h in hs)
            return new_c, new_a

        zc = tuple(jnp.zeros((tq, 1), F32) for _ in hs)
        za = tuple(jnp.zeros((tq, hd), F32) for _ in hs)
        carries, accs = block(qi, zc, za, True)
        _, _, accs = lax.while_loop(lambda st: _sb_more(st[0], st[1]),
                                    lambda st: (st[0] - 1,) + block(st[0], st[1], st[2], False),
                                    (qi - 1, carries, accs))
        o_ref[...] = jnp.concatenate(accs, axis=1)

    lanes = hb * hd
    return pl.pallas_call(
        body, grid=(groups, n_rows // tq), name=name,
        in_specs=[pl.BlockSpec((tq, lanes), lambda g, i: (i, g)),
                  pl.BlockSpec((n_rows, lanes), lambda g, i: (0, groups + g), pipeline_mode=pl.Buffered(1)),
                  pl.BlockSpec((n_rows, lanes), lambda g, i: (0, 2 * groups + g), pipeline_mode=pl.Buffered(1))],
        out_specs=pl.BlockSpec((tq, lanes), lambda g, i: (i, g)),
        out_shape=jax.ShapeDtypeStruct((n_rows, dm), F32),
        compiler_params=_params("parallel", "arbitrary"))(qkv, qkv, qkv)


def _sb_bwd(name, qkv, o, do, heads):
    n_rows, dm3 = qkv.shape
    dm = dm3 // 3
    hd = dm // heads
    tq = _tile(n_rows // 2, 256)
    hb = min(SB_HEADS_BWD, heads)
    groups = heads // hb
    scale = hd ** -0.5

    def body(q_ref, k_ref, v_ref, o_ref, do_ref, dq_ref, dk_ref, dv_ref):
        qi = pl.program_id(1)

        @pl.when(qi == 0)
        def _():
            dk_ref[...] = jnp.zeros(dk_ref.shape, F32)
            dv_ref[...] = jnp.zeros(dv_ref.shape, F32)

        hs = range(hb)
        cols = [slice(h * hd, (h + 1) * hd) for h in hs]
        qall = q_ref[...] * scale
        doall = do_ref[...].astype(BF16)
        prod = doall.astype(F32) * o_ref[...]
        qb = [qall[:, cols[h]] for h in hs]
        dob16 = [doall[:, cols[h]] for h in hs]
        delta = [jnp.sum(prod[:, cols[h]], axis=1, keepdims=True) for h in hs]
        row = lax.broadcasted_iota(jnp.int32, (tq, tq), 0)
        col = lax.broadcasted_iota(jnp.int32, (tq, tq), 1)
        tri = (row > col).astype(BF16)
        tri_incl = (row >= col).astype(BF16)
        causal = col < row

        def block(kb, carries, pcarries, dqs, diagonal):
            ks = pl.ds(pl.multiple_of(kb * tq, tq), tq)
            kall, vall = k_ref[ks, :], v_ref[ks, :]
            kblk = [kall[:, cols[h]] for h in hs]
            vblk = [vall[:, cols[h]] for h in hs]
            z = [_dot(qb[h], kblk[h], NT) for h in hs]
            da = [_dot(dob16[h], vblk[h], NT) for h in hs]
            sp = [_softplus(z[h]) for h in hs]
            lk = [-sp[h] for h in hs]
            if diagonal:
                lk = [jnp.where(causal, lk[h], 0.0) for h in hs]
            lb = [z[h] - sp[h] for h in hs]
            parts = [_split(lk[h]) for h in hs]
            r = [_dot(parts[h][0], tri) + _dot(parts[h][1], tri) for h in hs]
            a = [jnp.exp(lb[h] + r[h] + carries[h]) for h in hs]
            if diagonal:
                a = [jnp.where(causal, a[h], 0.0) for h in hs]
            a16 = [a[h].astype(BF16) for h in hs]
            p = [da[h] * a16[h].astype(F32) for h in hs]
            pparts = [_split(p[h]) for h in hs]
            pc = [_dot(pparts[h][0], tri_incl) + _dot(pparts[h][1], tri_incl) for h in hs]
            beta = [jnp.exp(lb[h]) for h in hs]
            dz = [p[h] * (1.0 - beta[h]) - beta[h] * (delta[h] - pcarries[h] - pc[h]) for h in hs]
            if diagonal:
                dz = [jnp.where(causal, dz[h], 0.0) for h in hs]
            dz16 = [dz[h].astype(BF16) for h in hs]
            dk_ref[ks, :] += jnp.concatenate([_dot(dz16[h], qb[h], TN) for h in hs], axis=1)
            dv_ref[ks, :] += jnp.concatenate([_dot(a16[h], dob16[h], TN) for h in hs], axis=1)
            return (tuple(carries[h] + jnp.sum(lk[h], axis=1, keepdims=True) for h in hs),
                    tuple(pcarries[h] + jnp.sum(p[h], axis=1, keepdims=True) for h in hs),
                    tuple(dqs[h] + _dot(dz16[h], kblk[h]) for h in hs))

        zc = tuple(jnp.zeros((tq, 1), F32) for _ in hs)
        zq = tuple(jnp.zeros((tq, hd), F32) for _ in hs)
        st = block(qi, zc, zc, zq, True)
        st = lax.while_loop(lambda s: _sb_more(s[0], s[1]),
                            lambda s: (s[0] - 1,) + block(s[0], s[1], s[2], s[3], False), (qi - 1,) + st)
        dq_ref[...] = jnp.concatenate(st[3], axis=1) * scale

    lanes = hb * hd
    tile = pl.BlockSpec((tq, lanes), lambda g, i: (i, g))
    whole = pl.BlockSpec((n_rows, lanes), lambda g, i: (0, g), pipeline_mode=pl.Buffered(1))
    full = jax.ShapeDtypeStruct((n_rows, dm), F32)
    return pl.pallas_call(
        body, grid=(groups, n_rows // tq), name=name,
        in_specs=[tile, pl.BlockSpec((n_rows, lanes), lambda g, i: (0, groups + g), pipeline_mode=pl.Buffered(1)),
                  pl.BlockSpec((n_rows, lanes), lambda g, i: (0, 2 * groups + g), pipeline_mode=pl.Buffered(1)),
                  tile, tile],
        out_specs=[tile, whole, whole], out_shape=[full, full, full],
        compiler_params=_params("parallel", "arbitrary"))(qkv, qkv, qkv, o, do)


def _block_diag(xb, w_ref_val, dims):
    nb = w_ref_val.shape[0]
    bw = xb.shape[1] // nb
    return jnp.concatenate([_dot(xb[:, n * bw:(n + 1) * bw], w_ref_val[n], dims) for n in range(nb)], axis=1)


def _lru_gates_fwd(name, gx, conv_w, conv_b, wa, ba, wx, bx, lam):
    n_rows, w2 = gx.shape
    w = w2 // 2
    tm = _tile(n_rows, 256)

    def fn(i, nt, br, prev, cw, cb, wa_v, ba_v, wx_v, bx_v, lam_v):
        xc = cb + sum(cw[k:k + 1] * _shift_down(br, prev, LRU_CONV - 1 - k, i == 0) for k in range(LRU_CONV))
        xb = xc.astype(BF16)
        r = _sigmoid(_block_diag(xb, wa_v, NN) + ba_v)
        ig = _sigmoid(_block_diag(xb, wx_v, NN) + bx_v)
        log_a = (-LRU_C * r) * _softplus(-lam_v)
        a = jnp.exp(log_a)
        gated = (ig * xc) * _one_minus_a2_sqrt(log_a)
        return (xc, r, ig, a, gated), ()

    return _rowwise(name, fn, [('t', gx, w, 1), ('p', gx, w, 1), ('b', conv_w), ('b', conv_b), ('b', wa), ('b', ba),
                               ('b', wx), ('b', bx), ('b', lam)], [(w, F32)] * 5, [], n_rows, tm)


def _lru_scan_fwd(name, a, gated, gx):
    n_rows, w = a.shape
    tm = _tile(n_rows, 256)

    def body(a_ref, x_ref, bg_ref, hs_ref, y_ref, carry_ref):
        @pl.when(pl.program_id(0) == 0)
        def _():
            carry_ref[...] = jnp.zeros(carry_ref.shape, F32)
        carry_ref[...] = _scan_tile(a_ref, x_ref, hs_ref, carry_ref[...], False, tm)
        y_ref[...] = (_gelu(bg_ref[...]) * hs_ref[...]).astype(BF16)

    tok = pl.BlockSpec((tm, w), lambda i: (i, 0))
    return pl.pallas_call(body, grid=(n_rows // tm,), in_specs=[tok, tok, tok], out_specs=[tok, tok], name=name,
                          out_shape=[jax.ShapeDtypeStruct((n_rows, w), F32), jax.ShapeDtypeStruct((n_rows, w), BF16)],
                          scratch_shapes=[pltpu.VMEM((1, w), F32)], compiler_params=_params("arbitrary"))(a, gated, gx)


def _lru_scan_bwd(name, a, dy, gx):
    n_rows, w = a.shape
    tm = _tile(n_rows, 256)
    nt = n_rows // tm
    per8 = tm // SUBLANES

    def body(a_ref, an_ref, dy_ref, bg_ref, lam_ref, carry_ref, aup_ref, dhs_ref):
        i = pl.program_id(0)

        @pl.when(i == 0)
        def _():
            carry_ref[...] = jnp.zeros(carry_ref.shape, F32)
        aup_ref[...] = _shift_up(a_ref[...], an_ref[...], 1, i == 0)
        dhs_ref[...] = dy_ref[...] * _gelu(bg_ref[...])
        carry_ref[...] = _scan_tile(aup_ref, dhs_ref, lam_ref, carry_ref[...], True, tm)

    tok = pl.BlockSpec((tm, w), lambda i: (nt - 1 - i, 0))
    nxt = pl.BlockSpec((SUBLANES, w), lambda i: (jnp.minimum((nt - i) * per8, n_rows // SUBLANES - 1), 0))
    return pl.pallas_call(body, grid=(nt,), in_specs=[tok, nxt, tok, tok], out_specs=tok, name=name,
                          out_shape=jax.ShapeDtypeStruct((n_rows, w), F32),
                          scratch_shapes=[pltpu.VMEM((1, w), F32), pltpu.VMEM((tm, w), F32), pltpu.VMEM((tm, w), F32)],
                          compiler_params=_params("arbitrary"))(a, a, dy, gx)


def _lru_gates_bwd(name, lam_t, hs, xc, r, ig, a, wa, wx, lam):
    n_rows, w = xc.shape
    nb, bw, _ = wa.shape
    tm = _tile(n_rows, 256)

    def fn(i, nt, lt, hs_v, hs_prev, xc_v, r_v, ig_v, a_v, wa_v, wx_v, lam_v):
        sp = _softplus(-lam_v)
        log_a = (-LRU_C * r_v) * sp
        mult = _one_minus_a2_sqrt(log_a)
        d_a = lt * _shift_down(hs_v, hs_prev, 1, i == 0)
        d_ig = lt * xc_v * mult
        d_mult = lt * ig_v * xc_v
        d_log_a = d_a * a_v - d_mult * (a_v * a_v) / mult
        d_ra = d_log_a * (-LRU_C * sp) * r_v * (1.0 - r_v)
        d_ia = d_ig * ig_v * (1.0 - ig_v)
        d_ra16, d_ia16, xb = d_ra.astype(BF16), d_ia.astype(BF16), xc_v.astype(BF16)
        dxc = lt * ig_v * mult + _block_diag(d_ra16, wa_v, NT) + _block_diag(d_ia16, wx_v, NT)
        dwa = jnp.concatenate([_dot(xb[:, n * bw:(n + 1) * bw], d_ra16[:, n * bw:(n + 1) * bw], TN)
                               for n in range(nb)], axis=0)
        dwx = jnp.concatenate([_dot(xb[:, n * bw:(n + 1) * bw], d_ia16[:, n * bw:(n + 1) * bw], TN)
                               for n in range(nb)], axis=0)
        col = lambda t: jnp.sum(t, axis=0, keepdims=True)
        return (dxc,), (dwa, dwx, col(d_ra), col(d_ia), col(d_log_a * (-LRU_C * r_v)))

    tiled = lambda arr: ('t', arr, w, 0)
    return _rowwise(name, fn, [tiled(lam_t), tiled(hs), ('p', hs, w, 0), tiled(xc), tiled(r), tiled(ig), tiled(a),
                               ('b', wa), ('b', wx), ('b', lam)],
                    [(w, F32)], [(nb * bw, bw), (nb * bw, bw), (1, w), (1, w), (1, w)], n_rows, tm)


def _lru_conv_bwd(name, dxc, gx, dy, hs, conv_w):
    n_rows, w = dxc.shape
    tm = _tile(n_rows, 256)

    def fn(i, nt, dxc_v, dxc_next, bg, br, br_prev, dy_v, hs_v, cw):
        dbr = sum(cw[k:k + 1] * _shift_up(dxc_v, dxc_next, LRU_CONV - 1 - k, i == nt - 1) for k in range(LRU_CONV))
        dbg = dy_v * hs_v * _gelu_grad(bg)
        dcw = [jnp.sum(dxc_v * _shift_down(br, br_prev, LRU_CONV - 1 - k, i == 0), axis=0, keepdims=True)
               for k in range(LRU_CONV)]
        dcw = jnp.concatenate(dcw + [jnp.zeros((SUBLANES - LRU_CONV, w), F32)], axis=0)
        return (jnp.concatenate([dbg, dbr], axis=1),), (dcw, jnp.sum(dxc_v, axis=0, keepdims=True))

    return _rowwise(name, fn, [('t', dxc, w, 0), ('n', dxc, w, 0), ('t', gx, w, 0), ('t', gx, w, 1), ('p', gx, w, 1),
                               ('t', dy, w, 0), ('t', hs, w, 0), ('b', conv_w)],
                    [(2 * w, BF16)], [(SUBLANES, w), (1, w)], n_rows, tm)


def _loss_head(name, h, gain, target):
    n_rows, dm = h.shape
    tm = _tile(n_rows, 512)

    def fn(i, nt, hv, tv, g):
        r, xhat = _rms(hv)
        err = xhat * g - tv
        dy = err * (1.0 / dm)
        return ((_rms_bwd(dy, xhat, r, g),),
                (jnp.sum(err * err, axis=0, keepdims=True), jnp.sum(dy * xhat, axis=0, keepdims=True)))

    return _rowwise(name, fn, [('t', h, dm, 0), ('t', target, dm, 0), ('b', gain)], [(dm, F32)], [(1, dm), (1, dm)],
                    n_rows, tm)


def _adamw(name, gparts, w, m, v):
    n_parts, n_rows, cols = gparts.shape
    tr = n_rows
    for cand in (256, 128, 64, 32, 16, 8):
        if n_rows % cand == 0:
            tr = cand
            break
    c1 = 1.0 - ADAM_B1 ** ADAM_STEP
    c2 = 1.0 - ADAM_B2 ** ADAM_STEP

    def body(gp_ref, w_ref, m_ref, v_ref, g_ref, d_ref, nm_ref, nv_ref):
        g = gp_ref[0].astype(F32)
        for p in range(1, n_parts):
            g = g + gp_ref[p].astype(F32)
        m_new = ADAM_B1 * m_ref[...] + (1.0 - ADAM_B1) * g
        v_new = ADAM_B2 * v_ref[...] + (1.0 - ADAM_B2) * (g * g)
        m_hat = m_new / c1
        v_hat = v_new / c2
        g_ref[...] = g
        d_ref[...] = -ADAM_LR * (m_hat / (jnp.sqrt(v_hat) + ADAM_EPS) + ADAM_WD * w_ref[...])
        nm_ref[...] = m_new
        nv_ref[...] = v_new

    blk = pl.BlockSpec((tr, cols), lambda i: (i, 0))
    shp = jax.ShapeDtypeStruct((n_rows, cols), F32)
    return pl.pallas_call(body, grid=(n_rows // tr,), name=name,
                          in_specs=[pl.BlockSpec((n_parts, tr, cols), lambda i: (0, i, 0)), blk, blk, blk],
                          out_specs=[blk, blk, blk, blk], out_shape=[shp, shp, shp, shp],
                          compiler_params=_params("parallel"))(gparts, w, m, v)


def _adamw_layers(name, recvs, w, m, v):
    n_layers, n_rows, cols = w.shape
    n_parts = recvs[0].shape[0]
    tr = max(t for t in range(16, ADAMW_LAYER_ROWS + 1, 16) if n_rows % t == 0)
    c1 = 1.0 - ADAM_B1 ** ADAM_STEP
    c2 = 1.0 - ADAM_B2 ** ADAM_STEP

    def body(*refs):
        gp_refs = refs[:n_layers]
        w_ref, m_ref, v_ref, g_ref, d_ref, nm_ref, nv_ref = refs[n_layers:]
        layer = pl.program_id(0)
        for k in range(n_layers):
            @pl.when(layer == k)
            def _(k=k):
                g = gp_refs[k][0].astype(F32)
                for p in range(1, n_parts):
                    g = g + gp_refs[k][p].astype(F32)
                m_new = ADAM_B1 * m_ref[...] + (1.0 - ADAM_B1) * g
                v_new = ADAM_B2 * v_ref[...] + (1.0 - ADAM_B2) * (g * g)
                g_ref[...] = g
                d_ref[...] = -ADAM_LR * ((m_new / c1) / (jnp.sqrt(v_new / c2) + ADAM_EPS) + ADAM_WD * w_ref[...])
                nm_ref[...] = m_new
                nv_ref[...] = v_new

    blk = pl.BlockSpec((None, tr, cols), lambda l, i: (l, i, 0))
    shp = jax.ShapeDtypeStruct((n_layers, n_rows, cols), F32)
    gp_specs = [pl.BlockSpec((n_parts, tr, cols), lambda l, i, k=k: (0, jnp.where(l == k, i, 0), 0))
                for k in range(n_layers)]
    return pl.pallas_call(body, grid=(n_layers, n_rows // tr), name=name, in_specs=gp_specs + [blk, blk, blk],
                          out_specs=[blk, blk, blk, blk], out_shape=[shp, shp, shp, shp],
                          compiler_params=_params("arbitrary", "arbitrary"))(*recvs, w, m, v)


def _pack_rows(arrays, cols, lead=0):
    flat = [a.reshape(a.shape[:lead] + (-1,)) for a in arrays]
    cat = jnp.concatenate(flat, axis=lead) if len(flat) > 1 else flat[0]
    n = cat.shape[lead]
    pad = (-n) % (cols * PACK_ROWS)
    if pad:
        cat = jnp.pad(cat, [(0, 0)] * lead + [(0, pad)])
    return cat.reshape(cat.shape[:lead] + (-1, cols))


def _unpack_rows(packed, shapes, lead=0):
    flat = packed.reshape(packed.shape[:lead] + (-1,))
    out, off = [], 0
    for s in shapes:
        n = math.prod(s)
        out.append(lax.slice_in_dim(flat, off, off + n, axis=lead).reshape(flat.shape[:lead] + tuple(s)))
        off += n
    return out


def kernel(x, ffn1_norm, ffn1_w_in, ffn1_w_out, mix_norm, ffn2_norm, ffn2_w_in, ffn2_w_out, final_norm, s5_w_in, s5_lam_re, s5_lam_im, s5_log_dt, s5_b_re, s5_b_im, s5_c_re, s5_c_im, s5_d, s5_w_out, sb_w_qkv, sb_w_out, lru_w_in, lru_conv_w, lru_conv_b, lru_w_a, lru_b_a, lru_w_x, lru_b_x, lru_lambda, lru_w_out, loss_target, m_ffn1_norm, m_ffn1_w_in, m_ffn1_w_out, m_mix_norm, m_ffn2_norm, m_ffn2_w_in, m_ffn2_w_out, m_final_norm, m_s5_w_in, m_s5_lam_re, m_s5_lam_im, m_s5_log_dt, m_s5_b_re, m_s5_b_im, m_s5_c_re, m_s5_c_im, m_s5_d, m_s5_w_out, m_sb_w_qkv, m_sb_w_out, m_lru_w_in, m_lru_conv_w, m_lru_conv_b, m_lru_w_a, m_lru_b_a, m_lru_w_x, m_lru_b_x, m_lru_lambda, m_lru_w_out, v_ffn1_norm, v_ffn1_w_in, v_ffn1_w_out, v_mix_norm, v_ffn2_norm, v_ffn2_w_in, v_ffn2_w_out, v_final_norm, v_s5_w_in, v_s5_lam_re, v_s5_lam_im, v_s5_log_dt, v_s5_b_re, v_s5_b_im, v_s5_c_re, v_s5_c_im, v_s5_d, v_s5_w_out, v_sb_w_qkv, v_sb_w_out, v_lru_w_in, v_lru_conv_w, v_lru_conv_b, v_lru_w_a, v_lru_b_a, v_lru_w_x, v_lru_b_x, v_lru_lambda, v_lru_w_out):
    local = dict(locals())
    W = {n: local[n] for n in WEIGHTS}
    M = {n: local["m_" + n] for n in WEIGHTS}
    V = {n: local["v_" + n] for n in WEIGHTS}

    h0 = x[0]
    target = loss_target[0]
    n_rows, dm = h0.shape
    depth = ffn1_norm.shape[0]

    ffn_seq = [(tag, layer) for layer in range(depth) for tag in ("ffn1", "ffn2")]

    def ffn_shards(tag, layer):
        return W[tag + "_w_in"][layer].astype(BF16), W[tag + "_w_out"][layer].astype(BF16)

    def ffn_views(wi, wo):
        return wi.reshape((2, N_DEV // 2) + wi.shape[1:]), wo.reshape((N_DEV // 2, 2) + wo.shape[1:])

    first_in, first_out = ffn_shards(*ffn_seq[0])
    ffn_w = {ffn_seq[0]: ffn_views(_all_gather("ag_first_w_in", first_in), _all_gather("ag_first_w_out", first_out))}

    ffn_saved = {}

    def ffn_forward(pos, h_in):
        tag, layer = ffn_seq[pos]
        comms = [("gather", a) for a in ffn_shards(*ffn_seq[pos + 1])] if pos + 1 < len(ffn_seq) else []
        res = _ffn_fwd("%s_fwd_%d" % (tag, layer), h_in, W[tag + "_norm"][layer:layer + 1], *ffn_w[ffn_seq[pos]],
                       comms=comms)
        ffn_saved[ffn_seq[pos]] = (res[1], res[2])
        if comms:
            ffn_w[ffn_seq[pos + 1]] = ffn_views(res[3], res[4])
        return res[0]

    mix_shapes = [W[n].shape for n in MIXER_BIG]

    def unpack_mixers(names, gathered):
        parts = _unpack_rows(gathered, [W[n].shape for n in names], lead=1)
        return {n: _unshard(a, SHARD_AXIS[n]) for n, a in zip(names, parts)}

    full = unpack_mixers(MIXER_EARLY, _all_gather("ag_mixers", _pack_rows([W[n].astype(BF16) for n in MIXER_EARLY], dm)))
    late_mixers = _pack_rows([W[n].astype(BF16) for n in MIXER_LATE], dm)
    small_shapes = [W[n].shape for n in SMALL_SHARDED]
    small_g = _all_gather("ag_small", _pack_rows([W[n] for n in SMALL_SHARDED], 128))
    full.update({n: _unshard(a, SHARD_AXIS[n])
                 for n, a in zip(SMALL_SHARDED, _unpack_rows(small_g, small_shapes, lead=1))})

    n_s5 = s5_w_in.shape[0]
    s5_groups = s5_lam_re.shape[1]
    heads = dm // SB_HEAD_DIM

    grads = {}
    saved = []
    h = h0

    for layer in range(depth):
        kind, j = layer % 3, layer // 3
        rec = {"h0": h}
        h = ffn_forward(2 * layer, h)
        rec["h1"] = h
        gain = mix_norm[layer:layer + 1]
        if kind == 0:
            (u,) = _mm_fwd("s5_in_%d" % layer, h, full["s5_w_in"][j], F32, gain=gain)
            pars = (s5_lam_re[j], s5_lam_im[j], s5_log_dt[j], s5_b_re[j], s5_b_im[j], s5_c_re[j], s5_c_im[j])
            mats, mats_vjp = jax.vjp(_s5_mats, *pars)
            apw, ar_fwd, ar_rev = _s5_powers(*pars[:3])
            res = _s5_fwd("s5_core_%d" % layer, u, *mats[:3], apw, ar_fwd,
                          comms=[("gather", late_mixers)] if layer == 0 else [])
            ys, sprev = res[0], res[1]
            if layer == 0:
                full.update(unpack_mixers(MIXER_LATE, res[2]))
            d_skip = full["s5_d"][j:j + 1]
            (z,) = _rowwise("s5_gelu_%d" % layer, lambda i, nt, ys_v, u_v, d_v: ((_gelu(ys_v + d_v * u_v),), ()),
                            [('t', ys, dm, 0), ('t', u, dm, 0), ('b', d_skip)], [(dm, BF16)], [], n_rows,
                            _tile(n_rows, 512))
            h, vg = _mm_fwd("s5_out_%d" % layer, z, full["s5_w_out"][j], F32, resid=h, glu=True)
            rec.update(u=u, ys=ys, sprev=sprev, z=z, vg=vg, mats=mats, mats_vjp=mats_vjp, apw=apw,
                       ar_rev=ar_rev, d_skip=d_skip)
        elif kind == 1:
            (qkv,) = _mm_fwd("sb_in_%d" % layer, h, full["sb_w_qkv"][j], BF16, gain=gain)
            o = _sb_fwd("sb_attn_%d" % layer, qkv, heads)
            (h,) = _mm_fwd("sb_out_%d" % layer, o, full["sb_w_out"][j], F32, resid=h)
            rec.update(qkv=qkv, o=o)
        else:
            (gx,) = _mm_fwd("lru_in_%d" % layer, h, full["lru_w_in"][j], F32, gain=gain)
            wa, wx = full["lru_w_a"][j], full["lru_w_x"][j]
            ba, bx = full["lru_b_a"][j].reshape(1, dm), full["lru_b_x"][j].reshape(1, dm)
            lam_row = full["lru_lambda"][j:j + 1]
            xc, r, ig, a, gated = _lru_gates_fwd("lru_gates_%d" % layer, gx, full["lru_conv_w"][j],
                                                 full["lru_conv_b"][j:j + 1], wa, ba, wx, bx, lam_row)
            hs, y = _lru_scan_fwd("lru_scan_%d" % layer, a, gated, gx)
            (h,) = _mm_fwd("lru_out_%d" % layer, y, full["lru_w_out"][j], F32, resid=h)
            rec.update(gx=gx, xc=xc, r=r, ig=ig, a=a, hs=hs, y=y, wa=wa, wx=wx, lam_row=lam_row)
        rec["h2"] = h
        h = ffn_forward(2 * layer + 1, h)
        saved.append(rec)

    dh, err2, dgf = _loss_head("loss_head", h, final_norm.reshape(1, dm), target)
    loss = lax.psum(0.5 / dm * jnp.sum(err2), ("x", "y", "c"))
    grads["final_norm"] = dgf.reshape(final_norm.shape)

    per_layer = {n: [None] * depth for n in ("ffn1_norm", "mix_norm", "ffn2_norm")}
    mixer_grads = {}
    recv_ffn = {}
    pending = []

    def ffn_backward(tag, layer, x_in, dh_in, more=None):
        comms = {0: [("exchange", pending[1])], 1: [("exchange", pending[2])]} if pending else {}
        comms.update(more or {})
        dx, dwi, dwo, dg, extra = _ffn_bwd("%s_bwd_%d" % (tag, layer), x_in, dh_in, W[tag + "_norm"][layer:layer + 1],
                                            *ffn_w[(tag, layer)], *ffn_saved[(tag, layer)], comms_by_block=comms)
        if pending:
            recv_ffn[pending[0]] = tuple(extra[:2])
            extra = extra[2:]
        pending[:] = [(tag, layer), dwi.reshape((N_DEV,) + dwi.shape[2:]), dwo.reshape(N_DEV, -1, dm)]
        per_layer[tag + "_norm"][layer] = dg
        return dx, extra

    def put(name, j, value, count):
        mixer_grads.setdefault(name, [None] * count)[j] = value

    for layer in reversed(range(depth)):
        kind, j = layer % 3, layer // 3
        rec = saved[layer]
        dh, _ = ffn_backward("ffn2", layer, rec["h2"], dh)
        gain = mix_norm[layer:layer + 1]
        if kind == 0:
            dvg, = _rowwise("s5_glu_bwd_%d" % layer,
                            lambda i, nt, d_v, vg_v: ((jnp.concatenate(
                                [d_v * _sigmoid(vg_v[:, dm:]),
                                 d_v * vg_v[:, :dm] * _sigmoid(vg_v[:, dm:]) * (1.0 - _sigmoid(vg_v[:, dm:]))],
                                axis=1),), ()),
                            [('t', dh, dm, 0), ('t', rec["vg"], 2 * dm, 0)], [(2 * dm, BF16)], [], n_rows,
                            _tile(n_rows, 256))
            dz, dw_out = _mm_bwd("s5_out_bwd_%d" % layer, rec["z"], dvg, full["s5_w_out"][j])

            def gelu_bwd(i, nt, dz_v, ys_v, u_v, d_v):
                dy_v = dz_v * _gelu_grad(ys_v + d_v * u_v)
                return (dy_v,), (jnp.sum(dy_v * u_v, axis=0, keepdims=True),)

            dys, dd = _rowwise("s5_gelu_bwd_%d" % layer, gelu_bwd,
                               [('t', dz, dm, 0), ('t', rec["ys"], dm, 0), ('t', rec["u"], dm, 0),
                                ('b', rec["d_skip"])], [(dm, F32)], [(1, dm)], n_rows, _tile(n_rows, 512))
            m_, bm_, cm_, _ = rec["mats"]
            du_core, dm_m, dm_b, dm_c, d_a = _s5_bwd("s5_core_bwd_%d" % layer, rec["u"], dys, rec["sprev"],
                                                    m_, bm_, cm_, rec["apw"], rec["ar_rev"])
            dpars = rec["mats_vjp"]((dm_m, dm_b, dm_c, d_a.reshape(s5_groups, -1)))
            for nme, val in zip(("s5_lam_re", "s5_lam_im", "s5_log_dt", "s5_b_re", "s5_b_im", "s5_c_re", "s5_c_im"),
                                dpars):
                put(nme, j, val, n_s5)
            dh, dw_in, dgm = _mm_bwd("s5_in_bwd_%d" % layer, rec["h1"], du_core, full["s5_w_in"][j], gain=gain, dres=dh,
                                     d_add=(dys, rec["d_skip"]))
            put("s5_d", j, dd[0], n_s5)
            put("s5_w_out", j, dw_out, n_s5)
            put("s5_w_in", j, dw_in, n_s5)
        elif kind == 1:
            do, dw_out = _mm_bwd("sb_out_bwd_%d" % layer, rec["o"], dh, full["sb_w_out"][j])
            dq, dk, dv = _sb_bwd("sb_attn_bwd_%d" % layer, rec["qkv"], rec["o"], do, heads)
            dqkv = jnp.concatenate([dq, dk, dv], axis=1).astype(BF16)
            dh, dw_in, dgm = _mm_bwd("sb_in_bwd_%d" % layer, rec["h1"], dqkv, full["sb_w_qkv"][j], gain=gain, dres=dh)
            put("sb_w_out", j, dw_out, 1)
            put("sb_w_qkv", j, dw_in, 1)
        else:
            dy, dw_out = _mm_bwd("lru_out_bwd_%d" % layer, rec["y"], dh, full["lru_w_out"][j])
            lam_t = _lru_scan_bwd("lru_scan_bwd_%d" % layer, rec["a"], dy, rec["gx"])
            dxc, dwa, dwx, dba, dbx, dsp = _lru_gates_bwd("lru_gates_bwd_%d" % layer, lam_t, rec["hs"], rec["xc"],
                                                          rec["r"], rec["ig"], rec["a"], rec["wa"], rec["wx"],
                                                          rec["lam_row"])
            dgx, dcw, dcb = _lru_conv_bwd("lru_conv_bwd_%d" % layer, dxc, rec["gx"], dy, rec["hs"],
                                          full["lru_conv_w"][j])
            dh, dw_in, dgm = _mm_bwd("lru_in_bwd_%d" % layer, rec["h1"], dgx, full["lru_w_in"][j], gain=gain, dres=dh)
            nb = rec["wa"].shape[0]
            put("lru_w_out", j, dw_out, 1)
            put("lru_w_in", j, dw_in, 1)
            put("lru_w_a", j, dwa.reshape(rec["wa"].shape), 1)
            put("lru_w_x", j, dwx.reshape(rec["wx"].shape), 1)
            put("lru_b_a", j, dba.reshape(nb, -1), 1)
            put("lru_b_x", j, dbx.reshape(nb, -1), 1)
            put("lru_conv_w", j, dcw[:LRU_CONV], 1)
            put("lru_conv_b", j, dcb[0], 1)
            put("lru_lambda", j, (dsp * -_sigmoid(-rec["lam_row"]))[0], 1)
        per_layer["mix_norm"][layer] = dgm
        more = None
        if layer == 0:
            for n, parts in mixer_grads.items():
                grads[n] = jnp.stack(parts)
            send = _pack_rows([_shard_blocks(grads[n], SHARD_AXIS[n]).astype(BF16) for n in MIXER_BIG], dm, lead=1)
            half = send.shape[1] // 2
            more = {2: [("exchange", send[:, :half])], 3: [("exchange", send[:, half:])]}
        dh, got = ffn_backward("ffn1", layer, rec["h0"], dh, more)
        if layer == 0:
            recv_mixers = jnp.concatenate(got, axis=1)

    grad_x = dh[None]
    for n in ("ffn1_norm", "mix_norm", "ffn2_norm"):
        grads[n] = jnp.concatenate(per_layer[n], axis=0)

    out_g, out_d, out_m, out_v = {}, {}, {}, {}

    def finish(names, res, shapes):
        for n, g_, d_, m_, v_ in zip(names, *[_unpack_rows(t, shapes) for t in res]):
            out_g[n], out_d[n], out_m[n], out_v[n] = g_, d_, m_, v_

    recv_ffn[pending[0]] = (_exchange("xchg_last_w_in", pending[1]), _exchange("xchg_last_w_out", pending[2]))
    for tag in ("ffn1", "ffn2"):
        for which, n in enumerate((tag + "_w_in", tag + "_w_out")):
            turn = (lambda t: jnp.swapaxes(t, 1, 2)) if which == 0 else (lambda t: t)
            res = _adamw_layers("adamw_" + n, [recv_ffn[(tag, layer)][which] for layer in range(depth)],
                                turn(W[n]), turn(M[n]), turn(V[n]))
            out_g[n], out_d[n], out_m[n], out_v[n] = [turn(r) for r in res]

    finish(MIXER_BIG, _adamw("adamw_mixers", recv_mixers,
                             *[_pack_rows([t[n] for n in MIXER_BIG], dm) for t in (W, M, V)]), mix_shapes)

    small_names = REPLICATED + SMALL_SHARDED
    small_full_shapes = [grads[n].shape for n in small_names]
    parts = _all_gather("ag_small_grads", _pack_rows([grads[n] for n in small_names], 128))
    zero = jnp.zeros(parts.shape[1:], F32)
    summed = _adamw("sum_small_grads", parts, zero, zero, zero)[0]
    small_sum = dict(zip(small_names, _unpack_rows(summed, small_full_shapes)))
    me = 4 * lax.axis_index("x") + 2 * lax.axis_index("y") + lax.axis_index("c")
    rep_shapes = [W[n].shape for n in REPLICATED]
    g_rep = _pack_rows([small_sum[n] for n in REPLICATED], 128)[None]
    finish(REPLICATED, _adamw("adamw_replicated", g_rep, *[_pack_rows([t[n] for n in REPLICATED], 128)
                                                           for t in (W, M, V)]), rep_shapes)
    g_loc = []
    for n in SMALL_SHARDED:
        ax = SHARD_AXIS[n]
        size = W[n].shape[ax]
        g_loc.append(lax.dynamic_slice_in_dim(small_sum[n], me * size, size, axis=ax))
    finish(SMALL_SHARDED, _adamw("adamw_small", _pack_rows(g_loc, 128)[None],
                                 *[_pack_rows([t[n] for n in SMALL_SHARDED], 128) for t in (W, M, V)]), small_shapes)

    return (loss, grad_x, *[out_g[n] for n in WEIGHTS], *[out_d[n] for n in WEIGHTS],
            *[out_m[n] for n in WEIGHTS], *[out_v[n] for n in WEIGHTS])
```
